```python
import math
import jax, jax.numpy as jnp
from jax import lax
import numpy as np

D_MODEL = 1024
BATCH = 8
SEQ = 4096
DEPTH = 2

CHUNK = 64
HEAD_DIM = 64
D_CONV = D_MODEL // 2
N_SB_HEADS = 8
D_SB = N_SB_HEADS * HEAD_DIM
D_MIX = D_CONV + D_SB
CONV_WIDTH = 3
PLE_DIM = 256
Q_BLOCK = 128
N_IN = 4 * D_CONV + 4 * D_SB
EPS = 1e-6

kernel_name = "hybrid_shortconv_stickbreaking_ple"


def rmsnorm(x, g):
    xf = x.astype(jnp.float32)
    y = xf * lax.rsqrt(jnp.mean(xf * xf, axis=-1, keepdims=True) + EPS)
    return (y * g.astype(jnp.float32)).astype(x.dtype)


def group_rmsnorm(y, g, group):
    shp = y.shape
    yf = y.astype(jnp.float32).reshape(shp[:-1] + (shp[-1] // group, group))
    yf = yf * lax.rsqrt(jnp.mean(yf * yf, axis=-1, keepdims=True) + EPS)
    return (yf.reshape(shp) * g.astype(jnp.float32)).astype(y.dtype)


def causal_dwconv(u, w, b):
    s = u.shape[1]
    up = jnp.pad(u, ((0, 0), (CONV_WIDTH - 1, 0), (0, 0)))
    y = b
    for j in range(CONV_WIDTH):
        y = y + up[:, j:j + s, :] * w[j]
    return y


def stick_breaking_block(q_blk, k_pre, v_pre, t0):
    dh = q_blk.shape[-1]
    z = jnp.einsum('bqhd,bkhd->bhqk', q_blk.astype(jnp.float32), k_pre.astype(jnp.float32)) / math.sqrt(dh)
    qb, kl = q_blk.shape[1], k_pre.shape[1]
    t_idx = t0 + jnp.arange(qb)[:, None]
    s_idx = jnp.arange(kl)[None, :]
    mask = s_idx < t_idx
    log_1m = jnp.where(mask, jax.nn.log_sigmoid(-z), 0.0)
    rem = lax.cumsum(log_1m, axis=3, reverse=True) - log_1m
    a = jnp.where(mask, jnp.exp(jax.nn.log_sigmoid(z) + rem), 0.0)
    out = jnp.einsum('bhqk,bkhd->bqhd', a, v_pre.astype(jnp.float32))
    return out.astype(q_blk.dtype)


def stick_breaking_attention(q, k, v):
    s = q.shape[1]
    outs = []
    for blk in range(s // Q_BLOCK):
        t0 = blk * Q_BLOCK
        kend = t0 + Q_BLOCK
        outs.append(stick_breaking_block(q[:, t0:kend], k[:, :kend], v[:, :kend], t0))
    return jnp.concatenate(outs, axis=1)


def _fwd_setup_inputs(seed: int = 0) -> dict:
    key = jax.random.key(seed)
    ks = jax.random.split(key, 14)
    f32 = jnp.float32
    x = jax.random.normal(ks[0], (BATCH, SEQ, D_MODEL), f32)
    p = jax.random.normal(ks[1], (DEPTH, BATCH, SEQ, PLE_DIM), f32)
    norm_g = 1.0 + 0.02 * jax.random.normal(ks[2], (DEPTH, D_MODEL), f32)
    w_in = jax.random.normal(ks[3], (DEPTH, D_MODEL, N_IN), f32) * D_MODEL ** -0.5
    conv_w = jax.random.normal(ks[4], (DEPTH, CONV_WIDTH, D_CONV), f32) * CONV_WIDTH ** -0.5
    conv_b = 0.02 * jax.random.normal(ks[5], (DEPTH, D_CONV), f32)
    branch_g = 1.0 + 0.02 * jax.random.normal(ks[6], (DEPTH, D_MIX), f32)
    w_out = jax.random.normal(ks[7], (DEPTH, D_MIX, D_MODEL), f32) * D_MIX ** -0.5
    ple_norm_g = 1.0 + 0.02 * jax.random.normal(ks[8], (DEPTH, D_MODEL), f32)
    w_pg = jax.random.normal(ks[9], (DEPTH, D_MODEL, D_MODEL), f32) * D_MODEL ** -0.5
    b_pg = 0.02 * jax.random.normal(ks[10], (DEPTH, D_MODEL), f32)
    w_pe = jax.random.normal(ks[11], (DEPTH, PLE_DIM, D_MODEL), f32) * PLE_DIM ** -0.5
    final_g = 1.0 + 0.02 * jax.random.normal(ks[12], (D_MODEL,), f32)
    return {"x": x, "p": p, "norm_g": norm_g, "w_in": w_in, "conv_w": conv_w,
            "conv_b": conv_b, "branch_g": branch_g, "w_out": w_out,
            "ple_norm_g": ple_norm_g, "w_pg": w_pg, "b_pg": b_pg, "w_pe": w_pe,
            "final_g": final_g}


def _fwd_reference(x, p, norm_g, w_in, conv_w, conv_b, branch_g, w_out,
              ple_norm_g, w_pg, b_pg, w_pe, final_g):
    bsz, s, _ = x.shape
    for i in range(DEPTH):
        h = rmsnorm(x, norm_g[i])
        proj = h @ w_in[i]
        c_b, c_c, c_h, c_z, q, k, v, a_z = jnp.split(
            proj, [D_CONV, 2 * D_CONV, 3 * D_CONV, 4 * D_CONV,
                   4 * D_CONV + D_SB, 4 * D_CONV + 2 * D_SB, 4 * D_CONV + 3 * D_SB], axis=-1)
        y_c = c_b * causal_dwconv(c_c * c_h, conv_w[i], conv_b[i])
        qh = q.reshape(bsz, s, N_SB_HEADS, HEAD_DIM)
        kh = k.reshape(bsz, s, N_SB_HEADS, HEAD_DIM)
        vh = v.reshape(bsz, s, N_SB_HEADS, HEAD_DIM)
        y_a = stick_breaking_attention(qh, kh, vh).reshape(bsz, s, D_SB)
        y = group_rmsnorm(jnp.concatenate([y_c, y_a], axis=-1), branch_g[i], HEAD_DIM)
        y = y * jax.nn.silu(jnp.concatenate([c_z, a_z], axis=-1))
        x = x + y @ w_out[i]
        gate = jax.nn.sigmoid(rmsnorm(x, ple_norm_g[i]) @ w_pg[i] + b_pg[i])
        x = x + gate * (p[i] @ w_pe[i])
    return rmsnorm(x, final_g)


import jax as _jax
import jax.numpy as _jnp

TWIN_FORMAT = 'train_step'
FWD_PARAMS = ['x', 'p', 'norm_g', 'w_in', 'conv_w', 'conv_b', 'branch_g', 'w_out', 'ple_norm_g', 'w_pg', 'b_pg', 'w_pe', 'final_g']
TWIN_WEIGHTS = ['norm_g', 'w_in', 'conv_w', 'conv_b', 'branch_g', 'w_out', 'ple_norm_g', 'w_pg', 'b_pg', 'w_pe', 'final_g']
TWIN_DIFF_INPUT = 'x'
TWIN_INPUTS = ['x', 'p', 'norm_g', 'w_in', 'conv_w', 'conv_b', 'branch_g', 'w_out', 'ple_norm_g', 'w_pg', 'b_pg', 'w_pe', 'final_g', 'loss_target', 'm_norm_g', 'm_w_in', 'm_conv_w', 'm_conv_b', 'm_branch_g', 'm_w_out', 'm_ple_norm_g', 'm_w_pg', 'm_b_pg', 'm_w_pe', 'm_final_g', 'v_norm_g', 'v_w_in', 'v_conv_w', 'v_conv_b', 'v_branch_g', 'v_w_out', 'v_ple_norm_g', 'v_w_pg', 'v_b_pg', 'v_w_pe', 'v_final_g']
TWIN_OUTPUTS = ['loss', 'grad_x', 'grad_norm_g', 'grad_w_in', 'grad_conv_w', 'grad_conv_b', 'grad_branch_g', 'grad_w_out', 'grad_ple_norm_g', 'grad_w_pg', 'grad_b_pg', 'grad_w_pe', 'grad_final_g', 'delta_norm_g', 'delta_w_in', 'delta_conv_w', 'delta_conv_b', 'delta_branch_g', 'delta_w_out', 'delta_ple_norm_g', 'delta_w_pg', 'delta_b_pg', 'delta_w_pe', 'delta_final_g', 'new_m_norm_g', 'new_m_w_in', 'new_m_conv_w', 'new_m_conv_b', 'new_m_branch_g', 'new_m_w_out', 'new_m_ple_norm_g', 'new_m_w_pg', 'new_m_b_pg', 'new_m_w_pe', 'new_m_final_g', 'new_v_norm_g', 'new_v_w_in', 'new_v_conv_w', 'new_v_conv_b', 'new_v_branch_g', 'new_v_w_out', 'new_v_ple_norm_g', 'new_v_w_pg', 'new_v_b_pg', 'new_v_w_pe', 'new_v_final_g']
TWIN_LEAF_KINDS = {'loss': 'loss', 'grad_x': 'grad_x', 'grad_norm_g': 'grad_w', 'grad_w_in': 'grad_w', 'grad_conv_w': 'grad_w', 'grad_conv_b': 'grad_w', 'grad_branch_g': 'grad_w', 'grad_w_out': 'grad_w', 'grad_ple_norm_g': 'grad_w', 'grad_w_pg': 'grad_w', 'grad_b_pg': 'grad_w', 'grad_w_pe': 'grad_w', 'grad_final_g': 'grad_w', 'delta_norm_g': 'delta_w', 'delta_w_in': 'delta_w', 'delta_conv_w': 'delta_w', 'delta_conv_b': 'delta_w', 'delta_branch_g': 'delta_w', 'delta_w_out': 'delta_w', 'delta_ple_norm_g': 'delta_w', 'delta_w_pg': 'delta_w', 'delta_b_pg': 'delta_w', 'delta_w_pe': 'delta_w', 'delta_final_g': 'delta_w', 'new_m_norm_g': 'new_m', 'new_m_w_in': 'new_m', 'new_m_conv_w': 'new_m', 'new_m_conv_b': 'new_m', 'new_m_branch_g': 'new_m', 'new_m_w_out': 'new_m', 'new_m_ple_norm_g': 'new_m', 'new_m_w_pg': 'new_m', 'new_m_b_pg': 'new_m', 'new_m_w_pe': 'new_m', 'new_m_final_g': 'new_m', 'new_v_norm_g': 'new_v', 'new_v_w_in': 'new_v', 'new_v_conv_w': 'new_v', 'new_v_conv_b': 'new_v', 'new_v_branch_g': 'new_v', 'new_v_w_out': 'new_v', 'new_v_ple_norm_g': 'new_v', 'new_v_w_pg': 'new_v', 'new_v_b_pg': 'new_v', 'new_v_w_pe': 'new_v', 'new_v_final_g': 'new_v'}


def _forward(args):
    return _fwd_reference(*[args[k] for k in FWD_PARAMS])


def _output_shape():
    def fwd():
        inp = _fwd_setup_inputs(0)
        return _fwd_reference(*[inp[k] for k in FWD_PARAMS])
    out = _jax.eval_shape(fwd)
    return out.shape, out.dtype

N_MICROBATCH = 1
ADAM_LR = 0.001
ADAM_B1 = 0.9
ADAM_B2 = 0.999
ADAM_EPS = 1e-08
ADAM_WD = 0.01
ADAM_STEP = 10
PER_EXAMPLE_BATCH_AXIS = {'x': 0, 'p': 1, 'loss_target': 0}
SHARED_INPUTS = []
_WEIGHT_DTYPES = {'norm_g': _jnp.float32, 'w_in': _jnp.float32, 'conv_w': _jnp.float32, 'conv_b': _jnp.float32, 'branch_g': _jnp.float32, 'w_out': _jnp.float32, 'ple_norm_g': _jnp.float32, 'w_pg': _jnp.float32, 'b_pg': _jnp.float32, 'w_pe': _jnp.float32, 'final_g': _jnp.float32}
MOMENT_SCALE = {'norm_g': 1.544941e-01, 'w_in': 7.741323e-02, 'conv_w': 8.809939e-02, 'conv_b': 9.486678e-02, 'branch_g': 8.270000e-02, 'w_out': 8.337886e-02, 'ple_norm_g': 2.973435e-02, 'w_pg': 2.925760e-02, 'b_pg': 5.090113e-02, 'w_pe': 7.486581e-02, 'final_g': 3.204505e+01}


def _to_microbatches(a, axis):
    t = _jnp.moveaxis(a, axis, 0)
    t = t.reshape((N_MICROBATCH, t.shape[0] // N_MICROBATCH) + t.shape[1:])
    return _jnp.moveaxis(t, 1, axis + 1)


def setup_inputs(seed: int = 0) -> dict:
    inp = _fwd_setup_inputs(seed)
    key = _jax.random.fold_in(_jax.random.key(seed), 7919)
    shape, _ = _output_shape()
    out = dict(inp)
    out["loss_target"] = _jax.random.normal(_jax.random.fold_in(key, 0), shape, _jnp.float32)
    for i, name in enumerate(TWIN_WEIGHTS):
        w = inp[name].astype(_jnp.float32)
        if MOMENT_SCALE is None:
            s = _jnp.sqrt(_jnp.mean(_jnp.square(w)) + 1e-30)
        else:
            s = MOMENT_SCALE[name]
        km, kv = _jax.random.split(_jax.random.fold_in(key, i + 1))
        out[name] = w
        out["m_" + name] = s * _jax.random.normal(km, w.shape, _jnp.float32)
        out["v_" + name] = (s * s) * _jax.random.uniform(kv, w.shape, _jnp.float32, 0.5, 1.5)
    if N_MICROBATCH > 1:
        for name, axis in PER_EXAMPLE_BATCH_AXIS.items():
            out[name] = _to_microbatches(out[name], axis)
    return {'x': out['x'], 'p': out['p'], 'norm_g': out['norm_g'], 'w_in': out['w_in'], 'conv_w': out['conv_w'], 'conv_b': out['conv_b'], 'branch_g': out['branch_g'], 'w_out': out['w_out'], 'ple_norm_g': out['ple_norm_g'], 'w_pg': out['w_pg'], 'b_pg': out['b_pg'], 'w_pe': out['w_pe'], 'final_g': out['final_g'], 'loss_target': out['loss_target'], 'm_norm_g': out['m_norm_g'], 'm_w_in': out['m_w_in'], 'm_conv_w': out['m_conv_w'], 'm_conv_b': out['m_conv_b'], 'm_branch_g': out['m_branch_g'], 'm_w_out': out['m_w_out'], 'm_ple_norm_g': out['m_ple_norm_g'], 'm_w_pg': out['m_w_pg'], 'm_b_pg': out['m_b_pg'], 'm_w_pe': out['m_w_pe'], 'm_final_g': out['m_final_g'], 'v_norm_g': out['v_norm_g'], 'v_w_in': out['v_w_in'], 'v_conv_w': out['v_conv_w'], 'v_conv_b': out['v_conv_b'], 'v_branch_g': out['v_branch_g'], 'v_w_out': out['v_w_out'], 'v_ple_norm_g': out['v_ple_norm_g'], 'v_w_pg': out['v_w_pg'], 'v_b_pg': out['v_b_pg'], 'v_w_pe': out['v_w_pe'], 'v_final_g': out['v_final_g']}


def _loss(weights, diff, rest, loss_target):
    with _jax.named_scope("forward"):
        args = {**rest, TWIN_DIFF_INPUT: diff, **{k: w.astype(_WEIGHT_DTYPES[k]) for k, w in weights.items()}}
        y = _forward(args)
    with _jax.named_scope("loss_head"):
        err = _jnp.square(y.astype(_jnp.float32) - loss_target)
        return 0.5 * _jnp.sum(_jnp.mean(err, axis=-1)) if err.ndim else 0.5 * err


def _adamw(w, g, m, v):
    m = ADAM_B1 * m + (1.0 - ADAM_B1) * g
    v = ADAM_B2 * v + (1.0 - ADAM_B2) * _jnp.square(g)
    m_hat = m / (1.0 - ADAM_B1 ** ADAM_STEP)
    v_hat = v / (1.0 - ADAM_B2 ** ADAM_STEP)
    delta = -ADAM_LR * (m_hat / (_jnp.sqrt(v_hat) + ADAM_EPS) + ADAM_WD * w)
    return delta, m, v


def reference(x, p, norm_g, w_in, conv_w, conv_b, branch_g, w_out, ple_norm_g, w_pg, b_pg, w_pe, final_g, loss_target, m_norm_g, m_w_in, m_conv_w, m_conv_b, m_branch_g, m_w_out, m_ple_norm_g, m_w_pg, m_b_pg, m_w_pe, m_final_g, v_norm_g, v_w_in, v_conv_w, v_conv_b, v_branch_g, v_w_out, v_ple_norm_g, v_w_pg, v_b_pg, v_w_pe, v_final_g):
    given = dict(x=x, p=p, norm_g=norm_g, w_in=w_in, conv_w=conv_w, conv_b=conv_b, branch_g=branch_g, w_out=w_out, ple_norm_g=ple_norm_g, w_pg=w_pg, b_pg=b_pg, w_pe=w_pe, final_g=final_g, loss_target=loss_target, m_norm_g=m_norm_g, m_w_in=m_w_in, m_conv_w=m_conv_w, m_conv_b=m_conv_b, m_branch_g=m_branch_g, m_w_out=m_w_out, m_ple_norm_g=m_ple_norm_g, m_w_pg=m_w_pg, m_b_pg=m_b_pg, m_w_pe=m_w_pe, m_final_g=m_final_g, v_norm_g=v_norm_g, v_w_in=v_w_in, v_conv_w=v_conv_w, v_conv_b=v_conv_b, v_branch_g=v_branch_g, v_w_out=v_w_out, v_ple_norm_g=v_ple_norm_g, v_w_pg=v_w_pg, v_b_pg=v_b_pg, v_w_pe=v_w_pe, v_final_g=v_final_g)
    weights = {n: given[n] for n in TWIN_WEIGHTS}
    shared = {n: given[n] for n in SHARED_INPUTS}
    per_example = {n: given[n] for n in ['x', 'p']}
    grad_fn = _jax.value_and_grad(_loss, argnums=(0, 1))

    def one_microbatch(ex, loss_target):
        ex = dict(ex)
        diff = ex.pop(TWIN_DIFF_INPUT)
        return grad_fn(weights, diff, {**shared, **ex}, loss_target)

    if N_MICROBATCH == 1:
        loss, (grad_w, grad_x) = one_microbatch(per_example, given["loss_target"])
    else:
        def body(carry, xs):
            loss_sum, grad_sum = carry
            l_k, (gw_k, gx_k) = one_microbatch(xs[0], xs[1])
            with _jax.named_scope("update"):
                return (loss_sum + l_k, _jax.tree.map(_jnp.add, grad_sum, gw_k)), gx_k

        init = (_jnp.zeros((), _jnp.float32), _jax.tree.map(_jnp.zeros_like, weights))
        (loss, grad_w), grad_x = _jax.lax.scan(body, init, (per_example, given["loss_target"]))
    with _jax.named_scope("update"):
        delta_w, new_m, new_v = {}, {}, {}
        for n in TWIN_WEIGHTS:
            delta_w[n], new_m[n], new_v[n] = _adamw(weights[n], grad_w[n], given["m_" + n], given["v_" + n])
    return (loss, grad_x, *[grad_w[n] for n in TWIN_WEIGHTS], *[delta_w[n] for n in TWIN_WEIGHTS],
            *[new_m[n] for n in TWIN_WEIGHTS], *[new_v[n] for n in TWIN_WEIGHTS])
```

```python
import jax
import jax.numpy as jnp
from jax import lax
from jax.experimental import pallas as pl
from jax.experimental.pallas import tpu as pltpu

F32 = jnp.float32
BF16 = jnp.bfloat16

D_MODEL = 1024
D_CONV = 512
D_SB = 512
N_IN = 4096
HEAD_DIM = 64
PLE_DIM = 256
DEPTH = 2
EPS = 1e-6
ADAM_LR = 0.001
ADAM_B1 = 0.9
ADAM_B2 = 0.999
ADAM_EPS = 1e-08
ADAM_WD = 0.01
ADAM_STEP = 10

LANES = 128
SUBLANES = 8
VMEM_BYTES_V7X = 64 * 1024 * 1024
VMEM_LIMIT = VMEM_BYTES_V7X - 8 * 1024 * 1024

N_DEV = 8
ROW_TILE = 256
ATTN_TILE = 256
SMALL_ROWS = 104
SMALL_REPL_ROWS = 80

NT = (((1,), (1,)), ((), ()))
TN = (((0,), (0,)), ((), ()))


def _call(body, **kw):
    return pl.pallas_call(body, **kw)


def _params(sem=None, vmem=None):
    return pltpu.CompilerParams(dimension_semantics=sem, vmem_limit_bytes=vmem)


def _sigmoid(z):
    return 1.0 / (1.0 + jnp.exp(-z))


def _group_bcast_sum(a, lo):
    s_lo = jnp.sum(jnp.where(lo, a, 0.0), axis=-1, keepdims=True)
    s_hi = jnp.sum(jnp.where(lo, 0.0, a), axis=-1, keepdims=True)
    return jnp.where(lo, s_lo, s_hi)


def _my_block():
    return 4 * lax.axis_index("x") + 2 * lax.axis_index("y") + lax.axis_index("c")


def _cast_bf16(a2d, name):
    rows, cols = a2d.shape
    tr = min(rows, 512)

    def body(a_ref, o_ref):
        o_ref[...] = a_ref[...].astype(BF16)

    return _call(
        body, name=name, grid=(rows // tr,),
        out_shape=jax.ShapeDtypeStruct((rows, cols), BF16),
        in_specs=[pl.BlockSpec((tr, cols), lambda i: (i, 0))],
        out_specs=pl.BlockSpec((tr, cols), lambda i: (i, 0)),
        compiler_params=_params(("parallel",)),
    )(a2d)


def _allgather_weights(win_s, wout_s, wpg_s, wpe_s, cw_s):
    n_t = 5

    def body(win_ref, wout_ref, wpg_ref, wpe_ref, cw_ref,
             win_o, wout_o, wpg_o, wpe_o, cw_o, send_sems, recv_sems, local_sems):
        x, y, c = lax.axis_index("x"), lax.axis_index("y"), lax.axis_index("c")
        me, sibling = (x, y, c), (x, y, 1 - c)
        chips = [(1 - x, y), (x, 1 - y), (1 - x, 1 - y)]
        srcs = [win_ref, wout_ref, wpg_ref, wpe_ref, cw_ref]
        outs = [win_o, wout_o, wpg_o, wpe_o, cw_o]

        def place(t, dev):
            b = 4 * dev[0] + 2 * dev[1] + dev[2]
            if t == 0:
                return outs[0].at[:, :, pl.ds(b * 512, 512)]
            if t in (1, 2):
                return outs[t].at[:, pl.ds(b * 128, 128), :]
            if t == 3:
                return outs[3].at[:, :, pl.ds(b * 128, 128)]
            return outs[4].at[b]

        def copy(t, k, block, to, own=False):
            return pltpu.make_async_remote_copy(
                src_ref=srcs[t] if own else place(t, block), dst_ref=place(t, block),
                send_sem=send_sems.at[t, k], recv_sem=recv_sems.at[t, k],
                device_id=to, device_id_type=pl.DeviceIdType.MESH)

        mine = [pltpu.make_async_copy(srcs[t], place(t, me), local_sems.at[t]) for t in range(n_t)]
        for cp in mine:
            cp.start()
        first = []
        for t in range(n_t):
            first.append(copy(t, 0, me, sibling, own=True))
            first += [copy(t, 1 + j, me, (*chip, c), own=True) for j, chip in enumerate(chips)]
        for cp in first:
            cp.start()
        passed = []
        for j, chip in enumerate(chips):
            for t in range(n_t):
                copy(t, 1 + j, (*chip, c), me).wait_recv()
                fwd = copy(t, 4 + j, (*chip, c), sibling)
                fwd.start()
                passed.append(fwd)
        for t in range(n_t):
            copy(t, 0, sibling, me).wait_recv()
            for j, chip in enumerate(chips):
                copy(t, 4 + j, (*chip, 1 - c), me).wait_recv()
        for cp in first + passed:
            cp.wait_send()
        for cp in mine:
            cp.wait()

    any_spec = pl.BlockSpec(memory_space=pl.ANY)
    return _call(
        body, name="allgather_weights",
        out_shape=(jax.ShapeDtypeStruct((DEPTH, D_MODEL, N_IN), BF16),
                   jax.ShapeDtypeStruct((DEPTH, D_MODEL, D_MODEL), BF16),
                   jax.ShapeDtypeStruct((DEPTH, D_MODEL, D_MODEL), BF16),
                   jax.ShapeDtypeStruct((DEPTH, PLE_DIM, D_MODEL), BF16),
                   jax.ShapeDtypeStruct((N_DEV, SUBLANES, LANES), F32)),
        in_specs=[any_spec] * 5, out_specs=[any_spec] * 5,
        scratch_shapes=[pltpu.SemaphoreType.DMA((n_t, 7)), pltpu.SemaphoreType.DMA((n_t, 7)),
                        pltpu.SemaphoreType.DMA((n_t,))],
    )(win_s, wout_s, wpg_s, wpe_s, cw_s)


def _fwd_in(x, g, w_full, layer, name):
    s = x.shape[0]
    ts = min(ROW_TILE, s)

    def body(x_ref, g_ref, w_ref, h_ref, pc_ref, qkv_ref, az_ref):
        xf = x_ref[...]
        r = lax.rsqrt(jnp.mean(xf * xf, axis=-1, keepdims=True) + EPS)
        h = (xf * r * g_ref[...]).astype(BF16)
        h_ref[...] = h
        pc_ref[...] = jnp.dot(h, w_ref[:, 0:2048], preferred_element_type=F32)
        q = jnp.dot(h, w_ref[:, 2048:2560], preferred_element_type=F32)
        qkv_ref[:, 0:512] = (q * 0.125).astype(BF16)
        qkv_ref[:, 512:1536] = jnp.dot(h, w_ref[:, 2560:3584], preferred_element_type=F32).astype(BF16)
        az_ref[...] = jnp.dot(h, w_ref[:, 3584:4096], preferred_element_type=F32)

    row = lambda width: pl.BlockSpec((ts, width), lambda i: (i, 0))
    return _call(
        body, name=name, grid=(s // ts,),
        out_shape=(jax.ShapeDtypeStruct((s, D_MODEL), BF16), jax.ShapeDtypeStruct((s, 2048), F32),
                   jax.ShapeDtypeStruct((s, 1536), BF16), jax.ShapeDtypeStruct((s, 512), F32)),
        in_specs=[row(D_MODEL), pl.BlockSpec((1, D_MODEL), lambda i: (0, 0)),
                  pl.BlockSpec((None, D_MODEL, N_IN), lambda i: (layer, 0, 0))],
        out_specs=(row(D_MODEL), row(2048), row(1536), row(512)),
        compiler_params=_params(("parallel",), VMEM_LIMIT),
    )(x, g, w_full)


def _attn_pieces(tq):
    lane = lax.broadcasted_iota(jnp.int32, (1, LANES), 1)
    lo = lane < HEAD_DIM
    row = lax.broadcasted_iota(jnp.int32, (tq, tq), 0)
    col = lax.broadcasted_iota(jnp.int32, (tq, tq), 1)
    causal = col < row
    tri_gt = jnp.where(row > col, 1.0, 0.0).astype(BF16)
    tri_le = jnp.where(row <= col, 1.0, 0.0).astype(BF16)
    return lo, causal, tri_gt, tri_le


def _split_heads(a, lo):
    z = jnp.zeros_like(a)
    return (jnp.where(lo, a, z), jnp.where(lo, z, a))


def _log_one_minus_beta(qh, k, causal, diag):
    z = lax.dot_general(qh, k, NT, preferred_element_type=F32)
    lg = -(jnp.maximum(z, 0.0) + jnp.log(1.0 + jnp.exp(-jnp.abs(z))))
    if diag:
        lg = jnp.where(causal, lg, 0.0)
    return z, lg


def _running_sum(a, tri):
    hi = a.astype(BF16)
    lo_part = (a - hi.astype(F32)).astype(BF16)
    return jnp.dot(hi, tri, preferred_element_type=F32) + jnp.dot(lo_part, tri, preferred_element_type=F32)


def _attn_fwd(qkv, name):
    s = qkv.shape[0]
    tq = min(ATTN_TILE, s)
    nq = s // tq

    def body(q_ref, k_ref, v_ref, o_ref, lsum_ref):
        qi = pl.program_id(1)
        lo, causal, tri_gt, _ = _attn_pieces(tq)
        qh = _split_heads(q_ref[...], lo)

        def block(kb, carry, diag):
            acc, c0, c1 = carry
            start = pl.multiple_of(kb * tq, tq)
            k = k_ref[pl.ds(start, tq), :]
            vh = _split_heads(v_ref[pl.ds(start, tq), :], lo)
            cs = [c0, c1]
            for hh in range(2):
                z, lg = _log_one_minus_beta(qh[hh], k, causal, diag)
                a = jnp.exp(z + lg + cs[hh] + _running_sum(lg, tri_gt))
                if diag:
                    a = jnp.where(causal, a, 0.0)
                cs[hh] = cs[hh] + jnp.sum(lg, axis=-1, keepdims=True)
                acc = acc + jnp.dot(a.astype(BF16), vh[hh], preferred_element_type=F32)
            return acc, cs[0], cs[1]

        zc = jnp.zeros((tq, 1), F32)
        carry = block(qi, (jnp.zeros((tq, LANES), F32), zc, zc), True)
        carry = lax.fori_loop(0, qi, lambda j, cr: block(qi - 1 - j, cr, False), carry)
        o_ref[...] = carry[0]
        lsum_ref[...] = jnp.where(lo, carry[1], carry[2])

    blk = pl.BlockSpec((tq, LANES), lambda hp, qi: (qi, hp))
    o512 = jax.ShapeDtypeStruct((s, D_SB), F32)
    return _call(
        body, name=name, grid=(4, nq),
        out_shape=(o512, o512),
        in_specs=[blk, pl.BlockSpec((s, LANES), lambda hp, qi: (0, 4 + hp)),
                  pl.BlockSpec((s, LANES), lambda hp, qi: (0, 8 + hp))],
        out_specs=(blk, blk),
        compiler_params=_params(("parallel", "parallel"), VMEM_LIMIT),
    )(qkv, qkv, qkv)


def _conv_taps(cc_ref, ch_ref, ccp_ref, chp_ref, first, ts):
    u = cc_ref[...] * ch_ref[...]
    keep = jnp.where(first, 0.0, 1.0)
    p6 = ccp_ref[6:7, :] * chp_ref[6:7, :] * keep
    p7 = ccp_ref[7:8, :] * chp_ref[7:8, :] * keep
    rowi = lax.broadcasted_iota(jnp.int32, u.shape, 0)
    u1 = jnp.where(rowi == 0, p7, pltpu.roll(u, 1, 0))
    u2 = jnp.where(rowi == 0, p6, jnp.where(rowi == 1, p7, pltpu.roll(u, 2, 0)))
    return u, u1, u2


def _fwd_mid(x, pc, az, ya, p4, layer, cw, cb, bg, wout_full, pg, wpg_full, bpg, wpe_full, name):
    s = x.shape[0]
    ts = min(ROW_TILE, s)
    blk8 = ts // SUBLANES

    def body(x_ref, cb_ref_, cc_ref, ch_ref, cz_ref, ccp_ref, chp_ref, az_ref, ya_ref, p_ref,
             cw_ref, cbias_ref, bg_ref, wout_ref, pg_ref, wpg_ref, bpg_ref, wpe_ref,
             x2_ref, x3_ref, gated_ref, h2_ref, gate_ref, e_ref):
        i = pl.program_id(0)
        lane = lax.broadcasted_iota(jnp.int32, (1, LANES), 1)
        lo = lane < HEAD_DIM
        u, u1, u2 = _conv_taps(cc_ref, ch_ref, ccp_ref, chp_ref, i == 0, ts)
        conv = cbias_ref[...] + cw_ref[0:1, :] * u2 + cw_ref[1:2, :] * u1 + cw_ref[2:3, :] * u
        yc = cb_ref_[...] * conv
        for sl in range(8):
            cols = slice(LANES * (sl % 4), LANES * (sl % 4 + 1))
            y = yc[:, cols] if sl < 4 else ya_ref[:, cols]
            zc = cz_ref[:, cols] if sl < 4 else az_ref[:, cols]
            rg = lax.rsqrt(_group_bcast_sum(y * y, lo) * (1.0 / HEAD_DIM) + EPS)
            yn = y * rg * bg_ref[:, LANES * sl:LANES * (sl + 1)]
            gated_ref[:, LANES * sl:LANES * (sl + 1)] = (yn * (zc * _sigmoid(zc))).astype(BF16)
        x2 = x_ref[...] + jnp.dot(gated_ref[...], wout_ref[...], preferred_element_type=F32)
        x2_ref[...] = x2
        r2 = lax.rsqrt(jnp.mean(x2 * x2, axis=-1, keepdims=True) + EPS)
        h2 = (x2 * r2 * pg_ref[...]).astype(BF16)
        h2_ref[...] = h2
        gate = _sigmoid(jnp.dot(h2, wpg_ref[...], preferred_element_type=F32) + bpg_ref[...])
        gate_ref[...] = gate
        e = jnp.dot(p_ref[...].astype(BF16), wpe_ref[...], preferred_element_type=F32)
        e_ref[...] = e
        x3_ref[...] = x2 + gate * e

    row = lambda width, cb_=0: pl.BlockSpec((ts, width), lambda i: (i, cb_))
    prev = lambda cb_: pl.BlockSpec((SUBLANES, 512), lambda i: (jnp.maximum(i * blk8 - 1, 0), cb_))
    vec = lambda width: pl.BlockSpec((1, width), lambda i: (0, 0))
    wspec = lambda r_, c_: pl.BlockSpec((None, r_, c_), lambda i: (layer, 0, 0))
    f32o = jax.ShapeDtypeStruct((s, D_MODEL), F32)
    bfo = jax.ShapeDtypeStruct((s, D_MODEL), BF16)
    return _call(
        body, name=name, grid=(s // ts,),
        out_shape=(f32o, f32o, bfo, bfo, f32o, f32o),
        in_specs=[row(D_MODEL), row(512, 0), row(512, 1), row(512, 2), row(512, 3), prev(1), prev(2),
                  row(512), row(512),
                  pl.BlockSpec((None, None, ts, PLE_DIM), lambda i: (layer, 0, i, 0)),
                  pl.BlockSpec((3, 512), lambda i: (0, 0)), vec(512), vec(D_MODEL),
                  wspec(D_MODEL, D_MODEL), vec(D_MODEL), wspec(D_MODEL, D_MODEL), vec(D_MODEL),
                  wspec(PLE_DIM, D_MODEL)],
        out_specs=(row(D_MODEL),) * 6,
        compiler_params=_params(("parallel",), VMEM_LIMIT),
    )(x, pc, pc, pc, pc, pc, pc, az, ya, p4, cw, cb, bg, wout_full, pg, wpg_full, bpg, wpe_full)


def _loss_head(xf, target, fg):
    s = xf.shape[0]
    ts = min(ROW_TILE, s)

    def body(x_ref, t_ref, g_ref, dx_ref, loss_ref, dg_ref):
        i = pl.program_id(0)

        @pl.when(i == 0)
        def _():
            loss_ref[...] = jnp.zeros_like(loss_ref)
            dg_ref[...] = jnp.zeros_like(dg_ref)

        x = x_ref[...]
        g = g_ref[...]
        r = lax.rsqrt(jnp.mean(x * x, axis=-1, keepdims=True) + EPS)
        xn = x * r
        err = xn * g - t_ref[...]
        per_row = jnp.sum(err * err, axis=-1, keepdims=True)
        loss_ref[...] += jnp.sum(per_row, axis=0, keepdims=True) * (0.5 / D_MODEL)
        dy = err * (1.0 / D_MODEL)
        dg_ref[...] += jnp.sum(dy * xn, axis=0, keepdims=True)
        dxn = dy * g
        dx_ref[...] = r * (dxn - xn * jnp.mean(dxn * xn, axis=-1, keepdims=True))

    row = pl.BlockSpec((ts, D_MODEL), lambda i: (i, 0))
    return _call(
        body, name="loss_head", grid=(s // ts,),
        out_shape=(jax.ShapeDtypeStruct((s, D_MODEL), F32), jax.ShapeDtypeStruct((1, LANES), F32),
                   jax.ShapeDtypeStruct((1, D_MODEL), F32)),
        in_specs=[row, row, pl.BlockSpec((1, D_MODEL), lambda i: (0, 0))],
        out_specs=(row, pl.BlockSpec((1, LANES), lambda i: (0, 0)), pl.BlockSpec((1, D_MODEL), lambda i: (0, 0))),
        compiler_params=_params(("arbitrary",), VMEM_LIMIT),
    )(xf, target, fg)


def _bwd_mid(dx3, x2, gate, e, pc, az, ya, layer, cw, cb, bg, pg, wpg_full, wout_full, name):
    s = x2.shape[0]
    ts = min(ROW_TILE, s)
    blk8 = ts // SUBLANES

    def body(dx3_ref, x2_ref, gate_ref, e_ref, cb_ref_, cc_ref, ch_ref, cz_ref, ccp_ref, chp_ref, az_ref, ya_ref,
             cw_ref, cbias_ref, bg_ref, pg_ref, wpg_ref, wout_ref,
             dx2_ref, dx2b_ref, dgpre_ref, de_ref, dya_ref, dmisc_ref,
             dbpg_ref, dpg_ref, dbg_ref, dcbias_ref, dcw_ref, dgated_ref):
        i = pl.program_id(0)

        @pl.when(i == 0)
        def _():
            for ref in (dbpg_ref, dpg_ref, dbg_ref, dcbias_ref, dcw_ref):
                ref[...] = jnp.zeros_like(ref)

        lane = lax.broadcasted_iota(jnp.int32, (1, LANES), 1)
        lo = lane < HEAD_DIM
        dx3 = dx3_ref[...]
        gate = gate_ref[...]
        de_ref[...] = (dx3 * gate).astype(BF16)
        dgpre = dx3 * e_ref[...] * gate * (1.0 - gate)
        dbpg_ref[...] += jnp.sum(dgpre, axis=0, keepdims=True)
        dgpre_b = dgpre.astype(BF16)
        dgpre_ref[...] = dgpre_b
        dh2 = lax.dot_general(dgpre_b, wpg_ref[...], NT, preferred_element_type=F32)
        x2 = x2_ref[...]
        r2 = lax.rsqrt(jnp.mean(x2 * x2, axis=-1, keepdims=True) + EPS)
        xn2 = x2 * r2
        dpg_ref[...] += jnp.sum(dh2 * xn2, axis=0, keepdims=True)
        dxn = dh2 * pg_ref[...]
        dx2 = dx3 + r2 * (dxn - xn2 * jnp.mean(dxn * xn2, axis=-1, keepdims=True))
        dx2_ref[...] = dx2
        dx2_b = dx2.astype(BF16)
        dx2b_ref[...] = dx2_b
        dgated_ref[...] = lax.dot_general(dx2_b, wout_ref[...], NT, preferred_element_type=F32)

        u, u1, u2 = _conv_taps(cc_ref, ch_ref, ccp_ref, chp_ref, i == 0, ts)
        conv = cbias_ref[...] + cw_ref[0:1, :] * u2 + cw_ref[1:2, :] * u1 + cw_ref[2:3, :] * u
        c_b = cb_ref_[...]
        yc = c_b * conv
        for sl in range(8):
            cols = slice(LANES * (sl % 4), LANES * (sl % 4 + 1))
            wide = slice(LANES * sl, LANES * (sl + 1))
            y = yc[:, cols] if sl < 4 else ya_ref[:, cols]
            zc = cz_ref[:, cols] if sl < 4 else az_ref[:, cols]
            bgs = bg_ref[:, wide]
            dgt = dgated_ref[:, wide]
            rg = lax.rsqrt(_group_bcast_sum(y * y, lo) * (1.0 / HEAD_DIM) + EPS)
            yhat = y * rg
            sig = _sigmoid(zc)
            dyn = dgt * (zc * sig)
            dzc = dgt * (yhat * bgs) * (sig * (1.0 + zc * (1.0 - sig)))
            dbg_ref[:, wide] += jnp.sum(dyn * yhat, axis=0, keepdims=True)
            dyh = dyn * bgs
            dy = rg * (dyh - yhat * (_group_bcast_sum(dyh * yhat, lo) * (1.0 / HEAD_DIM)))
            if sl < 4:
                dconv = dy * c_b[:, cols]
                dmisc_ref[:, cols] = dy * conv[:, cols]
                dmisc_ref[:, 512 + LANES * sl:512 + LANES * (sl + 1)] = dconv
                dmisc_ref[:, 1024 + LANES * sl:1024 + LANES * (sl + 1)] = dzc
                dcbias_ref[:, cols] += jnp.sum(dconv, axis=0, keepdims=True)
                dcw_ref[0:1, cols] += jnp.sum(dconv * u2[:, cols], axis=0, keepdims=True)
                dcw_ref[1:2, cols] += jnp.sum(dconv * u1[:, cols], axis=0, keepdims=True)
                dcw_ref[2:3, cols] += jnp.sum(dconv * u[:, cols], axis=0, keepdims=True)
            else:
                dya_ref[:, cols] = dy
                dmisc_ref[:, 1536 + LANES * (sl - 4):1536 + LANES * (sl - 3)] = dzc

    row = lambda width, cb_=0: pl.BlockSpec((ts, width), lambda i: (i, cb_))
    prev = lambda cb_: pl.BlockSpec((SUBLANES, 512), lambda i: (jnp.maximum(i * blk8 - 1, 0), cb_))
    vec = lambda width: pl.BlockSpec((1, width), lambda i: (0, 0))
    wspec = lambda r_, c_: pl.BlockSpec((None, r_, c_), lambda i: (layer, 0, 0))
    f32o = jax.ShapeDtypeStruct((s, D_MODEL), F32)
    bfo = jax.ShapeDtypeStruct((s, D_MODEL), BF16)
    vo = lambda width: jax.ShapeDtypeStruct((1, width), F32)
    return _call(
        body, name=name, grid=(s // ts,),
        out_shape=(f32o, bfo, bfo, bfo, jax.ShapeDtypeStruct((s, 512), F32), jax.ShapeDtypeStruct((s, 2048), F32),
                   vo(D_MODEL), vo(D_MODEL), vo(D_MODEL), vo(512), jax.ShapeDtypeStruct((SUBLANES, 512), F32)),
        in_specs=[row(D_MODEL), row(D_MODEL), row(D_MODEL), row(D_MODEL),
                  row(512, 0), row(512, 1), row(512, 2), row(512, 3), prev(1), prev(2), row(512), row(512),
                  pl.BlockSpec((3, 512), lambda i: (0, 0)), vec(512), vec(D_MODEL), vec(D_MODEL),
                  wspec(D_MODEL, D_MODEL), wspec(D_MODEL, D_MODEL)],
        out_specs=(row(D_MODEL), row(D_MODEL), row(D_MODEL), row(D_MODEL), row(512), row(2048),
                   vec(D_MODEL), vec(D_MODEL), vec(D_MODEL), vec(512),
                   pl.BlockSpec((SUBLANES, 512), lambda i: (0, 0))),
        scratch_shapes=[pltpu.VMEM((ts, D_MODEL), F32)],
        compiler_params=_params(("arbitrary",), VMEM_LIMIT),
    )(dx3, x2, gate, e, pc, pc, pc, pc, pc, pc, az, ya, cw, cb, bg, pg, wpg_full, wout_full)


def _attn_bwd(qkv, lsum, dya, name):
    s = qkv.shape[0]
    tq = min(ATTN_TILE, s)
    nq = s // tq

    def body(q_ref, k_ref, v_ref, lsum_ref, do_ref, dq_ref, dk_ref, dv_ref):
        qi = pl.program_id(1)

        @pl.when(qi == 0)
        def _():
            dk_ref[...] = jnp.zeros_like(dk_ref)
            dv_ref[...] = jnp.zeros_like(dv_ref)

        lo, causal, _, tri_le = _attn_pieces(tq)
        lane = lax.broadcasted_iota(jnp.int32, (1, LANES), 1)
        qh = _split_heads(q_ref[...], lo)
        lt = lsum_ref[...]
        ltot = (jnp.sum(jnp.where(lane == 0, lt, 0.0), axis=-1, keepdims=True),
                jnp.sum(jnp.where(lane == HEAD_DIM, lt, 0.0), axis=-1, keepdims=True))
        doh = _split_heads(do_ref[...].astype(BF16), lo)

        def block(kb, carry, diag):
            dq, cl0, cl1, cg0, cg1 = carry
            start = pl.multiple_of(kb * tq, tq)
            k = k_ref[pl.ds(start, tq), :]
            v = v_ref[pl.ds(start, tq), :]
            kh = _split_heads(k, lo)
            cl, cg = [cl0, cl1], [cg0, cg1]
            dk_blk = jnp.zeros((tq, LANES), F32)
            dv_blk = jnp.zeros((tq, LANES), F32)
            for hh in range(2):
                z, lg = _log_one_minus_beta(qh[hh], k, causal, diag)
                a = jnp.exp(z + lg + (ltot[hh] - cl[hh] - _running_sum(lg, tri_le)))
                if diag:
                    a = jnp.where(causal, a, 0.0)
                da = lax.dot_general(doh[hh], v, NT, preferred_element_type=F32)
                g = a * da
                dz = g - jnp.exp(z + lg) * (cg[hh] + _running_sum(g, tri_le))
                if diag:
                    dz = jnp.where(causal, dz, 0.0)
                cl[hh] = cl[hh] + jnp.sum(lg, axis=-1, keepdims=True)
                cg[hh] = cg[hh] + jnp.sum(g, axis=-1, keepdims=True)
                dz_b = dz.astype(BF16)
                dq = dq + jnp.dot(dz_b, kh[hh], preferred_element_type=F32)
                dk_blk = dk_blk + lax.dot_general(dz_b, qh[hh], TN, preferred_element_type=F32)
                dv_blk = dv_blk + lax.dot_general(a.astype(BF16), doh[hh], TN, preferred_element_type=F32)
            dk_ref[pl.ds(start, tq), :] += dk_blk
            dv_ref[pl.ds(start, tq), :] += dv_blk
            return dq, cl[0], cl[1], cg[0], cg[1]

        zc = jnp.zeros((tq, 1), F32)
        carry = (jnp.zeros((tq, LANES), F32), zc, zc, zc, zc)
        carry = lax.fori_loop(0, qi, lambda kb, cr: block(kb, cr, False), carry)
        carry = block(qi, carry, True)
        dq_ref[...] = carry[0] * 0.125

    blk = pl.BlockSpec((tq, LANES), lambda hp, qi: (qi, hp))
    col = pl.BlockSpec((s, LANES), lambda hp, qi: (0, hp))
    o512 = jax.ShapeDtypeStruct((s, D_SB), F32)
    return _call(
        body, name=name, grid=(4, nq),
        out_shape=(o512, o512, o512),
        in_specs=[blk, pl.BlockSpec((s, LANES), lambda hp, qi: (0, 4 + hp)),
                  pl.BlockSpec((s, LANES), lambda hp, qi: (0, 8 + hp)), blk, blk],
        out_specs=(blk, col, col),
        compiler_params=_params(("parallel", "arbitrary"), VMEM_LIMIT),
    )(qkv, qkv, qkv, lsum, dya)


def _bwd_dproj(dmisc, pc, dq, dk, dv, x, dx2, g, cw, win_full, layer, name):
    s = x.shape[0]
    ts = min(ROW_TILE, s)
    blk8 = ts // SUBLANES
    last8 = s // SUBLANES - 1

    def body(dcb_ref, dconv_ref, dcz_ref, daz_ref, nxt_ref, cc_ref, ch_ref, dq_ref, dk_ref, dv_ref,
             x_ref, dx2_ref, g_ref, cw_ref, w_ref, dproj_ref, dx_ref, dg_ref):
        i = pl.program_id(0)

        @pl.when(i == 0)
        def _():
            dg_ref[...] = jnp.zeros_like(dg_ref)

        keep = jnp.where(i == pl.num_programs(0) - 1, 0.0, 1.0)
        dc = dconv_ref[...]
        n0 = nxt_ref[0:1, :] * keep
        n1 = nxt_ref[1:2, :] * keep
        rowi = lax.broadcasted_iota(jnp.int32, dc.shape, 0)
        dc1 = jnp.where(rowi == ts - 1, n0, pltpu.roll(dc, ts - 1, 0))
        dc2 = jnp.where(rowi == ts - 2, n0, jnp.where(rowi == ts - 1, n1, pltpu.roll(dc, ts - 2, 0)))
        du = cw_ref[2:3, :] * dc + cw_ref[1:2, :] * dc1 + cw_ref[0:1, :] * dc2
        dproj_ref[:, 0:512] = dcb_ref[...].astype(BF16)
        dproj_ref[:, 512:1024] = (du * ch_ref[...]).astype(BF16)
        dproj_ref[:, 1024:1536] = (du * cc_ref[...]).astype(BF16)
        dproj_ref[:, 1536:2048] = dcz_ref[...].astype(BF16)
        dproj_ref[:, 2048:2560] = dq_ref[...].astype(BF16)
        dproj_ref[:, 2560:3072] = dk_ref[...].astype(BF16)
        dproj_ref[:, 3072:3584] = dv_ref[...].astype(BF16)
        dproj_ref[:, 3584:4096] = daz_ref[...].astype(BF16)
        dh = lax.dot_general(dproj_ref[...], w_ref[...], NT, preferred_element_type=F32)
        x = x_ref[...]
        r = lax.rsqrt(jnp.mean(x * x, axis=-1, keepdims=True) + EPS)
        xn = x * r
        dg_ref[...] += jnp.sum(dh * xn, axis=0, keepdims=True)
        dxn = dh * g_ref[...]
        dx_ref[...] = dx2_ref[...] + r * (dxn - xn * jnp.mean(dxn * xn, axis=-1, keepdims=True))

    row = lambda width, cb_=0: pl.BlockSpec((ts, width), lambda i: (i, cb_))
    nxt = pl.BlockSpec((SUBLANES, 512), lambda i: (jnp.minimum((i + 1) * blk8, last8), 1))
    vec = lambda width: pl.BlockSpec((1, width), lambda i: (0, 0))
    return _call(
        body, name=name, grid=(s // ts,),
        out_shape=(jax.ShapeDtypeStruct((s, N_IN), BF16), jax.ShapeDtypeStruct((s, D_MODEL), F32),
                   jax.ShapeDtypeStruct((1, D_MODEL), F32)),
        in_specs=[row(512, 0), row(512, 1), row(512, 2), row(512, 3), nxt, row(512, 1), row(512, 2),
                  row(512), row(512), row(512), row(D_MODEL), row(D_MODEL), vec(D_MODEL),
                  pl.BlockSpec((3, 512), lambda i: (0, 0)),
                  pl.BlockSpec((None, D_MODEL, N_IN), lambda i: (layer, 0, 0))],
        out_specs=(row(N_IN), row(D_MODEL), vec(D_MODEL)),
        compiler_params=_params(("arbitrary",), VMEM_LIMIT),
    )(dmisc, dmisc, dmisc, dmisc, dmisc, pc, pc, dq, dk, dv, x, dx2, g, cw, win_full)


def _atb(a, b, name, a_index=None):
    s, n = b.shape
    m = a.shape[-1]
    ts = min(512, s)
    tn = min(1024, n)
    if a_index is None:
        a_spec = pl.BlockSpec((ts, m), lambda j, i: (i, 0))
    else:
        a_spec = pl.BlockSpec((None, None, ts, m), lambda j, i: (a_index, 0, i, 0))

    def body(a_ref, b_ref, o_ref):
        @pl.when(pl.program_id(1) == 0)
        def _():
            o_ref[...] = jnp.zeros_like(o_ref)

        o_ref[...] += lax.dot_general(a_ref[...].astype(BF16), b_ref[...], TN, preferred_element_type=F32)

    return _call(
        body, name=name, grid=(n // tn, s // ts),
        out_shape=jax.ShapeDtypeStruct((m, n), F32),
        in_specs=[a_spec, pl.BlockSpec((ts, tn), lambda j, i: (i, j))],
        out_specs=pl.BlockSpec((m, tn), lambda j, i: (0, j)),
        compiler_params=_params(("parallel", "arbitrary"), VMEM_LIMIT),
    )(a, b)


def _exchange_grads(dwin, dwout, dwpg, dwpe, small):
    n_big = 4 * DEPTH
    n_cp = n_big + 1

    def body(*refs):
        ins = refs[:n_cp]
        win_r, wout_r, wpg_r, wpe_r, small_r = refs[n_cp:n_cp + 5]
        send_sems, recv_sems, local_sems = refs[-3:]
        x, y, c = lax.axis_index("x"), lax.axis_index("y"), lax.axis_index("c")
        me_blk = 4 * x + 2 * y + c

        def src_slab(t, blk):
            kind, layer = divmod(t, DEPTH)
            ref = ins[t]
            if kind == 0:
                return ref.at[:, pl.ds(blk * 512, 512)]
            if kind in (1, 2):
                return ref.at[pl.ds(blk * 128, 128), :]
            return ref.at[:, pl.ds(blk * 128, 128)]

        def dst_slot(t, blk):
            kind, layer = divmod(t, DEPTH)
            return (win_r, wout_r, wpg_r, wpe_r)[kind].at[blk, layer]

        copies = []
        for t in range(n_big):
            cp = pltpu.make_async_copy(src_slab(t, me_blk), dst_slot(t, me_blk), local_sems.at[t])
            cp.start()
            copies.append(cp)
        cp = pltpu.make_async_copy(ins[n_big], small_r.at[me_blk], local_sems.at[n_big])
        cp.start()
        copies.append(cp)

        remote = []
        for k in range(1, N_DEV):
            px = 1 - x if k & 4 else x
            py = 1 - y if k & 2 else y
            pc_ = 1 - c if k & 1 else c
            peer_blk = 4 * px + 2 * py + pc_
            for t in range(n_cp):
                if t < n_big:
                    src, dst = src_slab(t, peer_blk), dst_slot(t, me_blk)
                else:
                    src, dst = ins[n_big], small_r.at[me_blk]
                rc = pltpu.make_async_remote_copy(
                    src_ref=src, dst_ref=dst, send_sem=send_sems.at[k - 1, t], recv_sem=recv_sems.at[k - 1, t],
                    device_id=(px, py, pc_), device_id_type=pl.DeviceIdType.MESH)
                rc.start()
                remote.append(rc)
        for rc in remote:
            rc.wait_recv()
        for rc in remote:
            rc.wait_send()
        for cp in copies:
            cp.wait()

    any_spec = pl.BlockSpec(memory_space=pl.ANY)
    args = [*dwin, *dwout, *dwpg, *dwpe, small]
    return _call(
        body, name="exchange_grads",
        out_shape=(jax.ShapeDtypeStruct((N_DEV, DEPTH, D_MODEL, 512), F32),
                   jax.ShapeDtypeStruct((N_DEV, DEPTH, 128, D_MODEL), F32),
                   jax.ShapeDtypeStruct((N_DEV, DEPTH, 128, D_MODEL), F32),
                   jax.ShapeDtypeStruct((N_DEV, DEPTH, PLE_DIM, 128), F32),
                   jax.ShapeDtypeStruct((N_DEV, SMALL_ROWS, LANES), F32)),
        in_specs=[any_spec] * len(args), out_specs=[any_spec] * 5,
        scratch_shapes=[pltpu.SemaphoreType.DMA((N_DEV - 1, n_cp)), pltpu.SemaphoreType.DMA((N_DEV - 1, n_cp)),
                        pltpu.SemaphoreType.DMA((n_cp,))],
    )(*args)


def _adamw_math(w, g, m, v):
    m2 = ADAM_B1 * m + (1.0 - ADAM_B1) * g
    v2 = ADAM_B2 * v + (1.0 - ADAM_B2) * (g * g)
    m_hat = m2 / (1.0 - ADAM_B1 ** ADAM_STEP)
    v_hat = v2 / (1.0 - ADAM_B2 ** ADAM_STEP)
    delta = -ADAM_LR * (m_hat / (jnp.sqrt(v_hat) + ADAM_EPS) + ADAM_WD * w)
    return delta, m2, v2


def _adamw_sum8(parts, w, m, v, name):
    _, rows, cols = parts.shape
    tr = min(rows, 256)

    def body(p_ref, w_ref, m_ref, v_ref, g_ref, d_ref, m2_ref, v2_ref):
        g = p_ref[0]
        for d in range(1, N_DEV):
            g = g + p_ref[d]
        g_ref[...] = g
        d_ref[...], m2_ref[...], v2_ref[...] = _adamw_math(w_ref[...], g, m_ref[...], v_ref[...])

    tile = pl.BlockSpec((tr, cols), lambda i: (i, 0))
    o = jax.ShapeDtypeStruct((rows, cols), F32)
    return _call(
        body, name=name, grid=(rows // tr,),
        out_shape=(o, o, o, o),
        in_specs=[pl.BlockSpec((N_DEV, tr, cols), lambda i: (0, i, 0)), tile, tile, tile],
        out_specs=(tile, tile, tile, tile),
        compiler_params=_params(("parallel",), VMEM_LIMIT),
    )(parts, w, m, v)


def _adamw_plain(g, w, m, v, name):
    rows, cols = g.shape

    def body(g_ref, w_ref, m_ref, v_ref, d_ref, m2_ref, v2_ref):
        d_ref[...], m2_ref[...], v2_ref[...] = _adamw_math(w_ref[...], g_ref[...], m_ref[...], v_ref[...])

    full = pl.BlockSpec((rows, cols), lambda: (0, 0))
    o = jax.ShapeDtypeStruct((rows, cols), F32)
    return _call(body, name=name, out_shape=(o, o, o), in_specs=[full] * 4, out_specs=(full,) * 3)(g, w, m, v)


def _sum8_small(parts):
    def body(p_ref, g_ref):
        g = p_ref[0]
        for d in range(1, N_DEV):
            g = g + p_ref[d]
        g_ref[...] = g

    return _call(
        body, name="sum_small_grads",
        out_shape=jax.ShapeDtypeStruct((SMALL_ROWS, LANES), F32),
        in_specs=[pl.BlockSpec((N_DEV, SMALL_ROWS, LANES), lambda: (0, 0, 0))],
        out_specs=pl.BlockSpec((SMALL_ROWS, LANES), lambda: (0, 0)),
    )(parts)


def kernel(x, p, norm_g, w_in, conv_w, conv_b, branch_g, w_out, ple_norm_g, w_pg, b_pg, w_pe, final_g, loss_target, m_norm_g, m_w_in, m_conv_w, m_conv_b, m_branch_g, m_w_out, m_ple_norm_g, m_w_pg, m_b_pg, m_w_pe, m_final_g, v_norm_g, v_w_in, v_conv_w, v_conv_b, v_branch_g, v_w_out, v_ple_norm_g, v_w_pg, v_b_pg, v_w_pe, v_final_g):
    s = x.shape[1]
    x0 = x.reshape(s, D_MODEL)
    target = loss_target.reshape(s, D_MODEL)
    me_blk = _my_block()

    win_s = _cast_bf16(w_in.reshape(DEPTH * D_MODEL, 512), "cast_w_in").reshape(DEPTH, D_MODEL, 512)
    wout_s = _cast_bf16(w_out.reshape(DEPTH * 128, D_MODEL), "cast_w_out").reshape(DEPTH, 128, D_MODEL)
    wpg_s = _cast_bf16(w_pg.reshape(DEPTH * 128, D_MODEL), "cast_w_pg").reshape(DEPTH, 128, D_MODEL)
    wpe_s = _cast_bf16(w_pe.reshape(DEPTH * PLE_DIM, 128), "cast_w_pe").reshape(DEPTH, PLE_DIM, 128)
    cw_s = jnp.zeros((SUBLANES, LANES), F32).at[:DEPTH * 3, :HEAD_DIM].set(conv_w.reshape(DEPTH * 3, HEAD_DIM))
    win_f, wout_f, wpg_f, wpe_f, cw_all = _allgather_weights(win_s, wout_s, wpg_s, wpe_s, cw_s)
    cw_full = jnp.transpose(cw_all[:, :DEPTH * 3, :HEAD_DIM].reshape(N_DEV, DEPTH, 3, HEAD_DIM), (1, 2, 0, 3))
    cw_full = cw_full.reshape(DEPTH, 3, D_CONV)

    vec = lambda a, l: a[l][None, :]

    saved = []
    xl = x0
    for l in range(DEPTH):
        h, pc, qkv, az = _fwd_in(xl, vec(norm_g, l), win_f, l, f"fwd_in_{l}")
        ya, lsum = _attn_fwd(qkv, f"attn_fwd_{l}")
        x2, x3, gated, h2, gate, e = _fwd_mid(
            xl, pc, az, ya, p, l, cw_full[l], vec(conv_b, l), vec(branch_g, l), wout_f,
            vec(ple_norm_g, l), wpg_f, vec(b_pg, l), wpe_f, f"fwd_mid_{l}")
        saved.append(dict(x=xl, h=h, pc=pc, qkv=qkv, az=az, ya=ya, lsum=lsum, x2=x2, gated=gated, h2=h2,
                          gate=gate, e=e))
        xl = x3

    dx, loss_acc, d_final_g = _loss_head(xl, target, final_g[None, :])
    loss = lax.psum(loss_acc[0, 0], ("x", "y", "c"))

    dwin, dwout, dwpg, dwpe = [None] * DEPTH, [None] * DEPTH, [None] * DEPTH, [None] * DEPTH
    small = dict(norm_g=[None] * DEPTH, conv_b=[None] * DEPTH, branch_g=[None] * DEPTH,
                 ple_norm_g=[None] * DEPTH, b_pg=[None] * DEPTH, conv_w=[None] * DEPTH)
    for l in reversed(range(DEPTH)):
        sv = saved[l]
        (dx2, dx2_b, dgpre_b, de_b, dya, dmisc, d_bpg, d_pg, d_bg, d_cbias, d_cw) = _bwd_mid(
            dx, sv["x2"], sv["gate"], sv["e"], sv["pc"], sv["az"], sv["ya"], l, cw_full[l], vec(conv_b, l),
            vec(branch_g, l), vec(ple_norm_g, l), wpg_f, wout_f, f"bwd_mid_{l}")
        dq, dk, dv = _attn_bwd(sv["qkv"], sv["lsum"], dya, f"attn_bwd_{l}")
        dproj, dx, d_ng = _bwd_dproj(dmisc, sv["pc"], dq, dk, dv, sv["x"], dx2, vec(norm_g, l), cw_full[l],
                                     win_f, l, f"bwd_dproj_{l}")
        dwin[l] = _atb(sv["h"], dproj, f"dw_in_{l}")
        dwout[l] = _atb(sv["gated"], dx2_b, f"dw_out_{l}")
        dwpg[l] = _atb(sv["h2"], dgpre_b, f"dw_pg_{l}")
        dwpe[l] = _atb(p, de_b, f"dw_pe_{l}", a_index=l)
        small["norm_g"][l], small["conv_b"][l], small["branch_g"][l] = d_ng, d_cbias, d_bg
        small["ple_norm_g"][l], small["b_pg"][l], small["conv_w"][l] = d_pg, d_bpg, d_cw[:3]
    grad_x = dx.reshape(1, s, D_MODEL)

    flat = lambda parts: jnp.concatenate([a.reshape(-1) for a in parts])
    small_vec = jnp.concatenate([
        flat(small["norm_g"]), flat(small["conv_b"]), flat(small["branch_g"]), flat(small["ple_norm_g"]),
        flat(small["b_pg"]), d_final_g.reshape(-1), flat(small["conv_w"])]).reshape(SMALL_ROWS, LANES)
    r_in, r_out, r_pg, r_pe, r_small = _exchange_grads(dwin, dwout, dwpg, dwpe, small_vec)

    def big(parts, w, m, v, name):
        shape = w.shape
        rows = shape[0] * shape[1]
        to2d = lambda a: a.reshape(rows, shape[2])
        outs = _adamw_sum8(parts.reshape(N_DEV, rows, shape[2]), to2d(w), to2d(m), to2d(v), name)
        return [o.reshape(shape) for o in outs]

    g_win, d_win, m_win, v_win = big(r_in, w_in, m_w_in, v_w_in, "adamw_w_in")
    g_wout, d_wout, m_wout, v_wout = big(r_out, w_out, m_w_out, v_w_out, "adamw_w_out")
    g_wpg, d_wpg, m_wpg, v_wpg = big(r_pg, w_pg, m_w_pg, v_w_pg, "adamw_w_pg")
    g_wpe, d_wpe, m_wpe, v_wpe = big(r_pe, w_pe, m_w_pe, v_w_pe, "adamw_w_pe")

    g_small = _sum8_small(r_small)
    repl = [(norm_g, m_norm_g, v_norm_g), (conv_b, m_conv_b, v_conv_b), (branch_g, m_branch_g, v_branch_g),
            (ple_norm_g, m_ple_norm_g, v_ple_norm_g), (b_pg, m_b_pg, v_b_pg), (final_g, m_final_g, v_final_g)]
    pack = lambda idx: jnp.concatenate([t[idx].reshape(-1) for t in repl]).reshape(SMALL_REPL_ROWS, LANES)
    g_repl = g_small[:SMALL_REPL_ROWS]
    d_repl, m_repl, v_repl = _adamw_plain(g_repl, pack(0), pack(1), pack(2), "adamw_replicated")

    def unpack(a):
        flat_a = a.reshape(-1)
        out, off = [], 0
        for t in repl:
            n = t[0].size
            out.append(flat_a[off:off + n].reshape(t[0].shape))
            off += n
        return out

    g_r, d_r, m_r, v_r = unpack(g_repl), unpack(d_repl), unpack(m_repl), unpack(v_repl)

    g_cw_full = g_small[SMALL_REPL_ROWS:].reshape(DEPTH, 3, D_CONV)
    g_cw = lax.dynamic_slice(g_cw_full, (0, 0, me_blk * HEAD_DIM), (DEPTH, 3, HEAD_DIM))
    pad_cw = lambda a: jnp.zeros((SUBLANES, LANES), F32).at[:3].set(a.reshape(3, LANES))
    v_cw_pad = jnp.ones((SUBLANES, LANES), F32).at[:3].set(v_conv_w.reshape(3, LANES))
    d_cw, m_cw, v_cw = _adamw_plain(pad_cw(g_cw), pad_cw(conv_w), pad_cw(m_conv_w), v_cw_pad, "adamw_conv_w")
    un_cw = lambda a: a[:3].reshape(DEPTH, 3, HEAD_DIM)

    def ordered(r, win_, cw_, wout_, wpg_, wpe_):
        return [r[0], win_, cw_, r[1], r[2], wout_, r[3], wpg_, r[4], wpe_, r[5]]

    grads = ordered(g_r, g_win, g_cw, g_wout, g_wpg, g_wpe)
    deltas = ordered(d_r, d_win, un_cw(d_cw), d_wout, d_wpg, d_wpe)
    new_m = ordered(m_r, m_win, un_cw(m_cw), m_wout, m_wpg, m_wpe)
    new_v = ordered(v_r, v_win, un_cw(v_cw), v_wout, v_wpg, v_wpe)
    return (loss, grad_x, *grads, *deltas, *new_m, *new_v)
```

```python
import jax
import jax.numpy as jnp
from jax import lax
from jax.experimental import pallas as pl
from jax.experimental.pallas import tpu as pltpu

F32 = jnp.float32
BF16 = jnp.bfloat16

D_MODEL = 1024
D_CONV = 512
D_SB = 512
N_IN = 4096
HEAD_DIM = 64
PLE_DIM = 256
DEPTH = 2
EPS = 1e-6
ADAM_LR = 0.001
ADAM_B1 = 0.9
ADAM_B2 = 0.999
ADAM_EPS = 1e-08
ADAM_WD = 0.01
ADAM_STEP = 10

LANES = 128
SUBLANES = 8
VMEM_BYTES_V7X = 64 * 1024 * 1024
VMEM_LIMIT = VMEM_BYTES_V7X - 8 * 1024 * 1024

N_DEV = 8
ROW_TILE = 256
ATTN_TILE = 256
SMALL_ROWS = 104
SMALL_REPL_ROWS = 80

NT = (((1,), (1,)), ((), ()))
TN = (((0,), (0,)), ((), ()))


def _call(body, **kw):
    return pl.pallas_call(body, **kw)


def _params(sem=None, vmem=None):
    return pltpu.CompilerParams(dimension_semantics=sem, vmem_limit_bytes=vmem)


def _sigmoid(z):
    return 1.0 / (1.0 + jnp.exp(-z))


def _group_bcast_sum(a, lo):
    s_lo = jnp.sum(jnp.where(lo, a, 0.0), axis=-1, keepdims=True)
    s_hi = jnp.sum(jnp.where(lo, 0.0, a), axis=-1, keepdims=True)
    return jnp.where(lo, s_lo, s_hi)


def _my_block():
    return 4 * lax.axis_index("x") + 2 * lax.axis_index("y") + lax.axis_index("c")


def _cast_bf16(a2d, name):
    rows, cols = a2d.shape
    tr = min(rows, 512)

    def body(a_ref, o_ref):
        o_ref[...] = a_ref[...].astype(BF16)

    return _call(
        body, name=name, grid=(rows // tr,),
        out_shape=jax.ShapeDtypeStruct((rows, cols), BF16),
        in_specs=[pl.BlockSpec((tr, cols), lambda i: (i, 0))],
        out_specs=pl.BlockSpec((tr, cols), lambda i: (i, 0)),
        compiler_params=_params(("parallel",)),
    )(a2d)


def _allgather_weights(win_s, wout_s, wpg_s, wpe_s, cw_s):
    n_t = 5

    def body(win_ref, wout_ref, wpg_ref, wpe_ref, cw_ref,
             win_o, wout_o, wpg_o, wpe_o, cw_o, send_sems, recv_sems, local_sems):
        x, y, c = lax.axis_index("x"), lax.axis_index("y"), lax.axis_index("c")
        me, sibling = (x, y, c), (x, y, 1 - c)
        chips = [(1 - x, y), (x, 1 - y), (1 - x, 1 - y)]
        srcs = [win_ref, wout_ref, wpg_ref, wpe_ref, cw_ref]
        outs = [win_o, wout_o, wpg_o, wpe_o, cw_o]

        def place(t, dev):
            b = 4 * dev[0] + 2 * dev[1] + dev[2]
            if t == 0:
                return outs[0].at[:, :, pl.ds(b * 512, 512)]
            if t in (1, 2):
                return outs[t].at[:, pl.ds(b * 128, 128), :]
            if t == 3:
                return outs[3].at[:, :, pl.ds(b * 128, 128)]
            return outs[4].at[b]

        def copy(t, k, block, to, own=False):
            return pltpu.make_async_remote_copy(
                src_ref=srcs[t] if own else place(t, block), dst_ref=place(t, block),
                send_sem=send_sems.at[t, k], recv_sem=recv_sems.at[t, k],
                device_id=to, device_id_type=pl.DeviceIdType.MESH)

        mine = [pltpu.make_async_copy(srcs[t], place(t, me), local_sems.at[t]) for t in range(n_t)]
        for cp in mine:
            cp.start()
        first = []
        for t in range(n_t):
            first.append(copy(t, 0, me, sibling, own=True))
            first += [copy(t, 1 + j, me, (*chip, c), own=True) for j, chip in enumerate(chips)]
        for cp in first:
            cp.start()
        passed = []
        for j, chip in enumerate(chips):
            for t in range(n_t):
                copy(t, 1 + j, (*chip, c), me).wait_recv()
                fwd = copy(t, 4 + j, (*chip, c), sibling)
                fwd.start()
                passed.append(fwd)
        for t in range(n_t):
            copy(t, 0, sibling, me).wait_recv()
            for j, chip in enumerate(chips):
                copy(t, 4 + j, (*chip, 1 - c), me).wait_recv()
        for cp in first + passed:
            cp.wait_send()
        for cp in mine:
            cp.wait()

    any_spec = pl.BlockSpec(memory_space=pl.ANY)
    return _call(
        body, name="allgather_weights",
        out_shape=(jax.ShapeDtypeStruct((DEPTH, D_MODEL, N_IN), BF16),
                   jax.ShapeDtypeStruct((DEPTH, D_MODEL, D_MODEL), BF16),
                   jax.ShapeDtypeStruct((DEPTH, D_MODEL, D_MODEL), BF16),
                   jax.ShapeDtypeStruct((DEPTH, PLE_DIM, D_MODEL), BF16),
                   jax.ShapeDtypeStruct((N_DEV, SUBLANES, LANES), F32)),
        in_specs=[any_spec] * 5, out_specs=[any_spec] * 5,
        scratch_shapes=[pltpu.SemaphoreType.DMA((n_t, 7)), pltpu.SemaphoreType.DMA((n_t, 7)),
                        pltpu.SemaphoreType.DMA((n_t,))],
    )(win_s, wout_s, wpg_s, wpe_s, cw_s)


def _fwd_in(x, g, w_full, layer, name):
    s = x.shape[0]
    ts = min(ROW_TILE, s)

    def body(x_ref, g_ref, w_ref, h_ref, pc_ref, qkv_ref, az_ref):
        xf = x_ref[...]
        r = lax.rsqrt(jnp.mean(xf * xf, axis=-1, keepdims=True) + EPS)
        h = (xf * r * g_ref[...]).astype(BF16)
        h_ref[...] = h
        pc_ref[...] = jnp.dot(h, w_ref[:, 0:2048], preferred_element_type=F32)
        q = jnp.dot(h, w_ref[:, 2048:2560], preferred_element_type=F32)
        qkv_ref[:, 0:512] = (q * 0.125).astype(BF16)
        qkv_ref[:, 512:1536] = jnp.dot(h, w_ref[:, 2560:3584], preferred_element_type=F32).astype(BF16)
        az_ref[...] = jnp.dot(h, w_ref[:, 3584:4096], preferred_element_type=F32)

    row = lambda width: pl.BlockSpec((ts, width), lambda i: (i, 0))
    return _call(
        body, name=name, grid=(s // ts,),
        out_shape=(jax.ShapeDtypeStruct((s, D_MODEL), BF16), jax.ShapeDtypeStruct((s, 2048), F32),
                   jax.ShapeDtypeStruct((s, 1536), BF16), jax.ShapeDtypeStruct((s, 512), F32)),
        in_specs=[row(D_MODEL), pl.BlockSpec((1, D_MODEL), lambda i: (0, 0)),
                  pl.BlockSpec((None, D_MODEL, N_IN), lambda i: (layer, 0, 0))],
        out_specs=(row(D_MODEL), row(2048), row(1536), row(512)),
        compiler_params=_params(("parallel",), VMEM_LIMIT),
    )(x, g, w_full)


ATTN_ROWS = 128
ATTN_DONE = 104.0


def _attn_pieces(tq, rc):
    lane = lax.broadcasted_iota(jnp.int32, (1, LANES), 1)
    lo = lane < HEAD_DIM
    row = lax.broadcasted_iota(jnp.int32, (tq, tq), 0)
    col = lax.broadcasted_iota(jnp.int32, (tq, tq), 1)
    tri_gt = jnp.where(row > col, 1.0, 0.0).astype(BF16)
    tri_le = jnp.where(row <= col, 1.0, 0.0).astype(BF16)
    rrow = lax.broadcasted_iota(jnp.int32, (rc, tq), 0)
    rcol = lax.broadcasted_iota(jnp.int32, (rc, tq), 1)
    causal = [rcol < rrow + r * rc for r in range(tq // rc)]
    return lo, causal, tri_gt, tri_le


def _split_heads(a, lo):
    z = jnp.zeros_like(a)
    return (jnp.where(lo, a, z), jnp.where(lo, z, a))


def _softplus(z, causal, diag):
    neg_abs = lax.bitcast_convert_type(lax.bitcast_convert_type(z, jnp.uint32) | jnp.uint32(0x80000000), F32)
    sp = jnp.maximum(z, 0.0) + jnp.log(1.0 + jnp.exp(neg_abs))
    if diag:
        sp = jnp.where(causal, sp, 0.0)
    return sp


def _attn_fwd(qkv, name):
    s = qkv.shape[0]
    tq = min(ATTN_TILE, s)
    nq = s // tq
    rc = min(ATTN_ROWS, tq)
    n_rc = tq // rc
    chains = [(r, hh) for r in range(n_rc) for hh in range(2)]

    def body(q_ref, k_ref, v_ref, o_ref, lsum_ref, nblk_ref):
        hp, qi = pl.program_id(0), pl.program_id(1)
        lo, causal, tri_gt, _ = _attn_pieces(tq, rc)
        qh = _split_heads(q_ref[...], lo)
        qc = {(r, hh): qh[hh][r * rc:(r + 1) * rc] for r, hh in chains}

        def block(kb, carry, diag):
            start = pl.multiple_of(kb * tq, tq)
            k = k_ref[pl.ds(start, tq), :]
            vh = _split_heads(v_ref[pl.ds(start, tq), :], lo)
            z = {ch: lax.dot_general(qc[ch], k, NT, preferred_element_type=F32) for ch in chains}
            sp = {ch: _softplus(z[ch], causal[ch[0]], diag) for ch in chains}
            later = {ch: jnp.dot(sp[ch].astype(BF16), tri_gt, preferred_element_type=F32) for ch in chains}
            a = {}
            for ch in chains:
                a[ch] = jnp.exp((z[ch] - sp[ch]) - (carry[ch[0]][1 + ch[1]] + later[ch]))
                if diag:
                    a[ch] = jnp.where(causal[ch[0]], a[ch], 0.0)
            pv = {ch: jnp.dot(a[ch].astype(BF16), vh[ch[1]], preferred_element_type=F32) for ch in chains}
            out = []
            for r in range(n_rc):
                acc = carry[r][0] + pv[(r, 0)] + pv[(r, 1)]
                cs = [carry[r][1 + hh] + jnp.sum(sp[(r, hh)], axis=-1, keepdims=True) for hh in range(2)]
                out.append((acc, cs[0], cs[1]))
            return tuple(out)

        def least(carry):
            m = jnp.minimum(carry[0][1], carry[0][2])
            for r in range(1, n_rc):
                m = jnp.minimum(m, jnp.minimum(carry[r][1], carry[r][2]))
            return jnp.min(m)

        zc = jnp.zeros((rc, 1), F32)
        carry = tuple((jnp.zeros((rc, LANES), F32), zc, zc) for _ in range(n_rc))
        carry = block(qi, carry, True)

        def go_on(st):
            return jnp.logical_and(st[0] < qi, st[1] < ATTN_DONE)

        def step(st):
            new = block(qi - 1 - st[0], st[2], False)
            return st[0] + 1, least(new), new

        walked, _, carry = lax.while_loop(go_on, step, (jnp.int32(0), least(carry), carry))
        for r in range(n_rc):
            o_ref[r * rc:(r + 1) * rc, :] = carry[r][0]
            lsum_ref[r * rc:(r + 1) * rc, :] = jnp.where(lo, carry[r][1], carry[r][2])
        nblk_ref[hp, qi] = walked.astype(F32)

    blk = pl.BlockSpec((tq, LANES), lambda hp, qi: (qi, hp))
    o512 = jax.ShapeDtypeStruct((s, D_SB), F32)
    return _call(
        body, name=name, grid=(4, nq),
        out_shape=(o512, o512, jax.ShapeDtypeStruct((4, nq), F32)),
        in_specs=[blk, pl.BlockSpec((s, LANES), lambda hp, qi: (0, 4 + hp)),
                  pl.BlockSpec((s, LANES), lambda hp, qi: (0, 8 + hp))],
        out_specs=(blk, blk, pl.BlockSpec(memory_space=pltpu.SMEM)),
        compiler_params=_params(("arbitrary", "arbitrary"), VMEM_LIMIT),
    )(qkv, qkv, qkv)


def _conv_taps(cc_ref, ch_ref, ccp_ref, chp_ref, first, ts):
    u = cc_ref[...] * ch_ref[...]
    keep = jnp.where(first, 0.0, 1.0)
    p6 = ccp_ref[6:7, :] * chp_ref[6:7, :] * keep
    p7 = ccp_ref[7:8, :] * chp_ref[7:8, :] * keep
    rowi = lax.broadcasted_iota(jnp.int32, u.shape, 0)
    u1 = jnp.where(rowi == 0, p7, pltpu.roll(u, 1, 0))
    u2 = jnp.where(rowi == 0, p6, jnp.where(rowi == 1, p7, pltpu.roll(u, 2, 0)))
    return u, u1, u2


def _fwd_mid(x, pc, az, ya, p4, layer, cw, cb, bg, wout_full, pg, wpg_full, bpg, wpe_full, name):
    s = x.shape[0]
    ts = min(ROW_TILE, s)
    blk8 = ts // SUBLANES

    def body(x_ref, cb_ref_, cc_ref, ch_ref, cz_ref, ccp_ref, chp_ref, az_ref, ya_ref, p_ref,
             cw_ref, cbias_ref, bg_ref, wout_ref, pg_ref, wpg_ref, bpg_ref, wpe_ref,
             x2_ref, x3_ref, gated_ref, h2_ref, gate_ref, e_ref):
        i = pl.program_id(0)
        lane = lax.broadcasted_iota(jnp.int32, (1, LANES), 1)
        lo = lane < HEAD_DIM
        u, u1, u2 = _conv_taps(cc_ref, ch_ref, ccp_ref, chp_ref, i == 0, ts)
        conv = cbias_ref[...] + cw_ref[0:1, :] * u2 + cw_ref[1:2, :] * u1 + cw_ref[2:3, :] * u
        yc = cb_ref_[...] * conv
        for sl in range(8):
            cols = slice(LANES * (sl % 4), LANES * (sl % 4 + 1))
            y = yc[:, cols] if sl < 4 else ya_ref[:, cols]
            zc = cz_ref[:, cols] if sl < 4 else az_ref[:, cols]
            rg = lax.rsqrt(_group_bcast_sum(y * y, lo) * (1.0 / HEAD_DIM) + EPS)
            yn = y * rg * bg_ref[:, LANES * sl:LANES * (sl + 1)]
            gated_ref[:, LANES * sl:LANES * (sl + 1)] = (yn * (zc * _sigmoid(zc))).astype(BF16)
        x2 = x_ref[...] + jnp.dot(gated_ref[...], wout_ref[...], preferred_element_type=F32)
        x2_ref[...] = x2
        r2 = lax.rsqrt(jnp.mean(x2 * x2, axis=-1, keepdims=True) + EPS)
        h2 = (x2 * r2 * pg_ref[...]).astype(BF16)
        h2_ref[...] = h2
        gate = _sigmoid(jnp.dot(h2, wpg_ref[...], preferred_element_type=F32) + bpg_ref[...])
        gate_ref[...] = gate
        e = jnp.dot(p_ref[...].astype(BF16), wpe_ref[...], preferred_element_type=F32)
        e_ref[...] = e
        x3_ref[...] = x2 + gate * e

    row = lambda width, cb_=0: pl.BlockSpec((ts, width), lambda i: (i, cb_))
    prev = lambda cb_: pl.BlockSpec((SUBLANES, 512), lambda i: (jnp.maximum(i * blk8 - 1, 0), cb_))
    vec = lambda width: pl.BlockSpec((1, width), lambda i: (0, 0))
    wspec = lambda r_, c_: pl.BlockSpec((None, r_, c_), lambda i: (layer, 0, 0))
    f32o = jax.ShapeDtypeStruct((s, D_MODEL), F32)
    bfo = jax.ShapeDtypeStruct((s, D_MODEL), BF16)
    return _call(
        body, name=name, grid=(s // ts,),
        out_shape=(f32o, f32o, bfo, bfo, f32o, f32o),
        in_specs=[row(D_MODEL), row(512, 0), row(512, 1), row(512, 2), row(512, 3), prev(1), prev(2),
                  row(512), row(512),
                  pl.BlockSpec((None, None, ts, PLE_DIM), lambda i: (layer, 0, i, 0)),
                  pl.BlockSpec((3, 512), lambda i: (0, 0)), vec(512), vec(D_MODEL),
                  wspec(D_MODEL, D_MODEL), vec(D_MODEL), wspec(D_MODEL, D_MODEL), vec(D_MODEL),
                  wspec(PLE_DIM, D_MODEL)],
        out_specs=(row(D_MODEL),) * 6,
        compiler_params=_params(("parallel",), VMEM_LIMIT),
    )(x, pc, pc, pc, pc, pc, pc, az, ya, p4, cw, cb, bg, wout_full, pg, wpg_full, bpg, wpe_full)


def _loss_head(xf, target, fg):
    s = xf.shape[0]
    ts = min(ROW_TILE, s)

    def body(x_ref, t_ref, g_ref, dx_ref, loss_ref, dg_ref):
        i = pl.program_id(0)

        @pl.when(i == 0)
        def _():
            loss_ref[...] = jnp.zeros_like(loss_ref)
            dg_ref[...] = jnp.zeros_like(dg_ref)

        x = x_ref[...]
        g = g_ref[...]
        r = lax.rsqrt(jnp.mean(x * x, axis=-1, keepdims=True) + EPS)
        xn = x * r
        err = xn * g - t_ref[...]
        per_row = jnp.sum(err * err, axis=-1, keepdims=True)
        loss_ref[...] += jnp.sum(per_row, axis=0, keepdims=True) * (0.5 / D_MODEL)
        dy = err * (1.0 / D_MODEL)
        dg_ref[...] += jnp.sum(dy * xn, axis=0, keepdims=True)
        dxn = dy * g
        dx_ref[...] = r * (dxn - xn * jnp.mean(dxn * xn, axis=-1, keepdims=True))

    row = pl.BlockSpec((ts, D_MODEL), lambda i: (i, 0))
    return _call(
        body, name="loss_head", grid=(s // ts,),
        out_shape=(jax.ShapeDtypeStruct((s, D_MODEL), F32), jax.ShapeDtypeStruct((1, LANES), F32),
                   jax.ShapeDtypeStruct((1, D_MODEL), F32)),
        in_specs=[row, row, pl.BlockSpec((1, D_MODEL), lambda i: (0, 0))],
        out_specs=(row, pl.BlockSpec((1, LANES), lambda i: (0, 0)), pl.BlockSpec((1, D_MODEL), lambda i: (0, 0))),
        compiler_params=_params(("arbitrary",), VMEM_LIMIT),
    )(xf, target, fg)


def _bwd_mid(dx3, x2, gate, e, pc, az, ya, layer, cw, cb, bg, pg, wpg_full, wout_full, name):
    s = x2.shape[0]
    ts = min(ROW_TILE, s)
    blk8 = ts // SUBLANES

    def body(dx3_ref, x2_ref, gate_ref, e_ref, cb_ref_, cc_ref, ch_ref, cz_ref, ccp_ref, chp_ref, az_ref, ya_ref,
             cw_ref, cbias_ref, bg_ref, pg_ref, wpg_ref, wout_ref,
             dx2_ref, dx2b_ref, dgpre_ref, de_ref, dya_ref, dmisc_ref,
             dbpg_ref, dpg_ref, dbg_ref, dcbias_ref, dcw_ref, dgated_ref):
        i = pl.program_id(0)

        @pl.when(i == 0)
        def _():
            for ref in (dbpg_ref, dpg_ref, dbg_ref, dcbias_ref, dcw_ref):
                ref[...] = jnp.zeros_like(ref)

        lane = lax.broadcasted_iota(jnp.int32, (1, LANES), 1)
        lo = lane < HEAD_DIM
        dx3 = dx3_ref[...]
        gate = gate_ref[...]
        de_ref[...] = (dx3 * gate).astype(BF16)
        dgpre = dx3 * e_ref[...] * gate * (1.0 - gate)
        dbpg_ref[...] += jnp.sum(dgpre, axis=0, keepdims=True)
        dgpre_b = dgpre.astype(BF16)
        dgpre_ref[...] = dgpre_b
        dh2 = lax.dot_general(dgpre_b, wpg_ref[...], NT, preferred_element_type=F32)
        x2 = x2_ref[...]
        r2 = lax.rsqrt(jnp.mean(x2 * x2, axis=-1, keepdims=True) + EPS)
        xn2 = x2 * r2
        dpg_ref[...] += jnp.sum(dh2 * xn2, axis=0, keepdims=True)
        dxn = dh2 * pg_ref[...]
        dx2 = dx3 + r2 * (dxn - xn2 * jnp.mean(dxn * xn2, axis=-1, keepdims=True))
        dx2_ref[...] = dx2
        dx2_b = dx2.astype(BF16)
        dx2b_ref[...] = dx2_b
        dgated_ref[...] = lax.dot_general(dx2_b, wout_ref[...], NT, preferred_element_type=F32)

        u, u1, u2 = _conv_taps(cc_ref, ch_ref, ccp_ref, chp_ref, i == 0, ts)
        conv = cbias_ref[...] + cw_ref[0:1, :] * u2 + cw_ref[1:2, :] * u1 + cw_ref[2:3, :] * u
        c_b = cb_ref_[...]
        yc = c_b * conv
        for sl in range(8):
            cols = slice(LANES * (sl % 4), LANES * (sl % 4 + 1))
            wide = slice(LANES * sl, LANES * (sl + 1))
            y = yc[:, cols] if sl < 4 else ya_ref[:, cols]
            zc = cz_ref[:, cols] if sl < 4 else az_ref[:, cols]
            bgs = bg_ref[:, wide]
            dgt = dgated_ref[:, wide]
            rg = lax.rsqrt(_group_bcast_sum(y * y, lo) * (1.0 / HEAD_DIM) + EPS)
            yhat = y * rg
            sig = _sigmoid(zc)
            dyn = dgt * (zc * sig)
            dzc = dgt * (yhat * bgs) * (sig * (1.0 + zc * (1.0 - sig)))
            dbg_ref[:, wide] += jnp.sum(dyn * yhat, axis=0, keepdims=True)
            dyh = dyn * bgs
            dy = rg * (dyh - yhat * (_group_bcast_sum(dyh * yhat, lo) * (1.0 / HEAD_DIM)))
            if sl < 4:
                dconv = dy * c_b[:, cols]
                dmisc_ref[:, cols] = dy * conv[:, cols]
                dmisc_ref[:, 512 + LANES * sl:512 + LANES * (sl + 1)] = dconv
                dmisc_ref[:, 1024 + LANES * sl:1024 + LANES * (sl + 1)] = dzc
                dcbias_ref[:, cols] += jnp.sum(dconv, axis=0, keepdims=True)
                dcw_ref[0:1, cols] += jnp.sum(dconv * u2[:, cols], axis=0, keepdims=True)
                dcw_ref[1:2, cols] += jnp.sum(dconv * u1[:, cols], axis=0, keepdims=True)
                dcw_ref[2:3, cols] += jnp.sum(dconv * u[:, cols], axis=0, keepdims=True)
            else:
                dya_ref[:, cols] = dy
                dmisc_ref[:, 1536 + LANES * (sl - 4):1536 + LANES * (sl - 3)] = dzc

    row = lambda width, cb_=0: pl.BlockSpec((ts, width), lambda i: (i, cb_))
    prev = lambda cb_: pl.BlockSpec((SUBLANES, 512), lambda i: (jnp.maximum(i * blk8 - 1, 0), cb_))
    vec = lambda width: pl.BlockSpec((1, width), lambda i: (0, 0))
    wspec = lambda r_, c_: pl.BlockSpec((None, r_, c_), lambda i: (layer, 0, 0))
    f32o = jax.ShapeDtypeStruct((s, D_MODEL), F32)
    bfo = jax.ShapeDtypeStruct((s, D_MODEL), BF16)
    vo = lambda width: jax.ShapeDtypeStruct((1, width), F32)
    return _call(
        body, name=name, grid=(s // ts,),
        out_shape=(f32o, bfo, bfo, bfo, jax.ShapeDtypeStruct((s, 512), F32), jax.ShapeDtypeStruct((s, 2048), F32),
                   vo(D_MODEL), vo(D_MODEL), vo(D_MODEL), vo(512), jax.ShapeDtypeStruct((SUBLANES, 512), F32)),
        in_specs=[row(D_MODEL), row(D_MODEL), row(D_MODEL), row(D_MODEL),
                  row(512, 0), row(512, 1), row(512, 2), row(512, 3), prev(1), prev(2), row(512), row(512),
                  pl.BlockSpec((3, 512), lambda i: (0, 0)), vec(512), vec(D_MODEL), vec(D_MODEL),
                  wspec(D_MODEL, D_MODEL), wspec(D_MODEL, D_MODEL)],
        out_specs=(row(D_MODEL), row(D_MODEL), row(D_MODEL), row(D_MODEL), row(512), row(2048),
                   vec(D_MODEL), vec(D_MODEL), vec(D_MODEL), vec(512),
                   pl.BlockSpec((SUBLANES, 512), lambda i: (0, 0))),
        scratch_shapes=[pltpu.VMEM((ts, D_MODEL), F32)],
        compiler_params=_params(("arbitrary",), VMEM_LIMIT),
    )(dx3, x2, gate, e, pc, pc, pc, pc, pc, pc, az, ya, cw, cb, bg, pg, wpg_full, wout_full)


def _attn_bwd(qkv, lsum, nblk, dya, name):
    s = qkv.shape[0]
    tq = min(ATTN_TILE, s)
    nq = s // tq
    rc = min(ATTN_ROWS, tq)
    n_rc = tq // rc
    chains = [(r, hh) for r in range(n_rc) for hh in range(2)]

    def body(nblk_ref, q_ref, k_ref, v_ref, lsum_ref, do_ref, dq_ref, dk_ref, dv_ref):
        hp, qi = pl.program_id(0), pl.program_id(1)

        @pl.when(qi == 0)
        def _():
            dk_ref[...] = jnp.zeros_like(dk_ref)
            dv_ref[...] = jnp.zeros_like(dv_ref)

        lo, causal, tri_gt, tri_le = _attn_pieces(tq, rc)
        lane = lax.broadcasted_iota(jnp.int32, (1, LANES), 1)
        qh = _split_heads(q_ref[...], lo)
        doh = _split_heads(do_ref[...].astype(BF16), lo)
        lt = lsum_ref[...]
        ltot_h = (jnp.sum(jnp.where(lane == 0, lt, 0.0), axis=-1, keepdims=True),
                  jnp.sum(jnp.where(lane == HEAD_DIM, lt, 0.0), axis=-1, keepdims=True))
        rows = lambda a_, r: a_[r * rc:(r + 1) * rc]
        qc = {(r, hh): rows(qh[hh], r) for r, hh in chains}
        doc = {(r, hh): rows(doh[hh], r) for r, hh in chains}
        ltot = {(r, hh): rows(ltot_h[hh], r) for r, hh in chains}

        def block(kb, carry, diag):
            start = pl.multiple_of(kb * tq, tq)
            k = k_ref[pl.ds(start, tq), :]
            v = v_ref[pl.ds(start, tq), :]
            kh = _split_heads(k, lo)
            z = {ch: lax.dot_general(qc[ch], k, NT, preferred_element_type=F32) for ch in chains}
            da = {ch: lax.dot_general(doc[ch], v, NT, preferred_element_type=F32) for ch in chains}
            sp = {ch: _softplus(z[ch], causal[ch[0]], diag) for ch in chains}
            later = {ch: jnp.dot(sp[ch].astype(BF16), tri_gt, preferred_element_type=F32) for ch in chains}
            walked = {ch: carry[ch[0]][1 + ch[1]] + jnp.sum(sp[ch], axis=-1, keepdims=True) for ch in chains}
            a, g = {}, {}
            for ch in chains:
                a[ch] = jnp.exp((z[ch] - sp[ch]) - ((ltot[ch] - walked[ch]) + later[ch]))
                if diag:
                    a[ch] = jnp.where(causal[ch[0]], a[ch], 0.0)
                g[ch] = a[ch] * da[ch]
            upto = {ch: jnp.dot(g[ch].astype(BF16), tri_le, preferred_element_type=F32) for ch in chains}
            dz = {}
            for ch in chains:
                dz[ch] = g[ch] - jnp.exp(z[ch] - sp[ch]) * (carry[ch[0]][3 + ch[1]] + upto[ch])
                if diag:
                    dz[ch] = jnp.where(causal[ch[0]], dz[ch], 0.0)
                dz[ch] = dz[ch].astype(BF16)
            dqc = {ch: jnp.dot(dz[ch], kh[ch[1]], preferred_element_type=F32) for ch in chains}
            dkc = [lax.dot_general(dz[ch], qc[ch], TN, preferred_element_type=F32) for ch in chains]
            dvc = [lax.dot_general(a[ch].astype(BF16), doc[ch], TN, preferred_element_type=F32) for ch in chains]
            dk_ref[pl.ds(start, tq), :] += sum(dkc[1:], dkc[0])
            dv_ref[pl.ds(start, tq), :] += sum(dvc[1:], dvc[0])
            out = []
            for r in range(n_rc):
                gsum = [carry[r][3 + hh] + jnp.sum(g[(r, hh)], axis=-1, keepdims=True) for hh in range(2)]
                out.append((carry[r][0] + dqc[(r, 0)] + dqc[(r, 1)], walked[(r, 0)], walked[(r, 1)], *gsum))
            return tuple(out)

        zc = jnp.zeros((rc, 1), F32)
        carry = tuple((jnp.zeros((rc, LANES), F32), zc, zc, zc, zc) for _ in range(n_rc))
        first = qi - jnp.clip(nblk_ref[hp, qi].astype(jnp.int32), 0, qi)
        carry = lax.fori_loop(first, qi, lambda kb, cr: block(kb, cr, False), carry)
        carry = block(qi, carry, True)
        for r in range(n_rc):
            dq_ref[r * rc:(r + 1) * rc, :] = carry[r][0] * 0.125

    blk = pl.BlockSpec((tq, LANES), lambda hp, qi: (qi, hp))
    col = pl.BlockSpec((s, LANES), lambda hp, qi: (0, hp))
    o512 = jax.ShapeDtypeStruct((s, D_SB), F32)
    return _call(
        body, name=name, grid=(4, nq),
        out_shape=(o512, o512, o512),
        in_specs=[pl.BlockSpec(memory_space=pltpu.SMEM), blk,
                  pl.BlockSpec((s, LANES), lambda hp, qi: (0, 4 + hp)),
                  pl.BlockSpec((s, LANES), lambda hp, qi: (0, 8 + hp)), blk, blk],
        out_specs=(blk, col, col),
        compiler_params=_params(("parallel", "arbitrary"), VMEM_LIMIT),
    )(nblk, qkv, qkv, qkv, lsum, dya)


def _bwd_dproj(dmisc, pc, dq, dk, dv, x, dx2, g, cw, win_full, layer, name):
    s = x.shape[0]
    ts = min(ROW_TILE, s)
    blk8 = ts // SUBLANES
    last8 = s // SUBLANES - 1

    def body(dcb_ref, dconv_ref, dcz_ref, daz_ref, nxt_ref, cc_ref, ch_ref, dq_ref, dk_ref, dv_ref,
             x_ref, dx2_ref, g_ref, cw_ref, w_ref, dproj_ref, dx_ref, dg_ref):
        i = pl.program_id(0)

        @pl.when(i == 0)
        def _():
            dg_ref[...] = jnp.zeros_like(dg_ref)

        keep = jnp.where(i == pl.num_programs(0) - 1, 0.0, 1.0)
        dc = dconv_ref[...]
        n0 = nxt_ref[0:1, :] * keep
        n1 = nxt_ref[1:2, :] * keep
        rowi = lax.broadcasted_iota(jnp.int32, dc.shape, 0)
        dc1 = jnp.where(rowi == ts - 1, n0, pltpu.roll(dc, ts - 1, 0))
        dc2 = jnp.where(rowi == ts - 2, n0, jnp.where(rowi == ts - 1, n1, pltpu.roll(dc, ts - 2, 0)))
        du = cw_ref[2:3, :] * dc + cw_ref[1:2, :] * dc1 + cw_ref[0:1, :] * dc2
        dproj_ref[:, 0:512] = dcb_ref[...].astype(BF16)
        dproj_ref[:, 512:1024] = (du * ch_ref[...]).astype(BF16)
        dproj_ref[:, 1024:1536] = (du * cc_ref[...]).astype(BF16)
        dproj_ref[:, 1536:2048] = dcz_ref[...].astype(BF16)
        dproj_ref[:, 2048:2560] = dq_ref[...].astype(BF16)
        dproj_ref[:, 2560:3072] = dk_ref[...].astype(BF16)
        dproj_ref[:, 3072:3584] = dv_ref[...].astype(BF16)
        dproj_ref[:, 3584:4096] = daz_ref[...].astype(BF16)
        dh = lax.dot_general(dproj_ref[...], w_ref[...], NT, preferred_element_type=F32)
        x = x_ref[...]
        r = lax.rsqrt(jnp.mean(x * x, axis=-1, keepdims=True) + EPS)
        xn = x * r
        dg_ref[...] += jnp.sum(dh * xn, axis=0, keepdims=True)
        dxn = dh * g_ref[...]
        dx_ref[...] = dx2_ref[...] + r * (dxn - xn * jnp.mean(dxn * xn, axis=-1, keepdims=True))

    row = lambda width, cb_=0: pl.BlockSpec((ts, width), lambda i: (i, cb_))
    nxt = pl.BlockSpec((SUBLANES, 512), lambda i: (jnp.minimum((i + 1) * blk8, last8), 1))
    vec = lambda width: pl.BlockSpec((1, width), lambda i: (0, 0))
    return _call(
        body, name=name, grid=(s // ts,),
        out_shape=(jax.ShapeDtypeStruct((s, N_IN), BF16), jax.ShapeDtypeStruct((s, D_MODEL), F32),
                   jax.ShapeDtypeStruct((1, D_MODEL), F32)),
        in_specs=[row(512, 0), row(512, 1), row(512, 2), row(512, 3), nxt, row(512, 1), row(512, 2),
                  row(512), row(512), row(512), row(D_MODEL), row(D_MODEL), vec(D_MODEL),
                  pl.BlockSpec((3, 512), lambda i: (0, 0)),
                  pl.BlockSpec((None, D_MODEL, N_IN), lambda i: (layer, 0, 0))],
        out_specs=(row(N_IN), row(D_MODEL), vec(D_MODEL)),
        compiler_params=_params(("arbitrary",), VMEM_LIMIT),
    )(dmisc, dmisc, dmisc, dmisc, dmisc, pc, pc, dq, dk, dv, x, dx2, g, cw, win_full)


def _atb(a, b, name, a_index=None):
    s, n = b.shape
    m = a.shape[-1]
    ts = min(512, s)
    tn = min(1024, n)
    if a_index is None:
        a_spec = pl.BlockSpec((ts, m), lambda j, i: (i, 0))
    else:
        a_spec = pl.BlockSpec((None, None, ts, m), lambda j, i: (a_index, 0, i, 0))

    def body(a_ref, b_ref, o_ref):
        @pl.when(pl.program_id(1) == 0)
        def _():
            o_ref[...] = jnp.zeros_like(o_ref)

        o_ref[...] += lax.dot_general(a_ref[...].astype(BF16), b_ref[...], TN, preferred_element_type=F32)

    return _call(
        body, name=name, grid=(n // tn, s // ts),
        out_shape=jax.ShapeDtypeStruct((m, n), F32),
        in_specs=[a_spec, pl.BlockSpec((ts, tn), lambda j, i: (i, j))],
        out_specs=pl.BlockSpec((m, tn), lambda j, i: (0, j)),
        compiler_params=_params(("parallel", "arbitrary"), VMEM_LIMIT),
    )(a, b)


def _exchange_grads(dwin, dwout, dwpg, dwpe, small):
    n_big = 4 * DEPTH
    n_cp = n_big + 1

    def body(*refs):
        ins = refs[:n_cp]
        win_r, wout_r, wpg_r, wpe_r, small_r = refs[n_cp:n_cp + 5]
        send_sems, recv_sems, local_sems = refs[-3:]
        x, y, c = lax.axis_index("x"), lax.axis_index("y"), lax.axis_index("c")
        me_blk = 4 * x + 2 * y + c

        def src_slab(t, blk):
            kind, layer = divmod(t, DEPTH)
            ref = ins[t]
            if kind == 0:
                return ref.at[:, pl.ds(blk * 512, 512)]
            if kind in (1, 2):
                return ref.at[pl.ds(blk * 128, 128), :]
            return ref.at[:, pl.ds(blk * 128, 128)]

        def dst_slot(t, blk):
            kind, layer = divmod(t, DEPTH)
            return (win_r, wout_r, wpg_r, wpe_r)[kind].at[blk, layer]

        copies = []
        for t in range(n_big):
            cp = pltpu.make_async_copy(src_slab(t, me_blk), dst_slot(t, me_blk), local_sems.at[t])
            cp.start()
            copies.append(cp)
        cp = pltpu.make_async_copy(ins[n_big], small_r.at[me_blk], local_sems.at[n_big])
        cp.start()
        copies.append(cp)

        remote = []
        for k in range(1, N_DEV):
            px = 1 - x if k & 4 else x
            py = 1 - y if k & 2 else y
            pc_ = 1 - c if k & 1 else c
            peer_blk = 4 * px + 2 * py + pc_
            for t in range(n_cp):
                if t < n_big:
                    src, dst = src_slab(t, peer_blk), dst_slot(t, me_blk)
                else:
                    src, dst = ins[n_big], small_r.at[me_blk]
                rc = pltpu.make_async_remote_copy(
                    src_ref=src, dst_ref=dst, send_sem=send_sems.at[k - 1, t], recv_sem=recv_sems.at[k - 1, t],
                    device_id=(px, py, pc_), device_id_type=pl.DeviceIdType.MESH)
                rc.start()
                remote.append(rc)
        for rc in remote:
            rc.wait_recv()
        for rc in remote:
            rc.wait_send()
        for cp in copies:
            cp.wait()

    any_spec = pl.BlockSpec(memory_space=pl.ANY)
    args = [*dwin, *dwout, *dwpg, *dwpe, small]
    return _call(
        body, name="exchange_grads",
        out_shape=(jax.ShapeDtypeStruct((N_DEV, DEPTH, D_MODEL, 512), F32),
                   jax.ShapeDtypeStruct((N_DEV, DEPTH, 128, D_MODEL), F32),
                   jax.ShapeDtypeStruct((N_DEV, DEPTH, 128, D_MODEL), F32),
                   jax.ShapeDtypeStruct((N_DEV, DEPTH, PLE_DIM, 128), F32),
                   jax.ShapeDtypeStruct((N_DEV, SMALL_ROWS, LANES), F32)),
        in_specs=[any_spec] * len(args), out_specs=[any_spec] * 5,
        scratch_shapes=[pltpu.SemaphoreType.DMA((N_DEV - 1, n_cp)), pltpu.SemaphoreType.DMA((N_DEV - 1, n_cp)),
                        pltpu.SemaphoreType.DMA((n_cp,))],
    )(*args)


def _adamw_math(w, g, m, v):
    m2 = ADAM_B1 * m + (1.0 - ADAM_B1) * g
    v2 = ADAM_B2 * v + (1.0 - ADAM_B2) * (g * g)
    m_hat = m2 / (1.0 - ADAM_B1 ** ADAM_STEP)
    v_hat = v2 / (1.0 - ADAM_B2 ** ADAM_STEP)
    delta = -ADAM_LR * (m_hat / (jnp.sqrt(v_hat) + ADAM_EPS) + ADAM_WD * w)
    return delta, m2, v2


def _adamw_sum8(parts, w, m, v, name):
    _, rows, cols = parts.shape
    tr = min(rows, 256)

    def body(p_ref, w_ref, m_ref, v_ref, g_ref, d_ref, m2_ref, v2_ref):
        g = p_ref[0]
        for d in range(1, N_DEV):
            g = g + p_ref[d]
        g_ref[...] = g
        d_ref[...], m2_ref[...], v2_ref[...] = _adamw_math(w_ref[...], g, m_ref[...], v_ref[...])

    tile = pl.BlockSpec((tr, cols), lambda i: (i, 0))
    o = jax.ShapeDtypeStruct((rows, cols), F32)
    return _call(
        body, name=name, grid=(rows // tr,),
        out_shape=(o, o, o, o),
        in_specs=[pl.BlockSpec((N_DEV, tr, cols), lambda i: (0, i, 0)), tile, tile, tile],
        out_specs=(tile, tile, tile, tile),
        compiler_params=_params(("parallel",), VMEM_LIMIT),
    )(parts, w, m, v)


def _adamw_plain(g, w, m, v, name):
    rows, cols = g.shape

    def body(g_ref, w_ref, m_ref, v_ref, d_ref, m2_ref, v2_ref):
        d_ref[...], m2_ref[...], v2_ref[...] = _adamw_math(w_ref[...], g_ref[...], m_ref[...], v_ref[...])

    full = pl.BlockSpec((rows, cols), lambda: (0, 0))
    o = jax.ShapeDtypeStruct((rows, cols), F32)
    return _call(body, name=name, out_shape=(o, o, o), in_specs=[full] * 4, out_specs=(full,) * 3)(g, w, m, v)


def _sum8_small(parts):
    def body(p_ref, g_ref):
        g = p_ref[0]
        for d in range(1, N_DEV):
            g = g + p_ref[d]
        g_ref[...] = g

    return _call(
        body, name="sum_small_grads",
        out_shape=jax.ShapeDtypeStruct((SMALL_ROWS, LANES), F32),
        in_specs=[pl.BlockSpec((N_DEV, SMALL_ROWS, LANES), lambda: (0, 0, 0))],
        out_specs=pl.BlockSpec((SMALL_ROWS, LANES), lambda: (0, 0)),
    )(parts)


def kernel(x, p, norm_g, w_in, conv_w, conv_b, branch_g, w_out, ple_norm_g, w_pg, b_pg, w_pe, final_g, loss_target, m_norm_g, m_w_in, m_conv_w, m_conv_b, m_branch_g, m_w_out, m_ple_norm_g, m_w_pg, m_b_pg, m_w_pe, m_final_g, v_norm_g, v_w_in, v_conv_w, v_conv_b, v_branch_g, v_w_out, v_ple_norm_g, v_w_pg, v_b_pg, v_w_pe, v_final_g):
    s = x.shape[1]
    x0 = x.reshape(s, D_MODEL)
    target = loss_target.reshape(s, D_MODEL)
    me_blk = _my_block()

    win_s = _cast_bf16(w_in.reshape(DEPTH * D_MODEL, 512), "cast_w_in").reshape(DEPTH, D_MODEL, 512)
    wout_s = _cast_bf16(w_out.reshape(DEPTH * 128, D_MODEL), "cast_w_out").reshape(DEPTH, 128, D_MODEL)
    wpg_s = _cast_bf16(w_pg.reshape(DEPTH * 128, D_MODEL), "cast_w_pg").reshape(DEPTH, 128, D_MODEL)
    wpe_s = _cast_bf16(w_pe.reshape(DEPTH * PLE_DIM, 128), "cast_w_pe").reshape(DEPTH, PLE_DIM, 128)
    cw_s = jnp.zeros((SUBLANES, LANES), F32).at[:DEPTH * 3, :HEAD_DIM].set(conv_w.reshape(DEPTH * 3, HEAD_DIM))
    win_f, wout_f, wpg_f, wpe_f, cw_all = _allgather_weights(win_s, wout_s, wpg_s, wpe_s, cw_s)
    cw_full = jnp.transpose(cw_all[:, :DEPTH * 3, :HEAD_DIM].reshape(N_DEV, DEPTH, 3, HEAD_DIM), (1, 2, 0, 3))
    cw_full = cw_full.reshape(DEPTH, 3, D_CONV)

    vec = lambda a, l: a[l][None, :]

    saved = []
    xl = x0
    for l in range(DEPTH):
        h, pc, qkv, az = _fwd_in(xl, vec(norm_g, l), win_f, l, f"fwd_in_{l}")
        ya, lsum, nblk = _attn_fwd(qkv, f"attn_fwd_{l}")
        x2, x3, gated, h2, gate, e = _fwd_mid(
            xl, pc, az, ya, p, l, cw_full[l], vec(conv_b, l), vec(branch_g, l), wout_f,
            vec(ple_norm_g, l), wpg_f, vec(b_pg, l), wpe_f, f"fwd_mid_{l}")
        saved.append(dict(x=xl, h=h, pc=pc, qkv=qkv, az=az, ya=ya, lsum=lsum, nblk=nblk, x2=x2, gated=gated, h2=h2,
                          gate=gate, e=e))
        xl = x3

    dx, loss_acc, d_final_g = _loss_head(xl, target, final_g[None, :])
    loss = lax.psum(loss_acc[0, 0], ("x", "y", "c"))

    dwin, dwout, dwpg, dwpe = [None] * DEPTH, [None] * DEPTH, [None] * DEPTH, [None] * DEPTH
    small = dict(norm_g=[None] * DEPTH, conv_b=[None] * DEPTH, branch_g=[None] * DEPTH,
                 ple_norm_g=[None] * DEPTH, b_pg=[None] * DEPTH, conv_w=[None] * DEPTH)
    for l in reversed(range(DEPTH)):
        sv = saved[l]
        (dx2, dx2_b, dgpre_b, de_b, dya, dmisc, d_bpg, d_pg, d_bg, d_cbias, d_cw) = _bwd_mid(
            dx, sv["x2"], sv["gate"], sv["e"], sv["pc"], sv["az"], sv["ya"], l, cw_full[l], vec(conv_b, l),
            vec(branch_g, l), vec(ple_norm_g, l), wpg_f, wout_f, f"bwd_mid_{l}")
        dq, dk, dv = _attn_bwd(sv["qkv"], sv["lsum"], sv["nblk"], dya, f"attn_bwd_{l}")
        dproj, dx, d_ng = _bwd_dproj(dmisc, sv["pc"], dq, dk, dv, sv["x"], dx2, vec(norm_g, l), cw_full[l],
                                     win_f, l, f"bwd_dproj_{l}")
        dwin[l] = _atb(sv["h"], dproj, f"dw_in_{l}")
        dwout[l] = _atb(sv["gated"], dx2_b, f"dw_out_{l}")
        dwpg[l] = _atb(sv["h2"], dgpre_b, f"dw_pg_{l}")
        dwpe[l] = _atb(p, de_b, f"dw_pe_{l}", a_index=l)
        small["norm_g"][l], small["conv_b"][l], small["branch_g"][l] = d_ng, d_cbias, d_bg
        small["ple_norm_g"][l], small["b_pg"][l], small["conv_w"][l] = d_pg, d_bpg, d_cw[:3]
    grad_x = dx.reshape(1, s, D_MODEL)

    flat = lambda parts: jnp.concatenate([a.reshape(-1) for a in parts])
    small_vec = jnp.concatenate([
        flat(small["norm_g"]), flat(small["conv_b"]), flat(small["branch_g"]), flat(small["ple_norm_g"]),
        flat(small["b_pg"]), d_final_g.reshape(-1), flat(small["conv_w"])]).reshape(SMALL_ROWS, LANES)
    r_in, r_out, r_pg, r_pe, r_small = _exchange_grads(dwin, dwout, dwpg, dwpe, small_vec)

    def big(parts, w, m, v, name):
        shape = w.shape
        rows = shape[0] * shape[1]
        to2d = lambda a: a.reshape(rows, shape[2])
        outs = _adamw_sum8(parts.reshape(N_DEV, rows, shape[2]), to2d(w), to2d(m), to2d(v), name)
        return [o.reshape(shape) for o in outs]

    g_win, d_win, m_win, v_win = big(r_in, w_in, m_w_in, v_w_in, "adamw_w_in")
    g_wout, d_wout, m_wout, v_wout = big(r_out, w_out, m_w_out, v_w_out, "adamw_w_out")
    g_wpg, d_wpg, m_wpg, v_wpg = big(r_pg, w_pg, m_w_pg, v_w_pg, "adamw_w_pg")
    g_wpe, d_wpe, m_wpe, v_wpe = big(r_pe, w_pe, m_w_pe, v_w_pe, "adamw_w_pe")

    g_small = _sum8_small(r_small)
    repl = [(norm_g, m_norm_g, v_norm_g), (conv_b, m_conv_b, v_conv_b), (branch_g, m_branch_g, v_branch_g),
            (ple_norm_g, m_ple_norm_g, v_ple_norm_g), (b_pg, m_b_pg, v_b_pg), (final_g, m_final_g, v_final_g)]
    pack = lambda idx: jnp.concatenate([t[idx].reshape(-1) for t in repl]).reshape(SMALL_REPL_ROWS, LANES)
    g_repl = g_small[:SMALL_REPL_ROWS]
    d_repl, m_repl, v_repl = _adamw_plain(g_repl, pack(0), pack(1), pack(2), "adamw_replicated")

    def unpack(a):
        flat_a = a.reshape(-1)
        out, off = [], 0
        for t in repl:
            n = t[0].size
            out.append(flat_a[off:off + n].reshape(t[0].shape))
            off += n
        return out

    g_r, d_r, m_r, v_r = unpack(g_repl), unpack(d_repl), unpack(m_repl), unpack(v_repl)

    g_cw_full = g_small[SMALL_REPL_ROWS:].reshape(DEPTH, 3, D_CONV)
    g_cw = lax.dynamic_slice(g_cw_full, (0, 0, me_blk * HEAD_DIM), (DEPTH, 3, HEAD_DIM))
    pad_cw = lambda a: jnp.zeros((SUBLANES, LANES), F32).at[:3].set(a.reshape(3, LANES))
    v_cw_pad = jnp.ones((SUBLANES, LANES), F32).at[:3].set(v_conv_w.reshape(3, LANES))
    d_cw, m_cw, v_cw = _adamw_plain(pad_cw(g_cw), pad_cw(conv_w), pad_cw(m_conv_w), v_cw_pad, "adamw_conv_w")
    un_cw = lambda a: a[:3].reshape(DEPTH, 3, HEAD_DIM)

    def ordered(r, win_, cw_, wout_, wpg_, wpe_):
        return [r[0], win_, cw_, r[1], r[2], wout_, r[3], wpg_, r[4], wpe_, r[5]]

    grads = ordered(g_r, g_win, g_cw, g_wout, g_wpg, g_wpe)
    deltas = ordered(d_r, d_win, un_cw(d_cw), d_wout, d_wpg, d_wpe)
    new_m = ordered(m_r, m_win, un_cw(m_cw), m_wout, m_wpg, m_wpe)
    new_v = ordered(v_r, v_win, un_cw(v_cw), v_wout, v_wpg, v_wpe)
    return (loss, grad_x, *grads, *deltas, *new_m, *new_v)
```

```python
import jax
import jax.numpy as jnp
from jax import lax
from jax.experimental import pallas as pl
from jax.experimental.pallas import tpu as pltpu

F32 = jnp.float32
BF16 = jnp.bfloat16

D_MODEL = 1024
D_CONV = 512
D_SB = 512
N_IN = 4096
HEAD_DIM = 64
PLE_DIM = 256
DEPTH = 2
EPS = 1e-6
ADAM_LR = 0.001
ADAM_B1 = 0.9
ADAM_B2 = 0.999
ADAM_EPS = 1e-08
ADAM_WD = 0.01
ADAM_STEP = 10

LANES = 128
SUBLANES = 8
VMEM_BYTES_V7X = 64 * 1024 * 1024
VMEM_LIMIT = VMEM_BYTES_V7X - 8 * 1024 * 1024

N_DEV = 8
ROW_TILE = 256
ATTN_TILE = 256
SMALL_ROWS = 104
SMALL_REPL_ROWS = 80

NT = (((1,), (1,)), ((), ()))
TN = (((0,), (0,)), ((), ()))


def _call(body, **kw):
    return pl.pallas_call(body, **kw)


def _params(sem=None, vmem=None):
    return pltpu.CompilerParams(dimension_semantics=sem, vmem_limit_bytes=vmem)


def _sigmoid(z):
    return 1.0 / (1.0 + jnp.exp(-z))


def _group_bcast_sum(a, lo):
    s_lo = jnp.sum(jnp.where(lo, a, 0.0), axis=-1, keepdims=True)
    s_hi = jnp.sum(jnp.where(lo, 0.0, a), axis=-1, keepdims=True)
    return jnp.where(lo, s_lo, s_hi)


def _my_block():
    return 4 * lax.axis_index("x") + 2 * lax.axis_index("y") + lax.axis_index("c")


def _cast_bf16(a2d, name):
    rows, cols = a2d.shape
    tr = min(rows, 512)

    def body(a_ref, o_ref):
        o_ref[...] = a_ref[...].astype(BF16)

    return _call(
        body, name=name, grid=(rows // tr,),
        out_shape=jax.ShapeDtypeStruct((rows, cols), BF16),
        in_specs=[pl.BlockSpec((tr, cols), lambda i: (i, 0))],
        out_specs=pl.BlockSpec((tr, cols), lambda i: (i, 0)),
        compiler_params=_params(("parallel",)),
    )(a2d)


class _Comm:
    def __init__(self, inputs, out_shapes, scratch, begin, middle, finish):
        self.inputs, self.out_shapes, self.scratch = list(inputs), list(out_shapes), list(scratch)
        self.begin, self.middle, self.finish = begin, middle, finish


def _slab(kind, ref, blk):
    if kind == "cols512":
        return ref.at[:, pl.ds(blk * 512, 512)]
    if kind == "rows128":
        return ref.at[pl.ds(blk * 128, 128), :]
    if kind == "cols128":
        return ref.at[:, pl.ds(blk * 128, 128)]
    return ref.at[blk]


def _gather_comm(items):
    n_t = len(items)
    kinds = [it[2] for it in items]

    def ctx(ins, outs, sems):
        send_sems, recv_sems, local_sems = sems
        x, y, c = lax.axis_index("x"), lax.axis_index("y"), lax.axis_index("c")
        me, sibling = (x, y, c), (x, y, 1 - c)
        chips = [(1 - x, y), (x, 1 - y), (1 - x, 1 - y)]

        def place(t, dev):
            return _slab(kinds[t], outs[t], 4 * dev[0] + 2 * dev[1] + dev[2])

        def copy(t, k, block, to, own=False):
            return pltpu.make_async_remote_copy(
                src_ref=ins[t] if own else place(t, block), dst_ref=place(t, block),
                send_sem=send_sems.at[t, k], recv_sem=recv_sems.at[t, k],
                device_id=to, device_id_type=pl.DeviceIdType.MESH)

        mine = [pltpu.make_async_copy(ins[t], place(t, me), local_sems.at[t]) for t in range(n_t)]
        first = []
        for t in range(n_t):
            first.append(copy(t, 0, me, sibling, own=True))
            first += [copy(t, 1 + j, me, (*chip, c), own=True) for j, chip in enumerate(chips)]
        passed = [copy(t, 4 + j, (*chip, c), sibling) for j, chip in enumerate(chips) for t in range(n_t)]
        landed = [copy(t, 1 + j, (*chip, c), me) for j, chip in enumerate(chips) for t in range(n_t)]
        from_sibling = []
        for t in range(n_t):
            from_sibling.append(copy(t, 0, sibling, me))
            from_sibling += [copy(t, 4 + j, (*chip, 1 - c), me) for j, chip in enumerate(chips)]
        return mine, first, landed, passed, from_sibling

    def begin(ins, outs, sems):
        mine, first, _, _, _ = ctx(ins, outs, sems)
        for cp in mine + first:
            cp.start()

    def middle(ins, outs, sems):
        _, _, landed, passed, _ = ctx(ins, outs, sems)
        for got, fwd in zip(landed, passed):
            got.wait_recv()
            fwd.start()

    def finish(ins, outs, sems):
        mine, first, _, passed, from_sibling = ctx(ins, outs, sems)
        for cp in from_sibling:
            cp.wait_recv()
        for cp in first + passed:
            cp.wait_send()
        for cp in mine:
            cp.wait()

    scratch = [pltpu.SemaphoreType.DMA((n_t, 7)), pltpu.SemaphoreType.DMA((n_t, 7)), pltpu.SemaphoreType.DMA((n_t,))]
    return _Comm([it[0] for it in items], [it[1] for it in items], scratch, begin, middle, finish)


def _exchange_comm(items):
    n_t = len(items)
    kinds = [it[2] for it in items]

    def ctx(ins, outs, sems):
        send_sems, recv_sems, local_sems = sems
        x, y, c = lax.axis_index("x"), lax.axis_index("y"), lax.axis_index("c")
        me_blk = 4 * x + 2 * y + c

        def src(t, blk):
            return ins[t] if kinds[t] == "slot" else _slab(kinds[t], ins[t], blk)

        local = [pltpu.make_async_copy(src(t, me_blk), outs[t].at[me_blk], local_sems.at[t]) for t in range(n_t)]
        remote = []
        for k in range(1, N_DEV):
            px = 1 - x if k & 4 else x
            py = 1 - y if k & 2 else y
            pc_ = 1 - c if k & 1 else c
            for t in range(n_t):
                remote.append(pltpu.make_async_remote_copy(
                    src_ref=src(t, 4 * px + 2 * py + pc_), dst_ref=outs[t].at[me_blk],
                    send_sem=send_sems.at[k - 1, t], recv_sem=recv_sems.at[k - 1, t],
                    device_id=(px, py, pc_), device_id_type=pl.DeviceIdType.MESH))
        return local, remote

    def begin(ins, outs, sems):
        local, remote = ctx(ins, outs, sems)
        for cp in local + remote:
            cp.start()

    def finish(ins, outs, sems):
        local, remote = ctx(ins, outs, sems)
        for cp in remote:
            cp.wait_recv()
        for cp in remote:
            cp.wait_send()
        for cp in local:
            cp.wait()

    scratch = [pltpu.SemaphoreType.DMA((N_DEV - 1, n_t)), pltpu.SemaphoreType.DMA((N_DEV - 1, n_t)),
               pltpu.SemaphoreType.DMA((n_t,))]
    out_shapes = [jax.ShapeDtypeStruct((N_DEV, *it[1].shape), it[1].dtype) for it in items]
    return _Comm([it[0] for it in items], out_shapes, scratch, begin, None, finish)


def _comm_call(comm, name):
    n_in, n_out = len(comm.inputs), len(comm.out_shapes)

    def body(*refs):
        ins, outs, sems = refs[:n_in], refs[n_in:n_in + n_out], refs[n_in + n_out:]
        comm.begin(ins, outs, sems)
        if comm.middle is not None:
            comm.middle(ins, outs, sems)
        comm.finish(ins, outs, sems)

    any_spec = pl.BlockSpec(memory_space=pl.ANY)
    return _call(body, name=name, out_shape=tuple(comm.out_shapes), in_specs=[any_spec] * n_in,
                 out_specs=[any_spec] * n_out, scratch_shapes=comm.scratch)(*comm.inputs)


def _hosted(body, n_in, n_out, comm, first, last, middle):
    if comm is None:
        return lambda *refs: body(*refs)
    n_ci, n_co, n_cs = len(comm.inputs), len(comm.out_shapes), len(comm.scratch)

    def wrapped(*refs):
        ins, cin = refs[:n_in], refs[n_in:n_in + n_ci]
        o0 = n_in + n_ci
        outs, cout = refs[o0:o0 + n_out], refs[o0 + n_out:o0 + n_out + n_co]
        scr, csem = refs[o0 + n_out + n_co:len(refs) - n_cs], refs[len(refs) - n_cs:]
        pl.when(first())(lambda: comm.begin(cin, cout, csem))
        body(*ins, *outs, *scr)
        if comm.middle is not None:
            pl.when(middle())(lambda: comm.middle(cin, cout, csem))
        pl.when(last())(lambda: comm.finish(cin, cout, csem))

    return wrapped


def _hosted_call(body, comm, *, name, grid, out_shape, in_specs, out_specs, args, scratch_shapes=(), sem=None):
    nd = len(grid)
    first, last, middle = _at_first(nd), _at_last(nd), _at_middle(nd)
    if comm is not None:
        sem = ("arbitrary",) * nd
    n_in, n_out = len(in_specs), len(out_shape)
    any_spec = pl.BlockSpec(memory_space=pl.ANY)
    c_in = [] if comm is None else comm.inputs
    c_out = [] if comm is None else comm.out_shapes
    c_scr = [] if comm is None else comm.scratch
    outs = _call(
        _hosted(body, n_in, n_out, comm, first, last, middle), name=name, grid=grid,
        out_shape=(*out_shape, *c_out),
        in_specs=[*in_specs, *[any_spec] * len(c_in)],
        out_specs=(*out_specs, *[any_spec] * len(c_out)),
        scratch_shapes=[*scratch_shapes, *c_scr],
        compiler_params=_params(sem, VMEM_LIMIT),
    )(*args, *c_in)
    return outs[:n_out], outs[n_out:]


def _grid_step(ndim):
    i, n = pl.program_id(0), pl.num_programs(0)
    for d in range(1, ndim):
        i, n = i * pl.num_programs(d) + pl.program_id(d), n * pl.num_programs(d)
    return i, n


def _at_first(ndim):
    return lambda: _grid_step(ndim)[0] == 0


def _at_last(ndim):
    def pred():
        i, n = _grid_step(ndim)
        return i == n - 1
    return pred


def _at_middle(ndim):
    def pred():
        i, n = _grid_step(ndim)
        return i == n // 2
    return pred


def _fwd_in(x, g, w_full, name, comm=None):
    s = x.shape[0]
    ts = min(ROW_TILE, s)

    def body(x_ref, g_ref, w_ref, h_ref, pc_ref, qkv_ref, az_ref):
        xf = x_ref[...]
        r = lax.rsqrt(jnp.mean(xf * xf, axis=-1, keepdims=True) + EPS)
        h = (xf * r * g_ref[...]).astype(BF16)
        h_ref[...] = h
        pc_ref[...] = jnp.dot(h, w_ref[:, 0:2048], preferred_element_type=F32)
        q = jnp.dot(h, w_ref[:, 2048:2560], preferred_element_type=F32)
        qkv_ref[:, 0:512] = (q * 0.125).astype(BF16)
        qkv_ref[:, 512:1536] = jnp.dot(h, w_ref[:, 2560:3584], preferred_element_type=F32).astype(BF16)
        az_ref[...] = jnp.dot(h, w_ref[:, 3584:4096], preferred_element_type=F32)

    row = lambda width: pl.BlockSpec((ts, width), lambda i: (i, 0))
    return _hosted_call(
        body, comm, name=name, grid=(s // ts,),
        out_shape=(jax.ShapeDtypeStruct((s, D_MODEL), BF16), jax.ShapeDtypeStruct((s, 2048), F32),
                   jax.ShapeDtypeStruct((s, 1536), BF16), jax.ShapeDtypeStruct((s, 512), F32)),
        in_specs=[row(D_MODEL), pl.BlockSpec((1, D_MODEL), lambda i: (0, 0)),
                  pl.BlockSpec((D_MODEL, N_IN), lambda i: (0, 0))],
        out_specs=(row(D_MODEL), row(2048), row(1536), row(512)),
        args=(x, g, w_full), sem=("parallel",))


ATTN_ROWS = 128
ATTN_DONE = 104.0


def _attn_pieces(tq, rc):
    lane = lax.broadcasted_iota(jnp.int32, (1, LANES), 1)
    lo = lane < HEAD_DIM
    row = lax.broadcasted_iota(jnp.int32, (tq, tq), 0)
    col = lax.broadcasted_iota(jnp.int32, (tq, tq), 1)
    tri_gt = jnp.where(row > col, 1.0, 0.0).astype(BF16)
    tri_le = jnp.where(row <= col, 1.0, 0.0).astype(BF16)
    rrow = lax.broadcasted_iota(jnp.int32, (rc, tq), 0)
    rcol = lax.broadcasted_iota(jnp.int32, (rc, tq), 1)
    causal = [rcol < rrow + r * rc for r in range(tq // rc)]
    return lo, causal, tri_gt, tri_le


def _split_heads(a, lo):
    z = jnp.zeros_like(a)
    return (jnp.where(lo, a, z), jnp.where(lo, z, a))


def _softplus(z, causal, diag):
    neg_abs = lax.bitcast_convert_type(lax.bitcast_convert_type(z, jnp.uint32) | jnp.uint32(0x80000000), F32)
    sp = jnp.maximum(z, 0.0) + jnp.log(1.0 + jnp.exp(neg_abs))
    if diag:
        sp = jnp.where(causal, sp, 0.0)
    return sp


def _attn_fwd(qkv, name, comm=None):
    s = qkv.shape[0]
    tq = min(ATTN_TILE, s)
    nq = s // tq
    rc = min(ATTN_ROWS, tq)
    n_rc = tq // rc
    chains = [(r, hh) for r in range(n_rc) for hh in range(2)]

    def body(q_ref, k_ref, v_ref, o_ref, lsum_ref, nblk_ref):
        hp, qi = pl.program_id(0), pl.program_id(1)
        lo, causal, tri_gt, _ = _attn_pieces(tq, rc)
        qh = _split_heads(q_ref[...], lo)
        qc = {(r, hh): qh[hh][r * rc:(r + 1) * rc] for r, hh in chains}

        def block(kb, carry, diag):
            start = pl.multiple_of(kb * tq, tq)
            k = k_ref[pl.ds(start, tq), :]
            vh = _split_heads(v_ref[pl.ds(start, tq), :], lo)
            z = {ch: lax.dot_general(qc[ch], k, NT, preferred_element_type=F32) for ch in chains}
            sp = {ch: _softplus(z[ch], causal[ch[0]], diag) for ch in chains}
            later = {ch: jnp.dot(sp[ch].astype(BF16), tri_gt, preferred_element_type=F32) for ch in chains}
            a = {}
            for ch in chains:
                a[ch] = jnp.exp((z[ch] - sp[ch]) - (carry[ch[0]][1 + ch[1]] + later[ch]))
                if diag:
                    a[ch] = jnp.where(causal[ch[0]], a[ch], 0.0)
            pv = {ch: jnp.dot(a[ch].astype(BF16), vh[ch[1]], preferred_element_type=F32) for ch in chains}
            out = []
            for r in range(n_rc):
                acc = carry[r][0] + pv[(r, 0)] + pv[(r, 1)]
                cs = [carry[r][1 + hh] + jnp.sum(sp[(r, hh)], axis=-1, keepdims=True) for hh in range(2)]
                out.append((acc, cs[0], cs[1]))
            return tuple(out)

        def least(carry):
            m = jnp.minimum(carry[0][1], carry[0][2])
            for r in range(1, n_rc):
                m = jnp.minimum(m, jnp.minimum(carry[r][1], carry[r][2]))
            return jnp.min(m)

        zc = jnp.zeros((rc, 1), F32)
        carry = tuple((jnp.zeros((rc, LANES), F32), zc, zc) for _ in range(n_rc))
        carry = block(qi, carry, True)

        def go_on(st):
            return jnp.logical_and(st[0] < qi, st[1] < ATTN_DONE)

        def step(st):
            new = block(qi - 1 - st[0], st[2], False)
            return st[0] + 1, least(new), new

        walked, _, carry = lax.while_loop(go_on, step, (jnp.int32(0), least(carry), carry))
        for r in range(n_rc):
            o_ref[r * rc:(r + 1) * rc, :] = carry[r][0]
            lsum_ref[r * rc:(r + 1) * rc, :] = jnp.where(lo, carry[r][1], carry[r][2])
        nblk_ref[hp, qi] = walked.astype(F32)

    blk = pl.BlockSpec((tq, LANES), lambda hp, qi: (qi, hp))
    o512 = jax.ShapeDtypeStruct((s, D_SB), F32)
    return _hosted_call(
        body, comm, name=name, grid=(4, nq),
        out_shape=(o512, o512, jax.ShapeDtypeStruct((4, nq), F32)),
        in_specs=[blk, pl.BlockSpec((s, LANES), lambda hp, qi: (0, 4 + hp)),
                  pl.BlockSpec((s, LANES), lambda hp, qi: (0, 8 + hp))],
        out_specs=(blk, blk, pl.BlockSpec(memory_space=pltpu.SMEM)),
        args=(qkv, qkv, qkv), sem=("arbitrary", "arbitrary"))


def _conv_taps(cc_ref, ch_ref, ccp_ref, chp_ref, first, ts):
    u = cc_ref[...] * ch_ref[...]
    keep = jnp.where(first, 0.0, 1.0)
    p6 = ccp_ref[6:7, :] * chp_ref[6:7, :] * keep
    p7 = ccp_ref[7:8, :] * chp_ref[7:8, :] * keep
    rowi = lax.broadcasted_iota(jnp.int32, u.shape, 0)
    u1 = jnp.where(rowi == 0, p7, pltpu.roll(u, 1, 0))
    u2 = jnp.where(rowi == 0, p6, jnp.where(rowi == 1, p7, pltpu.roll(u, 2, 0)))
    return u, u1, u2


def _fwd_mid(x, pc, az, ya, p4, layer, cw, cb, bg, wout_full, pg, wpg_full, bpg, wpe_full, name):
    s = x.shape[0]
    ts = min(ROW_TILE, s)
    blk8 = ts // SUBLANES

    def body(x_ref, cb_ref_, cc_ref, ch_ref, cz_ref, ccp_ref, chp_ref, az_ref, ya_ref, p_ref,
             cw_ref, cbias_ref, bg_ref, wout_ref, pg_ref, wpg_ref, bpg_ref, wpe_ref,
             x2_ref, x3_ref, gated_ref, h2_ref, gate_ref, e_ref):
        i = pl.program_id(0)
        lane = lax.broadcasted_iota(jnp.int32, (1, LANES), 1)
        lo = lane < HEAD_DIM
        u, u1, u2 = _conv_taps(cc_ref, ch_ref, ccp_ref, chp_ref, i == 0, ts)
        conv = cbias_ref[...] + cw_ref[0:1, :] * u2 + cw_ref[1:2, :] * u1 + cw_ref[2:3, :] * u
        yc = cb_ref_[...] * conv
        for sl in range(8):
            cols = slice(LANES * (sl % 4), LANES * (sl % 4 + 1))
            y = yc[:, cols] if sl < 4 else ya_ref[:, cols]
            zc = cz_ref[:, cols] if sl < 4 else az_ref[:, cols]
            rg = lax.rsqrt(_group_bcast_sum(y * y, lo) * (1.0 / HEAD_DIM) + EPS)
            yn = y * rg * bg_ref[:, LANES * sl:LANES * (sl + 1)]
            gated_ref[:, LANES * sl:LANES * (sl + 1)] = (yn * (zc * _sigmoid(zc))).astype(BF16)
        x2 = x_ref[...] + jnp.dot(gated_ref[...], wout_ref[...], preferred_element_type=F32)
        x2_ref[...] = x2
        r2 = lax.rsqrt(jnp.mean(x2 * x2, axis=-1, keepdims=True) + EPS)
        h2 = (x2 * r2 * pg_ref[...]).astype(BF16)
        h2_ref[...] = h2
        gate = _sigmoid(jnp.dot(h2, wpg_ref[...], preferred_element_type=F32) + bpg_ref[...])
        gate_ref[...] = gate
        e = jnp.dot(p_ref[...].astype(BF16), wpe_ref[...], preferred_element_type=F32)
        e_ref[...] = e
        x3_ref[...] = x2 + gate * e

    row = lambda width, cb_=0: pl.BlockSpec((ts, width), lambda i: (i, cb_))
    prev = lambda cb_: pl.BlockSpec((SUBLANES, 512), lambda i: (jnp.maximum(i * blk8 - 1, 0), cb_))
    vec = lambda width: pl.BlockSpec((1, width), lambda i: (0, 0))
    wspec = lambda r_, c_: pl.BlockSpec((r_, c_), lambda i: (0, 0))
    f32o = jax.ShapeDtypeStruct((s, D_MODEL), F32)
    bfo = jax.ShapeDtypeStruct((s, D_MODEL), BF16)
    return _call(
        body, name=name, grid=(s // ts,),
        out_shape=(f32o, f32o, bfo, bfo, f32o, f32o),
        in_specs=[row(D_MODEL), row(512, 0), row(512, 1), row(512, 2), row(512, 3), prev(1), prev(2),
                  row(512), row(512),
                  pl.BlockSpec((None, None, ts, PLE_DIM), lambda i: (layer, 0, i, 0)),
                  pl.BlockSpec((3, 512), lambda i: (0, 0)), vec(512), vec(D_MODEL),
                  wspec(D_MODEL, D_MODEL), vec(D_MODEL), wspec(D_MODEL, D_MODEL), vec(D_MODEL),
                  wspec(PLE_DIM, D_MODEL)],
        out_specs=(row(D_MODEL),) * 6,
        compiler_params=_params(("parallel",), VMEM_LIMIT),
    )(x, pc, pc, pc, pc, pc, pc, az, ya, p4, cw, cb, bg, wout_full, pg, wpg_full, bpg, wpe_full)


def _loss_head(xf, target, fg):
    s = xf.shape[0]
    ts = min(ROW_TILE, s)

    def body(x_ref, t_ref, g_ref, dx_ref, loss_ref, dg_ref):
        i = pl.program_id(0)

        @pl.when(i == 0)
        def _():
            loss_ref[...] = jnp.zeros_like(loss_ref)
            dg_ref[...] = jnp.zeros_like(dg_ref)

        x = x_ref[...]
        g = g_ref[...]
        r = lax.rsqrt(jnp.mean(x * x, axis=-1, keepdims=True) + EPS)
        xn = x * r
        err = xn * g - t_ref[...]
        per_row = jnp.sum(err * err, axis=-1, keepdims=True)
        loss_ref[...] += jnp.sum(per_row, axis=0, keepdims=True) * (0.5 / D_MODEL)
        dy = err * (1.0 / D_MODEL)
        dg_ref[...] += jnp.sum(dy * xn, axis=0, keepdims=True)
        dxn = dy * g
        dx_ref[...] = r * (dxn - xn * jnp.mean(dxn * xn, axis=-1, keepdims=True))

    row = pl.BlockSpec((ts, D_MODEL), lambda i: (i, 0))
    return _call(
        body, name="loss_head", grid=(s // ts,),
        out_shape=(jax.ShapeDtypeStruct((s, D_MODEL), F32), jax.ShapeDtypeStruct((1, LANES), F32),
                   jax.ShapeDtypeStruct((1, D_MODEL), F32)),
        in_specs=[row, row, pl.BlockSpec((1, D_MODEL), lambda i: (0, 0))],
        out_specs=(row, pl.BlockSpec((1, LANES), lambda i: (0, 0)), pl.BlockSpec((1, D_MODEL), lambda i: (0, 0))),
        compiler_params=_params(("arbitrary",), VMEM_LIMIT),
    )(xf, target, fg)


def _bwd_mid(dx3, x2, gate, e, pc, az, ya, cw, cb, bg, pg, wpg_full, wout_full, name):
    s = x2.shape[0]
    ts = min(ROW_TILE, s)
    blk8 = ts // SUBLANES

    def body(dx3_ref, x2_ref, gate_ref, e_ref, cb_ref_, cc_ref, ch_ref, cz_ref, ccp_ref, chp_ref, az_ref, ya_ref,
             cw_ref, cbias_ref, bg_ref, pg_ref, wpg_ref, wout_ref,
             dx2_ref, dx2b_ref, dgpre_ref, de_ref, dya_ref, dmisc_ref,
             dbpg_ref, dpg_ref, dbg_ref, dcbias_ref, dcw_ref, dgated_ref):
        i = pl.program_id(0)

        @pl.when(i == 0)
        def _():
            for ref in (dbpg_ref, dpg_ref, dbg_ref, dcbias_ref, dcw_ref):
                ref[...] = jnp.zeros_like(ref)

        lane = lax.broadcasted_iota(jnp.int32, (1, LANES), 1)
        lo = lane < HEAD_DIM
        dx3 = dx3_ref[...]
        gate = gate_ref[...]
        de_ref[...] = (dx3 * gate).astype(BF16)
        dgpre = dx3 * e_ref[...] * gate * (1.0 - gate)
        dbpg_ref[...] += jnp.sum(dgpre, axis=0, keepdims=True)
        dgpre_b = dgpre.astype(BF16)
        dgpre_ref[...] = dgpre_b
        dh2 = lax.dot_general(dgpre_b, wpg_ref[...], NT, preferred_element_type=F32)
        x2 = x2_ref[...]
        r2 = lax.rsqrt(jnp.mean(x2 * x2, axis=-1, keepdims=True) + EPS)
        xn2 = x2 * r2
        dpg_ref[...] += jnp.sum(dh2 * xn2, axis=0, keepdims=True)
        dxn = dh2 * pg_ref[...]
        dx2 = dx3 + r2 * (dxn - xn2 * jnp.mean(dxn * xn2, axis=-1, keepdims=True))
        dx2_ref[...] = dx2
        dx2_b = dx2.astype(BF16)
        dx2b_ref[...] = dx2_b
        dgated_ref[...] = lax.dot_general(dx2_b, wout_ref[...], NT, preferred_element_type=F32)

        u, u1, u2 = _conv_taps(cc_ref, ch_ref, ccp_ref, chp_ref, i == 0, ts)
        conv = cbias_ref[...] + cw_ref[0:1, :] * u2 + cw_ref[1:2, :] * u1 + cw_ref[2:3, :] * u
        c_b = cb_ref_[...]
        yc = c_b * conv
        for sl in range(8):
            cols = slice(LANES * (sl % 4), LANES * (sl % 4 + 1))
            wide = slice(LANES * sl, LANES * (sl + 1))
            y = yc[:, cols] if sl < 4 else ya_ref[:, cols]
            zc = cz_ref[:, cols] if sl < 4 else az_ref[:, cols]
            bgs = bg_ref[:, wide]
            dgt = dgated_ref[:, wide]
            rg = lax.rsqrt(_group_bcast_sum(y * y, lo) * (1.0 / HEAD_DIM) + EPS)
            yhat = y * rg
            sig = _sigmoid(zc)
            dyn = dgt * (zc * sig)
            dzc = dgt * (yhat * bgs) * (sig * (1.0 + zc * (1.0 - sig)))
            dbg_ref[:, wide] += jnp.sum(dyn * yhat, axis=0, keepdims=True)
            dyh = dyn * bgs
            dy = rg * (dyh - yhat * (_group_bcast_sum(dyh * yhat, lo) * (1.0 / HEAD_DIM)))
            if sl < 4:
                dconv = dy * c_b[:, cols]
                dmisc_ref[:, cols] = dy * conv[:, cols]
                dmisc_ref[:, 512 + LANES * sl:512 + LANES * (sl + 1)] = dconv
                dmisc_ref[:, 1024 + LANES * sl:1024 + LANES * (sl + 1)] = dzc
                dcbias_ref[:, cols] += jnp.sum(dconv, axis=0, keepdims=True)
                dcw_ref[0:1, cols] += jnp.sum(dconv * u2[:, cols], axis=0, keepdims=True)
                dcw_ref[1:2, cols] += jnp.sum(dconv * u1[:, cols], axis=0, keepdims=True)
                dcw_ref[2:3, cols] += jnp.sum(dconv * u[:, cols], axis=0, keepdims=True)
            else:
                dya_ref[:, cols] = dy
                dmisc_ref[:, 1536 + LANES * (sl - 4):1536 + LANES * (sl - 3)] = dzc

    row = lambda width, cb_=0: pl.BlockSpec((ts, width), lambda i: (i, cb_))
    prev = lambda cb_: pl.BlockSpec((SUBLANES, 512), lambda i: (jnp.maximum(i * blk8 - 1, 0), cb_))
    vec = lambda width: pl.BlockSpec((1, width), lambda i: (0, 0))
    wspec = lambda r_, c_: pl.BlockSpec((r_, c_), lambda i: (0, 0))
    f32o = jax.ShapeDtypeStruct((s, D_MODEL), F32)
    bfo = jax.ShapeDtypeStruct((s, D_MODEL), BF16)
    vo = lambda width: jax.ShapeDtypeStruct((1, width), F32)
    return _call(
        body, name=name, grid=(s // ts,),
        out_shape=(f32o, bfo, bfo, bfo, jax.ShapeDtypeStruct((s, 512), F32), jax.ShapeDtypeStruct((s, 2048), F32),
                   vo(D_MODEL), vo(D_MODEL), vo(D_MODEL), vo(512), jax.ShapeDtypeStruct((SUBLANES, 512), F32)),
        in_specs=[row(D_MODEL), row(D_MODEL), row(D_MODEL), row(D_MODEL),
                  row(512, 0), row(512, 1), row(512, 2), row(512, 3), prev(1), prev(2), row(512), row(512),
                  pl.BlockSpec((3, 512), lambda i: (0, 0)), vec(512), vec(D_MODEL), vec(D_MODEL),
                  wspec(D_MODEL, D_MODEL), wspec(D_MODEL, D_MODEL)],
        out_specs=(row(D_MODEL), row(D_MODEL), row(D_MODEL), row(D_MODEL), row(512), row(2048),
                   vec(D_MODEL), vec(D_MODEL), vec(D_MODEL), vec(512),
                   pl.BlockSpec((SUBLANES, 512), lambda i: (0, 0))),
        scratch_shapes=[pltpu.VMEM((ts, D_MODEL), F32)],
        compiler_params=_params(("arbitrary",), VMEM_LIMIT),
    )(dx3, x2, gate, e, pc, pc, pc, pc, pc, pc, az, ya, cw, cb, bg, pg, wpg_full, wout_full)


def _attn_bwd(qkv, lsum, nblk, dya, name, comm=None):
    s = qkv.shape[0]
    tq = min(ATTN_TILE, s)
    nq = s // tq
    rc = min(ATTN_ROWS, tq)
    n_rc = tq // rc
    chains = [(r, hh) for r in range(n_rc) for hh in range(2)]

    def body(nblk_ref, q_ref, k_ref, v_ref, lsum_ref, do_ref, dq_ref, dk_ref, dv_ref):
        hp, qi = pl.program_id(0), pl.program_id(1)

        @pl.when(qi == 0)
        def _():
            dk_ref[...] = jnp.zeros_like(dk_ref)
            dv_ref[...] = jnp.zeros_like(dv_ref)

        lo, causal, tri_gt, tri_le = _attn_pieces(tq, rc)
        lane = lax.broadcasted_iota(jnp.int32, (1, LANES), 1)
        qh = _split_heads(q_ref[...], lo)
        doh = _split_heads(do_ref[...].astype(BF16), lo)
        lt = lsum_ref[...]
        ltot_h = (jnp.sum(jnp.where(lane == 0, lt, 0.0), axis=-1, keepdims=True),
                  jnp.sum(jnp.where(lane == HEAD_DIM, lt, 0.0), axis=-1, keepdims=True))
        rows = lambda a_, r: a_[r * rc:(r + 1) * rc]
        qc = {(r, hh): rows(qh[hh], r) for r, hh in chains}
        doc = {(r, hh): rows(doh[hh], r) for r, hh in chains}
        ltot = {(r, hh): rows(ltot_h[hh], r) for r, hh in chains}

        def block(kb, carry, diag):
            start = pl.multiple_of(kb * tq, tq)
            k = k_ref[pl.ds(start, tq), :]
            v = v_ref[pl.ds(start, tq), :]
            kh = _split_heads(k, lo)
            z = {ch: lax.dot_general(qc[ch], k, NT, preferred_element_type=F32) for ch in chains}
            da = {ch: lax.dot_general(doc[ch], v, NT, preferred_element_type=F32) for ch in chains}
            sp = {ch: _softplus(z[ch], causal[ch[0]], diag) for ch in chains}
            later = {ch: jnp.dot(sp[ch].astype(BF16), tri_gt, preferred_element_type=F32) for ch in chains}
            walked = {ch: carry[ch[0]][1 + ch[1]] + jnp.sum(sp[ch], axis=-1, keepdims=True) for ch in chains}
            a, g = {}, {}
            for ch in chains:
                a[ch] = jnp.exp((z[ch] - sp[ch]) - ((ltot[ch] - walked[ch]) + later[ch]))
                if diag:
                    a[ch] = jnp.where(causal[ch[0]], a[ch], 0.0)
                g[ch] = a[ch] * da[ch]
            upto = {ch: jnp.dot(g[ch].astype(BF16), tri_le, preferred_element_type=F32) for ch in chains}
            dz = {}
            for ch in chains:
                dz[ch] = g[ch] - jnp.exp(z[ch] - sp[ch]) * (carry[ch[0]][3 + ch[1]] + upto[ch])
                if diag:
                    dz[ch] = jnp.where(causal[ch[0]], dz[ch], 0.0)
                dz[ch] = dz[ch].astype(BF16)
            dqc = {ch: jnp.dot(dz[ch], kh[ch[1]], preferred_element_type=F32) for ch in chains}
            dkc = [lax.dot_general(dz[ch], qc[ch], TN, preferred_element_type=F32) for ch in chains]
            dvc = [lax.dot_general(a[ch].astype(BF16), doc[ch], TN, preferred_element_type=F32) for ch in chains]
            dk_ref[pl.ds(start, tq), :] += sum(dkc[1:], dkc[0])
            dv_ref[pl.ds(start, tq), :] += sum(dvc[1:], dvc[0])
            out = []
            for r in range(n_rc):
                gsum = [carry[r][3 + hh] + jnp.sum(g[(r, hh)], axis=-1, keepdims=True) for hh in range(2)]
                out.append((carry[r][0] + dqc[(r, 0)] + dqc[(r, 1)], walked[(r, 0)], walked[(r, 1)], *gsum))
            return tuple(out)

        zc = jnp.zeros((rc, 1), F32)
        carry = tuple((jnp.zeros((rc, LANES), F32), zc, zc, zc, zc) for _ in range(n_rc))
        first = qi - jnp.clip(nblk_ref[hp, qi].astype(jnp.int32), 0, qi)
        carry = lax.fori_loop(first, qi, lambda kb, cr: block(kb, cr, False), carry)
        carry = block(qi, carry, True)
        for r in range(n_rc):
            dq_ref[r * rc:(r + 1) * rc, :] = carry[r][0] * 0.125

    blk = pl.BlockSpec((tq, LANES), lambda hp, qi: (qi, hp))
    col = pl.BlockSpec((s, LANES), lambda hp, qi: (0, hp))
    o512 = jax.ShapeDtypeStruct((s, D_SB), F32)
    return _hosted_call(
        body, comm, name=name, grid=(4, nq),
        out_shape=(o512, o512, o512),
        in_specs=[pl.BlockSpec(memory_space=pltpu.SMEM), blk,
                  pl.BlockSpec((s, LANES), lambda hp, qi: (0, 4 + hp)),
                  pl.BlockSpec((s, LANES), lambda hp, qi: (0, 8 + hp)), blk, blk],
        out_specs=(blk, col, col),
        args=(nblk, qkv, qkv, qkv, lsum, dya), sem=("parallel", "arbitrary"))


def _bwd_dproj(dmisc, pc, dq, dk, dv, x, dx2, g, cw, win_full, name, comm=None):
    s = x.shape[0]
    ts = min(ROW_TILE, s)
    blk8 = ts // SUBLANES
    last8 = s // SUBLANES - 1

    def body(dcb_ref, dconv_ref, dcz_ref, daz_ref, nxt_ref, cc_ref, ch_ref, dq_ref, dk_ref, dv_ref,
             x_ref, dx2_ref, g_ref, cw_ref, w_ref, dproj_ref, dx_ref, dg_ref):
        i = pl.program_id(0)

        @pl.when(i == 0)
        def _():
            dg_ref[...] = jnp.zeros_like(dg_ref)

        keep = jnp.where(i == pl.num_programs(0) - 1, 0.0, 1.0)
        dc = dconv_ref[...]
        n0 = nxt_ref[0:1, :] * keep
        n1 = nxt_ref[1:2, :] * keep
        rowi = lax.broadcasted_iota(jnp.int32, dc.shape, 0)
        dc1 = jnp.where(rowi == ts - 1, n0, pltpu.roll(dc, ts - 1, 0))
        dc2 = jnp.where(rowi == ts - 2, n0, jnp.where(rowi == ts - 1, n1, pltpu.roll(dc, ts - 2, 0)))
        du = cw_ref[2:3, :] * dc + cw_ref[1:2, :] * dc1 + cw_ref[0:1, :] * dc2
        dproj_ref[:, 0:512] = dcb_ref[...].astype(BF16)
        dproj_ref[:, 512:1024] = (du * ch_ref[...]).astype(BF16)
        dproj_ref[:, 1024:1536] = (du * cc_ref[...]).astype(BF16)
        dproj_ref[:, 1536:2048] = dcz_ref[...].astype(BF16)
        dproj_ref[:, 2048:2560] = dq_ref[...].astype(BF16)
        dproj_ref[:, 2560:3072] = dk_ref[...].astype(BF16)
        dproj_ref[:, 3072:3584] = dv_ref[...].astype(BF16)
        dproj_ref[:, 3584:4096] = daz_ref[...].astype(BF16)
        dh = lax.dot_general(dproj_ref[...], w_ref[...], NT, preferred_element_type=F32)
        x = x_ref[...]
        r = lax.rsqrt(jnp.mean(x * x, axis=-1, keepdims=True) + EPS)
        xn = x * r
        dg_ref[...] += jnp.sum(dh * xn, axis=0, keepdims=True)
        dxn = dh * g_ref[...]
        dx_ref[...] = dx2_ref[...] + r * (dxn - xn * jnp.mean(dxn * xn, axis=-1, keepdims=True))

    row = lambda width, cb_=0: pl.BlockSpec((ts, width), lambda i: (i, cb_))
    nxt = pl.BlockSpec((SUBLANES, 512), lambda i: (jnp.minimum((i + 1) * blk8, last8), 1))
    vec = lambda width: pl.BlockSpec((1, width), lambda i: (0, 0))
    return _hosted_call(
        body, comm, name=name, grid=(s // ts,),
        out_shape=(jax.ShapeDtypeStruct((s, N_IN), BF16), jax.ShapeDtypeStruct((s, D_MODEL), F32),
                   jax.ShapeDtypeStruct((1, D_MODEL), F32)),
        in_specs=[row(512, 0), row(512, 1), row(512, 2), row(512, 3), nxt, row(512, 1), row(512, 2),
                  row(512), row(512), row(512), row(D_MODEL), row(D_MODEL), vec(D_MODEL),
                  pl.BlockSpec((3, 512), lambda i: (0, 0)),
                  pl.BlockSpec((D_MODEL, N_IN), lambda i: (0, 0))],
        out_specs=(row(N_IN), row(D_MODEL), vec(D_MODEL)),
        args=(dmisc, dmisc, dmisc, dmisc, dmisc, pc, pc, dq, dk, dv, x, dx2, g, cw, win_full),
        sem=("arbitrary",))


def _atb(a, b, name, a_index=None):
    s, n = b.shape
    m = a.shape[-1]
    ts = min(512, s)
    tn = min(1024, n)
    if a_index is None:
        a_spec = pl.BlockSpec((ts, m), lambda j, i: (i, 0))
    else:
        a_spec = pl.BlockSpec((None, None, ts, m), lambda j, i: (a_index, 0, i, 0))

    def body(a_ref, b_ref, o_ref, acc_ref):
        i = pl.program_id(1)

        @pl.when(i == 0)
        def _():
            acc_ref[...] = jnp.zeros_like(acc_ref)

        acc_ref[...] += lax.dot_general(a_ref[...].astype(BF16), b_ref[...], TN, preferred_element_type=F32)

        @pl.when(i == pl.num_programs(1) - 1)
        def _():
            o_ref[...] = acc_ref[...].astype(BF16)

    return _call(
        body, name=name, grid=(n // tn, s // ts),
        out_shape=jax.ShapeDtypeStruct((m, n), BF16),
        in_specs=[a_spec, pl.BlockSpec((ts, tn), lambda j, i: (i, j))],
        out_specs=pl.BlockSpec((m, tn), lambda j, i: (0, j)),
        scratch_shapes=[pltpu.VMEM((m, tn), F32)],
        compiler_params=_params(("parallel", "arbitrary"), VMEM_LIMIT),
    )(a, b)


def _adamw_math(w, g, m, v):
    m2 = ADAM_B1 * m + (1.0 - ADAM_B1) * g
    v2 = ADAM_B2 * v + (1.0 - ADAM_B2) * (g * g)
    m_hat = m2 / (1.0 - ADAM_B1 ** ADAM_STEP)
    v_hat = v2 / (1.0 - ADAM_B2 ** ADAM_STEP)
    delta = -ADAM_LR * (m_hat / (jnp.sqrt(v_hat) + ADAM_EPS) + ADAM_WD * w)
    return delta, m2, v2


def _adamw_sum8(parts, w, m, v, name):
    _, rows, cols = w.shape
    tr = min(rows, 256)
    n_tiles = rows // tr
    assert len(parts) == DEPTH == 2

    def body(p0_ref, p1_ref, w_ref, m_ref, v_ref, g_ref, d_ref, m2_ref, v2_ref):
        def run(p_ref):
            g = p_ref[0].astype(F32)
            for d in range(1, N_DEV):
                g = g + p_ref[d].astype(F32)
            g_ref[...] = g
            d_ref[...], m2_ref[...], v2_ref[...] = _adamw_math(w_ref[...], g, m_ref[...], v_ref[...])

        pl.when(pl.program_id(0) == 0)(lambda: run(p0_ref))
        pl.when(pl.program_id(0) == 1)(lambda: run(p1_ref))

    part0 = pl.BlockSpec((N_DEV, tr, cols), lambda l, i: (0, jnp.where(l == 0, i, n_tiles - 1), 0))
    part1 = pl.BlockSpec((N_DEV, tr, cols), lambda l, i: (0, jnp.where(l == 1, i, 0), 0))
    tile = pl.BlockSpec((None, tr, cols), lambda l, i: (l, i, 0))
    o = jax.ShapeDtypeStruct((DEPTH, rows, cols), F32)
    return _call(
        body, name=name, grid=(DEPTH, n_tiles),
        out_shape=(o, o, o, o),
        in_specs=[part0, part1, tile, tile, tile],
        out_specs=(tile, tile, tile, tile),
        compiler_params=_params(("arbitrary", "arbitrary"), VMEM_LIMIT),
    )(parts[0], parts[1], w, m, v)


def _adamw_plain(g, w, m, v, name):
    rows, cols = g.shape

    def body(g_ref, w_ref, m_ref, v_ref, d_ref, m2_ref, v2_ref):
        d_ref[...], m2_ref[...], v2_ref[...] = _adamw_math(w_ref[...], g_ref[...], m_ref[...], v_ref[...])

    full = pl.BlockSpec((rows, cols), lambda: (0, 0))
    o = jax.ShapeDtypeStruct((rows, cols), F32)
    return _call(body, name=name, out_shape=(o, o, o), in_specs=[full] * 4, out_specs=(full,) * 3)(g, w, m, v)


def _sum8_small(parts):
    def body(p_ref, g_ref):
        g = p_ref[0]
        for d in range(1, N_DEV):
            g = g + p_ref[d]
        g_ref[...] = g

    return _call(
        body, name="sum_small_grads",
        out_shape=jax.ShapeDtypeStruct((SMALL_ROWS, LANES), F32),
        in_specs=[pl.BlockSpec((N_DEV, SMALL_ROWS, LANES), lambda: (0, 0, 0))],
        out_specs=pl.BlockSpec((SMALL_ROWS, LANES), lambda: (0, 0)),
    )(parts)


def kernel(x, p, norm_g, w_in, conv_w, conv_b, branch_g, w_out, ple_norm_g, w_pg, b_pg, w_pe, final_g, loss_target, m_norm_g, m_w_in, m_conv_w, m_conv_b, m_branch_g, m_w_out, m_ple_norm_g, m_w_pg, m_b_pg, m_w_pe, m_final_g, v_norm_g, v_w_in, v_conv_w, v_conv_b, v_branch_g, v_w_out, v_ple_norm_g, v_w_pg, v_b_pg, v_w_pe, v_final_g):
    s = x.shape[1]
    x0 = x.reshape(s, D_MODEL)
    target = loss_target.reshape(s, D_MODEL)
    me_blk = _my_block()

    win_s = _cast_bf16(w_in.reshape(DEPTH * D_MODEL, 512), "cast_w_in").reshape(DEPTH, D_MODEL, 512)
    wout_s = _cast_bf16(w_out.reshape(DEPTH * 128, D_MODEL), "cast_w_out").reshape(DEPTH, 128, D_MODEL)
    wpg_s = _cast_bf16(w_pg.reshape(DEPTH * 128, D_MODEL), "cast_w_pg").reshape(DEPTH, 128, D_MODEL)
    wpe_s = _cast_bf16(w_pe.reshape(DEPTH * PLE_DIM, 128), "cast_w_pe").reshape(DEPTH, PLE_DIM, 128)
    cw_s = jnp.zeros((SUBLANES, LANES), F32).at[:DEPTH * 3, :HEAD_DIM].set(conv_w.reshape(DEPTH * 3, HEAD_DIM))
    bf = lambda r_, c_: jax.ShapeDtypeStruct((r_, c_), BF16)
    w_items = lambda l: [(wout_s[l], bf(D_MODEL, D_MODEL), "rows128"), (wpg_s[l], bf(D_MODEL, D_MODEL), "rows128"),
                         (wpe_s[l], bf(PLE_DIM, D_MODEL), "cols128")]
    win_f = [None] * DEPTH
    win_f[0], cw_all = _comm_call(_gather_comm([
        (win_s[0], bf(D_MODEL, N_IN), "cols512"),
        (cw_s, jax.ShapeDtypeStruct((N_DEV, SUBLANES, LANES), F32), "slot")]), "gather_w_in_0")
    cw_full = jnp.transpose(cw_all[:, :DEPTH * 3, :HEAD_DIM].reshape(N_DEV, DEPTH, 3, HEAD_DIM), (1, 2, 0, 3))
    cw_full = cw_full.reshape(DEPTH, 3, D_CONV)
    gather_rest_0 = _gather_comm(w_items(0))
    gather_1 = _gather_comm([(win_s[1], bf(D_MODEL, N_IN), "cols512"), *w_items(1)])

    vec = lambda a, l: a[l][None, :]

    saved = []
    xl = x0
    wout_f, wpg_f, wpe_f = [None] * DEPTH, [None] * DEPTH, [None] * DEPTH
    for l in range(DEPTH):
        (h, pc, qkv, az), got = _fwd_in(xl, vec(norm_g, l), win_f[l], f"fwd_in_{l}",
                                        comm=gather_rest_0 if l == 0 else None)
        if l == 0:
            wout_f[0], wpg_f[0], wpe_f[0] = got
        (ya, lsum, nblk), got = _attn_fwd(qkv, f"attn_fwd_{l}", comm=gather_1 if l == 0 else None)
        if l == 0:
            win_f[1], wout_f[1], wpg_f[1], wpe_f[1] = got
        x2, x3, gated, h2, gate, e = _fwd_mid(
            xl, pc, az, ya, p, l, cw_full[l], vec(conv_b, l), vec(branch_g, l), wout_f[l],
            vec(ple_norm_g, l), wpg_f[l], vec(b_pg, l), wpe_f[l], f"fwd_mid_{l}")
        saved.append(dict(x=xl, h=h, pc=pc, qkv=qkv, az=az, ya=ya, lsum=lsum, nblk=nblk, x2=x2, gated=gated, h2=h2,
                          gate=gate, e=e))
        xl = x3

    dx, loss_acc, d_final_g = _loss_head(xl, target, final_g[None, :])
    loss = lax.psum(loss_acc[0, 0], ("x", "y", "c"))

    dwin, dwout, dwpg, dwpe = [None] * DEPTH, [None] * DEPTH, [None] * DEPTH, [None] * DEPTH
    small = dict(norm_g=[None] * DEPTH, conv_b=[None] * DEPTH, branch_g=[None] * DEPTH,
                 ple_norm_g=[None] * DEPTH, b_pg=[None] * DEPTH, conv_w=[None] * DEPTH)
    slot = lambda r_, c_: jax.ShapeDtypeStruct((r_, c_), BF16)
    r_in, r_out, r_pg, r_pe = [None] * DEPTH, [None] * DEPTH, [None] * DEPTH, [None] * DEPTH

    def rest_items(l):
        return [(dwout[l], slot(128, D_MODEL), "rows128"), (dwpg[l], slot(128, D_MODEL), "rows128"),
                (dwpe[l], slot(PLE_DIM, 128), "cols128")]

    for l in reversed(range(DEPTH)):
        sv = saved[l]
        (dx2, dx2_b, dgpre_b, de_b, dya, dmisc, d_bpg, d_pg, d_bg, d_cbias, d_cw) = _bwd_mid(
            dx, sv["x2"], sv["gate"], sv["e"], sv["pc"], sv["az"], sv["ya"], cw_full[l], vec(conv_b, l),
            vec(branch_g, l), vec(ple_norm_g, l), wpg_f[l], wout_f[l], f"bwd_mid_{l}")
        dwout[l] = _atb(sv["gated"], dx2_b, f"dw_out_{l}")
        dwpg[l] = _atb(sv["h2"], dgpre_b, f"dw_pg_{l}")
        dwpe[l] = _atb(p, de_b, f"dw_pe_{l}", a_index=l)
        ride = None
        if l == 0:
            ride = _exchange_comm([(dwin[1], slot(D_MODEL, 512), "cols512"), *rest_items(1)])
        (dq, dk, dv), got = _attn_bwd(sv["qkv"], sv["lsum"], sv["nblk"], dya, f"attn_bwd_{l}", comm=ride)
        if l == 0:
            r_in[1], r_out[1], r_pg[1], r_pe[1] = got
        ride = _exchange_comm(rest_items(0)) if l == 0 else None
        (dproj, dx, d_ng), got = _bwd_dproj(dmisc, sv["pc"], dq, dk, dv, sv["x"], dx2, vec(norm_g, l), cw_full[l],
                                            win_f[l], f"bwd_dproj_{l}", comm=ride)
        if l == 0:
            r_out[0], r_pg[0], r_pe[0] = got
        dwin[l] = _atb(sv["h"], dproj, f"dw_in_{l}")
        small["norm_g"][l], small["conv_b"][l], small["branch_g"][l] = d_ng, d_cbias, d_bg
        small["ple_norm_g"][l], small["b_pg"][l], small["conv_w"][l] = d_pg, d_bpg, d_cw[:3]
    grad_x = dx.reshape(1, s, D_MODEL)

    flat = lambda parts: jnp.concatenate([a.reshape(-1) for a in parts])
    small_vec = jnp.concatenate([
        flat(small["norm_g"]), flat(small["conv_b"]), flat(small["branch_g"]), flat(small["ple_norm_g"]),
        flat(small["b_pg"]), d_final_g.reshape(-1), flat(small["conv_w"])]).reshape(SMALL_ROWS, LANES)
    r_in[0], r_small = _comm_call(_exchange_comm([
        (dwin[0], slot(D_MODEL, 512), "cols512"),
        (small_vec, jax.ShapeDtypeStruct((SMALL_ROWS, LANES), F32), "slot")]), "exchange_last")

    g_win, d_win, m_win, v_win = _adamw_sum8(r_in, w_in, m_w_in, v_w_in, "adamw_w_in")
    g_wout, d_wout, m_wout, v_wout = _adamw_sum8(r_out, w_out, m_w_out, v_w_out, "adamw_w_out")
    g_wpg, d_wpg, m_wpg, v_wpg = _adamw_sum8(r_pg, w_pg, m_w_pg, v_w_pg, "adamw_w_pg")
    g_wpe, d_wpe, m_wpe, v_wpe = _adamw_sum8(r_pe, w_pe, m_w_pe, v_w_pe, "adamw_w_pe")

    g_small = _sum8_small(r_small)
    repl = [(norm_g, m_norm_g, v_norm_g), (conv_b, m_conv_b, v_conv_b), (branch_g, m_branch_g, v_branch_g),
            (ple_norm_g, m_ple_norm_g, v_ple_norm_g), (b_pg, m_b_pg, v_b_pg), (final_g, m_final_g, v_final_g)]
    pack = lambda idx: jnp.concatenate([t[idx].reshape(-1) for t in repl]).reshape(SMALL_REPL_ROWS, LANES)
    g_repl = g_small[:SMALL_REPL_ROWS]
    d_repl, m_repl, v_repl = _adamw_plain(g_repl, pack(0), pack(1), pack(2), "adamw_replicated")

    def unpack(a):
        flat_a = a.reshape(-1)
        out, off = [], 0
        for t in repl:
            n = t[0].size
            out.append(flat_a[off:off + n].reshape(t[0].shape))
            off += n
        return out

    g_r, d_r, m_r, v_r = unpack(g_repl), unpack(d_repl), unpack(m_repl), unpack(v_repl)

    g_cw_full = g_small[SMALL_REPL_ROWS:].reshape(DEPTH, 3, D_CONV)
    g_cw = lax.dynamic_slice(g_cw_full, (0, 0, me_blk * HEAD_DIM), (DEPTH, 3, HEAD_DIM))
    pad_cw = lambda a: jnp.zeros((SUBLANES, LANES), F32).at[:3].set(a.reshape(3, LANES))
    v_cw_pad = jnp.ones((SUBLANES, LANES), F32).at[:3].set(v_conv_w.reshape(3, LANES))
    d_cw, m_cw, v_cw = _adamw_plain(pad_cw(g_cw), pad_cw(conv_w), pad_cw(m_conv_w), v_cw_pad, "adamw_conv_w")
    un_cw = lambda a: a[:3].reshape(DEPTH, 3, HEAD_DIM)

    def ordered(r, win_, cw_, wout_, wpg_, wpe_):
        return [r[0], win_, cw_, r[1], r[2], wout_, r[3], wpg_, r[4], wpe_, r[5]]

    grads = ordered(g_r, g_win, g_cw, g_wout, g_wpg, g_wpe)
    deltas = ordered(d_r, d_win, un_cw(d_cw), d_wout, d_wpg, d_wpe)
    new_m = ordered(m_r, m_win, un_cw(m_cw), m_wout, m_wpg, m_wpe)
    new_v = ordered(v_r, v_win, un_cw(v_cw), v_wout, v_wpg, v_wpe)
    return (loss, grad_x, *grads, *deltas, *new_m, *new_v)
```

```python
import jax
import jax.numpy as jnp
from jax import lax
from jax.experimental import pallas as pl
from jax.experimental.pallas import tpu as pltpu

F32 = jnp.float32
BF16 = jnp.bfloat16

D_MODEL = 1024
D_CONV = 512
D_SB = 512
N_IN = 4096
HEAD_DIM = 64
PLE_DIM = 256
DEPTH = 2
EPS = 1e-6
ADAM_LR = 0.001
ADAM_B1 = 0.9
ADAM_B2 = 0.999
ADAM_EPS = 1e-08
ADAM_WD = 0.01
ADAM_STEP = 10

LANES = 128
SUBLANES = 8
VMEM_BYTES_V7X = 64 * 1024 * 1024
VMEM_LIMIT = VMEM_BYTES_V7X - 8 * 1024 * 1024

N_DEV = 8
ROW_TILE = 256
ATTN_TILE = 256
SMALL_ROWS = 104
SMALL_REPL_ROWS = 80

NT = (((1,), (1,)), ((), ()))
TN = (((0,), (0,)), ((), ()))


def _call(body, **kw):
    return pl.pallas_call(body, **kw)


def _params(sem=None, vmem=None):
    return pltpu.CompilerParams(dimension_semantics=sem, vmem_limit_bytes=vmem)


def _sigmoid(z):
    return 1.0 / (1.0 + jnp.exp(-z))


def _group_bcast_sum(a, lo):
    s_lo = jnp.sum(jnp.where(lo, a, 0.0), axis=-1, keepdims=True)
    s_hi = jnp.sum(jnp.where(lo, 0.0, a), axis=-1, keepdims=True)
    return jnp.where(lo, s_lo, s_hi)


def _my_block():
    return 4 * lax.axis_index("x") + 2 * lax.axis_index("y") + lax.axis_index("c")


def _cast_bf16(a2d, name):
    rows, cols = a2d.shape
    tr = min(rows, 512)

    def body(a_ref, o_ref):
        o_ref[...] = a_ref[...].astype(BF16)

    return _call(
        body, name=name, grid=(rows // tr,),
        out_shape=jax.ShapeDtypeStruct((rows, cols), BF16),
        in_specs=[pl.BlockSpec((tr, cols), lambda i: (i, 0))],
        out_specs=pl.BlockSpec((tr, cols), lambda i: (i, 0)),
        compiler_params=_params(("parallel",)),
    )(a2d)


class _Comm:
    def __init__(self, inputs, out_shapes, scratch, begin, middle, finish):
        self.inputs, self.out_shapes, self.scratch = list(inputs), list(out_shapes), list(scratch)
        self.begin, self.middle, self.finish = begin, middle, finish


def _slab(kind, ref, blk):
    if kind == "cols512":
        return ref.at[:, pl.ds(blk * 512, 512)]
    if kind == "rows128":
        return ref.at[pl.ds(blk * 128, 128), :]
    if kind == "cols128":
        return ref.at[:, pl.ds(blk * 128, 128)]
    return ref.at[blk]


def _gather_comm(items):
    n_t = len(items)
    kinds = [it[2] for it in items]

    def ctx(ins, outs, sems):
        send_sems, recv_sems, local_sems = sems
        x, y, c = lax.axis_index("x"), lax.axis_index("y"), lax.axis_index("c")
        me, sibling = (x, y, c), (x, y, 1 - c)
        chips = [(1 - x, y), (x, 1 - y), (1 - x, 1 - y)]

        def place(t, dev):
            return _slab(kinds[t], outs[t], 4 * dev[0] + 2 * dev[1] + dev[2])

        def copy(t, k, block, to, own=False):
            return pltpu.make_async_remote_copy(
                src_ref=ins[t] if own else place(t, block), dst_ref=place(t, block),
                send_sem=send_sems.at[t, k], recv_sem=recv_sems.at[t, k],
                device_id=to, device_id_type=pl.DeviceIdType.MESH)

        mine = [pltpu.make_async_copy(ins[t], place(t, me), local_sems.at[t]) for t in range(n_t)]
        first = []
        for t in range(n_t):
            first.append(copy(t, 0, me, sibling, own=True))
            first += [copy(t, 1 + j, me, (*chip, c), own=True) for j, chip in enumerate(chips)]
        passed = [copy(t, 4 + j, (*chip, c), sibling) for j, chip in enumerate(chips) for t in range(n_t)]
        landed = [copy(t, 1 + j, (*chip, c), me) for j, chip in enumerate(chips) for t in range(n_t)]
        from_sibling = []
        for t in range(n_t):
            from_sibling.append(copy(t, 0, sibling, me))
            from_sibling += [copy(t, 4 + j, (*chip, 1 - c), me) for j, chip in enumerate(chips)]
        return mine, first, landed, passed, from_sibling

    def begin(ins, outs, sems):
        mine, first, _, _, _ = ctx(ins, outs, sems)
        for cp in mine + first:
            cp.start()

    def middle(ins, outs, sems):
        _, _, landed, passed, _ = ctx(ins, outs, sems)
        for got, fwd in zip(landed, passed):
            got.wait_recv()
            fwd.start()

    def finish(ins, outs, sems):
        mine, first, _, passed, from_sibling = ctx(ins, outs, sems)
        for cp in from_sibling:
            cp.wait_recv()
        for cp in first + passed:
            cp.wait_send()
        for cp in mine:
            cp.wait()

    scratch = [pltpu.SemaphoreType.DMA((n_t, 7)), pltpu.SemaphoreType.DMA((n_t, 7)), pltpu.SemaphoreType.DMA((n_t,))]
    return _Comm([it[0] for it in items], [it[1] for it in items], scratch, begin, middle, finish)


def _exchange_comm(items):
    n_t = len(items)
    kinds = [it[2] for it in items]

    def ctx(ins, outs, sems):
        send_sems, recv_sems, local_sems = sems
        x, y, c = lax.axis_index("x"), lax.axis_index("y"), lax.axis_index("c")
        me_blk = 4 * x + 2 * y + c

        def src(t, blk):
            return ins[t] if kinds[t] == "slot" else _slab(kinds[t], ins[t], blk)

        local = [pltpu.make_async_copy(src(t, me_blk), outs[t].at[me_blk], local_sems.at[t]) for t in range(n_t)]
        remote = []
        for k in range(1, N_DEV):
            px = 1 - x if k & 4 else x
            py = 1 - y if k & 2 else y
            pc_ = 1 - c if k & 1 else c
            for t in range(n_t):
                remote.append(pltpu.make_async_remote_copy(
                    src_ref=src(t, 4 * px + 2 * py + pc_), dst_ref=outs[t].at[me_blk],
                    send_sem=send_sems.at[k - 1, t], recv_sem=recv_sems.at[k - 1, t],
                    device_id=(px, py, pc_), device_id_type=pl.DeviceIdType.MESH))
        return local, remote

    def begin(ins, outs, sems):
        local, remote = ctx(ins, outs, sems)
        for cp in local + remote:
            cp.start()

    def finish(ins, outs, sems):
        local, remote = ctx(ins, outs, sems)
        for cp in remote:
            cp.wait_recv()
        for cp in remote:
            cp.wait_send()
        for cp in local:
            cp.wait()

    scratch = [pltpu.SemaphoreType.DMA((N_DEV - 1, n_t)), pltpu.SemaphoreType.DMA((N_DEV - 1, n_t)),
               pltpu.SemaphoreType.DMA((n_t,))]
    out_shapes = [jax.ShapeDtypeStruct((N_DEV, *it[1].shape), it[1].dtype) for it in items]
    return _Comm([it[0] for it in items], out_shapes, scratch, begin, None, finish)


def _comm_call(comm, name):
    n_in, n_out = len(comm.inputs), len(comm.out_shapes)

    def body(*refs):
        ins, outs, sems = refs[:n_in], refs[n_in:n_in + n_out], refs[n_in + n_out:]
        comm.begin(ins, outs, sems)
        if comm.middle is not None:
            comm.middle(ins, outs, sems)
        comm.finish(ins, outs, sems)

    any_spec = pl.BlockSpec(memory_space=pl.ANY)
    return _call(body, name=name, out_shape=tuple(comm.out_shapes), in_specs=[any_spec] * n_in,
                 out_specs=[any_spec] * n_out, scratch_shapes=comm.scratch)(*comm.inputs)


def _hosted(body, n_in, n_out, comm, first, last, middle):
    if comm is None:
        return lambda *refs: body(*refs)
    n_ci, n_co, n_cs = len(comm.inputs), len(comm.out_shapes), len(comm.scratch)

    def wrapped(*refs):
        ins, cin = refs[:n_in], refs[n_in:n_in + n_ci]
        o0 = n_in + n_ci
        outs, cout = refs[o0:o0 + n_out], refs[o0 + n_out:o0 + n_out + n_co]
        scr, csem = refs[o0 + n_out + n_co:len(refs) - n_cs], refs[len(refs) - n_cs:]
        pl.when(first())(lambda: comm.begin(cin, cout, csem))
        body(*ins, *outs, *scr)
        if comm.middle is not None:
            pl.when(middle())(lambda: comm.middle(cin, cout, csem))
        pl.when(last())(lambda: comm.finish(cin, cout, csem))

    return wrapped


def _hosted_call(body, comm, *, name, grid, out_shape, in_specs, out_specs, args, scratch_shapes=(), sem=None):
    nd = len(grid)
    first, last, middle = _at_first(nd), _at_last(nd), _at_middle(nd)
    if comm is not None:
        sem = ("arbitrary",) * nd
    n_in, n_out = len(in_specs), len(out_shape)
    any_spec = pl.BlockSpec(memory_space=pl.ANY)
    c_in = [] if comm is None else comm.inputs
    c_out = [] if comm is None else comm.out_shapes
    c_scr = [] if comm is None else comm.scratch
    outs = _call(
        _hosted(body, n_in, n_out, comm, first, last, middle), name=name, grid=grid,
        out_shape=(*out_shape, *c_out),
        in_specs=[*in_specs, *[any_spec] * len(c_in)],
        out_specs=(*out_specs, *[any_spec] * len(c_out)),
        scratch_shapes=[*scratch_shapes, *c_scr],
        compiler_params=_params(sem, VMEM_LIMIT),
    )(*args, *c_in)
    return outs[:n_out], outs[n_out:]


def _grid_step(ndim):
    i, n = pl.program_id(0), pl.num_programs(0)
    for d in range(1, ndim):
        i, n = i * pl.num_programs(d) + pl.program_id(d), n * pl.num_programs(d)
    return i, n


def _at_first(ndim):
    return lambda: _grid_step(ndim)[0] == 0


def _at_last(ndim):
    def pred():
        i, n = _grid_step(ndim)
        return i == n - 1
    return pred


def _at_middle(ndim):
    def pred():
        i, n = _grid_step(ndim)
        return i == (3 * n) // 4
    return pred


def _fwd_in(x, g, w_full, name, comm=None):
    s = x.shape[0]
    ts = min(ROW_TILE, s)

    def body(x_ref, g_ref, w_ref, h_ref, pc_ref, qkv_ref, az_ref):
        xf = x_ref[...]
        r = lax.rsqrt(jnp.mean(xf * xf, axis=-1, keepdims=True) + EPS)
        h = (xf * r * g_ref[...]).astype(BF16)
        h_ref[...] = h
        pc_ref[...] = jnp.dot(h, w_ref[:, 0:2048], preferred_element_type=F32).astype(BF16)
        q = jnp.dot(h, w_ref[:, 2048:2560], preferred_element_type=F32)
        qkv_ref[:, 0:512] = (q * 0.125).astype(BF16)
        qkv_ref[:, 512:1536] = jnp.dot(h, w_ref[:, 2560:3584], preferred_element_type=F32).astype(BF16)
        az_ref[...] = jnp.dot(h, w_ref[:, 3584:4096], preferred_element_type=F32).astype(BF16)

    row = lambda width: pl.BlockSpec((ts, width), lambda i: (i, 0))
    return _hosted_call(
        body, comm, name=name, grid=(s // ts,),
        out_shape=(jax.ShapeDtypeStruct((s, D_MODEL), BF16), jax.ShapeDtypeStruct((s, 2048), BF16),
                   jax.ShapeDtypeStruct((s, 1536), BF16), jax.ShapeDtypeStruct((s, 512), BF16)),
        in_specs=[row(D_MODEL), pl.BlockSpec((1, D_MODEL), lambda i: (0, 0)),
                  pl.BlockSpec((D_MODEL, N_IN), lambda i: (0, 0))],
        out_specs=(row(D_MODEL), row(2048), row(1536), row(512)),
        args=(x, g, w_full), sem=("parallel",))


ATTN_ROWS = 128
ATTN_DONE = 104.0


def _attn_pieces(tq, rc):
    lane = lax.broadcasted_iota(jnp.int32, (1, LANES), 1)
    lo = lane < HEAD_DIM
    row = lax.broadcasted_iota(jnp.int32, (tq, tq), 0)
    col = lax.broadcasted_iota(jnp.int32, (tq, tq), 1)
    tri_gt = jnp.where(row > col, 1.0, 0.0).astype(BF16)
    tri_le = jnp.where(row <= col, 1.0, 0.0).astype(BF16)
    rrow = lax.broadcasted_iota(jnp.int32, (rc, tq), 0)
    rcol = lax.broadcasted_iota(jnp.int32, (rc, tq), 1)
    causal = [rcol < rrow + r * rc for r in range(tq // rc)]
    return lo, causal, tri_gt, tri_le


def _split_heads(a, lo):
    z = jnp.zeros_like(a)
    return (jnp.where(lo, a, z), jnp.where(lo, z, a))


def _softplus(z, causal, diag):
    neg_abs = lax.bitcast_convert_type(lax.bitcast_convert_type(z, jnp.uint32) | jnp.uint32(0x80000000), F32)
    sp = jnp.maximum(z, 0.0) + jnp.log(1.0 + jnp.exp(neg_abs))
    if diag:
        sp = jnp.where(causal, sp, 0.0)
    return sp


def _attn_fwd(qkv, name, comm=None):
    s = qkv.shape[0]
    tq = min(ATTN_TILE, s)
    nq = s // tq
    rc = min(ATTN_ROWS, tq)
    n_rc = tq // rc
    chains = [(r, hh) for r in range(n_rc) for hh in range(2)]

    def body(q_ref, k_ref, v_ref, o_ref, lsum_ref, nblk_ref):
        hp, qi = pl.program_id(0), pl.program_id(1)
        lo, causal, tri_gt, _ = _attn_pieces(tq, rc)
        qh = _split_heads(q_ref[...], lo)
        qc = {(r, hh): qh[hh][r * rc:(r + 1) * rc] for r, hh in chains}

        def block(kb, carry, diag):
            start = pl.multiple_of(kb * tq, tq)
            k = k_ref[pl.ds(start, tq), :]
            vh = _split_heads(v_ref[pl.ds(start, tq), :], lo)
            z = {ch: lax.dot_general(qc[ch], k, NT, preferred_element_type=F32) for ch in chains}
            sp = {ch: _softplus(z[ch], causal[ch[0]], diag) for ch in chains}
            later = {ch: jnp.dot(sp[ch].astype(BF16), tri_gt, preferred_element_type=F32) for ch in chains}
            a = {}
            for ch in chains:
                a[ch] = jnp.exp((z[ch] - sp[ch]) - (carry[ch[0]][1 + ch[1]] + later[ch]))
                if diag:
                    a[ch] = jnp.where(causal[ch[0]], a[ch], 0.0)
            pv = {ch: jnp.dot(a[ch].astype(BF16), vh[ch[1]], preferred_element_type=F32) for ch in chains}
            out = []
            for r in range(n_rc):
                acc = carry[r][0] + pv[(r, 0)] + pv[(r, 1)]
                cs = [carry[r][1 + hh] + jnp.sum(sp[(r, hh)], axis=-1, keepdims=True) for hh in range(2)]
                out.append((acc, cs[0], cs[1]))
            return tuple(out)

        def least(carry):
            m = jnp.minimum(carry[0][1], carry[0][2])
            for r in range(1, n_rc):
                m = jnp.minimum(m, jnp.minimum(carry[r][1], carry[r][2]))
            return jnp.min(m)

        zc = jnp.zeros((rc, 1), F32)
        carry = tuple((jnp.zeros((rc, LANES), F32), zc, zc) for _ in range(n_rc))
        carry = block(qi, carry, True)

        def go_on(st):
            return jnp.logical_and(st[0] < qi, st[1] < ATTN_DONE)

        def step(st):
            new = block(qi - 1 - st[0], st[2], False)
            return st[0] + 1, least(new), new

        walked, _, carry = lax.while_loop(go_on, step, (jnp.int32(0), least(carry), carry))
        for r in range(n_rc):
            o_ref[r * rc:(r + 1) * rc, :] = carry[r][0].astype(BF16)
            lsum_ref[r * rc:(r + 1) * rc, :] = jnp.where(lo, carry[r][1], carry[r][2])
        nblk_ref[hp, qi] = walked.astype(F32)

    blk = pl.BlockSpec((tq, LANES), lambda hp, qi: (qi, hp))
    o512 = jax.ShapeDtypeStruct((s, D_SB), F32)
    return _hosted_call(
        body, comm, name=name, grid=(4, nq),
        out_shape=(jax.ShapeDtypeStruct((s, D_SB), BF16), o512, jax.ShapeDtypeStruct((4, nq), F32)),
        in_specs=[blk, pl.BlockSpec((s, LANES), lambda hp, qi: (0, 4 + hp)),
                  pl.BlockSpec((s, LANES), lambda hp, qi: (0, 8 + hp))],
        out_specs=(blk, blk, pl.BlockSpec(memory_space=pltpu.SMEM)),
        args=(qkv, qkv, qkv), sem=("arbitrary", "arbitrary"))


HALO = 16


def _conv_taps(cc_ref, ch_ref, ccp_ref, chp_ref, halo_ref, first):
    u = cc_ref[...].astype(F32) * ch_ref[...].astype(F32)
    halo_ref[...] = ccp_ref[...].astype(F32) * chp_ref[...].astype(F32) * jnp.where(first, 0.0, 1.0)
    p6 = halo_ref[HALO - 2:HALO - 1, :]
    p7 = halo_ref[HALO - 1:HALO, :]
    rowi = lax.broadcasted_iota(jnp.int32, u.shape, 0)
    u1 = jnp.where(rowi == 0, p7, pltpu.roll(u, 1, 0))
    u2 = jnp.where(rowi == 0, p6, jnp.where(rowi == 1, p7, pltpu.roll(u, 2, 0)))
    return u, u1, u2


def _fwd_mid(x, pc, az, ya, p4, layer, cw, cb, bg, wout_full, pg, wpg_full, bpg, wpe_full, name):
    s = x.shape[0]
    ts = min(ROW_TILE, s)
    blk_h = ts // HALO

    def body(x_ref, cb_ref_, cc_ref, ch_ref, cz_ref, ccp_ref, chp_ref, az_ref, ya_ref, p_ref,
             cw_ref, cbias_ref, bg_ref, wout_ref, pg_ref, wpg_ref, bpg_ref, wpe_ref,
             x2_ref, x3_ref, gated_ref, h2_ref, gate_ref, e_ref, halo_ref):
        i = pl.program_id(0)
        lane = lax.broadcasted_iota(jnp.int32, (1, LANES), 1)
        lo = lane < HEAD_DIM
        u, u1, u2 = _conv_taps(cc_ref, ch_ref, ccp_ref, chp_ref, halo_ref, i == 0)
        conv = cbias_ref[...] + cw_ref[0:1, :] * u2 + cw_ref[1:2, :] * u1 + cw_ref[2:3, :] * u
        yc = cb_ref_[...].astype(F32) * conv
        for sl in range(8):
            cols = slice(LANES * (sl % 4), LANES * (sl % 4 + 1))
            y = yc[:, cols] if sl < 4 else ya_ref[:, cols].astype(F32)
            zc = (cz_ref[:, cols] if sl < 4 else az_ref[:, cols]).astype(F32)
            rg = lax.rsqrt(_group_bcast_sum(y * y, lo) * (1.0 / HEAD_DIM) + EPS)
            yn = y * rg * bg_ref[:, LANES * sl:LANES * (sl + 1)]
            gated_ref[:, LANES * sl:LANES * (sl + 1)] = (yn * (zc * _sigmoid(zc))).astype(BF16)
        x2 = x_ref[...] + jnp.dot(gated_ref[...], wout_ref[...], preferred_element_type=F32)
        x2_ref[...] = x2
        r2 = lax.rsqrt(jnp.mean(x2 * x2, axis=-1, keepdims=True) + EPS)
        h2 = (x2 * r2 * pg_ref[...]).astype(BF16)
        h2_ref[...] = h2
        gate = _sigmoid(jnp.dot(h2, wpg_ref[...], preferred_element_type=F32) + bpg_ref[...])
        gate_ref[...] = gate.astype(BF16)
        e = jnp.dot(p_ref[...].astype(BF16), wpe_ref[...], preferred_element_type=F32)
        e_ref[...] = e.astype(BF16)
        x3_ref[...] = x2 + gate * e

    row = lambda width, cb_=0: pl.BlockSpec((ts, width), lambda i: (i, cb_))
    prev = lambda cb_: pl.BlockSpec((HALO, 512), lambda i: (jnp.maximum(i * blk_h - 1, 0), cb_))
    vec = lambda width: pl.BlockSpec((1, width), lambda i: (0, 0))
    wspec = lambda r_, c_: pl.BlockSpec((r_, c_), lambda i: (0, 0))
    f32o = jax.ShapeDtypeStruct((s, D_MODEL), F32)
    bfo = jax.ShapeDtypeStruct((s, D_MODEL), BF16)
    return _call(
        body, name=name, grid=(s // ts,),
        out_shape=(f32o, f32o, bfo, bfo, bfo, bfo),
        scratch_shapes=[pltpu.VMEM((HALO, 512), F32)],
        in_specs=[row(D_MODEL), row(512, 0), row(512, 1), row(512, 2), row(512, 3), prev(1), prev(2),
                  row(512), row(512),
                  pl.BlockSpec((None, None, ts, PLE_DIM), lambda i: (layer, 0, i, 0)),
                  pl.BlockSpec((3, 512), lambda i: (0, 0)), vec(512), vec(D_MODEL),
                  wspec(D_MODEL, D_MODEL), vec(D_MODEL), wspec(D_MODEL, D_MODEL), vec(D_MODEL),
                  wspec(PLE_DIM, D_MODEL)],
        out_specs=(row(D_MODEL),) * 6,
        compiler_params=_params(("parallel",), VMEM_LIMIT),
    )(x, pc, pc, pc, pc, pc, pc, az, ya, p4, cw, cb, bg, wout_full, pg, wpg_full, bpg, wpe_full)


def _loss_head(xf, target, fg):
    s = xf.shape[0]
    ts = min(ROW_TILE, s)

    def body(x_ref, t_ref, g_ref, dx_ref, loss_ref, dg_ref):
        i = pl.program_id(0)

        @pl.when(i == 0)
        def _():
            loss_ref[...] = jnp.zeros_like(loss_ref)
            dg_ref[...] = jnp.zeros_like(dg_ref)

        x = x_ref[...]
        g = g_ref[...]
        r = lax.rsqrt(jnp.mean(x * x, axis=-1, keepdims=True) + EPS)
        xn = x * r
        err = xn * g - t_ref[...]
        per_row = jnp.sum(err * err, axis=-1, keepdims=True)
        loss_ref[...] += jnp.sum(per_row, axis=0, keepdims=True) * (0.5 / D_MODEL)
        dy = err * (1.0 / D_MODEL)
        dg_ref[...] += jnp.sum(dy * xn, axis=0, keepdims=True)
        dxn = dy * g
        dx_ref[...] = r * (dxn - xn * jnp.mean(dxn * xn, axis=-1, keepdims=True))

    row = pl.BlockSpec((ts, D_MODEL), lambda i: (i, 0))
    return _call(
        body, name="loss_head", grid=(s // ts,),
        out_shape=(jax.ShapeDtypeStruct((s, D_MODEL), F32), jax.ShapeDtypeStruct((1, LANES), F32),
                   jax.ShapeDtypeStruct((1, D_MODEL), F32)),
        in_specs=[row, row, pl.BlockSpec((1, D_MODEL), lambda i: (0, 0))],
        out_specs=(row, pl.BlockSpec((1, LANES), lambda i: (0, 0)), pl.BlockSpec((1, D_MODEL), lambda i: (0, 0))),
        compiler_params=_params(("arbitrary",), VMEM_LIMIT),
    )(xf, target, fg)


def _bwd_mid(dx3, x2, gate, e, pc, az, ya, gated, h2, p4, layer, cw, cb, bg, pg, wpg_full, wout_full, name):
    s = x2.shape[0]
    ts = min(ROW_TILE, s)
    blk_h = ts // HALO

    def body(dx3_ref, x2_ref, gate_ref, e_ref, cb_ref_, cc_ref, ch_ref, cz_ref, ccp_ref, chp_ref, az_ref, ya_ref,
             gated_ref, h2_ref, p_ref, cw_ref, cbias_ref, bg_ref, pg_ref, wpg_ref, wout_ref,
             dx2_ref, dya_ref, dmisc_ref, dconv_ref, dwout_ref, dwpg_ref, dwpe_ref,
             dbpg_ref, dpg_ref, dbg_ref, dcbias_ref, dcw_ref,
             dgated_ref, halo_ref, acc_out, acc_pg, acc_pe):
        i = pl.program_id(0)

        @pl.when(i == 0)
        def _():
            for ref in (dbpg_ref, dpg_ref, dbg_ref, dcbias_ref, dcw_ref, acc_out, acc_pg, acc_pe):
                ref[...] = jnp.zeros_like(ref)

        lane = lax.broadcasted_iota(jnp.int32, (1, LANES), 1)
        lo = lane < HEAD_DIM
        dx3 = dx3_ref[...]
        gate = gate_ref[...].astype(F32)
        de_b = (dx3 * gate).astype(BF16)
        acc_pe[...] += lax.dot_general(p_ref[...].astype(BF16), de_b, TN, preferred_element_type=F32)
        dgpre = dx3 * e_ref[...].astype(F32) * gate * (1.0 - gate)
        dbpg_ref[...] += jnp.sum(dgpre, axis=0, keepdims=True)
        dgpre_b = dgpre.astype(BF16)
        acc_pg[...] += lax.dot_general(h2_ref[...], dgpre_b, TN, preferred_element_type=F32)
        dh2 = lax.dot_general(dgpre_b, wpg_ref[...], NT, preferred_element_type=F32)
        x2 = x2_ref[...]
        r2 = lax.rsqrt(jnp.mean(x2 * x2, axis=-1, keepdims=True) + EPS)
        xn2 = x2 * r2
        dpg_ref[...] += jnp.sum(dh2 * xn2, axis=0, keepdims=True)
        dxn = dh2 * pg_ref[...]
        dx2 = dx3 + r2 * (dxn - xn2 * jnp.mean(dxn * xn2, axis=-1, keepdims=True))
        dx2_ref[...] = dx2
        dx2_b = dx2.astype(BF16)
        acc_out[...] += lax.dot_general(gated_ref[...], dx2_b, TN, preferred_element_type=F32)
        dgated_ref[...] = lax.dot_general(dx2_b, wout_ref[...], NT, preferred_element_type=F32)

        @pl.when(i == pl.num_programs(0) - 1)
        def _():
            dwout_ref[...] = acc_out[...].astype(BF16)
            dwpg_ref[...] = acc_pg[...].astype(BF16)
            dwpe_ref[...] = acc_pe[...].astype(BF16)

        u, u1, u2 = _conv_taps(cc_ref, ch_ref, ccp_ref, chp_ref, halo_ref, i == 0)
        conv = cbias_ref[...] + cw_ref[0:1, :] * u2 + cw_ref[1:2, :] * u1 + cw_ref[2:3, :] * u
        c_b = cb_ref_[...].astype(F32)
        yc = c_b * conv
        for sl in range(8):
            cols = slice(LANES * (sl % 4), LANES * (sl % 4 + 1))
            wide = slice(LANES * sl, LANES * (sl + 1))
            y = yc[:, cols] if sl < 4 else ya_ref[:, cols].astype(F32)
            zc = (cz_ref[:, cols] if sl < 4 else az_ref[:, cols]).astype(F32)
            bgs = bg_ref[:, wide]
            dgt = dgated_ref[:, wide]
            rg = lax.rsqrt(_group_bcast_sum(y * y, lo) * (1.0 / HEAD_DIM) + EPS)
            yhat = y * rg
            sig = _sigmoid(zc)
            dyn = dgt * (zc * sig)
            dzc = dgt * (yhat * bgs) * (sig * (1.0 + zc * (1.0 - sig)))
            dbg_ref[:, wide] += jnp.sum(dyn * yhat, axis=0, keepdims=True)
            dyh = dyn * bgs
            dy = rg * (dyh - yhat * (_group_bcast_sum(dyh * yhat, lo) * (1.0 / HEAD_DIM)))
            if sl < 4:
                dconv = dy * c_b[:, cols]
                dmisc_ref[:, cols] = (dy * conv[:, cols]).astype(BF16)
                dmisc_ref[:, 512 + LANES * sl:512 + LANES * (sl + 1)] = dzc.astype(BF16)
                dconv_ref[:, cols] = dconv
                dcbias_ref[:, cols] += jnp.sum(dconv, axis=0, keepdims=True)
                dcw_ref[0:1, cols] += jnp.sum(dconv * u2[:, cols], axis=0, keepdims=True)
                dcw_ref[1:2, cols] += jnp.sum(dconv * u1[:, cols], axis=0, keepdims=True)
                dcw_ref[2:3, cols] += jnp.sum(dconv * u[:, cols], axis=0, keepdims=True)
            else:
                dya_ref[:, cols] = dy.astype(BF16)
                dmisc_ref[:, 1024 + LANES * (sl - 4):1024 + LANES * (sl - 3)] = dzc.astype(BF16)

    row = lambda width, cb_=0: pl.BlockSpec((ts, width), lambda i: (i, cb_))
    prev = lambda cb_: pl.BlockSpec((HALO, 512), lambda i: (jnp.maximum(i * blk_h - 1, 0), cb_))
    vec = lambda width: pl.BlockSpec((1, width), lambda i: (0, 0))
    wspec = lambda r_, c_: pl.BlockSpec((r_, c_), lambda i: (0, 0))
    vo = lambda width: jax.ShapeDtypeStruct((1, width), F32)
    sq = jax.ShapeDtypeStruct((D_MODEL, D_MODEL), BF16)
    return _call(
        body, name=name, grid=(s // ts,),
        out_shape=(jax.ShapeDtypeStruct((s, D_MODEL), F32), jax.ShapeDtypeStruct((s, 512), BF16),
                   jax.ShapeDtypeStruct((s, 1536), BF16), jax.ShapeDtypeStruct((s, 512), F32),
                   sq, sq, jax.ShapeDtypeStruct((PLE_DIM, D_MODEL), BF16),
                   vo(D_MODEL), vo(D_MODEL), vo(D_MODEL), vo(512), jax.ShapeDtypeStruct((SUBLANES, 512), F32)),
        in_specs=[row(D_MODEL), row(D_MODEL), row(D_MODEL), row(D_MODEL),
                  row(512, 0), row(512, 1), row(512, 2), row(512, 3), prev(1), prev(2), row(512), row(512),
                  row(D_MODEL), row(D_MODEL),
                  pl.BlockSpec((None, None, ts, PLE_DIM), lambda i: (layer, 0, i, 0)),
                  pl.BlockSpec((3, 512), lambda i: (0, 0)), vec(512), vec(D_MODEL), vec(D_MODEL),
                  wspec(D_MODEL, D_MODEL), wspec(D_MODEL, D_MODEL)],
        out_specs=(row(D_MODEL), row(512), row(1536), row(512),
                   wspec(D_MODEL, D_MODEL), wspec(D_MODEL, D_MODEL), wspec(PLE_DIM, D_MODEL),
                   vec(D_MODEL), vec(D_MODEL), vec(D_MODEL), vec(512),
                   pl.BlockSpec((SUBLANES, 512), lambda i: (0, 0))),
        scratch_shapes=[pltpu.VMEM((ts, D_MODEL), F32), pltpu.VMEM((HALO, 512), F32),
                        pltpu.VMEM((D_MODEL, D_MODEL), F32), pltpu.VMEM((D_MODEL, D_MODEL), F32),
                        pltpu.VMEM((PLE_DIM, D_MODEL), F32)],
        compiler_params=_params(("arbitrary",), VMEM_LIMIT),
    )(dx3, x2, gate, e, pc, pc, pc, pc, pc, pc, az, ya, gated, h2, p4, cw, cb, bg, pg, wpg_full, wout_full)


def _attn_bwd(qkv, lsum, nblk, dya, name, comm=None):
    s = qkv.shape[0]
    tq = min(ATTN_TILE, s)
    nq = s // tq
    rc = min(ATTN_ROWS, tq)
    n_rc = tq // rc
    chains = [(r, hh) for r in range(n_rc) for hh in range(2)]

    def body(nblk_ref, q_ref, k_ref, v_ref, lsum_ref, do_ref, dq_ref, dk_ref, dv_ref, dk_acc, dv_acc):
        hp, qi = pl.program_id(0), pl.program_id(1)

        @pl.when(qi == 0)
        def _():
            dk_acc[...] = jnp.zeros_like(dk_acc)
            dv_acc[...] = jnp.zeros_like(dv_acc)

        lo, causal, tri_gt, tri_le = _attn_pieces(tq, rc)
        lane = lax.broadcasted_iota(jnp.int32, (1, LANES), 1)
        qh = _split_heads(q_ref[...], lo)
        doh = _split_heads(do_ref[...].astype(BF16), lo)
        lt = lsum_ref[...]
        ltot_h = (jnp.sum(jnp.where(lane == 0, lt, 0.0), axis=-1, keepdims=True),
                  jnp.sum(jnp.where(lane == HEAD_DIM, lt, 0.0), axis=-1, keepdims=True))
        rows = lambda a_, r: a_[r * rc:(r + 1) * rc]
        qc = {(r, hh): rows(qh[hh], r) for r, hh in chains}
        doc = {(r, hh): rows(doh[hh], r) for r, hh in chains}
        ltot = {(r, hh): rows(ltot_h[hh], r) for r, hh in chains}

        def block(kb, carry, diag):
            start = pl.multiple_of(kb * tq, tq)
            k = k_ref[pl.ds(start, tq), :]
            v = v_ref[pl.ds(start, tq), :]
            kh = _split_heads(k, lo)
            z = {ch: lax.dot_general(qc[ch], k, NT, preferred_element_type=F32) for ch in chains}
            da = {ch: lax.dot_general(doc[ch], v, NT, preferred_element_type=F32) for ch in chains}
            sp = {ch: _softplus(z[ch], causal[ch[0]], diag) for ch in chains}
            later = {ch: jnp.dot(sp[ch].astype(BF16), tri_gt, preferred_element_type=F32) for ch in chains}
            walked = {ch: carry[ch[0]][1 + ch[1]] + jnp.sum(sp[ch], axis=-1, keepdims=True) for ch in chains}
            a, g = {}, {}
            for ch in chains:
                a[ch] = jnp.exp((z[ch] - sp[ch]) - ((ltot[ch] - walked[ch]) + later[ch]))
                if diag:
                    a[ch] = jnp.where(causal[ch[0]], a[ch], 0.0)
                g[ch] = a[ch] * da[ch]
            upto = {ch: jnp.dot(g[ch].astype(BF16), tri_le, preferred_element_type=F32) for ch in chains}
            dz = {}
            for ch in chains:
                dz[ch] = g[ch] - jnp.exp(z[ch] - sp[ch]) * (carry[ch[0]][3 + ch[1]] + upto[ch])
                if diag:
                    dz[ch] = jnp.where(causal[ch[0]], dz[ch], 0.0)
                dz[ch] = dz[ch].astype(BF16)
            dqc = {ch: jnp.dot(dz[ch], kh[ch[1]], preferred_element_type=F32) for ch in chains}
            dkc = [lax.dot_general(dz[ch], qc[ch], TN, preferred_element_type=F32) for ch in chains]
            dvc = [lax.dot_general(a[ch].astype(BF16), doc[ch], TN, preferred_element_type=F32) for ch in chains]
            dk_acc[pl.ds(start, tq), :] += sum(dkc[1:], dkc[0])
            dv_acc[pl.ds(start, tq), :] += sum(dvc[1:], dvc[0])
            out = []
            for r in range(n_rc):
                gsum = [carry[r][3 + hh] + jnp.sum(g[(r, hh)], axis=-1, keepdims=True) for hh in range(2)]
                out.append((carry[r][0] + dqc[(r, 0)] + dqc[(r, 1)], walked[(r, 0)], walked[(r, 1)], *gsum))
            return tuple(out)

        zc = jnp.zeros((rc, 1), F32)
        carry = tuple((jnp.zeros((rc, LANES), F32), zc, zc, zc, zc) for _ in range(n_rc))
        first = qi - jnp.clip(nblk_ref[hp, qi].astype(jnp.int32), 0, qi)
        carry = lax.fori_loop(first, qi, lambda kb, cr: block(kb, cr, False), carry)
        carry = block(qi, carry, True)
        for r in range(n_rc):
            dq_ref[r * rc:(r + 1) * rc, :] = (carry[r][0] * 0.125).astype(BF16)

        @pl.when(qi == pl.num_programs(1) - 1)
        def _():
            dk_ref[...] = dk_acc[...].astype(BF16)
            dv_ref[...] = dv_acc[...].astype(BF16)

    blk = pl.BlockSpec((tq, LANES), lambda hp, qi: (qi, hp))
    col = pl.BlockSpec((s, LANES), lambda hp, qi: (0, hp))
    o512 = jax.ShapeDtypeStruct((s, D_SB), BF16)
    return _hosted_call(
        body, comm, name=name, grid=(4, nq),
        out_shape=(o512, o512, o512),
        in_specs=[pl.BlockSpec(memory_space=pltpu.SMEM), blk,
                  pl.BlockSpec((s, LANES), lambda hp, qi: (0, 4 + hp)),
                  pl.BlockSpec((s, LANES), lambda hp, qi: (0, 8 + hp)), blk, blk],
        out_specs=(blk, col, col),
        scratch_shapes=[pltpu.VMEM((s, LANES), F32), pltpu.VMEM((s, LANES), F32)],
        args=(nblk, qkv, qkv, qkv, lsum, dya), sem=("parallel", "arbitrary"))


def _bwd_dproj(dmisc, dconv, pc, dq, dk, dv, x, dx2, g, cw, win_full, name, comm=None):
    s = x.shape[0]
    ts = min(ROW_TILE, s)
    blk8 = ts // SUBLANES
    last8 = s // SUBLANES - 1

    def body(dcb_ref, dcz_ref, daz_ref, dconv_ref, nxt_ref, cc_ref, ch_ref, dq_ref, dk_ref, dv_ref,
             x_ref, dx2_ref, g_ref, cw_ref, w_ref, dproj_ref, dx_ref, dg_ref):
        i = pl.program_id(0)

        @pl.when(i == 0)
        def _():
            dg_ref[...] = jnp.zeros_like(dg_ref)

        keep = jnp.where(i == pl.num_programs(0) - 1, 0.0, 1.0)
        dc = dconv_ref[...]
        n0 = nxt_ref[0:1, :] * keep
        n1 = nxt_ref[1:2, :] * keep
        rowi = lax.broadcasted_iota(jnp.int32, dc.shape, 0)
        dc1 = jnp.where(rowi == ts - 1, n0, pltpu.roll(dc, ts - 1, 0))
        dc2 = jnp.where(rowi == ts - 2, n0, jnp.where(rowi == ts - 1, n1, pltpu.roll(dc, ts - 2, 0)))
        du = cw_ref[2:3, :] * dc + cw_ref[1:2, :] * dc1 + cw_ref[0:1, :] * dc2
        dproj_ref[:, 0:512] = dcb_ref[...]
        dproj_ref[:, 512:1024] = (du * ch_ref[...].astype(F32)).astype(BF16)
        dproj_ref[:, 1024:1536] = (du * cc_ref[...].astype(F32)).astype(BF16)
        dproj_ref[:, 1536:2048] = dcz_ref[...]
        dproj_ref[:, 2048:2560] = dq_ref[...]
        dproj_ref[:, 2560:3072] = dk_ref[...]
        dproj_ref[:, 3072:3584] = dv_ref[...]
        dproj_ref[:, 3584:4096] = daz_ref[...]
        dh = lax.dot_general(dproj_ref[...], w_ref[...], NT, preferred_element_type=F32)
        x = x_ref[...]
        r = lax.rsqrt(jnp.mean(x * x, axis=-1, keepdims=True) + EPS)
        xn = x * r
        dg_ref[...] += jnp.sum(dh * xn, axis=0, keepdims=True)
        dxn = dh * g_ref[...]
        dx_ref[...] = dx2_ref[...] + r * (dxn - xn * jnp.mean(dxn * xn, axis=-1, keepdims=True))

    row = lambda width, cb_=0: pl.BlockSpec((ts, width), lambda i: (i, cb_))
    nxt = pl.BlockSpec((SUBLANES, 512), lambda i: (jnp.minimum((i + 1) * blk8, last8), 0))
    vec = lambda width: pl.BlockSpec((1, width), lambda i: (0, 0))
    return _hosted_call(
        body, comm, name=name, grid=(s // ts,),
        out_shape=(jax.ShapeDtypeStruct((s, N_IN), BF16), jax.ShapeDtypeStruct((s, D_MODEL), F32),
                   jax.ShapeDtypeStruct((1, D_MODEL), F32)),
        in_specs=[row(512, 0), row(512, 1), row(512, 2), row(512), nxt, row(512, 1), row(512, 2),
                  row(512), row(512), row(512), row(D_MODEL), row(D_MODEL), vec(D_MODEL),
                  pl.BlockSpec((3, 512), lambda i: (0, 0)),
                  pl.BlockSpec((D_MODEL, N_IN), lambda i: (0, 0))],
        out_specs=(row(N_IN), row(D_MODEL), vec(D_MODEL)),
        args=(dmisc, dmisc, dmisc, dconv, dconv, pc, pc, dq, dk, dv, x, dx2, g, cw, win_full),
        sem=("arbitrary",))


def _atb(a, b, name, a_index=None, comm=None):
    s, n = b.shape
    m = a.shape[-1]
    ts = min(512, s)
    tn = min(1024, n)
    if a_index is None:
        a_spec = pl.BlockSpec((ts, m), lambda j, i: (i, 0))
    else:
        a_spec = pl.BlockSpec((None, None, ts, m), lambda j, i: (a_index, 0, i, 0))

    def body(a_ref, b_ref, o_ref, acc_ref):
        i = pl.program_id(1)

        @pl.when(i == 0)
        def _():
            acc_ref[...] = jnp.zeros_like(acc_ref)

        acc_ref[...] += lax.dot_general(a_ref[...].astype(BF16), b_ref[...], TN, preferred_element_type=F32)

        @pl.when(i == pl.num_programs(1) - 1)
        def _():
            o_ref[...] = acc_ref[...].astype(BF16)

    (out,), got = _hosted_call(
        body, comm, name=name, grid=(n // tn, s // ts),
        out_shape=(jax.ShapeDtypeStruct((m, n), BF16),),
        in_specs=[a_spec, pl.BlockSpec((ts, tn), lambda j, i: (i, j))],
        out_specs=(pl.BlockSpec((m, tn), lambda j, i: (0, j)),),
        scratch_shapes=[pltpu.VMEM((m, tn), F32)],
        args=(a, b), sem=("parallel", "arbitrary"))
    return out, got


def _adamw_math(w, g, m, v):
    m2 = ADAM_B1 * m + (1.0 - ADAM_B1) * g
    v2 = ADAM_B2 * v + (1.0 - ADAM_B2) * (g * g)
    m_hat = m2 / (1.0 - ADAM_B1 ** ADAM_STEP)
    v_hat = v2 / (1.0 - ADAM_B2 ** ADAM_STEP)
    delta = -ADAM_LR * (m_hat / (jnp.sqrt(v_hat) + ADAM_EPS) + ADAM_WD * w)
    return delta, m2, v2


def _adamw_sum8(parts, w, m, v, name):
    _, rows, cols = w.shape
    tr = min(rows, 256)
    n_tiles = rows // tr
    assert len(parts) == DEPTH == 2

    def body(p0_ref, p1_ref, w_ref, m_ref, v_ref, g_ref, d_ref, m2_ref, v2_ref):
        def run(p_ref):
            g = p_ref[0].astype(F32)
            for d in range(1, N_DEV):
                g = g + p_ref[d].astype(F32)
            g_ref[...] = g
            d_ref[...], m2_ref[...], v2_ref[...] = _adamw_math(w_ref[...], g, m_ref[...], v_ref[...])

        pl.when(pl.program_id(0) == 0)(lambda: run(p0_ref))
        pl.when(pl.program_id(0) == 1)(lambda: run(p1_ref))

    part0 = pl.BlockSpec((N_DEV, tr, cols), lambda l, i: (0, jnp.where(l == 0, i, n_tiles - 1), 0))
    part1 = pl.BlockSpec((N_DEV, tr, cols), lambda l, i: (0, jnp.where(l == 1, i, 0), 0))
    tile = pl.BlockSpec((None, tr, cols), lambda l, i: (l, i, 0))
    o = jax.ShapeDtypeStruct((DEPTH, rows, cols), F32)
    return _call(
        body, name=name, grid=(DEPTH, n_tiles),
        out_shape=(o, o, o, o),
        in_specs=[part0, part1, tile, tile, tile],
        out_specs=(tile, tile, tile, tile),
        compiler_params=_params(("arbitrary", "arbitrary"), VMEM_LIMIT),
    )(parts[0], parts[1], w, m, v)


def _adamw_plain(g, w, m, v, name):
    rows, cols = g.shape

    def body(g_ref, w_ref, m_ref, v_ref, d_ref, m2_ref, v2_ref):
        d_ref[...], m2_ref[...], v2_ref[...] = _adamw_math(w_ref[...], g_ref[...], m_ref[...], v_ref[...])

    full = pl.BlockSpec((rows, cols), lambda: (0, 0))
    o = jax.ShapeDtypeStruct((rows, cols), F32)
    return _call(body, name=name, out_shape=(o, o, o), in_specs=[full] * 4, out_specs=(full,) * 3)(g, w, m, v)


def _sum8_small(parts):
    def body(p_ref, g_ref):
        g = p_ref[0]
        for d in range(1, N_DEV):
            g = g + p_ref[d]
        g_ref[...] = g

    return _call(
        body, name="sum_small_grads",
        out_shape=jax.ShapeDtypeStruct((SMALL_ROWS, LANES), F32),
        in_specs=[pl.BlockSpec((N_DEV, SMALL_ROWS, LANES), lambda: (0, 0, 0))],
        out_specs=pl.BlockSpec((SMALL_ROWS, LANES), lambda: (0, 0)),
    )(parts)


def kernel(x, p, norm_g, w_in, conv_w, conv_b, branch_g, w_out, ple_norm_g, w_pg, b_pg, w_pe, final_g, loss_target, m_norm_g, m_w_in, m_conv_w, m_conv_b, m_branch_g, m_w_out, m_ple_norm_g, m_w_pg, m_b_pg, m_w_pe, m_final_g, v_norm_g, v_w_in, v_conv_w, v_conv_b, v_branch_g, v_w_out, v_ple_norm_g, v_w_pg, v_b_pg, v_w_pe, v_final_g):
    s = x.shape[1]
    x0 = x.reshape(s, D_MODEL)
    target = loss_target.reshape(s, D_MODEL)
    me_blk = _my_block()

    win_s = _cast_bf16(w_in.reshape(DEPTH * D_MODEL, 512), "cast_w_in").reshape(DEPTH, D_MODEL, 512)
    wout_s = _cast_bf16(w_out.reshape(DEPTH * 128, D_MODEL), "cast_w_out").reshape(DEPTH, 128, D_MODEL)
    wpg_s = _cast_bf16(w_pg.reshape(DEPTH * 128, D_MODEL), "cast_w_pg").reshape(DEPTH, 128, D_MODEL)
    wpe_s = _cast_bf16(w_pe.reshape(DEPTH * PLE_DIM, 128), "cast_w_pe").reshape(DEPTH, PLE_DIM, 128)
    cw_s = jnp.zeros((SUBLANES, LANES), F32).at[:DEPTH * 3, :HEAD_DIM].set(conv_w.reshape(DEPTH * 3, HEAD_DIM))
    bf = lambda r_, c_: jax.ShapeDtypeStruct((r_, c_), BF16)
    w_items = lambda l: [(wout_s[l], bf(D_MODEL, D_MODEL), "rows128"), (wpg_s[l], bf(D_MODEL, D_MODEL), "rows128"),
                         (wpe_s[l], bf(PLE_DIM, D_MODEL), "cols128")]
    win_f = [None] * DEPTH
    win_f[0], cw_all = _comm_call(_gather_comm([
        (win_s[0], bf(D_MODEL, N_IN), "cols512"),
        (cw_s, jax.ShapeDtypeStruct((N_DEV, SUBLANES, LANES), F32), "slot")]), "gather_w_in_0")
    cw_full = jnp.transpose(cw_all[:, :DEPTH * 3, :HEAD_DIM].reshape(N_DEV, DEPTH, 3, HEAD_DIM), (1, 2, 0, 3))
    cw_full = cw_full.reshape(DEPTH, 3, D_CONV)
    gather_rest_0 = _gather_comm(w_items(0))
    gather_1 = _gather_comm([(win_s[1], bf(D_MODEL, N_IN), "cols512"), *w_items(1)])

    vec = lambda a, l: a[l][None, :]

    saved = []
    xl = x0
    wout_f, wpg_f, wpe_f = [None] * DEPTH, [None] * DEPTH, [None] * DEPTH
    for l in range(DEPTH):
        (h, pc, qkv, az), got = _fwd_in(xl, vec(norm_g, l), win_f[l], f"fwd_in_{l}",
                                        comm=gather_rest_0 if l == 0 else None)
        if l == 0:
            wout_f[0], wpg_f[0], wpe_f[0] = got
        (ya, lsum, nblk), got = _attn_fwd(qkv, f"attn_fwd_{l}", comm=gather_1 if l == 0 else None)
        if l == 0:
            win_f[1], wout_f[1], wpg_f[1], wpe_f[1] = got
        x2, x3, gated, h2, gate, e = _fwd_mid(
            xl, pc, az, ya, p, l, cw_full[l], vec(conv_b, l), vec(branch_g, l), wout_f[l],
            vec(ple_norm_g, l), wpg_f[l], vec(b_pg, l), wpe_f[l], f"fwd_mid_{l}")
        saved.append(dict(x=xl, h=h, pc=pc, qkv=qkv, az=az, ya=ya, lsum=lsum, nblk=nblk, x2=x2, gated=gated, h2=h2,
                          gate=gate, e=e))
        xl = x3

    dx, loss_acc, d_final_g = _loss_head(xl, target, final_g[None, :])
    loss = lax.psum(loss_acc[0, 0], ("x", "y", "c"))

    dwin, dwout, dwpg, dwpe = [None] * DEPTH, [None] * DEPTH, [None] * DEPTH, [None] * DEPTH
    small = dict(norm_g=[None] * DEPTH, conv_b=[None] * DEPTH, branch_g=[None] * DEPTH,
                 ple_norm_g=[None] * DEPTH, b_pg=[None] * DEPTH, conv_w=[None] * DEPTH)
    slot = lambda r_, c_: jax.ShapeDtypeStruct((r_, c_), BF16)
    r_in, r_out, r_pg, r_pe = [None] * DEPTH, [None] * DEPTH, [None] * DEPTH, [None] * DEPTH

    def rest_items(l):
        return [(dwout[l], slot(128, D_MODEL), "rows128"), (dwpg[l], slot(128, D_MODEL), "rows128"),
                (dwpe[l], slot(PLE_DIM, 128), "cols128")]

    for l in reversed(range(DEPTH)):
        sv = saved[l]
        (dx2, dya, dmisc, dconv, dwout[l], dwpg[l], dwpe[l], d_bpg, d_pg, d_bg, d_cbias, d_cw) = _bwd_mid(
            dx, sv["x2"], sv["gate"], sv["e"], sv["pc"], sv["az"], sv["ya"], sv["gated"], sv["h2"], p, l,
            cw_full[l], vec(conv_b, l), vec(branch_g, l), vec(ple_norm_g, l), wpg_f[l], wout_f[l], f"bwd_mid_{l}")
        ride = None
        if l == 0:
            ride = _exchange_comm([(dwin[1], slot(D_MODEL, 512), "cols512"), *rest_items(1)])
        (dq, dk, dv), got = _attn_bwd(sv["qkv"], sv["lsum"], sv["nblk"], dya, f"attn_bwd_{l}", comm=ride)
        if l == 0:
            r_in[1], r_out[1], r_pg[1], r_pe[1] = got
        (dproj, dx, d_ng), _ = _bwd_dproj(dmisc, dconv, sv["pc"], dq, dk, dv, sv["x"], dx2, vec(norm_g, l),
                                          cw_full[l], win_f[l], f"bwd_dproj_{l}")
        ride = _exchange_comm(rest_items(0)) if l == 0 else None
        dwin[l], got = _atb(sv["h"], dproj, f"dw_in_{l}", comm=ride)
        if l == 0:
            r_out[0], r_pg[0], r_pe[0] = got
        small["norm_g"][l], small["conv_b"][l], small["branch_g"][l] = d_ng, d_cbias, d_bg
        small["ple_norm_g"][l], small["b_pg"][l], small["conv_w"][l] = d_pg, d_bpg, d_cw[:3]
    grad_x = dx.reshape(1, s, D_MODEL)

    flat = lambda parts: jnp.concatenate([a.reshape(-1) for a in parts])
    small_vec = jnp.concatenate([
        flat(small["norm_g"]), flat(small["conv_b"]), flat(small["branch_g"]), flat(small["ple_norm_g"]),
        flat(small["b_pg"]), d_final_g.reshape(-1), flat(small["conv_w"])]).reshape(SMALL_ROWS, LANES)
    r_in[0], r_small = _comm_call(_exchange_comm([
        (dwin[0], slot(D_MODEL, 512), "cols512"),
        (small_vec, jax.ShapeDtypeStruct((SMALL_ROWS, LANES), F32), "slot")]), "exchange_last")

    g_win, d_win, m_win, v_win = _adamw_sum8(r_in, w_in, m_w_in, v_w_in, "adamw_w_in")
    g_wout, d_wout, m_wout, v_wout = _adamw_sum8(r_out, w_out, m_w_out, v_w_out, "adamw_w_out")
    g_wpg, d_wpg, m_wpg, v_wpg = _adamw_sum8(r_pg, w_pg, m_w_pg, v_w_pg, "adamw_w_pg")
    g_wpe, d_wpe, m_wpe, v_wpe = _adamw_sum8(r_pe, w_pe, m_w_pe, v_w_pe, "adamw_w_pe")

    g_small = _sum8_small(r_small)
    repl = [(norm_g, m_norm_g, v_norm_g), (conv_b, m_conv_b, v_conv_b), (branch_g, m_branch_g, v_branch_g),
            (ple_norm_g, m_ple_norm_g, v_ple_norm_g), (b_pg, m_b_pg, v_b_pg), (final_g, m_final_g, v_final_g)]
    pack = lambda idx: jnp.concatenate([t[idx].reshape(-1) for t in repl]).reshape(SMALL_REPL_ROWS, LANES)
    g_repl = g_small[:SMALL_REPL_ROWS]
    d_repl, m_repl, v_repl = _adamw_plain(g_repl, pack(0), pack(1), pack(2), "adamw_replicated")

    def unpack(a):
        flat_a = a.reshape(-1)
        out, off = [], 0
        for t in repl:
            n = t[0].size
            out.append(flat_a[off:off + n].reshape(t[0].shape))
            off += n
        return out

    g_r, d_r, m_r, v_r = unpack(g_repl), unpack(d_repl), unpack(m_repl), unpack(v_repl)

    g_cw_full = g_small[SMALL_REPL_ROWS:].reshape(DEPTH, 3, D_CONV)
    g_cw = lax.dynamic_slice(g_cw_full, (0, 0, me_blk * HEAD_DIM), (DEPTH, 3, HEAD_DIM))
    pad_cw = lambda a: jnp.zeros((SUBLANES, LANES), F32).at[:3].set(a.reshape(3, LANES))
    v_cw_pad = jnp.ones((SUBLANES, LANES), F32).at[:3].set(v_conv_w.reshape(3, LANES))
    d_cw, m_cw, v_cw = _adamw_plain(pad_cw(g_cw), pad_cw(conv_w), pad_cw(m_conv_w), v_cw_pad, "adamw_conv_w")
    un_cw = lambda a: a[:3].reshape(DEPTH, 3, HEAD_DIM)

    def ordered(r, win_, cw_, wout_, wpg_, wpe_):
        return [r[0], win_, cw_, r[1], r[2], wout_, r[3], wpg_, r[4], wpe_, r[5]]

    grads = ordered(g_r, g_win, g_cw, g_wout, g_wpg, g_wpe)
    deltas = ordered(d_r, d_win, un_cw(d_cw), d_wout, d_wpg, d_wpe)
    new_m = ordered(m_r, m_win, un_cw(m_cw), m_wout, m_wpg, m_wpe)
    new_v = ordered(v_r, v_win, un_cw(v_cw), v_wout, v_wpg, v_wpe)
    return (loss, grad_x, *grads, *deltas, *new_m, *new_v)
```

```python
import jax
import jax.numpy as jnp
from jax import lax
from jax.experimental import pallas as pl
from jax.experimental.pallas import tpu as pltpu

F32 = jnp.float32
BF16 = jnp.bfloat16

D_MODEL = 1024
D_CONV = 512
D_SB = 512
N_IN = 4096
HEAD_DIM = 64
PLE_DIM = 256
DEPTH = 2
EPS = 1e-6
ADAM_LR = 0.001
ADAM_B1 = 0.9
ADAM_B2 = 0.999
ADAM_EPS = 1e-08
ADAM_WD = 0.01
ADAM_STEP = 10

LANES = 128
SUBLANES = 8
VMEM_BYTES_V7X = 64 * 1024 * 1024
VMEM_LIMIT = VMEM_BYTES_V7X - 8 * 1024 * 1024

N_DEV = 8
ROW_TILE = 256
ATTN_TILE = 256
SMALL_ROWS = 104
SMALL_REPL_ROWS = 80

NT = (((1,), (1,)), ((), ()))
TN = (((0,), (0,)), ((), ()))


def _call(body, **kw):
    return pl.pallas_call(body, **kw)


def _params(sem=None, vmem=None):
    return pltpu.CompilerParams(dimension_semantics=sem, vmem_limit_bytes=vmem)


def _sigmoid(z):
    return 1.0 / (1.0 + jnp.exp(-z))


def _group_bcast_sum(a, lo):
    s_lo = jnp.sum(jnp.where(lo, a, 0.0), axis=-1, keepdims=True)
    s_hi = jnp.sum(jnp.where(lo, 0.0, a), axis=-1, keepdims=True)
    return jnp.where(lo, s_lo, s_hi)


def _my_block():
    return 4 * lax.axis_index("x") + 2 * lax.axis_index("y") + lax.axis_index("c")


def _cast_bf16(a2d, name):
    rows, cols = a2d.shape
    tr = min(rows, 512)

    def body(a_ref, o_ref):
        o_ref[...] = a_ref[...].astype(BF16)

    return _call(
        body, name=name, grid=(rows // tr,),
        out_shape=jax.ShapeDtypeStruct((rows, cols), BF16),
        in_specs=[pl.BlockSpec((tr, cols), lambda i: (i, 0))],
        out_specs=pl.BlockSpec((tr, cols), lambda i: (i, 0)),
        compiler_params=_params(("parallel",)),
    )(a2d)


class _Comm:
    def __init__(self, inputs, out_shapes, scratch, begin, middle, finish):
        self.inputs, self.out_shapes, self.scratch = list(inputs), list(out_shapes), list(scratch)
        self.begin, self.middle, self.finish = begin, middle, finish


def _slab(kind, ref, blk):
    if kind == "cols512":
        return ref.at[:, pl.ds(blk * 512, 512)]
    if kind == "rows128":
        return ref.at[pl.ds(blk * 128, 128), :]
    if kind == "cols128":
        return ref.at[:, pl.ds(blk * 128, 128)]
    return ref.at[blk]


def _gather_comm(items):
    n_t = len(items)
    kinds = [it[2] for it in items]

    def ctx(ins, outs, sems):
        send_sems, recv_sems, local_sems = sems
        x, y, c = lax.axis_index("x"), lax.axis_index("y"), lax.axis_index("c")
        me, sibling = (x, y, c), (x, y, 1 - c)
        chips = [(1 - x, y), (x, 1 - y), (1 - x, 1 - y)]

        def place(t, dev):
            return _slab(kinds[t], outs[t], 4 * dev[0] + 2 * dev[1] + dev[2])

        def copy(t, k, block, to, own=False):
            return pltpu.make_async_remote_copy(
                src_ref=ins[t] if own else place(t, block), dst_ref=place(t, block),
                send_sem=send_sems.at[t, k], recv_sem=recv_sems.at[t, k],
                device_id=to, device_id_type=pl.DeviceIdType.MESH)

        mine = [pltpu.make_async_copy(ins[t], place(t, me), local_sems.at[t]) for t in range(n_t)]
        first = []
        for t in range(n_t):
            first.append(copy(t, 0, me, sibling, own=True))
            first += [copy(t, 1 + j, me, (*chip, c), own=True) for j, chip in enumerate(chips)]
        passed = [copy(t, 4 + j, (*chip, c), sibling) for j, chip in enumerate(chips) for t in range(n_t)]
        landed = [copy(t, 1 + j, (*chip, c), me) for j, chip in enumerate(chips) for t in range(n_t)]
        from_sibling = []
        for t in range(n_t):
            from_sibling.append(copy(t, 0, sibling, me))
            from_sibling += [copy(t, 4 + j, (*chip, 1 - c), me) for j, chip in enumerate(chips)]
        return mine, first, landed, passed, from_sibling

    def begin(ins, outs, sems):
        mine, first, _, _, _ = ctx(ins, outs, sems)
        for cp in mine + first:
            cp.start()

    def middle(ins, outs, sems):
        _, _, landed, passed, _ = ctx(ins, outs, sems)
        for got, fwd in zip(landed, passed):
            got.wait_recv()
            fwd.start()

    def finish(ins, outs, sems):
        mine, first, _, passed, from_sibling = ctx(ins, outs, sems)
        for cp in from_sibling:
            cp.wait_recv()
        for cp in first + passed:
            cp.wait_send()
        for cp in mine:
            cp.wait()

    scratch = [pltpu.SemaphoreType.DMA((n_t, 7)), pltpu.SemaphoreType.DMA((n_t, 7)), pltpu.SemaphoreType.DMA((n_t,))]
    return _Comm([it[0] for it in items], [it[1] for it in items], scratch, begin, middle, finish)


def _exchange_comm(items):
    n_t = len(items)
    kinds = [it[2] for it in items]

    def ctx(ins, outs, sems):
        send_sems, recv_sems, local_sems = sems
        x, y, c = lax.axis_index("x"), lax.axis_index("y"), lax.axis_index("c")
        me_blk = 4 * x + 2 * y + c

        def src(t, blk):
            return ins[t] if kinds[t] == "slot" else _slab(kinds[t], ins[t], blk)

        local = [pltpu.make_async_copy(src(t, me_blk), outs[t].at[me_blk], local_sems.at[t]) for t in range(n_t)]
        remote = []
        for k in range(1, N_DEV):
            px = 1 - x if k & 4 else x
            py = 1 - y if k & 2 else y
            pc_ = 1 - c if k & 1 else c
            for t in range(n_t):
                remote.append(pltpu.make_async_remote_copy(
                    src_ref=src(t, 4 * px + 2 * py + pc_), dst_ref=outs[t].at[me_blk],
                    send_sem=send_sems.at[k - 1, t], recv_sem=recv_sems.at[k - 1, t],
                    device_id=(px, py, pc_), device_id_type=pl.DeviceIdType.MESH))
        return local, remote

    def begin(ins, outs, sems):
        local, remote = ctx(ins, outs, sems)
        for cp in local + remote:
            cp.start()

    def finish(ins, outs, sems):
        local, remote = ctx(ins, outs, sems)
        for cp in remote:
            cp.wait_recv()
        for cp in remote:
            cp.wait_send()
        for cp in local:
            cp.wait()

    scratch = [pltpu.SemaphoreType.DMA((N_DEV - 1, n_t)), pltpu.SemaphoreType.DMA((N_DEV - 1, n_t)),
               pltpu.SemaphoreType.DMA((n_t,))]
    out_shapes = [jax.ShapeDtypeStruct((N_DEV, *it[1].shape), it[1].dtype) for it in items]
    return _Comm([it[0] for it in items], out_shapes, scratch, begin, None, finish)


def _comm_call(comm, name):
    n_in, n_out = len(comm.inputs), len(comm.out_shapes)

    def body(*refs):
        ins, outs, sems = refs[:n_in], refs[n_in:n_in + n_out], refs[n_in + n_out:]
        comm.begin(ins, outs, sems)
        if comm.middle is not None:
            comm.middle(ins, outs, sems)
        comm.finish(ins, outs, sems)

    any_spec = pl.BlockSpec(memory_space=pl.ANY)
    return _call(body, name=name, out_shape=tuple(comm.out_shapes), in_specs=[any_spec] * n_in,
                 out_specs=[any_spec] * n_out, scratch_shapes=comm.scratch)(*comm.inputs)


def _hosted(body, n_in, n_out, comm, first, last, middle):
    if comm is None:
        return lambda *refs: body(*refs)
    n_ci, n_co, n_cs = len(comm.inputs), len(comm.out_shapes), len(comm.scratch)

    def wrapped(*refs):
        ins, cin = refs[:n_in], refs[n_in:n_in + n_ci]
        o0 = n_in + n_ci
        outs, cout = refs[o0:o0 + n_out], refs[o0 + n_out:o0 + n_out + n_co]
        scr, csem = refs[o0 + n_out + n_co:len(refs) - n_cs], refs[len(refs) - n_cs:]
        pl.when(first())(lambda: comm.begin(cin, cout, csem))
        body(*ins, *outs, *scr)
        if comm.middle is not None:
            pl.when(middle())(lambda: comm.middle(cin, cout, csem))
        pl.when(last())(lambda: comm.finish(cin, cout, csem))

    return wrapped


def _hosted_call(body, comm, *, name, grid, out_shape, in_specs, out_specs, args, scratch_shapes=(), sem=None):
    nd = len(grid)
    first, last, middle = _at_first(nd), _at_last(nd), _at_middle(nd)
    if comm is not None:
        sem = ("arbitrary",) * nd
    n_in, n_out = len(in_specs), len(out_shape)
    any_spec = pl.BlockSpec(memory_space=pl.ANY)
    c_in = [] if comm is None else comm.inputs
    c_out = [] if comm is None else comm.out_shapes
    c_scr = [] if comm is None else comm.scratch
    outs = _call(
        _hosted(body, n_in, n_out, comm, first, last, middle), name=name, grid=grid,
        out_shape=(*out_shape, *c_out),
        in_specs=[*in_specs, *[any_spec] * len(c_in)],
        out_specs=(*out_specs, *[any_spec] * len(c_out)),
        scratch_shapes=[*scratch_shapes, *c_scr],
        compiler_params=_params(sem, VMEM_LIMIT),
    )(*args, *c_in)
    return outs[:n_out], outs[n_out:]


def _grid_step(ndim):
    i, n = pl.program_id(0), pl.num_programs(0)
    for d in range(1, ndim):
        i, n = i * pl.num_programs(d) + pl.program_id(d), n * pl.num_programs(d)
    return i, n


def _at_first(ndim):
    return lambda: _grid_step(ndim)[0] == 0


def _at_last(ndim):
    def pred():
        i, n = _grid_step(ndim)
        return i == n - 1
    return pred


def _at_middle(ndim):
    def pred():
        i, n = _grid_step(ndim)
        return i == (3 * n) // 4
    return pred


def _fwd_in(x, g, w_full, name, comm=None):
    s = x.shape[0]
    ts = min(ROW_TILE, s)

    def body(x_ref, g_ref, w_ref, h_ref, pc_ref, qkv_ref, az_ref):
        xf = x_ref[...]
        r = lax.rsqrt(jnp.mean(xf * xf, axis=-1, keepdims=True) + EPS)
        h = (xf * r * g_ref[...]).astype(BF16)
        h_ref[...] = h
        pc_ref[...] = jnp.dot(h, w_ref[:, 0:2048], preferred_element_type=F32).astype(BF16)
        q = jnp.dot(h, w_ref[:, 2048:2560], preferred_element_type=F32)
        qkv_ref[:, 0:512] = (q * 0.125).astype(BF16)
        qkv_ref[:, 512:1536] = jnp.dot(h, w_ref[:, 2560:3584], preferred_element_type=F32).astype(BF16)
        az_ref[...] = jnp.dot(h, w_ref[:, 3584:4096], preferred_element_type=F32).astype(BF16)

    row = lambda width: pl.BlockSpec((ts, width), lambda i: (i, 0))
    return _hosted_call(
        body, comm, name=name, grid=(s // ts,),
        out_shape=(jax.ShapeDtypeStruct((s, D_MODEL), BF16), jax.ShapeDtypeStruct((s, 2048), BF16),
                   jax.ShapeDtypeStruct((s, 1536), BF16), jax.ShapeDtypeStruct((s, 512), BF16)),
        in_specs=[row(D_MODEL), pl.BlockSpec((1, D_MODEL), lambda i: (0, 0)),
                  pl.BlockSpec((D_MODEL, N_IN), lambda i: (0, 0))],
        out_specs=(row(D_MODEL), row(2048), row(1536), row(512)),
        args=(x, g, w_full), sem=("parallel",))


ATTN_ROWS = 128
ATTN_DONE = 104.0


def _attn_pieces(tq, rc):
    lane = lax.broadcasted_iota(jnp.int32, (1, LANES), 1)
    lo = lane < HEAD_DIM
    row = lax.broadcasted_iota(jnp.int32, (tq, tq), 0)
    col = lax.broadcasted_iota(jnp.int32, (tq, tq), 1)
    tri_gt = jnp.where(row > col, 1.0, 0.0).astype(BF16)
    tri_le = jnp.where(row <= col, 1.0, 0.0).astype(BF16)
    rrow = lax.broadcasted_iota(jnp.int32, (rc, tq), 0)
    rcol = lax.broadcasted_iota(jnp.int32, (rc, tq), 1)
    causal = [rcol < rrow + r * rc for r in range(tq // rc)]
    return lo, causal, tri_gt, tri_le


def _split_heads(a, lo):
    z = jnp.zeros_like(a)
    return (jnp.where(lo, a, z), jnp.where(lo, z, a))


def _softplus(z, causal, diag):
    neg_abs = lax.bitcast_convert_type(lax.bitcast_convert_type(z, jnp.uint32) | jnp.uint32(0x80000000), F32)
    sp = jnp.maximum(z, 0.0) + jnp.log(1.0 + jnp.exp(neg_abs))
    if diag:
        sp = jnp.where(causal, sp, 0.0)
    return sp


def _attn_fwd(qkv, name, comm=None):
    s = qkv.shape[0]
    tq = min(ATTN_TILE, s)
    nq = s // tq
    rc = min(ATTN_ROWS, tq)
    n_rc = tq // rc
    chains = [(r, hh) for r in range(n_rc) for hh in range(2)]

    def body(q_ref, k_ref, v_ref, o_ref, lsum_ref, nblk_ref):
        hp, qi = pl.program_id(0), pl.program_id(1)
        lo, causal, tri_gt, _ = _attn_pieces(tq, rc)
        qh = _split_heads(q_ref[...], lo)
        qc = {(r, hh): qh[hh][r * rc:(r + 1) * rc] for r, hh in chains}

        mm = lambda a_, b_: jnp.dot(a_.astype(BF16), b_, preferred_element_type=F32)
        rowsum = lambda a_: jnp.sum(a_, axis=-1, keepdims=True)

        def block(kb, carry):
            start = pl.multiple_of(kb * tq, tq)
            k = k_ref[pl.ds(start, tq), :]
            vh = _split_heads(v_ref[pl.ds(start, tq), :], lo)
            z = {ch: lax.dot_general(qc[ch], k, NT, preferred_element_type=F32) for ch in chains}
            sp = {ch: _softplus(z[ch], None, False) for ch in chains}
            later = {ch: mm(sp[ch], tri_gt) for ch in chains}
            a = {ch: jnp.exp((z[ch] - sp[ch]) - (carry[ch[0]][1 + ch[1]] + later[ch])) for ch in chains}
            pv = {ch: mm(a[ch], vh[ch[1]]) for ch in chains}
            return tuple((carry[r][0] + pv[(r, 0)] + pv[(r, 1)],
                          carry[r][1] + rowsum(sp[(r, 0)]), carry[r][2] + rowsum(sp[(r, 1)])) for r in range(n_rc))

        def first_two(prev_ok):
            d0 = pl.multiple_of(qi * tq, tq)
            p0 = pl.multiple_of(jnp.maximum(qi - 1, 0) * tq, tq)
            k_d, k_p = k_ref[pl.ds(d0, tq), :], k_ref[pl.ds(p0, tq), :]
            vh_d = _split_heads(v_ref[pl.ds(d0, tq), :], lo)
            vh_p = _split_heads(v_ref[pl.ds(p0, tq), :], lo)
            z_d = {ch: lax.dot_general(qc[ch], k_d, NT, preferred_element_type=F32) for ch in chains}
            z_p = {ch: lax.dot_general(qc[ch], k_p, NT, preferred_element_type=F32) for ch in chains}
            sp_d = {ch: _softplus(z_d[ch], causal[ch[0]], True) for ch in chains}
            sp_raw = {ch: _softplus(z_p[ch], None, False) for ch in chains}
            sp_p = {ch: jnp.where(prev_ok, sp_raw[ch], 0.0) for ch in chains}
            later_d = {ch: mm(sp_d[ch], tri_gt) for ch in chains}
            later_p = {ch: mm(sp_p[ch], tri_gt) for ch in chains}
            c_d = {ch: rowsum(sp_d[ch]) for ch in chains}
            a_d = {ch: jnp.where(causal[ch[0]], jnp.exp((z_d[ch] - sp_d[ch]) - later_d[ch]), 0.0) for ch in chains}
            a_p = {ch: jnp.where(prev_ok, jnp.exp((z_p[ch] - sp_raw[ch]) - (c_d[ch] + later_p[ch])), 0.0)
                   for ch in chains}
            pv = {ch: mm(a_d[ch], vh_d[ch[1]]) + mm(a_p[ch], vh_p[ch[1]]) for ch in chains}
            return tuple((pv[(r, 0)] + pv[(r, 1)],
                          c_d[(r, 0)] + rowsum(sp_p[(r, 0)]), c_d[(r, 1)] + rowsum(sp_p[(r, 1)]))
                         for r in range(n_rc))

        def least(carry):
            m = jnp.minimum(carry[0][1], carry[0][2])
            for r in range(1, n_rc):
                m = jnp.minimum(m, jnp.minimum(carry[r][1], carry[r][2]))
            return jnp.min(m)

        carry = first_two(qi > 0)

        def go_on(st):
            return jnp.logical_and(st[0] < qi - 1, st[1] < ATTN_DONE)

        def step(st):
            new = block(qi - 2 - st[0], st[2])
            return st[0] + 1, least(new), new

        walked, _, carry = lax.while_loop(go_on, step, (jnp.int32(0), least(carry), carry))
        for r in range(n_rc):
            o_ref[r * rc:(r + 1) * rc, :] = carry[r][0].astype(BF16)
            lsum_ref[r * rc:(r + 1) * rc, :] = jnp.where(lo, carry[r][1], carry[r][2])
        nblk_ref[hp, qi] = walked.astype(F32)

    blk = pl.BlockSpec((tq, LANES), lambda hp, qi: (qi, hp))
    o512 = jax.ShapeDtypeStruct((s, D_SB), F32)
    return _hosted_call(
        body, comm, name=name, grid=(4, nq),
        out_shape=(jax.ShapeDtypeStruct((s, D_SB), BF16), o512, jax.ShapeDtypeStruct((4, nq), F32)),
        in_specs=[blk, pl.BlockSpec((s, LANES), lambda hp, qi: (0, 4 + hp)),
                  pl.BlockSpec((s, LANES), lambda hp, qi: (0, 8 + hp))],
        out_specs=(blk, blk, pl.BlockSpec(memory_space=pltpu.SMEM)),
        args=(qkv, qkv, qkv), sem=("arbitrary", "arbitrary"))


HALO = 16


def _conv_taps(cc_ref, ch_ref, ccp_ref, chp_ref, halo_ref, first):
    u = cc_ref[...].astype(F32) * ch_ref[...].astype(F32)
    halo_ref[...] = ccp_ref[...].astype(F32) * chp_ref[...].astype(F32) * jnp.where(first, 0.0, 1.0)
    p6 = halo_ref[HALO - 2:HALO - 1, :]
    p7 = halo_ref[HALO - 1:HALO, :]
    rowi = lax.broadcasted_iota(jnp.int32, u.shape, 0)
    u1 = jnp.where(rowi == 0, p7, pltpu.roll(u, 1, 0))
    u2 = jnp.where(rowi == 0, p6, jnp.where(rowi == 1, p7, pltpu.roll(u, 2, 0)))
    return u, u1, u2


def _fwd_mid(x, pc, az, ya, p4, layer, cw, cb, bg, wout_full, pg, wpg_full, bpg, wpe_full, name):
    s = x.shape[0]
    ts = min(ROW_TILE, s)
    blk_h = ts // HALO

    def body(x_ref, cb_ref_, cc_ref, ch_ref, cz_ref, ccp_ref, chp_ref, az_ref, ya_ref, p_ref,
             cw_ref, cbias_ref, bg_ref, wout_ref, pg_ref, wpg_ref, bpg_ref, wpe_ref,
             x2_ref, x3_ref, gated_ref, h2_ref, gate_ref, e_ref, halo_ref):
        i = pl.program_id(0)
        lane = lax.broadcasted_iota(jnp.int32, (1, LANES), 1)
        lo = lane < HEAD_DIM
        u, u1, u2 = _conv_taps(cc_ref, ch_ref, ccp_ref, chp_ref, halo_ref, i == 0)
        conv = cbias_ref[...] + cw_ref[0:1, :] * u2 + cw_ref[1:2, :] * u1 + cw_ref[2:3, :] * u
        yc = cb_ref_[...].astype(F32) * conv
        for sl in range(8):
            cols = slice(LANES * (sl % 4), LANES * (sl % 4 + 1))
            y = yc[:, cols] if sl < 4 else ya_ref[:, cols].astype(F32)
            zc = (cz_ref[:, cols] if sl < 4 else az_ref[:, cols]).astype(F32)
            rg = lax.rsqrt(_group_bcast_sum(y * y, lo) * (1.0 / HEAD_DIM) + EPS)
            yn = y * rg * bg_ref[:, LANES * sl:LANES * (sl + 1)]
            gated_ref[:, LANES * sl:LANES * (sl + 1)] = (yn * (zc * _sigmoid(zc))).astype(BF16)
        x2 = x_ref[...] + jnp.dot(gated_ref[...], wout_ref[...], preferred_element_type=F32)
        x2_ref[...] = x2
        r2 = lax.rsqrt(jnp.mean(x2 * x2, axis=-1, keepdims=True) + EPS)
        h2 = (x2 * r2 * pg_ref[...]).astype(BF16)
        h2_ref[...] = h2
        gate = _sigmoid(jnp.dot(h2, wpg_ref[...], preferred_element_type=F32) + bpg_ref[...])
        gate_ref[...] = gate.astype(BF16)
        e = jnp.dot(p_ref[...].astype(BF16), wpe_ref[...], preferred_element_type=F32)
        e_ref[...] = e.astype(BF16)
        x3_ref[...] = x2 + gate * e

    row = lambda width, cb_=0: pl.BlockSpec((ts, width), lambda i: (i, cb_))
    prev = lambda cb_: pl.BlockSpec((HALO, 512), lambda i: (jnp.maximum(i * blk_h - 1, 0), cb_))
    vec = lambda width: pl.BlockSpec((1, width), lambda i: (0, 0))
    wspec = lambda r_, c_: pl.BlockSpec((r_, c_), lambda i: (0, 0))
    f32o = jax.ShapeDtypeStruct((s, D_MODEL), F32)
    bfo = jax.ShapeDtypeStruct((s, D_MODEL), BF16)
    return _call(
        body, name=name, grid=(s // ts,),
        out_shape=(f32o, f32o, bfo, bfo, bfo, bfo),
        scratch_shapes=[pltpu.VMEM((HALO, 512), F32)],
        in_specs=[row(D_MODEL), row(512, 0), row(512, 1), row(512, 2), row(512, 3), prev(1), prev(2),
                  row(512), row(512),
                  pl.BlockSpec((None, None, ts, PLE_DIM), lambda i: (layer, 0, i, 0)),
                  pl.BlockSpec((3, 512), lambda i: (0, 0)), vec(512), vec(D_MODEL),
                  wspec(D_MODEL, D_MODEL), vec(D_MODEL), wspec(D_MODEL, D_MODEL), vec(D_MODEL),
                  wspec(PLE_DIM, D_MODEL)],
        out_specs=(row(D_MODEL),) * 6,
        compiler_params=_params(("parallel",), VMEM_LIMIT),
    )(x, pc, pc, pc, pc, pc, pc, az, ya, p4, cw, cb, bg, wout_full, pg, wpg_full, bpg, wpe_full)


def _loss_head(xf, target, fg):
    s = xf.shape[0]
    ts = min(ROW_TILE, s)

    def body(x_ref, t_ref, g_ref, dx_ref, loss_ref, dg_ref):
        i = pl.program_id(0)

        @pl.when(i == 0)
        def _():
            loss_ref[...] = jnp.zeros_like(loss_ref)
            dg_ref[...] = jnp.zeros_like(dg_ref)

        x = x_ref[...]
        g = g_ref[...]
        r = lax.rsqrt(jnp.mean(x * x, axis=-1, keepdims=True) + EPS)
        xn = x * r
        err = xn * g - t_ref[...]
        per_row = jnp.sum(err * err, axis=-1, keepdims=True)
        loss_ref[...] += jnp.sum(per_row, axis=0, keepdims=True) * (0.5 / D_MODEL)
        dy = err * (1.0 / D_MODEL)
        dg_ref[...] += jnp.sum(dy * xn, axis=0, keepdims=True)
        dxn = dy * g
        dx_ref[...] = r * (dxn - xn * jnp.mean(dxn * xn, axis=-1, keepdims=True))

    row = pl.BlockSpec((ts, D_MODEL), lambda i: (i, 0))
    return _call(
        body, name="loss_head", grid=(s // ts,),
        out_shape=(jax.ShapeDtypeStruct((s, D_MODEL), F32), jax.ShapeDtypeStruct((1, LANES), F32),
                   jax.ShapeDtypeStruct((1, D_MODEL), F32)),
        in_specs=[row, row, pl.BlockSpec((1, D_MODEL), lambda i: (0, 0))],
        out_specs=(row, pl.BlockSpec((1, LANES), lambda i: (0, 0)), pl.BlockSpec((1, D_MODEL), lambda i: (0, 0))),
        compiler_params=_params(("arbitrary",), VMEM_LIMIT),
    )(xf, target, fg)


def _bwd_mid(dx3, x2, gate, e, pc, az, ya, gated, h2, p4, layer, cw, cb, bg, pg, wpg_full, wout_full, name):
    s = x2.shape[0]
    ts = min(ROW_TILE, s)
    blk_h = ts // HALO

    def body(dx3_ref, x2_ref, gate_ref, e_ref, cb_ref_, cc_ref, ch_ref, cz_ref, ccp_ref, chp_ref, az_ref, ya_ref,
             gated_ref, h2_ref, p_ref, cw_ref, cbias_ref, bg_ref, pg_ref, wpg_ref, wout_ref,
             dx2_ref, dya_ref, dmisc_ref, dconv_ref, dwout_ref, dwpg_ref, dwpe_ref,
             dbpg_ref, dpg_ref, dbg_ref, dcbias_ref, dcw_ref,
             dgated_ref, halo_ref, acc_out, acc_pg, acc_pe):
        i = pl.program_id(0)

        @pl.when(i == 0)
        def _():
            for ref in (dbpg_ref, dpg_ref, dbg_ref, dcbias_ref, dcw_ref, acc_out, acc_pg, acc_pe):
                ref[...] = jnp.zeros_like(ref)

        lane = lax.broadcasted_iota(jnp.int32, (1, LANES), 1)
        lo = lane < HEAD_DIM
        dx3 = dx3_ref[...]
        gate = gate_ref[...].astype(F32)
        de_b = (dx3 * gate).astype(BF16)
        acc_pe[...] += lax.dot_general(p_ref[...].astype(BF16), de_b, TN, preferred_element_type=F32)
        dgpre = dx3 * e_ref[...].astype(F32) * gate * (1.0 - gate)
        dbpg_ref[...] += jnp.sum(dgpre, axis=0, keepdims=True)
        dgpre_b = dgpre.astype(BF16)
        acc_pg[...] += lax.dot_general(h2_ref[...], dgpre_b, TN, preferred_element_type=F32)
        dh2 = lax.dot_general(dgpre_b, wpg_ref[...], NT, preferred_element_type=F32)
        x2 = x2_ref[...]
        r2 = lax.rsqrt(jnp.mean(x2 * x2, axis=-1, keepdims=True) + EPS)
        xn2 = x2 * r2
        dpg_ref[...] += jnp.sum(dh2 * xn2, axis=0, keepdims=True)
        dxn = dh2 * pg_ref[...]
        dx2 = dx3 + r2 * (dxn - xn2 * jnp.mean(dxn * xn2, axis=-1, keepdims=True))
        dx2_ref[...] = dx2
        dx2_b = dx2.astype(BF16)
        acc_out[...] += lax.dot_general(gated_ref[...], dx2_b, TN, preferred_element_type=F32)
        dgated_ref[...] = lax.dot_general(dx2_b, wout_ref[...], NT, preferred_element_type=F32)

        @pl.when(i == pl.num_programs(0) - 1)
        def _():
            dwout_ref[...] = acc_out[...].astype(BF16)
            dwpg_ref[...] = acc_pg[...].astype(BF16)
            dwpe_ref[...] = acc_pe[...].astype(BF16)

        u, u1, u2 = _conv_taps(cc_ref, ch_ref, ccp_ref, chp_ref, halo_ref, i == 0)
        conv = cbias_ref[...] + cw_ref[0:1, :] * u2 + cw_ref[1:2, :] * u1 + cw_ref[2:3, :] * u
        c_b = cb_ref_[...].astype(F32)
        yc = c_b * conv
        for sl in range(8):
            cols = slice(LANES * (sl % 4), LANES * (sl % 4 + 1))
            wide = slice(LANES * sl, LANES * (sl + 1))
            y = yc[:, cols] if sl < 4 else ya_ref[:, cols].astype(F32)
            zc = (cz_ref[:, cols] if sl < 4 else az_ref[:, cols]).astype(F32)
            bgs = bg_ref[:, wide]
            dgt = dgated_ref[:, wide]
            rg = lax.rsqrt(_group_bcast_sum(y * y, lo) * (1.0 / HEAD_DIM) + EPS)
            yhat = y * rg
            sig = _sigmoid(zc)
            dyn = dgt * (zc * sig)
            dzc = dgt * (yhat * bgs) * (sig * (1.0 + zc * (1.0 - sig)))
            dbg_ref[:, wide] += jnp.sum(dyn * yhat, axis=0, keepdims=True)
            dyh = dyn * bgs
            dy = rg * (dyh - yhat * (_group_bcast_sum(dyh * yhat, lo) * (1.0 / HEAD_DIM)))
            if sl < 4:
                dconv = dy * c_b[:, cols]
                dmisc_ref[:, cols] = (dy * conv[:, cols]).astype(BF16)
                dmisc_ref[:, 512 + LANES * sl:512 + LANES * (sl + 1)] = dzc.astype(BF16)
                dconv_ref[:, cols] = dconv
                dcbias_ref[:, cols] += jnp.sum(dconv, axis=0, keepdims=True)
                dcw_ref[0:1, cols] += jnp.sum(dconv * u2[:, cols], axis=0, keepdims=True)
                dcw_ref[1:2, cols] += jnp.sum(dconv * u1[:, cols], axis=0, keepdims=True)
                dcw_ref[2:3, cols] += jnp.sum(dconv * u[:, cols], axis=0, keepdims=True)
            else:
                dya_ref[:, cols] = dy.astype(BF16)
                dmisc_ref[:, 1024 + LANES * (sl - 4):1024 + LANES * (sl - 3)] = dzc.astype(BF16)

    row = lambda width, cb_=0: pl.BlockSpec((ts, width), lambda i: (i, cb_))
    prev = lambda cb_: pl.BlockSpec((HALO, 512), lambda i: (jnp.maximum(i * blk_h - 1, 0), cb_))
    vec = lambda width: pl.BlockSpec((1, width), lambda i: (0, 0))
    wspec = lambda r_, c_: pl.BlockSpec((r_, c_), lambda i: (0, 0))
    vo = lambda width: jax.ShapeDtypeStruct((1, width), F32)
    sq = jax.ShapeDtypeStruct((D_MODEL, D_MODEL), BF16)
    return _call(
        body, name=name, grid=(s // ts,),
        out_shape=(jax.ShapeDtypeStruct((s, D_MODEL), F32), jax.ShapeDtypeStruct((s, 512), BF16),
                   jax.ShapeDtypeStruct((s, 1536), BF16), jax.ShapeDtypeStruct((s, 512), F32),
                   sq, sq, jax.ShapeDtypeStruct((PLE_DIM, D_MODEL), BF16),
                   vo(D_MODEL), vo(D_MODEL), vo(D_MODEL), vo(512), jax.ShapeDtypeStruct((SUBLANES, 512), F32)),
        in_specs=[row(D_MODEL), row(D_MODEL), row(D_MODEL), row(D_MODEL),
                  row(512, 0), row(512, 1), row(512, 2), row(512, 3), prev(1), prev(2), row(512), row(512),
                  row(D_MODEL), row(D_MODEL),
                  pl.BlockSpec((None, None, ts, PLE_DIM), lambda i: (layer, 0, i, 0)),
                  pl.BlockSpec((3, 512), lambda i: (0, 0)), vec(512), vec(D_MODEL), vec(D_MODEL),
                  wspec(D_MODEL, D_MODEL), wspec(D_MODEL, D_MODEL)],
        out_specs=(row(D_MODEL), row(512), row(1536), row(512),
                   wspec(D_MODEL, D_MODEL), wspec(D_MODEL, D_MODEL), wspec(PLE_DIM, D_MODEL),
                   vec(D_MODEL), vec(D_MODEL), vec(D_MODEL), vec(512),
                   pl.BlockSpec((SUBLANES, 512), lambda i: (0, 0))),
        scratch_shapes=[pltpu.VMEM((ts, D_MODEL), F32), pltpu.VMEM((HALO, 512), F32),
                        pltpu.VMEM((D_MODEL, D_MODEL), F32), pltpu.VMEM((D_MODEL, D_MODEL), F32),
                        pltpu.VMEM((PLE_DIM, D_MODEL), F32)],
        compiler_params=_params(("arbitrary",), VMEM_LIMIT),
    )(dx3, x2, gate, e, pc, pc, pc, pc, pc, pc, az, ya, gated, h2, p4, cw, cb, bg, pg, wpg_full, wout_full)


def _attn_bwd(qkv, lsum, nblk, dya, name, comm=None):
    s = qkv.shape[0]
    tq = min(ATTN_TILE, s)
    nq = s // tq
    rc = min(ATTN_ROWS, tq)
    n_rc = tq // rc
    chains = [(r, hh) for r in range(n_rc) for hh in range(2)]

    def body(nblk_ref, q_ref, k_ref, v_ref, lsum_ref, do_ref, dq_ref, dk_ref, dv_ref, dk_acc, dv_acc):
        hp, qi = pl.program_id(0), pl.program_id(1)

        @pl.when(qi == 0)
        def _():
            dk_acc[...] = jnp.zeros_like(dk_acc)
            dv_acc[...] = jnp.zeros_like(dv_acc)

        lo, causal, tri_gt, tri_le = _attn_pieces(tq, rc)
        lane = lax.broadcasted_iota(jnp.int32, (1, LANES), 1)
        qh = _split_heads(q_ref[...], lo)
        doh = _split_heads(do_ref[...].astype(BF16), lo)
        lt = lsum_ref[...]
        ltot_h = (jnp.sum(jnp.where(lane == 0, lt, 0.0), axis=-1, keepdims=True),
                  jnp.sum(jnp.where(lane == HEAD_DIM, lt, 0.0), axis=-1, keepdims=True))
        rows = lambda a_, r: a_[r * rc:(r + 1) * rc]
        qc = {(r, hh): rows(qh[hh], r) for r, hh in chains}
        doc = {(r, hh): rows(doh[hh], r) for r, hh in chains}
        ltot = {(r, hh): rows(ltot_h[hh], r) for r, hh in chains}

        mm = lambda a_, b_: jnp.dot(a_.astype(BF16), b_, preferred_element_type=F32)
        mm_nt = lambda a_, b_: lax.dot_general(a_, b_, NT, preferred_element_type=F32)
        mm_tn = lambda a_, b_: lax.dot_general(a_.astype(BF16), b_, TN, preferred_element_type=F32)
        rowsum = lambda a_: jnp.sum(a_, axis=-1, keepdims=True)

        def block(kb, carry, diag=False):
            start = pl.multiple_of(kb * tq, tq)
            k = k_ref[pl.ds(start, tq), :]
            v = v_ref[pl.ds(start, tq), :]
            kh = _split_heads(k, lo)
            keep = (lambda ch, a_: jnp.where(causal[ch[0]], a_, 0.0)) if diag else (lambda ch, a_: a_)
            z = {ch: mm_nt(qc[ch], k) for ch in chains}
            da = {ch: mm_nt(doc[ch], v) for ch in chains}
            sp = {ch: _softplus(z[ch], causal[ch[0]], diag) for ch in chains}
            later = {ch: mm(sp[ch], tri_gt) for ch in chains}
            walked = {ch: carry[ch[0]][1 + ch[1]] + rowsum(sp[ch]) for ch in chains}
            a = {ch: keep(ch, jnp.exp((z[ch] - sp[ch]) - ((ltot[ch] - walked[ch]) + later[ch]))) for ch in chains}
            g = {ch: a[ch] * da[ch] for ch in chains}
            upto = {ch: mm(g[ch], tri_le) for ch in chains}
            dz = {ch: keep(ch, g[ch] - jnp.exp(z[ch] - sp[ch]) * (carry[ch[0]][3 + ch[1]] + upto[ch])).astype(BF16)
                  for ch in chains}
            dqc = {ch: mm(dz[ch], kh[ch[1]]) for ch in chains}
            dkc = [mm_tn(dz[ch], qc[ch]) for ch in chains]
            dvc = [mm_tn(a[ch], doc[ch]) for ch in chains]
            dk_acc[pl.ds(start, tq), :] += sum(dkc[1:], dkc[0])
            dv_acc[pl.ds(start, tq), :] += sum(dvc[1:], dvc[0])
            return tuple((carry[r][0] + dqc[(r, 0)] + dqc[(r, 1)], walked[(r, 0)], walked[(r, 1)],
                          carry[r][3] + rowsum(g[(r, 0)]), carry[r][4] + rowsum(g[(r, 1)])) for r in range(n_rc))

        zc = jnp.zeros((rc, 1), F32)
        carry = tuple((jnp.zeros((rc, LANES), F32), zc, zc, zc, zc) for _ in range(n_rc))
        near = jnp.maximum(qi - 1, 0)
        first = near - jnp.clip(nblk_ref[hp, qi].astype(jnp.int32), 0, near)
        carry = lax.fori_loop(first, qi, block, carry)
        carry = block(qi, carry, True)
        for r in range(n_rc):
            dq_ref[r * rc:(r + 1) * rc, :] = (carry[r][0] * 0.125).astype(BF16)

        @pl.when(qi == pl.num_programs(1) - 1)
        def _():
            dk_ref[...] = dk_acc[...].astype(BF16)
            dv_ref[...] = dv_acc[...].astype(BF16)

    blk = pl.BlockSpec((tq, LANES), lambda hp, qi: (qi, hp))
    col = pl.BlockSpec((s, LANES), lambda hp, qi: (0, hp))
    o512 = jax.ShapeDtypeStruct((s, D_SB), BF16)
    return _hosted_call(
        body, comm, name=name, grid=(4, nq),
        out_shape=(o512, o512, o512),
        in_specs=[pl.BlockSpec(memory_space=pltpu.SMEM), blk,
                  pl.BlockSpec((s, LANES), lambda hp, qi: (0, 4 + hp)),
                  pl.BlockSpec((s, LANES), lambda hp, qi: (0, 8 + hp)), blk, blk],
        out_specs=(blk, col, col),
        scratch_shapes=[pltpu.VMEM((s, LANES), F32), pltpu.VMEM((s, LANES), F32)],
        args=(nblk, qkv, qkv, qkv, lsum, dya), sem=("parallel", "arbitrary"))


def _bwd_dproj(dmisc, dconv, pc, dq, dk, dv, x, dx2, g, cw, win_full, name, comm=None):
    s = x.shape[0]
    ts = min(ROW_TILE, s)
    blk8 = ts // SUBLANES
    last8 = s // SUBLANES - 1

    def body(dcb_ref, dcz_ref, daz_ref, dconv_ref, nxt_ref, cc_ref, ch_ref, dq_ref, dk_ref, dv_ref,
             x_ref, dx2_ref, g_ref, cw_ref, w_ref, dproj_ref, dx_ref, dg_ref):
        i = pl.program_id(0)

        @pl.when(i == 0)
        def _():
            dg_ref[...] = jnp.zeros_like(dg_ref)

        keep = jnp.where(i == pl.num_programs(0) - 1, 0.0, 1.0)
        dc = dconv_ref[...]
        n0 = nxt_ref[0:1, :] * keep
        n1 = nxt_ref[1:2, :] * keep
        rowi = lax.broadcasted_iota(jnp.int32, dc.shape, 0)
        dc1 = jnp.where(rowi == ts - 1, n0, pltpu.roll(dc, ts - 1, 0))
        dc2 = jnp.where(rowi == ts - 2, n0, jnp.where(rowi == ts - 1, n1, pltpu.roll(dc, ts - 2, 0)))
        du = cw_ref[2:3, :] * dc + cw_ref[1:2, :] * dc1 + cw_ref[0:1, :] * dc2
        dproj_ref[:, 0:512] = dcb_ref[...]
        dproj_ref[:, 512:1024] = (du * ch_ref[...].astype(F32)).astype(BF16)
        dproj_ref[:, 1024:1536] = (du * cc_ref[...].astype(F32)).astype(BF16)
        dproj_ref[:, 1536:2048] = dcz_ref[...]
        dproj_ref[:, 2048:2560] = dq_ref[...]
        dproj_ref[:, 2560:3072] = dk_ref[...]
        dproj_ref[:, 3072:3584] = dv_ref[...]
        dproj_ref[:, 3584:4096] = daz_ref[...]
        dh = lax.dot_general(dproj_ref[...], w_ref[...], NT, preferred_element_type=F32)
        x = x_ref[...]
        r = lax.rsqrt(jnp.mean(x * x, axis=-1, keepdims=True) + EPS)
        xn = x * r
        dg_ref[...] += jnp.sum(dh * xn, axis=0, keepdims=True)
        dxn = dh * g_ref[...]
        dx_ref[...] = dx2_ref[...] + r * (dxn - xn * jnp.mean(dxn * xn, axis=-1, keepdims=True))

    row = lambda width, cb_=0: pl.BlockSpec((ts, width), lambda i: (i, cb_))
    nxt = pl.BlockSpec((SUBLANES, 512), lambda i: (jnp.minimum((i + 1) * blk8, last8), 0))
    vec = lambda width: pl.BlockSpec((1, width), lambda i: (0, 0))
    return _hosted_call(
        body, comm, name=name, grid=(s // ts,),
        out_shape=(jax.ShapeDtypeStruct((s, N_IN), BF16), jax.ShapeDtypeStruct((s, D_MODEL), F32),
                   jax.ShapeDtypeStruct((1, D_MODEL), F32)),
        in_specs=[row(512, 0), row(512, 1), row(512, 2), row(512), nxt, row(512, 1), row(512, 2),
                  row(512), row(512), row(512), row(D_MODEL), row(D_MODEL), vec(D_MODEL),
                  pl.BlockSpec((3, 512), lambda i: (0, 0)),
                  pl.BlockSpec((D_MODEL, N_IN), lambda i: (0, 0))],
        out_specs=(row(N_IN), row(D_MODEL), vec(D_MODEL)),
        args=(dmisc, dmisc, dmisc, dconv, dconv, pc, pc, dq, dk, dv, x, dx2, g, cw, win_full),
        sem=("arbitrary",))


def _atb(a, b, name, a_index=None, comm=None):
    s, n = b.shape
    m = a.shape[-1]
    ts = min(512, s)
    tn = min(1024, n)
    if a_index is None:
        a_spec = pl.BlockSpec((ts, m), lambda j, i: (i, 0))
    else:
        a_spec = pl.BlockSpec((None, None, ts, m), lambda j, i: (a_index, 0, i, 0))

    def body(a_ref, b_ref, o_ref, acc_ref):
        i = pl.program_id(1)

        @pl.when(i == 0)
        def _():
            acc_ref[...] = jnp.zeros_like(acc_ref)

        acc_ref[...] += lax.dot_general(a_ref[...].astype(BF16), b_ref[...], TN, preferred_element_type=F32)

        @pl.when(i == pl.num_programs(1) - 1)
        def _():
            o_ref[...] = acc_ref[...].astype(BF16)

    (out,), got = _hosted_call(
        body, comm, name=name, grid=(n // tn, s // ts),
        out_shape=(jax.ShapeDtypeStruct((m, n), BF16),),
        in_specs=[a_spec, pl.BlockSpec((ts, tn), lambda j, i: (i, j))],
        out_specs=(pl.BlockSpec((m, tn), lambda j, i: (0, j)),),
        scratch_shapes=[pltpu.VMEM((m, tn), F32)],
        args=(a, b), sem=("parallel", "arbitrary"))
    return out, got


def _adamw_math(w, g, m, v):
    m2 = ADAM_B1 * m + (1.0 - ADAM_B1) * g
    v2 = ADAM_B2 * v + (1.0 - ADAM_B2) * (g * g)
    m_hat = m2 / (1.0 - ADAM_B1 ** ADAM_STEP)
    v_hat = v2 / (1.0 - ADAM_B2 ** ADAM_STEP)
    delta = -ADAM_LR * (m_hat / (jnp.sqrt(v_hat) + ADAM_EPS) + ADAM_WD * w)
    return delta, m2, v2


def _adamw_sum8(parts, w, m, v, name):
    _, rows, cols = w.shape
    tr = min(rows, 256)
    n_tiles = rows // tr
    assert len(parts) == DEPTH == 2

    def body(p0_ref, p1_ref, w_ref, m_ref, v_ref, g_ref, d_ref, m2_ref, v2_ref):
        def run(p_ref):
            g = p_ref[0].astype(F32)
            for d in range(1, N_DEV):
                g = g + p_ref[d].astype(F32)
            g_ref[...] = g
            d_ref[...], m2_ref[...], v2_ref[...] = _adamw_math(w_ref[...], g, m_ref[...], v_ref[...])

        pl.when(pl.program_id(0) == 0)(lambda: run(p0_ref))
        pl.when(pl.program_id(0) == 1)(lambda: run(p1_ref))

    part0 = pl.BlockSpec((N_DEV, tr, cols), lambda l, i: (0, jnp.where(l == 0, i, n_tiles - 1), 0))
    part1 = pl.BlockSpec((N_DEV, tr, cols), lambda l, i: (0, jnp.where(l == 1, i, 0), 0))
    tile = pl.BlockSpec((None, tr, cols), lambda l, i: (l, i, 0))
    o = jax.ShapeDtypeStruct((DEPTH, rows, cols), F32)
    return _call(
        body, name=name, grid=(DEPTH, n_tiles),
        out_shape=(o, o, o, o),
        in_specs=[part0, part1, tile, tile, tile],
        out_specs=(tile, tile, tile, tile),
        compiler_params=_params(("arbitrary", "arbitrary"), VMEM_LIMIT),
    )(parts[0], parts[1], w, m, v)


def _adamw_plain(g, w, m, v, name):
    rows, cols = g.shape

    def body(g_ref, w_ref, m_ref, v_ref, d_ref, m2_ref, v2_ref):
        d_ref[...], m2_ref[...], v2_ref[...] = _adamw_math(w_ref[...], g_ref[...], m_ref[...], v_ref[...])

    full = pl.BlockSpec((rows, cols), lambda: (0, 0))
    o = jax.ShapeDtypeStruct((rows, cols), F32)
    return _call(body, name=name, out_shape=(o, o, o), in_specs=[full] * 4, out_specs=(full,) * 3)(g, w, m, v)


def _sum8_small(parts):
    def body(p_ref, g_ref):
        g = p_ref[0]
        for d in range(1, N_DEV):
            g = g + p_ref[d]
        g_ref[...] = g

    return _call(
        body, name="sum_small_grads",
        out_shape=jax.ShapeDtypeStruct((SMALL_ROWS, LANES), F32),
        in_specs=[pl.BlockSpec((N_DEV, SMALL_ROWS, LANES), lambda: (0, 0, 0))],
        out_specs=pl.BlockSpec((SMALL_ROWS, LANES), lambda: (0, 0)),
    )(parts)


def kernel(x, p, norm_g, w_in, conv_w, conv_b, branch_g, w_out, ple_norm_g, w_pg, b_pg, w_pe, final_g, loss_target, m_norm_g, m_w_in, m_conv_w, m_conv_b, m_branch_g, m_w_out, m_ple_norm_g, m_w_pg, m_b_pg, m_w_pe, m_final_g, v_norm_g, v_w_in, v_conv_w, v_conv_b, v_branch_g, v_w_out, v_ple_norm_g, v_w_pg, v_b_pg, v_w_pe, v_final_g):
    s = x.shape[1]
    x0 = x.reshape(s, D_MODEL)
    target = loss_target.reshape(s, D_MODEL)
    me_blk = _my_block()

    win_s = _cast_bf16(w_in.reshape(DEPTH * D_MODEL, 512), "cast_w_in").reshape(DEPTH, D_MODEL, 512)
    wout_s = _cast_bf16(w_out.reshape(DEPTH * 128, D_MODEL), "cast_w_out").reshape(DEPTH, 128, D_MODEL)
    wpg_s = _cast_bf16(w_pg.reshape(DEPTH * 128, D_MODEL), "cast_w_pg").reshape(DEPTH, 128, D_MODEL)
    wpe_s = _cast_bf16(w_pe.reshape(DEPTH * PLE_DIM, 128), "cast_w_pe").reshape(DEPTH, PLE_DIM, 128)
    cw_s = jnp.zeros((SUBLANES, LANES), F32).at[:DEPTH * 3, :HEAD_DIM].set(conv_w.reshape(DEPTH * 3, HEAD_DIM))
    bf = lambda r_, c_: jax.ShapeDtypeStruct((r_, c_), BF16)
    w_items = lambda l: [(wout_s[l], bf(D_MODEL, D_MODEL), "rows128"), (wpg_s[l], bf(D_MODEL, D_MODEL), "rows128"),
                         (wpe_s[l], bf(PLE_DIM, D_MODEL), "cols128")]
    win_f = [None] * DEPTH
    win_f[0], cw_all = _comm_call(_gather_comm([
        (win_s[0], bf(D_MODEL, N_IN), "cols512"),
        (cw_s, jax.ShapeDtypeStruct((N_DEV, SUBLANES, LANES), F32), "slot")]), "gather_w_in_0")
    cw_full = jnp.transpose(cw_all[:, :DEPTH * 3, :HEAD_DIM].reshape(N_DEV, DEPTH, 3, HEAD_DIM), (1, 2, 0, 3))
    cw_full = cw_full.reshape(DEPTH, 3, D_CONV)
    gather_rest_0 = _gather_comm(w_items(0))
    gather_1 = _gather_comm([(win_s[1], bf(D_MODEL, N_IN), "cols512"), *w_items(1)])

    vec = lambda a, l: a[l][None, :]

    saved = []
    xl = x0
    wout_f, wpg_f, wpe_f = [None] * DEPTH, [None] * DEPTH, [None] * DEPTH
    for l in range(DEPTH):
        (h, pc, qkv, az), got = _fwd_in(xl, vec(norm_g, l), win_f[l], f"fwd_in_{l}",
                                        comm=gather_rest_0 if l == 0 else None)
        if l == 0:
            wout_f[0], wpg_f[0], wpe_f[0] = got
        (ya, lsum, nblk), got = _attn_fwd(qkv, f"attn_fwd_{l}", comm=gather_1 if l == 0 else None)
        if l == 0:
            win_f[1], wout_f[1], wpg_f[1], wpe_f[1] = got
        x2, x3, gated, h2, gate, e = _fwd_mid(
            xl, pc, az, ya, p, l, cw_full[l], vec(conv_b, l), vec(branch_g, l), wout_f[l],
            vec(ple_norm_g, l), wpg_f[l], vec(b_pg, l), wpe_f[l], f"fwd_mid_{l}")
        saved.append(dict(x=xl, h=h, pc=pc, qkv=qkv, az=az, ya=ya, lsum=lsum, nblk=nblk, x2=x2, gated=gated, h2=h2,
                          gate=gate, e=e))
        xl = x3

    dx, loss_acc, d_final_g = _loss_head(xl, target, final_g[None, :])
    loss = lax.psum(loss_acc[0, 0], ("x", "y", "c"))

    dwin, dwout, dwpg, dwpe = [None] * DEPTH, [None] * DEPTH, [None] * DEPTH, [None] * DEPTH
    small = dict(norm_g=[None] * DEPTH, conv_b=[None] * DEPTH, branch_g=[None] * DEPTH,
                 ple_norm_g=[None] * DEPTH, b_pg=[None] * DEPTH, conv_w=[None] * DEPTH)
    slot = lambda r_, c_: jax.ShapeDtypeStruct((r_, c_), BF16)
    r_in, r_out, r_pg, r_pe = [None] * DEPTH, [None] * DEPTH, [None] * DEPTH, [None] * DEPTH

    def rest_items(l):
        return [(dwout[l], slot(128, D_MODEL), "rows128"), (dwpg[l], slot(128, D_MODEL), "rows128"),
                (dwpe[l], slot(PLE_DIM, 128), "cols128")]

    for l in reversed(range(DEPTH)):
        sv = saved[l]
        (dx2, dya, dmisc, dconv, dwout[l], dwpg[l], dwpe[l], d_bpg, d_pg, d_bg, d_cbias, d_cw) = _bwd_mid(
            dx, sv["x2"], sv["gate"], sv["e"], sv["pc"], sv["az"], sv["ya"], sv["gated"], sv["h2"], p, l,
            cw_full[l], vec(conv_b, l), vec(branch_g, l), vec(ple_norm_g, l), wpg_f[l], wout_f[l], f"bwd_mid_{l}")
        ride = None
        if l == 0:
            ride = _exchange_comm([(dwin[1], slot(D_MODEL, 512), "cols512"), *rest_items(1)])
        (dq, dk, dv), got = _attn_bwd(sv["qkv"], sv["lsum"], sv["nblk"], dya, f"attn_bwd_{l}", comm=ride)
        if l == 0:
            r_in[1], r_out[1], r_pg[1], r_pe[1] = got
        (dproj, dx, d_ng), _ = _bwd_dproj(dmisc, dconv, sv["pc"], dq, dk, dv, sv["x"], dx2, vec(norm_g, l),
                                          cw_full[l], win_f[l], f"bwd_dproj_{l}")
        ride = _exchange_comm(rest_items(0)) if l == 0 else None
        dwin[l], got = _atb(sv["h"], dproj, f"dw_in_{l}", comm=ride)
        if l == 0:
            r_out[0], r_pg[0], r_pe[0] = got
        small["norm_g"][l], small["conv_b"][l], small["branch_g"][l] = d_ng, d_cbias, d_bg
        small["ple_norm_g"][l], small["b_pg"][l], small["conv_w"][l] = d_pg, d_bpg, d_cw[:3]
    grad_x = dx.reshape(1, s, D_MODEL)

    flat = lambda parts: jnp.concatenate([a.reshape(-1) for a in parts])
    small_vec = jnp.concatenate([
        flat(small["norm_g"]), flat(small["conv_b"]), flat(small["branch_g"]), flat(small["ple_norm_g"]),
        flat(small["b_pg"]), d_final_g.reshape(-1), flat(small["conv_w"])]).reshape(SMALL_ROWS, LANES)
    r_in[0], r_small = _comm_call(_exchange_comm([
        (dwin[0], slot(D_MODEL, 512), "cols512"),
        (small_vec, jax.ShapeDtypeStruct((SMALL_ROWS, LANES), F32), "slot")]), "exchange_last")

    g_win, d_win, m_win, v_win = _adamw_sum8(r_in, w_in, m_w_in, v_w_in, "adamw_w_in")
    g_wout, d_wout, m_wout, v_wout = _adamw_sum8(r_out, w_out, m_w_out, v_w_out, "adamw_w_out")
    g_wpg, d_wpg, m_wpg, v_wpg = _adamw_sum8(r_pg, w_pg, m_w_pg, v_w_pg, "adamw_w_pg")
    g_wpe, d_wpe, m_wpe, v_wpe = _adamw_sum8(r_pe, w_pe, m_w_pe, v_w_pe, "adamw_w_pe")

    g_small = _sum8_small(r_small)
    repl = [(norm_g, m_norm_g, v_norm_g), (conv_b, m_conv_b, v_conv_b), (branch_g, m_branch_g, v_branch_g),
            (ple_norm_g, m_ple_norm_g, v_ple_norm_g), (b_pg, m_b_pg, v_b_pg), (final_g, m_final_g, v_final_g)]
    pack = lambda idx: jnp.concatenate([t[idx].reshape(-1) for t in repl]).reshape(SMALL_REPL_ROWS, LANES)
    g_repl = g_small[:SMALL_REPL_ROWS]
    d_repl, m_repl, v_repl = _adamw_plain(g_repl, pack(0), pack(1), pack(2), "adamw_replicated")

    def unpack(a):
        flat_a = a.reshape(-1)
        out, off = [], 0
        for t in repl:
            n = t[0].size
            out.append(flat_a[off:off + n].reshape(t[0].shape))
            off += n
        return out

    g_r, d_r, m_r, v_r = unpack(g_repl), unpack(d_repl), unpack(m_repl), unpack(v_repl)

    g_cw_full = g_small[SMALL_REPL_ROWS:].reshape(DEPTH, 3, D_CONV)
    g_cw = lax.dynamic_slice(g_cw_full, (0, 0, me_blk * HEAD_DIM), (DEPTH, 3, HEAD_DIM))
    pad_cw = lambda a: jnp.zeros((SUBLANES, LANES), F32).at[:3].set(a.reshape(3, LANES))
    v_cw_pad = jnp.ones((SUBLANES, LANES), F32).at[:3].set(v_conv_w.reshape(3, LANES))
    d_cw, m_cw, v_cw = _adamw_plain(pad_cw(g_cw), pad_cw(conv_w), pad_cw(m_conv_w), v_cw_pad, "adamw_conv_w")
    un_cw = lambda a: a[:3].reshape(DEPTH, 3, HEAD_DIM)

    def ordered(r, win_, cw_, wout_, wpg_, wpe_):
        return [r[0], win_, cw_, r[1], r[2], wout_, r[3], wpg_, r[4], wpe_, r[5]]

    grads = ordered(g_r, g_win, g_cw, g_wout, g_wpg, g_wpe)
    deltas = ordered(d_r, d_win, un_cw(d_cw), d_wout, d_wpg, d_wpe)
    new_m = ordered(m_r, m_win, un_cw(m_cw), m_wout, m_wpg, m_wpe)
    new_v = ordered(v_r, v_win, un_cw(v_cw), v_wout, v_wpg, v_wpe)
    return (loss, grad_x, *grads, *deltas, *new_m, *new_v)
```

```python
import jax
import jax.numpy as jnp
from jax import lax
from jax.experimental import pallas as pl
from jax.experimental.pallas import tpu as pltpu

F32 = jnp.float32
BF16 = jnp.bfloat16

D_MODEL = 1024
D_CONV = 512
D_SB = 512
N_IN = 4096
HEAD_DIM = 64
PLE_DIM = 256
DEPTH = 2
EPS = 1e-6
ADAM_LR = 0.001
ADAM_B1 = 0.9
ADAM_B2 = 0.999
ADAM_EPS = 1e-08
ADAM_WD = 0.01
ADAM_STEP = 10

LANES = 128
SUBLANES = 8
VMEM_BYTES_V7X = 64 * 1024 * 1024
VMEM_LIMIT = VMEM_BYTES_V7X - 8 * 1024 * 1024

N_DEV = 8
ROW_TILE = 256
ATTN_TILE = 256
SMALL_GRAD_ROWS = 104
SMALL_REPL_ROWS = 80
SMALL_ROWS = 112

NT = (((1,), (1,)), ((), ()))
TN = (((0,), (0,)), ((), ()))


def _call(body, **kw):
    return pl.pallas_call(body, **kw)


def _params(sem=None, vmem=None):
    return pltpu.CompilerParams(dimension_semantics=sem, vmem_limit_bytes=vmem)


def _sigmoid(z):
    return 0.5 * jnp.tanh(0.5 * z) + 0.5


def _group_bcast_sum(a, lo):
    s_lo = jnp.sum(jnp.where(lo, a, 0.0), axis=-1, keepdims=True)
    s_hi = jnp.sum(jnp.where(lo, 0.0, a), axis=-1, keepdims=True)
    return jnp.where(lo, s_lo, s_hi)


def _my_block():
    return 4 * lax.axis_index("x") + 2 * lax.axis_index("y") + lax.axis_index("c")


def _cast_bf16(arrays, name):
    n = len(arrays)

    def body(*refs):
        for a_ref, o_ref in zip(refs[:n], refs[n:]):
            o_ref[...] = a_ref[...].astype(BF16)

    whole = lambda a: pl.BlockSpec(a.shape, lambda: (0, 0))
    return _call(
        body, name=name,
        out_shape=tuple(jax.ShapeDtypeStruct(a.shape, BF16) for a in arrays),
        in_specs=[whole(a) for a in arrays], out_specs=tuple(whole(a) for a in arrays),
        compiler_params=_params(None, VMEM_LIMIT),
    )(*arrays)


class _Comm:
    def __init__(self, inputs, out_shapes, scratch, begin, middle, finish):
        self.inputs, self.out_shapes, self.scratch = list(inputs), list(out_shapes), list(scratch)
        self.begin, self.middle, self.finish = begin, middle, finish


def _slab(kind, ref, blk):
    if kind == "cols512":
        return ref.at[:, pl.ds(blk * 512, 512)]
    if kind == "rows128":
        return ref.at[pl.ds(blk * 128, 128), :]
    if kind == "cols128":
        return ref.at[:, pl.ds(blk * 128, 128)]
    return ref.at[blk]


def _gather_comm(items):
    n_t = len(items)
    kinds = [it[2] for it in items]

    def ctx(ins, outs, sems):
        send_sems, recv_sems, local_sems = sems
        x, y, c = lax.axis_index("x"), lax.axis_index("y"), lax.axis_index("c")
        me, sibling = (x, y, c), (x, y, 1 - c)
        chips = [(1 - x, y), (x, 1 - y), (1 - x, 1 - y)]

        def place(t, dev):
            return _slab(kinds[t], outs[t], 4 * dev[0] + 2 * dev[1] + dev[2])

        def copy(t, k, block, to, own=False):
            return pltpu.make_async_remote_copy(
                src_ref=ins[t] if own else place(t, block), dst_ref=place(t, block),
                send_sem=send_sems.at[t, k], recv_sem=recv_sems.at[t, k],
                device_id=to, device_id_type=pl.DeviceIdType.MESH)

        mine = [pltpu.make_async_copy(ins[t], place(t, me), local_sems.at[t]) for t in range(n_t)]
        first = []
        for t in range(n_t):
            first.append(copy(t, 0, me, sibling, own=True))
            first += [copy(t, 1 + j, me, (*chip, c), own=True) for j, chip in enumerate(chips)]
        passed = [copy(t, 4 + j, (*chip, c), sibling) for j, chip in enumerate(chips) for t in range(n_t)]
        landed = [copy(t, 1 + j, (*chip, c), me) for j, chip in enumerate(chips) for t in range(n_t)]
        from_sibling = []
        for t in range(n_t):
            from_sibling.append(copy(t, 0, sibling, me))
            from_sibling += [copy(t, 4 + j, (*chip, 1 - c), me) for j, chip in enumerate(chips)]
        return mine, first, landed, passed, from_sibling

    def begin(ins, outs, sems):
        mine, first, _, _, _ = ctx(ins, outs, sems)
        for cp in mine + first:
            cp.start()

    def middle(ins, outs, sems):
        _, _, landed, passed, _ = ctx(ins, outs, sems)
        for got, fwd in zip(landed, passed):
            got.wait_recv()
            fwd.start()

    def finish(ins, outs, sems):
        mine, first, _, passed, from_sibling = ctx(ins, outs, sems)
        for cp in from_sibling:
            cp.wait_recv()
        for cp in first + passed:
            cp.wait_send()
        for cp in mine:
            cp.wait()

    scratch = [pltpu.SemaphoreType.DMA((n_t, 7)), pltpu.SemaphoreType.DMA((n_t, 7)), pltpu.SemaphoreType.DMA((n_t,))]
    return _Comm([it[0] for it in items], [it[1] for it in items], scratch, begin, middle, finish)


def _exchange_comm(items):
    n_t = len(items)
    kinds = [it[2] for it in items]

    def ctx(ins, outs, sems):
        send_sems, recv_sems, local_sems = sems
        x, y, c = lax.axis_index("x"), lax.axis_index("y"), lax.axis_index("c")
        me_blk = 4 * x + 2 * y + c

        def src(t, blk):
            return ins[t] if kinds[t] == "slot" else _slab(kinds[t], ins[t], blk)

        local = [pltpu.make_async_copy(src(t, me_blk), outs[t].at[me_blk], local_sems.at[t]) for t in range(n_t)]
        remote = []
        for k in range(1, N_DEV):
            px = 1 - x if k & 4 else x
            py = 1 - y if k & 2 else y
            pc_ = 1 - c if k & 1 else c
            for t in range(n_t):
                remote.append(pltpu.make_async_remote_copy(
                    src_ref=src(t, 4 * px + 2 * py + pc_), dst_ref=outs[t].at[me_blk],
                    send_sem=send_sems.at[k - 1, t], recv_sem=recv_sems.at[k - 1, t],
                    device_id=(px, py, pc_), device_id_type=pl.DeviceIdType.MESH))
        return local, remote

    def begin(ins, outs, sems):
        local, remote = ctx(ins, outs, sems)
        for cp in local + remote:
            cp.start()

    def finish(ins, outs, sems):
        local, remote = ctx(ins, outs, sems)
        for cp in remote:
            cp.wait_recv()
        for cp in remote:
            cp.wait_send()
        for cp in local:
            cp.wait()

    scratch = [pltpu.SemaphoreType.DMA((N_DEV - 1, n_t)), pltpu.SemaphoreType.DMA((N_DEV - 1, n_t)),
               pltpu.SemaphoreType.DMA((n_t,))]
    out_shapes = [jax.ShapeDtypeStruct((N_DEV, *it[1].shape), it[1].dtype) for it in items]
    return _Comm([it[0] for it in items], out_shapes, scratch, begin, None, finish)


def _comm_call(comm, name):
    n_in, n_out = len(comm.inputs), len(comm.out_shapes)

    def body(*refs):
        ins, outs, sems = refs[:n_in], refs[n_in:n_in + n_out], refs[n_in + n_out:]
        comm.begin(ins, outs, sems)
        if comm.middle is not None:
            comm.middle(ins, outs, sems)
        comm.finish(ins, outs, sems)

    any_spec = pl.BlockSpec(memory_space=pl.ANY)
    return _call(body, name=name, out_shape=tuple(comm.out_shapes), in_specs=[any_spec] * n_in,
                 out_specs=[any_spec] * n_out, scratch_shapes=comm.scratch)(*comm.inputs)


def _hosted(body, n_in, n_out, comm, first, last, middle):
    if comm is None:
        return lambda *refs: body(*refs)
    n_ci, n_co, n_cs = len(comm.inputs), len(comm.out_shapes), len(comm.scratch)

    def wrapped(*refs):
        ins, cin = refs[:n_in], refs[n_in:n_in + n_ci]
        o0 = n_in + n_ci
        outs, cout = refs[o0:o0 + n_out], refs[o0 + n_out:o0 + n_out + n_co]
        scr, csem = refs[o0 + n_out + n_co:len(refs) - n_cs], refs[len(refs) - n_cs:]
        pl.when(first())(lambda: comm.begin(cin, cout, csem))
        body(*ins, *outs, *scr)
        if comm.middle is not None:
            pl.when(middle())(lambda: comm.middle(cin, cout, csem))
        pl.when(last())(lambda: comm.finish(cin, cout, csem))

    return wrapped


def _hosted_call(body, comm, *, name, grid, out_shape, in_specs, out_specs, args, scratch_shapes=(), sem=None):
    nd = len(grid)
    first, last, middle = _at_first(nd), _at_last(nd), _at_middle(nd)
    if comm is not None:
        sem = ("arbitrary",) * nd
    n_in, n_out = len(in_specs), len(out_shape)
    any_spec = pl.BlockSpec(memory_space=pl.ANY)
    c_in = [] if comm is None else comm.inputs
    c_out = [] if comm is None else comm.out_shapes
    c_scr = [] if comm is None else comm.scratch
    outs = _call(
        _hosted(body, n_in, n_out, comm, first, last, middle), name=name, grid=grid,
        out_shape=(*out_shape, *c_out),
        in_specs=[*in_specs, *[any_spec] * len(c_in)],
        out_specs=(*out_specs, *[any_spec] * len(c_out)),
        scratch_shapes=[*scratch_shapes, *c_scr],
        compiler_params=_params(sem, VMEM_LIMIT),
    )(*args, *c_in)
    return outs[:n_out], outs[n_out:]


def _grid_step(ndim):
    i, n = pl.program_id(0), pl.num_programs(0)
    for d in range(1, ndim):
        i, n = i * pl.num_programs(d) + pl.program_id(d), n * pl.num_programs(d)
    return i, n


def _at_first(ndim):
    return lambda: _grid_step(ndim)[0] == 0


def _at_last(ndim):
    def pred():
        i, n = _grid_step(ndim)
        return i == n - 1
    return pred


def _at_middle(ndim):
    def pred():
        i, n = _grid_step(ndim)
        return i == (3 * n) // 4
    return pred


def _fwd_in(x, g, w_full, name, comm=None):
    s = x.shape[0]
    ts = min(ROW_TILE, s)

    def body(x_ref, g_ref, w_ref, h_ref, pc_ref, qkv_ref, az_ref):
        xf = x_ref[...]
        r = lax.rsqrt(jnp.mean(xf * xf, axis=-1, keepdims=True) + EPS)
        h = (xf * r * g_ref[...]).astype(BF16)
        h_ref[...] = h
        pc_ref[...] = jnp.dot(h, w_ref[:, 0:2048], preferred_element_type=F32).astype(BF16)
        q = jnp.dot(h, w_ref[:, 2048:2560], preferred_element_type=F32)
        qkv_ref[:, 0:512] = (q * 0.125).astype(BF16)
        qkv_ref[:, 512:1536] = jnp.dot(h, w_ref[:, 2560:3584], preferred_element_type=F32).astype(BF16)
        az_ref[...] = jnp.dot(h, w_ref[:, 3584:4096], preferred_element_type=F32).astype(BF16)

    row = lambda width: pl.BlockSpec((ts, width), lambda i: (i, 0))
    return _hosted_call(
        body, comm, name=name, grid=(s // ts,),
        out_shape=(jax.ShapeDtypeStruct((s, D_MODEL), BF16), jax.ShapeDtypeStruct((s, 2048), BF16),
                   jax.ShapeDtypeStruct((s, 1536), BF16), jax.ShapeDtypeStruct((s, 512), BF16)),
        in_specs=[row(D_MODEL), pl.BlockSpec((1, D_MODEL), lambda i: (0, 0)),
                  pl.BlockSpec((D_MODEL, N_IN), lambda i: (0, 0))],
        out_specs=(row(D_MODEL), row(2048), row(1536), row(512)),
        args=(x, g, w_full), sem=("parallel",))


ATTN_ROWS = 128
ATTN_DONE = 104.0


def _attn_pieces(tq, rc):
    lane = lax.broadcasted_iota(jnp.int32, (1, LANES), 1)
    lo = lane < HEAD_DIM
    row = lax.broadcasted_iota(jnp.int32, (tq, tq), 0)
    col = lax.broadcasted_iota(jnp.int32, (tq, tq), 1)
    tri_gt = jnp.where(row > col, 1.0, 0.0).astype(BF16)
    tri_le = jnp.where(row <= col, 1.0, 0.0).astype(BF16)
    rrow = lax.broadcasted_iota(jnp.int32, (rc, tq), 0)
    rcol = lax.broadcasted_iota(jnp.int32, (rc, tq), 1)
    causal = [rcol < rrow + r * rc for r in range(tq // rc)]
    return lo, causal, tri_gt, tri_le


def _split_heads(a, lo):
    z = jnp.zeros_like(a)
    return (jnp.where(lo, a, z), jnp.where(lo, z, a))


def _softplus(z, causal, diag):
    neg_abs = lax.bitcast_convert_type(lax.bitcast_convert_type(z, jnp.uint32) | jnp.uint32(0x80000000), F32)
    sp = jnp.maximum(z, 0.0) + jnp.log(1.0 + jnp.exp(neg_abs))
    if diag:
        sp = jnp.where(causal, sp, 0.0)
    return sp


def _attn_fwd(qkv, name, comm=None):
    s = qkv.shape[0]
    tq = min(ATTN_TILE, s)
    nq = s // tq
    rc = min(ATTN_ROWS, tq)
    n_rc = tq // rc
    chains = [(r, hh) for r in range(n_rc) for hh in range(2)]

    def body(q_ref, k_ref, v_ref, o_ref, lsum_ref, nblk_ref):
        hp, qi = pl.program_id(0), pl.program_id(1)
        lo, causal, tri_gt, _ = _attn_pieces(tq, rc)
        qh = _split_heads(q_ref[...], lo)
        qc = {(r, hh): qh[hh][r * rc:(r + 1) * rc] for r, hh in chains}

        mm = lambda a_, b_: jnp.dot(a_.astype(BF16), b_, preferred_element_type=F32)
        rowsum = lambda a_: jnp.sum(a_, axis=-1, keepdims=True)

        def block(kb, carry):
            start = pl.multiple_of(kb * tq, tq)
            k = k_ref[pl.ds(start, tq), :]
            vh = _split_heads(v_ref[pl.ds(start, tq), :], lo)
            z = {ch: lax.dot_general(qc[ch], k, NT, preferred_element_type=F32) for ch in chains}
            sp = {ch: _softplus(z[ch], None, False) for ch in chains}
            later = {ch: mm(sp[ch], tri_gt) for ch in chains}
            a = {ch: jnp.exp((z[ch] - sp[ch]) - (carry[ch[0]][1 + ch[1]] + later[ch])) for ch in chains}
            pv = {ch: mm(a[ch], vh[ch[1]]) for ch in chains}
            return tuple((carry[r][0] + pv[(r, 0)] + pv[(r, 1)],
                          carry[r][1] + rowsum(sp[(r, 0)]), carry[r][2] + rowsum(sp[(r, 1)])) for r in range(n_rc))

        def first_two(prev_ok):
            d0 = pl.multiple_of(qi * tq, tq)
            p0 = pl.multiple_of(jnp.maximum(qi - 1, 0) * tq, tq)
            k_d, k_p = k_ref[pl.ds(d0, tq), :], k_ref[pl.ds(p0, tq), :]
            vh_d = _split_heads(v_ref[pl.ds(d0, tq), :], lo)
            vh_p = _split_heads(v_ref[pl.ds(p0, tq), :], lo)
            z_d = {ch: lax.dot_general(qc[ch], k_d, NT, preferred_element_type=F32) for ch in chains}
            z_p = {ch: lax.dot_general(qc[ch], k_p, NT, preferred_element_type=F32) for ch in chains}
            sp_d = {ch: _softplus(z_d[ch], causal[ch[0]], True) for ch in chains}
            sp_raw = {ch: _softplus(z_p[ch], None, False) for ch in chains}
            sp_p = {ch: jnp.where(prev_ok, sp_raw[ch], 0.0) for ch in chains}
            later_d = {ch: mm(sp_d[ch], tri_gt) for ch in chains}
            later_p = {ch: mm(sp_p[ch], tri_gt) for ch in chains}
            c_d = {ch: rowsum(sp_d[ch]) for ch in chains}
            a_d = {ch: jnp.where(causal[ch[0]], jnp.exp((z_d[ch] - sp_d[ch]) - later_d[ch]), 0.0) for ch in chains}
            a_p = {ch: jnp.where(prev_ok, jnp.exp((z_p[ch] - sp_raw[ch]) - (c_d[ch] + later_p[ch])), 0.0)
                   for ch in chains}
            pv = {ch: mm(a_d[ch], vh_d[ch[1]]) + mm(a_p[ch], vh_p[ch[1]]) for ch in chains}
            return tuple((pv[(r, 0)] + pv[(r, 1)],
                          c_d[(r, 0)] + rowsum(sp_p[(r, 0)]), c_d[(r, 1)] + rowsum(sp_p[(r, 1)]))
                         for r in range(n_rc))

        def least(carry):
            m = jnp.minimum(carry[0][1], carry[0][2])
            for r in range(1, n_rc):
                m = jnp.minimum(m, jnp.minimum(carry[r][1], carry[r][2]))
            return jnp.min(m)

        carry = first_two(qi > 0)

        def go_on(st):
            return jnp.logical_and(st[0] < qi - 1, st[1] < ATTN_DONE)

        def step(st):
            new = block(qi - 2 - st[0], st[2])
            return st[0] + 1, least(new), new

        walked, _, carry = lax.while_loop(go_on, step, (jnp.int32(0), least(carry), carry))
        for r in range(n_rc):
            o_ref[r * rc:(r + 1) * rc, :] = carry[r][0].astype(BF16)
            lsum_ref[r * rc:(r + 1) * rc, :] = jnp.where(lo, carry[r][1], carry[r][2])
        nblk_ref[hp, qi] = walked.astype(F32)

    blk = pl.BlockSpec((tq, LANES), lambda hp, qi: (qi, hp))
    o512 = jax.ShapeDtypeStruct((s, D_SB), F32)
    return _hosted_call(
        body, comm, name=name, grid=(4, nq),
        out_shape=(jax.ShapeDtypeStruct((s, D_SB), BF16), o512, jax.ShapeDtypeStruct((4, nq), F32)),
        in_specs=[blk, pl.BlockSpec((s, LANES), lambda hp, qi: (0, 4 + hp)),
                  pl.BlockSpec((s, LANES), lambda hp, qi: (0, 8 + hp))],
        out_specs=(blk, blk, pl.BlockSpec(memory_space=pltpu.SMEM)),
        args=(qkv, qkv, qkv), sem=("arbitrary", "arbitrary"))


HALO = 16


def _conv_taps(cc_ref, ch_ref, ccp_ref, chp_ref, halo_ref, first):
    u = cc_ref[...].astype(F32) * ch_ref[...].astype(F32)
    halo_ref[...] = ccp_ref[...].astype(F32) * chp_ref[...].astype(F32) * jnp.where(first, 0.0, 1.0)
    p6 = halo_ref[HALO - 2:HALO - 1, :]
    p7 = halo_ref[HALO - 1:HALO, :]
    rowi = lax.broadcasted_iota(jnp.int32, u.shape, 0)
    u1 = jnp.where(rowi == 0, p7, pltpu.roll(u, 1, 0))
    u2 = jnp.where(rowi == 0, p6, jnp.where(rowi == 1, p7, pltpu.roll(u, 2, 0)))
    return u, u1, u2


def _fwd_mid(x, pc, az, ya, p4, layer, cw, cb, bg, wout_full, pg, wpg_full, bpg, wpe_full, name, comm=None):
    s = x.shape[0]
    ts = min(ROW_TILE, s)
    blk_h = ts // HALO

    def body(x_ref, cb_ref_, cc_ref, ch_ref, cz_ref, ccp_ref, chp_ref, az_ref, ya_ref, p_ref,
             cw_ref, cbias_ref, bg_ref, wout_ref, pg_ref, wpg_ref, bpg_ref, wpe_ref,
             x2_ref, x3_ref, gated_ref, h2_ref, gate_ref, e_ref, halo_ref):
        i = pl.program_id(0)
        lane = lax.broadcasted_iota(jnp.int32, (1, LANES), 1)
        lo = lane < HEAD_DIM
        u, u1, u2 = _conv_taps(cc_ref, ch_ref, ccp_ref, chp_ref, halo_ref, i == 0)
        conv = cbias_ref[...] + cw_ref[0:1, :] * u2 + cw_ref[1:2, :] * u1 + cw_ref[2:3, :] * u
        yc = cb_ref_[...].astype(F32) * conv
        for sl in range(8):
            cols = slice(LANES * (sl % 4), LANES * (sl % 4 + 1))
            y = yc[:, cols] if sl < 4 else ya_ref[:, cols].astype(F32)
            zc = (cz_ref[:, cols] if sl < 4 else az_ref[:, cols]).astype(F32)
            rg = lax.rsqrt(_group_bcast_sum(y * y, lo) * (1.0 / HEAD_DIM) + EPS)
            yn = y * rg * bg_ref[:, LANES * sl:LANES * (sl + 1)]
            gated_ref[:, LANES * sl:LANES * (sl + 1)] = (yn * (zc * _sigmoid(zc))).astype(BF16)
        x2 = x_ref[...] + jnp.dot(gated_ref[...], wout_ref[...], preferred_element_type=F32)
        x2_ref[...] = x2
        r2 = lax.rsqrt(jnp.mean(x2 * x2, axis=-1, keepdims=True) + EPS)
        h2 = (x2 * r2 * pg_ref[...]).astype(BF16)
        h2_ref[...] = h2
        gate = _sigmoid(jnp.dot(h2, wpg_ref[...], preferred_element_type=F32) + bpg_ref[...])
        gate_ref[...] = gate.astype(BF16)
        e = jnp.dot(p_ref[...].astype(BF16), wpe_ref[...], preferred_element_type=F32)
        e_ref[...] = e.astype(BF16)
        x3_ref[...] = x2 + gate * e

    row = lambda width, cb_=0: pl.BlockSpec((ts, width), lambda i: (i, cb_))
    prev = lambda cb_: pl.BlockSpec((HALO, 512), lambda i: (jnp.maximum(i * blk_h - 1, 0), cb_))
    vec = lambda width: pl.BlockSpec((1, width), lambda i: (0, 0))
    wspec = lambda r_, c_: pl.BlockSpec((r_, c_), lambda i: (0, 0))
    f32o = jax.ShapeDtypeStruct((s, D_MODEL), F32)
    bfo = jax.ShapeDtypeStruct((s, D_MODEL), BF16)
    return _hosted_call(
        body, comm, name=name, grid=(s // ts,),
        out_shape=(f32o, f32o, bfo, bfo, bfo, bfo),
        scratch_shapes=[pltpu.VMEM((HALO, 512), F32)],
        in_specs=[row(D_MODEL), row(512, 0), row(512, 1), row(512, 2), row(512, 3), prev(1), prev(2),
                  row(512), row(512),
                  pl.BlockSpec((None, None, ts, PLE_DIM), lambda i: (layer, 0, i, 0)),
                  pl.BlockSpec((3, 512), lambda i: (0, 0)), vec(512), vec(D_MODEL),
                  wspec(D_MODEL, D_MODEL), vec(D_MODEL), wspec(D_MODEL, D_MODEL), vec(D_MODEL),
                  wspec(PLE_DIM, D_MODEL)],
        out_specs=(row(D_MODEL),) * 6,
        args=(x, pc, pc, pc, pc, pc, pc, az, ya, p4, cw, cb, bg, wout_full, pg, wpg_full, bpg, wpe_full),
        sem=("parallel",))


def _loss_head(xf, target, fg):
    s = xf.shape[0]
    ts = min(ROW_TILE, s)

    def body(x_ref, t_ref, g_ref, dx_ref, loss_ref, dg_ref):
        i = pl.program_id(0)

        @pl.when(i == 0)
        def _():
            loss_ref[...] = jnp.zeros_like(loss_ref)
            dg_ref[...] = jnp.zeros_like(dg_ref)

        x = x_ref[...]
        g = g_ref[...]
        r = lax.rsqrt(jnp.mean(x * x, axis=-1, keepdims=True) + EPS)
        xn = x * r
        err = xn * g - t_ref[...]
        per_row = jnp.sum(err * err, axis=-1, keepdims=True)
        loss_ref[...] += jnp.sum(per_row, axis=0, keepdims=True) * (0.5 / D_MODEL)
        dy = err * (1.0 / D_MODEL)
        dg_ref[...] += jnp.sum(dy * xn, axis=0, keepdims=True)
        dxn = dy * g
        dx_ref[...] = r * (dxn - xn * jnp.mean(dxn * xn, axis=-1, keepdims=True))

    row = pl.BlockSpec((ts, D_MODEL), lambda i: (i, 0))
    return _call(
        body, name="loss_head", grid=(s // ts,),
        out_shape=(jax.ShapeDtypeStruct((s, D_MODEL), F32), jax.ShapeDtypeStruct((1, LANES), F32),
                   jax.ShapeDtypeStruct((1, D_MODEL), F32)),
        in_specs=[row, row, pl.BlockSpec((1, D_MODEL), lambda i: (0, 0))],
        out_specs=(row, pl.BlockSpec((1, LANES), lambda i: (0, 0)), pl.BlockSpec((1, D_MODEL), lambda i: (0, 0))),
        compiler_params=_params(("arbitrary",), VMEM_LIMIT),
    )(xf, target, fg)


def _bwd_mid(dx3, x2, gate, e, pc, az, ya, gated, h2, p4, layer, cw, cb, bg, pg, wpg_full, wout_full, name):
    s = x2.shape[0]
    ts = min(ROW_TILE, s)
    blk_h = ts // HALO

    def body(dx3_ref, x2_ref, gate_ref, e_ref, cb_ref_, cc_ref, ch_ref, cz_ref, ccp_ref, chp_ref, az_ref, ya_ref,
             gated_ref, h2_ref, p_ref, cw_ref, cbias_ref, bg_ref, pg_ref, wpg_ref, wout_ref,
             dx2_ref, dya_ref, dmisc_ref, dconv_ref, dwout_ref, dwpg_ref, dwpe_ref,
             dbpg_ref, dpg_ref, dbg_ref, dcbias_ref, dcw_ref,
             dgated_ref, halo_ref, acc_out, acc_pg, acc_pe):
        i = pl.program_id(0)

        @pl.when(i == 0)
        def _():
            for ref in (dbpg_ref, dpg_ref, dbg_ref, dcbias_ref, dcw_ref, acc_out, acc_pg, acc_pe):
                ref[...] = jnp.zeros_like(ref)

        lane = lax.broadcasted_iota(jnp.int32, (1, LANES), 1)
        lo = lane < HEAD_DIM
        dx3 = dx3_ref[...]
        gate = gate_ref[...].astype(F32)
        de_b = (dx3 * gate).astype(BF16)
        acc_pe[...] += lax.dot_general(p_ref[...].astype(BF16), de_b, TN, preferred_element_type=F32)
        dgpre = dx3 * e_ref[...].astype(F32) * gate * (1.0 - gate)
        dbpg_ref[...] += jnp.sum(dgpre, axis=0, keepdims=True)
        dgpre_b = dgpre.astype(BF16)
        acc_pg[...] += lax.dot_general(h2_ref[...], dgpre_b, TN, preferred_element_type=F32)
        dh2 = lax.dot_general(dgpre_b, wpg_ref[...], NT, preferred_element_type=F32)
        x2 = x2_ref[...]
        r2 = lax.rsqrt(jnp.mean(x2 * x2, axis=-1, keepdims=True) + EPS)
        xn2 = x2 * r2
        dpg_ref[...] += jnp.sum(dh2 * xn2, axis=0, keepdims=True)
        dxn = dh2 * pg_ref[...]
        dx2 = dx3 + r2 * (dxn - xn2 * jnp.mean(dxn * xn2, axis=-1, keepdims=True))
        dx2_ref[...] = dx2
        dx2_b = dx2.astype(BF16)
        acc_out[...] += lax.dot_general(gated_ref[...], dx2_b, TN, preferred_element_type=F32)
        dgated_ref[...] = lax.dot_general(dx2_b, wout_ref[...], NT, preferred_element_type=F32)

        @pl.when(i == pl.num_programs(0) - 1)
        def _():
            dwout_ref[...] = acc_out[...].astype(BF16)
            dwpg_ref[...] = acc_pg[...].astype(BF16)
            dwpe_ref[...] = acc_pe[...].astype(BF16)

        u, u1, u2 = _conv_taps(cc_ref, ch_ref, ccp_ref, chp_ref, halo_ref, i == 0)
        conv = cbias_ref[...] + cw_ref[0:1, :] * u2 + cw_ref[1:2, :] * u1 + cw_ref[2:3, :] * u
        c_b = cb_ref_[...].astype(F32)
        yc = c_b * conv
        for sl in range(8):
            cols = slice(LANES * (sl % 4), LANES * (sl % 4 + 1))
            wide = slice(LANES * sl, LANES * (sl + 1))
            y = yc[:, cols] if sl < 4 else ya_ref[:, cols].astype(F32)
            zc = (cz_ref[:, cols] if sl < 4 else az_ref[:, cols]).astype(F32)
            bgs = bg_ref[:, wide]
            dgt = dgated_ref[:, wide]
            rg = lax.rsqrt(_group_bcast_sum(y * y, lo) * (1.0 / HEAD_DIM) + EPS)
            yhat = y * rg
            sig = _sigmoid(zc)
            dyn = dgt * (zc * sig)
            dzc = dgt * (yhat * bgs) * (sig * (1.0 + zc * (1.0 - sig)))
            dbg_ref[:, wide] += jnp.sum(dyn * yhat, axis=0, keepdims=True)
            dyh = dyn * bgs
            dy = rg * (dyh - yhat * (_group_bcast_sum(dyh * yhat, lo) * (1.0 / HEAD_DIM)))
            if sl < 4:
                dconv = dy * c_b[:, cols]
                dmisc_ref[:, cols] = (dy * conv[:, cols]).astype(BF16)
                dmisc_ref[:, 512 + LANES * sl:512 + LANES * (sl + 1)] = dzc.astype(BF16)
                dconv_ref[:, cols] = dconv
                dcbias_ref[:, cols] += jnp.sum(dconv, axis=0, keepdims=True)
                dcw_ref[0:1, cols] += jnp.sum(dconv * u2[:, cols], axis=0, keepdims=True)
                dcw_ref[1:2, cols] += jnp.sum(dconv * u1[:, cols], axis=0, keepdims=True)
                dcw_ref[2:3, cols] += jnp.sum(dconv * u[:, cols], axis=0, keepdims=True)
            else:
                dya_ref[:, cols] = dy.astype(BF16)
                dmisc_ref[:, 1024 + LANES * (sl - 4):1024 + LANES * (sl - 3)] = dzc.astype(BF16)

    row = lambda width, cb_=0: pl.BlockSpec((ts, width), lambda i: (i, cb_))
    prev = lambda cb_: pl.BlockSpec((HALO, 512), lambda i: (jnp.maximum(i * blk_h - 1, 0), cb_))
    vec = lambda width: pl.BlockSpec((1, width), lambda i: (0, 0))
    wspec = lambda r_, c_: pl.BlockSpec((r_, c_), lambda i: (0, 0))
    vo = lambda width: jax.ShapeDtypeStruct((1, width), F32)
    sq = jax.ShapeDtypeStruct((D_MODEL, D_MODEL), BF16)
    return _call(
        body, name=name, grid=(s // ts,),
        out_shape=(jax.ShapeDtypeStruct((s, D_MODEL), F32), jax.ShapeDtypeStruct((s, 512), BF16),
                   jax.ShapeDtypeStruct((s, 1536), BF16), jax.ShapeDtypeStruct((s, 512), F32),
                   sq, sq, jax.ShapeDtypeStruct((PLE_DIM, D_MODEL), BF16),
                   vo(D_MODEL), vo(D_MODEL), vo(D_MODEL), vo(512), jax.ShapeDtypeStruct((SUBLANES, 512), F32)),
        in_specs=[row(D_MODEL), row(D_MODEL), row(D_MODEL), row(D_MODEL),
                  row(512, 0), row(512, 1), row(512, 2), row(512, 3), prev(1), prev(2), row(512), row(512),
                  row(D_MODEL), row(D_MODEL),
                  pl.BlockSpec((None, None, ts, PLE_DIM), lambda i: (layer, 0, i, 0)),
                  pl.BlockSpec((3, 512), lambda i: (0, 0)), vec(512), vec(D_MODEL), vec(D_MODEL),
                  wspec(D_MODEL, D_MODEL), wspec(D_MODEL, D_MODEL)],
        out_specs=(row(D_MODEL), row(512), row(1536), row(512),
                   wspec(D_MODEL, D_MODEL), wspec(D_MODEL, D_MODEL), wspec(PLE_DIM, D_MODEL),
                   vec(D_MODEL), vec(D_MODEL), vec(D_MODEL), vec(512),
                   pl.BlockSpec((SUBLANES, 512), lambda i: (0, 0))),
        scratch_shapes=[pltpu.VMEM((ts, D_MODEL), F32), pltpu.VMEM((HALO, 512), F32),
                        pltpu.VMEM((D_MODEL, D_MODEL), F32), pltpu.VMEM((D_MODEL, D_MODEL), F32),
                        pltpu.VMEM((PLE_DIM, D_MODEL), F32)],
        compiler_params=_params(("arbitrary",), VMEM_LIMIT),
    )(dx3, x2, gate, e, pc, pc, pc, pc, pc, pc, az, ya, gated, h2, p4, cw, cb, bg, pg, wpg_full, wout_full)


def _attn_bwd(qkv, lsum, nblk, dya, name, comm=None):
    s = qkv.shape[0]
    tq = min(ATTN_TILE, s)
    nq = s // tq
    rc = min(ATTN_ROWS, tq)
    n_rc = tq // rc
    chains = [(r, hh) for r in range(n_rc) for hh in range(2)]

    def body(nblk_ref, q_ref, k_ref, v_ref, lsum_ref, do_ref, dq_ref, dk_ref, dv_ref, dk_acc, dv_acc):
        hp, qi = pl.program_id(0), pl.program_id(1)

        @pl.when(qi == 0)
        def _():
            dk_acc[...] = jnp.zeros_like(dk_acc)
            dv_acc[...] = jnp.zeros_like(dv_acc)

        lo, causal, tri_gt, tri_le = _attn_pieces(tq, rc)
        lane = lax.broadcasted_iota(jnp.int32, (1, LANES), 1)
        qh = _split_heads(q_ref[...], lo)
        doh = _split_heads(do_ref[...].astype(BF16), lo)
        lt = lsum_ref[...]
        ltot_h = (jnp.sum(jnp.where(lane == 0, lt, 0.0), axis=-1, keepdims=True),
                  jnp.sum(jnp.where(lane == HEAD_DIM, lt, 0.0), axis=-1, keepdims=True))
        rows = lambda a_, r: a_[r * rc:(r + 1) * rc]
        qc = {(r, hh): rows(qh[hh], r) for r, hh in chains}
        doc = {(r, hh): rows(doh[hh], r) for r, hh in chains}
        ltot = {(r, hh): rows(ltot_h[hh], r) for r, hh in chains}

        mm = lambda a_, b_: jnp.dot(a_.astype(BF16), b_, preferred_element_type=F32)
        mm_nt = lambda a_, b_: lax.dot_general(a_, b_, NT, preferred_element_type=F32)
        mm_tn = lambda a_, b_: lax.dot_general(a_.astype(BF16), b_, TN, preferred_element_type=F32)
        rowsum = lambda a_: jnp.sum(a_, axis=-1, keepdims=True)

        def block(kb, carry, diag=False):
            start = pl.multiple_of(kb * tq, tq)
            k = k_ref[pl.ds(start, tq), :]
            v = v_ref[pl.ds(start, tq), :]
            kh = _split_heads(k, lo)
            keep = (lambda ch, a_: jnp.where(causal[ch[0]], a_, 0.0)) if diag else (lambda ch, a_: a_)
            z = {ch: mm_nt(qc[ch], k) for ch in chains}
            da = {ch: mm_nt(doc[ch], v) for ch in chains}
            sp = {ch: _softplus(z[ch], causal[ch[0]], diag) for ch in chains}
            later = {ch: mm(sp[ch], tri_gt) for ch in chains}
            walked = {ch: carry[ch[0]][1 + ch[1]] + rowsum(sp[ch]) for ch in chains}
            a = {ch: keep(ch, jnp.exp((z[ch] - sp[ch]) - ((ltot[ch] - walked[ch]) + later[ch]))) for ch in chains}
            g = {ch: a[ch] * da[ch] for ch in chains}
            upto = {ch: mm(g[ch], tri_le) for ch in chains}
            dz = {ch: keep(ch, g[ch] - jnp.exp(z[ch] - sp[ch]) * (carry[ch[0]][3 + ch[1]] + upto[ch])).astype(BF16)
                  for ch in chains}
            dqc = {ch: mm(dz[ch], kh[ch[1]]) for ch in chains}
            dkc = [mm_tn(dz[ch], qc[ch]) for ch in chains]
            dvc = [mm_tn(a[ch], doc[ch]) for ch in chains]
            dk_acc[pl.ds(start, tq), :] += sum(dkc[1:], dkc[0])
            dv_acc[pl.ds(start, tq), :] += sum(dvc[1:], dvc[0])
            return tuple((carry[r][0] + dqc[(r, 0)] + dqc[(r, 1)], walked[(r, 0)], walked[(r, 1)],
                          carry[r][3] + rowsum(g[(r, 0)]), carry[r][4] + rowsum(g[(r, 1)])) for r in range(n_rc))

        zc = jnp.zeros((rc, 1), F32)
        carry = tuple((jnp.zeros((rc, LANES), F32), zc, zc, zc, zc) for _ in range(n_rc))
        near = jnp.maximum(qi - 1, 0)
        first = near - jnp.clip(nblk_ref[hp, qi].astype(jnp.int32), 0, near)
        carry = lax.fori_loop(first, qi, block, carry)
        carry = block(qi, carry, True)
        for r in range(n_rc):
            dq_ref[r * rc:(r + 1) * rc, :] = (carry[r][0] * 0.125).astype(BF16)

        @pl.when(qi == pl.num_programs(1) - 1)
        def _():
            dk_ref[...] = dk_acc[...].astype(BF16)
            dv_ref[...] = dv_acc[...].astype(BF16)

    blk = pl.BlockSpec((tq, LANES), lambda hp, qi: (qi, hp))
    col = pl.BlockSpec((s, LANES), lambda hp, qi: (0, hp))
    o512 = jax.ShapeDtypeStruct((s, D_SB), BF16)
    return _hosted_call(
        body, comm, name=name, grid=(4, nq),
        out_shape=(o512, o512, o512),
        in_specs=[pl.BlockSpec(memory_space=pltpu.SMEM), blk,
                  pl.BlockSpec((s, LANES), lambda hp, qi: (0, 4 + hp)),
                  pl.BlockSpec((s, LANES), lambda hp, qi: (0, 8 + hp)), blk, blk],
        out_specs=(blk, col, col),
        scratch_shapes=[pltpu.VMEM((s, LANES), F32), pltpu.VMEM((s, LANES), F32)],
        args=(nblk, qkv, qkv, qkv, lsum, dya), sem=("parallel", "arbitrary"))


def _bwd_dproj(dmisc, dconv, pc, dq, dk, dv, x, dx2, g, cw, win_full, name, comm=None):
    s = x.shape[0]
    ts = min(ROW_TILE, s)
    blk8 = ts // SUBLANES
    last8 = s // SUBLANES - 1

    def body(dcb_ref, dcz_ref, daz_ref, dconv_ref, nxt_ref, cc_ref, ch_ref, dq_ref, dk_ref, dv_ref,
             x_ref, dx2_ref, g_ref, cw_ref, w_ref, dproj_ref, dx_ref, dg_ref):
        i = pl.program_id(0)

        @pl.when(i == 0)
        def _():
            dg_ref[...] = jnp.zeros_like(dg_ref)

        keep = jnp.where(i == pl.num_programs(0) - 1, 0.0, 1.0)
        dc = dconv_ref[...]
        n0 = nxt_ref[0:1, :] * keep
        n1 = nxt_ref[1:2, :] * keep
        rowi = lax.broadcasted_iota(jnp.int32, dc.shape, 0)
        dc1 = jnp.where(rowi == ts - 1, n0, pltpu.roll(dc, ts - 1, 0))
        dc2 = jnp.where(rowi == ts - 2, n0, jnp.where(rowi == ts - 1, n1, pltpu.roll(dc, ts - 2, 0)))
        du = cw_ref[2:3, :] * dc + cw_ref[1:2, :] * dc1 + cw_ref[0:1, :] * dc2
        dproj_ref[:, 0:512] = dcb_ref[...]
        dproj_ref[:, 512:1024] = (du * ch_ref[...].astype(F32)).astype(BF16)
        dproj_ref[:, 1024:1536] = (du * cc_ref[...].astype(F32)).astype(BF16)
        dproj_ref[:, 1536:2048] = dcz_ref[...]
        dproj_ref[:, 2048:2560] = dq_ref[...]
        dproj_ref[:, 2560:3072] = dk_ref[...]
        dproj_ref[:, 3072:3584] = dv_ref[...]
        dproj_ref[:, 3584:4096] = daz_ref[...]
        dh = lax.dot_general(dproj_ref[...], w_ref[...], NT, preferred_element_type=F32)
        x = x_ref[...]
        r = lax.rsqrt(jnp.mean(x * x, axis=-1, keepdims=True) + EPS)
        xn = x * r
        dg_ref[...] += jnp.sum(dh * xn, axis=0, keepdims=True)
        dxn = dh * g_ref[...]
        dx_ref[...] = dx2_ref[...] + r * (dxn - xn * jnp.mean(dxn * xn, axis=-1, keepdims=True))

    row = lambda width, cb_=0: pl.BlockSpec((ts, width), lambda i: (i, cb_))
    nxt = pl.BlockSpec((SUBLANES, 512), lambda i: (jnp.minimum((i + 1) * blk8, last8), 0))
    vec = lambda width: pl.BlockSpec((1, width), lambda i: (0, 0))
    return _hosted_call(
        body, comm, name=name, grid=(s // ts,),
        out_shape=(jax.ShapeDtypeStruct((s, N_IN), BF16), jax.ShapeDtypeStruct((s, D_MODEL), F32),
                   jax.ShapeDtypeStruct((1, D_MODEL), F32)),
        in_specs=[row(512, 0), row(512, 1), row(512, 2), row(512), nxt, row(512, 1), row(512, 2),
                  row(512), row(512), row(512), row(D_MODEL), row(D_MODEL), vec(D_MODEL),
                  pl.BlockSpec((3, 512), lambda i: (0, 0)),
                  pl.BlockSpec((D_MODEL, N_IN), lambda i: (0, 0))],
        out_specs=(row(N_IN), row(D_MODEL), vec(D_MODEL)),
        args=(dmisc, dmisc, dmisc, dconv, dconv, pc, pc, dq, dk, dv, x, dx2, g, cw, win_full),
        sem=("arbitrary",))


def _atb(a, b, name, a_index=None, comm=None):
    s, n = b.shape
    m = a.shape[-1]
    ts = min(512, s)
    tn = min(1024, n)
    if a_index is None:
        a_spec = pl.BlockSpec((ts, m), lambda j, i: (i, 0))
    else:
        a_spec = pl.BlockSpec((None, None, ts, m), lambda j, i: (a_index, 0, i, 0))

    def body(a_ref, b_ref, o_ref, acc_ref):
        i = pl.program_id(1)

        @pl.when(i == 0)
        def _():
            acc_ref[...] = jnp.zeros_like(acc_ref)

        acc_ref[...] += lax.dot_general(a_ref[...].astype(BF16), b_ref[...], TN, preferred_element_type=F32)

        @pl.when(i == pl.num_programs(1) - 1)
        def _():
            o_ref[...] = acc_ref[...].astype(BF16)

    (out,), got = _hosted_call(
        body, comm, name=name, grid=(n // tn, s // ts),
        out_shape=(jax.ShapeDtypeStruct((m, n), BF16),),
        in_specs=[a_spec, pl.BlockSpec((ts, tn), lambda j, i: (i, j))],
        out_specs=(pl.BlockSpec((m, tn), lambda j, i: (0, j)),),
        scratch_shapes=[pltpu.VMEM((m, tn), F32)],
        args=(a, b), sem=("parallel", "arbitrary"))
    return out, got


def _adamw_math(w, g, m, v):
    m2 = ADAM_B1 * m + (1.0 - ADAM_B1) * g
    v2 = ADAM_B2 * v + (1.0 - ADAM_B2) * (g * g)
    m_hat = m2 / (1.0 - ADAM_B1 ** ADAM_STEP)
    v_hat = v2 / (1.0 - ADAM_B2 ** ADAM_STEP)
    delta = -ADAM_LR * (m_hat / (jnp.sqrt(v_hat) + ADAM_EPS) + ADAM_WD * w)
    return delta, m2, v2


def _adamw_sum8(parts, w, m, v, name):
    _, rows, cols = w.shape
    tr = min(rows, 256)
    n_tiles = rows // tr
    assert len(parts) == DEPTH == 2

    def body(p0_ref, p1_ref, w_ref, m_ref, v_ref, g_ref, d_ref, m2_ref, v2_ref):
        def run(p_ref):
            g = p_ref[0].astype(F32)
            for d in range(1, N_DEV):
                g = g + p_ref[d].astype(F32)
            g_ref[...] = g
            d_ref[...], m2_ref[...], v2_ref[...] = _adamw_math(w_ref[...], g, m_ref[...], v_ref[...])

        pl.when(pl.program_id(0) == 0)(lambda: run(p0_ref))
        pl.when(pl.program_id(0) == 1)(lambda: run(p1_ref))

    part0 = pl.BlockSpec((N_DEV, tr, cols), lambda l, i: (0, jnp.where(l == 0, i, n_tiles - 1), 0))
    part1 = pl.BlockSpec((N_DEV, tr, cols), lambda l, i: (0, jnp.where(l == 1, i, 0), 0))
    tile = pl.BlockSpec((None, tr, cols), lambda l, i: (l, i, 0))
    o = jax.ShapeDtypeStruct((DEPTH, rows, cols), F32)
    return _call(
        body, name=name, grid=(DEPTH, n_tiles),
        out_shape=(o, o, o, o),
        in_specs=[part0, part1, tile, tile, tile],
        out_specs=(tile, tile, tile, tile),
        compiler_params=_params(("arbitrary", "arbitrary"), VMEM_LIMIT),
    )(parts[0], parts[1], w, m, v)


def _adamw_plain(g, w, m, v, name):
    rows, cols = g.shape

    def body(g_ref, w_ref, m_ref, v_ref, d_ref, m2_ref, v2_ref):
        d_ref[...], m2_ref[...], v2_ref[...] = _adamw_math(w_ref[...], g_ref[...], m_ref[...], v_ref[...])

    full = pl.BlockSpec((rows, cols), lambda: (0, 0))
    o = jax.ShapeDtypeStruct((rows, cols), F32)
    return _call(body, name=name, out_shape=(o, o, o), in_specs=[full] * 4, out_specs=(full,) * 3)(g, w, m, v)


def _sum8_small(parts):
    def body(p_ref, g_ref):
        g = p_ref[0]
        for d in range(1, N_DEV):
            g = g + p_ref[d]
        g_ref[...] = g

    return _call(
        body, name="sum_small_grads",
        out_shape=jax.ShapeDtypeStruct((SMALL_ROWS, LANES), F32),
        in_specs=[pl.BlockSpec((N_DEV, SMALL_ROWS, LANES), lambda: (0, 0, 0))],
        out_specs=pl.BlockSpec((SMALL_ROWS, LANES), lambda: (0, 0)),
    )(parts)


def kernel(x, p, norm_g, w_in, conv_w, conv_b, branch_g, w_out, ple_norm_g, w_pg, b_pg, w_pe, final_g, loss_target, m_norm_g, m_w_in, m_conv_w, m_conv_b, m_branch_g, m_w_out, m_ple_norm_g, m_w_pg, m_b_pg, m_w_pe, m_final_g, v_norm_g, v_w_in, v_conv_w, v_conv_b, v_branch_g, v_w_out, v_ple_norm_g, v_w_pg, v_b_pg, v_w_pe, v_final_g):
    s = x.shape[1]
    x0 = x.reshape(s, D_MODEL)
    target = loss_target.reshape(s, D_MODEL)
    me_blk = _my_block()

    win_s, wout_s, wpg_s, wpe_s = _cast_bf16(
        [w_in.reshape(DEPTH * D_MODEL, 512), w_out.reshape(DEPTH * 128, D_MODEL),
         w_pg.reshape(DEPTH * 128, D_MODEL), w_pe.reshape(DEPTH * PLE_DIM, 128)], "cast_weights")
    win_s, wout_s = win_s.reshape(DEPTH, D_MODEL, 512), wout_s.reshape(DEPTH, 128, D_MODEL)
    wpg_s, wpe_s = wpg_s.reshape(DEPTH, 128, D_MODEL), wpe_s.reshape(DEPTH, PLE_DIM, 128)
    cw_s = jnp.zeros((SUBLANES, LANES), F32).at[:DEPTH * 3, :HEAD_DIM].set(conv_w.reshape(DEPTH * 3, HEAD_DIM))
    bf = lambda r_, c_: jax.ShapeDtypeStruct((r_, c_), BF16)
    w_items = lambda l: [(wout_s[l], bf(D_MODEL, D_MODEL), "rows128"), (wpg_s[l], bf(D_MODEL, D_MODEL), "rows128"),
                         (wpe_s[l], bf(PLE_DIM, D_MODEL), "cols128")]
    win_f = [None] * DEPTH
    win_f[0], cw_all = _comm_call(_gather_comm([
        (win_s[0], bf(D_MODEL, N_IN), "cols512"),
        (cw_s, jax.ShapeDtypeStruct((N_DEV, SUBLANES, LANES), F32), "slot")]), "gather_w_in_0")
    cw_full = jnp.transpose(cw_all[:, :DEPTH * 3, :HEAD_DIM].reshape(N_DEV, DEPTH, 3, HEAD_DIM), (1, 2, 0, 3))
    cw_full = cw_full.reshape(DEPTH, 3, D_CONV)
    gather_rest_0 = _gather_comm(w_items(0))
    gather_win_1 = _gather_comm([(win_s[1], bf(D_MODEL, N_IN), "cols512")])
    gather_rest_1 = _gather_comm(w_items(1))

    vec = lambda a, l: a[l][None, :]

    saved = []
    xl = x0
    wout_f, wpg_f, wpe_f = [None] * DEPTH, [None] * DEPTH, [None] * DEPTH
    for l in range(DEPTH):
        (h, pc, qkv, az), got = _fwd_in(xl, vec(norm_g, l), win_f[l], f"fwd_in_{l}",
                                        comm=gather_rest_0 if l == 0 else None)
        if l == 0:
            wout_f[0], wpg_f[0], wpe_f[0] = got
        (ya, lsum, nblk), got = _attn_fwd(qkv, f"attn_fwd_{l}", comm=gather_win_1 if l == 0 else None)
        if l == 0:
            (win_f[1],) = got
        (x2, x3, gated, h2, gate, e), got = _fwd_mid(
            xl, pc, az, ya, p, l, cw_full[l], vec(conv_b, l), vec(branch_g, l), wout_f[l],
            vec(ple_norm_g, l), wpg_f[l], vec(b_pg, l), wpe_f[l], f"fwd_mid_{l}",
            comm=gather_rest_1 if l == 0 else None)
        if l == 0:
            wout_f[1], wpg_f[1], wpe_f[1] = got
        saved.append(dict(x=xl, h=h, pc=pc, qkv=qkv, az=az, ya=ya, lsum=lsum, nblk=nblk, x2=x2, gated=gated, h2=h2,
                          gate=gate, e=e))
        xl = x3

    dx, loss_acc, d_final_g = _loss_head(xl, target, final_g[None, :])

    dwin, dwout, dwpg, dwpe = [None] * DEPTH, [None] * DEPTH, [None] * DEPTH, [None] * DEPTH
    small = dict(norm_g=[None] * DEPTH, conv_b=[None] * DEPTH, branch_g=[None] * DEPTH,
                 ple_norm_g=[None] * DEPTH, b_pg=[None] * DEPTH, conv_w=[None] * DEPTH)
    slot = lambda r_, c_: jax.ShapeDtypeStruct((r_, c_), BF16)
    r_in, r_out, r_pg, r_pe = [None] * DEPTH, [None] * DEPTH, [None] * DEPTH, [None] * DEPTH

    def rest_items(l):
        return [(dwout[l], slot(128, D_MODEL), "rows128"), (dwpg[l], slot(128, D_MODEL), "rows128"),
                (dwpe[l], slot(PLE_DIM, 128), "cols128")]

    for l in reversed(range(DEPTH)):
        sv = saved[l]
        (dx2, dya, dmisc, dconv, dwout[l], dwpg[l], dwpe[l], d_bpg, d_pg, d_bg, d_cbias, d_cw) = _bwd_mid(
            dx, sv["x2"], sv["gate"], sv["e"], sv["pc"], sv["az"], sv["ya"], sv["gated"], sv["h2"], p, l,
            cw_full[l], vec(conv_b, l), vec(branch_g, l), vec(ple_norm_g, l), wpg_f[l], wout_f[l], f"bwd_mid_{l}")
        ride = None
        if l == 0:
            ride = _exchange_comm([(dwin[1], slot(D_MODEL, 512), "cols512"), *rest_items(1)])
        (dq, dk, dv), got = _attn_bwd(sv["qkv"], sv["lsum"], sv["nblk"], dya, f"attn_bwd_{l}", comm=ride)
        if l == 0:
            r_in[1], r_out[1], r_pg[1], r_pe[1] = got
        (dproj, dx, d_ng), _ = _bwd_dproj(dmisc, dconv, sv["pc"], dq, dk, dv, sv["x"], dx2, vec(norm_g, l),
                                          cw_full[l], win_f[l], f"bwd_dproj_{l}")
        ride = _exchange_comm(rest_items(0)) if l == 0 else None
        dwin[l], got = _atb(sv["h"], dproj, f"dw_in_{l}", comm=ride)
        if l == 0:
            r_out[0], r_pg[0], r_pe[0] = got
        small["norm_g"][l], small["conv_b"][l], small["branch_g"][l] = d_ng, d_cbias, d_bg
        small["ple_norm_g"][l], small["b_pg"][l], small["conv_w"][l] = d_pg, d_bpg, d_cw[:3]
    grad_x = dx.reshape(1, s, D_MODEL)

    flat = lambda parts: jnp.concatenate([a.reshape(-1) for a in parts])
    small_vec = jnp.concatenate([
        flat(small["norm_g"]), flat(small["conv_b"]), flat(small["branch_g"]), flat(small["ple_norm_g"]),
        flat(small["b_pg"]), d_final_g.reshape(-1), flat(small["conv_w"]), loss_acc.reshape(-1),
        jnp.zeros(((SMALL_ROWS - SMALL_GRAD_ROWS - 1) * LANES,), F32)]).reshape(SMALL_ROWS, LANES)
    r_in[0], r_small = _comm_call(_exchange_comm([
        (dwin[0], slot(D_MODEL, 512), "cols512"),
        (small_vec, jax.ShapeDtypeStruct((SMALL_ROWS, LANES), F32), "slot")]), "exchange_last")

    g_win, d_win, m_win, v_win = _adamw_sum8(r_in, w_in, m_w_in, v_w_in, "adamw_w_in")
    g_wout, d_wout, m_wout, v_wout = _adamw_sum8(r_out, w_out, m_w_out, v_w_out, "adamw_w_out")
    g_wpg, d_wpg, m_wpg, v_wpg = _adamw_sum8(r_pg, w_pg, m_w_pg, v_w_pg, "adamw_w_pg")
    g_wpe, d_wpe, m_wpe, v_wpe = _adamw_sum8(r_pe, w_pe, m_w_pe, v_w_pe, "adamw_w_pe")

    g_small = _sum8_small(r_small)
    repl = [(norm_g, m_norm_g, v_norm_g), (conv_b, m_conv_b, v_conv_b), (branch_g, m_branch_g, v_branch_g),
            (ple_norm_g, m_ple_norm_g, v_ple_norm_g), (b_pg, m_b_pg, v_b_pg), (final_g, m_final_g, v_final_g)]
    pack = lambda idx: jnp.concatenate([t[idx].reshape(-1) for t in repl]).reshape(SMALL_REPL_ROWS, LANES)
    g_repl = g_small[:SMALL_REPL_ROWS]
    d_repl, m_repl, v_repl = _adamw_plain(g_repl, pack(0), pack(1), pack(2), "adamw_replicated")

    def unpack(a):
        flat_a = a.reshape(-1)
        out, off = [], 0
        for t in repl:
            n = t[0].size
            out.append(flat_a[off:off + n].reshape(t[0].shape))
            off += n
        return out

    g_r, d_r, m_r, v_r = unpack(g_repl), unpack(d_repl), unpack(m_repl), unpack(v_repl)

    loss = g_small[SMALL_GRAD_ROWS, 0]
    g_cw_full = g_small[SMALL_REPL_ROWS:SMALL_GRAD_ROWS].reshape(DEPTH, 3, D_CONV)
    g_cw = lax.dynamic_slice(g_cw_full, (0, 0, me_blk * HEAD_DIM), (DEPTH, 3, HEAD_DIM))
    pad_cw = lambda a: jnp.zeros((SUBLANES, LANES), F32).at[:3].set(a.reshape(3, LANES))
    v_cw_pad = jnp.ones((SUBLANES, LANES), F32).at[:3].set(v_conv_w.reshape(3, LANES))
    d_cw, m_cw, v_cw = _adamw_plain(pad_cw(g_cw), pad_cw(conv_w), pad_cw(m_conv_w), v_cw_pad, "adamw_conv_w")
    un_cw = lambda a: a[:3].reshape(DEPTH, 3, HEAD_DIM)

    def ordered(r, win_, cw_, wout_, wpg_, wpe_):
        return [r[0], win_, cw_, r[1], r[2], wout_, r[3], wpg_, r[4], wpe_, r[5]]

    grads = ordered(g_r, g_win, g_cw, g_wout, g_wpg, g_wpe)
    deltas = ordered(d_r, d_win, un_cw(d_cw), d_wout, d_wpg, d_wpe)
    new_m = ordered(m_r, m_win, un_cw(m_cw), m_wout, m_wpg, m_wpe)
    new_v = ordered(v_r, v_win, un_cw(v_cw), v_wout, v_wpg, v_wpe)
    return (loss, grad_x, *grads, *deltas, *new_m, *new_v)
```

```python
import jax
import jax.numpy as jnp
from jax import lax
from jax.experimental import pallas as pl
from jax.experimental.pallas import tpu as pltpu

F32 = jnp.float32
BF16 = jnp.bfloat16

D_MODEL = 1024
D_CONV = 512
D_SB = 512
N_IN = 4096
HEAD_DIM = 64
PLE_DIM = 256
DEPTH = 2
EPS = 1e-6
ADAM_LR = 0.001
ADAM_B1 = 0.9
ADAM_B2 = 0.999
ADAM_EPS = 1e-08
ADAM_WD = 0.01
ADAM_STEP = 10

LANES = 128
SUBLANES = 8
VMEM_BYTES_V7X = 64 * 1024 * 1024
VMEM_LIMIT = VMEM_BYTES_V7X - 8 * 1024 * 1024

N_DEV = 8
ROW_TILE = 256
ATTN_TILE = 256
SMALL_GRAD_ROWS = 104
SMALL_REPL_ROWS = 80
SMALL_ROWS = 112

NT = (((1,), (1,)), ((), ()))
TN = (((0,), (0,)), ((), ()))


def _call(body, **kw):
    return pl.pallas_call(body, **kw)


def _params(sem=None, vmem=None):
    return pltpu.CompilerParams(dimension_semantics=sem, vmem_limit_bytes=vmem)


def _sigmoid(z):
    return 0.5 * jnp.tanh(0.5 * z) + 0.5


def _group_bcast_sum(a, lo):
    s_lo = jnp.sum(jnp.where(lo, a, 0.0), axis=-1, keepdims=True)
    s_hi = jnp.sum(jnp.where(lo, 0.0, a), axis=-1, keepdims=True)
    return jnp.where(lo, s_lo, s_hi)


def _my_block():
    return 4 * lax.axis_index("x") + 2 * lax.axis_index("y") + lax.axis_index("c")


def _cast_bf16(arrays, name):
    n = len(arrays)

    def body(*refs):
        for a_ref, o_ref in zip(refs[:n], refs[n:]):
            o_ref[...] = a_ref[...].astype(BF16)

    whole = lambda a: pl.BlockSpec(a.shape, lambda: (0, 0))
    return _call(
        body, name=name,
        out_shape=tuple(jax.ShapeDtypeStruct(a.shape, BF16) for a in arrays),
        in_specs=[whole(a) for a in arrays], out_specs=tuple(whole(a) for a in arrays),
        compiler_params=_params(None, VMEM_LIMIT),
    )(*arrays)


class _Comm:
    def __init__(self, inputs, out_shapes, scratch, begin, middle, finish):
        self.inputs, self.out_shapes, self.scratch = list(inputs), list(out_shapes), list(scratch)
        self.begin, self.middle, self.finish = begin, middle, finish


def _slab(kind, ref, blk):
    if kind == "cols512":
        return ref.at[:, pl.ds(blk * 512, 512)]
    if kind == "rows128":
        return ref.at[pl.ds(blk * 128, 128), :]
    if kind == "cols128":
        return ref.at[:, pl.ds(blk * 128, 128)]
    return ref.at[blk]


def _gather_comm(items):
    n_t = len(items)
    kinds = [it[2] for it in items]

    def ctx(ins, outs, sems):
        send_sems, recv_sems, local_sems = sems
        x, y, c = lax.axis_index("x"), lax.axis_index("y"), lax.axis_index("c")
        me, sibling = (x, y, c), (x, y, 1 - c)
        chips = [(1 - x, y), (x, 1 - y), (1 - x, 1 - y)]

        def place(t, dev):
            return _slab(kinds[t], outs[t], 4 * dev[0] + 2 * dev[1] + dev[2])

        def copy(t, k, block, to, own=False):
            return pltpu.make_async_remote_copy(
                src_ref=ins[t] if own else place(t, block), dst_ref=place(t, block),
                send_sem=send_sems.at[t, k], recv_sem=recv_sems.at[t, k],
                device_id=to, device_id_type=pl.DeviceIdType.MESH)

        mine = [pltpu.make_async_copy(ins[t], place(t, me), local_sems.at[t]) for t in range(n_t)]
        first = []
        for t in range(n_t):
            first.append(copy(t, 0, me, sibling, own=True))
            first += [copy(t, 1 + j, me, (*chip, c), own=True) for j, chip in enumerate(chips)]
        passed = [copy(t, 4 + j, (*chip, c), sibling) for j, chip in enumerate(chips) for t in range(n_t)]
        landed = [copy(t, 1 + j, (*chip, c), me) for j, chip in enumerate(chips) for t in range(n_t)]
        from_sibling = []
        for t in range(n_t):
            from_sibling.append(copy(t, 0, sibling, me))
            from_sibling += [copy(t, 4 + j, (*chip, 1 - c), me) for j, chip in enumerate(chips)]
        return mine, first, landed, passed, from_sibling

    def begin(ins, outs, sems):
        mine, first, _, _, _ = ctx(ins, outs, sems)
        for cp in mine + first:
            cp.start()

    def middle(ins, outs, sems):
        _, _, landed, passed, _ = ctx(ins, outs, sems)
        for got, fwd in zip(landed, passed):
            got.wait_recv()
            fwd.start()

    def finish(ins, outs, sems):
        mine, first, _, passed, from_sibling = ctx(ins, outs, sems)
        for cp in from_sibling:
            cp.wait_recv()
        for cp in first + passed:
            cp.wait_send()
        for cp in mine:
            cp.wait()

    scratch = [pltpu.SemaphoreType.DMA((n_t, 7)), pltpu.SemaphoreType.DMA((n_t, 7)), pltpu.SemaphoreType.DMA((n_t,))]
    return _Comm([it[0] for it in items], [it[1] for it in items], scratch, begin, middle, finish)


def _exchange_comm(items):
    n_t = len(items)
    kinds = [it[2] for it in items]

    def ctx(ins, outs, sems):
        send_sems, recv_sems, local_sems = sems
        x, y, c = lax.axis_index("x"), lax.axis_index("y"), lax.axis_index("c")
        me_blk = 4 * x + 2 * y + c

        def src(t, blk):
            return ins[t] if kinds[t] == "slot" else _slab(kinds[t], ins[t], blk)

        local = [pltpu.make_async_copy(src(t, me_blk), outs[t].at[me_blk], local_sems.at[t]) for t in range(n_t)]
        remote = []
        for k in range(1, N_DEV):
            px = 1 - x if k & 4 else x
            py = 1 - y if k & 2 else y
            pc_ = 1 - c if k & 1 else c
            for t in range(n_t):
                remote.append(pltpu.make_async_remote_copy(
                    src_ref=src(t, 4 * px + 2 * py + pc_), dst_ref=outs[t].at[me_blk],
                    send_sem=send_sems.at[k - 1, t], recv_sem=recv_sems.at[k - 1, t],
                    device_id=(px, py, pc_), device_id_type=pl.DeviceIdType.MESH))
        return local, remote

    def begin(ins, outs, sems):
        local, remote = ctx(ins, outs, sems)
        for cp in local + remote:
            cp.start()

    def finish(ins, outs, sems):
        local, remote = ctx(ins, outs, sems)
        for cp in remote:
            cp.wait_recv()
        for cp in remote:
            cp.wait_send()
        for cp in local:
            cp.wait()

    scratch = [pltpu.SemaphoreType.DMA((N_DEV - 1, n_t)), pltpu.SemaphoreType.DMA((N_DEV - 1, n_t)),
               pltpu.SemaphoreType.DMA((n_t,))]
    out_shapes = [jax.ShapeDtypeStruct((N_DEV, *it[1].shape), it[1].dtype) for it in items]
    return _Comm([it[0] for it in items], out_shapes, scratch, begin, None, finish)


def _comm_call(comm, name):
    n_in, n_out = len(comm.inputs), len(comm.out_shapes)

    def body(*refs):
        ins, outs, sems = refs[:n_in], refs[n_in:n_in + n_out], refs[n_in + n_out:]
        comm.begin(ins, outs, sems)
        if comm.middle is not None:
            comm.middle(ins, outs, sems)
        comm.finish(ins, outs, sems)

    any_spec = pl.BlockSpec(memory_space=pl.ANY)
    return _call(body, name=name, out_shape=tuple(comm.out_shapes), in_specs=[any_spec] * n_in,
                 out_specs=[any_spec] * n_out, scratch_shapes=comm.scratch)(*comm.inputs)


def _hosted(body, n_in, n_out, comm, first, last, middle):
    if comm is None:
        return lambda *refs: body(*refs)
    n_ci, n_co, n_cs = len(comm.inputs), len(comm.out_shapes), len(comm.scratch)

    def wrapped(*refs):
        ins, cin = refs[:n_in], refs[n_in:n_in + n_ci]
        o0 = n_in + n_ci
        outs, cout = refs[o0:o0 + n_out], refs[o0 + n_out:o0 + n_out + n_co]
        scr, csem = refs[o0 + n_out + n_co:len(refs) - n_cs], refs[len(refs) - n_cs:]
        pl.when(first())(lambda: comm.begin(cin, cout, csem))
        body(*ins, *outs, *scr)
        if comm.middle is not None:
            pl.when(middle())(lambda: comm.middle(cin, cout, csem))
        pl.when(last())(lambda: comm.finish(cin, cout, csem))

    return wrapped


def _hosted_call(body, comm, *, name, grid, out_shape, in_specs, out_specs, args, scratch_shapes=(), sem=None):
    nd = len(grid)
    first, last, middle = _at_first(nd), _at_last(nd), _at_middle(nd)
    if comm is not None:
        sem = ("arbitrary",) * nd
    n_in, n_out = len(in_specs), len(out_shape)
    any_spec = pl.BlockSpec(memory_space=pl.ANY)
    c_in = [] if comm is None else comm.inputs
    c_out = [] if comm is None else comm.out_shapes
    c_scr = [] if comm is None else comm.scratch
    outs = _call(
        _hosted(body, n_in, n_out, comm, first, last, middle), name=name, grid=grid,
        out_shape=(*out_shape, *c_out),
        in_specs=[*in_specs, *[any_spec] * len(c_in)],
        out_specs=(*out_specs, *[any_spec] * len(c_out)),
        scratch_shapes=[*scratch_shapes, *c_scr],
        compiler_params=_params(sem, VMEM_LIMIT),
    )(*args, *c_in)
    return outs[:n_out], outs[n_out:]


def _grid_step(ndim):
    i, n = pl.program_id(0), pl.num_programs(0)
    for d in range(1, ndim):
        i, n = i * pl.num_programs(d) + pl.program_id(d), n * pl.num_programs(d)
    return i, n


def _at_first(ndim):
    return lambda: _grid_step(ndim)[0] == 0


def _at_last(ndim):
    def pred():
        i, n = _grid_step(ndim)
        return i == n - 1
    return pred


def _at_middle(ndim):
    def pred():
        i, n = _grid_step(ndim)
        return i == (3 * n) // 4
    return pred


def _fwd_in(x, g, w_full, name, comm=None):
    s = x.shape[0]
    ts = min(ROW_TILE, s)

    def body(x_ref, g_ref, w_ref, h_ref, pc_ref, qkv_ref, az_ref):
        xf = x_ref[...]
        r = lax.rsqrt(jnp.mean(xf * xf, axis=-1, keepdims=True) + EPS)
        h = (xf * r * g_ref[...]).astype(BF16)
        h_ref[...] = h
        pc_ref[...] = jnp.dot(h, w_ref[:, 0:2048], preferred_element_type=F32).astype(BF16)
        q = jnp.dot(h, w_ref[:, 2048:2560], preferred_element_type=F32)
        qkv_ref[:, 0:512] = (q * 0.125).astype(BF16)
        qkv_ref[:, 512:1536] = jnp.dot(h, w_ref[:, 2560:3584], preferred_element_type=F32).astype(BF16)
        az_ref[...] = jnp.dot(h, w_ref[:, 3584:4096], preferred_element_type=F32).astype(BF16)

    row = lambda width: pl.BlockSpec((ts, width), lambda i: (i, 0))
    return _hosted_call(
        body, comm, name=name, grid=(s // ts,),
        out_shape=(jax.ShapeDtypeStruct((s, D_MODEL), BF16), jax.ShapeDtypeStruct((s, 2048), BF16),
                   jax.ShapeDtypeStruct((s, 1536), BF16), jax.ShapeDtypeStruct((s, 512), BF16)),
        in_specs=[row(D_MODEL), pl.BlockSpec((1, D_MODEL), lambda i: (0, 0)),
                  pl.BlockSpec((D_MODEL, N_IN), lambda i: (0, 0))],
        out_specs=(row(D_MODEL), row(2048), row(1536), row(512)),
        args=(x, g, w_full), sem=("parallel",))


ATTN_ROWS = 128
ATTN_DONE = 104.0


def _attn_pieces(tq, rc):
    lane = lax.broadcasted_iota(jnp.int32, (1, LANES), 1)
    lo = lane < HEAD_DIM
    row = lax.broadcasted_iota(jnp.int32, (tq, tq), 0)
    col = lax.broadcasted_iota(jnp.int32, (tq, tq), 1)
    tri_gt = jnp.where(row > col, 1.0, 0.0).astype(BF16)
    tri_le = jnp.where(row <= col, 1.0, 0.0).astype(BF16)
    rrow = lax.broadcasted_iota(jnp.int32, (rc, tq), 0)
    rcol = lax.broadcasted_iota(jnp.int32, (rc, tq), 1)
    causal = [rcol < rrow + r * rc for r in range(tq // rc)]
    return lo, causal, tri_gt, tri_le


def _split_heads(a, lo):
    z = jnp.zeros_like(a)
    return (jnp.where(lo, a, z), jnp.where(lo, z, a))


def _softplus(z, causal, diag):
    neg_abs = lax.bitcast_convert_type(lax.bitcast_convert_type(z, jnp.uint32) | jnp.uint32(0x80000000), F32)
    sp = jnp.maximum(z, 0.0) + jnp.log(1.0 + jnp.exp(neg_abs))
    if diag:
        sp = jnp.where(causal, sp, 0.0)
    return sp


def _attn_fwd(qkv, name, comm=None):
    s = qkv.shape[0]
    tq = min(ATTN_TILE, s)
    nq = s // tq
    rc = min(ATTN_ROWS, tq)
    n_rc = tq // rc
    chains = [(r, hh) for r in range(n_rc) for hh in range(2)]

    def body(q_ref, k_ref, v_ref, o_ref, lsum_ref, nblk_ref):
        hp, qi = pl.program_id(0), pl.program_id(1)
        lo, causal, tri_gt, _ = _attn_pieces(tq, rc)
        qh = _split_heads(q_ref[...], lo)
        qc = {(r, hh): qh[hh][r * rc:(r + 1) * rc] for r, hh in chains}

        mm = lambda a_, b_: jnp.dot(a_.astype(BF16), b_, preferred_element_type=F32)
        rowsum = lambda a_: jnp.sum(a_, axis=-1, keepdims=True)

        def block(kb, carry):
            start = pl.multiple_of(kb * tq, tq)
            k = k_ref[pl.ds(start, tq), :]
            vh = _split_heads(v_ref[pl.ds(start, tq), :], lo)
            z = {ch: lax.dot_general(qc[ch], k, NT, preferred_element_type=F32) for ch in chains}
            sp = {ch: _softplus(z[ch], None, False) for ch in chains}
            later = {ch: mm(sp[ch], tri_gt) for ch in chains}
            a = {ch: jnp.exp((z[ch] - sp[ch]) - (carry[ch[0]][1 + ch[1]] + later[ch])) for ch in chains}
            pv = {ch: mm(a[ch], vh[ch[1]]) for ch in chains}
            return tuple((carry[r][0] + pv[(r, 0)] + pv[(r, 1)],
                          carry[r][1] + rowsum(sp[(r, 0)]), carry[r][2] + rowsum(sp[(r, 1)])) for r in range(n_rc))

        def first_two(prev_ok):
            d0 = pl.multiple_of(qi * tq, tq)
            p0 = pl.multiple_of(jnp.maximum(qi - 1, 0) * tq, tq)
            k_d, k_p = k_ref[pl.ds(d0, tq), :], k_ref[pl.ds(p0, tq), :]
            vh_d = _split_heads(v_ref[pl.ds(d0, tq), :], lo)
            vh_p = _split_heads(v_ref[pl.ds(p0, tq), :], lo)
            z_d = {ch: lax.dot_general(qc[ch], k_d, NT, preferred_element_type=F32) for ch in chains}
            z_p = {ch: lax.dot_general(qc[ch], k_p, NT, preferred_element_type=F32) for ch in chains}
            sp_d = {ch: _softplus(z_d[ch], causal[ch[0]], True) for ch in chains}
            sp_raw = {ch: _softplus(z_p[ch], None, False) for ch in chains}
            sp_p = {ch: jnp.where(prev_ok, sp_raw[ch], 0.0) for ch in chains}
            later_d = {ch: mm(sp_d[ch], tri_gt) for ch in chains}
            later_p = {ch: mm(sp_p[ch], tri_gt) for ch in chains}
            c_d = {ch: rowsum(sp_d[ch]) for ch in chains}
            a_d = {ch: jnp.where(causal[ch[0]], jnp.exp((z_d[ch] - sp_d[ch]) - later_d[ch]), 0.0) for ch in chains}
            a_p = {ch: jnp.where(prev_ok, jnp.exp((z_p[ch] - sp_raw[ch]) - (c_d[ch] + later_p[ch])), 0.0)
                   for ch in chains}
            pv = {ch: mm(a_d[ch], vh_d[ch[1]]) + mm(a_p[ch], vh_p[ch[1]]) for ch in chains}
            return tuple((pv[(r, 0)] + pv[(r, 1)],
                          c_d[(r, 0)] + rowsum(sp_p[(r, 0)]), c_d[(r, 1)] + rowsum(sp_p[(r, 1)]))
                         for r in range(n_rc))

        def least(carry):
            m = jnp.minimum(carry[0][1], carry[0][2])
            for r in range(1, n_rc):
                m = jnp.minimum(m, jnp.minimum(carry[r][1], carry[r][2]))
            return jnp.min(m)

        carry = first_two(qi > 0)

        def go_on(st):
            return jnp.logical_and(st[0] < qi - 1, st[1] < ATTN_DONE)

        def step(st):
            new = block(qi - 2 - st[0], st[2])
            return st[0] + 1, least(new), new

        walked, _, carry = lax.while_loop(go_on, step, (jnp.int32(0), least(carry), carry))
        for r in range(n_rc):
            o_ref[r * rc:(r + 1) * rc, :] = carry[r][0].astype(BF16)
            lsum_ref[r * rc:(r + 1) * rc, :] = jnp.where(lo, carry[r][1], carry[r][2])
        nblk_ref[hp, qi] = walked.astype(F32)

    blk = pl.BlockSpec((tq, LANES), lambda hp, qi: (qi, hp))
    o512 = jax.ShapeDtypeStruct((s, D_SB), F32)
    return _hosted_call(
        body, comm, name=name, grid=(4, nq),
        out_shape=(jax.ShapeDtypeStruct((s, D_SB), BF16), o512, jax.ShapeDtypeStruct((4, nq), F32)),
        in_specs=[blk, pl.BlockSpec((s, LANES), lambda hp, qi: (0, 4 + hp)),
                  pl.BlockSpec((s, LANES), lambda hp, qi: (0, 8 + hp))],
        out_specs=(blk, blk, pl.BlockSpec(memory_space=pltpu.SMEM)),
        args=(qkv, qkv, qkv), sem=("arbitrary", "arbitrary"))


HALO = 16


def _conv_taps(cc_ref, ch_ref, ccp_ref, chp_ref, halo_ref, first):
    u = cc_ref[...].astype(F32) * ch_ref[...].astype(F32)
    halo_ref[...] = ccp_ref[...].astype(F32) * chp_ref[...].astype(F32) * jnp.where(first, 0.0, 1.0)
    p6 = halo_ref[HALO - 2:HALO - 1, :]
    p7 = halo_ref[HALO - 1:HALO, :]
    rowi = lax.broadcasted_iota(jnp.int32, u.shape, 0)
    u1 = jnp.where(rowi == 0, p7, pltpu.roll(u, 1, 0))
    u2 = jnp.where(rowi == 0, p6, jnp.where(rowi == 1, p7, pltpu.roll(u, 2, 0)))
    return u, u1, u2


def _fwd_mid(x, pc, az, ya, p4, layer, cw, cb, bg, wout_full, pg, wpg_full, bpg, wpe_full, name, comm=None):
    s = x.shape[0]
    ts = min(ROW_TILE, s)
    blk_h = ts // HALO

    def body(x_ref, cb_ref_, cc_ref, ch_ref, cz_ref, ccp_ref, chp_ref, az_ref, ya_ref, p_ref,
             cw_ref, cbias_ref, bg_ref, wout_ref, pg_ref, wpg_ref, bpg_ref, wpe_ref,
             x2_ref, x3_ref, gated_ref, h2_ref, gate_ref, e_ref, halo_ref):
        i = pl.program_id(0)
        lane = lax.broadcasted_iota(jnp.int32, (1, LANES), 1)
        lo = lane < HEAD_DIM
        u, u1, u2 = _conv_taps(cc_ref, ch_ref, ccp_ref, chp_ref, halo_ref, i == 0)
        conv = cbias_ref[...] + cw_ref[0:1, :] * u2 + cw_ref[1:2, :] * u1 + cw_ref[2:3, :] * u
        yc = cb_ref_[...].astype(F32) * conv
        for sl in range(8):
            cols = slice(LANES * (sl % 4), LANES * (sl % 4 + 1))
            y = yc[:, cols] if sl < 4 else ya_ref[:, cols].astype(F32)
            zc = (cz_ref[:, cols] if sl < 4 else az_ref[:, cols]).astype(F32)
            rg = lax.rsqrt(_group_bcast_sum(y * y, lo) * (1.0 / HEAD_DIM) + EPS)
            yn = y * rg * bg_ref[:, LANES * sl:LANES * (sl + 1)]
            gated_ref[:, LANES * sl:LANES * (sl + 1)] = (yn * (zc * _sigmoid(zc))).astype(BF16)
        x2 = x_ref[...] + jnp.dot(gated_ref[...], wout_ref[...], preferred_element_type=F32)
        x2_ref[...] = x2
        r2 = lax.rsqrt(jnp.mean(x2 * x2, axis=-1, keepdims=True) + EPS)
        h2 = (x2 * r2 * pg_ref[...]).astype(BF16)
        h2_ref[...] = h2
        gate = _sigmoid(jnp.dot(h2, wpg_ref[...], preferred_element_type=F32) + bpg_ref[...])
        gate_ref[...] = gate.astype(BF16)
        e = jnp.dot(p_ref[...].astype(BF16), wpe_ref[...], preferred_element_type=F32)
        e_ref[...] = e.astype(BF16)
        x3_ref[...] = x2 + gate * e

    row = lambda width, cb_=0: pl.BlockSpec((ts, width), lambda i: (i, cb_))
    prev = lambda cb_: pl.BlockSpec((HALO, 512), lambda i: (jnp.maximum(i * blk_h - 1, 0), cb_))
    vec = lambda width: pl.BlockSpec((1, width), lambda i: (0, 0))
    wspec = lambda r_, c_: pl.BlockSpec((r_, c_), lambda i: (0, 0))
    f32o = jax.ShapeDtypeStruct((s, D_MODEL), F32)
    bfo = jax.ShapeDtypeStruct((s, D_MODEL), BF16)
    return _hosted_call(
        body, comm, name=name, grid=(s // ts,),
        out_shape=(f32o, f32o, bfo, bfo, bfo, bfo),
        scratch_shapes=[pltpu.VMEM((HALO, 512), F32)],
        in_specs=[row(D_MODEL), row(512, 0), row(512, 1), row(512, 2), row(512, 3), prev(1), prev(2),
                  row(512), row(512),
                  pl.BlockSpec((None, None, ts, PLE_DIM), lambda i: (layer, 0, i, 0)),
                  pl.BlockSpec((3, 512), lambda i: (0, 0)), vec(512), vec(D_MODEL),
                  wspec(D_MODEL, D_MODEL), vec(D_MODEL), wspec(D_MODEL, D_MODEL), vec(D_MODEL),
                  wspec(PLE_DIM, D_MODEL)],
        out_specs=(row(D_MODEL),) * 6,
        args=(x, pc, pc, pc, pc, pc, pc, az, ya, p4, cw, cb, bg, wout_full, pg, wpg_full, bpg, wpe_full),
        sem=("parallel",))


def _loss_head(xf, target, fg):
    s = xf.shape[0]
    ts = min(ROW_TILE, s)

    def body(x_ref, t_ref, g_ref, dx_ref, loss_ref, dg_ref):
        i = pl.program_id(0)

        @pl.when(i == 0)
        def _():
            loss_ref[...] = jnp.zeros_like(loss_ref)
            dg_ref[...] = jnp.zeros_like(dg_ref)

        x = x_ref[...]
        g = g_ref[...]
        r = lax.rsqrt(jnp.mean(x * x, axis=-1, keepdims=True) + EPS)
        xn = x * r
        err = xn * g - t_ref[...]
        per_row = jnp.sum(err * err, axis=-1, keepdims=True)
        loss_ref[...] += jnp.sum(per_row, axis=0, keepdims=True) * (0.5 / D_MODEL)
        dy = err * (1.0 / D_MODEL)
        dg_ref[...] += jnp.sum(dy * xn, axis=0, keepdims=True)
        dxn = dy * g
        dx_ref[...] = r * (dxn - xn * jnp.mean(dxn * xn, axis=-1, keepdims=True))

    row = pl.BlockSpec((ts, D_MODEL), lambda i: (i, 0))
    return _call(
        body, name="loss_head", grid=(s // ts,),
        out_shape=(jax.ShapeDtypeStruct((s, D_MODEL), F32), jax.ShapeDtypeStruct((1, LANES), F32),
                   jax.ShapeDtypeStruct((1, D_MODEL), F32)),
        in_specs=[row, row, pl.BlockSpec((1, D_MODEL), lambda i: (0, 0))],
        out_specs=(row, pl.BlockSpec((1, LANES), lambda i: (0, 0)), pl.BlockSpec((1, D_MODEL), lambda i: (0, 0))),
        compiler_params=_params(("arbitrary",), VMEM_LIMIT),
    )(xf, target, fg)


def _bwd_mid(dx3, x2, gate, e, pc, az, ya, gated, h2, p4, layer, cw, cb, bg, pg, wpg_full, wout_full, name):
    s = x2.shape[0]
    ts = min(ROW_TILE, s)
    blk_h = ts // HALO

    def body(dx3_ref, x2_ref, gate_ref, e_ref, cb_ref_, cc_ref, ch_ref, cz_ref, ccp_ref, chp_ref, az_ref, ya_ref,
             gated_ref, h2_ref, p_ref, cw_ref, cbias_ref, bg_ref, pg_ref, wpg_ref, wout_ref,
             dx2_ref, dya_ref, dmisc_ref, dconv_ref, dwout_ref, dwpg_ref, dwpe_ref,
             dbpg_ref, dpg_ref, dbg_ref, dcbias_ref, dcw_ref,
             dgated_ref, halo_ref, acc_out, acc_pg, acc_pe):
        i = pl.program_id(0)

        @pl.when(i == 0)
        def _():
            for ref in (dbpg_ref, dpg_ref, dbg_ref, dcbias_ref, dcw_ref, acc_out, acc_pg, acc_pe):
                ref[...] = jnp.zeros_like(ref)

        lane = lax.broadcasted_iota(jnp.int32, (1, LANES), 1)
        lo = lane < HEAD_DIM
        dx3 = dx3_ref[...]
        gate = gate_ref[...].astype(F32)
        de_b = (dx3 * gate).astype(BF16)
        dgpre = dx3 * e_ref[...].astype(F32) * gate * (1.0 - gate)
        dbpg_ref[...] += jnp.sum(dgpre, axis=0, keepdims=True)
        dgpre_b = dgpre.astype(BF16)
        dh2 = lax.dot_general(dgpre_b, wpg_ref[...], NT, preferred_element_type=F32)
        acc_pe[...] += lax.dot_general(p_ref[...].astype(BF16), de_b, TN, preferred_element_type=F32)
        acc_pg[...] += lax.dot_general(h2_ref[...], dgpre_b, TN, preferred_element_type=F32)

        u, u1, u2 = _conv_taps(cc_ref, ch_ref, ccp_ref, chp_ref, halo_ref, i == 0)
        conv = cbias_ref[...] + cw_ref[0:1, :] * u2 + cw_ref[1:2, :] * u1 + cw_ref[2:3, :] * u
        c_b = cb_ref_[...].astype(F32)
        yc = c_b * conv
        fwd = []
        for sl in range(8):
            cols = slice(LANES * (sl % 4), LANES * (sl % 4 + 1))
            y = yc[:, cols] if sl < 4 else ya_ref[:, cols].astype(F32)
            zc = (cz_ref[:, cols] if sl < 4 else az_ref[:, cols]).astype(F32)
            rg = lax.rsqrt(_group_bcast_sum(y * y, lo) * (1.0 / HEAD_DIM) + EPS)
            sig = _sigmoid(zc)
            fwd.append((rg, y * rg, zc * sig, sig * (1.0 + zc * (1.0 - sig))))

        x2 = x2_ref[...]
        r2 = lax.rsqrt(jnp.mean(x2 * x2, axis=-1, keepdims=True) + EPS)
        xn2 = x2 * r2
        dpg_ref[...] += jnp.sum(dh2 * xn2, axis=0, keepdims=True)
        dxn = dh2 * pg_ref[...]
        dx2 = dx3 + r2 * (dxn - xn2 * jnp.mean(dxn * xn2, axis=-1, keepdims=True))
        dx2_ref[...] = dx2
        dx2_b = dx2.astype(BF16)
        dgated_ref[...] = lax.dot_general(dx2_b, wout_ref[...], NT, preferred_element_type=F32)
        acc_out[...] += lax.dot_general(gated_ref[...], dx2_b, TN, preferred_element_type=F32)

        for sl in range(8):
            cols = slice(LANES * (sl % 4), LANES * (sl % 4 + 1))
            wide = slice(LANES * sl, LANES * (sl + 1))
            rg, yhat, silu, dsilu = fwd[sl]
            bgs = bg_ref[:, wide]
            dgt = dgated_ref[:, wide]
            dyn = dgt * silu
            dzc = dgt * (yhat * bgs) * dsilu
            dbg_ref[:, wide] += jnp.sum(dyn * yhat, axis=0, keepdims=True)
            dyh = dyn * bgs
            dy = rg * (dyh - yhat * (_group_bcast_sum(dyh * yhat, lo) * (1.0 / HEAD_DIM)))
            if sl < 4:
                dconv = dy * c_b[:, cols]
                dmisc_ref[:, cols] = (dy * conv[:, cols]).astype(BF16)
                dmisc_ref[:, 512 + LANES * sl:512 + LANES * (sl + 1)] = dzc.astype(BF16)
                dconv_ref[:, cols] = dconv
                dcbias_ref[:, cols] += jnp.sum(dconv, axis=0, keepdims=True)
                dcw_ref[0:1, cols] += jnp.sum(dconv * u2[:, cols], axis=0, keepdims=True)
                dcw_ref[1:2, cols] += jnp.sum(dconv * u1[:, cols], axis=0, keepdims=True)
                dcw_ref[2:3, cols] += jnp.sum(dconv * u[:, cols], axis=0, keepdims=True)
            else:
                dya_ref[:, cols] = dy.astype(BF16)
                dmisc_ref[:, 1024 + LANES * (sl - 4):1024 + LANES * (sl - 3)] = dzc.astype(BF16)

        @pl.when(i == pl.num_programs(0) - 1)
        def _():
            dwout_ref[...] = acc_out[...].astype(BF16)
            dwpg_ref[...] = acc_pg[...].astype(BF16)
            dwpe_ref[...] = acc_pe[...].astype(BF16)

    row = lambda width, cb_=0: pl.BlockSpec((ts, width), lambda i: (i, cb_))
    prev = lambda cb_: pl.BlockSpec((HALO, 512), lambda i: (jnp.maximum(i * blk_h - 1, 0), cb_))
    vec = lambda width: pl.BlockSpec((1, width), lambda i: (0, 0))
    wspec = lambda r_, c_: pl.BlockSpec((r_, c_), lambda i: (0, 0))
    vo = lambda width: jax.ShapeDtypeStruct((1, width), F32)
    sq = jax.ShapeDtypeStruct((D_MODEL, D_MODEL), BF16)
    return _call(
        body, name=name, grid=(s // ts,),
        out_shape=(jax.ShapeDtypeStruct((s, D_MODEL), F32), jax.ShapeDtypeStruct((s, 512), BF16),
                   jax.ShapeDtypeStruct((s, 1536), BF16), jax.ShapeDtypeStruct((s, 512), F32),
                   sq, sq, jax.ShapeDtypeStruct((PLE_DIM, D_MODEL), BF16),
                   vo(D_MODEL), vo(D_MODEL), vo(D_MODEL), vo(512), jax.ShapeDtypeStruct((SUBLANES, 512), F32)),
        in_specs=[row(D_MODEL), row(D_MODEL), row(D_MODEL), row(D_MODEL),
                  row(512, 0), row(512, 1), row(512, 2), row(512, 3), prev(1), prev(2), row(512), row(512),
                  row(D_MODEL), row(D_MODEL),
                  pl.BlockSpec((None, None, ts, PLE_DIM), lambda i: (layer, 0, i, 0)),
                  pl.BlockSpec((3, 512), lambda i: (0, 0)), vec(512), vec(D_MODEL), vec(D_MODEL),
                  wspec(D_MODEL, D_MODEL), wspec(D_MODEL, D_MODEL)],
        out_specs=(row(D_MODEL), row(512), row(1536), row(512),
                   wspec(D_MODEL, D_MODEL), wspec(D_MODEL, D_MODEL), wspec(PLE_DIM, D_MODEL),
                   vec(D_MODEL), vec(D_MODEL), vec(D_MODEL), vec(512),
                   pl.BlockSpec((SUBLANES, 512), lambda i: (0, 0))),
        scratch_shapes=[pltpu.VMEM((ts, D_MODEL), F32), pltpu.VMEM((HALO, 512), F32),
                        pltpu.VMEM((D_MODEL, D_MODEL), F32), pltpu.VMEM((D_MODEL, D_MODEL), F32),
                        pltpu.VMEM((PLE_DIM, D_MODEL), F32)],
        compiler_params=_params(("arbitrary",), VMEM_LIMIT),
    )(dx3, x2, gate, e, pc, pc, pc, pc, pc, pc, az, ya, gated, h2, p4, cw, cb, bg, pg, wpg_full, wout_full)


def _attn_bwd(qkv, lsum, nblk, dya, name, comm=None):
    s = qkv.shape[0]
    tq = min(ATTN_TILE, s)
    nq = s // tq
    rc = min(ATTN_ROWS, tq)
    n_rc = tq // rc
    chains = [(r, hh) for r in range(n_rc) for hh in range(2)]

    def body(nblk_ref, q_ref, k_ref, v_ref, lsum_ref, do_ref, dq_ref, dk_ref, dv_ref, dk_acc, dv_acc):
        hp, qi = pl.program_id(0), pl.program_id(1)

        @pl.when(qi == 0)
        def _():
            dk_acc[...] = jnp.zeros_like(dk_acc)
            dv_acc[...] = jnp.zeros_like(dv_acc)

        lo, causal, tri_gt, tri_le = _attn_pieces(tq, rc)
        lane = lax.broadcasted_iota(jnp.int32, (1, LANES), 1)
        qh = _split_heads(q_ref[...], lo)
        doh = _split_heads(do_ref[...].astype(BF16), lo)
        lt = lsum_ref[...]
        ltot_h = (jnp.sum(jnp.where(lane == 0, lt, 0.0), axis=-1, keepdims=True),
                  jnp.sum(jnp.where(lane == HEAD_DIM, lt, 0.0), axis=-1, keepdims=True))
        rows = lambda a_, r: a_[r * rc:(r + 1) * rc]
        qc = {(r, hh): rows(qh[hh], r) for r, hh in chains}
        doc = {(r, hh): rows(doh[hh], r) for r, hh in chains}
        ltot = {(r, hh): rows(ltot_h[hh], r) for r, hh in chains}

        mm = lambda a_, b_: jnp.dot(a_.astype(BF16), b_, preferred_element_type=F32)
        mm_nt = lambda a_, b_: lax.dot_general(a_, b_, NT, preferred_element_type=F32)
        mm_tn = lambda a_, b_: lax.dot_general(a_.astype(BF16), b_, TN, preferred_element_type=F32)
        rowsum = lambda a_: jnp.sum(a_, axis=-1, keepdims=True)

        def block(kb, carry, diag=False):
            start = pl.multiple_of(kb * tq, tq)
            k = k_ref[pl.ds(start, tq), :]
            v = v_ref[pl.ds(start, tq), :]
            kh = _split_heads(k, lo)
            keep = (lambda ch, a_: jnp.where(causal[ch[0]], a_, 0.0)) if diag else (lambda ch, a_: a_)
            z = {ch: mm_nt(qc[ch], k) for ch in chains}
            da = {ch: mm_nt(doc[ch], v) for ch in chains}
            sp = {ch: _softplus(z[ch], causal[ch[0]], diag) for ch in chains}
            later = {ch: mm(sp[ch], tri_gt) for ch in chains}
            walked = {ch: carry[ch[0]][1 + ch[1]] + rowsum(sp[ch]) for ch in chains}
            a = {ch: keep(ch, jnp.exp((z[ch] - sp[ch]) - ((ltot[ch] - walked[ch]) + later[ch]))) for ch in chains}
            g = {ch: a[ch] * da[ch] for ch in chains}
            upto = {ch: mm(g[ch], tri_le) for ch in chains}
            dz = {ch: keep(ch, g[ch] - jnp.exp(z[ch] - sp[ch]) * (carry[ch[0]][3 + ch[1]] + upto[ch])).astype(BF16)
                  for ch in chains}
            dqc = {ch: mm(dz[ch], kh[ch[1]]) for ch in chains}
            dkc = [mm_tn(dz[ch], qc[ch]) for ch in chains]
            dvc = [mm_tn(a[ch], doc[ch]) for ch in chains]
            dk_acc[pl.ds(start, tq), :] += sum(dkc[1:], dkc[0])
            dv_acc[pl.ds(start, tq), :] += sum(dvc[1:], dvc[0])
            return tuple((carry[r][0] + dqc[(r, 0)] + dqc[(r, 1)], walked[(r, 0)], walked[(r, 1)],
                          carry[r][3] + rowsum(g[(r, 0)]), carry[r][4] + rowsum(g[(r, 1)])) for r in range(n_rc))

        zc = jnp.zeros((rc, 1), F32)
        carry = tuple((jnp.zeros((rc, LANES), F32), zc, zc, zc, zc) for _ in range(n_rc))
        near = jnp.maximum(qi - 1, 0)
        first = near - jnp.clip(nblk_ref[hp, qi].astype(jnp.int32), 0, near)
        carry = lax.fori_loop(first, qi, block, carry)
        carry = block(qi, carry, True)
        for r in range(n_rc):
            dq_ref[r * rc:(r + 1) * rc, :] = (carry[r][0] * 0.125).astype(BF16)

        @pl.when(qi == pl.num_programs(1) - 1)
        def _():
            dk_ref[...] = dk_acc[...].astype(BF16)
            dv_ref[...] = dv_acc[...].astype(BF16)

    blk = pl.BlockSpec((tq, LANES), lambda hp, qi: (qi, hp))
    col = pl.BlockSpec((s, LANES), lambda hp, qi: (0, hp))
    o512 = jax.ShapeDtypeStruct((s, D_SB), BF16)
    return _hosted_call(
        body, comm, name=name, grid=(4, nq),
        out_shape=(o512, o512, o512),
        in_specs=[pl.BlockSpec(memory_space=pltpu.SMEM), blk,
                  pl.BlockSpec((s, LANES), lambda hp, qi: (0, 4 + hp)),
                  pl.BlockSpec((s, LANES), lambda hp, qi: (0, 8 + hp)), blk, blk],
        out_specs=(blk, col, col),
        scratch_shapes=[pltpu.VMEM((s, LANES), F32), pltpu.VMEM((s, LANES), F32)],
        args=(nblk, qkv, qkv, qkv, lsum, dya), sem=("parallel", "arbitrary"))


def _bwd_dproj(dmisc, dconv, pc, dq, dk, dv, x, dx2, g, cw, win_full, name, comm=None):
    s = x.shape[0]
    ts = min(ROW_TILE, s)
    blk8 = ts // SUBLANES
    last8 = s // SUBLANES - 1

    def body(dcb_ref, dcz_ref, daz_ref, dconv_ref, nxt_ref, cc_ref, ch_ref, dq_ref, dk_ref, dv_ref,
             x_ref, dx2_ref, g_ref, cw_ref, w_ref, dproj_ref, dx_ref, dg_ref):
        i = pl.program_id(0)

        @pl.when(i == 0)
        def _():
            dg_ref[...] = jnp.zeros_like(dg_ref)

        keep = jnp.where(i == pl.num_programs(0) - 1, 0.0, 1.0)
        dc = dconv_ref[...]
        n0 = nxt_ref[0:1, :] * keep
        n1 = nxt_ref[1:2, :] * keep
        rowi = lax.broadcasted_iota(jnp.int32, dc.shape, 0)
        dc1 = jnp.where(rowi == ts - 1, n0, pltpu.roll(dc, ts - 1, 0))
        dc2 = jnp.where(rowi == ts - 2, n0, jnp.where(rowi == ts - 1, n1, pltpu.roll(dc, ts - 2, 0)))
        du = cw_ref[2:3, :] * dc + cw_ref[1:2, :] * dc1 + cw_ref[0:1, :] * dc2
        dproj_ref[:, 0:512] = dcb_ref[...]
        dproj_ref[:, 512:1024] = (du * ch_ref[...].astype(F32)).astype(BF16)
        dproj_ref[:, 1024:1536] = (du * cc_ref[...].astype(F32)).astype(BF16)
        dproj_ref[:, 1536:2048] = dcz_ref[...]
        dproj_ref[:, 2048:2560] = dq_ref[...]
        dproj_ref[:, 2560:3072] = dk_ref[...]
        dproj_ref[:, 3072:3584] = dv_ref[...]
        dproj_ref[:, 3584:4096] = daz_ref[...]
        dh = lax.dot_general(dproj_ref[...], w_ref[...], NT, preferred_element_type=F32)
        x = x_ref[...]
        r = lax.rsqrt(jnp.mean(x * x, axis=-1, keepdims=True) + EPS)
        xn = x * r
        dg_ref[...] += jnp.sum(dh * xn, axis=0, keepdims=True)
        dxn = dh * g_ref[...]
        dx_ref[...] = dx2_ref[...] + r * (dxn - xn * jnp.mean(dxn * xn, axis=-1, keepdims=True))

    row = lambda width, cb_=0: pl.BlockSpec((ts, width), lambda i: (i, cb_))
    nxt = pl.BlockSpec((SUBLANES, 512), lambda i: (jnp.minimum((i + 1) * blk8, last8), 0))
    vec = lambda width: pl.BlockSpec((1, width), lambda i: (0, 0))
    return _hosted_call(
        body, comm, name=name, grid=(s // ts,),
        out_shape=(jax.ShapeDtypeStruct((s, N_IN), BF16), jax.ShapeDtypeStruct((s, D_MODEL), F32),
                   jax.ShapeDtypeStruct((1, D_MODEL), F32)),
        in_specs=[row(512, 0), row(512, 1), row(512, 2), row(512), nxt, row(512, 1), row(512, 2),
                  row(512), row(512), row(512), row(D_MODEL), row(D_MODEL), vec(D_MODEL),
                  pl.BlockSpec((3, 512), lambda i: (0, 0)),
                  pl.BlockSpec((D_MODEL, N_IN), lambda i: (0, 0))],
        out_specs=(row(N_IN), row(D_MODEL), vec(D_MODEL)),
        args=(dmisc, dmisc, dmisc, dconv, dconv, pc, pc, dq, dk, dv, x, dx2, g, cw, win_full),
        sem=("arbitrary",))


def _atb(a, b, name, a_index=None, comm=None):
    s, n = b.shape
    m = a.shape[-1]
    ts = min(512, s)
    tn = min(2048, n)
    if a_index is None:
        a_spec = pl.BlockSpec((ts, m), lambda j, i: (i, 0))
    else:
        a_spec = pl.BlockSpec((None, None, ts, m), lambda j, i: (a_index, 0, i, 0))

    def body(a_ref, b_ref, o_ref, acc_ref):
        i = pl.program_id(1)

        @pl.when(i == 0)
        def _():
            acc_ref[...] = jnp.zeros_like(acc_ref)

        acc_ref[...] += lax.dot_general(a_ref[...].astype(BF16), b_ref[...], TN, preferred_element_type=F32)

        @pl.when(i == pl.num_programs(1) - 1)
        def _():
            o_ref[...] = acc_ref[...].astype(BF16)

    (out,), got = _hosted_call(
        body, comm, name=name, grid=(n // tn, s // ts),
        out_shape=(jax.ShapeDtypeStruct((m, n), BF16),),
        in_specs=[a_spec, pl.BlockSpec((ts, tn), lambda j, i: (i, j))],
        out_specs=(pl.BlockSpec((m, tn), lambda j, i: (0, j)),),
        scratch_shapes=[pltpu.VMEM((m, tn), F32)],
        args=(a, b), sem=("parallel", "arbitrary"))
    return out, got


def _adamw_math(w, g, m, v):
    m2 = ADAM_B1 * m + (1.0 - ADAM_B1) * g
    v2 = ADAM_B2 * v + (1.0 - ADAM_B2) * (g * g)
    m_hat = m2 / (1.0 - ADAM_B1 ** ADAM_STEP)
    v_hat = v2 / (1.0 - ADAM_B2 ** ADAM_STEP)
    delta = -ADAM_LR * (m_hat / (jnp.sqrt(v_hat) + ADAM_EPS) + ADAM_WD * w)
    return delta, m2, v2


def _adamw_sum8(parts, w, m, v, name):
    _, rows, cols = w.shape
    tr = min(rows, 256)
    n_tiles = rows // tr
    assert len(parts) == DEPTH == 2

    def body(p0_ref, p1_ref, w_ref, m_ref, v_ref, g_ref, d_ref, m2_ref, v2_ref):
        def run(p_ref):
            g = p_ref[0].astype(F32)
            for d in range(1, N_DEV):
                g = g + p_ref[d].astype(F32)
            g_ref[...] = g
            d_ref[...], m2_ref[...], v2_ref[...] = _adamw_math(w_ref[...], g, m_ref[...], v_ref[...])

        pl.when(pl.program_id(0) == 0)(lambda: run(p0_ref))
        pl.when(pl.program_id(0) == 1)(lambda: run(p1_ref))

    part0 = pl.BlockSpec((N_DEV, tr, cols), lambda l, i: (0, jnp.where(l == 0, i, n_tiles - 1), 0))
    part1 = pl.BlockSpec((N_DEV, tr, cols), lambda l, i: (0, jnp.where(l == 1, i, 0), 0))
    tile = pl.BlockSpec((None, tr, cols), lambda l, i: (l, i, 0))
    o = jax.ShapeDtypeStruct((DEPTH, rows, cols), F32)
    return _call(
        body, name=name, grid=(DEPTH, n_tiles),
        out_shape=(o, o, o, o),
        in_specs=[part0, part1, tile, tile, tile],
        out_specs=(tile, tile, tile, tile),
        compiler_params=_params(("arbitrary", "arbitrary"), VMEM_LIMIT),
    )(parts[0], parts[1], w, m, v)


def _adamw_plain(g, w, m, v, name):
    rows, cols = g.shape

    def body(g_ref, w_ref, m_ref, v_ref, d_ref, m2_ref, v2_ref):
        d_ref[...], m2_ref[...], v2_ref[...] = _adamw_math(w_ref[...], g_ref[...], m_ref[...], v_ref[...])

    full = pl.BlockSpec((rows, cols), lambda: (0, 0))
    o = jax.ShapeDtypeStruct((rows, cols), F32)
    return _call(body, name=name, out_shape=(o, o, o), in_specs=[full] * 4, out_specs=(full,) * 3)(g, w, m, v)


def _sum8_small(parts):
    def body(p_ref, g_ref):
        g = p_ref[0]
        for d in range(1, N_DEV):
            g = g + p_ref[d]
        g_ref[...] = g

    return _call(
        body, name="sum_small_grads",
        out_shape=jax.ShapeDtypeStruct((SMALL_ROWS, LANES), F32),
        in_specs=[pl.BlockSpec((N_DEV, SMALL_ROWS, LANES), lambda: (0, 0, 0))],
        out_specs=pl.BlockSpec((SMALL_ROWS, LANES), lambda: (0, 0)),
    )(parts)


def kernel(x, p, norm_g, w_in, conv_w, conv_b, branch_g, w_out, ple_norm_g, w_pg, b_pg, w_pe, final_g, loss_target, m_norm_g, m_w_in, m_conv_w, m_conv_b, m_branch_g, m_w_out, m_ple_norm_g, m_w_pg, m_b_pg, m_w_pe, m_final_g, v_norm_g, v_w_in, v_conv_w, v_conv_b, v_branch_g, v_w_out, v_ple_norm_g, v_w_pg, v_b_pg, v_w_pe, v_final_g):
    s = x.shape[1]
    x0 = x.reshape(s, D_MODEL)
    target = loss_target.reshape(s, D_MODEL)
    me_blk = _my_block()

    win_s, wout_s, wpg_s, wpe_s = _cast_bf16(
        [w_in.reshape(DEPTH * D_MODEL, 512), w_out.reshape(DEPTH * 128, D_MODEL),
         w_pg.reshape(DEPTH * 128, D_MODEL), w_pe.reshape(DEPTH * PLE_DIM, 128)], "cast_weights")
    win_s, wout_s = win_s.reshape(DEPTH, D_MODEL, 512), wout_s.reshape(DEPTH, 128, D_MODEL)
    wpg_s, wpe_s = wpg_s.reshape(DEPTH, 128, D_MODEL), wpe_s.reshape(DEPTH, PLE_DIM, 128)
    cw_s = jnp.zeros((SUBLANES, LANES), F32).at[:DEPTH * 3, :HEAD_DIM].set(conv_w.reshape(DEPTH * 3, HEAD_DIM))
    bf = lambda r_, c_: jax.ShapeDtypeStruct((r_, c_), BF16)
    w_items = lambda l: [(wout_s[l], bf(D_MODEL, D_MODEL), "rows128"), (wpg_s[l], bf(D_MODEL, D_MODEL), "rows128"),
                         (wpe_s[l], bf(PLE_DIM, D_MODEL), "cols128")]
    win_f = [None] * DEPTH
    win_f[0], cw_all = _comm_call(_gather_comm([
        (win_s[0], bf(D_MODEL, N_IN), "cols512"),
        (cw_s, jax.ShapeDtypeStruct((N_DEV, SUBLANES, LANES), F32), "slot")]), "gather_w_in_0")
    cw_full = jnp.transpose(cw_all[:, :DEPTH * 3, :HEAD_DIM].reshape(N_DEV, DEPTH, 3, HEAD_DIM), (1, 2, 0, 3))
    cw_full = cw_full.reshape(DEPTH, 3, D_CONV)
    gather_rest_0 = _gather_comm(w_items(0))
    gather_win_1 = _gather_comm([(win_s[1], bf(D_MODEL, N_IN), "cols512")])
    gather_rest_1 = _gather_comm(w_items(1))

    vec = lambda a, l: a[l][None, :]

    saved = []
    xl = x0
    wout_f, wpg_f, wpe_f = [None] * DEPTH, [None] * DEPTH, [None] * DEPTH
    for l in range(DEPTH):
        (h, pc, qkv, az), got = _fwd_in(xl, vec(norm_g, l), win_f[l], f"fwd_in_{l}",
                                        comm=gather_rest_0 if l == 0 else None)
        if l == 0:
            wout_f[0], wpg_f[0], wpe_f[0] = got
        (ya, lsum, nblk), got = _attn_fwd(qkv, f"attn_fwd_{l}", comm=gather_win_1 if l == 0 else None)
        if l == 0:
            (win_f[1],) = got
        (x2, x3, gated, h2, gate, e), got = _fwd_mid(
            xl, pc, az, ya, p, l, cw_full[l], vec(conv_b, l), vec(branch_g, l), wout_f[l],
            vec(ple_norm_g, l), wpg_f[l], vec(b_pg, l), wpe_f[l], f"fwd_mid_{l}",
            comm=gather_rest_1 if l == 0 else None)
        if l == 0:
            wout_f[1], wpg_f[1], wpe_f[1] = got
        saved.append(dict(x=xl, h=h, pc=pc, qkv=qkv, az=az, ya=ya, lsum=lsum, nblk=nblk, x2=x2, gated=gated, h2=h2,
                          gate=gate, e=e))
        xl = x3

    dx, loss_acc, d_final_g = _loss_head(xl, target, final_g[None, :])

    dwin, dwout, dwpg, dwpe = [None] * DEPTH, [None] * DEPTH, [None] * DEPTH, [None] * DEPTH
    small = dict(norm_g=[None] * DEPTH, conv_b=[None] * DEPTH, branch_g=[None] * DEPTH,
                 ple_norm_g=[None] * DEPTH, b_pg=[None] * DEPTH, conv_w=[None] * DEPTH)
    slot = lambda r_, c_: jax.ShapeDtypeStruct((r_, c_), BF16)
    r_in, r_out, r_pg, r_pe = [None] * DEPTH, [None] * DEPTH, [None] * DEPTH, [None] * DEPTH

    def rest_items(l):
        return [(dwout[l], slot(128, D_MODEL), "rows128"), (dwpg[l], slot(128, D_MODEL), "rows128"),
                (dwpe[l], slot(PLE_DIM, 128), "cols128")]

    for l in reversed(range(DEPTH)):
        sv = saved[l]
        (dx2, dya, dmisc, dconv, dwout[l], dwpg[l], dwpe[l], d_bpg, d_pg, d_bg, d_cbias, d_cw) = _bwd_mid(
            dx, sv["x2"], sv["gate"], sv["e"], sv["pc"], sv["az"], sv["ya"], sv["gated"], sv["h2"], p, l,
            cw_full[l], vec(conv_b, l), vec(branch_g, l), vec(ple_norm_g, l), wpg_f[l], wout_f[l], f"bwd_mid_{l}")
        ride = None
        if l == 0:
            ride = _exchange_comm([(dwin[1], slot(D_MODEL, 512), "cols512"), *rest_items(1)])
        (dq, dk, dv), got = _attn_bwd(sv["qkv"], sv["lsum"], sv["nblk"], dya, f"attn_bwd_{l}", comm=ride)
        if l == 0:
            r_in[1], r_out[1], r_pg[1], r_pe[1] = got
        (dproj, dx, d_ng), _ = _bwd_dproj(dmisc, dconv, sv["pc"], dq, dk, dv, sv["x"], dx2, vec(norm_g, l),
                                          cw_full[l], win_f[l], f"bwd_dproj_{l}")
        ride = _exchange_comm(rest_items(0)) if l == 0 else None
        dwin[l], got = _atb(sv["h"], dproj, f"dw_in_{l}", comm=ride)
        if l == 0:
            r_out[0], r_pg[0], r_pe[0] = got
        small["norm_g"][l], small["conv_b"][l], small["branch_g"][l] = d_ng, d_cbias, d_bg
        small["ple_norm_g"][l], small["b_pg"][l], small["conv_w"][l] = d_pg, d_bpg, d_cw[:3]
    grad_x = dx.reshape(1, s, D_MODEL)

    flat = lambda parts: jnp.concatenate([a.reshape(-1) for a in parts])
    small_vec = jnp.concatenate([
        flat(small["norm_g"]), flat(small["conv_b"]), flat(small["branch_g"]), flat(small["ple_norm_g"]),
        flat(small["b_pg"]), d_final_g.reshape(-1), flat(small["conv_w"]), loss_acc.reshape(-1),
        jnp.zeros(((SMALL_ROWS - SMALL_GRAD_ROWS - 1) * LANES,), F32)]).reshape(SMALL_ROWS, LANES)
    r_in[0], r_small = _comm_call(_exchange_comm([
        (dwin[0], slot(D_MODEL, 512), "cols512"),
        (small_vec, jax.ShapeDtypeStruct((SMALL_ROWS, LANES), F32), "slot")]), "exchange_last")

    g_win, d_win, m_win, v_win = _adamw_sum8(r_in, w_in, m_w_in, v_w_in, "adamw_w_in")
    g_wout, d_wout, m_wout, v_wout = _adamw_sum8(r_out, w_out, m_w_out, v_w_out, "adamw_w_out")
    g_wpg, d_wpg, m_wpg, v_wpg = _adamw_sum8(r_pg, w_pg, m_w_pg, v_w_pg, "adamw_w_pg")
    g_wpe, d_wpe, m_wpe, v_wpe = _adamw_sum8(r_pe, w_pe, m_w_pe, v_w_pe, "adamw_w_pe")

    g_small = _sum8_small(r_small)
    repl = [(norm_g, m_norm_g, v_norm_g), (conv_b, m_conv_b, v_conv_b), (branch_g, m_branch_g, v_branch_g),
            (ple_norm_g, m_ple_norm_g, v_ple_norm_g), (b_pg, m_b_pg, v_b_pg), (final_g, m_final_g, v_final_g)]
    pack = lambda idx: jnp.concatenate([t[idx].reshape(-1) for t in repl]).reshape(SMALL_REPL_ROWS, LANES)
    g_repl = g_small[:SMALL_REPL_ROWS]
    d_repl, m_repl, v_repl = _adamw_plain(g_repl, pack(0), pack(1), pack(2), "adamw_replicated")

    def unpack(a):
        flat_a = a.reshape(-1)
        out, off = [], 0
        for t in repl:
            n = t[0].size
            out.append(flat_a[off:off + n].reshape(t[0].shape))
            off += n
        return out

    g_r, d_r, m_r, v_r = unpack(g_repl), unpack(d_repl), unpack(m_repl), unpack(v_repl)

    loss = g_small[SMALL_GRAD_ROWS, 0]
    g_cw_full = g_small[SMALL_REPL_ROWS:SMALL_GRAD_ROWS].reshape(DEPTH, 3, D_CONV)
    g_cw = lax.dynamic_slice(g_cw_full, (0, 0, me_blk * HEAD_DIM), (DEPTH, 3, HEAD_DIM))
    pad_cw = lambda a: jnp.zeros((SUBLANES, LANES), F32).at[:3].set(a.reshape(3, LANES))
    v_cw_pad = jnp.ones((SUBLANES, LANES), F32).at[:3].set(v_conv_w.reshape(3, LANES))
    d_cw, m_cw, v_cw = _adamw_plain(pad_cw(g_cw), pad_cw(conv_w), pad_cw(m_conv_w), v_cw_pad, "adamw_conv_w")
    un_cw = lambda a: a[:3].reshape(DEPTH, 3, HEAD_DIM)

    def ordered(r, win_, cw_, wout_, wpg_, wpe_):
        return [r[0], win_, cw_, r[1], r[2], wout_, r[3], wpg_, r[4], wpe_, r[5]]

    grads = ordered(g_r, g_win, g_cw, g_wout, g_wpg, g_wpe)
    deltas = ordered(d_r, d_win, un_cw(d_cw), d_wout, d_wpg, d_wpe)
    new_m = ordered(m_r, m_win, un_cw(m_cw), m_wout, m_wpg, m_wpe)
    new_v = ordered(v_r, v_win, un_cw(v_cw), v_wout, v_wpg, v_wpe)
    return (loss, grad_x, *grads, *deltas, *new_m, *new_v)
```

```python
import jax
import jax.numpy as jnp
from jax import lax
from jax.experimental import pallas as pl
from jax.experimental.pallas import tpu as pltpu

F32 = jnp.float32
BF16 = jnp.bfloat16

D_MODEL = 1024
D_CONV = 512
D_SB = 512
N_IN = 4096
HEAD_DIM = 64
PLE_DIM = 256
DEPTH = 2
EPS = 1e-6
ADAM_LR = 0.001
ADAM_B1 = 0.9
ADAM_B2 = 0.999
ADAM_EPS = 1e-08
ADAM_WD = 0.01
ADAM_STEP = 10

LANES = 128
SUBLANES = 8
VMEM_BYTES_V7X = 64 * 1024 * 1024
VMEM_LIMIT = VMEM_BYTES_V7X - 8 * 1024 * 1024

N_DEV = 8
ROW_TILE = 256
ATTN_TILE = 256
SMALL_GRAD_ROWS = 104
SMALL_REPL_ROWS = 80
SMALL_ROWS = 112

NT = (((1,), (1,)), ((), ()))
TN = (((0,), (0,)), ((), ()))


def _call(body, **kw):
    return pl.pallas_call(body, **kw)


def _params(sem=None, vmem=None):
    return pltpu.CompilerParams(dimension_semantics=sem, vmem_limit_bytes=vmem)


def _sigmoid(z):
    return 0.5 * jnp.tanh(0.5 * z) + 0.5


def _group_bcast_sum(a, lo):
    s_lo = jnp.sum(jnp.where(lo, a, 0.0), axis=-1, keepdims=True)
    s_hi = jnp.sum(jnp.where(lo, 0.0, a), axis=-1, keepdims=True)
    return jnp.where(lo, s_lo, s_hi)


def _my_block():
    return 4 * lax.axis_index("x") + 2 * lax.axis_index("y") + lax.axis_index("c")


def _cast_bf16(arrays, name):
    n = len(arrays)

    def body(*refs):
        for a_ref, o_ref in zip(refs[:n], refs[n:]):
            o_ref[...] = a_ref[...].astype(BF16)

    whole = lambda a: pl.BlockSpec(a.shape, lambda: (0, 0))
    return _call(
        body, name=name,
        out_shape=tuple(jax.ShapeDtypeStruct(a.shape, BF16) for a in arrays),
        in_specs=[whole(a) for a in arrays], out_specs=tuple(whole(a) for a in arrays),
        compiler_params=_params(None, VMEM_LIMIT),
    )(*arrays)


class _Comm:
    def __init__(self, inputs, out_shapes, scratch, begin, middle, finish):
        self.inputs, self.out_shapes, self.scratch = list(inputs), list(out_shapes), list(scratch)
        self.begin, self.middle, self.finish = begin, middle, finish


def _slab(kind, ref, blk):
    if kind == "cols512":
        return ref.at[:, pl.ds(blk * 512, 512)]
    if kind == "rows128":
        return ref.at[pl.ds(blk * 128, 128), :]
    if kind == "cols128":
        return ref.at[:, pl.ds(blk * 128, 128)]
    return ref.at[blk]


def _gather_comm(items):
    n_t = len(items)
    kinds = [it[2] for it in items]

    def ctx(ins, outs, sems):
        send_sems, recv_sems, local_sems = sems
        x, y, c = lax.axis_index("x"), lax.axis_index("y"), lax.axis_index("c")
        me, sibling = (x, y, c), (x, y, 1 - c)
        chips = [(1 - x, y), (x, 1 - y), (1 - x, 1 - y)]

        def place(t, dev):
            return _slab(kinds[t], outs[t], 4 * dev[0] + 2 * dev[1] + dev[2])

        def copy(t, k, block, to, own=False):
            return pltpu.make_async_remote_copy(
                src_ref=ins[t] if own else place(t, block), dst_ref=place(t, block),
                send_sem=send_sems.at[t, k], recv_sem=recv_sems.at[t, k],
                device_id=to, device_id_type=pl.DeviceIdType.MESH)

        mine = [pltpu.make_async_copy(ins[t], place(t, me), local_sems.at[t]) for t in range(n_t)]
        first = []
        for t in range(n_t):
            first.append(copy(t, 0, me, sibling, own=True))
            first += [copy(t, 1 + j, me, (*chip, c), own=True) for j, chip in enumerate(chips)]
        passed = [copy(t, 4 + j, (*chip, c), sibling) for j, chip in enumerate(chips) for t in range(n_t)]
        landed = [copy(t, 1 + j, (*chip, c), me) for j, chip in enumerate(chips) for t in range(n_t)]
        from_sibling = []
        for t in range(n_t):
            from_sibling.append(copy(t, 0, sibling, me))
            from_sibling += [copy(t, 4 + j, (*chip, 1 - c), me) for j, chip in enumerate(chips)]
        return mine, first, landed, passed, from_sibling

    def begin(ins, outs, sems):
        mine, first, _, _, _ = ctx(ins, outs, sems)
        for cp in mine + first:
            cp.start()

    def middle(ins, outs, sems):
        _, _, landed, passed, _ = ctx(ins, outs, sems)
        for got, fwd in zip(landed, passed):
            got.wait_recv()
            fwd.start()

    def finish(ins, outs, sems):
        mine, first, _, passed, from_sibling = ctx(ins, outs, sems)
        for cp in from_sibling:
            cp.wait_recv()
        for cp in first + passed:
            cp.wait_send()
        for cp in mine:
            cp.wait()

    scratch = [pltpu.SemaphoreType.DMA((n_t, 7)), pltpu.SemaphoreType.DMA((n_t, 7)), pltpu.SemaphoreType.DMA((n_t,))]
    return _Comm([it[0] for it in items], [it[1] for it in items], scratch, begin, middle, finish)


def _exchange_comm(items):
    n_t = len(items)
    kinds = [it[2] for it in items]

    def ctx(ins, outs, sems):
        send_sems, recv_sems, local_sems = sems
        x, y, c = lax.axis_index("x"), lax.axis_index("y"), lax.axis_index("c")
        me_blk = 4 * x + 2 * y + c

        def src(t, blk):
            return ins[t] if kinds[t] == "slot" else _slab(kinds[t], ins[t], blk)

        local = [pltpu.make_async_copy(src(t, me_blk), outs[t].at[me_blk], local_sems.at[t]) for t in range(n_t)]
        remote = []
        for k in range(1, N_DEV):
            px = 1 - x if k & 4 else x
            py = 1 - y if k & 2 else y
            pc_ = 1 - c if k & 1 else c
            for t in range(n_t):
                remote.append(pltpu.make_async_remote_copy(
                    src_ref=src(t, 4 * px + 2 * py + pc_), dst_ref=outs[t].at[me_blk],
                    send_sem=send_sems.at[k - 1, t], recv_sem=recv_sems.at[k - 1, t],
                    device_id=(px, py, pc_), device_id_type=pl.DeviceIdType.MESH))
        return local, remote

    def begin(ins, outs, sems):
        local, remote = ctx(ins, outs, sems)
        for cp in local + remote:
            cp.start()

    def finish(ins, outs, sems):
        local, remote = ctx(ins, outs, sems)
        for cp in remote:
            cp.wait_recv()
        for cp in remote:
            cp.wait_send()
        for cp in local:
            cp.wait()

    scratch = [pltpu.SemaphoreType.DMA((N_DEV - 1, n_t)), pltpu.SemaphoreType.DMA((N_DEV - 1, n_t)),
               pltpu.SemaphoreType.DMA((n_t,))]
    out_shapes = [jax.ShapeDtypeStruct((N_DEV, *it[1].shape), it[1].dtype) for it in items]
    return _Comm([it[0] for it in items], out_shapes, scratch, begin, None, finish)


def _comm_call(comm, name):
    n_in, n_out = len(comm.inputs), len(comm.out_shapes)

    def body(*refs):
        ins, outs, sems = refs[:n_in], refs[n_in:n_in + n_out], refs[n_in + n_out:]
        comm.begin(ins, outs, sems)
        if comm.middle is not None:
            comm.middle(ins, outs, sems)
        comm.finish(ins, outs, sems)

    any_spec = pl.BlockSpec(memory_space=pl.ANY)
    return _call(body, name=name, out_shape=tuple(comm.out_shapes), in_specs=[any_spec] * n_in,
                 out_specs=[any_spec] * n_out, scratch_shapes=comm.scratch)(*comm.inputs)


def _hosted(body, n_in, n_out, comm, first, last, middle):
    if comm is None:
        return lambda *refs: body(*refs)
    n_ci, n_co, n_cs = len(comm.inputs), len(comm.out_shapes), len(comm.scratch)

    def wrapped(*refs):
        ins, cin = refs[:n_in], refs[n_in:n_in + n_ci]
        o0 = n_in + n_ci
        outs, cout = refs[o0:o0 + n_out], refs[o0 + n_out:o0 + n_out + n_co]
        scr, csem = refs[o0 + n_out + n_co:len(refs) - n_cs], refs[len(refs) - n_cs:]
        pl.when(first())(lambda: comm.begin(cin, cout, csem))
        body(*ins, *outs, *scr)
        if comm.middle is not None:
            pl.when(middle())(lambda: comm.middle(cin, cout, csem))
        pl.when(last())(lambda: comm.finish(cin, cout, csem))

    return wrapped


def _hosted_call(body, comm, *, name, grid, out_shape, in_specs, out_specs, args, scratch_shapes=(), sem=None):
    nd = len(grid)
    first, last, middle = _at_first(nd), _at_last(nd), _at_middle(nd)
    if comm is not None:
        sem = ("arbitrary",) * nd
    n_in, n_out = len(in_specs), len(out_shape)
    any_spec = pl.BlockSpec(memory_space=pl.ANY)
    c_in = [] if comm is None else comm.inputs
    c_out = [] if comm is None else comm.out_shapes
    c_scr = [] if comm is None else comm.scratch
    outs = _call(
        _hosted(body, n_in, n_out, comm, first, last, middle), name=name, grid=grid,
        out_shape=(*out_shape, *c_out),
        in_specs=[*in_specs, *[any_spec] * len(c_in)],
        out_specs=(*out_specs, *[any_spec] * len(c_out)),
        scratch_shapes=[*scratch_shapes, *c_scr],
        compiler_params=_params(sem, VMEM_LIMIT),
    )(*args, *c_in)
    return outs[:n_out], outs[n_out:]


def _grid_step(ndim):
    i, n = pl.program_id(0), pl.num_programs(0)
    for d in range(1, ndim):
        i, n = i * pl.num_programs(d) + pl.program_id(d), n * pl.num_programs(d)
    return i, n


def _at_first(ndim):
    return lambda: _grid_step(ndim)[0] == 0


def _at_last(ndim):
    def pred():
        i, n = _grid_step(ndim)
        return i == n - 1
    return pred


def _at_middle(ndim):
    def pred():
        i, n = _grid_step(ndim)
        return i == (3 * n) // 4
    return pred


def _fwd_in(x, g, w_full, name, comm=None):
    s = x.shape[0]
    ts = min(ROW_TILE, s)

    def body(x_ref, g_ref, w_ref, h_ref, pc_ref, qkv_ref, az_ref):
        xf = x_ref[...]
        r = lax.rsqrt(jnp.mean(xf * xf, axis=-1, keepdims=True) + EPS)
        h = (xf * r * g_ref[...]).astype(BF16)
        h_ref[...] = h
        pc_ref[...] = jnp.dot(h, w_ref[:, 0:2048], preferred_element_type=F32).astype(BF16)
        q = jnp.dot(h, w_ref[:, 2048:2560], preferred_element_type=F32)
        qkv_ref[:, 0:512] = (q * 0.125).astype(BF16)
        qkv_ref[:, 512:1536] = jnp.dot(h, w_ref[:, 2560:3584], preferred_element_type=F32).astype(BF16)
        az_ref[...] = jnp.dot(h, w_ref[:, 3584:4096], preferred_element_type=F32).astype(BF16)

    row = lambda width: pl.BlockSpec((ts, width), lambda i: (i, 0))
    return _hosted_call(
        body, comm, name=name, grid=(s // ts,),
        out_shape=(jax.ShapeDtypeStruct((s, D_MODEL), BF16), jax.ShapeDtypeStruct((s, 2048), BF16),
                   jax.ShapeDtypeStruct((s, 1536), BF16), jax.ShapeDtypeStruct((s, 512), BF16)),
        in_specs=[row(D_MODEL), pl.BlockSpec((1, D_MODEL), lambda i: (0, 0)),
                  pl.BlockSpec((D_MODEL, N_IN), lambda i: (0, 0))],
        out_specs=(row(D_MODEL), row(2048), row(1536), row(512)),
        args=(x, g, w_full), sem=("parallel",))


ATTN_ROWS = 128
ATTN_DONE = 104.0


def _attn_pieces(tq, rc):
    lane = lax.broadcasted_iota(jnp.int32, (1, LANES), 1)
    lo = lane < HEAD_DIM
    row = lax.broadcasted_iota(jnp.int32, (tq, tq), 0)
    col = lax.broadcasted_iota(jnp.int32, (tq, tq), 1)
    tri_gt = jnp.where(row > col, 1.0, 0.0).astype(BF16)
    tri_le = jnp.where(row <= col, 1.0, 0.0).astype(BF16)
    rrow = lax.broadcasted_iota(jnp.int32, (rc, tq), 0)
    rcol = lax.broadcasted_iota(jnp.int32, (rc, tq), 1)
    causal = [rcol < rrow + r * rc for r in range(tq // rc)]
    return lo, causal, tri_gt, tri_le


def _split_heads(a, lo):
    z = jnp.zeros_like(a)
    return (jnp.where(lo, a, z), jnp.where(lo, z, a))


def _softplus(z, causal, diag):
    neg_abs = lax.bitcast_convert_type(lax.bitcast_convert_type(z, jnp.uint32) | jnp.uint32(0x80000000), F32)
    sp = jnp.maximum(z, 0.0) + jnp.log(1.0 + jnp.exp(neg_abs))
    if diag:
        sp = jnp.where(causal, sp, 0.0)
    return sp


def _attn_fwd(qkv, name, comm=None):
    s = qkv.shape[0]
    tq = min(ATTN_TILE, s)
    nq = s // tq
    rc = min(ATTN_ROWS, tq)
    n_rc = tq // rc
    chains = [(r, hh) for r in range(n_rc) for hh in range(2)]

    def body(q_ref, k_ref, v_ref, o_ref, lsum_ref, nblk_ref):
        hp, qi = pl.program_id(0), pl.program_id(1)
        lo, causal, tri_gt, _ = _attn_pieces(tq, rc)
        qh = _split_heads(q_ref[...], lo)
        qc = {(r, hh): qh[hh][r * rc:(r + 1) * rc] for r, hh in chains}

        mm = lambda a_, b_: jnp.dot(a_.astype(BF16), b_, preferred_element_type=F32)
        rowsum = lambda a_: jnp.sum(a_, axis=-1, keepdims=True)

        def block(kb, carry):
            start = pl.multiple_of(kb * tq, tq)
            k = k_ref[pl.ds(start, tq), :]
            vh = _split_heads(v_ref[pl.ds(start, tq), :], lo)
            z = {ch: lax.dot_general(qc[ch], k, NT, preferred_element_type=F32) for ch in chains}
            sp = {ch: _softplus(z[ch], None, False) for ch in chains}
            later = {ch: mm(sp[ch], tri_gt) for ch in chains}
            a = {ch: jnp.exp((z[ch] - sp[ch]) - (carry[ch[0]][1 + ch[1]] + later[ch])) for ch in chains}
            pv = {ch: mm(a[ch], vh[ch[1]]) for ch in chains}
            return tuple((carry[r][0] + pv[(r, 0)] + pv[(r, 1)],
                          carry[r][1] + rowsum(sp[(r, 0)]), carry[r][2] + rowsum(sp[(r, 1)])) for r in range(n_rc))

        def first_two(prev_ok):
            d0 = pl.multiple_of(qi * tq, tq)
            p0 = pl.multiple_of(jnp.maximum(qi - 1, 0) * tq, tq)
            k_d, k_p = k_ref[pl.ds(d0, tq), :], k_ref[pl.ds(p0, tq), :]
            vh_d = _split_heads(v_ref[pl.ds(d0, tq), :], lo)
            vh_p = _split_heads(v_ref[pl.ds(p0, tq), :], lo)
            z_d = {ch: lax.dot_general(qc[ch], k_d, NT, preferred_element_type=F32) for ch in chains}
            z_p = {ch: lax.dot_general(qc[ch], k_p, NT, preferred_element_type=F32) for ch in chains}
            sp_d = {ch: _softplus(z_d[ch], causal[ch[0]], True) for ch in chains}
            sp_raw = {ch: _softplus(z_p[ch], None, False) for ch in chains}
            sp_p = {ch: jnp.where(prev_ok, sp_raw[ch], 0.0) for ch in chains}
            later_d = {ch: mm(sp_d[ch], tri_gt) for ch in chains}
            later_p = {ch: mm(sp_p[ch], tri_gt) for ch in chains}
            c_d = {ch: rowsum(sp_d[ch]) for ch in chains}
            a_d = {ch: jnp.where(causal[ch[0]], jnp.exp((z_d[ch] - sp_d[ch]) - later_d[ch]), 0.0) for ch in chains}
            a_p = {ch: jnp.where(prev_ok, jnp.exp((z_p[ch] - sp_raw[ch]) - (c_d[ch] + later_p[ch])), 0.0)
                   for ch in chains}
            pv = {ch: mm(a_d[ch], vh_d[ch[1]]) + mm(a_p[ch], vh_p[ch[1]]) for ch in chains}
            return tuple((pv[(r, 0)] + pv[(r, 1)],
                          c_d[(r, 0)] + rowsum(sp_p[(r, 0)]), c_d[(r, 1)] + rowsum(sp_p[(r, 1)]))
                         for r in range(n_rc))

        def least(carry):
            m = jnp.minimum(carry[0][1], carry[0][2])
            for r in range(1, n_rc):
                m = jnp.minimum(m, jnp.minimum(carry[r][1], carry[r][2]))
            return jnp.min(m)

        carry = first_two(qi > 0)

        def go_on(st):
            return jnp.logical_and(st[0] < qi - 1, st[1] < ATTN_DONE)

        def step(st):
            new = block(qi - 2 - st[0], st[2])
            return st[0] + 1, least(new), new

        walked, _, carry = lax.while_loop(go_on, step, (jnp.int32(0), least(carry), carry))
        for r in range(n_rc):
            o_ref[r * rc:(r + 1) * rc, :] = carry[r][0].astype(BF16)
            lsum_ref[r * rc:(r + 1) * rc, :] = jnp.where(lo, carry[r][1], carry[r][2])
        nblk_ref[hp, qi] = walked.astype(F32)

    blk = pl.BlockSpec((tq, LANES), lambda hp, qi: (qi, hp))
    o512 = jax.ShapeDtypeStruct((s, D_SB), F32)
    return _hosted_call(
        body, comm, name=name, grid=(4, nq),
        out_shape=(jax.ShapeDtypeStruct((s, D_SB), BF16), o512, jax.ShapeDtypeStruct((4, nq), F32)),
        in_specs=[blk, pl.BlockSpec((s, LANES), lambda hp, qi: (0, 4 + hp)),
                  pl.BlockSpec((s, LANES), lambda hp, qi: (0, 8 + hp))],
        out_specs=(blk, blk, pl.BlockSpec(memory_space=pltpu.SMEM)),
        args=(qkv, qkv, qkv), sem=("arbitrary", "arbitrary"))


HALO = 16


def _conv_taps(cc_ref, ch_ref, ccp_ref, chp_ref, halo_ref, first):
    u = cc_ref[...].astype(F32) * ch_ref[...].astype(F32)
    halo_ref[...] = ccp_ref[...].astype(F32) * chp_ref[...].astype(F32) * jnp.where(first, 0.0, 1.0)
    p6 = halo_ref[HALO - 2:HALO - 1, :]
    p7 = halo_ref[HALO - 1:HALO, :]
    rowi = lax.broadcasted_iota(jnp.int32, u.shape, 0)
    u1 = jnp.where(rowi == 0, p7, pltpu.roll(u, 1, 0))
    u2 = jnp.where(rowi == 0, p6, jnp.where(rowi == 1, p7, pltpu.roll(u, 2, 0)))
    return u, u1, u2


def _fwd_mid(x, pc, az, ya, p4, layer, cw, cb, bg, wout_full, pg, wpg_full, bpg, wpe_full, name, comm=None):
    s = x.shape[0]
    ts = min(ROW_TILE, s)
    blk_h = ts // HALO

    def body(x_ref, cb_ref_, cc_ref, ch_ref, cz_ref, ccp_ref, chp_ref, az_ref, ya_ref, p_ref,
             cw_ref, cbias_ref, bg_ref, wout_ref, pg_ref, wpg_ref, bpg_ref, wpe_ref,
             x2_ref, x3_ref, gated_ref, h2_ref, gate_ref, e_ref, halo_ref):
        i = pl.program_id(0)
        lane = lax.broadcasted_iota(jnp.int32, (1, LANES), 1)
        lo = lane < HEAD_DIM
        u, u1, u2 = _conv_taps(cc_ref, ch_ref, ccp_ref, chp_ref, halo_ref, i == 0)
        conv = cbias_ref[...] + cw_ref[0:1, :] * u2 + cw_ref[1:2, :] * u1 + cw_ref[2:3, :] * u
        yc = cb_ref_[...].astype(F32) * conv
        for sl in range(8):
            cols = slice(LANES * (sl % 4), LANES * (sl % 4 + 1))
            y = yc[:, cols] if sl < 4 else ya_ref[:, cols].astype(F32)
            zc = (cz_ref[:, cols] if sl < 4 else az_ref[:, cols]).astype(F32)
            rg = lax.rsqrt(_group_bcast_sum(y * y, lo) * (1.0 / HEAD_DIM) + EPS)
            yn = y * rg * bg_ref[:, LANES * sl:LANES * (sl + 1)]
            gated_ref[:, LANES * sl:LANES * (sl + 1)] = (yn * (zc * _sigmoid(zc))).astype(BF16)
        x2 = x_ref[...] + jnp.dot(gated_ref[...], wout_ref[...], preferred_element_type=F32)
        x2_ref[...] = x2
        r2 = lax.rsqrt(jnp.mean(x2 * x2, axis=-1, keepdims=True) + EPS)
        h2 = (x2 * r2 * pg_ref[...]).astype(BF16)
        h2_ref[...] = h2
        gate = _sigmoid(jnp.dot(h2, wpg_ref[...], preferred_element_type=F32) + bpg_ref[...])
        gate_ref[...] = gate.astype(BF16)
        e = jnp.dot(p_ref[...].astype(BF16), wpe_ref[...], preferred_element_type=F32)
        e_ref[...] = e.astype(BF16)
        x3_ref[...] = x2 + gate * e

    row = lambda width, cb_=0: pl.BlockSpec((ts, width), lambda i: (i, cb_))
    prev = lambda cb_: pl.BlockSpec((HALO, 512), lambda i: (jnp.maximum(i * blk_h - 1, 0), cb_))
    vec = lambda width: pl.BlockSpec((1, width), lambda i: (0, 0))
    wspec = lambda r_, c_: pl.BlockSpec((r_, c_), lambda i: (0, 0))
    f32o = jax.ShapeDtypeStruct((s, D_MODEL), F32)
    bfo = jax.ShapeDtypeStruct((s, D_MODEL), BF16)
    return _hosted_call(
        body, comm, name=name, grid=(s // ts,),
        out_shape=(f32o, f32o, bfo, bfo, bfo, bfo),
        scratch_shapes=[pltpu.VMEM((HALO, 512), F32)],
        in_specs=[row(D_MODEL), row(512, 0), row(512, 1), row(512, 2), row(512, 3), prev(1), prev(2),
                  row(512), row(512),
                  pl.BlockSpec((None, None, ts, PLE_DIM), lambda i: (layer, 0, i, 0)),
                  pl.BlockSpec((3, 512), lambda i: (0, 0)), vec(512), vec(D_MODEL),
                  wspec(D_MODEL, D_MODEL), vec(D_MODEL), wspec(D_MODEL, D_MODEL), vec(D_MODEL),
                  wspec(PLE_DIM, D_MODEL)],
        out_specs=(row(D_MODEL),) * 6,
        args=(x, pc, pc, pc, pc, pc, pc, az, ya, p4, cw, cb, bg, wout_full, pg, wpg_full, bpg, wpe_full),
        sem=("parallel",))


def _loss_head(xf, target, fg):
    s = xf.shape[0]
    ts = min(ROW_TILE, s)

    def body(x_ref, t_ref, g_ref, dx_ref, loss_ref, dg_ref):
        i = pl.program_id(0)

        @pl.when(i == 0)
        def _():
            loss_ref[...] = jnp.zeros_like(loss_ref)
            dg_ref[...] = jnp.zeros_like(dg_ref)

        x = x_ref[...]
        g = g_ref[...]
        r = lax.rsqrt(jnp.mean(x * x, axis=-1, keepdims=True) + EPS)
        xn = x * r
        err = xn * g - t_ref[...]
        per_row = jnp.sum(err * err, axis=-1, keepdims=True)
        loss_ref[...] += jnp.sum(per_row, axis=0, keepdims=True) * (0.5 / D_MODEL)
        dy = err * (1.0 / D_MODEL)
        dg_ref[...] += jnp.sum(dy * xn, axis=0, keepdims=True)
        dxn = dy * g
        dx_ref[...] = r * (dxn - xn * jnp.mean(dxn * xn, axis=-1, keepdims=True))

    row = pl.BlockSpec((ts, D_MODEL), lambda i: (i, 0))
    return _call(
        body, name="loss_head", grid=(s // ts,),
        out_shape=(jax.ShapeDtypeStruct((s, D_MODEL), F32), jax.ShapeDtypeStruct((1, LANES), F32),
                   jax.ShapeDtypeStruct((1, D_MODEL), F32)),
        in_specs=[row, row, pl.BlockSpec((1, D_MODEL), lambda i: (0, 0))],
        out_specs=(row, pl.BlockSpec((1, LANES), lambda i: (0, 0)), pl.BlockSpec((1, D_MODEL), lambda i: (0, 0))),
        compiler_params=_params(("arbitrary",), VMEM_LIMIT),
    )(xf, target, fg)


def _bwd_mid(dx3, x2, gate, e, pc, az, ya, gated, h2, p4, layer, cw, cb, bg, pg, wpg_full, wout_full, name,
             comm=None):
    s = x2.shape[0]
    ts = min(ROW_TILE, s)
    blk_h = ts // HALO

    def body(dx3_ref, x2_ref, gate_ref, e_ref, cb_ref_, cc_ref, ch_ref, cz_ref, ccp_ref, chp_ref, az_ref, ya_ref,
             gated_ref, h2_ref, p_ref, cw_ref, cbias_ref, bg_ref, pg_ref, wpg_ref, wout_ref,
             dx2_ref, dya_ref, dmisc_ref, dconv_ref, dwout_ref, dwpg_ref, dwpe_ref,
             dbpg_ref, dpg_ref, dbg_ref, dcbias_ref, dcw_ref,
             dgated_ref, halo_ref, acc_out, acc_pg, acc_pe):
        i = pl.program_id(0)

        @pl.when(i == 0)
        def _():
            for ref in (dbpg_ref, dpg_ref, dbg_ref, dcbias_ref, dcw_ref, acc_out, acc_pg, acc_pe):
                ref[...] = jnp.zeros_like(ref)

        lane = lax.broadcasted_iota(jnp.int32, (1, LANES), 1)
        lo = lane < HEAD_DIM
        dx3 = dx3_ref[...]
        gate = gate_ref[...].astype(F32)
        de_b = (dx3 * gate).astype(BF16)
        dgpre = dx3 * e_ref[...].astype(F32) * gate * (1.0 - gate)
        dbpg_ref[...] += jnp.sum(dgpre, axis=0, keepdims=True)
        dgpre_b = dgpre.astype(BF16)
        dh2 = lax.dot_general(dgpre_b, wpg_ref[...], NT, preferred_element_type=F32)
        acc_pe[...] += lax.dot_general(p_ref[...].astype(BF16), de_b, TN, preferred_element_type=F32)
        acc_pg[...] += lax.dot_general(h2_ref[...], dgpre_b, TN, preferred_element_type=F32)

        u, u1, u2 = _conv_taps(cc_ref, ch_ref, ccp_ref, chp_ref, halo_ref, i == 0)
        conv = cbias_ref[...] + cw_ref[0:1, :] * u2 + cw_ref[1:2, :] * u1 + cw_ref[2:3, :] * u
        c_b = cb_ref_[...].astype(F32)
        yc = c_b * conv
        fwd = []
        for sl in range(8):
            cols = slice(LANES * (sl % 4), LANES * (sl % 4 + 1))
            y = yc[:, cols] if sl < 4 else ya_ref[:, cols].astype(F32)
            zc = (cz_ref[:, cols] if sl < 4 else az_ref[:, cols]).astype(F32)
            rg = lax.rsqrt(_group_bcast_sum(y * y, lo) * (1.0 / HEAD_DIM) + EPS)
            sig = _sigmoid(zc)
            fwd.append((rg, y * rg, zc * sig, sig * (1.0 + zc * (1.0 - sig))))

        x2 = x2_ref[...]
        r2 = lax.rsqrt(jnp.mean(x2 * x2, axis=-1, keepdims=True) + EPS)
        xn2 = x2 * r2
        dpg_ref[...] += jnp.sum(dh2 * xn2, axis=0, keepdims=True)
        dxn = dh2 * pg_ref[...]
        dx2 = dx3 + r2 * (dxn - xn2 * jnp.mean(dxn * xn2, axis=-1, keepdims=True))
        dx2_ref[...] = dx2
        dx2_b = dx2.astype(BF16)
        dgated_ref[...] = lax.dot_general(dx2_b, wout_ref[...], NT, preferred_element_type=F32)
        acc_out[...] += lax.dot_general(gated_ref[...], dx2_b, TN, preferred_element_type=F32)

        for sl in range(8):
            cols = slice(LANES * (sl % 4), LANES * (sl % 4 + 1))
            wide = slice(LANES * sl, LANES * (sl + 1))
            rg, yhat, silu, dsilu = fwd[sl]
            bgs = bg_ref[:, wide]
            dgt = dgated_ref[:, wide]
            dyn = dgt * silu
            dzc = dgt * (yhat * bgs) * dsilu
            dbg_ref[:, wide] += jnp.sum(dyn * yhat, axis=0, keepdims=True)
            dyh = dyn * bgs
            dy = rg * (dyh - yhat * (_group_bcast_sum(dyh * yhat, lo) * (1.0 / HEAD_DIM)))
            if sl < 4:
                dconv = dy * c_b[:, cols]
                dmisc_ref[:, cols] = (dy * conv[:, cols]).astype(BF16)
                dmisc_ref[:, 512 + LANES * sl:512 + LANES * (sl + 1)] = dzc.astype(BF16)
                dconv_ref[:, cols] = dconv
                dcbias_ref[:, cols] += jnp.sum(dconv, axis=0, keepdims=True)
                dcw_ref[0:1, cols] += jnp.sum(dconv * u2[:, cols], axis=0, keepdims=True)
                dcw_ref[1:2, cols] += jnp.sum(dconv * u1[:, cols], axis=0, keepdims=True)
                dcw_ref[2:3, cols] += jnp.sum(dconv * u[:, cols], axis=0, keepdims=True)
            else:
                dya_ref[:, cols] = dy.astype(BF16)
                dmisc_ref[:, 1024 + LANES * (sl - 4):1024 + LANES * (sl - 3)] = dzc.astype(BF16)

        @pl.when(i == pl.num_programs(0) - 1)
        def _():
            dwout_ref[...] = acc_out[...].astype(BF16)
            dwpg_ref[...] = acc_pg[...].astype(BF16)
            dwpe_ref[...] = acc_pe[...].astype(BF16)

    row = lambda width, cb_=0: pl.BlockSpec((ts, width), lambda i: (i, cb_))
    prev = lambda cb_: pl.BlockSpec((HALO, 512), lambda i: (jnp.maximum(i * blk_h - 1, 0), cb_))
    vec = lambda width: pl.BlockSpec((1, width), lambda i: (0, 0))
    wspec = lambda r_, c_: pl.BlockSpec((r_, c_), lambda i: (0, 0))
    vo = lambda width: jax.ShapeDtypeStruct((1, width), F32)
    sq = jax.ShapeDtypeStruct((D_MODEL, D_MODEL), BF16)
    return _hosted_call(
        body, comm, name=name, grid=(s // ts,), sem=("arbitrary",),
        args=(dx3, x2, gate, e, pc, pc, pc, pc, pc, pc, az, ya, gated, h2, p4, cw, cb, bg, pg, wpg_full, wout_full),
        out_shape=(jax.ShapeDtypeStruct((s, D_MODEL), F32), jax.ShapeDtypeStruct((s, 512), BF16),
                   jax.ShapeDtypeStruct((s, 1536), BF16), jax.ShapeDtypeStruct((s, 512), F32),
                   sq, sq, jax.ShapeDtypeStruct((PLE_DIM, D_MODEL), BF16),
                   vo(D_MODEL), vo(D_MODEL), vo(D_MODEL), vo(512), jax.ShapeDtypeStruct((SUBLANES, 512), F32)),
        in_specs=[row(D_MODEL), row(D_MODEL), row(D_MODEL), row(D_MODEL),
                  row(512, 0), row(512, 1), row(512, 2), row(512, 3), prev(1), prev(2), row(512), row(512),
                  row(D_MODEL), row(D_MODEL),
                  pl.BlockSpec((None, None, ts, PLE_DIM), lambda i: (layer, 0, i, 0)),
                  pl.BlockSpec((3, 512), lambda i: (0, 0)), vec(512), vec(D_MODEL), vec(D_MODEL),
                  wspec(D_MODEL, D_MODEL), wspec(D_MODEL, D_MODEL)],
        out_specs=(row(D_MODEL), row(512), row(1536), row(512),
                   wspec(D_MODEL, D_MODEL), wspec(D_MODEL, D_MODEL), wspec(PLE_DIM, D_MODEL),
                   vec(D_MODEL), vec(D_MODEL), vec(D_MODEL), vec(512),
                   pl.BlockSpec((SUBLANES, 512), lambda i: (0, 0))),
        scratch_shapes=[pltpu.VMEM((ts, D_MODEL), F32), pltpu.VMEM((HALO, 512), F32),
                        pltpu.VMEM((D_MODEL, D_MODEL), F32), pltpu.VMEM((D_MODEL, D_MODEL), F32),
                        pltpu.VMEM((PLE_DIM, D_MODEL), F32)])


def _attn_bwd(qkv, lsum, nblk, dya, name, comm=None):
    s = qkv.shape[0]
    tq = min(ATTN_TILE, s)
    nq = s // tq
    rc = min(ATTN_ROWS, tq)
    n_rc = tq // rc
    chains = [(r, hh) for r in range(n_rc) for hh in range(2)]

    def body(nblk_ref, q_ref, k_ref, v_ref, lsum_ref, do_ref, dq_ref, dk_ref, dv_ref, dk_acc, dv_acc):
        hp, qi = pl.program_id(0), pl.program_id(1)

        @pl.when(qi == 0)
        def _():
            dk_acc[...] = jnp.zeros_like(dk_acc)
            dv_acc[...] = jnp.zeros_like(dv_acc)

        lo, causal, tri_gt, tri_le = _attn_pieces(tq, rc)
        lane = lax.broadcasted_iota(jnp.int32, (1, LANES), 1)
        qh = _split_heads(q_ref[...], lo)
        doh = _split_heads(do_ref[...].astype(BF16), lo)
        lt = lsum_ref[...]
        ltot_h = (jnp.sum(jnp.where(lane == 0, lt, 0.0), axis=-1, keepdims=True),
                  jnp.sum(jnp.where(lane == HEAD_DIM, lt, 0.0), axis=-1, keepdims=True))
        rows = lambda a_, r: a_[r * rc:(r + 1) * rc]
        qc = {(r, hh): rows(qh[hh], r) for r, hh in chains}
        doc = {(r, hh): rows(doh[hh], r) for r, hh in chains}
        ltot = {(r, hh): rows(ltot_h[hh], r) for r, hh in chains}

        mm = lambda a_, b_: jnp.dot(a_.astype(BF16), b_, preferred_element_type=F32)
        mm_nt = lambda a_, b_: lax.dot_general(a_, b_, NT, preferred_element_type=F32)
        mm_tn = lambda a_, b_: lax.dot_general(a_.astype(BF16), b_, TN, preferred_element_type=F32)
        rowsum = lambda a_: jnp.sum(a_, axis=-1, keepdims=True)

        def block(kb, carry, diag=False):
            start = pl.multiple_of(kb * tq, tq)
            k = k_ref[pl.ds(start, tq), :]
            v = v_ref[pl.ds(start, tq), :]
            kh = _split_heads(k, lo)
            keep = (lambda ch, a_: jnp.where(causal[ch[0]], a_, 0.0)) if diag else (lambda ch, a_: a_)
            z = {ch: mm_nt(qc[ch], k) for ch in chains}
            da = {ch: mm_nt(doc[ch], v) for ch in chains}
            sp = {ch: _softplus(z[ch], causal[ch[0]], diag) for ch in chains}
            later = {ch: mm(sp[ch], tri_gt) for ch in chains}
            walked = {ch: carry[ch[0]][1 + ch[1]] + rowsum(sp[ch]) for ch in chains}
            a = {ch: keep(ch, jnp.exp((z[ch] - sp[ch]) - ((ltot[ch] - walked[ch]) + later[ch]))) for ch in chains}
            g = {ch: a[ch] * da[ch] for ch in chains}
            upto = {ch: mm(g[ch], tri_le) for ch in chains}
            dz = {ch: keep(ch, g[ch] - jnp.exp(z[ch] - sp[ch]) * (carry[ch[0]][3 + ch[1]] + upto[ch])).astype(BF16)
                  for ch in chains}
            dqc = {ch: mm(dz[ch], kh[ch[1]]) for ch in chains}
            dkc = [mm_tn(dz[ch], qc[ch]) for ch in chains]
            dvc = [mm_tn(a[ch], doc[ch]) for ch in chains]
            dk_acc[pl.ds(start, tq), :] += sum(dkc[1:], dkc[0])
            dv_acc[pl.ds(start, tq), :] += sum(dvc[1:], dvc[0])
            return tuple((carry[r][0] + dqc[(r, 0)] + dqc[(r, 1)], walked[(r, 0)], walked[(r, 1)],
                          carry[r][3] + rowsum(g[(r, 0)]), carry[r][4] + rowsum(g[(r, 1)])) for r in range(n_rc))

        zc = jnp.zeros((rc, 1), F32)
        carry = tuple((jnp.zeros((rc, LANES), F32), zc, zc, zc, zc) for _ in range(n_rc))
        near = jnp.maximum(qi - 1, 0)
        first = near - jnp.clip(nblk_ref[hp, qi].astype(jnp.int32), 0, near)
        carry = lax.fori_loop(first, qi, block, carry)
        carry = block(qi, carry, True)
        for r in range(n_rc):
            dq_ref[r * rc:(r + 1) * rc, :] = (carry[r][0] * 0.125).astype(BF16)

        @pl.when(qi == pl.num_programs(1) - 1)
        def _():
            dk_ref[...] = dk_acc[...].astype(BF16)
            dv_ref[...] = dv_acc[...].astype(BF16)

    blk = pl.BlockSpec((tq, LANES), lambda hp, qi: (qi, hp))
    col = pl.BlockSpec((s, LANES), lambda hp, qi: (0, hp))
    o512 = jax.ShapeDtypeStruct((s, D_SB), BF16)
    return _hosted_call(
        body, comm, name=name, grid=(4, nq),
        out_shape=(o512, o512, o512),
        in_specs=[pl.BlockSpec(memory_space=pltpu.SMEM), blk,
                  pl.BlockSpec((s, LANES), lambda hp, qi: (0, 4 + hp)),
                  pl.BlockSpec((s, LANES), lambda hp, qi: (0, 8 + hp)), blk, blk],
        out_specs=(blk, col, col),
        scratch_shapes=[pltpu.VMEM((s, LANES), F32), pltpu.VMEM((s, LANES), F32)],
        args=(nblk, qkv, qkv, qkv, lsum, dya), sem=("parallel", "arbitrary"))


def _bwd_dproj(dmisc, dconv, pc, dq, dk, dv, x, dx2, g, cw, win_full, name, comm=None):
    s = x.shape[0]
    ts = min(ROW_TILE, s)
    blk8 = ts // SUBLANES
    last8 = s // SUBLANES - 1

    def body(dcb_ref, dcz_ref, daz_ref, dconv_ref, nxt_ref, cc_ref, ch_ref, dq_ref, dk_ref, dv_ref,
             x_ref, dx2_ref, g_ref, cw_ref, w_ref, dproj_ref, dx_ref, dg_ref):
        i = pl.program_id(0)

        @pl.when(i == 0)
        def _():
            dg_ref[...] = jnp.zeros_like(dg_ref)

        keep = jnp.where(i == pl.num_programs(0) - 1, 0.0, 1.0)
        dc = dconv_ref[...]
        n0 = nxt_ref[0:1, :] * keep
        n1 = nxt_ref[1:2, :] * keep
        rowi = lax.broadcasted_iota(jnp.int32, dc.shape, 0)
        dc1 = jnp.where(rowi == ts - 1, n0, pltpu.roll(dc, ts - 1, 0))
        dc2 = jnp.where(rowi == ts - 2, n0, jnp.where(rowi == ts - 1, n1, pltpu.roll(dc, ts - 2, 0)))
        du = cw_ref[2:3, :] * dc + cw_ref[1:2, :] * dc1 + cw_ref[0:1, :] * dc2
        dproj_ref[:, 0:512] = dcb_ref[...]
        dproj_ref[:, 512:1024] = (du * ch_ref[...].astype(F32)).astype(BF16)
        dproj_ref[:, 1024:1536] = (du * cc_ref[...].astype(F32)).astype(BF16)
        dproj_ref[:, 1536:2048] = dcz_ref[...]
        dproj_ref[:, 2048:2560] = dq_ref[...]
        dproj_ref[:, 2560:3072] = dk_ref[...]
        dproj_ref[:, 3072:3584] = dv_ref[...]
        dproj_ref[:, 3584:4096] = daz_ref[...]
        dh = lax.dot_general(dproj_ref[...], w_ref[...], NT, preferred_element_type=F32)
        x = x_ref[...]
        r = lax.rsqrt(jnp.mean(x * x, axis=-1, keepdims=True) + EPS)
        xn = x * r
        dg_ref[...] += jnp.sum(dh * xn, axis=0, keepdims=True)
        dxn = dh * g_ref[...]
        dx_ref[...] = dx2_ref[...] + r * (dxn - xn * jnp.mean(dxn * xn, axis=-1, keepdims=True))

    row = lambda width, cb_=0: pl.BlockSpec((ts, width), lambda i: (i, cb_))
    nxt = pl.BlockSpec((SUBLANES, 512), lambda i: (jnp.minimum((i + 1) * blk8, last8), 0))
    vec = lambda width: pl.BlockSpec((1, width), lambda i: (0, 0))
    return _hosted_call(
        body, comm, name=name, grid=(s // ts,),
        out_shape=(jax.ShapeDtypeStruct((s, N_IN), BF16), jax.ShapeDtypeStruct((s, D_MODEL), F32),
                   jax.ShapeDtypeStruct((1, D_MODEL), F32)),
        in_specs=[row(512, 0), row(512, 1), row(512, 2), row(512), nxt, row(512, 1), row(512, 2),
                  row(512), row(512), row(512), row(D_MODEL), row(D_MODEL), vec(D_MODEL),
                  pl.BlockSpec((3, 512), lambda i: (0, 0)),
                  pl.BlockSpec((D_MODEL, N_IN), lambda i: (0, 0))],
        out_specs=(row(N_IN), row(D_MODEL), vec(D_MODEL)),
        args=(dmisc, dmisc, dmisc, dconv, dconv, pc, pc, dq, dk, dv, x, dx2, g, cw, win_full),
        sem=("arbitrary",))


def _atb(a, b, name, a_cols=None, comm=None):
    s, n = b.shape
    m, a_blk = (a.shape[-1], 0) if a_cols is None else a_cols
    ts = min(512, s)
    tn = min(2048, n)
    a_spec = pl.BlockSpec((ts, m), lambda j, i: (i, a_blk))

    def body(a_ref, b_ref, o_ref, acc_ref):
        i = pl.program_id(1)

        @pl.when(i == 0)
        def _():
            acc_ref[...] = jnp.zeros_like(acc_ref)

        acc_ref[...] += lax.dot_general(a_ref[...].astype(BF16), b_ref[...], TN, preferred_element_type=F32)

        @pl.when(i == pl.num_programs(1) - 1)
        def _():
            o_ref[...] = acc_ref[...].astype(BF16)

    (out,), got = _hosted_call(
        body, comm, name=name, grid=(n // tn, s // ts),
        out_shape=(jax.ShapeDtypeStruct((m, n), BF16),),
        in_specs=[a_spec, pl.BlockSpec((ts, tn), lambda j, i: (i, j))],
        out_specs=(pl.BlockSpec((m, tn), lambda j, i: (0, j)),),
        scratch_shapes=[pltpu.VMEM((m, tn), F32)],
        args=(a, b), sem=("parallel", "arbitrary"))
    return out, got


def _adamw_math(w, g, m, v):
    m2 = ADAM_B1 * m + (1.0 - ADAM_B1) * g
    v2 = ADAM_B2 * v + (1.0 - ADAM_B2) * (g * g)
    m_hat = m2 / (1.0 - ADAM_B1 ** ADAM_STEP)
    v_hat = v2 / (1.0 - ADAM_B2 ** ADAM_STEP)
    delta = -ADAM_LR * (m_hat / (jnp.sqrt(v_hat) + ADAM_EPS) + ADAM_WD * w)
    return delta, m2, v2


def _adamw_sum8(pieces, w, m, v, name):
    _, rows, cols = w.shape
    tr = min([rows, 256] + [pc_[0].shape[1] for pc_ in pieces])
    n_tiles = rows // tr
    n_p = len(pieces)
    spans = [(layer, row0 // tr, arr.shape[1] // tr) for arr, layer, row0 in pieces]

    def body(*refs):
        p_refs = refs[:n_p]
        w_ref, m_ref, v_ref, g_ref, d_ref, m2_ref, v2_ref = refs[n_p:]
        l, i = pl.program_id(0), pl.program_id(1)

        def run(p_ref):
            g = p_ref[0].astype(F32)
            for d in range(1, N_DEV):
                g = g + p_ref[d].astype(F32)
            g_ref[...] = g
            d_ref[...], m2_ref[...], v2_ref[...] = _adamw_math(w_ref[...], g, m_ref[...], v_ref[...])

        for p_ref, (layer, t0, nt) in zip(p_refs, spans):
            mine = jnp.logical_and(l == layer, jnp.logical_and(i >= t0, i < t0 + nt))
            pl.when(mine)(lambda p_ref=p_ref: run(p_ref))

    def piece_spec(layer, t0, nt):
        return pl.BlockSpec((N_DEV, tr, cols),
                            lambda l, i: (0, jnp.clip(jnp.where(l == layer, i - t0, jnp.where(l < layer, 0, nt - 1)),
                                                      0, nt - 1), 0))

    tile = pl.BlockSpec((None, tr, cols), lambda l, i: (l, i, 0))
    o = jax.ShapeDtypeStruct((DEPTH, rows, cols), F32)
    return _call(
        body, name=name, grid=(DEPTH, n_tiles),
        out_shape=(o, o, o, o),
        in_specs=[*[piece_spec(*sp) for sp in spans], tile, tile, tile],
        out_specs=(tile, tile, tile, tile),
        compiler_params=_params(("arbitrary", "arbitrary"), VMEM_LIMIT),
    )(*[pc_[0] for pc_ in pieces], w, m, v)


def _adamw_plain(g, w, m, v, name):
    rows, cols = g.shape

    def body(g_ref, w_ref, m_ref, v_ref, d_ref, m2_ref, v2_ref):
        d_ref[...], m2_ref[...], v2_ref[...] = _adamw_math(w_ref[...], g_ref[...], m_ref[...], v_ref[...])

    full = pl.BlockSpec((rows, cols), lambda: (0, 0))
    o = jax.ShapeDtypeStruct((rows, cols), F32)
    return _call(body, name=name, out_shape=(o, o, o), in_specs=[full] * 4, out_specs=(full,) * 3)(g, w, m, v)


def _small_update(parts, params):
    n = len(params)
    rows = [w.shape[0] for w, _, _ in params]
    offs = [sum(rows[:k]) for k in range(n)]

    def body(*refs):
        p_ref, wmv, outs = refs[0], refs[1:1 + 3 * n], refs[1 + 3 * n:]
        g = p_ref[0]
        for d in range(1, N_DEV):
            g = g + p_ref[d]
        outs[0][...] = g
        for k in range(n):
            gk = g[offs[k]:offs[k] + rows[k]]
            w_ref, m_ref, v_ref = wmv[3 * k:3 * k + 3]
            g_ref, d_ref, m2_ref, v2_ref = outs[1 + 4 * k:5 + 4 * k]
            g_ref[...] = gk
            d_ref[...], m2_ref[...], v2_ref[...] = _adamw_math(w_ref[...], gk, m_ref[...], v_ref[...])

    whole = lambda shape: pl.BlockSpec(shape, lambda: (0,) * len(shape))
    flat_in = [a for wmv in params for a in wmv]
    out_shape = [jax.ShapeDtypeStruct((SMALL_ROWS, LANES), F32)]
    for r in rows:
        out_shape += [jax.ShapeDtypeStruct((r, LANES), F32)] * 4
    outs = _call(
        body, name="adamw_small",
        out_shape=tuple(out_shape),
        in_specs=[whole(parts.shape)] + [whole(a.shape) for a in flat_in],
        out_specs=tuple(whole(o.shape) for o in out_shape),
    )(parts, *flat_in)
    return outs[0], [outs[1 + 4 * k:5 + 4 * k] for k in range(n)]


def kernel(x, p, norm_g, w_in, conv_w, conv_b, branch_g, w_out, ple_norm_g, w_pg, b_pg, w_pe, final_g, loss_target, m_norm_g, m_w_in, m_conv_w, m_conv_b, m_branch_g, m_w_out, m_ple_norm_g, m_w_pg, m_b_pg, m_w_pe, m_final_g, v_norm_g, v_w_in, v_conv_w, v_conv_b, v_branch_g, v_w_out, v_ple_norm_g, v_w_pg, v_b_pg, v_w_pe, v_final_g):
    s = x.shape[1]
    x0 = x.reshape(s, D_MODEL)
    target = loss_target.reshape(s, D_MODEL)
    me_blk = _my_block()

    win_s, wout_s, wpg_s, wpe_s = _cast_bf16(
        [w_in.reshape(DEPTH * D_MODEL, 512), w_out.reshape(DEPTH * 128, D_MODEL),
         w_pg.reshape(DEPTH * 128, D_MODEL), w_pe.reshape(DEPTH * PLE_DIM, 128)], "cast_weights")
    win_s, wout_s = win_s.reshape(DEPTH, D_MODEL, 512), wout_s.reshape(DEPTH, 128, D_MODEL)
    wpg_s, wpe_s = wpg_s.reshape(DEPTH, 128, D_MODEL), wpe_s.reshape(DEPTH, PLE_DIM, 128)
    cw_s = jnp.zeros((SUBLANES, LANES), F32).at[:DEPTH * 3, :HEAD_DIM].set(conv_w.reshape(DEPTH * 3, HEAD_DIM))
    bf = lambda r_, c_: jax.ShapeDtypeStruct((r_, c_), BF16)
    w_items = lambda l: [(wout_s[l], bf(D_MODEL, D_MODEL), "rows128"), (wpg_s[l], bf(D_MODEL, D_MODEL), "rows128"),
                         (wpe_s[l], bf(PLE_DIM, D_MODEL), "cols128")]
    win_f = [None] * DEPTH
    win_f[0], cw_all = _comm_call(_gather_comm([
        (win_s[0], bf(D_MODEL, N_IN), "cols512"),
        (cw_s, jax.ShapeDtypeStruct((N_DEV, SUBLANES, LANES), F32), "slot")]), "gather_w_in_0")
    cw_full = jnp.transpose(cw_all[:, :DEPTH * 3, :HEAD_DIM].reshape(N_DEV, DEPTH, 3, HEAD_DIM), (1, 2, 0, 3))
    cw_full = cw_full.reshape(DEPTH, 3, D_CONV)
    gather_rest_0 = _gather_comm(w_items(0))
    gather_win_1 = _gather_comm([(win_s[1], bf(D_MODEL, N_IN), "cols512")])
    gather_rest_1 = _gather_comm(w_items(1))

    vec = lambda a, l: a[l][None, :]

    saved = []
    xl = x0
    wout_f, wpg_f, wpe_f = [None] * DEPTH, [None] * DEPTH, [None] * DEPTH
    for l in range(DEPTH):
        (h, pc, qkv, az), got = _fwd_in(xl, vec(norm_g, l), win_f[l], f"fwd_in_{l}",
                                        comm=gather_rest_0 if l == 0 else None)
        if l == 0:
            wout_f[0], wpg_f[0], wpe_f[0] = got
        (ya, lsum, nblk), got = _attn_fwd(qkv, f"attn_fwd_{l}", comm=gather_win_1 if l == 0 else None)
        if l == 0:
            (win_f[1],) = got
        (x2, x3, gated, h2, gate, e), got = _fwd_mid(
            xl, pc, az, ya, p, l, cw_full[l], vec(conv_b, l), vec(branch_g, l), wout_f[l],
            vec(ple_norm_g, l), wpg_f[l], vec(b_pg, l), wpe_f[l], f"fwd_mid_{l}",
            comm=gather_rest_1 if l == 0 else None)
        if l == 0:
            wout_f[1], wpg_f[1], wpe_f[1] = got
        saved.append(dict(x=xl, h=h, pc=pc, qkv=qkv, az=az, ya=ya, lsum=lsum, nblk=nblk, x2=x2, gated=gated, h2=h2,
                          gate=gate, e=e))
        xl = x3

    dx, loss_acc, d_final_g = _loss_head(xl, target, final_g[None, :])

    dwin, dwout, dwpg, dwpe = [None] * DEPTH, [None] * DEPTH, [None] * DEPTH, [None] * DEPTH
    small = dict(norm_g=[None] * DEPTH, conv_b=[None] * DEPTH, branch_g=[None] * DEPTH,
                 ple_norm_g=[None] * DEPTH, b_pg=[None] * DEPTH, conv_w=[None] * DEPTH)
    slot = lambda r_, c_: jax.ShapeDtypeStruct((r_, c_), BF16)
    half = D_MODEL // 2
    r_in1, r_out, r_pg, r_pe = None, [None] * DEPTH, [None] * DEPTH, [None] * DEPTH

    def rest_items(l):
        return [(dwout[l], slot(128, D_MODEL), "rows128"), (dwpg[l], slot(128, D_MODEL), "rows128"),
                (dwpe[l], slot(PLE_DIM, 128), "cols128")]

    for l in reversed(range(DEPTH)):
        sv = saved[l]
        ride = _exchange_comm([(dwin[1], slot(D_MODEL, 512), "cols512")]) if l == 0 else None
        (dx2, dya, dmisc, dconv, dwout[l], dwpg[l], dwpe[l], d_bpg, d_pg, d_bg, d_cbias, d_cw), got = _bwd_mid(
            dx, sv["x2"], sv["gate"], sv["e"], sv["pc"], sv["az"], sv["ya"], sv["gated"], sv["h2"], p, l,
            cw_full[l], vec(conv_b, l), vec(branch_g, l), vec(ple_norm_g, l), wpg_f[l], wout_f[l], f"bwd_mid_{l}",
            comm=ride)
        if l == 0:
            (r_in1,) = got
        ride = _exchange_comm(rest_items(1) + rest_items(0)) if l == 0 else None
        (dq, dk, dv), got = _attn_bwd(sv["qkv"], sv["lsum"], sv["nblk"], dya, f"attn_bwd_{l}", comm=ride)
        if l == 0:
            r_out[1], r_pg[1], r_pe[1], r_out[0], r_pg[0], r_pe[0] = got
        (dproj, dx, d_ng), _ = _bwd_dproj(dmisc, dconv, sv["pc"], dq, dk, dv, sv["x"], dx2, vec(norm_g, l),
                                          cw_full[l], win_f[l], f"bwd_dproj_{l}")
        if l == 1:
            dwin[1], _ = _atb(sv["h"], dproj, "dw_in_1")
        else:
            dwin_top, _ = _atb(sv["h"], dproj, "dw_in_0_top", a_cols=(half, 0))
            dwin_bot, (r_in0_top,) = _atb(sv["h"], dproj, "dw_in_0_bottom", a_cols=(half, 1),
                                          comm=_exchange_comm([(dwin_top, slot(half, 512), "cols512")]))
        small["norm_g"][l], small["conv_b"][l], small["branch_g"][l] = d_ng, d_cbias, d_bg
        small["ple_norm_g"][l], small["b_pg"][l], small["conv_w"][l] = d_pg, d_bpg, d_cw[:3]
    grad_x = dx.reshape(1, s, D_MODEL)

    flat = lambda parts: jnp.concatenate([a.reshape(-1) for a in parts])
    small_vec = jnp.concatenate([
        flat(small["norm_g"]), flat(small["conv_b"]), flat(small["branch_g"]), flat(small["ple_norm_g"]),
        flat(small["b_pg"]), d_final_g.reshape(-1), flat(small["conv_w"]), loss_acc.reshape(-1),
        jnp.zeros(((SMALL_ROWS - SMALL_GRAD_ROWS - 1) * LANES,), F32)]).reshape(SMALL_ROWS, LANES)
    r_in0_bot, r_small = _comm_call(_exchange_comm([
        (dwin_bot, slot(half, 512), "cols512"),
        (small_vec, jax.ShapeDtypeStruct((SMALL_ROWS, LANES), F32), "slot")]), "exchange_last")

    per_layer = lambda r: [(r[0], 0, 0), (r[1], 1, 0)]
    g_win, d_win, m_win, v_win = _adamw_sum8([(r_in0_top, 0, 0), (r_in0_bot, 0, half), (r_in1, 1, 0)],
                                             w_in, m_w_in, v_w_in, "adamw_w_in")
    g_wout, d_wout, m_wout, v_wout = _adamw_sum8(per_layer(r_out), w_out, m_w_out, v_w_out, "adamw_w_out")
    g_wpg, d_wpg, m_wpg, v_wpg = _adamw_sum8(per_layer(r_pg), w_pg, m_w_pg, v_w_pg, "adamw_w_pg")
    g_wpe, d_wpe, m_wpe, v_wpe = _adamw_sum8(per_layer(r_pe), w_pe, m_w_pe, v_w_pe, "adamw_w_pe")

    repl = [(norm_g, m_norm_g, v_norm_g), (conv_b, m_conv_b, v_conv_b), (branch_g, m_branch_g, v_branch_g),
            (ple_norm_g, m_ple_norm_g, v_ple_norm_g), (b_pg, m_b_pg, v_b_pg), (final_g, m_final_g, v_final_g)]
    g_small, upd = _small_update(r_small, [tuple(a.reshape(-1, LANES) for a in t) for t in repl])
    g_r, d_r, m_r, v_r = [[upd[k][j].reshape(repl[k][0].shape) for k in range(len(repl))] for j in range(4)]

    loss = g_small[SMALL_GRAD_ROWS, 0]
    g_cw_full = g_small[SMALL_REPL_ROWS:SMALL_GRAD_ROWS].reshape(DEPTH, 3, D_CONV)
    g_cw = lax.dynamic_slice(g_cw_full, (0, 0, me_blk * HEAD_DIM), (DEPTH, 3, HEAD_DIM))
    pad_cw = lambda a: jnp.zeros((SUBLANES, LANES), F32).at[:3].set(a.reshape(3, LANES))
    v_cw_pad = jnp.ones((SUBLANES, LANES), F32).at[:3].set(v_conv_w.reshape(3, LANES))
    d_cw, m_cw, v_cw = _adamw_plain(pad_cw(g_cw), pad_cw(conv_w), pad_cw(m_conv_w), v_cw_pad, "adamw_conv_w")
    un_cw = lambda a: a[:3].reshape(DEPTH, 3, HEAD_DIM)

    def ordered(r, win_, cw_, wout_, wpg_, wpe_):
        return [r[0], win_, cw_, r[1], r[2], wout_, r[3], wpg_, r[4], wpe_, r[5]]

    grads = ordered(g_r, g_win, g_cw, g_wout, g_wpg, g_wpe)
    deltas = ordered(d_r, d_win, un_cw(d_cw), d_wout, d_wpg, d_wpe)
    new_m = ordered(m_r, m_win, un_cw(m_cw), m_wout, m_wpg, m_wpe)
    new_v = ordered(v_r, v_win, un_cw(v_cw), v_wout, v_wpg, v_wpe)
    return (loss, grad_x, *grads, *deltas, *new_m, *new_v)
```

```python
import jax
import jax.numpy as jnp
from jax import lax
from jax.experimental import pallas as pl
from jax.experimental.pallas import tpu as pltpu

F32 = jnp.float32
BF16 = jnp.bfloat16

D_MODEL = 1024
D_CONV = 512
D_SB = 512
N_IN = 4096
HEAD_DIM = 64
PLE_DIM = 256
DEPTH = 2
EPS = 1e-6
ADAM_LR = 0.001
ADAM_B1 = 0.9
ADAM_B2 = 0.999
ADAM_EPS = 1e-08
ADAM_WD = 0.01
ADAM_STEP = 10

LANES = 128
SUBLANES = 8
VMEM_BYTES_V7X = 64 * 1024 * 1024
VMEM_LIMIT = VMEM_BYTES_V7X - 8 * 1024 * 1024

N_DEV = 8
ROW_TILE = 256
ATTN_TILE = 256
SMALL_GRAD_ROWS = 104
SMALL_REPL_ROWS = 80
SMALL_ROWS = 112

NT = (((1,), (1,)), ((), ()))
TN = (((0,), (0,)), ((), ()))


def _call(body, **kw):
    return pl.pallas_call(body, **kw)


def _params(sem=None, vmem=None):
    return pltpu.CompilerParams(dimension_semantics=sem, vmem_limit_bytes=vmem)


def _sigmoid(z):
    return 0.5 * jnp.tanh(0.5 * z) + 0.5


def _group_bcast_sum(a, lo):
    s_lo = jnp.sum(jnp.where(lo, a, 0.0), axis=-1, keepdims=True)
    s_hi = jnp.sum(jnp.where(lo, 0.0, a), axis=-1, keepdims=True)
    return jnp.where(lo, s_lo, s_hi)


def _my_block():
    return 4 * lax.axis_index("x") + 2 * lax.axis_index("y") + lax.axis_index("c")


def _cast_bf16(arrays, name):
    n = len(arrays)

    def body(*refs):
        for a_ref, o_ref in zip(refs[:n], refs[n:]):
            o_ref[...] = a_ref[...].astype(BF16)

    whole = lambda a: pl.BlockSpec(a.shape, lambda: (0, 0))
    return _call(
        body, name=name,
        out_shape=tuple(jax.ShapeDtypeStruct(a.shape, BF16) for a in arrays),
        in_specs=[whole(a) for a in arrays], out_specs=tuple(whole(a) for a in arrays),
        compiler_params=_params(None, VMEM_LIMIT),
    )(*arrays)


class _Comm:
    def __init__(self, inputs, out_shapes, scratch, begin, middle, finish):
        self.inputs, self.out_shapes, self.scratch = list(inputs), list(out_shapes), list(scratch)
        self.begin, self.middle, self.finish = begin, middle, finish


def _slab(kind, ref, blk):
    if kind == "cols512":
        return ref.at[:, pl.ds(blk * 512, 512)]
    if kind == "rows128":
        return ref.at[pl.ds(blk * 128, 128), :]
    if kind == "cols128":
        return ref.at[:, pl.ds(blk * 128, 128)]
    return ref.at[blk]


def _gather_comm(items):
    n_t = len(items)
    kinds = [it[2] for it in items]

    def ctx(ins, outs, sems):
        send_sems, recv_sems, local_sems = sems
        x, y, c = lax.axis_index("x"), lax.axis_index("y"), lax.axis_index("c")
        me, sibling = (x, y, c), (x, y, 1 - c)
        chips = [(1 - x, y), (x, 1 - y), (1 - x, 1 - y)]

        def place(t, dev):
            return _slab(kinds[t], outs[t], 4 * dev[0] + 2 * dev[1] + dev[2])

        def copy(t, k, block, to, own=False):
            return pltpu.make_async_remote_copy(
                src_ref=ins[t] if own else place(t, block), dst_ref=place(t, block),
                send_sem=send_sems.at[t, k], recv_sem=recv_sems.at[t, k],
                device_id=to, device_id_type=pl.DeviceIdType.MESH)

        mine = [pltpu.make_async_copy(ins[t], place(t, me), local_sems.at[t]) for t in range(n_t)]
        first = []
        for t in range(n_t):
            first.append(copy(t, 0, me, sibling, own=True))
            first += [copy(t, 1 + j, me, (*chip, c), own=True) for j, chip in enumerate(chips)]
        passed = [copy(t, 4 + j, (*chip, c), sibling) for j, chip in enumerate(chips) for t in range(n_t)]
        landed = [copy(t, 1 + j, (*chip, c), me) for j, chip in enumerate(chips) for t in range(n_t)]
        from_sibling = []
        for t in range(n_t):
            from_sibling.append(copy(t, 0, sibling, me))
            from_sibling += [copy(t, 4 + j, (*chip, 1 - c), me) for j, chip in enumerate(chips)]
        return mine, first, landed, passed, from_sibling

    def begin(ins, outs, sems):
        mine, first, _, _, _ = ctx(ins, outs, sems)
        for cp in mine + first:
            cp.start()

    def middle(ins, outs, sems):
        _, _, landed, passed, _ = ctx(ins, outs, sems)
        for got, fwd in zip(landed, passed):
            got.wait_recv()
            fwd.start()

    def finish(ins, outs, sems):
        mine, first, _, passed, from_sibling = ctx(ins, outs, sems)
        for cp in from_sibling:
            cp.wait_recv()
        for cp in first + passed:
            cp.wait_send()
        for cp in mine:
            cp.wait()

    scratch = [pltpu.SemaphoreType.DMA((n_t, 7)), pltpu.SemaphoreType.DMA((n_t, 7)), pltpu.SemaphoreType.DMA((n_t,))]
    return _Comm([it[0] for it in items], [it[1] for it in items], scratch, begin, middle, finish)


def _exchange_comm(items):
    n_t = len(items)
    kinds = [it[2] for it in items]

    def ctx(ins, outs, sems):
        send_sems, recv_sems, local_sems = sems
        x, y, c = lax.axis_index("x"), lax.axis_index("y"), lax.axis_index("c")
        me_blk = 4 * x + 2 * y + c

        def src(t, blk):
            return ins[t] if kinds[t] == "slot" else _slab(kinds[t], ins[t], blk)

        local = [pltpu.make_async_copy(src(t, me_blk), outs[t].at[me_blk], local_sems.at[t]) for t in range(n_t)]
        remote = []
        for k in range(1, N_DEV):
            px = 1 - x if k & 4 else x
            py = 1 - y if k & 2 else y
            pc_ = 1 - c if k & 1 else c
            for t in range(n_t):
                remote.append(pltpu.make_async_remote_copy(
                    src_ref=src(t, 4 * px + 2 * py + pc_), dst_ref=outs[t].at[me_blk],
                    send_sem=send_sems.at[k - 1, t], recv_sem=recv_sems.at[k - 1, t],
                    device_id=(px, py, pc_), device_id_type=pl.DeviceIdType.MESH))
        return local, remote

    def begin(ins, outs, sems):
        local, remote = ctx(ins, outs, sems)
        for cp in local + remote:
            cp.start()

    def finish(ins, outs, sems):
        local, remote = ctx(ins, outs, sems)
        for cp in remote:
            cp.wait_recv()
        for cp in remote:
            cp.wait_send()
        for cp in local:
            cp.wait()

    scratch = [pltpu.SemaphoreType.DMA((N_DEV - 1, n_t)), pltpu.SemaphoreType.DMA((N_DEV - 1, n_t)),
               pltpu.SemaphoreType.DMA((n_t,))]
    out_shapes = [jax.ShapeDtypeStruct((N_DEV, *it[1].shape), it[1].dtype) for it in items]
    return _Comm([it[0] for it in items], out_shapes, scratch, begin, None, finish)


def _comm_call(comm, name):
    n_in, n_out = len(comm.inputs), len(comm.out_shapes)

    def body(*refs):
        ins, outs, sems = refs[:n_in], refs[n_in:n_in + n_out], refs[n_in + n_out:]
        comm.begin(ins, outs, sems)
        if comm.middle is not None:
            comm.middle(ins, outs, sems)
        comm.finish(ins, outs, sems)

    any_spec = pl.BlockSpec(memory_space=pl.ANY)
    return _call(body, name=name, out_shape=tuple(comm.out_shapes), in_specs=[any_spec] * n_in,
                 out_specs=[any_spec] * n_out, scratch_shapes=comm.scratch)(*comm.inputs)


def _hosted(body, n_in, n_out, comm, first, last, middle):
    if comm is None:
        return lambda *refs: body(*refs)
    n_ci, n_co, n_cs = len(comm.inputs), len(comm.out_shapes), len(comm.scratch)

    def wrapped(*refs):
        ins, cin = refs[:n_in], refs[n_in:n_in + n_ci]
        o0 = n_in + n_ci
        outs, cout = refs[o0:o0 + n_out], refs[o0 + n_out:o0 + n_out + n_co]
        scr, csem = refs[o0 + n_out + n_co:len(refs) - n_cs], refs[len(refs) - n_cs:]
        pl.when(first())(lambda: comm.begin(cin, cout, csem))
        body(*ins, *outs, *scr)
        if comm.middle is not None:
            pl.when(middle())(lambda: comm.middle(cin, cout, csem))
        pl.when(last())(lambda: comm.finish(cin, cout, csem))

    return wrapped


def _hosted_call(body, comm, *, name, grid, out_shape, in_specs, out_specs, args, scratch_shapes=(), sem=None):
    nd = len(grid)
    first, last, middle = _at_first(nd), _at_last(nd), _at_middle(nd)
    if comm is not None:
        sem = ("arbitrary",) * nd
    n_in, n_out = len(in_specs), len(out_shape)
    any_spec = pl.BlockSpec(memory_space=pl.ANY)
    c_in = [] if comm is None else comm.inputs
    c_out = [] if comm is None else comm.out_shapes
    c_scr = [] if comm is None else comm.scratch
    outs = _call(
        _hosted(body, n_in, n_out, comm, first, last, middle), name=name, grid=grid,
        out_shape=(*out_shape, *c_out),
        in_specs=[*in_specs, *[any_spec] * len(c_in)],
        out_specs=(*out_specs, *[any_spec] * len(c_out)),
        scratch_shapes=[*scratch_shapes, *c_scr],
        compiler_params=_params(sem, VMEM_LIMIT),
    )(*args, *c_in)
    return outs[:n_out], outs[n_out:]


def _grid_step(ndim):
    i, n = pl.program_id(0), pl.num_programs(0)
    for d in range(1, ndim):
        i, n = i * pl.num_programs(d) + pl.program_id(d), n * pl.num_programs(d)
    return i, n


def _at_first(ndim):
    return lambda: _grid_step(ndim)[0] == 0


def _at_last(ndim):
    def pred():
        i, n = _grid_step(ndim)
        return i == n - 1
    return pred


def _at_middle(ndim):
    def pred():
        i, n = _grid_step(ndim)
        return i == (3 * n) // 4
    return pred


def _fwd_in(x, g, w_full, name, comm=None):
    s = x.shape[0]
    ts = min(ROW_TILE, s)

    def body(x_ref, g_ref, w_ref, h_ref, pc_ref, qkv_ref, az_ref):
        xf = x_ref[...]
        r = lax.rsqrt(jnp.mean(xf * xf, axis=-1, keepdims=True) + EPS)
        h = (xf * r * g_ref[...]).astype(BF16)
        h_ref[...] = h
        pc_ref[...] = jnp.dot(h, w_ref[:, 0:2048], preferred_element_type=F32).astype(BF16)
        q = jnp.dot(h, w_ref[:, 2048:2560], preferred_element_type=F32)
        qkv_ref[:, 0:512] = (q * 0.125).astype(BF16)
        qkv_ref[:, 512:1536] = jnp.dot(h, w_ref[:, 2560:3584], preferred_element_type=F32).astype(BF16)
        az_ref[...] = jnp.dot(h, w_ref[:, 3584:4096], preferred_element_type=F32).astype(BF16)

    row = lambda width: pl.BlockSpec((ts, width), lambda i: (i, 0))
    return _hosted_call(
        body, comm, name=name, grid=(s // ts,),
        out_shape=(jax.ShapeDtypeStruct((s, D_MODEL), BF16), jax.ShapeDtypeStruct((s, 2048), BF16),
                   jax.ShapeDtypeStruct((s, 1536), BF16), jax.ShapeDtypeStruct((s, 512), BF16)),
        in_specs=[row(D_MODEL), pl.BlockSpec((1, D_MODEL), lambda i: (0, 0)),
                  pl.BlockSpec((D_MODEL, N_IN), lambda i: (0, 0))],
        out_specs=(row(D_MODEL), row(2048), row(1536), row(512)),
        args=(x, g, w_full), sem=("parallel",))


ATTN_ROWS = 128
ATTN_DONE = 104.0


def _attn_pieces(tq, rc):
    lane = lax.broadcasted_iota(jnp.int32, (1, LANES), 1)
    lo = lane < HEAD_DIM
    row = lax.broadcasted_iota(jnp.int32, (tq, tq), 0)
    col = lax.broadcasted_iota(jnp.int32, (tq, tq), 1)
    tri_gt = jnp.where(row > col, 1.0, 0.0).astype(BF16)
    tri_le = jnp.where(row <= col, 1.0, 0.0).astype(BF16)
    rrow = lax.broadcasted_iota(jnp.int32, (rc, tq), 0)
    rcol = lax.broadcasted_iota(jnp.int32, (rc, tq), 1)
    causal = [rcol < rrow + r * rc for r in range(tq // rc)]
    return lo, causal, tri_gt, tri_le


def _split_heads(a, lo):
    z = jnp.zeros_like(a)
    return (jnp.where(lo, a, z), jnp.where(lo, z, a))


def _softplus(z, causal, diag):
    neg_abs = lax.bitcast_convert_type(lax.bitcast_convert_type(z, jnp.uint32) | jnp.uint32(0x80000000), F32)
    sp = jnp.maximum(z, 0.0) + jnp.log(1.0 + jnp.exp(neg_abs))
    if diag:
        sp = jnp.where(causal, sp, 0.0)
    return sp


def _attn_fwd(qkv, name, comm=None):
    s = qkv.shape[0]
    tq = min(ATTN_TILE, s)
    nq = s // tq
    rc = min(ATTN_ROWS, tq)
    n_rc = tq // rc
    chains = [(r, hh) for r in range(n_rc) for hh in range(2)]

    def body(q_ref, k_ref, v_ref, o_ref, lsum_ref, nblk_ref):
        hp, qi = pl.program_id(0), pl.program_id(1)
        lo, causal, tri_gt, _ = _attn_pieces(tq, rc)
        qh = _split_heads(q_ref[...], lo)
        qc = {(r, hh): qh[hh][r * rc:(r + 1) * rc] for r, hh in chains}

        mm = lambda a_, b_: jnp.dot(a_.astype(BF16), b_, preferred_element_type=F32)
        rowsum = lambda a_: jnp.sum(a_, axis=-1, keepdims=True)

        def block(kb, carry):
            start = pl.multiple_of(kb * tq, tq)
            k = k_ref[pl.ds(start, tq), :]
            vh = _split_heads(v_ref[pl.ds(start, tq), :], lo)
            z = {ch: lax.dot_general(qc[ch], k, NT, preferred_element_type=F32) for ch in chains}
            sp = {ch: _softplus(z[ch], None, False) for ch in chains}
            later = {ch: mm(sp[ch], tri_gt) for ch in chains}
            a = {ch: jnp.exp((z[ch] - sp[ch]) - (carry[ch[0]][1 + ch[1]] + later[ch])) for ch in chains}
            pv = {ch: mm(a[ch], vh[ch[1]]) for ch in chains}
            return tuple((carry[r][0] + pv[(r, 0)] + pv[(r, 1)],
                          carry[r][1] + rowsum(sp[(r, 0)]), carry[r][2] + rowsum(sp[(r, 1)])) for r in range(n_rc))

        def first_two(prev_ok):
            d0 = pl.multiple_of(qi * tq, tq)
            p0 = pl.multiple_of(jnp.maximum(qi - 1, 0) * tq, tq)
            k_d, k_p = k_ref[pl.ds(d0, tq), :], k_ref[pl.ds(p0, tq), :]
            vh_d = _split_heads(v_ref[pl.ds(d0, tq), :], lo)
            vh_p = _split_heads(v_ref[pl.ds(p0, tq), :], lo)
            z_d = {ch: lax.dot_general(qc[ch], k_d, NT, preferred_element_type=F32) for ch in chains}
            z_p = {ch: lax.dot_general(qc[ch], k_p, NT, preferred_element_type=F32) for ch in chains}
            sp_d = {ch: _softplus(z_d[ch], causal[ch[0]], True) for ch in chains}
            sp_raw = {ch: _softplus(z_p[ch], None, False) for ch in chains}
            sp_p = {ch: jnp.where(prev_ok, sp_raw[ch], 0.0) for ch in chains}
            later_d = {ch: mm(sp_d[ch], tri_gt) for ch in chains}
            later_p = {ch: mm(sp_p[ch], tri_gt) for ch in chains}
            c_d = {ch: rowsum(sp_d[ch]) for ch in chains}
            a_d = {ch: jnp.where(causal[ch[0]], jnp.exp((z_d[ch] - sp_d[ch]) - later_d[ch]), 0.0) for ch in chains}
            a_p = {ch: jnp.where(prev_ok, jnp.exp((z_p[ch] - sp_raw[ch]) - (c_d[ch] + later_p[ch])), 0.0)
                   for ch in chains}
            pv = {ch: mm(a_d[ch], vh_d[ch[1]]) + mm(a_p[ch], vh_p[ch[1]]) for ch in chains}
            return tuple((pv[(r, 0)] + pv[(r, 1)],
                          c_d[(r, 0)] + rowsum(sp_p[(r, 0)]), c_d[(r, 1)] + rowsum(sp_p[(r, 1)]))
                         for r in range(n_rc))

        def least(carry):
            m = jnp.minimum(carry[0][1], carry[0][2])
            for r in range(1, n_rc):
                m = jnp.minimum(m, jnp.minimum(carry[r][1], carry[r][2]))
            return jnp.min(m)

        carry = first_two(qi > 0)

        def go_on(st):
            return jnp.logical_and(st[0] < qi - 1, st[1] < ATTN_DONE)

        def step(st):
            new = block(qi - 2 - st[0], st[2])
            return st[0] + 1, least(new), new

        walked, _, carry = lax.while_loop(go_on, step, (jnp.int32(0), least(carry), carry))
        for r in range(n_rc):
            o_ref[r * rc:(r + 1) * rc, :] = carry[r][0].astype(BF16)
            lsum_ref[r * rc:(r + 1) * rc, :] = jnp.where(lo, carry[r][1], carry[r][2])
        nblk_ref[hp, qi] = walked.astype(F32)

    blk = pl.BlockSpec((tq, LANES), lambda hp, qi: (qi, hp))
    o512 = jax.ShapeDtypeStruct((s, D_SB), F32)
    return _hosted_call(
        body, comm, name=name, grid=(4, nq),
        out_shape=(jax.ShapeDtypeStruct((s, D_SB), BF16), o512, jax.ShapeDtypeStruct((4, nq), F32)),
        in_specs=[blk, pl.BlockSpec((s, LANES), lambda hp, qi: (0, 4 + hp)),
                  pl.BlockSpec((s, LANES), lambda hp, qi: (0, 8 + hp))],
        out_specs=(blk, blk, pl.BlockSpec(memory_space=pltpu.SMEM)),
        args=(qkv, qkv, qkv), sem=("arbitrary", "arbitrary"))


HALO = 16


def _conv_taps(cc_ref, ch_ref, ccp_ref, chp_ref, halo_ref, first):
    u = cc_ref[...].astype(F32) * ch_ref[...].astype(F32)
    halo_ref[...] = ccp_ref[...].astype(F32) * chp_ref[...].astype(F32) * jnp.where(first, 0.0, 1.0)
    p6 = halo_ref[HALO - 2:HALO - 1, :]
    p7 = halo_ref[HALO - 1:HALO, :]
    rowi = lax.broadcasted_iota(jnp.int32, u.shape, 0)
    u1 = jnp.where(rowi == 0, p7, pltpu.roll(u, 1, 0))
    u2 = jnp.where(rowi == 0, p6, jnp.where(rowi == 1, p7, pltpu.roll(u, 2, 0)))
    return u, u1, u2


def _fwd_mid(x, pc, az, ya, p4, layer, cw, cb, bg, wout_full, pg, wpg_full, bpg, wpe_full, name, comm=None):
    s = x.shape[0]
    ts = min(ROW_TILE, s)
    blk_h = ts // HALO

    def body(x_ref, cb_ref_, cc_ref, ch_ref, cz_ref, ccp_ref, chp_ref, az_ref, ya_ref, p_ref,
             cw_ref, cbias_ref, bg_ref, wout_ref, pg_ref, wpg_ref, bpg_ref, wpe_ref,
             x2_ref, x3_ref, gated_ref, h2_ref, gate_ref, e_ref, halo_ref):
        i = pl.program_id(0)
        lane = lax.broadcasted_iota(jnp.int32, (1, LANES), 1)
        lo = lane < HEAD_DIM
        u, u1, u2 = _conv_taps(cc_ref, ch_ref, ccp_ref, chp_ref, halo_ref, i == 0)
        conv = cbias_ref[...] + cw_ref[0:1, :] * u2 + cw_ref[1:2, :] * u1 + cw_ref[2:3, :] * u
        yc = cb_ref_[...].astype(F32) * conv
        for sl in range(8):
            cols = slice(LANES * (sl % 4), LANES * (sl % 4 + 1))
            y = yc[:, cols] if sl < 4 else ya_ref[:, cols].astype(F32)
            zc = (cz_ref[:, cols] if sl < 4 else az_ref[:, cols]).astype(F32)
            rg = lax.rsqrt(_group_bcast_sum(y * y, lo) * (1.0 / HEAD_DIM) + EPS)
            yn = y * rg * bg_ref[:, LANES * sl:LANES * (sl + 1)]
            gated_ref[:, LANES * sl:LANES * (sl + 1)] = (yn * (zc * _sigmoid(zc))).astype(BF16)
        x2 = x_ref[...] + jnp.dot(gated_ref[...], wout_ref[...], preferred_element_type=F32)
        x2_ref[...] = x2
        r2 = lax.rsqrt(jnp.mean(x2 * x2, axis=-1, keepdims=True) + EPS)
        h2 = (x2 * r2 * pg_ref[...]).astype(BF16)
        h2_ref[...] = h2
        gate = _sigmoid(jnp.dot(h2, wpg_ref[...], preferred_element_type=F32) + bpg_ref[...])
        gate_ref[...] = gate.astype(BF16)
        e = jnp.dot(p_ref[...].astype(BF16), wpe_ref[...], preferred_element_type=F32)
        e_ref[...] = e.astype(BF16)
        x3_ref[...] = x2 + gate * e

    row = lambda width, cb_=0: pl.BlockSpec((ts, width), lambda i: (i, cb_))
    prev = lambda cb_: pl.BlockSpec((HALO, 512), lambda i: (jnp.maximum(i * blk_h - 1, 0), cb_))
    vec = lambda width: pl.BlockSpec((1, width), lambda i: (0, 0))
    wspec = lambda r_, c_: pl.BlockSpec((r_, c_), lambda i: (0, 0))
    f32o = jax.ShapeDtypeStruct((s, D_MODEL), F32)
    bfo = jax.ShapeDtypeStruct((s, D_MODEL), BF16)
    return _hosted_call(
        body, comm, name=name, grid=(s // ts,),
        out_shape=(f32o, f32o, bfo, bfo, bfo, bfo),
        scratch_shapes=[pltpu.VMEM((HALO, 512), F32)],
        in_specs=[row(D_MODEL), row(512, 0), row(512, 1), row(512, 2), row(512, 3), prev(1), prev(2),
                  row(512), row(512),
                  pl.BlockSpec((None, None, ts, PLE_DIM), lambda i: (layer, 0, i, 0)),
                  pl.BlockSpec((3, 512), lambda i: (0, 0)), vec(512), vec(D_MODEL),
                  wspec(D_MODEL, D_MODEL), vec(D_MODEL), wspec(D_MODEL, D_MODEL), vec(D_MODEL),
                  wspec(PLE_DIM, D_MODEL)],
        out_specs=(row(D_MODEL),) * 6,
        args=(x, pc, pc, pc, pc, pc, pc, az, ya, p4, cw, cb, bg, wout_full, pg, wpg_full, bpg, wpe_full),
        sem=("parallel",))


def _loss_head(xf, target, fg):
    s = xf.shape[0]
    ts = min(ROW_TILE, s)

    def body(x_ref, t_ref, g_ref, dx_ref, loss_ref, dg_ref):
        i = pl.program_id(0)

        @pl.when(i == 0)
        def _():
            loss_ref[...] = jnp.zeros_like(loss_ref)
            dg_ref[...] = jnp.zeros_like(dg_ref)

        x = x_ref[...]
        g = g_ref[...]
        r = lax.rsqrt(jnp.mean(x * x, axis=-1, keepdims=True) + EPS)
        xn = x * r
        err = xn * g - t_ref[...]
        per_row = jnp.sum(err * err, axis=-1, keepdims=True)
        loss_ref[...] += jnp.sum(per_row, axis=0, keepdims=True) * (0.5 / D_MODEL)
        dy = err * (1.0 / D_MODEL)
        dg_ref[...] += jnp.sum(dy * xn, axis=0, keepdims=True)
        dxn = dy * g
        dx_ref[...] = r * (dxn - xn * jnp.mean(dxn * xn, axis=-1, keepdims=True))

    row = pl.BlockSpec((ts, D_MODEL), lambda i: (i, 0))
    return _call(
        body, name="loss_head", grid=(s // ts,),
        out_shape=(jax.ShapeDtypeStruct((s, D_MODEL), F32), jax.ShapeDtypeStruct((1, LANES), F32),
                   jax.ShapeDtypeStruct((1, D_MODEL), F32)),
        in_specs=[row, row, pl.BlockSpec((1, D_MODEL), lambda i: (0, 0))],
        out_specs=(row, pl.BlockSpec((1, LANES), lambda i: (0, 0)), pl.BlockSpec((1, D_MODEL), lambda i: (0, 0))),
        compiler_params=_params(("arbitrary",), VMEM_LIMIT),
    )(xf, target, fg)


def _bwd_mid(dx3, x2, gate, e, pc, az, ya, gated, h2, p4, layer, cw, cb, bg, pg, wpg_full, wout_full, name,
             comm=None):
    s = x2.shape[0]
    ts = min(ROW_TILE, s)
    blk_h = ts // HALO

    def body(dx3_ref, x2_ref, gate_ref, e_ref, cb_ref_, cc_ref, ch_ref, cz_ref, ccp_ref, chp_ref, az_ref, ya_ref,
             gated_ref, h2_ref, p_ref, cw_ref, cbias_ref, bg_ref, pg_ref, wpg_ref, wout_ref,
             dx2_ref, dya_ref, dmisc_ref, dconv_ref, dwout_ref, dwpg_ref, dwpe_ref,
             dbpg_ref, dpg_ref, dbg_ref, dcbias_ref, dcw_ref,
             dgated_ref, halo_ref, acc_out, acc_pg, acc_pe):
        i = pl.program_id(0)

        @pl.when(i == 0)
        def _():
            for ref in (dbpg_ref, dpg_ref, dbg_ref, dcbias_ref, dcw_ref, acc_out, acc_pg, acc_pe):
                ref[...] = jnp.zeros_like(ref)

        lane = lax.broadcasted_iota(jnp.int32, (1, LANES), 1)
        lo = lane < HEAD_DIM
        dx3 = dx3_ref[...]
        gate = gate_ref[...].astype(F32)
        de_b = (dx3 * gate).astype(BF16)
        dgpre = dx3 * e_ref[...].astype(F32) * gate * (1.0 - gate)
        dbpg_ref[...] += jnp.sum(dgpre, axis=0, keepdims=True)
        dgpre_b = dgpre.astype(BF16)
        dh2 = lax.dot_general(dgpre_b, wpg_ref[...], NT, preferred_element_type=F32)
        acc_pe[...] += lax.dot_general(p_ref[...].astype(BF16), de_b, TN, preferred_element_type=F32)
        acc_pg[...] += lax.dot_general(h2_ref[...], dgpre_b, TN, preferred_element_type=F32)

        u, u1, u2 = _conv_taps(cc_ref, ch_ref, ccp_ref, chp_ref, halo_ref, i == 0)
        conv = cbias_ref[...] + cw_ref[0:1, :] * u2 + cw_ref[1:2, :] * u1 + cw_ref[2:3, :] * u
        c_b = cb_ref_[...].astype(F32)
        yc = c_b * conv
        fwd = []
        for sl in range(8):
            cols = slice(LANES * (sl % 4), LANES * (sl % 4 + 1))
            y = yc[:, cols] if sl < 4 else ya_ref[:, cols].astype(F32)
            zc = (cz_ref[:, cols] if sl < 4 else az_ref[:, cols]).astype(F32)
            rg = lax.rsqrt(_group_bcast_sum(y * y, lo) * (1.0 / HEAD_DIM) + EPS)
            sig = _sigmoid(zc)
            fwd.append((rg, y * rg, zc * sig, sig * (1.0 + zc * (1.0 - sig))))

        x2 = x2_ref[...]
        r2 = lax.rsqrt(jnp.mean(x2 * x2, axis=-1, keepdims=True) + EPS)
        xn2 = x2 * r2
        dpg_ref[...] += jnp.sum(dh2 * xn2, axis=0, keepdims=True)
        dxn = dh2 * pg_ref[...]
        dx2 = dx3 + r2 * (dxn - xn2 * jnp.mean(dxn * xn2, axis=-1, keepdims=True))
        dx2_ref[...] = dx2
        dx2_b = dx2.astype(BF16)
        dgated_ref[...] = lax.dot_general(dx2_b, wout_ref[...], NT, preferred_element_type=F32)
        acc_out[...] += lax.dot_general(gated_ref[...], dx2_b, TN, preferred_element_type=F32)

        for sl in range(8):
            cols = slice(LANES * (sl % 4), LANES * (sl % 4 + 1))
            wide = slice(LANES * sl, LANES * (sl + 1))
            rg, yhat, silu, dsilu = fwd[sl]
            bgs = bg_ref[:, wide]
            dgt = dgated_ref[:, wide]
            dyn = dgt * silu
            dzc = dgt * (yhat * bgs) * dsilu
            dbg_ref[:, wide] += jnp.sum(dyn * yhat, axis=0, keepdims=True)
            dyh = dyn * bgs
            dy = rg * (dyh - yhat * (_group_bcast_sum(dyh * yhat, lo) * (1.0 / HEAD_DIM)))
            if sl < 4:
                dconv = dy * c_b[:, cols]
                dmisc_ref[:, cols] = (dy * conv[:, cols]).astype(BF16)
                dmisc_ref[:, 512 + LANES * sl:512 + LANES * (sl + 1)] = dzc.astype(BF16)
                dconv_ref[:, cols] = dconv
                dcbias_ref[:, cols] += jnp.sum(dconv, axis=0, keepdims=True)
                dcw_ref[0:1, cols] += jnp.sum(dconv * u2[:, cols], axis=0, keepdims=True)
                dcw_ref[1:2, cols] += jnp.sum(dconv * u1[:, cols], axis=0, keepdims=True)
                dcw_ref[2:3, cols] += jnp.sum(dconv * u[:, cols], axis=0, keepdims=True)
            else:
                dya_ref[:, cols] = dy.astype(BF16)
                dmisc_ref[:, 1024 + LANES * (sl - 4):1024 + LANES * (sl - 3)] = dzc.astype(BF16)

        @pl.when(i == pl.num_programs(0) - 1)
        def _():
            dwout_ref[...] = acc_out[...].astype(BF16)
            dwpg_ref[...] = acc_pg[...].astype(BF16)
            dwpe_ref[...] = acc_pe[...].astype(BF16)

    row = lambda width, cb_=0: pl.BlockSpec((ts, width), lambda i: (i, cb_))
    prev = lambda cb_: pl.BlockSpec((HALO, 512), lambda i: (jnp.maximum(i * blk_h - 1, 0), cb_))
    vec = lambda width: pl.BlockSpec((1, width), lambda i: (0, 0))
    wspec = lambda r_, c_: pl.BlockSpec((r_, c_), lambda i: (0, 0))
    vo = lambda width: jax.ShapeDtypeStruct((1, width), F32)
    sq = jax.ShapeDtypeStruct((D_MODEL, D_MODEL), BF16)
    return _hosted_call(
        body, comm, name=name, grid=(s // ts,), sem=("arbitrary",),
        args=(dx3, x2, gate, e, pc, pc, pc, pc, pc, pc, az, ya, gated, h2, p4, cw, cb, bg, pg, wpg_full, wout_full),
        out_shape=(jax.ShapeDtypeStruct((s, D_MODEL), F32), jax.ShapeDtypeStruct((s, 512), BF16),
                   jax.ShapeDtypeStruct((s, 1536), BF16), jax.ShapeDtypeStruct((s, 512), F32),
                   sq, sq, jax.ShapeDtypeStruct((PLE_DIM, D_MODEL), BF16),
                   vo(D_MODEL), vo(D_MODEL), vo(D_MODEL), vo(512), jax.ShapeDtypeStruct((SUBLANES, 512), F32)),
        in_specs=[row(D_MODEL), row(D_MODEL), row(D_MODEL), row(D_MODEL),
                  row(512, 0), row(512, 1), row(512, 2), row(512, 3), prev(1), prev(2), row(512), row(512),
                  row(D_MODEL), row(D_MODEL),
                  pl.BlockSpec((None, None, ts, PLE_DIM), lambda i: (layer, 0, i, 0)),
                  pl.BlockSpec((3, 512), lambda i: (0, 0)), vec(512), vec(D_MODEL), vec(D_MODEL),
                  wspec(D_MODEL, D_MODEL), wspec(D_MODEL, D_MODEL)],
        out_specs=(row(D_MODEL), row(512), row(1536), row(512),
                   wspec(D_MODEL, D_MODEL), wspec(D_MODEL, D_MODEL), wspec(PLE_DIM, D_MODEL),
                   vec(D_MODEL), vec(D_MODEL), vec(D_MODEL), vec(512),
                   pl.BlockSpec((SUBLANES, 512), lambda i: (0, 0))),
        scratch_shapes=[pltpu.VMEM((ts, D_MODEL), F32), pltpu.VMEM((HALO, 512), F32),
                        pltpu.VMEM((D_MODEL, D_MODEL), F32), pltpu.VMEM((D_MODEL, D_MODEL), F32),
                        pltpu.VMEM((PLE_DIM, D_MODEL), F32)])


def _attn_bwd(qkv, lsum, nblk, dya, name, comm=None):
    s = qkv.shape[0]
    tq = min(ATTN_TILE, s)
    nq = s // tq
    rc = min(ATTN_ROWS, tq)
    n_rc = tq // rc
    chains = [(r, hh) for r in range(n_rc) for hh in range(2)]

    def body(nblk_ref, q_ref, k_ref, v_ref, lsum_ref, do_ref, dq_ref, dk_ref, dv_ref, dk_acc, dv_acc):
        hp, qi = pl.program_id(0), pl.program_id(1)

        @pl.when(qi == 0)
        def _():
            dk_acc[...] = jnp.zeros_like(dk_acc)
            dv_acc[...] = jnp.zeros_like(dv_acc)

        lo, causal, tri_gt, tri_le = _attn_pieces(tq, rc)
        lane = lax.broadcasted_iota(jnp.int32, (1, LANES), 1)
        qh = _split_heads(q_ref[...], lo)
        doh = _split_heads(do_ref[...].astype(BF16), lo)
        lt = lsum_ref[...]
        ltot_h = (jnp.sum(jnp.where(lane == 0, lt, 0.0), axis=-1, keepdims=True),
                  jnp.sum(jnp.where(lane == HEAD_DIM, lt, 0.0), axis=-1, keepdims=True))
        rows = lambda a_, r: a_[r * rc:(r + 1) * rc]
        qc = {(r, hh): rows(qh[hh], r) for r, hh in chains}
        doc = {(r, hh): rows(doh[hh], r) for r, hh in chains}
        ltot = {(r, hh): rows(ltot_h[hh], r) for r, hh in chains}

        mm = lambda a_, b_: jnp.dot(a_.astype(BF16), b_, preferred_element_type=F32)
        mm_nt = lambda a_, b_: lax.dot_general(a_, b_, NT, preferred_element_type=F32)
        mm_tn = lambda a_, b_: lax.dot_general(a_.astype(BF16), b_, TN, preferred_element_type=F32)
        rowsum = lambda a_: jnp.sum(a_, axis=-1, keepdims=True)

        def block(kb, carry, diag=False):
            start = pl.multiple_of(kb * tq, tq)
            k = k_ref[pl.ds(start, tq), :]
            v = v_ref[pl.ds(start, tq), :]
            kh = _split_heads(k, lo)
            keep = (lambda ch, a_: jnp.where(causal[ch[0]], a_, 0.0)) if diag else (lambda ch, a_: a_)
            z = {ch: mm_nt(qc[ch], k) for ch in chains}
            da = {ch: mm_nt(doc[ch], v) for ch in chains}
            sp = {ch: _softplus(z[ch], causal[ch[0]], diag) for ch in chains}
            later = {ch: mm(sp[ch], tri_gt) for ch in chains}
            walked = {ch: carry[ch[0]][1 + ch[1]] + rowsum(sp[ch]) for ch in chains}
            a = {ch: keep(ch, jnp.exp((z[ch] - sp[ch]) - ((ltot[ch] - walked[ch]) + later[ch]))) for ch in chains}
            g = {ch: a[ch] * da[ch] for ch in chains}
            upto = {ch: mm(g[ch], tri_le) for ch in chains}
            dz = {ch: keep(ch, g[ch] - jnp.exp(z[ch] - sp[ch]) * (carry[ch[0]][3 + ch[1]] + upto[ch])).astype(BF16)
                  for ch in chains}
            dqc = {ch: mm(dz[ch], kh[ch[1]]) for ch in chains}
            dkc = [mm_tn(dz[ch], qc[ch]) for ch in chains]
            dvc = [mm_tn(a[ch], doc[ch]) for ch in chains]
            dk_acc[pl.ds(start, tq), :] += sum(dkc[1:], dkc[0])
            dv_acc[pl.ds(start, tq), :] += sum(dvc[1:], dvc[0])
            return tuple((carry[r][0] + dqc[(r, 0)] + dqc[(r, 1)], walked[(r, 0)], walked[(r, 1)],
                          carry[r][3] + rowsum(g[(r, 0)]), carry[r][4] + rowsum(g[(r, 1)])) for r in range(n_rc))

        zc = jnp.zeros((rc, 1), F32)
        carry = tuple((jnp.zeros((rc, LANES), F32), zc, zc, zc, zc) for _ in range(n_rc))
        near = jnp.maximum(qi - 1, 0)
        first = near - jnp.clip(nblk_ref[hp, qi].astype(jnp.int32), 0, near)
        carry = lax.fori_loop(first, qi, block, carry)
        carry = block(qi, carry, True)
        for r in range(n_rc):
            dq_ref[r * rc:(r + 1) * rc, :] = (carry[r][0] * 0.125).astype(BF16)

        @pl.when(qi == pl.num_programs(1) - 1)
        def _():
            dk_ref[...] = dk_acc[...].astype(BF16)
            dv_ref[...] = dv_acc[...].astype(BF16)

    blk = pl.BlockSpec((tq, LANES), lambda hp, qi: (qi, hp))
    col = pl.BlockSpec((s, LANES), lambda hp, qi: (0, hp))
    o512 = jax.ShapeDtypeStruct((s, D_SB), BF16)
    return _hosted_call(
        body, comm, name=name, grid=(4, nq),
        out_shape=(o512, o512, o512),
        in_specs=[pl.BlockSpec(memory_space=pltpu.SMEM), blk,
                  pl.BlockSpec((s, LANES), lambda hp, qi: (0, 4 + hp)),
                  pl.BlockSpec((s, LANES), lambda hp, qi: (0, 8 + hp)), blk, blk],
        out_specs=(blk, col, col),
        scratch_shapes=[pltpu.VMEM((s, LANES), F32), pltpu.VMEM((s, LANES), F32)],
        args=(nblk, qkv, qkv, qkv, lsum, dya), sem=("parallel", "arbitrary"))


def _bwd_dproj(dmisc, dconv, pc, dq, dk, dv, x, dx2, g, cw, win_full, name, comm=None):
    s = x.shape[0]
    ts = min(ROW_TILE, s)
    blk8 = ts // SUBLANES
    last8 = s // SUBLANES - 1

    def body(dcb_ref, dcz_ref, daz_ref, dconv_ref, nxt_ref, cc_ref, ch_ref, dq_ref, dk_ref, dv_ref,
             x_ref, dx2_ref, g_ref, cw_ref, w_ref, dproj_ref, dx_ref, dg_ref):
        i = pl.program_id(0)

        @pl.when(i == 0)
        def _():
            dg_ref[...] = jnp.zeros_like(dg_ref)

        keep = jnp.where(i == pl.num_programs(0) - 1, 0.0, 1.0)
        dc = dconv_ref[...]
        n0 = nxt_ref[0:1, :] * keep
        n1 = nxt_ref[1:2, :] * keep
        rowi = lax.broadcasted_iota(jnp.int32, dc.shape, 0)
        dc1 = jnp.where(rowi == ts - 1, n0, pltpu.roll(dc, ts - 1, 0))
        dc2 = jnp.where(rowi == ts - 2, n0, jnp.where(rowi == ts - 1, n1, pltpu.roll(dc, ts - 2, 0)))
        du = cw_ref[2:3, :] * dc + cw_ref[1:2, :] * dc1 + cw_ref[0:1, :] * dc2
        dproj_ref[:, 0:512] = dcb_ref[...]
        dproj_ref[:, 512:1024] = (du * ch_ref[...].astype(F32)).astype(BF16)
        dproj_ref[:, 1024:1536] = (du * cc_ref[...].astype(F32)).astype(BF16)
        dproj_ref[:, 1536:2048] = dcz_ref[...]
        dproj_ref[:, 2048:2560] = dq_ref[...]
        dproj_ref[:, 2560:3072] = dk_ref[...]
        dproj_ref[:, 3072:3584] = dv_ref[...]
        dproj_ref[:, 3584:4096] = daz_ref[...]
        dh = lax.dot_general(dproj_ref[...], w_ref[...], NT, preferred_element_type=F32)
        x = x_ref[...]
        r = lax.rsqrt(jnp.mean(x * x, axis=-1, keepdims=True) + EPS)
        xn = x * r
        dg_ref[...] += jnp.sum(dh * xn, axis=0, keepdims=True)
        dxn = dh * g_ref[...]
        dx_ref[...] = dx2_ref[...] + r * (dxn - xn * jnp.mean(dxn * xn, axis=-1, keepdims=True))

    row = lambda width, cb_=0: pl.BlockSpec((ts, width), lambda i: (i, cb_))
    nxt = pl.BlockSpec((SUBLANES, 512), lambda i: (jnp.minimum((i + 1) * blk8, last8), 0))
    vec = lambda width: pl.BlockSpec((1, width), lambda i: (0, 0))
    return _hosted_call(
        body, comm, name=name, grid=(s // ts,),
        out_shape=(jax.ShapeDtypeStruct((s, N_IN), BF16), jax.ShapeDtypeStruct((s, D_MODEL), F32),
                   jax.ShapeDtypeStruct((1, D_MODEL), F32)),
        in_specs=[row(512, 0), row(512, 1), row(512, 2), row(512), nxt, row(512, 1), row(512, 2),
                  row(512), row(512), row(512), row(D_MODEL), row(D_MODEL), vec(D_MODEL),
                  pl.BlockSpec((3, 512), lambda i: (0, 0)),
                  pl.BlockSpec((D_MODEL, N_IN), lambda i: (0, 0))],
        out_specs=(row(N_IN), row(D_MODEL), vec(D_MODEL)),
        args=(dmisc, dmisc, dmisc, dconv, dconv, pc, pc, dq, dk, dv, x, dx2, g, cw, win_full),
        sem=("arbitrary",))


def _atb(a, b, name, a_cols=None, comm=None):
    s, n = b.shape
    m, a_blk = (a.shape[-1], 0) if a_cols is None else a_cols
    ts = min(512, s)
    tn = min(2048, n)
    a_spec = pl.BlockSpec((ts, m), lambda j, i: (i, a_blk))

    def body(a_ref, b_ref, o_ref, acc_ref):
        i = pl.program_id(1)

        @pl.when(i == 0)
        def _():
            acc_ref[...] = jnp.zeros_like(acc_ref)

        acc_ref[...] += lax.dot_general(a_ref[...].astype(BF16), b_ref[...], TN, preferred_element_type=F32)

        @pl.when(i == pl.num_programs(1) - 1)
        def _():
            o_ref[...] = acc_ref[...].astype(BF16)

    (out,), got = _hosted_call(
        body, comm, name=name, grid=(n // tn, s // ts),
        out_shape=(jax.ShapeDtypeStruct((m, n), BF16),),
        in_specs=[a_spec, pl.BlockSpec((ts, tn), lambda j, i: (i, j))],
        out_specs=(pl.BlockSpec((m, tn), lambda j, i: (0, j)),),
        scratch_shapes=[pltpu.VMEM((m, tn), F32)],
        args=(a, b), sem=("parallel", "arbitrary"))
    return out, got


def _adamw_math(w, g, m, v):
    m2 = ADAM_B1 * m + (1.0 - ADAM_B1) * g
    v2 = ADAM_B2 * v + (1.0 - ADAM_B2) * (g * g)
    m_hat = m2 / (1.0 - ADAM_B1 ** ADAM_STEP)
    v_hat = v2 / (1.0 - ADAM_B2 ** ADAM_STEP)
    delta = -ADAM_LR * (m_hat / (jnp.sqrt(v_hat) + ADAM_EPS) + ADAM_WD * w)
    return delta, m2, v2


def _adamw_sum8(pieces, w, m, v, name):
    _, rows, cols = w.shape
    tr = min([rows, 256] + [pc_[0].shape[1] for pc_ in pieces])
    n_tiles = rows // tr
    n_p = len(pieces)
    spans = [(layer, row0 // tr, arr.shape[1] // tr) for arr, layer, row0 in pieces]

    def body(*refs):
        p_refs = refs[:n_p]
        w_ref, m_ref, v_ref, g_ref, d_ref, m2_ref, v2_ref = refs[n_p:]
        l, i = pl.program_id(0), pl.program_id(1)

        def run(p_ref):
            g = p_ref[0].astype(F32)
            for d in range(1, N_DEV):
                g = g + p_ref[d].astype(F32)
            g_ref[...] = g
            d_ref[...], m2_ref[...], v2_ref[...] = _adamw_math(w_ref[...], g, m_ref[...], v_ref[...])

        for p_ref, (layer, t0, nt) in zip(p_refs, spans):
            mine = jnp.logical_and(l == layer, jnp.logical_and(i >= t0, i < t0 + nt))
            pl.when(mine)(lambda p_ref=p_ref: run(p_ref))

    def piece_spec(layer, t0, nt):
        return pl.BlockSpec((N_DEV, tr, cols),
                            lambda l, i: (0, jnp.clip(jnp.where(l == layer, i - t0, jnp.where(l < layer, 0, nt - 1)),
                                                      0, nt - 1), 0))

    tile = pl.BlockSpec((None, tr, cols), lambda l, i: (l, i, 0))
    o = jax.ShapeDtypeStruct((DEPTH, rows, cols), F32)
    return _call(
        body, name=name, grid=(DEPTH, n_tiles),
        out_shape=(o, o, o, o),
        in_specs=[*[piece_spec(*sp) for sp in spans], tile, tile, tile],
        out_specs=(tile, tile, tile, tile),
        compiler_params=_params(("arbitrary", "arbitrary"), VMEM_LIMIT),
    )(*[pc_[0] for pc_ in pieces], w, m, v)


def _adamw_plain(g, w, m, v, name):
    rows, cols = g.shape

    def body(g_ref, w_ref, m_ref, v_ref, d_ref, m2_ref, v2_ref):
        d_ref[...], m2_ref[...], v2_ref[...] = _adamw_math(w_ref[...], g_ref[...], m_ref[...], v_ref[...])

    full = pl.BlockSpec((rows, cols), lambda: (0, 0))
    o = jax.ShapeDtypeStruct((rows, cols), F32)
    return _call(body, name=name, out_shape=(o, o, o), in_specs=[full] * 4, out_specs=(full,) * 3)(g, w, m, v)


def _small_update(parts, params):
    n = len(params)
    rows = [w.shape[0] for w, _, _ in params]
    offs = [sum(rows[:k]) for k in range(n)]

    def body(*refs):
        p_ref, wmv, outs = refs[0], refs[1:1 + 3 * n], refs[1 + 3 * n:]
        g = p_ref[0]
        for d in range(1, N_DEV):
            g = g + p_ref[d]
        outs[0][...] = g
        for k in range(n):
            gk = g[offs[k]:offs[k] + rows[k]]
            w_ref, m_ref, v_ref = wmv[3 * k:3 * k + 3]
            g_ref, d_ref, m2_ref, v2_ref = outs[1 + 4 * k:5 + 4 * k]
            g_ref[...] = gk
            d_ref[...], m2_ref[...], v2_ref[...] = _adamw_math(w_ref[...], gk, m_ref[...], v_ref[...])

    whole = lambda shape: pl.BlockSpec(shape, lambda: (0,) * len(shape))
    flat_in = [a for wmv in params for a in wmv]
    out_shape = [jax.ShapeDtypeStruct((SMALL_ROWS, LANES), F32)]
    for r in rows:
        out_shape += [jax.ShapeDtypeStruct((r, LANES), F32)] * 4
    outs = _call(
        body, name="adamw_small",
        out_shape=tuple(out_shape),
        in_specs=[whole(parts.shape)] + [whole(a.shape) for a in flat_in],
        out_specs=tuple(whole(o.shape) for o in out_shape),
    )(parts, *flat_in)
    return outs[0], [outs[1 + 4 * k:5 + 4 * k] for k in range(n)]


def kernel(x, p, norm_g, w_in, conv_w, conv_b, branch_g, w_out, ple_norm_g, w_pg, b_pg, w_pe, final_g, loss_target, m_norm_g, m_w_in, m_conv_w, m_conv_b, m_branch_g, m_w_out, m_ple_norm_g, m_w_pg, m_b_pg, m_w_pe, m_final_g, v_norm_g, v_w_in, v_conv_w, v_conv_b, v_branch_g, v_w_out, v_ple_norm_g, v_w_pg, v_b_pg, v_w_pe, v_final_g):
    s = x.shape[1]
    x0 = x.reshape(s, D_MODEL)
    target = loss_target.reshape(s, D_MODEL)
    me_blk = _my_block()

    win_s, wout_s, wpg_s, wpe_s = _cast_bf16(
        [w_in.reshape(DEPTH * D_MODEL, 512), w_out.reshape(DEPTH * 128, D_MODEL),
         w_pg.reshape(DEPTH * 128, D_MODEL), w_pe.reshape(DEPTH * PLE_DIM, 128)], "cast_weights")
    win_s, wout_s = win_s.reshape(DEPTH, D_MODEL, 512), wout_s.reshape(DEPTH, 128, D_MODEL)
    wpg_s, wpe_s = wpg_s.reshape(DEPTH, 128, D_MODEL), wpe_s.reshape(DEPTH, PLE_DIM, 128)
    cw_s = jnp.zeros((SUBLANES, LANES), F32).at[:DEPTH * 3, :HEAD_DIM].set(conv_w.reshape(DEPTH * 3, HEAD_DIM))
    bf = lambda r_, c_: jax.ShapeDtypeStruct((r_, c_), BF16)
    w_items = lambda l: [(wout_s[l], bf(D_MODEL, D_MODEL), "rows128"), (wpg_s[l], bf(D_MODEL, D_MODEL), "rows128"),
                         (wpe_s[l], bf(PLE_DIM, D_MODEL), "cols128")]
    win_f = [None] * DEPTH
    win_f[0], cw_all = _comm_call(_gather_comm([
        (win_s[0], bf(D_MODEL, N_IN), "cols512"),
        (cw_s, jax.ShapeDtypeStruct((N_DEV, SUBLANES, LANES), F32), "slot")]), "gather_w_in_0")
    cw_full = jnp.transpose(cw_all[:, :DEPTH * 3, :HEAD_DIM].reshape(N_DEV, DEPTH, 3, HEAD_DIM), (1, 2, 0, 3))
    cw_full = cw_full.reshape(DEPTH, 3, D_CONV)
    gather_rest_0 = _gather_comm(w_items(0))
    gather_win_1 = _gather_comm([(win_s[1], bf(D_MODEL, N_IN), "cols512")])
    gather_rest_1 = _gather_comm(w_items(1))

    vec = lambda a, l: a[l][None, :]

    saved = []
    xl = x0
    wout_f, wpg_f, wpe_f = [None] * DEPTH, [None] * DEPTH, [None] * DEPTH
    for l in range(DEPTH):
        (h, pc, qkv, az), got = _fwd_in(xl, vec(norm_g, l), win_f[l], f"fwd_in_{l}",
                                        comm=gather_rest_0 if l == 0 else None)
        if l == 0:
            wout_f[0], wpg_f[0], wpe_f[0] = got
        (ya, lsum, nblk), got = _attn_fwd(qkv, f"attn_fwd_{l}", comm=gather_win_1 if l == 0 else None)
        if l == 0:
            (win_f[1],) = got
        (x2, x3, gated, h2, gate, e), got = _fwd_mid(
            xl, pc, az, ya, p, l, cw_full[l], vec(conv_b, l), vec(branch_g, l), wout_f[l],
            vec(ple_norm_g, l), wpg_f[l], vec(b_pg, l), wpe_f[l], f"fwd_mid_{l}",
            comm=gather_rest_1 if l == 0 else None)
        if l == 0:
            wout_f[1], wpg_f[1], wpe_f[1] = got
        saved.append(dict(x=xl, h=h, pc=pc, qkv=qkv, az=az, ya=ya, lsum=lsum, nblk=nblk, x2=x2, gated=gated, h2=h2,
                          gate=gate, e=e))
        xl = x3

    dx, loss_acc, d_final_g = _loss_head(xl, target, final_g[None, :])

    dwin, dwout, dwpg, dwpe = [None] * DEPTH, [None] * DEPTH, [None] * DEPTH, [None] * DEPTH
    small = dict(norm_g=[None] * DEPTH, conv_b=[None] * DEPTH, branch_g=[None] * DEPTH,
                 ple_norm_g=[None] * DEPTH, b_pg=[None] * DEPTH, conv_w=[None] * DEPTH)
    slot = lambda r_, c_: jax.ShapeDtypeStruct((r_, c_), BF16)
    half = D_MODEL // 2
    r_in1, r_out, r_pg, r_pe = None, [None] * DEPTH, [None] * DEPTH, [None] * DEPTH

    def rest_items(l):
        return [(dwout[l], slot(128, D_MODEL), "rows128"), (dwpg[l], slot(128, D_MODEL), "rows128"),
                (dwpe[l], slot(PLE_DIM, 128), "cols128")]

    for l in reversed(range(DEPTH)):
        sv = saved[l]
        ride = _exchange_comm(rest_items(1)) if l == 0 else None
        (dx2, dya, dmisc, dconv, dwout[l], dwpg[l], dwpe[l], d_bpg, d_pg, d_bg, d_cbias, d_cw), got = _bwd_mid(
            dx, sv["x2"], sv["gate"], sv["e"], sv["pc"], sv["az"], sv["ya"], sv["gated"], sv["h2"], p, l,
            cw_full[l], vec(conv_b, l), vec(branch_g, l), vec(ple_norm_g, l), wpg_f[l], wout_f[l], f"bwd_mid_{l}",
            comm=ride)
        if l == 0:
            r_out[1], r_pg[1], r_pe[1] = got
        ride = _exchange_comm([(dwin[1], slot(D_MODEL, 512), "cols512")] + rest_items(0)) if l == 0 else None
        (dq, dk, dv), got = _attn_bwd(sv["qkv"], sv["lsum"], sv["nblk"], dya, f"attn_bwd_{l}", comm=ride)
        if l == 0:
            r_in1, r_out[0], r_pg[0], r_pe[0] = got
        (dproj, dx, d_ng), _ = _bwd_dproj(dmisc, dconv, sv["pc"], dq, dk, dv, sv["x"], dx2, vec(norm_g, l),
                                          cw_full[l], win_f[l], f"bwd_dproj_{l}")
        if l == 1:
            dwin[1], _ = _atb(sv["h"], dproj, "dw_in_1")
        else:
            dwin_top, _ = _atb(sv["h"], dproj, "dw_in_0_top", a_cols=(half, 0))
            dwin_bot, (r_in0_top,) = _atb(sv["h"], dproj, "dw_in_0_bottom", a_cols=(half, 1),
                                          comm=_exchange_comm([(dwin_top, slot(half, 512), "cols512")]))
        small["norm_g"][l], small["conv_b"][l], small["branch_g"][l] = d_ng, d_cbias, d_bg
        small["ple_norm_g"][l], small["b_pg"][l], small["conv_w"][l] = d_pg, d_bpg, d_cw[:3]
    grad_x = dx.reshape(1, s, D_MODEL)

    flat = lambda parts: jnp.concatenate([a.reshape(-1) for a in parts])
    small_vec = jnp.concatenate([
        flat(small["norm_g"]), flat(small["conv_b"]), flat(small["branch_g"]), flat(small["ple_norm_g"]),
        flat(small["b_pg"]), d_final_g.reshape(-1), flat(small["conv_w"]), loss_acc.reshape(-1),
        jnp.zeros(((SMALL_ROWS - SMALL_GRAD_ROWS - 1) * LANES,), F32)]).reshape(SMALL_ROWS, LANES)
    r_in0_bot, r_small = _comm_call(_exchange_comm([
        (dwin_bot, slot(half, 512), "cols512"),
        (small_vec, jax.ShapeDtypeStruct((SMALL_ROWS, LANES), F32), "slot")]), "exchange_last")

    per_layer = lambda r: [(r[0], 0, 0), (r[1], 1, 0)]
    g_win, d_win, m_win, v_win = _adamw_sum8([(r_in0_top, 0, 0), (r_in0_bot, 0, half), (r_in1, 1, 0)],
                                             w_in, m_w_in, v_w_in, "adamw_w_in")
    g_wout, d_wout, m_wout, v_wout = _adamw_sum8(per_layer(r_out), w_out, m_w_out, v_w_out, "adamw_w_out")
    g_wpg, d_wpg, m_wpg, v_wpg = _adamw_sum8(per_layer(r_pg), w_pg, m_w_pg, v_w_pg, "adamw_w_pg")
    g_wpe, d_wpe, m_wpe, v_wpe = _adamw_sum8(per_layer(r_pe), w_pe, m_w_pe, v_w_pe, "adamw_w_pe")

    repl = [(norm_g, m_norm_g, v_norm_g), (conv_b, m_conv_b, v_conv_b), (branch_g, m_branch_g, v_branch_g),
            (ple_norm_g, m_ple_norm_g, v_ple_norm_g), (b_pg, m_b_pg, v_b_pg), (final_g, m_final_g, v_final_g)]
    g_small, upd = _small_update(r_small, [tuple(a.reshape(-1, LANES) for a in t) for t in repl])
    g_r, d_r, m_r, v_r = [[upd[k][j].reshape(repl[k][0].shape) for k in range(len(repl))] for j in range(4)]

    loss = g_small[SMALL_GRAD_ROWS, 0]
    g_cw_full = g_small[SMALL_REPL_ROWS:SMALL_GRAD_ROWS].reshape(DEPTH, 3, D_CONV)
    g_cw = lax.dynamic_slice(g_cw_full, (0, 0, me_blk * HEAD_DIM), (DEPTH, 3, HEAD_DIM))
    pad_cw = lambda a: jnp.zeros((SUBLANES, LANES), F32).at[:3].set(a.reshape(3, LANES))
    v_cw_pad = jnp.ones((SUBLANES, LANES), F32).at[:3].set(v_conv_w.reshape(3, LANES))
    d_cw, m_cw, v_cw = _adamw_plain(pad_cw(g_cw), pad_cw(conv_w), pad_cw(m_conv_w), v_cw_pad, "adamw_conv_w")
    un_cw = lambda a: a[:3].reshape(DEPTH, 3, HEAD_DIM)

    def ordered(r, win_, cw_, wout_, wpg_, wpe_):
        return [r[0], win_, cw_, r[1], r[2], wout_, r[3], wpg_, r[4], wpe_, r[5]]

    grads = ordered(g_r, g_win, g_cw, g_wout, g_wpg, g_wpe)
    deltas = ordered(d_r, d_win, un_cw(d_cw), d_wout, d_wpg, d_wpe)
    new_m = ordered(m_r, m_win, un_cw(m_cw), m_wout, m_wpg, m_wpe)
    new_v = ordered(v_r, v_win, un_cw(v_cw), v_wout, v_wpg, v_wpe)
    return (loss, grad_x, *grads, *deltas, *new_m, *new_v)
```

```python
import jax
import jax.numpy as jnp
from jax import lax
from jax.experimental import pallas as pl
from jax.experimental.pallas import tpu as pltpu

F32 = jnp.float32
BF16 = jnp.bfloat16

D_MODEL = 1024
D_CONV = 512
D_SB = 512
N_IN = 4096
HEAD_DIM = 64
PLE_DIM = 256
DEPTH = 2
EPS = 1e-6
ADAM_LR = 0.001
ADAM_B1 = 0.9
ADAM_B2 = 0.999
ADAM_EPS = 1e-08
ADAM_WD = 0.01
ADAM_STEP = 10

LANES = 128
SUBLANES = 8
VMEM_BYTES_V7X = 64 * 1024 * 1024
VMEM_LIMIT = VMEM_BYTES_V7X - 8 * 1024 * 1024

N_DEV = 8
ROW_TILE = 256
ATTN_TILE = 256
SMALL_GRAD_ROWS = 104
SMALL_REPL_ROWS = 80
SMALL_ROWS = 112

NT = (((1,), (1,)), ((), ()))
TN = (((0,), (0,)), ((), ()))


def _call(body, **kw):
    return pl.pallas_call(body, **kw)


def _params(sem=None, vmem=None):
    return pltpu.CompilerParams(dimension_semantics=sem, vmem_limit_bytes=vmem)


def _sigmoid(z):
    return 0.5 * jnp.tanh(0.5 * z) + 0.5


def _group_bcast_sum(a, lo):
    s_lo = jnp.sum(jnp.where(lo, a, 0.0), axis=-1, keepdims=True)
    s_hi = jnp.sum(jnp.where(lo, 0.0, a), axis=-1, keepdims=True)
    return jnp.where(lo, s_lo, s_hi)


def _layer_rows(layer, rows, width):
    return pl.BlockSpec((None, rows, width), lambda i: (layer, 0, 0))


def _my_block():
    return 4 * lax.axis_index("x") + 2 * lax.axis_index("y") + lax.axis_index("c")


def _cast_bf16(arrays, name):
    n = len(arrays)

    def body(*refs):
        for a_ref, o_ref in zip(refs[:n], refs[n:]):
            o_ref[...] = a_ref[...].astype(BF16)

    whole = lambda a: pl.BlockSpec(a.shape, lambda: (0, 0))
    return _call(
        body, name=name,
        out_shape=tuple(jax.ShapeDtypeStruct(a.shape, BF16) for a in arrays),
        in_specs=[whole(a) for a in arrays], out_specs=tuple(whole(a) for a in arrays),
        compiler_params=_params(None, VMEM_LIMIT),
    )(*arrays)


class _Comm:
    def __init__(self, inputs, out_shapes, scratch, begin, middle, finish):
        self.inputs, self.out_shapes, self.scratch = list(inputs), list(out_shapes), list(scratch)
        self.begin, self.middle, self.finish = begin, middle, finish


def _slab(kind, ref, blk):
    if kind == "cols512":
        return ref.at[:, pl.ds(blk * 512, 512)]
    if kind == "rows128":
        return ref.at[pl.ds(blk * 128, 128), :]
    if kind == "cols128":
        return ref.at[:, pl.ds(blk * 128, 128)]
    return ref.at[blk]


def _gather_comm(items):
    n_t = len(items)
    kinds = [it[2] for it in items]

    def ctx(ins, outs, sems):
        send_sems, recv_sems, local_sems = sems
        x, y, c = lax.axis_index("x"), lax.axis_index("y"), lax.axis_index("c")
        me, sibling = (x, y, c), (x, y, 1 - c)
        chips = [(1 - x, y), (x, 1 - y), (1 - x, 1 - y)]

        def place(t, dev):
            return _slab(kinds[t], outs[t], 4 * dev[0] + 2 * dev[1] + dev[2])

        def copy(t, k, block, to, own=False):
            return pltpu.make_async_remote_copy(
                src_ref=ins[t] if own else place(t, block), dst_ref=place(t, block),
                send_sem=send_sems.at[t, k], recv_sem=recv_sems.at[t, k],
                device_id=to, device_id_type=pl.DeviceIdType.MESH)

        mine = [pltpu.make_async_copy(ins[t], place(t, me), local_sems.at[t]) for t in range(n_t)]
        first = []
        for t in range(n_t):
            first.append(copy(t, 0, me, sibling, own=True))
            first += [copy(t, 1 + j, me, (*chip, c), own=True) for j, chip in enumerate(chips)]
        passed = [copy(t, 4 + j, (*chip, c), sibling) for j, chip in enumerate(chips) for t in range(n_t)]
        landed = [copy(t, 1 + j, (*chip, c), me) for j, chip in enumerate(chips) for t in range(n_t)]
        from_sibling = []
        for t in range(n_t):
            from_sibling.append(copy(t, 0, sibling, me))
            from_sibling += [copy(t, 4 + j, (*chip, 1 - c), me) for j, chip in enumerate(chips)]
        return mine, first, landed, passed, from_sibling

    def begin(ins, outs, sems):
        mine, first, _, _, _ = ctx(ins, outs, sems)
        for cp in mine + first:
            cp.start()

    def middle(ins, outs, sems):
        _, _, landed, passed, _ = ctx(ins, outs, sems)
        for got, fwd in zip(landed, passed):
            got.wait_recv()
            fwd.start()

    def finish(ins, outs, sems):
        mine, first, _, passed, from_sibling = ctx(ins, outs, sems)
        for cp in from_sibling:
            cp.wait_recv()
        for cp in first + passed:
            cp.wait_send()
        for cp in mine:
            cp.wait()

    scratch = [pltpu.SemaphoreType.DMA((n_t, 7)), pltpu.SemaphoreType.DMA((n_t, 7)), pltpu.SemaphoreType.DMA((n_t,))]
    return _Comm([it[0] for it in items], [it[1] for it in items], scratch, begin, middle, finish)


def _exchange_comm(items):
    n_t = len(items)
    kinds = [it[2] for it in items]

    def ctx(ins, outs, sems):
        send_sems, recv_sems, local_sems = sems
        x, y, c = lax.axis_index("x"), lax.axis_index("y"), lax.axis_index("c")
        me_blk = 4 * x + 2 * y + c

        def src(t, blk):
            return ins[t] if kinds[t] == "slot" else _slab(kinds[t], ins[t], blk)

        local = [pltpu.make_async_copy(src(t, me_blk), outs[t].at[me_blk], local_sems.at[t]) for t in range(n_t)]
        remote = []
        for k in range(1, N_DEV):
            px = 1 - x if k & 4 else x
            py = 1 - y if k & 2 else y
            pc_ = 1 - c if k & 1 else c
            for t in range(n_t):
                remote.append(pltpu.make_async_remote_copy(
                    src_ref=src(t, 4 * px + 2 * py + pc_), dst_ref=outs[t].at[me_blk],
                    send_sem=send_sems.at[k - 1, t], recv_sem=recv_sems.at[k - 1, t],
                    device_id=(px, py, pc_), device_id_type=pl.DeviceIdType.MESH))
        return local, remote

    def begin(ins, outs, sems):
        local, remote = ctx(ins, outs, sems)
        for cp in local + remote:
            cp.start()

    def finish(ins, outs, sems):
        local, remote = ctx(ins, outs, sems)
        for cp in remote:
            cp.wait_recv()
        for cp in remote:
            cp.wait_send()
        for cp in local:
            cp.wait()

    scratch = [pltpu.SemaphoreType.DMA((N_DEV - 1, n_t)), pltpu.SemaphoreType.DMA((N_DEV - 1, n_t)),
               pltpu.SemaphoreType.DMA((n_t,))]
    out_shapes = [jax.ShapeDtypeStruct((N_DEV, *it[1].shape), it[1].dtype) for it in items]
    return _Comm([it[0] for it in items], out_shapes, scratch, begin, None, finish)


def _comm_call(comm, name):
    n_in, n_out = len(comm.inputs), len(comm.out_shapes)

    def body(*refs):
        ins, outs, sems = refs[:n_in], refs[n_in:n_in + n_out], refs[n_in + n_out:]
        comm.begin(ins, outs, sems)
        if comm.middle is not None:
            comm.middle(ins, outs, sems)
        comm.finish(ins, outs, sems)

    any_spec = pl.BlockSpec(memory_space=pl.ANY)
    return _call(body, name=name, out_shape=tuple(comm.out_shapes), in_specs=[any_spec] * n_in,
                 out_specs=[any_spec] * n_out, scratch_shapes=comm.scratch)(*comm.inputs)


def _hosted(body, n_in, n_out, comm, first, last, middle):
    if comm is None:
        return lambda *refs: body(*refs)
    n_ci, n_co, n_cs = len(comm.inputs), len(comm.out_shapes), len(comm.scratch)

    def wrapped(*refs):
        ins, cin = refs[:n_in], refs[n_in:n_in + n_ci]
        o0 = n_in + n_ci
        outs, cout = refs[o0:o0 + n_out], refs[o0 + n_out:o0 + n_out + n_co]
        scr, csem = refs[o0 + n_out + n_co:len(refs) - n_cs], refs[len(refs) - n_cs:]
        pl.when(first())(lambda: comm.begin(cin, cout, csem))
        body(*ins, *outs, *scr)
        if comm.middle is not None:
            pl.when(middle())(lambda: comm.middle(cin, cout, csem))
        pl.when(last())(lambda: comm.finish(cin, cout, csem))

    return wrapped


def _hosted_call(body, comm, *, name, grid, out_shape, in_specs, out_specs, args, scratch_shapes=(), sem=None):
    nd = len(grid)
    first, last, middle = _at_first(nd), _at_last(nd), _at_middle(nd)
    if comm is not None:
        sem = ("arbitrary",) * nd
    n_in, n_out = len(in_specs), len(out_shape)
    any_spec = pl.BlockSpec(memory_space=pl.ANY)
    c_in = [] if comm is None else comm.inputs
    c_out = [] if comm is None else comm.out_shapes
    c_scr = [] if comm is None else comm.scratch
    outs = _call(
        _hosted(body, n_in, n_out, comm, first, last, middle), name=name, grid=grid,
        out_shape=(*out_shape, *c_out),
        in_specs=[*in_specs, *[any_spec] * len(c_in)],
        out_specs=(*out_specs, *[any_spec] * len(c_out)),
        scratch_shapes=[*scratch_shapes, *c_scr],
        compiler_params=_params(sem, VMEM_LIMIT),
    )(*args, *c_in)
    return outs[:n_out], outs[n_out:]


def _grid_step(ndim):
    i, n = pl.program_id(0), pl.num_programs(0)
    for d in range(1, ndim):
        i, n = i * pl.num_programs(d) + pl.program_id(d), n * pl.num_programs(d)
    return i, n


def _at_first(ndim):
    return lambda: _grid_step(ndim)[0] == 0


def _at_last(ndim):
    def pred():
        i, n = _grid_step(ndim)
        return i == n - 1
    return pred


def _at_middle(ndim):
    def pred():
        i, n = _grid_step(ndim)
        return i == (3 * n) // 4
    return pred


def _fwd_in(x, g, layer, w_full, name, comm=None):
    s = x.shape[0]
    ts = min(ROW_TILE, s)

    def body(x_ref, g_ref, w_ref, h_ref, pc_ref, qkv_ref, az_ref):
        xf = x_ref[...]
        r = lax.rsqrt(jnp.mean(xf * xf, axis=-1, keepdims=True) + EPS)
        h = (xf * r * g_ref[...]).astype(BF16)
        h_ref[...] = h
        pc_ref[...] = jnp.dot(h, w_ref[:, 0:2048], preferred_element_type=F32).astype(BF16)
        q = jnp.dot(h, w_ref[:, 2048:2560], preferred_element_type=F32)
        qkv_ref[:, 0:512] = (q * 0.125).astype(BF16)
        qkv_ref[:, 512:1536] = jnp.dot(h, w_ref[:, 2560:3584], preferred_element_type=F32).astype(BF16)
        az_ref[...] = jnp.dot(h, w_ref[:, 3584:4096], preferred_element_type=F32).astype(BF16)

    row = lambda width: pl.BlockSpec((ts, width), lambda i: (i, 0))
    return _hosted_call(
        body, comm, name=name, grid=(s // ts,),
        out_shape=(jax.ShapeDtypeStruct((s, D_MODEL), BF16), jax.ShapeDtypeStruct((s, 2048), BF16),
                   jax.ShapeDtypeStruct((s, 1536), BF16), jax.ShapeDtypeStruct((s, 512), BF16)),
        in_specs=[row(D_MODEL), _layer_rows(layer, 1, D_MODEL),
                  pl.BlockSpec((D_MODEL, N_IN), lambda i: (0, 0))],
        out_specs=(row(D_MODEL), row(2048), row(1536), row(512)),
        args=(x, g, w_full), sem=("parallel",))


ATTN_ROWS = 128
ATTN_DONE = 104.0


def _attn_pieces(tq, rc):
    lane = lax.broadcasted_iota(jnp.int32, (1, LANES), 1)
    lo = lane < HEAD_DIM
    row = lax.broadcasted_iota(jnp.int32, (tq, tq), 0)
    col = lax.broadcasted_iota(jnp.int32, (tq, tq), 1)
    tri_gt = jnp.where(row > col, 1.0, 0.0).astype(BF16)
    tri_le = jnp.where(row <= col, 1.0, 0.0).astype(BF16)
    rrow = lax.broadcasted_iota(jnp.int32, (rc, tq), 0)
    rcol = lax.broadcasted_iota(jnp.int32, (rc, tq), 1)
    causal = [rcol < rrow + r * rc for r in range(tq // rc)]
    return lo, causal, tri_gt, tri_le


def _split_heads(a, lo):
    z = jnp.zeros_like(a)
    return (jnp.where(lo, a, z), jnp.where(lo, z, a))


def _softplus(z, causal, diag):
    neg_abs = lax.bitcast_convert_type(lax.bitcast_convert_type(z, jnp.uint32) | jnp.uint32(0x80000000), F32)
    sp = jnp.maximum(z, 0.0) + jnp.log(1.0 + jnp.exp(neg_abs))
    if diag:
        sp = jnp.where(causal, sp, 0.0)
    return sp


def _attn_fwd(qkv, name, comm=None):
    s = qkv.shape[0]
    tq = min(ATTN_TILE, s)
    nq = s // tq
    rc = min(ATTN_ROWS, tq)
    n_rc = tq // rc
    chains = [(r, hh) for r in range(n_rc) for hh in range(2)]

    def body(q_ref, k_ref, v_ref, o_ref, lsum_ref, nblk_ref):
        hp, qi = pl.program_id(0), pl.program_id(1)
        lo, causal, tri_gt, _ = _attn_pieces(tq, rc)
        qh = _split_heads(q_ref[...], lo)
        qc = {(r, hh): qh[hh][r * rc:(r + 1) * rc] for r, hh in chains}

        mm = lambda a_, b_: jnp.dot(a_.astype(BF16), b_, preferred_element_type=F32)
        rowsum = lambda a_: jnp.sum(a_, axis=-1, keepdims=True)

        def block(kb, carry):
            start = pl.multiple_of(kb * tq, tq)
            k = k_ref[pl.ds(start, tq), :]
            vh = _split_heads(v_ref[pl.ds(start, tq), :], lo)
            z = {ch: lax.dot_general(qc[ch], k, NT, preferred_element_type=F32) for ch in chains}
            sp = {ch: _softplus(z[ch], None, False) for ch in chains}
            later = {ch: mm(sp[ch], tri_gt) for ch in chains}
            a = {ch: jnp.exp((z[ch] - sp[ch]) - (carry[ch[0]][1 + ch[1]] + later[ch])) for ch in chains}
            pv = {ch: mm(a[ch], vh[ch[1]]) for ch in chains}
            return tuple((carry[r][0] + pv[(r, 0)] + pv[(r, 1)],
                          carry[r][1] + rowsum(sp[(r, 0)]), carry[r][2] + rowsum(sp[(r, 1)])) for r in range(n_rc))

        def first_two(prev_ok):
            d0 = pl.multiple_of(qi * tq, tq)
            p0 = pl.multiple_of(jnp.maximum(qi - 1, 0) * tq, tq)
            k_d, k_p = k_ref[pl.ds(d0, tq), :], k_ref[pl.ds(p0, tq), :]
            vh_d = _split_heads(v_ref[pl.ds(d0, tq), :], lo)
            vh_p = _split_heads(v_ref[pl.ds(p0, tq), :], lo)
            z_d = {ch: lax.dot_general(qc[ch], k_d, NT, preferred_element_type=F32) for ch in chains}
            z_p = {ch: lax.dot_general(qc[ch], k_p, NT, preferred_element_type=F32) for ch in chains}
            sp_d = {ch: _softplus(z_d[ch], causal[ch[0]], True) for ch in chains}
            sp_raw = {ch: _softplus(z_p[ch], None, False) for ch in chains}
            sp_p = {ch: jnp.where(prev_ok, sp_raw[ch], 0.0) for ch in chains}
            later_d = {ch: mm(sp_d[ch], tri_gt) for ch in chains}
            later_p = {ch: mm(sp_p[ch], tri_gt) for ch in chains}
            c_d = {ch: rowsum(sp_d[ch]) for ch in chains}
            a_d = {ch: jnp.where(causal[ch[0]], jnp.exp((z_d[ch] - sp_d[ch]) - later_d[ch]), 0.0) for ch in chains}
            a_p = {ch: jnp.where(prev_ok, jnp.exp((z_p[ch] - sp_raw[ch]) - (c_d[ch] + later_p[ch])), 0.0)
                   for ch in chains}
            pv = {ch: mm(a_d[ch], vh_d[ch[1]]) + mm(a_p[ch], vh_p[ch[1]]) for ch in chains}
            return tuple((pv[(r, 0)] + pv[(r, 1)],
                          c_d[(r, 0)] + rowsum(sp_p[(r, 0)]), c_d[(r, 1)] + rowsum(sp_p[(r, 1)]))
                         for r in range(n_rc))

        def least(carry):
            m = jnp.minimum(carry[0][1], carry[0][2])
            for r in range(1, n_rc):
                m = jnp.minimum(m, jnp.minimum(carry[r][1], carry[r][2]))
            return jnp.min(m)

        carry = first_two(qi > 0)

        def go_on(st):
            return jnp.logical_and(st[0] < qi - 1, st[1] < ATTN_DONE)

        def step(st):
            new = block(qi - 2 - st[0], st[2])
            return st[0] + 1, least(new), new

        walked, _, carry = lax.while_loop(go_on, step, (jnp.int32(0), least(carry), carry))
        for r in range(n_rc):
            o_ref[r * rc:(r + 1) * rc, :] = carry[r][0].astype(BF16)
            lsum_ref[r * rc:(r + 1) * rc, :] = jnp.where(lo, carry[r][1], carry[r][2])
        nblk_ref[hp, qi] = walked.astype(F32)

    blk = pl.BlockSpec((tq, LANES), lambda hp, qi: (qi, hp))
    o512 = jax.ShapeDtypeStruct((s, D_SB), F32)
    return _hosted_call(
        body, comm, name=name, grid=(4, nq),
        out_shape=(jax.ShapeDtypeStruct((s, D_SB), BF16), o512, jax.ShapeDtypeStruct((4, nq), F32)),
        in_specs=[blk, pl.BlockSpec((s, LANES), lambda hp, qi: (0, 4 + hp)),
                  pl.BlockSpec((s, LANES), lambda hp, qi: (0, 8 + hp))],
        out_specs=(blk, blk, pl.BlockSpec(memory_space=pltpu.SMEM)),
        args=(qkv, qkv, qkv), sem=("arbitrary", "arbitrary"))


HALO = 16


def _conv_taps(cc_ref, ch_ref, ccp_ref, chp_ref, halo_ref, first):
    u = cc_ref[...].astype(F32) * ch_ref[...].astype(F32)
    halo_ref[...] = ccp_ref[...].astype(F32) * chp_ref[...].astype(F32) * jnp.where(first, 0.0, 1.0)
    p6 = halo_ref[HALO - 2:HALO - 1, :]
    p7 = halo_ref[HALO - 1:HALO, :]
    rowi = lax.broadcasted_iota(jnp.int32, u.shape, 0)
    u1 = jnp.where(rowi == 0, p7, pltpu.roll(u, 1, 0))
    u2 = jnp.where(rowi == 0, p6, jnp.where(rowi == 1, p7, pltpu.roll(u, 2, 0)))
    return u, u1, u2


def _fwd_mid(x, pc, az, ya, p4, layer, cw, cb, bg, wout_full, pg, wpg_full, bpg, wpe_full, name, comm=None,
             head=None):
    s = x.shape[0]
    ts = min(ROW_TILE, s)
    blk_h = ts // HALO

    n_in = 18 + (2 if head else 0)

    def body(*refs):
        (x_ref, cb_ref_, cc_ref, ch_ref, cz_ref, ccp_ref, chp_ref, az_ref, ya_ref, p_ref,
         cw_ref, cbias_ref, bg_ref, wout_ref, pg_ref, wpg_ref, bpg_ref, wpe_ref) = refs[:18]
        x2_ref, x3_ref, gated_ref, h2_ref, gate_ref, e_ref = refs[n_in:n_in + 6]
        halo_ref = refs[-1]
        i = pl.program_id(0)
        lane = lax.broadcasted_iota(jnp.int32, (1, LANES), 1)
        lo = lane < HEAD_DIM
        u, u1, u2 = _conv_taps(cc_ref, ch_ref, ccp_ref, chp_ref, halo_ref, i == 0)
        conv = cbias_ref[...] + cw_ref[0:1, :] * u2 + cw_ref[1:2, :] * u1 + cw_ref[2:3, :] * u
        yc = cb_ref_[...].astype(F32) * conv
        for sl in range(8):
            cols = slice(LANES * (sl % 4), LANES * (sl % 4 + 1))
            y = yc[:, cols] if sl < 4 else ya_ref[:, cols].astype(F32)
            zc = (cz_ref[:, cols] if sl < 4 else az_ref[:, cols]).astype(F32)
            rg = lax.rsqrt(_group_bcast_sum(y * y, lo) * (1.0 / HEAD_DIM) + EPS)
            yn = y * rg * bg_ref[:, LANES * sl:LANES * (sl + 1)]
            gated_ref[:, LANES * sl:LANES * (sl + 1)] = (yn * (zc * _sigmoid(zc))).astype(BF16)
        x2 = x_ref[...] + jnp.dot(gated_ref[...], wout_ref[...], preferred_element_type=F32)
        x2_ref[...] = x2
        r2 = lax.rsqrt(jnp.mean(x2 * x2, axis=-1, keepdims=True) + EPS)
        h2 = (x2 * r2 * pg_ref[...]).astype(BF16)
        h2_ref[...] = h2
        gate = _sigmoid(jnp.dot(h2, wpg_ref[...], preferred_element_type=F32) + bpg_ref[...])
        gate_ref[...] = gate.astype(BF16)
        e = jnp.dot(p_ref[...].astype(BF16), wpe_ref[...], preferred_element_type=F32)
        e_ref[...] = e.astype(BF16)
        x3 = x2 + gate * e
        if not head:
            x3_ref[...] = x3
            return
        t_ref, fg_ref = refs[18:20]
        loss_ref, dfg_ref = refs[n_in + 6:n_in + 8]
        dx, loss, dfg = _loss_math(x3, t_ref[...], fg_ref[...])

        @pl.when(i == 0)
        def _():
            loss_ref[...] = jnp.zeros_like(loss_ref)
            dfg_ref[...] = jnp.zeros_like(dfg_ref)

        x3_ref[...] = dx
        loss_ref[...] += loss
        dfg_ref[...] += dfg

    row = lambda width, cb_=0: pl.BlockSpec((ts, width), lambda i: (i, cb_))
    prev = lambda cb_: pl.BlockSpec((HALO, 512), lambda i: (jnp.maximum(i * blk_h - 1, 0), cb_))
    vec = lambda width: pl.BlockSpec((1, width), lambda i: (0, 0))
    lvec = lambda width: _layer_rows(layer, 1, width)
    wspec = lambda r_, c_: pl.BlockSpec((r_, c_), lambda i: (0, 0))
    f32o = jax.ShapeDtypeStruct((s, D_MODEL), F32)
    bfo = jax.ShapeDtypeStruct((s, D_MODEL), BF16)
    head_in = [row(D_MODEL), vec(D_MODEL)] if head else []
    head_out = [jax.ShapeDtypeStruct((1, LANES), F32), jax.ShapeDtypeStruct((1, D_MODEL), F32)] if head else []
    return _hosted_call(
        body, comm, name=name, grid=(s // ts,),
        out_shape=(f32o, f32o, bfo, bfo, bfo, bfo, *head_out),
        scratch_shapes=[pltpu.VMEM((HALO, 512), F32)],
        in_specs=[row(D_MODEL), row(512, 0), row(512, 1), row(512, 2), row(512, 3), prev(1), prev(2),
                  row(512), row(512),
                  pl.BlockSpec((None, None, ts, PLE_DIM), lambda i: (layer, 0, i, 0)),
                  _layer_rows(layer, 3, 512), lvec(512), lvec(D_MODEL),
                  wspec(D_MODEL, D_MODEL), lvec(D_MODEL), wspec(D_MODEL, D_MODEL), lvec(D_MODEL),
                  wspec(PLE_DIM, D_MODEL), *head_in],
        out_specs=(*[row(D_MODEL)] * 6, *([vec(LANES), vec(D_MODEL)] if head else [])),
        args=(x, pc, pc, pc, pc, pc, pc, az, ya, p4, cw, cb, bg, wout_full, pg, wpg_full, bpg, wpe_full,
              *(head or ())),
        sem=("arbitrary",) if head else ("parallel",))


def _loss_math(x, target, g):
    r = lax.rsqrt(jnp.mean(x * x, axis=-1, keepdims=True) + EPS)
    xn = x * r
    err = xn * g - target
    per_row = jnp.sum(err * err, axis=-1, keepdims=True)
    loss = jnp.sum(per_row, axis=0, keepdims=True) * (0.5 / D_MODEL)
    dy = err * (1.0 / D_MODEL)
    dg = jnp.sum(dy * xn, axis=0, keepdims=True)
    dxn = dy * g
    return r * (dxn - xn * jnp.mean(dxn * xn, axis=-1, keepdims=True)), loss, dg


def _bwd_mid(dx3, x2, gate, e, pc, az, ya, gated, h2, p4, layer, cw, cb, bg, pg, wpg_full, wout_full, name,
             comm=None):
    s = x2.shape[0]
    ts = min(ROW_TILE, s)
    blk_h = ts // HALO

    def body(dx3_ref, x2_ref, gate_ref, e_ref, cb_ref_, cc_ref, ch_ref, cz_ref, ccp_ref, chp_ref, az_ref, ya_ref,
             gated_ref, h2_ref, p_ref, cw_ref, cbias_ref, bg_ref, pg_ref, wpg_ref, wout_ref,
             dx2_ref, dya_ref, dmisc_ref, dconv_ref, dwout_ref, dwpg_ref, dwpe_ref,
             dbpg_ref, dpg_ref, dbg_ref, dcbias_ref, dcw_ref,
             dgated_ref, halo_ref, acc_out, acc_pg, acc_pe):
        i = pl.program_id(0)

        @pl.when(i == 0)
        def _():
            for ref in (dbpg_ref, dpg_ref, dbg_ref, dcbias_ref, dcw_ref, acc_out, acc_pg, acc_pe):
                ref[...] = jnp.zeros_like(ref)

        lane = lax.broadcasted_iota(jnp.int32, (1, LANES), 1)
        lo = lane < HEAD_DIM
        dx3 = dx3_ref[...]
        gate = gate_ref[...].astype(F32)
        de_b = (dx3 * gate).astype(BF16)
        dgpre = dx3 * e_ref[...].astype(F32) * gate * (1.0 - gate)
        dbpg_ref[...] += jnp.sum(dgpre, axis=0, keepdims=True)
        dgpre_b = dgpre.astype(BF16)
        dh2 = lax.dot_general(dgpre_b, wpg_ref[...], NT, preferred_element_type=F32)
        acc_pe[...] += lax.dot_general(p_ref[...].astype(BF16), de_b, TN, preferred_element_type=F32)
        acc_pg[...] += lax.dot_general(h2_ref[...], dgpre_b, TN, preferred_element_type=F32)

        u, u1, u2 = _conv_taps(cc_ref, ch_ref, ccp_ref, chp_ref, halo_ref, i == 0)
        conv = cbias_ref[...] + cw_ref[0:1, :] * u2 + cw_ref[1:2, :] * u1 + cw_ref[2:3, :] * u
        c_b = cb_ref_[...].astype(F32)
        yc = c_b * conv
        fwd = []
        for sl in range(8):
            cols = slice(LANES * (sl % 4), LANES * (sl % 4 + 1))
            y = yc[:, cols] if sl < 4 else ya_ref[:, cols].astype(F32)
            zc = (cz_ref[:, cols] if sl < 4 else az_ref[:, cols]).astype(F32)
            rg = lax.rsqrt(_group_bcast_sum(y * y, lo) * (1.0 / HEAD_DIM) + EPS)
            sig = _sigmoid(zc)
            fwd.append((rg, y * rg, zc * sig, sig * (1.0 + zc * (1.0 - sig))))

        x2 = x2_ref[...]
        r2 = lax.rsqrt(jnp.mean(x2 * x2, axis=-1, keepdims=True) + EPS)
        xn2 = x2 * r2
        dpg_ref[...] += jnp.sum(dh2 * xn2, axis=0, keepdims=True)
        dxn = dh2 * pg_ref[...]
        dx2 = dx3 + r2 * (dxn - xn2 * jnp.mean(dxn * xn2, axis=-1, keepdims=True))
        dx2_ref[...] = dx2
        dx2_b = dx2.astype(BF16)
        dgated_ref[...] = lax.dot_general(dx2_b, wout_ref[...], NT, preferred_element_type=F32)
        acc_out[...] += lax.dot_general(gated_ref[...], dx2_b, TN, preferred_element_type=F32)

        for sl in range(8):
            cols = slice(LANES * (sl % 4), LANES * (sl % 4 + 1))
            wide = slice(LANES * sl, LANES * (sl + 1))
            rg, yhat, silu, dsilu = fwd[sl]
            bgs = bg_ref[:, wide]
            dgt = dgated_ref[:, wide]
            dyn = dgt * silu
            dzc = dgt * (yhat * bgs) * dsilu
            dbg_ref[:, wide] += jnp.sum(dyn * yhat, axis=0, keepdims=True)
            dyh = dyn * bgs
            dy = rg * (dyh - yhat * (_group_bcast_sum(dyh * yhat, lo) * (1.0 / HEAD_DIM)))
            if sl < 4:
                dconv = dy * c_b[:, cols]
                dmisc_ref[:, cols] = (dy * conv[:, cols]).astype(BF16)
                dmisc_ref[:, 512 + LANES * sl:512 + LANES * (sl + 1)] = dzc.astype(BF16)
                dconv_ref[:, cols] = dconv
                dcbias_ref[:, cols] += jnp.sum(dconv, axis=0, keepdims=True)
                dcw_ref[0:1, cols] += jnp.sum(dconv * u2[:, cols], axis=0, keepdims=True)
                dcw_ref[1:2, cols] += jnp.sum(dconv * u1[:, cols], axis=0, keepdims=True)
                dcw_ref[2:3, cols] += jnp.sum(dconv * u[:, cols], axis=0, keepdims=True)
            else:
                dya_ref[:, cols] = dy.astype(BF16)
                dmisc_ref[:, 1024 + LANES * (sl - 4):1024 + LANES * (sl - 3)] = dzc.astype(BF16)

        @pl.when(i == pl.num_programs(0) - 1)
        def _():
            dwout_ref[...] = acc_out[...].astype(BF16)
            dwpg_ref[...] = acc_pg[...].astype(BF16)
            dwpe_ref[...] = acc_pe[...].astype(BF16)

    row = lambda width, cb_=0: pl.BlockSpec((ts, width), lambda i: (i, cb_))
    prev = lambda cb_: pl.BlockSpec((HALO, 512), lambda i: (jnp.maximum(i * blk_h - 1, 0), cb_))
    vec = lambda width: pl.BlockSpec((1, width), lambda i: (0, 0))
    lvec = lambda width: _layer_rows(layer, 1, width)
    wspec = lambda r_, c_: pl.BlockSpec((r_, c_), lambda i: (0, 0))
    vo = lambda width: jax.ShapeDtypeStruct((1, width), F32)
    sq = jax.ShapeDtypeStruct((D_MODEL, D_MODEL), BF16)
    return _hosted_call(
        body, comm, name=name, grid=(s // ts,), sem=("arbitrary",),
        args=(dx3, x2, gate, e, pc, pc, pc, pc, pc, pc, az, ya, gated, h2, p4, cw, cb, bg, pg, wpg_full, wout_full),
        out_shape=(jax.ShapeDtypeStruct((s, D_MODEL), F32), jax.ShapeDtypeStruct((s, 512), BF16),
                   jax.ShapeDtypeStruct((s, 1536), BF16), jax.ShapeDtypeStruct((s, 512), F32),
                   sq, sq, jax.ShapeDtypeStruct((PLE_DIM, D_MODEL), BF16),
                   vo(D_MODEL), vo(D_MODEL), vo(D_MODEL), vo(512), jax.ShapeDtypeStruct((SUBLANES, 512), F32)),
        in_specs=[row(D_MODEL), row(D_MODEL), row(D_MODEL), row(D_MODEL),
                  row(512, 0), row(512, 1), row(512, 2), row(512, 3), prev(1), prev(2), row(512), row(512),
                  row(D_MODEL), row(D_MODEL),
                  pl.BlockSpec((None, None, ts, PLE_DIM), lambda i: (layer, 0, i, 0)),
                  _layer_rows(layer, 3, 512), lvec(512), lvec(D_MODEL), lvec(D_MODEL),
                  wspec(D_MODEL, D_MODEL), wspec(D_MODEL, D_MODEL)],
        out_specs=(row(D_MODEL), row(512), row(1536), row(512),
                   wspec(D_MODEL, D_MODEL), wspec(D_MODEL, D_MODEL), wspec(PLE_DIM, D_MODEL),
                   vec(D_MODEL), vec(D_MODEL), vec(D_MODEL), vec(512),
                   pl.BlockSpec((SUBLANES, 512), lambda i: (0, 0))),
        scratch_shapes=[pltpu.VMEM((ts, D_MODEL), F32), pltpu.VMEM((HALO, 512), F32),
                        pltpu.VMEM((D_MODEL, D_MODEL), F32), pltpu.VMEM((D_MODEL, D_MODEL), F32),
                        pltpu.VMEM((PLE_DIM, D_MODEL), F32)])


def _attn_bwd(qkv, lsum, nblk, dya, name, comm=None):
    s = qkv.shape[0]
    tq = min(ATTN_TILE, s)
    nq = s // tq
    rc = min(ATTN_ROWS, tq)
    n_rc = tq // rc
    chains = [(r, hh) for r in range(n_rc) for hh in range(2)]

    def body(nblk_ref, q_ref, k_ref, v_ref, lsum_ref, do_ref, dq_ref, dk_ref, dv_ref, dk_acc, dv_acc):
        hp, qi = pl.program_id(0), pl.program_id(1)

        @pl.when(qi == 0)
        def _():
            dk_acc[...] = jnp.zeros_like(dk_acc)
            dv_acc[...] = jnp.zeros_like(dv_acc)

        lo, causal, tri_gt, tri_le = _attn_pieces(tq, rc)
        lane = lax.broadcasted_iota(jnp.int32, (1, LANES), 1)
        qh = _split_heads(q_ref[...], lo)
        doh = _split_heads(do_ref[...].astype(BF16), lo)
        lt = lsum_ref[...]
        ltot_h = (jnp.sum(jnp.where(lane == 0, lt, 0.0), axis=-1, keepdims=True),
                  jnp.sum(jnp.where(lane == HEAD_DIM, lt, 0.0), axis=-1, keepdims=True))
        rows = lambda a_, r: a_[r * rc:(r + 1) * rc]
        qc = {(r, hh): rows(qh[hh], r) for r, hh in chains}
        doc = {(r, hh): rows(doh[hh], r) for r, hh in chains}
        ltot = {(r, hh): rows(ltot_h[hh], r) for r, hh in chains}

        mm = lambda a_, b_: jnp.dot(a_.astype(BF16), b_, preferred_element_type=F32)
        mm_nt = lambda a_, b_: lax.dot_general(a_, b_, NT, preferred_element_type=F32)
        mm_tn = lambda a_, b_: lax.dot_general(a_.astype(BF16), b_, TN, preferred_element_type=F32)
        rowsum = lambda a_: jnp.sum(a_, axis=-1, keepdims=True)

        def block(kb, carry, diag=False):
            start = pl.multiple_of(kb * tq, tq)
            k = k_ref[pl.ds(start, tq), :]
            v = v_ref[pl.ds(start, tq), :]
            kh = _split_heads(k, lo)
            keep = (lambda ch, a_: jnp.where(causal[ch[0]], a_, 0.0)) if diag else (lambda ch, a_: a_)
            z = {ch: mm_nt(qc[ch], k) for ch in chains}
            da = {ch: mm_nt(doc[ch], v) for ch in chains}
            sp = {ch: _softplus(z[ch], causal[ch[0]], diag) for ch in chains}
            later = {ch: mm(sp[ch], tri_gt) for ch in chains}
            walked = {ch: carry[ch[0]][1 + ch[1]] + rowsum(sp[ch]) for ch in chains}
            a = {ch: keep(ch, jnp.exp((z[ch] - sp[ch]) - ((ltot[ch] - walked[ch]) + later[ch]))) for ch in chains}
            g = {ch: a[ch] * da[ch] for ch in chains}
            upto = {ch: mm(g[ch], tri_le) for ch in chains}
            dz = {ch: keep(ch, g[ch] - jnp.exp(z[ch] - sp[ch]) * (carry[ch[0]][3 + ch[1]] + upto[ch])).astype(BF16)
                  for ch in chains}
            dqc = {ch: mm(dz[ch], kh[ch[1]]) for ch in chains}
            dkc = [mm_tn(dz[ch], qc[ch]) for ch in chains]
            dvc = [mm_tn(a[ch], doc[ch]) for ch in chains]
            dk_acc[pl.ds(start, tq), :] += sum(dkc[1:], dkc[0])
            dv_acc[pl.ds(start, tq), :] += sum(dvc[1:], dvc[0])
            return tuple((carry[r][0] + dqc[(r, 0)] + dqc[(r, 1)], walked[(r, 0)], walked[(r, 1)],
                          carry[r][3] + rowsum(g[(r, 0)]), carry[r][4] + rowsum(g[(r, 1)])) for r in range(n_rc))

        zc = jnp.zeros((rc, 1), F32)
        carry = tuple((jnp.zeros((rc, LANES), F32), zc, zc, zc, zc) for _ in range(n_rc))
        near = jnp.maximum(qi - 1, 0)
        first = near - jnp.clip(nblk_ref[hp, qi].astype(jnp.int32), 0, near)
        carry = lax.fori_loop(first, qi, block, carry)
        carry = block(qi, carry, True)
        for r in range(n_rc):
            dq_ref[r * rc:(r + 1) * rc, :] = (carry[r][0] * 0.125).astype(BF16)

        @pl.when(qi == pl.num_programs(1) - 1)
        def _():
            dk_ref[...] = dk_acc[...].astype(BF16)
            dv_ref[...] = dv_acc[...].astype(BF16)

    blk = pl.BlockSpec((tq, LANES), lambda hp, qi: (qi, hp))
    col = pl.BlockSpec((s, LANES), lambda hp, qi: (0, hp))
    o512 = jax.ShapeDtypeStruct((s, D_SB), BF16)
    return _hosted_call(
        body, comm, name=name, grid=(4, nq),
        out_shape=(o512, o512, o512),
        in_specs=[pl.BlockSpec(memory_space=pltpu.SMEM), blk,
                  pl.BlockSpec((s, LANES), lambda hp, qi: (0, 4 + hp)),
                  pl.BlockSpec((s, LANES), lambda hp, qi: (0, 8 + hp)), blk, blk],
        out_specs=(blk, col, col),
        scratch_shapes=[pltpu.VMEM((s, LANES), F32), pltpu.VMEM((s, LANES), F32)],
        args=(nblk, qkv, qkv, qkv, lsum, dya), sem=("parallel", "arbitrary"))


def _bwd_dproj(dmisc, dconv, pc, dq, dk, dv, x, dx2, g, cw, layer, win_full, name, comm=None):
    s = x.shape[0]
    ts = min(ROW_TILE, s)
    blk8 = ts // SUBLANES
    last8 = s // SUBLANES - 1

    def body(dcb_ref, dcz_ref, daz_ref, dconv_ref, nxt_ref, cc_ref, ch_ref, dq_ref, dk_ref, dv_ref,
             x_ref, dx2_ref, g_ref, cw_ref, w_ref, dproj_ref, dx_ref, dg_ref):
        i = pl.program_id(0)

        @pl.when(i == 0)
        def _():
            dg_ref[...] = jnp.zeros_like(dg_ref)

        keep = jnp.where(i == pl.num_programs(0) - 1, 0.0, 1.0)
        dc = dconv_ref[...]
        n0 = nxt_ref[0:1, :] * keep
        n1 = nxt_ref[1:2, :] * keep
        rowi = lax.broadcasted_iota(jnp.int32, dc.shape, 0)
        dc1 = jnp.where(rowi == ts - 1, n0, pltpu.roll(dc, ts - 1, 0))
        dc2 = jnp.where(rowi == ts - 2, n0, jnp.where(rowi == ts - 1, n1, pltpu.roll(dc, ts - 2, 0)))
        du = cw_ref[2:3, :] * dc + cw_ref[1:2, :] * dc1 + cw_ref[0:1, :] * dc2
        dproj_ref[:, 0:512] = dcb_ref[...]
        dproj_ref[:, 512:1024] = (du * ch_ref[...].astype(F32)).astype(BF16)
        dproj_ref[:, 1024:1536] = (du * cc_ref[...].astype(F32)).astype(BF16)
        dproj_ref[:, 1536:2048] = dcz_ref[...]
        dproj_ref[:, 2048:2560] = dq_ref[...]
        dproj_ref[:, 2560:3072] = dk_ref[...]
        dproj_ref[:, 3072:3584] = dv_ref[...]
        dproj_ref[:, 3584:4096] = daz_ref[...]
        dh = lax.dot_general(dproj_ref[...], w_ref[...], NT, preferred_element_type=F32)
        x = x_ref[...]
        r = lax.rsqrt(jnp.mean(x * x, axis=-1, keepdims=True) + EPS)
        xn = x * r
        dg_ref[...] += jnp.sum(dh * xn, axis=0, keepdims=True)
        dxn = dh * g_ref[...]
        dx_ref[...] = dx2_ref[...] + r * (dxn - xn * jnp.mean(dxn * xn, axis=-1, keepdims=True))

    row = lambda width, cb_=0: pl.BlockSpec((ts, width), lambda i: (i, cb_))
    nxt = pl.BlockSpec((SUBLANES, 512), lambda i: (jnp.minimum((i + 1) * blk8, last8), 0))
    vec = lambda width: pl.BlockSpec((1, width), lambda i: (0, 0))
    lvec = lambda width: _layer_rows(layer, 1, width)
    return _hosted_call(
        body, comm, name=name, grid=(s // ts,),
        out_shape=(jax.ShapeDtypeStruct((s, N_IN), BF16), jax.ShapeDtypeStruct((s, D_MODEL), F32),
                   jax.ShapeDtypeStruct((1, D_MODEL), F32)),
        in_specs=[row(512, 0), row(512, 1), row(512, 2), row(512), nxt, row(512, 1), row(512, 2),
                  row(512), row(512), row(512), row(D_MODEL), row(D_MODEL), lvec(D_MODEL),
                  _layer_rows(layer, 3, 512),
                  pl.BlockSpec((D_MODEL, N_IN), lambda i: (0, 0))],
        out_specs=(row(N_IN), row(D_MODEL), vec(D_MODEL)),
        args=(dmisc, dmisc, dmisc, dconv, dconv, pc, pc, dq, dk, dv, x, dx2, g, cw, win_full),
        sem=("arbitrary",))


def _atb(a, b, name, a_cols=None, comm=None):
    s, n = b.shape
    m, a_blk = (a.shape[-1], 0) if a_cols is None else a_cols
    ts = min(512, s)
    tn = min(2048, n)
    a_spec = pl.BlockSpec((ts, m), lambda j, i: (i, a_blk))

    def body(a_ref, b_ref, o_ref, acc_ref):
        i = pl.program_id(1)

        @pl.when(i == 0)
        def _():
            acc_ref[...] = jnp.zeros_like(acc_ref)

        acc_ref[...] += lax.dot_general(a_ref[...].astype(BF16), b_ref[...], TN, preferred_element_type=F32)

        @pl.when(i == pl.num_programs(1) - 1)
        def _():
            o_ref[...] = acc_ref[...].astype(BF16)

    (out,), got = _hosted_call(
        body, comm, name=name, grid=(n // tn, s // ts),
        out_shape=(jax.ShapeDtypeStruct((m, n), BF16),),
        in_specs=[a_spec, pl.BlockSpec((ts, tn), lambda j, i: (i, j))],
        out_specs=(pl.BlockSpec((m, tn), lambda j, i: (0, j)),),
        scratch_shapes=[pltpu.VMEM((m, tn), F32)],
        args=(a, b), sem=("parallel", "arbitrary"))
    return out, got


def _adamw_math(w, g, m, v):
    m2 = ADAM_B1 * m + (1.0 - ADAM_B1) * g
    v2 = ADAM_B2 * v + (1.0 - ADAM_B2) * (g * g)
    m_hat = m2 / (1.0 - ADAM_B1 ** ADAM_STEP)
    v_hat = v2 / (1.0 - ADAM_B2 ** ADAM_STEP)
    delta = -ADAM_LR * (m_hat / (jnp.sqrt(v_hat) + ADAM_EPS) + ADAM_WD * w)
    return delta, m2, v2


def _adamw_sum8(pieces, w, m, v, name):
    _, rows, cols = w.shape
    tr = min([rows, 256] + [pc_[0].shape[1] for pc_ in pieces])
    n_tiles = rows // tr
    n_p = len(pieces)
    spans = [(layer, row0 // tr, arr.shape[1] // tr) for arr, layer, row0 in pieces]

    def body(*refs):
        p_refs = refs[:n_p]
        w_ref, m_ref, v_ref, g_ref, d_ref, m2_ref, v2_ref = refs[n_p:]
        l, i = pl.program_id(0), pl.program_id(1)

        def run(p_ref):
            g = p_ref[0].astype(F32)
            for d in range(1, N_DEV):
                g = g + p_ref[d].astype(F32)
            g_ref[...] = g
            d_ref[...], m2_ref[...], v2_ref[...] = _adamw_math(w_ref[...], g, m_ref[...], v_ref[...])

        for p_ref, (layer, t0, nt) in zip(p_refs, spans):
            mine = jnp.logical_and(l == layer, jnp.logical_and(i >= t0, i < t0 + nt))
            pl.when(mine)(lambda p_ref=p_ref: run(p_ref))

    def piece_spec(layer, t0, nt):
        return pl.BlockSpec((N_DEV, tr, cols),
                            lambda l, i: (0, jnp.clip(jnp.where(l == layer, i - t0, jnp.where(l < layer, 0, nt - 1)),
                                                      0, nt - 1), 0))

    tile = pl.BlockSpec((None, tr, cols), lambda l, i: (l, i, 0))
    o = jax.ShapeDtypeStruct((DEPTH, rows, cols), F32)
    return _call(
        body, name=name, grid=(DEPTH, n_tiles),
        out_shape=(o, o, o, o),
        in_specs=[*[piece_spec(*sp) for sp in spans], tile, tile, tile],
        out_specs=(tile, tile, tile, tile),
        compiler_params=_params(("arbitrary", "arbitrary"), VMEM_LIMIT),
    )(*[pc_[0] for pc_ in pieces], w, m, v)


def _adamw_plain(g, w, m, v, name):
    rows, cols = g.shape

    def body(g_ref, w_ref, m_ref, v_ref, d_ref, m2_ref, v2_ref):
        d_ref[...], m2_ref[...], v2_ref[...] = _adamw_math(w_ref[...], g_ref[...], m_ref[...], v_ref[...])

    full = pl.BlockSpec((rows, cols), lambda: (0, 0))
    o = jax.ShapeDtypeStruct((rows, cols), F32)
    return _call(body, name=name, out_shape=(o, o, o), in_specs=[full] * 4, out_specs=(full,) * 3)(g, w, m, v)


def _small_update(parts, params):
    n = len(params)
    rows = [w.shape[0] for w, _, _ in params]
    offs = [sum(rows[:k]) for k in range(n)]

    def body(*refs):
        p_ref, wmv, outs = refs[0], refs[1:1 + 3 * n], refs[1 + 3 * n:]
        g = p_ref[0]
        for d in range(1, N_DEV):
            g = g + p_ref[d]
        outs[0][...] = g
        for k in range(n):
            gk = g[offs[k]:offs[k] + rows[k]]
            w_ref, m_ref, v_ref = wmv[3 * k:3 * k + 3]
            g_ref, d_ref, m2_ref, v2_ref = outs[1 + 4 * k:5 + 4 * k]
            g_ref[...] = gk
            d_ref[...], m2_ref[...], v2_ref[...] = _adamw_math(w_ref[...], gk, m_ref[...], v_ref[...])

    whole = lambda shape: pl.BlockSpec(shape, lambda: (0,) * len(shape))
    flat_in = [a for wmv in params for a in wmv]
    out_shape = [jax.ShapeDtypeStruct((SMALL_ROWS, LANES), F32)]
    for r in rows:
        out_shape += [jax.ShapeDtypeStruct((r, LANES), F32)] * 4
    outs = _call(
        body, name="adamw_small",
        out_shape=tuple(out_shape),
        in_specs=[whole(parts.shape)] + [whole(a.shape) for a in flat_in],
        out_specs=tuple(whole(o.shape) for o in out_shape),
    )(parts, *flat_in)
    return outs[0], [outs[1 + 4 * k:5 + 4 * k] for k in range(n)]


def kernel(x, p, norm_g, w_in, conv_w, conv_b, branch_g, w_out, ple_norm_g, w_pg, b_pg, w_pe, final_g, loss_target, m_norm_g, m_w_in, m_conv_w, m_conv_b, m_branch_g, m_w_out, m_ple_norm_g, m_w_pg, m_b_pg, m_w_pe, m_final_g, v_norm_g, v_w_in, v_conv_w, v_conv_b, v_branch_g, v_w_out, v_ple_norm_g, v_w_pg, v_b_pg, v_w_pe, v_final_g):
    s = x.shape[1]
    x0 = x.reshape(s, D_MODEL)
    target = loss_target.reshape(s, D_MODEL)
    me_blk = _my_block()

    win_s, wout_s, wpg_s, wpe_s = _cast_bf16(
        [w_in.reshape(DEPTH * D_MODEL, 512), w_out.reshape(DEPTH * 128, D_MODEL),
         w_pg.reshape(DEPTH * 128, D_MODEL), w_pe.reshape(DEPTH * PLE_DIM, 128)], "cast_weights")
    win_s, wout_s = win_s.reshape(DEPTH, D_MODEL, 512), wout_s.reshape(DEPTH, 128, D_MODEL)
    wpg_s, wpe_s = wpg_s.reshape(DEPTH, 128, D_MODEL), wpe_s.reshape(DEPTH, PLE_DIM, 128)
    cw_s = jnp.zeros((SUBLANES, LANES), F32).at[:DEPTH * 3, :HEAD_DIM].set(conv_w.reshape(DEPTH * 3, HEAD_DIM))
    bf = lambda r_, c_: jax.ShapeDtypeStruct((r_, c_), BF16)
    w_items = lambda l: [(wout_s[l], bf(D_MODEL, D_MODEL), "rows128"), (wpg_s[l], bf(D_MODEL, D_MODEL), "rows128"),
                         (wpe_s[l], bf(PLE_DIM, D_MODEL), "cols128")]
    win_f = [None] * DEPTH
    win_f[0], cw_all = _comm_call(_gather_comm([
        (win_s[0], bf(D_MODEL, N_IN), "cols512"),
        (cw_s, jax.ShapeDtypeStruct((N_DEV, SUBLANES, LANES), F32), "slot")]), "gather_w_in_0")
    cw_full = jnp.transpose(cw_all[:, :DEPTH * 3, :HEAD_DIM].reshape(N_DEV, DEPTH, 3, HEAD_DIM), (1, 2, 0, 3))
    cw_full = cw_full.reshape(DEPTH, 3, D_CONV)
    gather_rest_0 = _gather_comm(w_items(0))
    gather_win_1 = _gather_comm([(win_s[1], bf(D_MODEL, N_IN), "cols512")])
    gather_rest_1 = _gather_comm(w_items(1))

    norm3, convb3, branch3, ple3, bpg3 = [a.reshape(DEPTH, 1, -1) for a in (norm_g, conv_b, branch_g, ple_norm_g, b_pg)]

    saved = []
    xl = x0
    wout_f, wpg_f, wpe_f = [None] * DEPTH, [None] * DEPTH, [None] * DEPTH
    for l in range(DEPTH):
        (h, pc, qkv, az), got = _fwd_in(xl, norm3, l, win_f[l], f"fwd_in_{l}",
                                        comm=gather_rest_0 if l == 0 else None)
        if l == 0:
            wout_f[0], wpg_f[0], wpe_f[0] = got
        (ya, lsum, nblk), got = _attn_fwd(qkv, f"attn_fwd_{l}", comm=gather_win_1 if l == 0 else None)
        if l == 0:
            (win_f[1],) = got
        last = l == DEPTH - 1
        outs, got = _fwd_mid(
            xl, pc, az, ya, p, l, cw_full, convb3, branch3, wout_f[l],
            ple3, wpg_f[l], bpg3, wpe_f[l], f"fwd_mid_{l}",
            comm=gather_rest_1 if l == 0 else None, head=(target, final_g[None, :]) if last else None)
        x2, x3, gated, h2, gate, e = outs[:6]
        if l == 0:
            wout_f[1], wpg_f[1], wpe_f[1] = got
        saved.append(dict(x=xl, h=h, pc=pc, qkv=qkv, az=az, ya=ya, lsum=lsum, nblk=nblk, x2=x2, gated=gated, h2=h2,
                          gate=gate, e=e))
        xl = x3

    dx, (loss_acc, d_final_g) = xl, outs[6:]

    dwin, dwout, dwpg, dwpe = [None] * DEPTH, [None] * DEPTH, [None] * DEPTH, [None] * DEPTH
    small = dict(norm_g=[None] * DEPTH, conv_b=[None] * DEPTH, branch_g=[None] * DEPTH,
                 ple_norm_g=[None] * DEPTH, b_pg=[None] * DEPTH, conv_w=[None] * DEPTH)
    slot = lambda r_, c_: jax.ShapeDtypeStruct((r_, c_), BF16)
    half = D_MODEL // 2
    r_in1, r_out, r_pg, r_pe = None, [None] * DEPTH, [None] * DEPTH, [None] * DEPTH

    def rest_items(l):
        return [(dwout[l], slot(128, D_MODEL), "rows128"), (dwpg[l], slot(128, D_MODEL), "rows128"),
                (dwpe[l], slot(PLE_DIM, 128), "cols128")]

    for l in reversed(range(DEPTH)):
        sv = saved[l]
        ride = _exchange_comm(rest_items(1)) if l == 0 else None
        (dx2, dya, dmisc, dconv, dwout[l], dwpg[l], dwpe[l], d_bpg, d_pg, d_bg, d_cbias, d_cw), got = _bwd_mid(
            dx, sv["x2"], sv["gate"], sv["e"], sv["pc"], sv["az"], sv["ya"], sv["gated"], sv["h2"], p, l,
            cw_full, convb3, branch3, ple3, wpg_f[l], wout_f[l], f"bwd_mid_{l}",
            comm=ride)
        if l == 0:
            r_out[1], r_pg[1], r_pe[1] = got
        ride = _exchange_comm([(dwin[1], slot(D_MODEL, 512), "cols512")] + rest_items(0)) if l == 0 else None
        (dq, dk, dv), got = _attn_bwd(sv["qkv"], sv["lsum"], sv["nblk"], dya, f"attn_bwd_{l}", comm=ride)
        if l == 0:
            r_in1, r_out[0], r_pg[0], r_pe[0] = got
        (dproj, dx, d_ng), _ = _bwd_dproj(dmisc, dconv, sv["pc"], dq, dk, dv, sv["x"], dx2, norm3, cw_full, l,
                                          win_f[l], f"bwd_dproj_{l}")
        if l == 1:
            dwin[1], _ = _atb(sv["h"], dproj, "dw_in_1")
        else:
            dwin_top, _ = _atb(sv["h"], dproj, "dw_in_0_top", a_cols=(half, 0))
            dwin_bot, (r_in0_top,) = _atb(sv["h"], dproj, "dw_in_0_bottom", a_cols=(half, 1),
                                          comm=_exchange_comm([(dwin_top, slot(half, 512), "cols512")]))
        small["norm_g"][l], small["conv_b"][l], small["branch_g"][l] = d_ng, d_cbias, d_bg
        small["ple_norm_g"][l], small["b_pg"][l], small["conv_w"][l] = d_pg, d_bpg, d_cw[:3]
    grad_x = dx.reshape(1, s, D_MODEL)

    flat = lambda parts: jnp.concatenate([a.reshape(-1) for a in parts])
    small_vec = jnp.concatenate([
        flat(small["norm_g"]), flat(small["conv_b"]), flat(small["branch_g"]), flat(small["ple_norm_g"]),
        flat(small["b_pg"]), d_final_g.reshape(-1), flat(small["conv_w"]), loss_acc.reshape(-1),
        jnp.zeros(((SMALL_ROWS - SMALL_GRAD_ROWS - 1) * LANES,), F32)]).reshape(SMALL_ROWS, LANES)
    r_in0_bot, r_small = _comm_call(_exchange_comm([
        (dwin_bot, slot(half, 512), "cols512"),
        (small_vec, jax.ShapeDtypeStruct((SMALL_ROWS, LANES), F32), "slot")]), "exchange_last")

    per_layer = lambda r: [(r[0], 0, 0), (r[1], 1, 0)]
    g_win, d_win, m_win, v_win = _adamw_sum8([(r_in0_top, 0, 0), (r_in0_bot, 0, half), (r_in1, 1, 0)],
                                             w_in, m_w_in, v_w_in, "adamw_w_in")
    g_wout, d_wout, m_wout, v_wout = _adamw_sum8(per_layer(r_out), w_out, m_w_out, v_w_out, "adamw_w_out")
    g_wpg, d_wpg, m_wpg, v_wpg = _adamw_sum8(per_layer(r_pg), w_pg, m_w_pg, v_w_pg, "adamw_w_pg")
    g_wpe, d_wpe, m_wpe, v_wpe = _adamw_sum8(per_layer(r_pe), w_pe, m_w_pe, v_w_pe, "adamw_w_pe")

    repl = [(norm_g, m_norm_g, v_norm_g), (conv_b, m_conv_b, v_conv_b), (branch_g, m_branch_g, v_branch_g),
            (ple_norm_g, m_ple_norm_g, v_ple_norm_g), (b_pg, m_b_pg, v_b_pg), (final_g, m_final_g, v_final_g)]
    g_small, upd = _small_update(r_small, [tuple(a.reshape(-1, LANES) for a in t) for t in repl])
    g_r, d_r, m_r, v_r = [[upd[k][j].reshape(repl[k][0].shape) for k in range(len(repl))] for j in range(4)]

    loss = g_small[SMALL_GRAD_ROWS, 0]
    g_cw_full = g_small[SMALL_REPL_ROWS:SMALL_GRAD_ROWS].reshape(DEPTH, 3, D_CONV)
    g_cw = lax.dynamic_slice(g_cw_full, (0, 0, me_blk * HEAD_DIM), (DEPTH, 3, HEAD_DIM))
    pad_cw = lambda a: jnp.zeros((SUBLANES, LANES), F32).at[:3].set(a.reshape(3, LANES))
    v_cw_pad = jnp.ones((SUBLANES, LANES), F32).at[:3].set(v_conv_w.reshape(3, LANES))
    d_cw, m_cw, v_cw = _adamw_plain(pad_cw(g_cw), pad_cw(conv_w), pad_cw(m_conv_w), v_cw_pad, "adamw_conv_w")
    un_cw = lambda a: a[:3].reshape(DEPTH, 3, HEAD_DIM)

    def ordered(r, win_, cw_, wout_, wpg_, wpe_):
        return [r[0], win_, cw_, r[1], r[2], wout_, r[3], wpg_, r[4], wpe_, r[5]]

    grads = ordered(g_r, g_win, g_cw, g_wout, g_wpg, g_wpe)
    deltas = ordered(d_r, d_win, un_cw(d_cw), d_wout, d_wpg, d_wpe)
    new_m = ordered(m_r, m_win, un_cw(m_cw), m_wout, m_wpg, m_wpe)
    new_v = ordered(v_r, v_win, un_cw(v_cw), v_wout, v_wpg, v_wpe)
    return (loss, grad_x, *grads, *deltas, *new_m, *new_v)
```

```python
import jax
import jax.numpy as jnp
from jax import lax
from jax.experimental import pallas as pl
from jax.experimental.pallas import tpu as pltpu

F32 = jnp.float32
BF16 = jnp.bfloat16

D_MODEL = 1024
D_CONV = 512
D_SB = 512
N_IN = 4096
HEAD_DIM = 64
PLE_DIM = 256
DEPTH = 2
EPS = 1e-6
ADAM_LR = 0.001
ADAM_B1 = 0.9
ADAM_B2 = 0.999
ADAM_EPS = 1e-08
ADAM_WD = 0.01
ADAM_STEP = 10

LANES = 128
SUBLANES = 8
VMEM_BYTES_V7X = 64 * 1024 * 1024
VMEM_LIMIT = VMEM_BYTES_V7X - 8 * 1024 * 1024

N_DEV = 8
ROW_TILE = 256
ATTN_TILE = 256

NT = (((1,), (1,)), ((), ()))
TN = (((0,), (0,)), ((), ()))


def _call(body, **kw):
    return pl.pallas_call(body, **kw)


def _params(sem=None, vmem=None):
    return pltpu.CompilerParams(dimension_semantics=sem, vmem_limit_bytes=vmem)


def _sigmoid(z):
    return 0.5 * jnp.tanh(0.5 * z) + 0.5


def _group_bcast_sum(a, lo):
    s_lo = jnp.sum(jnp.where(lo, a, 0.0), axis=-1, keepdims=True)
    s_hi = jnp.sum(jnp.where(lo, 0.0, a), axis=-1, keepdims=True)
    return jnp.where(lo, s_lo, s_hi)


def _layer_rows(layer, rows, width):
    return pl.BlockSpec((None, rows, width), lambda i: (layer, 0, 0))


def _my_block():
    return 4 * lax.axis_index("x") + 2 * lax.axis_index("y") + lax.axis_index("c")


def _cast_bf16(arrays, name):
    n = len(arrays)

    def body(*refs):
        for a_ref, o_ref in zip(refs[:n], refs[n:]):
            o_ref[...] = a_ref[...].astype(BF16)

    whole = lambda a: pl.BlockSpec(a.shape, lambda: (0, 0))
    return _call(
        body, name=name,
        out_shape=tuple(jax.ShapeDtypeStruct(a.shape, BF16) for a in arrays),
        in_specs=[whole(a) for a in arrays], out_specs=tuple(whole(a) for a in arrays),
        compiler_params=_params(None, VMEM_LIMIT),
    )(*arrays)


class _Comm:
    def __init__(self, inputs, out_shapes, scratch, begin, middle, finish):
        self.inputs, self.out_shapes, self.scratch = list(inputs), list(out_shapes), list(scratch)
        self.begin, self.middle, self.finish = begin, middle, finish


def _slab(kind, ref, blk):
    if kind == "cols512":
        return ref.at[:, pl.ds(blk * 512, 512)]
    if kind == "rows128":
        return ref.at[pl.ds(blk * 128, 128), :]
    if kind == "cols128":
        return ref.at[:, pl.ds(blk * 128, 128)]
    return ref.at[blk]


def _gather_comm(items):
    n_t = len(items)
    kinds = [it[2] for it in items]

    def ctx(ins, outs, sems):
        send_sems, recv_sems, local_sems = sems
        x, y, c = lax.axis_index("x"), lax.axis_index("y"), lax.axis_index("c")
        me, sibling = (x, y, c), (x, y, 1 - c)
        chips = [(1 - x, y), (x, 1 - y), (1 - x, 1 - y)]

        def place(t, dev):
            return _slab(kinds[t], outs[t], 4 * dev[0] + 2 * dev[1] + dev[2])

        def copy(t, k, block, to, own=False):
            return pltpu.make_async_remote_copy(
                src_ref=ins[t] if own else place(t, block), dst_ref=place(t, block),
                send_sem=send_sems.at[t, k], recv_sem=recv_sems.at[t, k],
                device_id=to, device_id_type=pl.DeviceIdType.MESH)

        mine = [pltpu.make_async_copy(ins[t], place(t, me), local_sems.at[t]) for t in range(n_t)]
        first = []
        for t in range(n_t):
            first.append(copy(t, 0, me, sibling, own=True))
            first += [copy(t, 1 + j, me, (*chip, c), own=True) for j, chip in enumerate(chips)]
        passed = [copy(t, 4 + j, (*chip, c), sibling) for j, chip in enumerate(chips) for t in range(n_t)]
        landed = [copy(t, 1 + j, (*chip, c), me) for j, chip in enumerate(chips) for t in range(n_t)]
        from_sibling = []
        for t in range(n_t):
            from_sibling.append(copy(t, 0, sibling, me))
            from_sibling += [copy(t, 4 + j, (*chip, 1 - c), me) for j, chip in enumerate(chips)]
        return mine, first, landed, passed, from_sibling

    def begin(ins, outs, sems):
        mine, first, _, _, _ = ctx(ins, outs, sems)
        for cp in mine + first:
            cp.start()

    def middle(ins, outs, sems):
        _, _, landed, passed, _ = ctx(ins, outs, sems)
        for got, fwd in zip(landed, passed):
            got.wait_recv()
            fwd.start()

    def finish(ins, outs, sems):
        mine, first, _, passed, from_sibling = ctx(ins, outs, sems)
        for cp in from_sibling:
            cp.wait_recv()
        for cp in first + passed:
            cp.wait_send()
        for cp in mine:
            cp.wait()

    scratch = [pltpu.SemaphoreType.DMA((n_t, 7)), pltpu.SemaphoreType.DMA((n_t, 7)), pltpu.SemaphoreType.DMA((n_t,))]
    return _Comm([it[0] for it in items], [it[1] for it in items], scratch, begin, middle, finish)


def _exchange_comm(items):
    n_t = len(items)
    kinds = [it[2] for it in items]

    def ctx(ins, outs, sems):
        send_sems, recv_sems, local_sems = sems
        x, y, c = lax.axis_index("x"), lax.axis_index("y"), lax.axis_index("c")
        me_blk = 4 * x + 2 * y + c

        def src(t, blk):
            return ins[t] if kinds[t] == "slot" else _slab(kinds[t], ins[t], blk)

        local = [pltpu.make_async_copy(src(t, me_blk), outs[t].at[me_blk], local_sems.at[t]) for t in range(n_t)]
        remote = []
        for k in range(1, N_DEV):
            px = 1 - x if k & 4 else x
            py = 1 - y if k & 2 else y
            pc_ = 1 - c if k & 1 else c
            for t in range(n_t):
                remote.append(pltpu.make_async_remote_copy(
                    src_ref=src(t, 4 * px + 2 * py + pc_), dst_ref=outs[t].at[me_blk],
                    send_sem=send_sems.at[k - 1, t], recv_sem=recv_sems.at[k - 1, t],
                    device_id=(px, py, pc_), device_id_type=pl.DeviceIdType.MESH))
        return local, remote

    def begin(ins, outs, sems):
        local, remote = ctx(ins, outs, sems)
        for cp in local + remote:
            cp.start()

    def finish(ins, outs, sems):
        local, remote = ctx(ins, outs, sems)
        for cp in remote:
            cp.wait_recv()
        for cp in remote:
            cp.wait_send()
        for cp in local:
            cp.wait()

    scratch = [pltpu.SemaphoreType.DMA((N_DEV - 1, n_t)), pltpu.SemaphoreType.DMA((N_DEV - 1, n_t)),
               pltpu.SemaphoreType.DMA((n_t,))]
    out_shapes = [jax.ShapeDtypeStruct((N_DEV, *it[1].shape), it[1].dtype) for it in items]
    return _Comm([it[0] for it in items], out_shapes, scratch, begin, None, finish)


def _comm_call(comm, name):
    n_in, n_out = len(comm.inputs), len(comm.out_shapes)

    def body(*refs):
        ins, outs, sems = refs[:n_in], refs[n_in:n_in + n_out], refs[n_in + n_out:]
        comm.begin(ins, outs, sems)
        if comm.middle is not None:
            comm.middle(ins, outs, sems)
        comm.finish(ins, outs, sems)

    any_spec = pl.BlockSpec(memory_space=pl.ANY)
    return _call(body, name=name, out_shape=tuple(comm.out_shapes), in_specs=[any_spec] * n_in,
                 out_specs=[any_spec] * n_out, scratch_shapes=comm.scratch)(*comm.inputs)


def _hosted(body, n_in, n_out, comm, first, last, middle):
    if comm is None:
        return lambda *refs: body(*refs)
    n_ci, n_co, n_cs = len(comm.inputs), len(comm.out_shapes), len(comm.scratch)

    def wrapped(*refs):
        ins, cin = refs[:n_in], refs[n_in:n_in + n_ci]
        o0 = n_in + n_ci
        outs, cout = refs[o0:o0 + n_out], refs[o0 + n_out:o0 + n_out + n_co]
        scr, csem = refs[o0 + n_out + n_co:len(refs) - n_cs], refs[len(refs) - n_cs:]
        pl.when(first())(lambda: comm.begin(cin, cout, csem))
        body(*ins, *outs, *scr)
        if comm.middle is not None:
            pl.when(middle())(lambda: comm.middle(cin, cout, csem))
        pl.when(last())(lambda: comm.finish(cin, cout, csem))

    return wrapped


def _hosted_call(body, comm, *, name, grid, out_shape, in_specs, out_specs, args, scratch_shapes=(), sem=None):
    nd = len(grid)
    first, last, middle = _at_first(nd), _at_last(nd), _at_middle(nd)
    if comm is not None:
        sem = ("arbitrary",) * nd
    n_in, n_out = len(in_specs), len(out_shape)
    any_spec = pl.BlockSpec(memory_space=pl.ANY)
    c_in = [] if comm is None else comm.inputs
    c_out = [] if comm is None else comm.out_shapes
    c_scr = [] if comm is None else comm.scratch
    outs = _call(
        _hosted(body, n_in, n_out, comm, first, last, middle), name=name, grid=grid,
        out_shape=(*out_shape, *c_out),
        in_specs=[*in_specs, *[any_spec] * len(c_in)],
        out_specs=(*out_specs, *[any_spec] * len(c_out)),
        scratch_shapes=[*scratch_shapes, *c_scr],
        compiler_params=_params(sem, VMEM_LIMIT),
    )(*args, *c_in)
    return outs[:n_out], outs[n_out:]


def _grid_step(ndim):
    i, n = pl.program_id(0), pl.num_programs(0)
    for d in range(1, ndim):
        i, n = i * pl.num_programs(d) + pl.program_id(d), n * pl.num_programs(d)
    return i, n


def _at_first(ndim):
    return lambda: _grid_step(ndim)[0] == 0


def _at_last(ndim):
    def pred():
        i, n = _grid_step(ndim)
        return i == n - 1
    return pred


def _at_middle(ndim):
    def pred():
        i, n = _grid_step(ndim)
        return i == (3 * n) // 4
    return pred


def _fwd_in(x, g, layer, w_full, name, comm=None):
    s = x.shape[0]
    ts = min(ROW_TILE, s)

    def body(x_ref, g_ref, w_ref, h_ref, pc_ref, qkv_ref, az_ref):
        xf = x_ref[...]
        r = lax.rsqrt(jnp.mean(xf * xf, axis=-1, keepdims=True) + EPS)
        h = (xf * r * g_ref[...]).astype(BF16)
        h_ref[...] = h
        pc_ref[...] = jnp.dot(h, w_ref[:, 0:2048], preferred_element_type=F32).astype(BF16)
        q = jnp.dot(h, w_ref[:, 2048:2560], preferred_element_type=F32)
        qkv_ref[:, 0:512] = (q * 0.125).astype(BF16)
        qkv_ref[:, 512:1536] = jnp.dot(h, w_ref[:, 2560:3584], preferred_element_type=F32).astype(BF16)
        az_ref[...] = jnp.dot(h, w_ref[:, 3584:4096], preferred_element_type=F32).astype(BF16)

    row = lambda width: pl.BlockSpec((ts, width), lambda i: (i, 0))
    return _hosted_call(
        body, comm, name=name, grid=(s // ts,),
        out_shape=(jax.ShapeDtypeStruct((s, D_MODEL), BF16), jax.ShapeDtypeStruct((s, 2048), BF16),
                   jax.ShapeDtypeStruct((s, 1536), BF16), jax.ShapeDtypeStruct((s, 512), BF16)),
        in_specs=[row(D_MODEL), _layer_rows(layer, 1, D_MODEL),
                  pl.BlockSpec((D_MODEL, N_IN), lambda i: (0, 0))],
        out_specs=(row(D_MODEL), row(2048), row(1536), row(512)),
        args=(x, g, w_full), sem=("parallel",))


ATTN_ROWS = 128
ATTN_DONE = 104.0


def _attn_pieces(tq, rc):
    lane = lax.broadcasted_iota(jnp.int32, (1, LANES), 1)
    lo = lane < HEAD_DIM
    row = lax.broadcasted_iota(jnp.int32, (tq, tq), 0)
    col = lax.broadcasted_iota(jnp.int32, (tq, tq), 1)
    tri_gt = jnp.where(row > col, 1.0, 0.0).astype(BF16)
    tri_le = jnp.where(row <= col, 1.0, 0.0).astype(BF16)
    rrow = lax.broadcasted_iota(jnp.int32, (rc, tq), 0)
    rcol = lax.broadcasted_iota(jnp.int32, (rc, tq), 1)
    causal = [rcol < rrow + r * rc for r in range(tq // rc)]
    return lo, causal, tri_gt, tri_le


def _split_heads(a, lo):
    z = jnp.zeros_like(a)
    return (jnp.where(lo, a, z), jnp.where(lo, z, a))


def _softplus(z, causal, diag):
    neg_abs = lax.bitcast_convert_type(lax.bitcast_convert_type(z, jnp.uint32) | jnp.uint32(0x80000000), F32)
    sp = jnp.maximum(z, 0.0) + jnp.log(1.0 + jnp.exp(neg_abs))
    if diag:
        sp = jnp.where(causal, sp, 0.0)
    return sp


def _attn_fwd(qkv, name, comm=None):
    s = qkv.shape[0]
    tq = min(ATTN_TILE, s)
    nq = s // tq
    rc = min(ATTN_ROWS, tq)
    n_rc = tq // rc
    chains = [(r, hh) for r in range(n_rc) for hh in range(2)]

    def body(q_ref, k_ref, v_ref, o_ref, lsum_ref, nblk_ref):
        hp, qi = pl.program_id(0), pl.program_id(1)
        lo, causal, tri_gt, _ = _attn_pieces(tq, rc)
        qh = _split_heads(q_ref[...], lo)
        qc = {(r, hh): qh[hh][r * rc:(r + 1) * rc] for r, hh in chains}

        mm = lambda a_, b_: jnp.dot(a_.astype(BF16), b_, preferred_element_type=F32)
        rowsum = lambda a_: jnp.sum(a_, axis=-1, keepdims=True)

        def block(kb, carry):
            start = pl.multiple_of(kb * tq, tq)
            k = k_ref[pl.ds(start, tq), :]
            vh = _split_heads(v_ref[pl.ds(start, tq), :], lo)
            z = {ch: lax.dot_general(qc[ch], k, NT, preferred_element_type=F32) for ch in chains}
            sp = {ch: _softplus(z[ch], None, False) for ch in chains}
            later = {ch: mm(sp[ch], tri_gt) for ch in chains}
            a = {ch: jnp.exp((z[ch] - sp[ch]) - (carry[ch[0]][1 + ch[1]] + later[ch])) for ch in chains}
            pv = {ch: mm(a[ch], vh[ch[1]]) for ch in chains}
            return tuple((carry[r][0] + pv[(r, 0)] + pv[(r, 1)],
                          carry[r][1] + rowsum(sp[(r, 0)]), carry[r][2] + rowsum(sp[(r, 1)])) for r in range(n_rc))

        def first_two(prev_ok):
            d0 = pl.multiple_of(qi * tq, tq)
            p0 = pl.multiple_of(jnp.maximum(qi - 1, 0) * tq, tq)
            k_d, k_p = k_ref[pl.ds(d0, tq), :], k_ref[pl.ds(p0, tq), :]
            vh_d = _split_heads(v_ref[pl.ds(d0, tq), :], lo)
            vh_p = _split_heads(v_ref[pl.ds(p0, tq), :], lo)
            z_d = {ch: lax.dot_general(qc[ch], k_d, NT, preferred_element_type=F32) for ch in chains}
            z_p = {ch: lax.dot_general(qc[ch], k_p, NT, preferred_element_type=F32) for ch in chains}
            sp_d = {ch: _softplus(z_d[ch], causal[ch[0]], True) for ch in chains}
            sp_raw = {ch: _softplus(z_p[ch], None, False) for ch in chains}
            sp_p = {ch: jnp.where(prev_ok, sp_raw[ch], 0.0) for ch in chains}
            later_d = {ch: mm(sp_d[ch], tri_gt) for ch in chains}
            later_p = {ch: mm(sp_p[ch], tri_gt) for ch in chains}
            c_d = {ch: rowsum(sp_d[ch]) for ch in chains}
            a_d = {ch: jnp.where(causal[ch[0]], jnp.exp((z_d[ch] - sp_d[ch]) - later_d[ch]), 0.0) for ch in chains}
            a_p = {ch: jnp.where(prev_ok, jnp.exp((z_p[ch] - sp_raw[ch]) - (c_d[ch] + later_p[ch])), 0.0)
                   for ch in chains}
            pv = {ch: mm(a_d[ch], vh_d[ch[1]]) + mm(a_p[ch], vh_p[ch[1]]) for ch in chains}
            return tuple((pv[(r, 0)] + pv[(r, 1)],
                          c_d[(r, 0)] + rowsum(sp_p[(r, 0)]), c_d[(r, 1)] + rowsum(sp_p[(r, 1)]))
                         for r in range(n_rc))

        def least(carry):
            m = jnp.minimum(carry[0][1], carry[0][2])
            for r in range(1, n_rc):
                m = jnp.minimum(m, jnp.minimum(carry[r][1], carry[r][2]))
            return jnp.min(m)

        carry = first_two(qi > 0)

        def go_on(st):
            return jnp.logical_and(st[0] < qi - 1, st[1] < ATTN_DONE)

        def step(st):
            new = block(qi - 2 - st[0], st[2])
            return st[0] + 1, least(new), new

        walked, _, carry = lax.while_loop(go_on, step, (jnp.int32(0), least(carry), carry))
        for r in range(n_rc):
            o_ref[r * rc:(r + 1) * rc, :] = carry[r][0].astype(BF16)
            lsum_ref[r * rc:(r + 1) * rc, :] = jnp.where(lo, carry[r][1], carry[r][2])
        nblk_ref[hp, qi] = walked.astype(F32)

    blk = pl.BlockSpec((tq, LANES), lambda hp, qi: (qi, hp))
    o512 = jax.ShapeDtypeStruct((s, D_SB), F32)
    return _hosted_call(
        body, comm, name=name, grid=(4, nq),
        out_shape=(jax.ShapeDtypeStruct((s, D_SB), BF16), o512, jax.ShapeDtypeStruct((4, nq), F32)),
        in_specs=[blk, pl.BlockSpec((s, LANES), lambda hp, qi: (0, 4 + hp)),
                  pl.BlockSpec((s, LANES), lambda hp, qi: (0, 8 + hp))],
        out_specs=(blk, blk, pl.BlockSpec(memory_space=pltpu.SMEM)),
        args=(qkv, qkv, qkv), sem=("arbitrary", "arbitrary"))


HALO = 16


def _conv_taps(cc_ref, ch_ref, ccp_ref, chp_ref, halo_ref, first):
    u = cc_ref[...].astype(F32) * ch_ref[...].astype(F32)
    halo_ref[...] = ccp_ref[...].astype(F32) * chp_ref[...].astype(F32) * jnp.where(first, 0.0, 1.0)
    p6 = halo_ref[HALO - 2:HALO - 1, :]
    p7 = halo_ref[HALO - 1:HALO, :]
    rowi = lax.broadcasted_iota(jnp.int32, u.shape, 0)
    u1 = jnp.where(rowi == 0, p7, pltpu.roll(u, 1, 0))
    u2 = jnp.where(rowi == 0, p6, jnp.where(rowi == 1, p7, pltpu.roll(u, 2, 0)))
    return u, u1, u2


def _fwd_mid(x, pc, az, ya, p4, layer, cw, cb, bg, wout_full, pg, wpg_full, bpg, wpe_full, name, comm=None,
             head=None):
    s = x.shape[0]
    ts = min(ROW_TILE, s)
    blk_h = ts // HALO

    n_in = 18 + (2 if head else 0)

    def body(*refs):
        (x_ref, cb_ref_, cc_ref, ch_ref, cz_ref, ccp_ref, chp_ref, az_ref, ya_ref, p_ref,
         cw_ref, cbias_ref, bg_ref, wout_ref, pg_ref, wpg_ref, bpg_ref, wpe_ref) = refs[:18]
        x2_ref, x3_ref, gated_ref, h2_ref, gate_ref, e_ref = refs[n_in:n_in + 6]
        halo_ref = refs[-1]
        i = pl.program_id(0)
        lane = lax.broadcasted_iota(jnp.int32, (1, LANES), 1)
        lo = lane < HEAD_DIM
        u, u1, u2 = _conv_taps(cc_ref, ch_ref, ccp_ref, chp_ref, halo_ref, i == 0)
        conv = cbias_ref[...] + cw_ref[0:1, :] * u2 + cw_ref[1:2, :] * u1 + cw_ref[2:3, :] * u
        yc = cb_ref_[...].astype(F32) * conv
        for sl in range(8):
            cols = slice(LANES * (sl % 4), LANES * (sl % 4 + 1))
            y = yc[:, cols] if sl < 4 else ya_ref[:, cols].astype(F32)
            zc = (cz_ref[:, cols] if sl < 4 else az_ref[:, cols]).astype(F32)
            rg = lax.rsqrt(_group_bcast_sum(y * y, lo) * (1.0 / HEAD_DIM) + EPS)
            yn = y * rg * bg_ref[:, LANES * sl:LANES * (sl + 1)]
            gated_ref[:, LANES * sl:LANES * (sl + 1)] = (yn * (zc * _sigmoid(zc))).astype(BF16)
        x2 = x_ref[...] + jnp.dot(gated_ref[...], wout_ref[...], preferred_element_type=F32)
        x2_ref[...] = x2
        r2 = lax.rsqrt(jnp.mean(x2 * x2, axis=-1, keepdims=True) + EPS)
        h2 = (x2 * r2 * pg_ref[...]).astype(BF16)
        h2_ref[...] = h2
        gate = _sigmoid(jnp.dot(h2, wpg_ref[...], preferred_element_type=F32) + bpg_ref[...])
        gate_ref[...] = gate.astype(BF16)
        e = jnp.dot(p_ref[...].astype(BF16), wpe_ref[...], preferred_element_type=F32)
        e_ref[...] = e.astype(BF16)
        x3 = x2 + gate * e
        if not head:
            x3_ref[...] = x3
            return
        t_ref, fg_ref = refs[18:20]
        loss_ref, dfg_ref = refs[n_in + 6:n_in + 8]
        dx, loss, dfg = _loss_math(x3, t_ref[...], fg_ref[...])

        @pl.when(i == 0)
        def _():
            loss_ref[...] = jnp.zeros_like(loss_ref)
            dfg_ref[...] = jnp.zeros_like(dfg_ref)

        x3_ref[...] = dx
        loss_ref[...] += loss
        dfg_ref[...] += dfg

    row = lambda width, cb_=0: pl.BlockSpec((ts, width), lambda i: (i, cb_))
    prev = lambda cb_: pl.BlockSpec((HALO, 512), lambda i: (jnp.maximum(i * blk_h - 1, 0), cb_))
    vec = lambda width: pl.BlockSpec((1, width), lambda i: (0, 0))
    lvec = lambda width: _layer_rows(layer, 1, width)
    wspec = lambda r_, c_: pl.BlockSpec((r_, c_), lambda i: (0, 0))
    f32o = jax.ShapeDtypeStruct((s, D_MODEL), F32)
    bfo = jax.ShapeDtypeStruct((s, D_MODEL), BF16)
    head_in = [row(D_MODEL), vec(D_MODEL)] if head else []
    head_out = [jax.ShapeDtypeStruct((1, LANES), F32), jax.ShapeDtypeStruct((1, D_MODEL), F32)] if head else []
    return _hosted_call(
        body, comm, name=name, grid=(s // ts,),
        out_shape=(f32o, f32o, bfo, bfo, bfo, bfo, *head_out),
        scratch_shapes=[pltpu.VMEM((HALO, 512), F32)],
        in_specs=[row(D_MODEL), row(512, 0), row(512, 1), row(512, 2), row(512, 3), prev(1), prev(2),
                  row(512), row(512),
                  pl.BlockSpec((None, None, ts, PLE_DIM), lambda i: (layer, 0, i, 0)),
                  _layer_rows(layer, 3, 512), lvec(512), lvec(D_MODEL),
                  wspec(D_MODEL, D_MODEL), lvec(D_MODEL), wspec(D_MODEL, D_MODEL), lvec(D_MODEL),
                  wspec(PLE_DIM, D_MODEL), *head_in],
        out_specs=(*[row(D_MODEL)] * 6, *([vec(LANES), vec(D_MODEL)] if head else [])),
        args=(x, pc, pc, pc, pc, pc, pc, az, ya, p4, cw, cb, bg, wout_full, pg, wpg_full, bpg, wpe_full,
              *(head or ())),
        sem=("arbitrary",) if head else ("parallel",))


def _loss_math(x, target, g):
    r = lax.rsqrt(jnp.mean(x * x, axis=-1, keepdims=True) + EPS)
    xn = x * r
    err = xn * g - target
    per_row = jnp.sum(err * err, axis=-1, keepdims=True)
    loss = jnp.sum(per_row, axis=0, keepdims=True) * (0.5 / D_MODEL)
    dy = err * (1.0 / D_MODEL)
    dg = jnp.sum(dy * xn, axis=0, keepdims=True)
    dxn = dy * g
    return r * (dxn - xn * jnp.mean(dxn * xn, axis=-1, keepdims=True)), loss, dg


def _bwd_mid(dx3, x2, gate, e, pc, az, ya, gated, h2, p4, layer, cw, cb, bg, pg, wpg_full, wout_full, name,
             comm=None):
    s = x2.shape[0]
    ts = min(ROW_TILE, s)
    blk_h = ts // HALO

    def body(dx3_ref, x2_ref, gate_ref, e_ref, cb_ref_, cc_ref, ch_ref, cz_ref, ccp_ref, chp_ref, az_ref, ya_ref,
             gated_ref, h2_ref, p_ref, cw_ref, cbias_ref, bg_ref, pg_ref, wpg_ref, wout_ref,
             dx2_ref, dya_ref, dmisc_ref, dconv_ref, dwout_ref, dwpg_ref, dwpe_ref,
             dbpg_ref, dpg_ref, dbg_ref, dcbias_ref, dcw_ref,
             dgated_ref, halo_ref, acc_out, acc_pg, acc_pe):
        i = pl.program_id(0)

        @pl.when(i == 0)
        def _():
            for ref in (dbpg_ref, dpg_ref, dbg_ref, dcbias_ref, dcw_ref, acc_out, acc_pg, acc_pe):
                ref[...] = jnp.zeros_like(ref)

        lane = lax.broadcasted_iota(jnp.int32, (1, LANES), 1)
        lo = lane < HEAD_DIM
        dx3 = dx3_ref[...]
        gate = gate_ref[...].astype(F32)
        de_b = (dx3 * gate).astype(BF16)
        dgpre = dx3 * e_ref[...].astype(F32) * gate * (1.0 - gate)
        dbpg_ref[...] += jnp.sum(dgpre, axis=0, keepdims=True)
        dgpre_b = dgpre.astype(BF16)
        dh2 = lax.dot_general(dgpre_b, wpg_ref[...], NT, preferred_element_type=F32)
        acc_pe[...] += lax.dot_general(p_ref[...].astype(BF16), de_b, TN, preferred_element_type=F32)
        acc_pg[...] += lax.dot_general(h2_ref[...], dgpre_b, TN, preferred_element_type=F32)

        u, u1, u2 = _conv_taps(cc_ref, ch_ref, ccp_ref, chp_ref, halo_ref, i == 0)
        conv = cbias_ref[...] + cw_ref[0:1, :] * u2 + cw_ref[1:2, :] * u1 + cw_ref[2:3, :] * u
        c_b = cb_ref_[...].astype(F32)
        yc = c_b * conv
        fwd = []
        for sl in range(8):
            cols = slice(LANES * (sl % 4), LANES * (sl % 4 + 1))
            y = yc[:, cols] if sl < 4 else ya_ref[:, cols].astype(F32)
            zc = (cz_ref[:, cols] if sl < 4 else az_ref[:, cols]).astype(F32)
            rg = lax.rsqrt(_group_bcast_sum(y * y, lo) * (1.0 / HEAD_DIM) + EPS)
            sig = _sigmoid(zc)
            fwd.append((rg, y * rg, zc * sig, sig * (1.0 + zc * (1.0 - sig))))

        x2 = x2_ref[...]
        r2 = lax.rsqrt(jnp.mean(x2 * x2, axis=-1, keepdims=True) + EPS)
        xn2 = x2 * r2
        dpg_ref[...] += jnp.sum(dh2 * xn2, axis=0, keepdims=True)
        dxn = dh2 * pg_ref[...]
        dx2 = dx3 + r2 * (dxn - xn2 * jnp.mean(dxn * xn2, axis=-1, keepdims=True))
        dx2_ref[...] = dx2
        dx2_b = dx2.astype(BF16)
        dgated_ref[...] = lax.dot_general(dx2_b, wout_ref[...], NT, preferred_element_type=F32)
        acc_out[...] += lax.dot_general(gated_ref[...], dx2_b, TN, preferred_element_type=F32)

        for sl in range(8):
            cols = slice(LANES * (sl % 4), LANES * (sl % 4 + 1))
            wide = slice(LANES * sl, LANES * (sl + 1))
            rg, yhat, silu, dsilu = fwd[sl]
            bgs = bg_ref[:, wide]
            dgt = dgated_ref[:, wide]
            dyn = dgt * silu
            dzc = dgt * (yhat * bgs) * dsilu
            dbg_ref[:, wide] += jnp.sum(dyn * yhat, axis=0, keepdims=True)
            dyh = dyn * bgs
            dy = rg * (dyh - yhat * (_group_bcast_sum(dyh * yhat, lo) * (1.0 / HEAD_DIM)))
            if sl < 4:
                dconv = dy * c_b[:, cols]
                dmisc_ref[:, cols] = (dy * conv[:, cols]).astype(BF16)
                dmisc_ref[:, 512 + LANES * sl:512 + LANES * (sl + 1)] = dzc.astype(BF16)
                dconv_ref[:, cols] = dconv
                dcbias_ref[:, cols] += jnp.sum(dconv, axis=0, keepdims=True)
                dcw_ref[0:1, cols] += jnp.sum(dconv * u2[:, cols], axis=0, keepdims=True)
                dcw_ref[1:2, cols] += jnp.sum(dconv * u1[:, cols], axis=0, keepdims=True)
                dcw_ref[2:3, cols] += jnp.sum(dconv * u[:, cols], axis=0, keepdims=True)
            else:
                dya_ref[:, cols] = dy.astype(BF16)
                dmisc_ref[:, 1024 + LANES * (sl - 4):1024 + LANES * (sl - 3)] = dzc.astype(BF16)

        @pl.when(i == pl.num_programs(0) - 1)
        def _():
            dwout_ref[...] = acc_out[...].astype(BF16)
            dwpg_ref[...] = acc_pg[...].astype(BF16)
            dwpe_ref[...] = acc_pe[...].astype(BF16)

    row = lambda width, cb_=0: pl.BlockSpec((ts, width), lambda i: (i, cb_))
    prev = lambda cb_: pl.BlockSpec((HALO, 512), lambda i: (jnp.maximum(i * blk_h - 1, 0), cb_))
    vec = lambda width: pl.BlockSpec((1, width), lambda i: (0, 0))
    lvec = lambda width: _layer_rows(layer, 1, width)
    wspec = lambda r_, c_: pl.BlockSpec((r_, c_), lambda i: (0, 0))
    vo = lambda width: jax.ShapeDtypeStruct((1, width), F32)
    sq = jax.ShapeDtypeStruct((D_MODEL, D_MODEL), BF16)
    return _hosted_call(
        body, comm, name=name, grid=(s // ts,), sem=("arbitrary",),
        args=(dx3, x2, gate, e, pc, pc, pc, pc, pc, pc, az, ya, gated, h2, p4, cw, cb, bg, pg, wpg_full, wout_full),
        out_shape=(jax.ShapeDtypeStruct((s, D_MODEL), F32), jax.ShapeDtypeStruct((s, 512), BF16),
                   jax.ShapeDtypeStruct((s, 1536), BF16), jax.ShapeDtypeStruct((s, 512), F32),
                   sq, sq, jax.ShapeDtypeStruct((PLE_DIM, D_MODEL), BF16),
                   vo(D_MODEL), vo(D_MODEL), vo(D_MODEL), vo(512), jax.ShapeDtypeStruct((SUBLANES, 512), F32)),
        in_specs=[row(D_MODEL), row(D_MODEL), row(D_MODEL), row(D_MODEL),
                  row(512, 0), row(512, 1), row(512, 2), row(512, 3), prev(1), prev(2), row(512), row(512),
                  row(D_MODEL), row(D_MODEL),
                  pl.BlockSpec((None, None, ts, PLE_DIM), lambda i: (layer, 0, i, 0)),
                  _layer_rows(layer, 3, 512), lvec(512), lvec(D_MODEL), lvec(D_MODEL),
                  wspec(D_MODEL, D_MODEL), wspec(D_MODEL, D_MODEL)],
        out_specs=(row(D_MODEL), row(512), row(1536), row(512),
                   wspec(D_MODEL, D_MODEL), wspec(D_MODEL, D_MODEL), wspec(PLE_DIM, D_MODEL),
                   vec(D_MODEL), vec(D_MODEL), vec(D_MODEL), vec(512),
                   pl.BlockSpec((SUBLANES, 512), lambda i: (0, 0))),
        scratch_shapes=[pltpu.VMEM((ts, D_MODEL), F32), pltpu.VMEM((HALO, 512), F32),
                        pltpu.VMEM((D_MODEL, D_MODEL), F32), pltpu.VMEM((D_MODEL, D_MODEL), F32),
                        pltpu.VMEM((PLE_DIM, D_MODEL), F32)])


def _attn_bwd(qkv, lsum, nblk, dya, name, comm=None):
    s = qkv.shape[0]
    tq = min(ATTN_TILE, s)
    nq = s // tq
    rc = min(ATTN_ROWS, tq)
    n_rc = tq // rc
    chains = [(r, hh) for r in range(n_rc) for hh in range(2)]

    def body(nblk_ref, q_ref, k_ref, v_ref, lsum_ref, do_ref, dq_ref, dk_ref, dv_ref, dk_acc, dv_acc):
        hp, qi = pl.program_id(0), pl.program_id(1)

        @pl.when(qi == 0)
        def _():
            dk_acc[...] = jnp.zeros_like(dk_acc)
            dv_acc[...] = jnp.zeros_like(dv_acc)

        lo, causal, tri_gt, tri_le = _attn_pieces(tq, rc)
        lane = lax.broadcasted_iota(jnp.int32, (1, LANES), 1)
        qh = _split_heads(q_ref[...], lo)
        doh = _split_heads(do_ref[...].astype(BF16), lo)
        lt = lsum_ref[...]
        ltot_h = (jnp.sum(jnp.where(lane == 0, lt, 0.0), axis=-1, keepdims=True),
                  jnp.sum(jnp.where(lane == HEAD_DIM, lt, 0.0), axis=-1, keepdims=True))
        rows = lambda a_, r: a_[r * rc:(r + 1) * rc]
        qc = {(r, hh): rows(qh[hh], r) for r, hh in chains}
        doc = {(r, hh): rows(doh[hh], r) for r, hh in chains}
        ltot = {(r, hh): rows(ltot_h[hh], r) for r, hh in chains}

        mm = lambda a_, b_: jnp.dot(a_.astype(BF16), b_, preferred_element_type=F32)
        mm_nt = lambda a_, b_: lax.dot_general(a_, b_, NT, preferred_element_type=F32)
        mm_tn = lambda a_, b_: lax.dot_general(a_.astype(BF16), b_, TN, preferred_element_type=F32)
        rowsum = lambda a_: jnp.sum(a_, axis=-1, keepdims=True)

        def block(kb, carry, diag=False):
            start = pl.multiple_of(kb * tq, tq)
            k = k_ref[pl.ds(start, tq), :]
            v = v_ref[pl.ds(start, tq), :]
            kh = _split_heads(k, lo)
            keep = (lambda ch, a_: jnp.where(causal[ch[0]], a_, 0.0)) if diag else (lambda ch, a_: a_)
            z = {ch: mm_nt(qc[ch], k) for ch in chains}
            da = {ch: mm_nt(doc[ch], v) for ch in chains}
            sp = {ch: _softplus(z[ch], causal[ch[0]], diag) for ch in chains}
            later = {ch: mm(sp[ch], tri_gt) for ch in chains}
            walked = {ch: carry[ch[0]][1 + ch[1]] + rowsum(sp[ch]) for ch in chains}
            a = {ch: keep(ch, jnp.exp((z[ch] - sp[ch]) - ((ltot[ch] - walked[ch]) + later[ch]))) for ch in chains}
            g = {ch: a[ch] * da[ch] for ch in chains}
            upto = {ch: mm(g[ch], tri_le) for ch in chains}
            dz = {ch: keep(ch, g[ch] - jnp.exp(z[ch] - sp[ch]) * (carry[ch[0]][3 + ch[1]] + upto[ch])).astype(BF16)
                  for ch in chains}
            dqc = {ch: mm(dz[ch], kh[ch[1]]) for ch in chains}
            dkc = [mm_tn(dz[ch], qc[ch]) for ch in chains]
            dvc = [mm_tn(a[ch], doc[ch]) for ch in chains]
            dk_acc[pl.ds(start, tq), :] += sum(dkc[1:], dkc[0])
            dv_acc[pl.ds(start, tq), :] += sum(dvc[1:], dvc[0])
            return tuple((carry[r][0] + dqc[(r, 0)] + dqc[(r, 1)], walked[(r, 0)], walked[(r, 1)],
                          carry[r][3] + rowsum(g[(r, 0)]), carry[r][4] + rowsum(g[(r, 1)])) for r in range(n_rc))

        zc = jnp.zeros((rc, 1), F32)
        carry = tuple((jnp.zeros((rc, LANES), F32), zc, zc, zc, zc) for _ in range(n_rc))
        near = jnp.maximum(qi - 1, 0)
        first = near - jnp.clip(nblk_ref[hp, qi].astype(jnp.int32), 0, near)
        carry = lax.fori_loop(first, qi, block, carry)
        carry = block(qi, carry, True)
        for r in range(n_rc):
            dq_ref[r * rc:(r + 1) * rc, :] = (carry[r][0] * 0.125).astype(BF16)

        @pl.when(qi == pl.num_programs(1) - 1)
        def _():
            dk_ref[...] = dk_acc[...].astype(BF16)
            dv_ref[...] = dv_acc[...].astype(BF16)

    blk = pl.BlockSpec((tq, LANES), lambda hp, qi: (qi, hp))
    col = pl.BlockSpec((s, LANES), lambda hp, qi: (0, hp))
    o512 = jax.ShapeDtypeStruct((s, D_SB), BF16)
    return _hosted_call(
        body, comm, name=name, grid=(4, nq),
        out_shape=(o512, o512, o512),
        in_specs=[pl.BlockSpec(memory_space=pltpu.SMEM), blk,
                  pl.BlockSpec((s, LANES), lambda hp, qi: (0, 4 + hp)),
                  pl.BlockSpec((s, LANES), lambda hp, qi: (0, 8 + hp)), blk, blk],
        out_specs=(blk, col, col),
        scratch_shapes=[pltpu.VMEM((s, LANES), F32), pltpu.VMEM((s, LANES), F32)],
        args=(nblk, qkv, qkv, qkv, lsum, dya), sem=("parallel", "arbitrary"))


def _bwd_dproj(dmisc, dconv, pc, dq, dk, dv, x, dx2, g, cw, layer, win_full, name, comm=None):
    s = x.shape[0]
    ts = min(ROW_TILE, s)
    blk8 = ts // SUBLANES
    last8 = s // SUBLANES - 1

    def body(dcb_ref, dcz_ref, daz_ref, dconv_ref, nxt_ref, cc_ref, ch_ref, dq_ref, dk_ref, dv_ref,
             x_ref, dx2_ref, g_ref, cw_ref, w_ref, dproj_ref, dx_ref, dg_ref):
        i = pl.program_id(0)

        @pl.when(i == 0)
        def _():
            dg_ref[...] = jnp.zeros_like(dg_ref)

        keep = jnp.where(i == pl.num_programs(0) - 1, 0.0, 1.0)
        dc = dconv_ref[...]
        n0 = nxt_ref[0:1, :] * keep
        n1 = nxt_ref[1:2, :] * keep
        rowi = lax.broadcasted_iota(jnp.int32, dc.shape, 0)
        dc1 = jnp.where(rowi == ts - 1, n0, pltpu.roll(dc, ts - 1, 0))
        dc2 = jnp.where(rowi == ts - 2, n0, jnp.where(rowi == ts - 1, n1, pltpu.roll(dc, ts - 2, 0)))
        du = cw_ref[2:3, :] * dc + cw_ref[1:2, :] * dc1 + cw_ref[0:1, :] * dc2
        dproj_ref[:, 0:512] = dcb_ref[...]
        dproj_ref[:, 512:1024] = (du * ch_ref[...].astype(F32)).astype(BF16)
        dproj_ref[:, 1024:1536] = (du * cc_ref[...].astype(F32)).astype(BF16)
        dproj_ref[:, 1536:2048] = dcz_ref[...]
        dproj_ref[:, 2048:2560] = dq_ref[...]
        dproj_ref[:, 2560:3072] = dk_ref[...]
        dproj_ref[:, 3072:3584] = dv_ref[...]
        dproj_ref[:, 3584:4096] = daz_ref[...]
        dh = lax.dot_general(dproj_ref[...], w_ref[...], NT, preferred_element_type=F32)
        x = x_ref[...]
        r = lax.rsqrt(jnp.mean(x * x, axis=-1, keepdims=True) + EPS)
        xn = x * r
        dg_ref[...] += jnp.sum(dh * xn, axis=0, keepdims=True)
        dxn = dh * g_ref[...]
        dx_ref[...] = dx2_ref[...] + r * (dxn - xn * jnp.mean(dxn * xn, axis=-1, keepdims=True))

    row = lambda width, cb_=0: pl.BlockSpec((ts, width), lambda i: (i, cb_))
    nxt = pl.BlockSpec((SUBLANES, 512), lambda i: (jnp.minimum((i + 1) * blk8, last8), 0))
    vec = lambda width: pl.BlockSpec((1, width), lambda i: (0, 0))
    lvec = lambda width: _layer_rows(layer, 1, width)
    return _hosted_call(
        body, comm, name=name, grid=(s // ts,),
        out_shape=(jax.ShapeDtypeStruct((s, N_IN), BF16), jax.ShapeDtypeStruct((s, D_MODEL), F32),
                   jax.ShapeDtypeStruct((1, D_MODEL), F32)),
        in_specs=[row(512, 0), row(512, 1), row(512, 2), row(512), nxt, row(512, 1), row(512, 2),
                  row(512), row(512), row(512), row(D_MODEL), row(D_MODEL), lvec(D_MODEL),
                  _layer_rows(layer, 3, 512),
                  pl.BlockSpec((D_MODEL, N_IN), lambda i: (0, 0))],
        out_specs=(row(N_IN), row(D_MODEL), vec(D_MODEL)),
        args=(dmisc, dmisc, dmisc, dconv, dconv, pc, pc, dq, dk, dv, x, dx2, g, cw, win_full),
        sem=("arbitrary",))


def _atb(a, b, name, a_cols=None, comm=None):
    s, n = b.shape
    m, a_blk = (a.shape[-1], 0) if a_cols is None else a_cols
    ts = min(512, s)
    tn = min(2048, n)
    a_spec = pl.BlockSpec((ts, m), lambda j, i: (i, a_blk))

    def body(a_ref, b_ref, o_ref, acc_ref):
        i = pl.program_id(1)

        @pl.when(i == 0)
        def _():
            acc_ref[...] = jnp.zeros_like(acc_ref)

        acc_ref[...] += lax.dot_general(a_ref[...].astype(BF16), b_ref[...], TN, preferred_element_type=F32)

        @pl.when(i == pl.num_programs(1) - 1)
        def _():
            o_ref[...] = acc_ref[...].astype(BF16)

    (out,), got = _hosted_call(
        body, comm, name=name, grid=(n // tn, s // ts),
        out_shape=(jax.ShapeDtypeStruct((m, n), BF16),),
        in_specs=[a_spec, pl.BlockSpec((ts, tn), lambda j, i: (i, j))],
        out_specs=(pl.BlockSpec((m, tn), lambda j, i: (0, j)),),
        scratch_shapes=[pltpu.VMEM((m, tn), F32)],
        args=(a, b), sem=("parallel", "arbitrary"))
    return out, got


def _adamw_math(w, g, m, v):
    m2 = ADAM_B1 * m + (1.0 - ADAM_B1) * g
    v2 = ADAM_B2 * v + (1.0 - ADAM_B2) * (g * g)
    m_hat = m2 / (1.0 - ADAM_B1 ** ADAM_STEP)
    v_hat = v2 / (1.0 - ADAM_B2 ** ADAM_STEP)
    delta = -ADAM_LR * (m_hat / (jnp.sqrt(v_hat) + ADAM_EPS) + ADAM_WD * w)
    return delta, m2, v2


def _adamw_sum8(pieces, w, m, v, name):
    _, rows, cols = w.shape
    tr = min([rows, 256] + [pc_[0].shape[1] for pc_ in pieces])
    n_tiles = rows // tr
    n_p = len(pieces)
    spans = [(layer, row0 // tr, arr.shape[1] // tr) for arr, layer, row0 in pieces]

    def body(*refs):
        p_refs = refs[:n_p]
        w_ref, m_ref, v_ref, g_ref, d_ref, m2_ref, v2_ref = refs[n_p:]
        l, i = pl.program_id(0), pl.program_id(1)

        def run(p_ref):
            g = p_ref[0].astype(F32)
            for d in range(1, N_DEV):
                g = g + p_ref[d].astype(F32)
            g_ref[...] = g
            d_ref[...], m2_ref[...], v2_ref[...] = _adamw_math(w_ref[...], g, m_ref[...], v_ref[...])

        for p_ref, (layer, t0, nt) in zip(p_refs, spans):
            mine = jnp.logical_and(l == layer, jnp.logical_and(i >= t0, i < t0 + nt))
            pl.when(mine)(lambda p_ref=p_ref: run(p_ref))

    def piece_spec(layer, t0, nt):
        return pl.BlockSpec((N_DEV, tr, cols),
                            lambda l, i: (0, jnp.clip(jnp.where(l == layer, i - t0, jnp.where(l < layer, 0, nt - 1)),
                                                      0, nt - 1), 0))

    tile = pl.BlockSpec((None, tr, cols), lambda l, i: (l, i, 0))
    o = jax.ShapeDtypeStruct((DEPTH, rows, cols), F32)
    return _call(
        body, name=name, grid=(DEPTH, n_tiles),
        out_shape=(o, o, o, o),
        in_specs=[*[piece_spec(*sp) for sp in spans], tile, tile, tile],
        out_specs=(tile, tile, tile, tile),
        compiler_params=_params(("arbitrary", "arbitrary"), VMEM_LIMIT),
    )(*[pc_[0] for pc_ in pieces], w, m, v)


def _small_update(blk, layered, final, conv, loss_parts):
    n_l = len(layered)
    ins = [a for item in layered for a in item] + list(final) + list(conv) + [loss_parts]
    shapes = [item[2].shape for item in layered] + [final[1].shape, conv[2].shape]
    out_shape = [jax.ShapeDtypeStruct(sh, F32) for sh in shapes for _ in range(4)]
    out_shape.append(jax.ShapeDtypeStruct((1, LANES), F32))

    def body(*refs):
        blk_ref, refs = refs[0], refs[1:]
        in_refs, out_refs, pick_ref = refs[:len(ins)], refs[len(ins):-1], refs[-1]

        def total(ref):
            g = ref[0]
            for d in range(1, N_DEV):
                g = g + ref[d]
            return g

        def update(k, at, g, w_ref, m_ref, v_ref):
            g_ref, d_ref, m2_ref, v2_ref = out_refs[4 * k:4 * k + 4]
            g_ref[at] = g
            d_ref[at], m2_ref[at], v2_ref[at] = _adamw_math(w_ref[at], g, m_ref[at], v_ref[at])

        for k in range(n_l):
            p0, p1, w_ref, m_ref, v_ref = in_refs[5 * k:5 * k + 5]
            for layer, parts in enumerate((p0, p1)):
                update(k, pl.ds(layer, 1), total(parts), w_ref, m_ref, v_ref)
        pf, w_ref, m_ref, v_ref = in_refs[5 * n_l:5 * n_l + 4]
        update(n_l, pl.ds(0, 1), total(pf), w_ref, m_ref, v_ref)
        c0, c1, w_ref, m_ref, v_ref = in_refs[5 * n_l + 4:5 * n_l + 9]
        for layer, parts in enumerate((c0, c1)):
            g8 = total(parts)
            mine = jnp.zeros((SUBLANES, HEAD_DIM), F32)
            for j in range(N_DEV):
                mine = mine + jnp.where(blk_ref[0] == j, g8[:, HEAD_DIM * j:HEAD_DIM * (j + 1)], 0.0)
            pick_ref[...] = mine
            update(n_l + 1, layer, pick_ref[0:3, :], w_ref, m_ref, v_ref)
        out_refs[-1][...] = total(in_refs[-1])

    whole = lambda shape: pl.BlockSpec(shape, lambda: (0,) * len(shape))
    outs = _call(
        body, name="adamw_small",
        out_shape=tuple(out_shape),
        in_specs=[pl.BlockSpec(memory_space=pltpu.SMEM)] + [whole(a.shape) for a in ins],
        out_specs=tuple(whole(o.shape) for o in out_shape),
        scratch_shapes=[pltpu.VMEM((SUBLANES, HEAD_DIM), F32)],
    )(blk, *ins)
    return [outs[4 * k:4 * k + 4] for k in range(n_l + 2)], outs[-1]


def kernel(x, p, norm_g, w_in, conv_w, conv_b, branch_g, w_out, ple_norm_g, w_pg, b_pg, w_pe, final_g, loss_target, m_norm_g, m_w_in, m_conv_w, m_conv_b, m_branch_g, m_w_out, m_ple_norm_g, m_w_pg, m_b_pg, m_w_pe, m_final_g, v_norm_g, v_w_in, v_conv_w, v_conv_b, v_branch_g, v_w_out, v_ple_norm_g, v_w_pg, v_b_pg, v_w_pe, v_final_g):
    s = x.shape[1]
    x0 = x.reshape(s, D_MODEL)
    target = loss_target.reshape(s, D_MODEL)
    me_blk = _my_block()

    win_s, wout_s, wpg_s, wpe_s = _cast_bf16(
        [w_in.reshape(DEPTH * D_MODEL, 512), w_out.reshape(DEPTH * 128, D_MODEL),
         w_pg.reshape(DEPTH * 128, D_MODEL), w_pe.reshape(DEPTH * PLE_DIM, 128)], "cast_weights")
    win_s, wout_s = win_s.reshape(DEPTH, D_MODEL, 512), wout_s.reshape(DEPTH, 128, D_MODEL)
    wpg_s, wpe_s = wpg_s.reshape(DEPTH, 128, D_MODEL), wpe_s.reshape(DEPTH, PLE_DIM, 128)
    cw_s = jnp.zeros((SUBLANES, LANES), F32).at[:DEPTH * 3, :HEAD_DIM].set(conv_w.reshape(DEPTH * 3, HEAD_DIM))
    bf = lambda r_, c_: jax.ShapeDtypeStruct((r_, c_), BF16)
    w_items = lambda l: [(wout_s[l], bf(D_MODEL, D_MODEL), "rows128"), (wpg_s[l], bf(D_MODEL, D_MODEL), "rows128"),
                         (wpe_s[l], bf(PLE_DIM, D_MODEL), "cols128")]
    win_f = [None] * DEPTH
    win_f[0], cw_all = _comm_call(_gather_comm([
        (win_s[0], bf(D_MODEL, N_IN), "cols512"),
        (cw_s, jax.ShapeDtypeStruct((N_DEV, SUBLANES, LANES), F32), "slot")]), "gather_w_in_0")
    cw_full = jnp.transpose(cw_all[:, :DEPTH * 3, :HEAD_DIM].reshape(N_DEV, DEPTH, 3, HEAD_DIM), (1, 2, 0, 3))
    cw_full = cw_full.reshape(DEPTH, 3, D_CONV)
    gather_rest_0 = _gather_comm(w_items(0))
    gather_win_1 = _gather_comm([(win_s[1], bf(D_MODEL, N_IN), "cols512")])
    gather_rest_1 = _gather_comm(w_items(1))

    norm3, convb3, branch3, ple3, bpg3 = [a.reshape(DEPTH, 1, -1) for a in (norm_g, conv_b, branch_g, ple_norm_g, b_pg)]

    saved = []
    xl = x0
    wout_f, wpg_f, wpe_f = [None] * DEPTH, [None] * DEPTH, [None] * DEPTH
    for l in range(DEPTH):
        (h, pc, qkv, az), got = _fwd_in(xl, norm3, l, win_f[l], f"fwd_in_{l}",
                                        comm=gather_rest_0 if l == 0 else None)
        if l == 0:
            wout_f[0], wpg_f[0], wpe_f[0] = got
        (ya, lsum, nblk), got = _attn_fwd(qkv, f"attn_fwd_{l}", comm=gather_win_1 if l == 0 else None)
        if l == 0:
            (win_f[1],) = got
        last = l == DEPTH - 1
        outs, got = _fwd_mid(
            xl, pc, az, ya, p, l, cw_full, convb3, branch3, wout_f[l],
            ple3, wpg_f[l], bpg3, wpe_f[l], f"fwd_mid_{l}",
            comm=gather_rest_1 if l == 0 else None, head=(target, final_g[None, :]) if last else None)
        x2, x3, gated, h2, gate, e = outs[:6]
        if l == 0:
            wout_f[1], wpg_f[1], wpe_f[1] = got
        saved.append(dict(x=xl, h=h, pc=pc, qkv=qkv, az=az, ya=ya, lsum=lsum, nblk=nblk, x2=x2, gated=gated, h2=h2,
                          gate=gate, e=e))
        xl = x3

    dx, (loss_acc, d_final_g) = xl, outs[6:]

    dwin, dwout, dwpg, dwpe = [None] * DEPTH, [None] * DEPTH, [None] * DEPTH, [None] * DEPTH
    small = dict(norm_g=[None] * DEPTH, conv_b=[None] * DEPTH, branch_g=[None] * DEPTH,
                 ple_norm_g=[None] * DEPTH, b_pg=[None] * DEPTH, conv_w=[None] * DEPTH)
    slot = lambda r_, c_: jax.ShapeDtypeStruct((r_, c_), BF16)
    half = D_MODEL // 2
    r_in1, r_out, r_pg, r_pe = None, [None] * DEPTH, [None] * DEPTH, [None] * DEPTH

    def rest_items(l):
        return [(dwout[l], slot(128, D_MODEL), "rows128"), (dwpg[l], slot(128, D_MODEL), "rows128"),
                (dwpe[l], slot(PLE_DIM, 128), "cols128")]

    for l in reversed(range(DEPTH)):
        sv = saved[l]
        ride = _exchange_comm(rest_items(1)) if l == 0 else None
        (dx2, dya, dmisc, dconv, dwout[l], dwpg[l], dwpe[l], d_bpg, d_pg, d_bg, d_cbias, d_cw), got = _bwd_mid(
            dx, sv["x2"], sv["gate"], sv["e"], sv["pc"], sv["az"], sv["ya"], sv["gated"], sv["h2"], p, l,
            cw_full, convb3, branch3, ple3, wpg_f[l], wout_f[l], f"bwd_mid_{l}",
            comm=ride)
        if l == 0:
            r_out[1], r_pg[1], r_pe[1] = got
        ride = _exchange_comm([(dwin[1], slot(D_MODEL, 512), "cols512")] + rest_items(0)) if l == 0 else None
        (dq, dk, dv), got = _attn_bwd(sv["qkv"], sv["lsum"], sv["nblk"], dya, f"attn_bwd_{l}", comm=ride)
        if l == 0:
            r_in1, r_out[0], r_pg[0], r_pe[0] = got
        (dproj, dx, d_ng), _ = _bwd_dproj(dmisc, dconv, sv["pc"], dq, dk, dv, sv["x"], dx2, norm3, cw_full, l,
                                          win_f[l], f"bwd_dproj_{l}")
        if l == 1:
            dwin[1], _ = _atb(sv["h"], dproj, "dw_in_1")
        else:
            dwin_top, _ = _atb(sv["h"], dproj, "dw_in_0_top", a_cols=(half, 0))
            dwin_bot, (r_in0_top,) = _atb(sv["h"], dproj, "dw_in_0_bottom", a_cols=(half, 1),
                                          comm=_exchange_comm([(dwin_top, slot(half, 512), "cols512")]))
        small["norm_g"][l], small["conv_b"][l], small["branch_g"][l] = d_ng, d_cbias, d_bg
        small["ple_norm_g"][l], small["b_pg"][l], small["conv_w"][l] = d_pg, d_bpg, d_cw
    grad_x = dx.reshape(1, s, D_MODEL)

    names = ["norm_g", "conv_b", "branch_g", "ple_norm_g", "b_pg", "conv_w"]
    small_list = [small[n][l] for n in names for l in range(DEPTH)] + [d_final_g, loss_acc]
    got = _comm_call(_exchange_comm(
        [(dwin_bot, slot(half, 512), "cols512")]
        + [(a, jax.ShapeDtypeStruct(a.shape, F32), "slot") for a in small_list]), "exchange_last")
    r_in0_bot, r_small = got[0], got[1:]

    per_layer = lambda r: [(r[0], 0, 0), (r[1], 1, 0)]
    g_win, d_win, m_win, v_win = _adamw_sum8([(r_in0_top, 0, 0), (r_in0_bot, 0, half), (r_in1, 1, 0)],
                                             w_in, m_w_in, v_w_in, "adamw_w_in")
    g_wout, d_wout, m_wout, v_wout = _adamw_sum8(per_layer(r_out), w_out, m_w_out, v_w_out, "adamw_w_out")
    g_wpg, d_wpg, m_wpg, v_wpg = _adamw_sum8(per_layer(r_pg), w_pg, m_w_pg, v_w_pg, "adamw_w_pg")
    g_wpe, d_wpe, m_wpe, v_wpe = _adamw_sum8(per_layer(r_pe), w_pe, m_w_pe, v_w_pe, "adamw_w_pe")

    layered = [(norm_g, m_norm_g, v_norm_g), (conv_b, m_conv_b, v_conv_b), (branch_g, m_branch_g, v_branch_g),
               (ple_norm_g, m_ple_norm_g, v_ple_norm_g), (b_pg, m_b_pg, v_b_pg)]
    row = lambda a: a.reshape(1, -1)
    upd, loss_row = _small_update(
        jnp.reshape(me_blk, (1,)).astype(jnp.int32),
        [(r_small[2 * k], r_small[2 * k + 1], *wmv) for k, wmv in enumerate(layered)],
        (r_small[12], row(final_g), row(m_final_g), row(v_final_g)),
        (r_small[10], r_small[11], conv_w, m_conv_w, v_conv_w), r_small[13])
    loss = loss_row[0, 0]
    upd[5] = [a.reshape(-1) for a in upd[5]]

    big = {1: (g_win, d_win, m_win, v_win), 5: (g_wout, d_wout, m_wout, v_wout), 7: (g_wpg, d_wpg, m_wpg, v_wpg),
           9: (g_wpe, d_wpe, m_wpe, v_wpe)}
    small_at = {0: 0, 2: 6, 3: 1, 4: 2, 6: 3, 8: 4, 10: 5}
    per_kind = [[(big[i] if i in big else upd[small_at[i]])[j] for i in range(11)] for j in range(4)]
    return (loss, grad_x, *per_kind[0], *per_kind[1], *per_kind[2], *per_kind[3])
```

```python
import jax
import jax.numpy as jnp
from jax import lax
from jax.experimental import pallas as pl
from jax.experimental.pallas import tpu as pltpu

F32 = jnp.float32
BF16 = jnp.bfloat16

D_MODEL = 1024
D_CONV = 512
D_SB = 512
N_IN = 4096
HEAD_DIM = 64
PLE_DIM = 256
DEPTH = 2
EPS = 1e-6
ADAM_LR = 0.001
ADAM_B1 = 0.9
ADAM_B2 = 0.999
ADAM_EPS = 1e-08
ADAM_WD = 0.01
ADAM_STEP = 10

LANES = 128
SUBLANES = 8
VMEM_BYTES_V7X = 64 * 1024 * 1024
VMEM_LIMIT = VMEM_BYTES_V7X - 8 * 1024 * 1024

N_DEV = 8
ROW_TILE = 256
ATTN_TILE = 256

NT = (((1,), (1,)), ((), ()))
TN = (((0,), (0,)), ((), ()))


def _call(body, **kw):
    return pl.pallas_call(body, **kw)


def _params(sem=None, vmem=None):
    return pltpu.CompilerParams(dimension_semantics=sem, vmem_limit_bytes=vmem)


def _sigmoid(z):
    return 0.5 * jnp.tanh(0.5 * z) + 0.5


def _group_bcast_sum(a, lo):
    s_lo = jnp.sum(jnp.where(lo, a, 0.0), axis=-1, keepdims=True)
    s_hi = jnp.sum(jnp.where(lo, 0.0, a), axis=-1, keepdims=True)
    return jnp.where(lo, s_lo, s_hi)


def _layer_rows(layer, rows, width):
    return pl.BlockSpec((None, rows, width), lambda i: (layer, 0, 0))


def _my_block():
    return 4 * lax.axis_index("x") + 2 * lax.axis_index("y") + lax.axis_index("c")


def _cast_bf16(arrays, name):
    n = len(arrays)

    def body(*refs):
        for a_ref, o_ref in zip(refs[:n], refs[n:]):
            o_ref[...] = a_ref[...].astype(BF16)

    whole = lambda a: pl.BlockSpec(a.shape, lambda: (0, 0))
    return _call(
        body, name=name,
        out_shape=tuple(jax.ShapeDtypeStruct(a.shape, BF16) for a in arrays),
        in_specs=[whole(a) for a in arrays], out_specs=tuple(whole(a) for a in arrays),
        compiler_params=_params(None, VMEM_LIMIT),
    )(*arrays)


class _Comm:
    def __init__(self, inputs, out_shapes, scratch, begin, middle, finish):
        self.inputs, self.out_shapes, self.scratch = list(inputs), list(out_shapes), list(scratch)
        self.begin, self.middle, self.finish = begin, middle, finish


def _slab(kind, ref, blk):
    if kind == "cols512":
        return ref.at[:, pl.ds(blk * 512, 512)]
    if kind == "rows128":
        return ref.at[pl.ds(blk * 128, 128), :]
    if kind == "cols128":
        return ref.at[:, pl.ds(blk * 128, 128)]
    return ref.at[blk]


def _gather_comm(items):
    n_t = len(items)
    kinds = [it[2] for it in items]

    def ctx(ins, outs, sems):
        send_sems, recv_sems, local_sems = sems
        x, y, c = lax.axis_index("x"), lax.axis_index("y"), lax.axis_index("c")
        me, sibling = (x, y, c), (x, y, 1 - c)
        chips = [(1 - x, y), (x, 1 - y), (1 - x, 1 - y)]

        def place(t, dev):
            return _slab(kinds[t], outs[t], 4 * dev[0] + 2 * dev[1] + dev[2])

        def copy(t, k, block, to, own=False):
            return pltpu.make_async_remote_copy(
                src_ref=ins[t] if own else place(t, block), dst_ref=place(t, block),
                send_sem=send_sems.at[t, k], recv_sem=recv_sems.at[t, k],
                device_id=to, device_id_type=pl.DeviceIdType.MESH)

        mine = [pltpu.make_async_copy(ins[t], place(t, me), local_sems.at[t]) for t in range(n_t)]
        first = []
        for t in range(n_t):
            first.append(copy(t, 0, me, sibling, own=True))
            first += [copy(t, 1 + j, me, (*chip, c), own=True) for j, chip in enumerate(chips)]
        passed = [copy(t, 4 + j, (*chip, c), sibling) for j, chip in enumerate(chips) for t in range(n_t)]
        landed = [copy(t, 1 + j, (*chip, c), me) for j, chip in enumerate(chips) for t in range(n_t)]
        from_sibling = []
        for t in range(n_t):
            from_sibling.append(copy(t, 0, sibling, me))
            from_sibling += [copy(t, 4 + j, (*chip, 1 - c), me) for j, chip in enumerate(chips)]
        return mine, first, landed, passed, from_sibling

    def begin(ins, outs, sems):
        mine, first, _, _, _ = ctx(ins, outs, sems)
        for cp in mine + first:
            cp.start()

    def middle(ins, outs, sems):
        _, _, landed, passed, _ = ctx(ins, outs, sems)
        for got, fwd in zip(landed, passed):
            got.wait_recv()
            fwd.start()

    def finish(ins, outs, sems):
        mine, first, _, passed, from_sibling = ctx(ins, outs, sems)
        for cp in from_sibling:
            cp.wait_recv()
        for cp in first + passed:
            cp.wait_send()
        for cp in mine:
            cp.wait()

    scratch = [pltpu.SemaphoreType.DMA((n_t, 7)), pltpu.SemaphoreType.DMA((n_t, 7)), pltpu.SemaphoreType.DMA((n_t,))]
    return _Comm([it[0] for it in items], [it[1] for it in items], scratch, begin, middle, finish)


def _exchange_comm(items):
    n_t = len(items)
    kinds = [it[2] for it in items]

    def ctx(ins, outs, sems):
        send_sems, recv_sems, local_sems = sems
        x, y, c = lax.axis_index("x"), lax.axis_index("y"), lax.axis_index("c")
        me_blk = 4 * x + 2 * y + c

        def src(t, blk):
            return ins[t] if kinds[t] == "slot" else _slab(kinds[t], ins[t], blk)

        local = [pltpu.make_async_copy(src(t, me_blk), outs[t].at[me_blk], local_sems.at[t]) for t in range(n_t)]
        remote = []
        for k in range(1, N_DEV):
            px = 1 - x if k & 4 else x
            py = 1 - y if k & 2 else y
            pc_ = 1 - c if k & 1 else c
            for t in range(n_t):
                remote.append(pltpu.make_async_remote_copy(
                    src_ref=src(t, 4 * px + 2 * py + pc_), dst_ref=outs[t].at[me_blk],
                    send_sem=send_sems.at[k - 1, t], recv_sem=recv_sems.at[k - 1, t],
                    device_id=(px, py, pc_), device_id_type=pl.DeviceIdType.MESH))
        return local, remote

    def begin(ins, outs, sems):
        local, remote = ctx(ins, outs, sems)
        for cp in local + remote:
            cp.start()

    def finish(ins, outs, sems):
        local, remote = ctx(ins, outs, sems)
        for cp in remote:
            cp.wait_recv()
        for cp in remote:
            cp.wait_send()
        for cp in local:
            cp.wait()

    scratch = [pltpu.SemaphoreType.DMA((N_DEV - 1, n_t)), pltpu.SemaphoreType.DMA((N_DEV - 1, n_t)),
               pltpu.SemaphoreType.DMA((n_t,))]
    out_shapes = [jax.ShapeDtypeStruct((N_DEV, *it[1].shape), it[1].dtype) for it in items]
    return _Comm([it[0] for it in items], out_shapes, scratch, begin, None, finish)


def _comm_call(comm, name):
    n_in, n_out = len(comm.inputs), len(comm.out_shapes)

    def body(*refs):
        ins, outs, sems = refs[:n_in], refs[n_in:n_in + n_out], refs[n_in + n_out:]
        comm.begin(ins, outs, sems)
        if comm.middle is not None:
            comm.middle(ins, outs, sems)
        comm.finish(ins, outs, sems)

    any_spec = pl.BlockSpec(memory_space=pl.ANY)
    return _call(body, name=name, out_shape=tuple(comm.out_shapes), in_specs=[any_spec] * n_in,
                 out_specs=[any_spec] * n_out, scratch_shapes=comm.scratch)(*comm.inputs)


def _hosted(body, n_in, n_out, comm, first, last, middle):
    if comm is None:
        return lambda *refs: body(*refs)
    n_ci, n_co, n_cs = len(comm.inputs), len(comm.out_shapes), len(comm.scratch)

    def wrapped(*refs):
        ins, cin = refs[:n_in], refs[n_in:n_in + n_ci]
        o0 = n_in + n_ci
        outs, cout = refs[o0:o0 + n_out], refs[o0 + n_out:o0 + n_out + n_co]
        scr, csem = refs[o0 + n_out + n_co:len(refs) - n_cs], refs[len(refs) - n_cs:]
        pl.when(first())(lambda: comm.begin(cin, cout, csem))
        body(*ins, *outs, *scr)
        if comm.middle is not None:
            pl.when(middle())(lambda: comm.middle(cin, cout, csem))
        pl.when(last())(lambda: comm.finish(cin, cout, csem))

    return wrapped


def _hosted_call(body, comm, *, name, grid, out_shape, in_specs, out_specs, args, scratch_shapes=(), sem=None):
    nd = len(grid)
    first, last, middle = _at_first(nd), _at_last(nd), _at_middle(nd)
    if comm is not None:
        sem = ("arbitrary",) * nd
    n_in, n_out = len(in_specs), len(out_shape)
    any_spec = pl.BlockSpec(memory_space=pl.ANY)
    c_in = [] if comm is None else comm.inputs
    c_out = [] if comm is None else comm.out_shapes
    c_scr = [] if comm is None else comm.scratch
    outs = _call(
        _hosted(body, n_in, n_out, comm, first, last, middle), name=name, grid=grid,
        out_shape=(*out_shape, *c_out),
        in_specs=[*in_specs, *[any_spec] * len(c_in)],
        out_specs=(*out_specs, *[any_spec] * len(c_out)),
        scratch_shapes=[*scratch_shapes, *c_scr],
        compiler_params=_params(sem, VMEM_LIMIT),
    )(*args, *c_in)
    return outs[:n_out], outs[n_out:]


def _grid_step(ndim):
    i, n = pl.program_id(0), pl.num_programs(0)
    for d in range(1, ndim):
        i, n = i * pl.num_programs(d) + pl.program_id(d), n * pl.num_programs(d)
    return i, n


def _at_first(ndim):
    return lambda: _grid_step(ndim)[0] == 0


def _at_last(ndim):
    def pred():
        i, n = _grid_step(ndim)
        return i == n - 1
    return pred


def _at_middle(ndim):
    def pred():
        i, n = _grid_step(ndim)
        return i == (3 * n) // 4
    return pred


def _fwd_in(x, g, layer, w_full, name, comm=None):
    s = x.shape[0]
    ts = min(ROW_TILE, s)

    def body(x_ref, g_ref, w_ref, h_ref, pc_ref, qkv_ref, az_ref):
        xf = x_ref[...]
        r = lax.rsqrt(jnp.mean(xf * xf, axis=-1, keepdims=True) + EPS)
        h = (xf * r * g_ref[...]).astype(BF16)
        h_ref[...] = h
        pc_ref[...] = jnp.dot(h, w_ref[:, 0:2048], preferred_element_type=F32).astype(BF16)
        q = jnp.dot(h, w_ref[:, 2048:2560], preferred_element_type=F32)
        qkv_ref[:, 0:512] = (q * 0.125).astype(BF16)
        qkv_ref[:, 512:1536] = jnp.dot(h, w_ref[:, 2560:3584], preferred_element_type=F32).astype(BF16)
        az_ref[...] = jnp.dot(h, w_ref[:, 3584:4096], preferred_element_type=F32).astype(BF16)

    row = lambda width: pl.BlockSpec((ts, width), lambda i: (i, 0))
    return _hosted_call(
        body, comm, name=name, grid=(s // ts,),
        out_shape=(jax.ShapeDtypeStruct((s, D_MODEL), BF16), jax.ShapeDtypeStruct((s, 2048), BF16),
                   jax.ShapeDtypeStruct((s, 1536), BF16), jax.ShapeDtypeStruct((s, 512), BF16)),
        in_specs=[row(D_MODEL), _layer_rows(layer, 1, D_MODEL),
                  pl.BlockSpec((D_MODEL, N_IN), lambda i: (0, 0))],
        out_specs=(row(D_MODEL), row(2048), row(1536), row(512)),
        args=(x, g, w_full), sem=("parallel",))


ATTN_ROWS = 128
ATTN_DONE = 104.0


def _attn_pieces(tq, rc):
    lane = lax.broadcasted_iota(jnp.int32, (1, LANES), 1)
    lo = lane < HEAD_DIM
    row = lax.broadcasted_iota(jnp.int32, (tq, tq), 0)
    col = lax.broadcasted_iota(jnp.int32, (tq, tq), 1)
    tri_gt = jnp.where(row > col, 1.0, 0.0).astype(BF16)
    tri_le = jnp.where(row <= col, 1.0, 0.0).astype(BF16)
    rrow = lax.broadcasted_iota(jnp.int32, (rc, tq), 0)
    rcol = lax.broadcasted_iota(jnp.int32, (rc, tq), 1)
    causal = [rcol < rrow + r * rc for r in range(tq // rc)]
    return lo, causal, tri_gt, tri_le


def _split_heads(a, lo):
    z = jnp.zeros_like(a)
    return (jnp.where(lo, a, z), jnp.where(lo, z, a))


def _softplus(z, causal, diag):
    neg_abs = lax.bitcast_convert_type(lax.bitcast_convert_type(z, jnp.uint32) | jnp.uint32(0x80000000), F32)
    sp = jnp.maximum(z, 0.0) + jnp.log(1.0 + jnp.exp(neg_abs))
    if diag:
        sp = jnp.where(causal, sp, 0.0)
    return sp


def _attn_fwd(qkv, name, comm=None):
    s = qkv.shape[0]
    tq = min(ATTN_TILE, s)
    nq = s // tq
    rc = min(ATTN_ROWS, tq)
    n_rc = tq // rc
    chains = [(r, hh) for r in range(n_rc) for hh in range(2)]

    def body(q_ref, k_ref, v_ref, o_ref, lsum_ref, nblk_ref):
        hp, qi = pl.program_id(0), pl.program_id(1)
        lo, causal, tri_gt, _ = _attn_pieces(tq, rc)
        qh = _split_heads(q_ref[...], lo)
        qc = {(r, hh): qh[hh][r * rc:(r + 1) * rc] for r, hh in chains}

        mm = lambda a_, b_: jnp.dot(a_.astype(BF16), b_, preferred_element_type=F32)
        rowsum = lambda a_: jnp.sum(a_, axis=-1, keepdims=True)

        def block(kb, carry):
            start = pl.multiple_of(kb * tq, tq)
            k = k_ref[pl.ds(start, tq), :]
            vh = _split_heads(v_ref[pl.ds(start, tq), :], lo)
            z = {ch: lax.dot_general(qc[ch], k, NT, preferred_element_type=F32) for ch in chains}
            sp = {ch: _softplus(z[ch], None, False) for ch in chains}
            later = {ch: mm(sp[ch], tri_gt) for ch in chains}
            a = {ch: jnp.exp((z[ch] - sp[ch]) - (carry[ch[0]][1 + ch[1]] + later[ch])) for ch in chains}
            pv = {ch: mm(a[ch], vh[ch[1]]) for ch in chains}
            return tuple((carry[r][0] + pv[(r, 0)] + pv[(r, 1)],
                          carry[r][1] + rowsum(sp[(r, 0)]), carry[r][2] + rowsum(sp[(r, 1)])) for r in range(n_rc))

        def first_two(prev_ok):
            d0 = pl.multiple_of(qi * tq, tq)
            p0 = pl.multiple_of(jnp.maximum(qi - 1, 0) * tq, tq)
            k_d, k_p = k_ref[pl.ds(d0, tq), :], k_ref[pl.ds(p0, tq), :]
            vh_d = _split_heads(v_ref[pl.ds(d0, tq), :], lo)
            vh_p = _split_heads(v_ref[pl.ds(p0, tq), :], lo)
            z_d = {ch: lax.dot_general(qc[ch], k_d, NT, preferred_element_type=F32) for ch in chains}
            z_p = {ch: lax.dot_general(qc[ch], k_p, NT, preferred_element_type=F32) for ch in chains}
            sp_d = {ch: _softplus(z_d[ch], causal[ch[0]], True) for ch in chains}
            sp_raw = {ch: _softplus(z_p[ch], None, False) for ch in chains}
            sp_p = {ch: jnp.where(prev_ok, sp_raw[ch], 0.0) for ch in chains}
            later_d = {ch: mm(sp_d[ch], tri_gt) for ch in chains}
            later_p = {ch: mm(sp_p[ch], tri_gt) for ch in chains}
            c_d = {ch: rowsum(sp_d[ch]) for ch in chains}
            a_d = {ch: jnp.where(causal[ch[0]], jnp.exp((z_d[ch] - sp_d[ch]) - later_d[ch]), 0.0) for ch in chains}
            a_p = {ch: jnp.where(prev_ok, jnp.exp((z_p[ch] - sp_raw[ch]) - (c_d[ch] + later_p[ch])), 0.0)
                   for ch in chains}
            pv = {ch: mm(a_d[ch], vh_d[ch[1]]) + mm(a_p[ch], vh_p[ch[1]]) for ch in chains}
            return tuple((pv[(r, 0)] + pv[(r, 1)],
                          c_d[(r, 0)] + rowsum(sp_p[(r, 0)]), c_d[(r, 1)] + rowsum(sp_p[(r, 1)]))
                         for r in range(n_rc))

        def least(carry):
            m = jnp.minimum(carry[0][1], carry[0][2])
            for r in range(1, n_rc):
                m = jnp.minimum(m, jnp.minimum(carry[r][1], carry[r][2]))
            return jnp.min(m)

        carry = first_two(qi > 0)

        def go_on(st):
            return jnp.logical_and(st[0] < qi - 1, st[1] < ATTN_DONE)

        def step(st):
            new = block(qi - 2 - st[0], st[2])
            return st[0] + 1, least(new), new

        walked, _, carry = lax.while_loop(go_on, step, (jnp.int32(0), least(carry), carry))
        for r in range(n_rc):
            o_ref[r * rc:(r + 1) * rc, :] = carry[r][0].astype(BF16)
            lsum_ref[r * rc:(r + 1) * rc, :] = jnp.where(lo, carry[r][1], carry[r][2])
        nblk_ref[hp, qi] = walked.astype(F32)

    blk = pl.BlockSpec((tq, LANES), lambda hp, qi: (qi, hp))
    o512 = jax.ShapeDtypeStruct((s, D_SB), F32)
    return _hosted_call(
        body, comm, name=name, grid=(4, nq),
        out_shape=(jax.ShapeDtypeStruct((s, D_SB), BF16), o512, jax.ShapeDtypeStruct((4, nq), F32)),
        in_specs=[blk, pl.BlockSpec((s, LANES), lambda hp, qi: (0, 4 + hp)),
                  pl.BlockSpec((s, LANES), lambda hp, qi: (0, 8 + hp))],
        out_specs=(blk, blk, pl.BlockSpec(memory_space=pltpu.SMEM)),
        args=(qkv, qkv, qkv), sem=("arbitrary", "arbitrary"))


HALO = 16


def _conv_taps(cc_ref, ch_ref, ccp_ref, chp_ref, halo_ref, first):
    u = cc_ref[...].astype(F32) * ch_ref[...].astype(F32)
    halo_ref[...] = ccp_ref[...].astype(F32) * chp_ref[...].astype(F32) * jnp.where(first, 0.0, 1.0)
    p6 = halo_ref[HALO - 2:HALO - 1, :]
    p7 = halo_ref[HALO - 1:HALO, :]
    rowi = lax.broadcasted_iota(jnp.int32, u.shape, 0)
    u1 = jnp.where(rowi == 0, p7, pltpu.roll(u, 1, 0))
    u2 = jnp.where(rowi == 0, p6, jnp.where(rowi == 1, p7, pltpu.roll(u, 2, 0)))
    return u, u1, u2


def _fwd_mid(x, pc, az, ya, p4, layer, cw, cb, bg, wout_full, pg, wpg_full, bpg, wpe_full, name, comm=None,
             head=None):
    s = x.shape[0]
    ts = min(ROW_TILE, s)
    blk_h = ts // HALO

    n_in = 18 + (2 if head else 0)

    def body(*refs):
        (x_ref, cb_ref_, cc_ref, ch_ref, cz_ref, ccp_ref, chp_ref, az_ref, ya_ref, p_ref,
         cw_ref, cbias_ref, bg_ref, wout_ref, pg_ref, wpg_ref, bpg_ref, wpe_ref) = refs[:18]
        x2_ref, x3_ref, gated_ref, h2_ref, gate_ref, e_ref = refs[n_in:n_in + 6]
        halo_ref = refs[-1]
        i = pl.program_id(0)
        lane = lax.broadcasted_iota(jnp.int32, (1, LANES), 1)
        lo = lane < HEAD_DIM
        u, u1, u2 = _conv_taps(cc_ref, ch_ref, ccp_ref, chp_ref, halo_ref, i == 0)
        conv = cbias_ref[...] + cw_ref[0:1, :] * u2 + cw_ref[1:2, :] * u1 + cw_ref[2:3, :] * u
        yc = cb_ref_[...].astype(F32) * conv
        for sl in range(8):
            cols = slice(LANES * (sl % 4), LANES * (sl % 4 + 1))
            y = yc[:, cols] if sl < 4 else ya_ref[:, cols].astype(F32)
            zc = (cz_ref[:, cols] if sl < 4 else az_ref[:, cols]).astype(F32)
            rg = lax.rsqrt(_group_bcast_sum(y * y, lo) * (1.0 / HEAD_DIM) + EPS)
            yn = y * rg * bg_ref[:, LANES * sl:LANES * (sl + 1)]
            gated_ref[:, LANES * sl:LANES * (sl + 1)] = (yn * (zc * _sigmoid(zc))).astype(BF16)
        x2 = x_ref[...] + jnp.dot(gated_ref[...], wout_ref[...], preferred_element_type=F32)
        x2_ref[...] = x2
        r2 = lax.rsqrt(jnp.mean(x2 * x2, axis=-1, keepdims=True) + EPS)
        h2 = (x2 * r2 * pg_ref[...]).astype(BF16)
        h2_ref[...] = h2
        gate = _sigmoid(jnp.dot(h2, wpg_ref[...], preferred_element_type=F32) + bpg_ref[...])
        gate_ref[...] = gate.astype(BF16)
        e = jnp.dot(p_ref[...].astype(BF16), wpe_ref[...], preferred_element_type=F32)
        e_ref[...] = e.astype(BF16)
        x3 = x2 + gate * e
        if not head:
            x3_ref[...] = x3
            return
        t_ref, fg_ref = refs[18:20]
        loss_ref, dfg_ref = refs[n_in + 6:n_in + 8]
        dx, loss, dfg = _loss_math(x3, t_ref[...], fg_ref[...])

        @pl.when(i == 0)
        def _():
            loss_ref[...] = jnp.zeros_like(loss_ref)
            dfg_ref[...] = jnp.zeros_like(dfg_ref)

        x3_ref[...] = dx
        loss_ref[...] += loss
        dfg_ref[...] += dfg

    row = lambda width, cb_=0: pl.BlockSpec((ts, width), lambda i: (i, cb_))
    prev = lambda cb_: pl.BlockSpec((HALO, 512), lambda i: (jnp.maximum(i * blk_h - 1, 0), cb_))
    vec = lambda width: pl.BlockSpec((1, width), lambda i: (0, 0))
    lvec = lambda width: _layer_rows(layer, 1, width)
    wspec = lambda r_, c_: pl.BlockSpec((r_, c_), lambda i: (0, 0))
    f32o = jax.ShapeDtypeStruct((s, D_MODEL), F32)
    bfo = jax.ShapeDtypeStruct((s, D_MODEL), BF16)
    head_in = [row(D_MODEL), vec(D_MODEL)] if head else []
    head_out = [jax.ShapeDtypeStruct((1, LANES), F32), jax.ShapeDtypeStruct((1, D_MODEL), F32)] if head else []
    return _hosted_call(
        body, comm, name=name, grid=(s // ts,),
        out_shape=(f32o, f32o, bfo, bfo, bfo, bfo, *head_out),
        scratch_shapes=[pltpu.VMEM((HALO, 512), F32)],
        in_specs=[row(D_MODEL), row(512, 0), row(512, 1), row(512, 2), row(512, 3), prev(1), prev(2),
                  row(512), row(512),
                  pl.BlockSpec((None, None, ts, PLE_DIM), lambda i: (layer, 0, i, 0)),
                  _layer_rows(layer, 3, 512), lvec(512), lvec(D_MODEL),
                  wspec(D_MODEL, D_MODEL), lvec(D_MODEL), wspec(D_MODEL, D_MODEL), lvec(D_MODEL),
                  wspec(PLE_DIM, D_MODEL), *head_in],
        out_specs=(*[row(D_MODEL)] * 6, *([vec(LANES), vec(D_MODEL)] if head else [])),
        args=(x, pc, pc, pc, pc, pc, pc, az, ya, p4, cw, cb, bg, wout_full, pg, wpg_full, bpg, wpe_full,
              *(head or ())),
        sem=("arbitrary",) if head else ("parallel",))


def _loss_math(x, target, g):
    r = lax.rsqrt(jnp.mean(x * x, axis=-1, keepdims=True) + EPS)
    xn = x * r
    err = xn * g - target
    per_row = jnp.sum(err * err, axis=-1, keepdims=True)
    loss = jnp.sum(per_row, axis=0, keepdims=True) * (0.5 / D_MODEL)
    dy = err * (1.0 / D_MODEL)
    dg = jnp.sum(dy * xn, axis=0, keepdims=True)
    dxn = dy * g
    return r * (dxn - xn * jnp.mean(dxn * xn, axis=-1, keepdims=True)), loss, dg


def _bwd_mid(dx3, x2, gate, e, pc, az, ya, gated, h2, p4, layer, cw, cb, bg, pg, wpg_full, wout_full, name,
             comm=None):
    s = x2.shape[0]
    ts = min(ROW_TILE, s)
    blk_h = ts // HALO

    def body(dx3_ref, x2_ref, gate_ref, e_ref, cb_ref_, cc_ref, ch_ref, cz_ref, ccp_ref, chp_ref, az_ref, ya_ref,
             gated_ref, h2_ref, p_ref, cw_ref, cbias_ref, bg_ref, pg_ref, wpg_ref, wout_ref,
             dx2_ref, dya_ref, dmisc_ref, dconv_ref, dwout_ref, dwpg_ref, dwpe_ref,
             dbpg_ref, dpg_ref, dbg_ref, dcbias_ref, dcw_ref,
             dgated_ref, halo_ref, acc_out, acc_pg, acc_pe):
        i = pl.program_id(0)

        @pl.when(i == 0)
        def _():
            for ref in (dbpg_ref, dpg_ref, dbg_ref, dcbias_ref, dcw_ref, acc_out, acc_pg, acc_pe):
                ref[...] = jnp.zeros_like(ref)

        lane = lax.broadcasted_iota(jnp.int32, (1, LANES), 1)
        lo = lane < HEAD_DIM
        dx3 = dx3_ref[...]
        gate = gate_ref[...].astype(F32)
        de_b = (dx3 * gate).astype(BF16)
        dgpre = dx3 * e_ref[...].astype(F32) * gate * (1.0 - gate)
        dbpg_ref[...] += jnp.sum(dgpre, axis=0, keepdims=True)
        dgpre_b = dgpre.astype(BF16)
        dh2 = lax.dot_general(dgpre_b, wpg_ref[...], NT, preferred_element_type=F32)
        acc_pe[...] += lax.dot_general(p_ref[...].astype(BF16), de_b, TN, preferred_element_type=F32)
        acc_pg[...] += lax.dot_general(h2_ref[...], dgpre_b, TN, preferred_element_type=F32)

        u, u1, u2 = _conv_taps(cc_ref, ch_ref, ccp_ref, chp_ref, halo_ref, i == 0)
        conv = cbias_ref[...] + cw_ref[0:1, :] * u2 + cw_ref[1:2, :] * u1 + cw_ref[2:3, :] * u
        c_b = cb_ref_[...].astype(F32)
        yc = c_b * conv
        fwd = []
        for sl in range(8):
            cols = slice(LANES * (sl % 4), LANES * (sl % 4 + 1))
            y = yc[:, cols] if sl < 4 else ya_ref[:, cols].astype(F32)
            zc = (cz_ref[:, cols] if sl < 4 else az_ref[:, cols]).astype(F32)
            rg = lax.rsqrt(_group_bcast_sum(y * y, lo) * (1.0 / HEAD_DIM) + EPS)
            sig = _sigmoid(zc)
            fwd.append((rg, y * rg, zc * sig, sig * (1.0 + zc * (1.0 - sig))))

        x2 = x2_ref[...]
        r2 = lax.rsqrt(jnp.mean(x2 * x2, axis=-1, keepdims=True) + EPS)
        xn2 = x2 * r2
        dpg_ref[...] += jnp.sum(dh2 * xn2, axis=0, keepdims=True)
        dxn = dh2 * pg_ref[...]
        dx2 = dx3 + r2 * (dxn - xn2 * jnp.mean(dxn * xn2, axis=-1, keepdims=True))
        dx2_ref[...] = dx2
        dx2_b = dx2.astype(BF16)
        dgated_ref[...] = lax.dot_general(dx2_b, wout_ref[...], NT, preferred_element_type=F32)
        acc_out[...] += lax.dot_general(gated_ref[...], dx2_b, TN, preferred_element_type=F32)

        for sl in range(8):
            cols = slice(LANES * (sl % 4), LANES * (sl % 4 + 1))
            wide = slice(LANES * sl, LANES * (sl + 1))
            rg, yhat, silu, dsilu = fwd[sl]
            bgs = bg_ref[:, wide]
            dgt = dgated_ref[:, wide]
            dyn = dgt * silu
            dzc = dgt * (yhat * bgs) * dsilu
            dbg_ref[:, wide] += jnp.sum(dyn * yhat, axis=0, keepdims=True)
            dyh = dyn * bgs
            dy = rg * (dyh - yhat * (_group_bcast_sum(dyh * yhat, lo) * (1.0 / HEAD_DIM)))
            if sl < 4:
                dconv = dy * c_b[:, cols]
                dmisc_ref[:, cols] = (dy * conv[:, cols]).astype(BF16)
                dmisc_ref[:, 512 + LANES * sl:512 + LANES * (sl + 1)] = dzc.astype(BF16)
                dconv_ref[:, cols] = dconv
                dcbias_ref[:, cols] += jnp.sum(dconv, axis=0, keepdims=True)
                dcw_ref[0:1, cols] += jnp.sum(dconv * u2[:, cols], axis=0, keepdims=True)
                dcw_ref[1:2, cols] += jnp.sum(dconv * u1[:, cols], axis=0, keepdims=True)
                dcw_ref[2:3, cols] += jnp.sum(dconv * u[:, cols], axis=0, keepdims=True)
            else:
                dya_ref[:, cols] = dy.astype(BF16)
                dmisc_ref[:, 1024 + LANES * (sl - 4):1024 + LANES * (sl - 3)] = dzc.astype(BF16)

        @pl.when(i == pl.num_programs(0) - 1)
        def _():
            dwout_ref[...] = acc_out[...].astype(BF16)
            dwpg_ref[...] = acc_pg[...].astype(BF16)
            dwpe_ref[...] = acc_pe[...].astype(BF16)

    row = lambda width, cb_=0: pl.BlockSpec((ts, width), lambda i: (i, cb_))
    prev = lambda cb_: pl.BlockSpec((HALO, 512), lambda i: (jnp.maximum(i * blk_h - 1, 0), cb_))
    vec = lambda width: pl.BlockSpec((1, width), lambda i: (0, 0))
    lvec = lambda width: _layer_rows(layer, 1, width)
    wspec = lambda r_, c_: pl.BlockSpec((r_, c_), lambda i: (0, 0))
    vo = lambda width: jax.ShapeDtypeStruct((1, width), F32)
    sq = jax.ShapeDtypeStruct((D_MODEL, D_MODEL), BF16)
    return _hosted_call(
        body, comm, name=name, grid=(s // ts,), sem=("arbitrary",),
        args=(dx3, x2, gate, e, pc, pc, pc, pc, pc, pc, az, ya, gated, h2, p4, cw, cb, bg, pg, wpg_full, wout_full),
        out_shape=(jax.ShapeDtypeStruct((s, D_MODEL), F32), jax.ShapeDtypeStruct((s, 512), BF16),
                   jax.ShapeDtypeStruct((s, 1536), BF16), jax.ShapeDtypeStruct((s, 512), F32),
                   sq, sq, jax.ShapeDtypeStruct((PLE_DIM, D_MODEL), BF16),
                   vo(D_MODEL), vo(D_MODEL), vo(D_MODEL), vo(512), jax.ShapeDtypeStruct((SUBLANES, 512), F32)),
        in_specs=[row(D_MODEL), row(D_MODEL), row(D_MODEL), row(D_MODEL),
                  row(512, 0), row(512, 1), row(512, 2), row(512, 3), prev(1), prev(2), row(512), row(512),
                  row(D_MODEL), row(D_MODEL),
                  pl.BlockSpec((None, None, ts, PLE_DIM), lambda i: (layer, 0, i, 0)),
                  _layer_rows(layer, 3, 512), lvec(512), lvec(D_MODEL), lvec(D_MODEL),
                  wspec(D_MODEL, D_MODEL), wspec(D_MODEL, D_MODEL)],
        out_specs=(row(D_MODEL), row(512), row(1536), row(512),
                   wspec(D_MODEL, D_MODEL), wspec(D_MODEL, D_MODEL), wspec(PLE_DIM, D_MODEL),
                   vec(D_MODEL), vec(D_MODEL), vec(D_MODEL), vec(512),
                   pl.BlockSpec((SUBLANES, 512), lambda i: (0, 0))),
        scratch_shapes=[pltpu.VMEM((ts, D_MODEL), F32), pltpu.VMEM((HALO, 512), F32),
                        pltpu.VMEM((D_MODEL, D_MODEL), F32), pltpu.VMEM((D_MODEL, D_MODEL), F32),
                        pltpu.VMEM((PLE_DIM, D_MODEL), F32)])


def _attn_bwd(qkv, lsum, nblk, dya, name, comm=None):
    s = qkv.shape[0]
    tq = min(ATTN_TILE, s)
    nq = s // tq
    rc = min(ATTN_ROWS, tq)
    n_rc = tq // rc
    chains = [(r, hh) for r in range(n_rc) for hh in range(2)]

    def body(nblk_ref, q_ref, k_ref, v_ref, lsum_ref, do_ref, dq_ref, dk_ref, dv_ref, dk_acc, dv_acc):
        hp, qi = pl.program_id(0), pl.program_id(1)

        @pl.when(qi == 0)
        def _():
            dk_acc[...] = jnp.zeros_like(dk_acc)
            dv_acc[...] = jnp.zeros_like(dv_acc)

        lo, causal, tri_gt, tri_le = _attn_pieces(tq, rc)
        lane = lax.broadcasted_iota(jnp.int32, (1, LANES), 1)
        qh = _split_heads(q_ref[...], lo)
        doh = _split_heads(do_ref[...].astype(BF16), lo)
        lt = lsum_ref[...]
        ltot_h = (jnp.sum(jnp.where(lane == 0, lt, 0.0), axis=-1, keepdims=True),
                  jnp.sum(jnp.where(lane == HEAD_DIM, lt, 0.0), axis=-1, keepdims=True))
        rows = lambda a_, r: a_[r * rc:(r + 1) * rc]
        qc = {(r, hh): rows(qh[hh], r) for r, hh in chains}
        doc = {(r, hh): rows(doh[hh], r) for r, hh in chains}
        ltot = {(r, hh): rows(ltot_h[hh], r) for r, hh in chains}

        mm = lambda a_, b_: jnp.dot(a_.astype(BF16), b_, preferred_element_type=F32)
        mm_nt = lambda a_, b_: lax.dot_general(a_, b_, NT, preferred_element_type=F32)
        mm_tn = lambda a_, b_: lax.dot_general(a_.astype(BF16), b_, TN, preferred_element_type=F32)
        rowsum = lambda a_: jnp.sum(a_, axis=-1, keepdims=True)

        def block(kb, carry, diag=False):
            start = pl.multiple_of(kb * tq, tq)
            k = k_ref[pl.ds(start, tq), :]
            v = v_ref[pl.ds(start, tq), :]
            kh = _split_heads(k, lo)
            keep = (lambda ch, a_: jnp.where(causal[ch[0]], a_, 0.0)) if diag else (lambda ch, a_: a_)
            z = {ch: mm_nt(qc[ch], k) for ch in chains}
            da = {ch: mm_nt(doc[ch], v) for ch in chains}
            sp = {ch: _softplus(z[ch], causal[ch[0]], diag) for ch in chains}
            later = {ch: mm(sp[ch], tri_gt) for ch in chains}
            walked = {ch: carry[ch[0]][1 + ch[1]] + rowsum(sp[ch]) for ch in chains}
            a = {ch: keep(ch, jnp.exp((z[ch] - sp[ch]) - ((ltot[ch] - walked[ch]) + later[ch]))) for ch in chains}
            g = {ch: a[ch] * da[ch] for ch in chains}
            upto = {ch: mm(g[ch], tri_le) for ch in chains}
            dz = {ch: keep(ch, g[ch] - jnp.exp(z[ch] - sp[ch]) * (carry[ch[0]][3 + ch[1]] + upto[ch])).astype(BF16)
                  for ch in chains}
            dqc = {ch: mm(dz[ch], kh[ch[1]]) for ch in chains}
            dkc = [mm_tn(dz[ch], qc[ch]) for ch in chains]
            dvc = [mm_tn(a[ch], doc[ch]) for ch in chains]
            dk_acc[pl.ds(start, tq), :] += sum(dkc[1:], dkc[0])
            dv_acc[pl.ds(start, tq), :] += sum(dvc[1:], dvc[0])
            return tuple((carry[r][0] + dqc[(r, 0)] + dqc[(r, 1)], walked[(r, 0)], walked[(r, 1)],
                          carry[r][3] + rowsum(g[(r, 0)]), carry[r][4] + rowsum(g[(r, 1)])) for r in range(n_rc))

        zc = jnp.zeros((rc, 1), F32)
        carry = tuple((jnp.zeros((rc, LANES), F32), zc, zc, zc, zc) for _ in range(n_rc))
        near = jnp.maximum(qi - 1, 0)
        first = near - jnp.clip(nblk_ref[hp, qi].astype(jnp.int32), 0, near)
        carry = lax.fori_loop(first, qi, block, carry)
        carry = block(qi, carry, True)
        for r in range(n_rc):
            dq_ref[r * rc:(r + 1) * rc, :] = (carry[r][0] * 0.125).astype(BF16)

        @pl.when(qi == pl.num_programs(1) - 1)
        def _():
            dk_ref[...] = dk_acc[...].astype(BF16)
            dv_ref[...] = dv_acc[...].astype(BF16)

    blk = pl.BlockSpec((tq, LANES), lambda hp, qi: (qi, hp))
    col = pl.BlockSpec((s, LANES), lambda hp, qi: (0, hp))
    o512 = jax.ShapeDtypeStruct((s, D_SB), BF16)
    return _hosted_call(
        body, comm, name=name, grid=(4, nq),
        out_shape=(o512, o512, o512),
        in_specs=[pl.BlockSpec(memory_space=pltpu.SMEM), blk,
                  pl.BlockSpec((s, LANES), lambda hp, qi: (0, 4 + hp)),
                  pl.BlockSpec((s, LANES), lambda hp, qi: (0, 8 + hp)), blk, blk],
        out_specs=(blk, col, col),
        scratch_shapes=[pltpu.VMEM((s, LANES), F32), pltpu.VMEM((s, LANES), F32)],
        args=(nblk, qkv, qkv, qkv, lsum, dya), sem=("parallel", "arbitrary"))


def _bwd_dproj(dmisc, dconv, pc, dq, dk, dv, x, dx2, g, cw, layer, win_full, name, comm=None, h=None):
    s = x.shape[0]
    ts = min(ROW_TILE, s)
    blk8 = ts // SUBLANES
    last8 = s // SUBLANES - 1
    fused = h is not None

    def body(*refs):
        (dcb_ref, dcz_ref, daz_ref, dconv_ref, nxt_ref, cc_ref, ch_ref, dq_ref, dk_ref, dv_ref,
         x_ref, dx2_ref, g_ref, cw_ref, w_ref) = refs[:15]
        if fused:
            h_ref, dx_ref, dg_ref, dw_ref, dproj_ref, acc_ref = refs[15:21]
        else:
            dproj_ref, dx_ref, dg_ref = refs[15:18]
        i = pl.program_id(0)

        @pl.when(i == 0)
        def _():
            dg_ref[...] = jnp.zeros_like(dg_ref)
            if fused:
                acc_ref[...] = jnp.zeros_like(acc_ref)

        keep = jnp.where(i == pl.num_programs(0) - 1, 0.0, 1.0)
        dc = dconv_ref[...]
        n0 = nxt_ref[0:1, :] * keep
        n1 = nxt_ref[1:2, :] * keep
        rowi = lax.broadcasted_iota(jnp.int32, dc.shape, 0)
        dc1 = jnp.where(rowi == ts - 1, n0, pltpu.roll(dc, ts - 1, 0))
        dc2 = jnp.where(rowi == ts - 2, n0, jnp.where(rowi == ts - 1, n1, pltpu.roll(dc, ts - 2, 0)))
        du = cw_ref[2:3, :] * dc + cw_ref[1:2, :] * dc1 + cw_ref[0:1, :] * dc2
        dproj_ref[:, 0:512] = dcb_ref[...]
        dproj_ref[:, 512:1024] = (du * ch_ref[...].astype(F32)).astype(BF16)
        dproj_ref[:, 1024:1536] = (du * cc_ref[...].astype(F32)).astype(BF16)
        dproj_ref[:, 1536:2048] = dcz_ref[...]
        dproj_ref[:, 2048:2560] = dq_ref[...]
        dproj_ref[:, 2560:3072] = dk_ref[...]
        dproj_ref[:, 3072:3584] = dv_ref[...]
        dproj_ref[:, 3584:4096] = daz_ref[...]
        dh = lax.dot_general(dproj_ref[...], w_ref[...], NT, preferred_element_type=F32)
        if fused:
            acc_ref[...] += lax.dot_general(h_ref[...], dproj_ref[...], TN, preferred_element_type=F32)
        x = x_ref[...]
        r = lax.rsqrt(jnp.mean(x * x, axis=-1, keepdims=True) + EPS)
        xn = x * r
        dg_ref[...] += jnp.sum(dh * xn, axis=0, keepdims=True)
        dxn = dh * g_ref[...]
        dx_ref[...] = dx2_ref[...] + r * (dxn - xn * jnp.mean(dxn * xn, axis=-1, keepdims=True))
        if fused:
            @pl.when(i == pl.num_programs(0) - 1)
            def _():
                dw_ref[...] = acc_ref[...].astype(BF16)

    row = lambda width, cb_=0: pl.BlockSpec((ts, width), lambda i: (i, cb_))
    nxt = pl.BlockSpec((SUBLANES, 512), lambda i: (jnp.minimum((i + 1) * blk8, last8), 0))
    vec = lambda width: pl.BlockSpec((1, width), lambda i: (0, 0))
    lvec = lambda width: _layer_rows(layer, 1, width)
    once = dict(pipeline_mode=pl.Buffered(1)) if fused else {}
    whole = lambda: pl.BlockSpec((D_MODEL, N_IN), lambda i: (0, 0), **once)
    f32o = jax.ShapeDtypeStruct((s, D_MODEL), F32)
    in_specs = [row(512, 0), row(512, 1), row(512, 2), row(512), nxt, row(512, 1), row(512, 2),
                row(512), row(512), row(512), row(D_MODEL), row(D_MODEL), lvec(D_MODEL),
                _layer_rows(layer, 3, 512), whole()]
    args = (dmisc, dmisc, dmisc, dconv, dconv, pc, pc, dq, dk, dv, x, dx2, g, cw, win_full)
    if fused:
        return _hosted_call(
            body, comm, name=name, grid=(s // ts,),
            out_shape=(f32o, jax.ShapeDtypeStruct((1, D_MODEL), F32), jax.ShapeDtypeStruct((D_MODEL, N_IN), BF16)),
            in_specs=in_specs + [row(D_MODEL)], out_specs=(row(D_MODEL), vec(D_MODEL), whole()),
            scratch_shapes=[pltpu.VMEM((ts, N_IN), BF16), pltpu.VMEM((D_MODEL, N_IN), F32)],
            args=args + (h,), sem=("arbitrary",))
    return _hosted_call(
        body, comm, name=name, grid=(s // ts,),
        out_shape=(jax.ShapeDtypeStruct((s, N_IN), BF16), f32o, jax.ShapeDtypeStruct((1, D_MODEL), F32)),
        in_specs=in_specs, out_specs=(row(N_IN), row(D_MODEL), vec(D_MODEL)), args=args, sem=("arbitrary",))


def _atb(a, b, name, a_cols=None, comm=None):
    s, n = b.shape
    m, a_blk = (a.shape[-1], 0) if a_cols is None else a_cols
    ts = min(512, s)
    tn = min(2048, n)
    a_spec = pl.BlockSpec((ts, m), lambda j, i: (i, a_blk))

    def body(a_ref, b_ref, o_ref, acc_ref):
        i = pl.program_id(1)

        @pl.when(i == 0)
        def _():
            acc_ref[...] = jnp.zeros_like(acc_ref)

        acc_ref[...] += lax.dot_general(a_ref[...].astype(BF16), b_ref[...], TN, preferred_element_type=F32)

        @pl.when(i == pl.num_programs(1) - 1)
        def _():
            o_ref[...] = acc_ref[...].astype(BF16)

    (out,), got = _hosted_call(
        body, comm, name=name, grid=(n // tn, s // ts),
        out_shape=(jax.ShapeDtypeStruct((m, n), BF16),),
        in_specs=[a_spec, pl.BlockSpec((ts, tn), lambda j, i: (i, j))],
        out_specs=(pl.BlockSpec((m, tn), lambda j, i: (0, j)),),
        scratch_shapes=[pltpu.VMEM((m, tn), F32)],
        args=(a, b), sem=("parallel", "arbitrary"))
    return out, got


def _adamw_math(w, g, m, v):
    m2 = ADAM_B1 * m + (1.0 - ADAM_B1) * g
    v2 = ADAM_B2 * v + (1.0 - ADAM_B2) * (g * g)
    m_hat = m2 / (1.0 - ADAM_B1 ** ADAM_STEP)
    v_hat = v2 / (1.0 - ADAM_B2 ** ADAM_STEP)
    delta = -ADAM_LR * (m_hat / (jnp.sqrt(v_hat) + ADAM_EPS) + ADAM_WD * w)
    return delta, m2, v2


def _adamw_sum8(pieces, w, m, v, name):
    _, rows, cols = w.shape
    tr = min([rows, 256] + [pc_[0].shape[1] for pc_ in pieces])
    n_tiles = rows // tr
    n_p = len(pieces)
    spans = [(layer, row0 // tr, arr.shape[1] // tr) for arr, layer, row0 in pieces]

    def body(*refs):
        p_refs = refs[:n_p]
        w_ref, m_ref, v_ref, g_ref, d_ref, m2_ref, v2_ref = refs[n_p:]
        l, i = pl.program_id(0), pl.program_id(1)

        def run(p_ref):
            g = p_ref[0].astype(F32)
            for d in range(1, N_DEV):
                g = g + p_ref[d].astype(F32)
            g_ref[...] = g
            d_ref[...], m2_ref[...], v2_ref[...] = _adamw_math(w_ref[...], g, m_ref[...], v_ref[...])

        for p_ref, (layer, t0, nt) in zip(p_refs, spans):
            mine = jnp.logical_and(l == layer, jnp.logical_and(i >= t0, i < t0 + nt))
            pl.when(mine)(lambda p_ref=p_ref: run(p_ref))

    def piece_spec(layer, t0, nt):
        return pl.BlockSpec((N_DEV, tr, cols),
                            lambda l, i: (0, jnp.clip(jnp.where(l == layer, i - t0, jnp.where(l < layer, 0, nt - 1)),
                                                      0, nt - 1), 0))

    tile = pl.BlockSpec((None, tr, cols), lambda l, i: (l, i, 0))
    o = jax.ShapeDtypeStruct((DEPTH, rows, cols), F32)
    return _call(
        body, name=name, grid=(DEPTH, n_tiles),
        out_shape=(o, o, o, o),
        in_specs=[*[piece_spec(*sp) for sp in spans], tile, tile, tile],
        out_specs=(tile, tile, tile, tile),
        compiler_params=_params(("arbitrary", "arbitrary"), VMEM_LIMIT),
    )(*[pc_[0] for pc_ in pieces], w, m, v)


def _small_update(blk, layered, final, conv, loss_parts):
    n_l = len(layered)
    ins = [a for item in layered for a in item] + list(final) + list(conv) + [loss_parts]
    shapes = [item[2].shape for item in layered] + [final[1].shape, conv[2].shape]
    out_shape = [jax.ShapeDtypeStruct(sh, F32) for sh in shapes for _ in range(4)]
    out_shape.append(jax.ShapeDtypeStruct((1, LANES), F32))

    def body(*refs):
        blk_ref, refs = refs[0], refs[1:]
        in_refs, out_refs, pick_ref = refs[:len(ins)], refs[len(ins):-1], refs[-1]

        def total(ref):
            g = ref[0]
            for d in range(1, N_DEV):
                g = g + ref[d]
            return g

        def update(k, at, g, w_ref, m_ref, v_ref):
            g_ref, d_ref, m2_ref, v2_ref = out_refs[4 * k:4 * k + 4]
            g_ref[at] = g
            d_ref[at], m2_ref[at], v2_ref[at] = _adamw_math(w_ref[at], g, m_ref[at], v_ref[at])

        for k in range(n_l):
            p0, p1, w_ref, m_ref, v_ref = in_refs[5 * k:5 * k + 5]
            for layer, parts in enumerate((p0, p1)):
                update(k, pl.ds(layer, 1), total(parts), w_ref, m_ref, v_ref)
        pf, w_ref, m_ref, v_ref = in_refs[5 * n_l:5 * n_l + 4]
        update(n_l, pl.ds(0, 1), total(pf), w_ref, m_ref, v_ref)
        c0, c1, w_ref, m_ref, v_ref = in_refs[5 * n_l + 4:5 * n_l + 9]
        for layer, parts in enumerate((c0, c1)):
            g8 = total(parts)
            mine = jnp.zeros((SUBLANES, HEAD_DIM), F32)
            for j in range(N_DEV):
                mine = mine + jnp.where(blk_ref[0] == j, g8[:, HEAD_DIM * j:HEAD_DIM * (j + 1)], 0.0)
            pick_ref[...] = mine
            update(n_l + 1, layer, pick_ref[0:3, :], w_ref, m_ref, v_ref)
        out_refs[-1][...] = total(in_refs[-1])

    whole = lambda shape: pl.BlockSpec(shape, lambda: (0,) * len(shape))
    outs = _call(
        body, name="adamw_small",
        out_shape=tuple(out_shape),
        in_specs=[pl.BlockSpec(memory_space=pltpu.SMEM)] + [whole(a.shape) for a in ins],
        out_specs=tuple(whole(o.shape) for o in out_shape),
        scratch_shapes=[pltpu.VMEM((SUBLANES, HEAD_DIM), F32)],
    )(blk, *ins)
    return [outs[4 * k:4 * k + 4] for k in range(n_l + 2)], outs[-1]


def kernel(x, p, norm_g, w_in, conv_w, conv_b, branch_g, w_out, ple_norm_g, w_pg, b_pg, w_pe, final_g, loss_target, m_norm_g, m_w_in, m_conv_w, m_conv_b, m_branch_g, m_w_out, m_ple_norm_g, m_w_pg, m_b_pg, m_w_pe, m_final_g, v_norm_g, v_w_in, v_conv_w, v_conv_b, v_branch_g, v_w_out, v_ple_norm_g, v_w_pg, v_b_pg, v_w_pe, v_final_g):
    s = x.shape[1]
    x0 = x.reshape(s, D_MODEL)
    target = loss_target.reshape(s, D_MODEL)
    me_blk = _my_block()

    win_s, wout_s, wpg_s, wpe_s = _cast_bf16(
        [w_in.reshape(DEPTH * D_MODEL, 512), w_out.reshape(DEPTH * 128, D_MODEL),
         w_pg.reshape(DEPTH * 128, D_MODEL), w_pe.reshape(DEPTH * PLE_DIM, 128)], "cast_weights")
    win_s, wout_s = win_s.reshape(DEPTH, D_MODEL, 512), wout_s.reshape(DEPTH, 128, D_MODEL)
    wpg_s, wpe_s = wpg_s.reshape(DEPTH, 128, D_MODEL), wpe_s.reshape(DEPTH, PLE_DIM, 128)
    cw_s = jnp.zeros((SUBLANES, LANES), F32).at[:DEPTH * 3, :HEAD_DIM].set(conv_w.reshape(DEPTH * 3, HEAD_DIM))
    bf = lambda r_, c_: jax.ShapeDtypeStruct((r_, c_), BF16)
    w_items = lambda l: [(wout_s[l], bf(D_MODEL, D_MODEL), "rows128"), (wpg_s[l], bf(D_MODEL, D_MODEL), "rows128"),
                         (wpe_s[l], bf(PLE_DIM, D_MODEL), "cols128")]
    win_f = [None] * DEPTH
    win_f[0], cw_all = _comm_call(_gather_comm([
        (win_s[0], bf(D_MODEL, N_IN), "cols512"),
        (cw_s, jax.ShapeDtypeStruct((N_DEV, SUBLANES, LANES), F32), "slot")]), "gather_w_in_0")
    cw_full = jnp.transpose(cw_all[:, :DEPTH * 3, :HEAD_DIM].reshape(N_DEV, DEPTH, 3, HEAD_DIM), (1, 2, 0, 3))
    cw_full = cw_full.reshape(DEPTH, 3, D_CONV)
    gather_rest_0 = _gather_comm(w_items(0))
    gather_win_1 = _gather_comm([(win_s[1], bf(D_MODEL, N_IN), "cols512")])
    gather_rest_1 = _gather_comm(w_items(1))

    norm3, convb3, branch3, ple3, bpg3 = [a.reshape(DEPTH, 1, -1) for a in (norm_g, conv_b, branch_g, ple_norm_g, b_pg)]

    saved = []
    xl = x0
    wout_f, wpg_f, wpe_f = [None] * DEPTH, [None] * DEPTH, [None] * DEPTH
    for l in range(DEPTH):
        (h, pc, qkv, az), got = _fwd_in(xl, norm3, l, win_f[l], f"fwd_in_{l}",
                                        comm=gather_rest_0 if l == 0 else None)
        if l == 0:
            wout_f[0], wpg_f[0], wpe_f[0] = got
        (ya, lsum, nblk), got = _attn_fwd(qkv, f"attn_fwd_{l}", comm=gather_win_1 if l == 0 else None)
        if l == 0:
            (win_f[1],) = got
        last = l == DEPTH - 1
        outs, got = _fwd_mid(
            xl, pc, az, ya, p, l, cw_full, convb3, branch3, wout_f[l],
            ple3, wpg_f[l], bpg3, wpe_f[l], f"fwd_mid_{l}",
            comm=gather_rest_1 if l == 0 else None, head=(target, final_g[None, :]) if last else None)
        x2, x3, gated, h2, gate, e = outs[:6]
        if l == 0:
            wout_f[1], wpg_f[1], wpe_f[1] = got
        saved.append(dict(x=xl, h=h, pc=pc, qkv=qkv, az=az, ya=ya, lsum=lsum, nblk=nblk, x2=x2, gated=gated, h2=h2,
                          gate=gate, e=e))
        xl = x3

    dx, (loss_acc, d_final_g) = xl, outs[6:]

    dwin, dwout, dwpg, dwpe = [None] * DEPTH, [None] * DEPTH, [None] * DEPTH, [None] * DEPTH
    small = dict(norm_g=[None] * DEPTH, conv_b=[None] * DEPTH, branch_g=[None] * DEPTH,
                 ple_norm_g=[None] * DEPTH, b_pg=[None] * DEPTH, conv_w=[None] * DEPTH)
    slot = lambda r_, c_: jax.ShapeDtypeStruct((r_, c_), BF16)
    half = D_MODEL // 2
    r_in1, r_out, r_pg, r_pe = None, [None] * DEPTH, [None] * DEPTH, [None] * DEPTH

    def rest_items(l):
        return [(dwout[l], slot(128, D_MODEL), "rows128"), (dwpg[l], slot(128, D_MODEL), "rows128"),
                (dwpe[l], slot(PLE_DIM, 128), "cols128")]

    for l in reversed(range(DEPTH)):
        sv = saved[l]
        ride = _exchange_comm(rest_items(1)) if l == 0 else None
        (dx2, dya, dmisc, dconv, dwout[l], dwpg[l], dwpe[l], d_bpg, d_pg, d_bg, d_cbias, d_cw), got = _bwd_mid(
            dx, sv["x2"], sv["gate"], sv["e"], sv["pc"], sv["az"], sv["ya"], sv["gated"], sv["h2"], p, l,
            cw_full, convb3, branch3, ple3, wpg_f[l], wout_f[l], f"bwd_mid_{l}",
            comm=ride)
        if l == 0:
            r_out[1], r_pg[1], r_pe[1] = got
        ride = _exchange_comm([(dwin[1], slot(D_MODEL, 512), "cols512")] + rest_items(0)) if l == 0 else None
        (dq, dk, dv), got = _attn_bwd(sv["qkv"], sv["lsum"], sv["nblk"], dya, f"attn_bwd_{l}", comm=ride)
        if l == 0:
            r_in1, r_out[0], r_pg[0], r_pe[0] = got
        dproj_args = (dmisc, dconv, sv["pc"], dq, dk, dv, sv["x"], dx2, norm3, cw_full, l, win_f[l])
        if l == 1:
            (dx, d_ng, dwin[1]), _ = _bwd_dproj(*dproj_args, "bwd_dproj_dw_1", h=sv["h"])
        else:
            (dproj, dx, d_ng), _ = _bwd_dproj(*dproj_args, "bwd_dproj_0")
            dwin_top, _ = _atb(sv["h"], dproj, "dw_in_0_top", a_cols=(half, 0))
            dwin_bot, (r_in0_top,) = _atb(sv["h"], dproj, "dw_in_0_bottom", a_cols=(half, 1),
                                          comm=_exchange_comm([(dwin_top, slot(half, 512), "cols512")]))
        small["norm_g"][l], small["conv_b"][l], small["branch_g"][l] = d_ng, d_cbias, d_bg
        small["ple_norm_g"][l], small["b_pg"][l], small["conv_w"][l] = d_pg, d_bpg, d_cw
    grad_x = dx.reshape(1, s, D_MODEL)

    names = ["norm_g", "conv_b", "branch_g", "ple_norm_g", "b_pg", "conv_w"]
    small_list = [small[n][l] for n in names for l in range(DEPTH)] + [d_final_g, loss_acc]
    got = _comm_call(_exchange_comm(
        [(dwin_bot, slot(half, 512), "cols512")]
        + [(a, jax.ShapeDtypeStruct(a.shape, F32), "slot") for a in small_list]), "exchange_last")
    r_in0_bot, r_small = got[0], got[1:]

    per_layer = lambda r: [(r[0], 0, 0), (r[1], 1, 0)]
    g_win, d_win, m_win, v_win = _adamw_sum8([(r_in0_top, 0, 0), (r_in0_bot, 0, half), (r_in1, 1, 0)],
                                             w_in, m_w_in, v_w_in, "adamw_w_in")
    g_wout, d_wout, m_wout, v_wout = _adamw_sum8(per_layer(r_out), w_out, m_w_out, v_w_out, "adamw_w_out")
    g_wpg, d_wpg, m_wpg, v_wpg = _adamw_sum8(per_layer(r_pg), w_pg, m_w_pg, v_w_pg, "adamw_w_pg")
    g_wpe, d_wpe, m_wpe, v_wpe = _adamw_sum8(per_layer(r_pe), w_pe, m_w_pe, v_w_pe, "adamw_w_pe")

    layered = [(norm_g, m_norm_g, v_norm_g), (conv_b, m_conv_b, v_conv_b), (branch_g, m_branch_g, v_branch_g),
               (ple_norm_g, m_ple_norm_g, v_ple_norm_g), (b_pg, m_b_pg, v_b_pg)]
    row = lambda a: a.reshape(1, -1)
    upd, loss_row = _small_update(
        jnp.reshape(me_blk, (1,)).astype(jnp.int32),
        [(r_small[2 * k], r_small[2 * k + 1], *wmv) for k, wmv in enumerate(layered)],
        (r_small[12], row(final_g), row(m_final_g), row(v_final_g)),
        (r_small[10], r_small[11], conv_w, m_conv_w, v_conv_w), r_small[13])
    loss = loss_row[0, 0]
    upd[5] = [a.reshape(-1) for a in upd[5]]

    big = {1: (g_win, d_win, m_win, v_win), 5: (g_wout, d_wout, m_wout, v_wout), 7: (g_wpg, d_wpg, m_wpg, v_wpg),
           9: (g_wpe, d_wpe, m_wpe, v_wpe)}
    small_at = {0: 0, 2: 6, 3: 1, 4: 2, 6: 3, 8: 4, 10: 5}
    per_kind = [[(big[i] if i in big else upd[small_at[i]])[j] for i in range(11)] for j in range(4)]
    return (loss, grad_x, *per_kind[0], *per_kind[1], *per_kind[2], *per_kind[3])
```

```python
import jax
import jax.numpy as jnp
from jax import lax
from jax.experimental import pallas as pl
from jax.experimental.pallas import tpu as pltpu

F32 = jnp.float32
BF16 = jnp.bfloat16

D_MODEL = 1024
D_CONV = 512
D_SB = 512
N_IN = 4096
HEAD_DIM = 64
PLE_DIM = 256
DEPTH = 2
EPS = 1e-6
ADAM_LR = 0.001
ADAM_B1 = 0.9
ADAM_B2 = 0.999
ADAM_EPS = 1e-08
ADAM_WD = 0.01
ADAM_STEP = 10

LANES = 128
SUBLANES = 8
VMEM_BYTES_V7X = 64 * 1024 * 1024
VMEM_LIMIT = VMEM_BYTES_V7X - 8 * 1024 * 1024

N_DEV = 8
ROW_TILE = 256
ATTN_TILE = 256

NT = (((1,), (1,)), ((), ()))
TN = (((0,), (0,)), ((), ()))


def _call(body, **kw):
    return pl.pallas_call(body, **kw)


def _params(sem=None, vmem=None):
    return pltpu.CompilerParams(dimension_semantics=sem, vmem_limit_bytes=vmem)


def _sigmoid(z):
    return 0.5 * jnp.tanh(0.5 * z) + 0.5


def _group_bcast_sum(a, lo):
    s_lo = jnp.sum(jnp.where(lo, a, 0.0), axis=-1, keepdims=True)
    s_hi = jnp.sum(jnp.where(lo, 0.0, a), axis=-1, keepdims=True)
    return jnp.where(lo, s_lo, s_hi)


def _layer_rows(layer, rows, width):
    return pl.BlockSpec((None, rows, width), lambda i: (layer, 0, 0))


def _my_block():
    return 4 * lax.axis_index("x") + 2 * lax.axis_index("y") + lax.axis_index("c")


def _cast_bf16(arrays, name):
    n = len(arrays)

    def body(*refs):
        for a_ref, o_ref in zip(refs[:n], refs[n:]):
            o_ref[...] = a_ref[...].astype(BF16)

    whole = lambda a: pl.BlockSpec(a.shape, lambda: (0, 0))
    return _call(
        body, name=name,
        out_shape=tuple(jax.ShapeDtypeStruct(a.shape, BF16) for a in arrays),
        in_specs=[whole(a) for a in arrays], out_specs=tuple(whole(a) for a in arrays),
        compiler_params=_params(None, VMEM_LIMIT),
    )(*arrays)


class _Comm:
    def __init__(self, inputs, out_shapes, scratch, begin, middle, finish):
        self.inputs, self.out_shapes, self.scratch = list(inputs), list(out_shapes), list(scratch)
        self.begin, self.middle, self.finish = begin, middle, finish


def _slab(kind, ref, blk):
    if kind == "cols512":
        return ref.at[:, pl.ds(blk * 512, 512)]
    if kind == "rows128":
        return ref.at[pl.ds(blk * 128, 128), :]
    if kind == "cols128":
        return ref.at[:, pl.ds(blk * 128, 128)]
    return ref.at[blk]


def _gather_comm(items):
    n_t = len(items)
    kinds = [it[2] for it in items]

    def ctx(ins, outs, sems):
        send_sems, recv_sems, local_sems = sems
        x, y, c = lax.axis_index("x"), lax.axis_index("y"), lax.axis_index("c")
        me, sibling = (x, y, c), (x, y, 1 - c)
        chips = [(1 - x, y), (x, 1 - y), (1 - x, 1 - y)]

        def place(t, dev):
            return _slab(kinds[t], outs[t], 4 * dev[0] + 2 * dev[1] + dev[2])

        def copy(t, k, block, to, own=False):
            return pltpu.make_async_remote_copy(
                src_ref=ins[t] if own else place(t, block), dst_ref=place(t, block),
                send_sem=send_sems.at[t, k], recv_sem=recv_sems.at[t, k],
                device_id=to, device_id_type=pl.DeviceIdType.MESH)

        mine = [pltpu.make_async_copy(ins[t], place(t, me), local_sems.at[t]) for t in range(n_t)]
        first = []
        for t in range(n_t):
            first.append(copy(t, 0, me, sibling, own=True))
            first += [copy(t, 1 + j, me, (*chip, c), own=True) for j, chip in enumerate(chips)]
        passed = [copy(t, 4 + j, (*chip, c), sibling) for j, chip in enumerate(chips) for t in range(n_t)]
        landed = [copy(t, 1 + j, (*chip, c), me) for j, chip in enumerate(chips) for t in range(n_t)]
        from_sibling = []
        for t in range(n_t):
            from_sibling.append(copy(t, 0, sibling, me))
            from_sibling += [copy(t, 4 + j, (*chip, 1 - c), me) for j, chip in enumerate(chips)]
        return mine, first, landed, passed, from_sibling

    def begin(ins, outs, sems):
        mine, first, _, _, _ = ctx(ins, outs, sems)
        for cp in mine + first:
            cp.start()

    def middle(ins, outs, sems):
        _, _, landed, passed, _ = ctx(ins, outs, sems)
        for got, fwd in zip(landed, passed):
            got.wait_recv()
            fwd.start()

    def finish(ins, outs, sems):
        mine, first, _, passed, from_sibling = ctx(ins, outs, sems)
        for cp in from_sibling:
            cp.wait_recv()
        for cp in first + passed:
            cp.wait_send()
        for cp in mine:
            cp.wait()

    scratch = [pltpu.SemaphoreType.DMA((n_t, 7)), pltpu.SemaphoreType.DMA((n_t, 7)), pltpu.SemaphoreType.DMA((n_t,))]
    return _Comm([it[0] for it in items], [it[1] for it in items], scratch, begin, middle, finish)


def _exchange_comm(items):
    n_t = len(items)
    kinds = [it[2] for it in items]

    def ctx(ins, outs, sems):
        send_sems, recv_sems, local_sems = sems
        x, y, c = lax.axis_index("x"), lax.axis_index("y"), lax.axis_index("c")
        me_blk = 4 * x + 2 * y + c

        def src(t, blk):
            return ins[t] if kinds[t] == "slot" else _slab(kinds[t], ins[t], blk)

        local = [pltpu.make_async_copy(src(t, me_blk), outs[t].at[me_blk], local_sems.at[t]) for t in range(n_t)]
        remote = []
        for k in range(1, N_DEV):
            px = 1 - x if k & 4 else x
            py = 1 - y if k & 2 else y
            pc_ = 1 - c if k & 1 else c
            for t in range(n_t):
                remote.append(pltpu.make_async_remote_copy(
                    src_ref=src(t, 4 * px + 2 * py + pc_), dst_ref=outs[t].at[me_blk],
                    send_sem=send_sems.at[k - 1, t], recv_sem=recv_sems.at[k - 1, t],
                    device_id=(px, py, pc_), device_id_type=pl.DeviceIdType.MESH))
        return local, remote

    def begin(ins, outs, sems):
        local, remote = ctx(ins, outs, sems)
        for cp in local + remote:
            cp.start()

    def finish(ins, outs, sems):
        local, remote = ctx(ins, outs, sems)
        for cp in remote:
            cp.wait_recv()
        for cp in remote:
            cp.wait_send()
        for cp in local:
            cp.wait()

    scratch = [pltpu.SemaphoreType.DMA((N_DEV - 1, n_t)), pltpu.SemaphoreType.DMA((N_DEV - 1, n_t)),
               pltpu.SemaphoreType.DMA((n_t,))]
    out_shapes = [jax.ShapeDtypeStruct((N_DEV, *it[1].shape), it[1].dtype) for it in items]
    return _Comm([it[0] for it in items], out_shapes, scratch, begin, None, finish)


def _comm_call(comm, name):
    n_in, n_out = len(comm.inputs), len(comm.out_shapes)

    def body(*refs):
        ins, outs, sems = refs[:n_in], refs[n_in:n_in + n_out], refs[n_in + n_out:]
        comm.begin(ins, outs, sems)
        if comm.middle is not None:
            comm.middle(ins, outs, sems)
        comm.finish(ins, outs, sems)

    any_spec = pl.BlockSpec(memory_space=pl.ANY)
    return _call(body, name=name, out_shape=tuple(comm.out_shapes), in_specs=[any_spec] * n_in,
                 out_specs=[any_spec] * n_out, scratch_shapes=comm.scratch)(*comm.inputs)


def _hosted(body, n_in, n_out, comm, first, last, middle):
    if comm is None:
        return lambda *refs: body(*refs)
    n_ci, n_co, n_cs = len(comm.inputs), len(comm.out_shapes), len(comm.scratch)

    def wrapped(*refs):
        ins, cin = refs[:n_in], refs[n_in:n_in + n_ci]
        o0 = n_in + n_ci
        outs, cout = refs[o0:o0 + n_out], refs[o0 + n_out:o0 + n_out + n_co]
        scr, csem = refs[o0 + n_out + n_co:len(refs) - n_cs], refs[len(refs) - n_cs:]
        pl.when(first())(lambda: comm.begin(cin, cout, csem))
        body(*ins, *outs, *scr)
        if comm.middle is not None:
            pl.when(middle())(lambda: comm.middle(cin, cout, csem))
        pl.when(last())(lambda: comm.finish(cin, cout, csem))

    return wrapped


def _hosted_call(body, comm, *, name, grid, out_shape, in_specs, out_specs, args, scratch_shapes=(), sem=None):
    nd = len(grid)
    first, last, middle = _at_first(nd), _at_last(nd), _at_middle(nd)
    if comm is not None:
        sem = ("arbitrary",) * nd
    n_in, n_out = len(in_specs), len(out_shape)
    any_spec = pl.BlockSpec(memory_space=pl.ANY)
    c_in = [] if comm is None else comm.inputs
    c_out = [] if comm is None else comm.out_shapes
    c_scr = [] if comm is None else comm.scratch
    outs = _call(
        _hosted(body, n_in, n_out, comm, first, last, middle), name=name, grid=grid,
        out_shape=(*out_shape, *c_out),
        in_specs=[*in_specs, *[any_spec] * len(c_in)],
        out_specs=(*out_specs, *[any_spec] * len(c_out)),
        scratch_shapes=[*scratch_shapes, *c_scr],
        compiler_params=_params(sem, VMEM_LIMIT),
    )(*args, *c_in)
    return outs[:n_out], outs[n_out:]


def _grid_step(ndim):
    i, n = pl.program_id(0), pl.num_programs(0)
    for d in range(1, ndim):
        i, n = i * pl.num_programs(d) + pl.program_id(d), n * pl.num_programs(d)
    return i, n


def _at_first(ndim):
    return lambda: _grid_step(ndim)[0] == 0


def _at_last(ndim):
    def pred():
        i, n = _grid_step(ndim)
        return i == n - 1
    return pred


def _at_middle(ndim):
    def pred():
        i, n = _grid_step(ndim)
        return i == (3 * n) // 4
    return pred


def _fwd_in(x, g, layer, w_full, name, comm=None):
    s = x.shape[0]
    ts = min(ROW_TILE, s)

    def body(x_ref, g_ref, w_ref, h_ref, pc_ref, qkv_ref, az_ref):
        xf = x_ref[...]
        r = lax.rsqrt(jnp.mean(xf * xf, axis=-1, keepdims=True) + EPS)
        h = (xf * r * g_ref[...]).astype(BF16)
        h_ref[...] = h
        pc_ref[...] = jnp.dot(h, w_ref[:, 0:2048], preferred_element_type=F32).astype(BF16)
        q = jnp.dot(h, w_ref[:, 2048:2560], preferred_element_type=F32)
        qkv_ref[:, 0:512] = (q * 0.125).astype(BF16)
        qkv_ref[:, 512:1536] = jnp.dot(h, w_ref[:, 2560:3584], preferred_element_type=F32).astype(BF16)
        az_ref[...] = jnp.dot(h, w_ref[:, 3584:4096], preferred_element_type=F32).astype(BF16)

    row = lambda width: pl.BlockSpec((ts, width), lambda i: (i, 0))
    return _hosted_call(
        body, comm, name=name, grid=(s // ts,),
        out_shape=(jax.ShapeDtypeStruct((s, D_MODEL), BF16), jax.ShapeDtypeStruct((s, 2048), BF16),
                   jax.ShapeDtypeStruct((s, 1536), BF16), jax.ShapeDtypeStruct((s, 512), BF16)),
        in_specs=[row(D_MODEL), _layer_rows(layer, 1, D_MODEL),
                  pl.BlockSpec((D_MODEL, N_IN), lambda i: (0, 0))],
        out_specs=(row(D_MODEL), row(2048), row(1536), row(512)),
        args=(x, g, w_full), sem=("parallel",))


ATTN_ROWS = 128
ATTN_DONE = 104.0


def _attn_pieces(tq, rc):
    lane = lax.broadcasted_iota(jnp.int32, (1, LANES), 1)
    lo = lane < HEAD_DIM
    row = lax.broadcasted_iota(jnp.int32, (tq, tq), 0)
    col = lax.broadcasted_iota(jnp.int32, (tq, tq), 1)
    tri_gt = jnp.where(row > col, 1.0, 0.0).astype(BF16)
    tri_le = jnp.where(row <= col, 1.0, 0.0).astype(BF16)
    rrow = lax.broadcasted_iota(jnp.int32, (rc, tq), 0)
    rcol = lax.broadcasted_iota(jnp.int32, (rc, tq), 1)
    causal = [rcol < rrow + r * rc for r in range(tq // rc)]
    return lo, causal, tri_gt, tri_le


def _split_heads(a, lo):
    z = jnp.zeros_like(a)
    return (jnp.where(lo, a, z), jnp.where(lo, z, a))


def _softplus(z, causal, diag):
    neg_abs = lax.bitcast_convert_type(lax.bitcast_convert_type(z, jnp.uint32) | jnp.uint32(0x80000000), F32)
    sp = jnp.maximum(z, 0.0) + jnp.log(1.0 + jnp.exp(neg_abs))
    if diag:
        sp = jnp.where(causal, sp, 0.0)
    return sp


def _attn_fwd(qkv, name, comm=None):
    s = qkv.shape[0]
    tq = min(ATTN_TILE, s)
    nq = s // tq
    rc = min(ATTN_ROWS, tq)
    n_rc = tq // rc
    chains = [(r, hh) for r in range(n_rc) for hh in range(2)]

    def body(q_ref, k_ref, v_ref, o_ref, lsum_ref, nblk_ref):
        hp, qi = pl.program_id(0), pl.program_id(1)
        lo, causal, tri_gt, _ = _attn_pieces(tq, rc)
        qh = _split_heads(q_ref[...], lo)
        qc = {(r, hh): qh[hh][r * rc:(r + 1) * rc] for r, hh in chains}

        mm = lambda a_, b_: jnp.dot(a_.astype(BF16), b_, preferred_element_type=F32)
        rowsum = lambda a_: jnp.sum(a_, axis=-1, keepdims=True)

        def block(kb, carry):
            start = pl.multiple_of(kb * tq, tq)
            k = k_ref[pl.ds(start, tq), :]
            vh = _split_heads(v_ref[pl.ds(start, tq), :], lo)
            z = {ch: lax.dot_general(qc[ch], k, NT, preferred_element_type=F32) for ch in chains}
            sp = {ch: _softplus(z[ch], None, False) for ch in chains}
            later = {ch: mm(sp[ch], tri_gt) for ch in chains}
            a = {ch: jnp.exp((z[ch] - sp[ch]) - (carry[ch[0]][1 + ch[1]] + later[ch])) for ch in chains}
            pv = {ch: mm(a[ch], vh[ch[1]]) for ch in chains}
            return tuple((carry[r][0] + pv[(r, 0)] + pv[(r, 1)],
                          carry[r][1] + rowsum(sp[(r, 0)]), carry[r][2] + rowsum(sp[(r, 1)])) for r in range(n_rc))

        def first_two(prev_ok):
            d0 = pl.multiple_of(qi * tq, tq)
            p0 = pl.multiple_of(jnp.maximum(qi - 1, 0) * tq, tq)
            k_d, k_p = k_ref[pl.ds(d0, tq), :], k_ref[pl.ds(p0, tq), :]
            vh_d = _split_heads(v_ref[pl.ds(d0, tq), :], lo)
            vh_p = _split_heads(v_ref[pl.ds(p0, tq), :], lo)
            z_d = {ch: lax.dot_general(qc[ch], k_d, NT, preferred_element_type=F32) for ch in chains}
            z_p = {ch: lax.dot_general(qc[ch], k_p, NT, preferred_element_type=F32) for ch in chains}
            sp_d = {ch: _softplus(z_d[ch], causal[ch[0]], True) for ch in chains}
            sp_raw = {ch: _softplus(z_p[ch], None, False) for ch in chains}
            sp_p = {ch: jnp.where(prev_ok, sp_raw[ch], 0.0) for ch in chains}
            later_d = {ch: mm(sp_d[ch], tri_gt) for ch in chains}
            later_p = {ch: mm(sp_p[ch], tri_gt) for ch in chains}
            c_d = {ch: rowsum(sp_d[ch]) for ch in chains}
            a_d = {ch: jnp.where(causal[ch[0]], jnp.exp((z_d[ch] - sp_d[ch]) - later_d[ch]), 0.0) for ch in chains}
            a_p = {ch: jnp.where(prev_ok, jnp.exp((z_p[ch] - sp_raw[ch]) - (c_d[ch] + later_p[ch])), 0.0)
                   for ch in chains}
            pv = {ch: mm(a_d[ch], vh_d[ch[1]]) + mm(a_p[ch], vh_p[ch[1]]) for ch in chains}
            return tuple((pv[(r, 0)] + pv[(r, 1)],
                          c_d[(r, 0)] + rowsum(sp_p[(r, 0)]), c_d[(r, 1)] + rowsum(sp_p[(r, 1)]))
                         for r in range(n_rc))

        def least(carry):
            m = jnp.minimum(carry[0][1], carry[0][2])
            for r in range(1, n_rc):
                m = jnp.minimum(m, jnp.minimum(carry[r][1], carry[r][2]))
            return jnp.min(m)

        carry = first_two(qi > 0)

        def go_on(st):
            return jnp.logical_and(st[0] < qi - 1, st[1] < ATTN_DONE)

        def step(st):
            new = block(qi - 2 - st[0], st[2])
            return st[0] + 1, least(new), new

        walked, _, carry = lax.while_loop(go_on, step, (jnp.int32(0), least(carry), carry))
        for r in range(n_rc):
            o_ref[r * rc:(r + 1) * rc, :] = carry[r][0].astype(BF16)
            lsum_ref[r * rc:(r + 1) * rc, :] = jnp.where(lo, carry[r][1], carry[r][2])
        nblk_ref[hp, qi] = walked.astype(F32)

    blk = pl.BlockSpec((tq, LANES), lambda hp, qi: (qi, hp))
    o512 = jax.ShapeDtypeStruct((s, D_SB), F32)
    return _hosted_call(
        body, comm, name=name, grid=(4, nq),
        out_shape=(jax.ShapeDtypeStruct((s, D_SB), BF16), o512, jax.ShapeDtypeStruct((4, nq), F32)),
        in_specs=[blk, pl.BlockSpec((s, LANES), lambda hp, qi: (0, 4 + hp)),
                  pl.BlockSpec((s, LANES), lambda hp, qi: (0, 8 + hp))],
        out_specs=(blk, blk, pl.BlockSpec(memory_space=pltpu.SMEM)),
        args=(qkv, qkv, qkv), sem=("arbitrary", "arbitrary"))


HALO = 16


def _conv_taps(cc_ref, ch_ref, ccp_ref, chp_ref, halo_ref, first):
    u = cc_ref[...].astype(F32) * ch_ref[...].astype(F32)
    halo_ref[...] = ccp_ref[...].astype(F32) * chp_ref[...].astype(F32) * jnp.where(first, 0.0, 1.0)
    p6 = halo_ref[HALO - 2:HALO - 1, :]
    p7 = halo_ref[HALO - 1:HALO, :]
    rowi = lax.broadcasted_iota(jnp.int32, u.shape, 0)
    u1 = jnp.where(rowi == 0, p7, pltpu.roll(u, 1, 0))
    u2 = jnp.where(rowi == 0, p6, jnp.where(rowi == 1, p7, pltpu.roll(u, 2, 0)))
    return u, u1, u2


def _fwd_mid(x, pc, az, ya, p4, layer, cw, cb, bg, wout_full, pg, wpg_full, bpg, wpe_full, name, comm=None,
             head=None):
    s = x.shape[0]
    ts = min(ROW_TILE, s)
    blk_h = ts // HALO

    n_in = 18 + (2 if head else 0)

    def body(*refs):
        (x_ref, cb_ref_, cc_ref, ch_ref, cz_ref, ccp_ref, chp_ref, az_ref, ya_ref, p_ref,
         cw_ref, cbias_ref, bg_ref, wout_ref, pg_ref, wpg_ref, bpg_ref, wpe_ref) = refs[:18]
        x2_ref, x3_ref, gated_ref, h2_ref, gate_ref, e_ref = refs[n_in:n_in + 6]
        halo_ref = refs[-1]
        i = pl.program_id(0)
        lane = lax.broadcasted_iota(jnp.int32, (1, LANES), 1)
        lo = lane < HEAD_DIM
        u, u1, u2 = _conv_taps(cc_ref, ch_ref, ccp_ref, chp_ref, halo_ref, i == 0)
        conv = cbias_ref[...] + cw_ref[0:1, :] * u2 + cw_ref[1:2, :] * u1 + cw_ref[2:3, :] * u
        yc = cb_ref_[...].astype(F32) * conv
        for sl in range(8):
            cols = slice(LANES * (sl % 4), LANES * (sl % 4 + 1))
            y = yc[:, cols] if sl < 4 else ya_ref[:, cols].astype(F32)
            zc = (cz_ref[:, cols] if sl < 4 else az_ref[:, cols]).astype(F32)
            rg = lax.rsqrt(_group_bcast_sum(y * y, lo) * (1.0 / HEAD_DIM) + EPS)
            yn = y * rg * bg_ref[:, LANES * sl:LANES * (sl + 1)]
            gated_ref[:, LANES * sl:LANES * (sl + 1)] = (yn * (zc * _sigmoid(zc))).astype(BF16)
        x2 = x_ref[...] + jnp.dot(gated_ref[...], wout_ref[...], preferred_element_type=F32)
        x2_ref[...] = x2
        r2 = lax.rsqrt(jnp.mean(x2 * x2, axis=-1, keepdims=True) + EPS)
        h2 = (x2 * r2 * pg_ref[...]).astype(BF16)
        h2_ref[...] = h2
        gate = _sigmoid(jnp.dot(h2, wpg_ref[...], preferred_element_type=F32) + bpg_ref[...])
        gate_ref[...] = gate.astype(BF16)
        e = jnp.dot(p_ref[...].astype(BF16), wpe_ref[...], preferred_element_type=F32)
        e_ref[...] = e.astype(BF16)
        x3 = x2 + gate * e
        if not head:
            x3_ref[...] = x3
            return
        t_ref, fg_ref = refs[18:20]
        loss_ref, dfg_ref = refs[n_in + 6:n_in + 8]
        dx, loss, dfg = _loss_math(x3, t_ref[...], fg_ref[...])

        @pl.when(i == 0)
        def _():
            loss_ref[...] = jnp.zeros_like(loss_ref)
            dfg_ref[...] = jnp.zeros_like(dfg_ref)

        x3_ref[...] = dx
        loss_ref[...] += loss
        dfg_ref[...] += dfg

    row = lambda width, cb_=0: pl.BlockSpec((ts, width), lambda i: (i, cb_))
    prev = lambda cb_: pl.BlockSpec((HALO, 512), lambda i: (jnp.maximum(i * blk_h - 1, 0), cb_))
    vec = lambda width: pl.BlockSpec((1, width), lambda i: (0, 0))
    lvec = lambda width: _layer_rows(layer, 1, width)
    wspec = lambda r_, c_: pl.BlockSpec((r_, c_), lambda i: (0, 0))
    f32o = jax.ShapeDtypeStruct((s, D_MODEL), F32)
    bfo = jax.ShapeDtypeStruct((s, D_MODEL), BF16)
    head_in = [row(D_MODEL), vec(D_MODEL)] if head else []
    head_out = [jax.ShapeDtypeStruct((1, LANES), F32), jax.ShapeDtypeStruct((1, D_MODEL), F32)] if head else []
    return _hosted_call(
        body, comm, name=name, grid=(s // ts,),
        out_shape=(f32o, f32o, bfo, bfo, bfo, bfo, *head_out),
        scratch_shapes=[pltpu.VMEM((HALO, 512), F32)],
        in_specs=[row(D_MODEL), row(512, 0), row(512, 1), row(512, 2), row(512, 3), prev(1), prev(2),
                  row(512), row(512),
                  pl.BlockSpec((None, None, ts, PLE_DIM), lambda i: (layer, 0, i, 0)),
                  _layer_rows(layer, 3, 512), lvec(512), lvec(D_MODEL),
                  wspec(D_MODEL, D_MODEL), lvec(D_MODEL), wspec(D_MODEL, D_MODEL), lvec(D_MODEL),
                  wspec(PLE_DIM, D_MODEL), *head_in],
        out_specs=(*[row(D_MODEL)] * 6, *([vec(LANES), vec(D_MODEL)] if head else [])),
        args=(x, pc, pc, pc, pc, pc, pc, az, ya, p4, cw, cb, bg, wout_full, pg, wpg_full, bpg, wpe_full,
              *(head or ())),
        sem=("arbitrary",) if head else ("parallel",))


def _loss_math(x, target, g):
    r = lax.rsqrt(jnp.mean(x * x, axis=-1, keepdims=True) + EPS)
    xn = x * r
    err = xn * g - target
    per_row = jnp.sum(err * err, axis=-1, keepdims=True)
    loss = jnp.sum(per_row, axis=0, keepdims=True) * (0.5 / D_MODEL)
    dy = err * (1.0 / D_MODEL)
    dg = jnp.sum(dy * xn, axis=0, keepdims=True)
    dxn = dy * g
    return r * (dxn - xn * jnp.mean(dxn * xn, axis=-1, keepdims=True)), loss, dg


def _bwd_mid(dx3, x2, gate, e, pc, az, ya, gated, h2, p4, layer, cw, cb, bg, pg, wpg_full, wout_full, name,
             comm=None):
    s = x2.shape[0]
    ts = min(ROW_TILE, s)
    blk_h = ts // HALO

    def body(dx3_ref, x2_ref, gate_ref, e_ref, cb_ref_, cc_ref, ch_ref, cz_ref, ccp_ref, chp_ref, az_ref, ya_ref,
             gated_ref, h2_ref, p_ref, cw_ref, cbias_ref, bg_ref, pg_ref, wpg_ref, wout_ref,
             dx2_ref, dya_ref, dmisc_ref, dconv_ref, dwout_ref, dwpg_ref, dwpe_ref,
             dbpg_ref, dpg_ref, dbg_ref, dcbias_ref, dcw_ref,
             dgated_ref, halo_ref, acc_out, acc_pg, acc_pe):
        i = pl.program_id(0)

        @pl.when(i == 0)
        def _():
            for ref in (dbpg_ref, dpg_ref, dbg_ref, dcbias_ref, dcw_ref, acc_out, acc_pg, acc_pe):
                ref[...] = jnp.zeros_like(ref)

        lane = lax.broadcasted_iota(jnp.int32, (1, LANES), 1)
        lo = lane < HEAD_DIM
        dx3 = dx3_ref[...]
        gate = gate_ref[...].astype(F32)
        de_b = (dx3 * gate).astype(BF16)
        dgpre = dx3 * e_ref[...].astype(F32) * gate * (1.0 - gate)
        dbpg_ref[...] += jnp.sum(dgpre, axis=0, keepdims=True)
        dgpre_b = dgpre.astype(BF16)
        dh2 = lax.dot_general(dgpre_b, wpg_ref[...], NT, preferred_element_type=F32)
        acc_pe[...] += lax.dot_general(p_ref[...].astype(BF16), de_b, TN, preferred_element_type=F32)
        acc_pg[...] += lax.dot_general(h2_ref[...], dgpre_b, TN, preferred_element_type=F32)

        u, u1, u2 = _conv_taps(cc_ref, ch_ref, ccp_ref, chp_ref, halo_ref, i == 0)
        conv = cbias_ref[...] + cw_ref[0:1, :] * u2 + cw_ref[1:2, :] * u1 + cw_ref[2:3, :] * u
        c_b = cb_ref_[...].astype(F32)
        yc = c_b * conv
        fwd = []
        for sl in range(8):
            cols = slice(LANES * (sl % 4), LANES * (sl % 4 + 1))
            y = yc[:, cols] if sl < 4 else ya_ref[:, cols].astype(F32)
            zc = (cz_ref[:, cols] if sl < 4 else az_ref[:, cols]).astype(F32)
            rg = lax.rsqrt(_group_bcast_sum(y * y, lo) * (1.0 / HEAD_DIM) + EPS)
            sig = _sigmoid(zc)
            fwd.append((rg, y * rg, zc * sig, sig * (1.0 + zc * (1.0 - sig))))

        x2 = x2_ref[...]
        r2 = lax.rsqrt(jnp.mean(x2 * x2, axis=-1, keepdims=True) + EPS)
        xn2 = x2 * r2
        dpg_ref[...] += jnp.sum(dh2 * xn2, axis=0, keepdims=True)
        dxn = dh2 * pg_ref[...]
        dx2 = dx3 + r2 * (dxn - xn2 * jnp.mean(dxn * xn2, axis=-1, keepdims=True))
        dx2_ref[...] = dx2
        dx2_b = dx2.astype(BF16)
        dgated_ref[...] = lax.dot_general(dx2_b, wout_ref[...], NT, preferred_element_type=F32)
        acc_out[...] += lax.dot_general(gated_ref[...], dx2_b, TN, preferred_element_type=F32)

        for sl in range(8):
            cols = slice(LANES * (sl % 4), LANES * (sl % 4 + 1))
            wide = slice(LANES * sl, LANES * (sl + 1))
            rg, yhat, silu, dsilu = fwd[sl]
            bgs = bg_ref[:, wide]
            dgt = dgated_ref[:, wide]
            dyn = dgt * silu
            dzc = dgt * (yhat * bgs) * dsilu
            dbg_ref[:, wide] += jnp.sum(dyn * yhat, axis=0, keepdims=True)
            dyh = dyn * bgs
            dy = rg * (dyh - yhat * (_group_bcast_sum(dyh * yhat, lo) * (1.0 / HEAD_DIM)))
            if sl < 4:
                dconv = dy * c_b[:, cols]
                dmisc_ref[:, cols] = (dy * conv[:, cols]).astype(BF16)
                dmisc_ref[:, 512 + LANES * sl:512 + LANES * (sl + 1)] = dzc.astype(BF16)
                dconv_ref[:, cols] = dconv
                dcbias_ref[:, cols] += jnp.sum(dconv, axis=0, keepdims=True)
                dcw_ref[0:1, cols] += jnp.sum(dconv * u2[:, cols], axis=0, keepdims=True)
                dcw_ref[1:2, cols] += jnp.sum(dconv * u1[:, cols], axis=0, keepdims=True)
                dcw_ref[2:3, cols] += jnp.sum(dconv * u[:, cols], axis=0, keepdims=True)
            else:
                dya_ref[:, cols] = dy.astype(BF16)
                dmisc_ref[:, 1024 + LANES * (sl - 4):1024 + LANES * (sl - 3)] = dzc.astype(BF16)

        @pl.when(i == pl.num_programs(0) - 1)
        def _():
            dwout_ref[...] = acc_out[...].astype(BF16)
            dwpg_ref[...] = acc_pg[...].astype(BF16)
            dwpe_ref[...] = acc_pe[...].astype(BF16)

    row = lambda width, cb_=0: pl.BlockSpec((ts, width), lambda i: (i, cb_))
    prev = lambda cb_: pl.BlockSpec((HALO, 512), lambda i: (jnp.maximum(i * blk_h - 1, 0), cb_))
    vec = lambda width: pl.BlockSpec((1, width), lambda i: (0, 0))
    lvec = lambda width: _layer_rows(layer, 1, width)
    wspec = lambda r_, c_: pl.BlockSpec((r_, c_), lambda i: (0, 0))
    vo = lambda width: jax.ShapeDtypeStruct((1, width), F32)
    sq = jax.ShapeDtypeStruct((D_MODEL, D_MODEL), BF16)
    return _hosted_call(
        body, comm, name=name, grid=(s // ts,), sem=("arbitrary",),
        args=(dx3, x2, gate, e, pc, pc, pc, pc, pc, pc, az, ya, gated, h2, p4, cw, cb, bg, pg, wpg_full, wout_full),
        out_shape=(jax.ShapeDtypeStruct((s, D_MODEL), F32), jax.ShapeDtypeStruct((s, 512), BF16),
                   jax.ShapeDtypeStruct((s, 1536), BF16), jax.ShapeDtypeStruct((s, 512), F32),
                   sq, sq, jax.ShapeDtypeStruct((PLE_DIM, D_MODEL), BF16),
                   vo(D_MODEL), vo(D_MODEL), vo(D_MODEL), vo(512), jax.ShapeDtypeStruct((SUBLANES, 512), F32)),
        in_specs=[row(D_MODEL), row(D_MODEL), row(D_MODEL), row(D_MODEL),
                  row(512, 0), row(512, 1), row(512, 2), row(512, 3), prev(1), prev(2), row(512), row(512),
                  row(D_MODEL), row(D_MODEL),
                  pl.BlockSpec((None, None, ts, PLE_DIM), lambda i: (layer, 0, i, 0)),
                  _layer_rows(layer, 3, 512), lvec(512), lvec(D_MODEL), lvec(D_MODEL),
                  wspec(D_MODEL, D_MODEL), wspec(D_MODEL, D_MODEL)],
        out_specs=(row(D_MODEL), row(512), row(1536), row(512),
                   wspec(D_MODEL, D_MODEL), wspec(D_MODEL, D_MODEL), wspec(PLE_DIM, D_MODEL),
                   vec(D_MODEL), vec(D_MODEL), vec(D_MODEL), vec(512),
                   pl.BlockSpec((SUBLANES, 512), lambda i: (0, 0))),
        scratch_shapes=[pltpu.VMEM((ts, D_MODEL), F32), pltpu.VMEM((HALO, 512), F32),
                        pltpu.VMEM((D_MODEL, D_MODEL), F32), pltpu.VMEM((D_MODEL, D_MODEL), F32),
                        pltpu.VMEM((PLE_DIM, D_MODEL), F32)])


def _attn_bwd(qkv, lsum, nblk, dya, name, comm=None):
    s = qkv.shape[0]
    tq = min(ATTN_TILE, s)
    nq = s // tq
    rc = min(ATTN_ROWS, tq)
    n_rc = tq // rc
    chains = [(r, hh) for r in range(n_rc) for hh in range(2)]

    def body(nblk_ref, q_ref, k_ref, v_ref, lsum_ref, do_ref, dq_ref, dk_ref, dv_ref, dk_acc, dv_acc):
        hp, qi = pl.program_id(0), pl.program_id(1)

        @pl.when(qi == 0)
        def _():
            dk_acc[...] = jnp.zeros_like(dk_acc)
            dv_acc[...] = jnp.zeros_like(dv_acc)

        lo, causal, tri_gt, tri_le = _attn_pieces(tq, rc)
        lane = lax.broadcasted_iota(jnp.int32, (1, LANES), 1)
        qh = _split_heads(q_ref[...], lo)
        doh = _split_heads(do_ref[...].astype(BF16), lo)
        lt = lsum_ref[...]
        ltot_h = (jnp.sum(jnp.where(lane == 0, lt, 0.0), axis=-1, keepdims=True),
                  jnp.sum(jnp.where(lane == HEAD_DIM, lt, 0.0), axis=-1, keepdims=True))
        rows = lambda a_, r: a_[r * rc:(r + 1) * rc]
        qc = {(r, hh): rows(qh[hh], r) for r, hh in chains}
        doc = {(r, hh): rows(doh[hh], r) for r, hh in chains}
        ltot = {(r, hh): rows(ltot_h[hh], r) for r, hh in chains}

        mm = lambda a_, b_: jnp.dot(a_.astype(BF16), b_, preferred_element_type=F32)
        mm_nt = lambda a_, b_: lax.dot_general(a_, b_, NT, preferred_element_type=F32)
        mm_tn = lambda a_, b_: lax.dot_general(a_.astype(BF16), b_, TN, preferred_element_type=F32)
        rowsum = lambda a_: jnp.sum(a_, axis=-1, keepdims=True)

        def block(kb, carry, diag=False):
            start = pl.multiple_of(kb * tq, tq)
            k = k_ref[pl.ds(start, tq), :]
            v = v_ref[pl.ds(start, tq), :]
            kh = _split_heads(k, lo)
            keep = (lambda ch, a_: jnp.where(causal[ch[0]], a_, 0.0)) if diag else (lambda ch, a_: a_)
            z = {ch: mm_nt(qc[ch], k) for ch in chains}
            da = {ch: mm_nt(doc[ch], v) for ch in chains}
            sp = {ch: _softplus(z[ch], causal[ch[0]], diag) for ch in chains}
            later = {ch: mm(sp[ch], tri_gt) for ch in chains}
            walked = {ch: carry[ch[0]][1 + ch[1]] + rowsum(sp[ch]) for ch in chains}
            a = {ch: keep(ch, jnp.exp((z[ch] - sp[ch]) - ((ltot[ch] - walked[ch]) + later[ch]))) for ch in chains}
            g = {ch: a[ch] * da[ch] for ch in chains}
            upto = {ch: mm(g[ch], tri_le) for ch in chains}
            dz = {ch: keep(ch, g[ch] - jnp.exp(z[ch] - sp[ch]) * (carry[ch[0]][3 + ch[1]] + upto[ch])).astype(BF16)
                  for ch in chains}
            dqc = {ch: mm(dz[ch], kh[ch[1]]) for ch in chains}
            dkc = [mm_tn(dz[ch], qc[ch]) for ch in chains]
            dvc = [mm_tn(a[ch], doc[ch]) for ch in chains]
            dk_acc[pl.ds(start, tq), :] += sum(dkc[1:], dkc[0])
            dv_acc[pl.ds(start, tq), :] += sum(dvc[1:], dvc[0])
            return tuple((carry[r][0] + dqc[(r, 0)] + dqc[(r, 1)], walked[(r, 0)], walked[(r, 1)],
                          carry[r][3] + rowsum(g[(r, 0)]), carry[r][4] + rowsum(g[(r, 1)])) for r in range(n_rc))

        zc = jnp.zeros((rc, 1), F32)
        carry = tuple((jnp.zeros((rc, LANES), F32), zc, zc, zc, zc) for _ in range(n_rc))
        near = jnp.maximum(qi - 1, 0)
        first = near - jnp.clip(nblk_ref[hp, qi].astype(jnp.int32), 0, near)
        carry = lax.fori_loop(first, qi, block, carry)
        carry = block(qi, carry, True)
        for r in range(n_rc):
            dq_ref[r * rc:(r + 1) * rc, :] = (carry[r][0] * 0.125).astype(BF16)

        @pl.when(qi == pl.num_programs(1) - 1)
        def _():
            dk_ref[...] = dk_acc[...].astype(BF16)
            dv_ref[...] = dv_acc[...].astype(BF16)

    blk = pl.BlockSpec((tq, LANES), lambda hp, qi: (qi, hp))
    col = pl.BlockSpec((s, LANES), lambda hp, qi: (0, hp))
    o512 = jax.ShapeDtypeStruct((s, D_SB), BF16)
    return _hosted_call(
        body, comm, name=name, grid=(4, nq),
        out_shape=(o512, o512, o512),
        in_specs=[pl.BlockSpec(memory_space=pltpu.SMEM), blk,
                  pl.BlockSpec((s, LANES), lambda hp, qi: (0, 4 + hp)),
                  pl.BlockSpec((s, LANES), lambda hp, qi: (0, 8 + hp)), blk, blk],
        out_specs=(blk, col, col),
        scratch_shapes=[pltpu.VMEM((s, LANES), F32), pltpu.VMEM((s, LANES), F32)],
        args=(nblk, qkv, qkv, qkv, lsum, dya), sem=("parallel", "arbitrary"))


def _bwd_dproj(dmisc, dconv, pc, dq, dk, dv, x, dx2, g, cw, layer, win_full, name, comm=None, h=None,
               h_rows=None):
    s = x.shape[0]
    ts = min(ROW_TILE, s)
    blk8 = ts // SUBLANES
    last8 = s // SUBLANES - 1
    fused = h is not None
    emit_dproj = not fused or h_rows is not None
    dw_rows, h_blk = (D_MODEL, 0) if h_rows is None else h_rows

    def body(*refs):
        (dcb_ref, dcz_ref, daz_ref, dconv_ref, nxt_ref, cc_ref, ch_ref, dq_ref, dk_ref, dv_ref,
         x_ref, dx2_ref, g_ref, cw_ref, w_ref) = refs[:15]
        rest = list(refs[15:])
        h_ref = rest.pop(0) if fused else None
        dproj_ref = rest.pop(0) if emit_dproj else None
        dx_ref, dg_ref = rest.pop(0), rest.pop(0)
        dw_ref = rest.pop(0) if fused else None
        dproj_ref = dproj_ref if emit_dproj else rest.pop(0)
        acc_ref = rest.pop(0) if fused else None
        i = pl.program_id(0)

        @pl.when(i == 0)
        def _():
            dg_ref[...] = jnp.zeros_like(dg_ref)
            if fused:
                acc_ref[...] = jnp.zeros_like(acc_ref)

        keep = jnp.where(i == pl.num_programs(0) - 1, 0.0, 1.0)
        dc = dconv_ref[...]
        n0 = nxt_ref[0:1, :] * keep
        n1 = nxt_ref[1:2, :] * keep
        rowi = lax.broadcasted_iota(jnp.int32, dc.shape, 0)
        dc1 = jnp.where(rowi == ts - 1, n0, pltpu.roll(dc, ts - 1, 0))
        dc2 = jnp.where(rowi == ts - 2, n0, jnp.where(rowi == ts - 1, n1, pltpu.roll(dc, ts - 2, 0)))
        du = cw_ref[2:3, :] * dc + cw_ref[1:2, :] * dc1 + cw_ref[0:1, :] * dc2
        dproj_ref[:, 0:512] = dcb_ref[...]
        dproj_ref[:, 512:1024] = (du * ch_ref[...].astype(F32)).astype(BF16)
        dproj_ref[:, 1024:1536] = (du * cc_ref[...].astype(F32)).astype(BF16)
        dproj_ref[:, 1536:2048] = dcz_ref[...]
        dproj_ref[:, 2048:2560] = dq_ref[...]
        dproj_ref[:, 2560:3072] = dk_ref[...]
        dproj_ref[:, 3072:3584] = dv_ref[...]
        dproj_ref[:, 3584:4096] = daz_ref[...]
        dh = lax.dot_general(dproj_ref[...], w_ref[...], NT, preferred_element_type=F32)
        if fused:
            acc_ref[...] += lax.dot_general(h_ref[...], dproj_ref[...], TN, preferred_element_type=F32)
        x = x_ref[...]
        r = lax.rsqrt(jnp.mean(x * x, axis=-1, keepdims=True) + EPS)
        xn = x * r
        dg_ref[...] += jnp.sum(dh * xn, axis=0, keepdims=True)
        dxn = dh * g_ref[...]
        dx_ref[...] = dx2_ref[...] + r * (dxn - xn * jnp.mean(dxn * xn, axis=-1, keepdims=True))
        if fused:
            @pl.when(i == pl.num_programs(0) - 1)
            def _():
                dw_ref[...] = acc_ref[...].astype(BF16)

    row = lambda width, cb_=0: pl.BlockSpec((ts, width), lambda i: (i, cb_))
    nxt = pl.BlockSpec((SUBLANES, 512), lambda i: (jnp.minimum((i + 1) * blk8, last8), 0))
    vec = lambda width: pl.BlockSpec((1, width), lambda i: (0, 0))
    lvec = lambda width: _layer_rows(layer, 1, width)
    once = dict(pipeline_mode=pl.Buffered(1)) if fused else {}
    whole = lambda rows_: pl.BlockSpec((rows_, N_IN), lambda i: (0, 0), **once)
    in_specs = [row(512, 0), row(512, 1), row(512, 2), row(512), nxt, row(512, 1), row(512, 2),
                row(512), row(512), row(512), row(D_MODEL), row(D_MODEL), lvec(D_MODEL),
                _layer_rows(layer, 3, 512), whole(D_MODEL)]
    args = [dmisc, dmisc, dmisc, dconv, dconv, pc, pc, dq, dk, dv, x, dx2, g, cw, win_full]
    out_shape = [jax.ShapeDtypeStruct((s, D_MODEL), F32), jax.ShapeDtypeStruct((1, D_MODEL), F32)]
    out_specs = [row(D_MODEL), vec(D_MODEL)]
    scratch = []
    if emit_dproj:
        out_shape.insert(0, jax.ShapeDtypeStruct((s, N_IN), BF16))
        out_specs.insert(0, row(N_IN))
    else:
        scratch.append(pltpu.VMEM((ts, N_IN), BF16))
    if fused:
        in_specs.append(row(dw_rows, h_blk))
        args.append(h)
        out_shape.append(jax.ShapeDtypeStruct((dw_rows, N_IN), BF16))
        out_specs.append(whole(dw_rows))
        scratch.append(pltpu.VMEM((dw_rows, N_IN), F32))
    return _hosted_call(
        body, comm, name=name, grid=(s // ts,), out_shape=tuple(out_shape), in_specs=in_specs,
        out_specs=tuple(out_specs), scratch_shapes=scratch, args=tuple(args), sem=("arbitrary",))


def _atb(a, b, name, a_cols=None, comm=None):
    s, n = b.shape
    m, a_blk = (a.shape[-1], 0) if a_cols is None else a_cols
    ts = min(512, s)
    tn = min(2048, n)
    a_spec = pl.BlockSpec((ts, m), lambda j, i: (i, a_blk))

    def body(a_ref, b_ref, o_ref, acc_ref):
        i = pl.program_id(1)

        @pl.when(i == 0)
        def _():
            acc_ref[...] = jnp.zeros_like(acc_ref)

        acc_ref[...] += lax.dot_general(a_ref[...].astype(BF16), b_ref[...], TN, preferred_element_type=F32)

        @pl.when(i == pl.num_programs(1) - 1)
        def _():
            o_ref[...] = acc_ref[...].astype(BF16)

    (out,), got = _hosted_call(
        body, comm, name=name, grid=(n // tn, s // ts),
        out_shape=(jax.ShapeDtypeStruct((m, n), BF16),),
        in_specs=[a_spec, pl.BlockSpec((ts, tn), lambda j, i: (i, j))],
        out_specs=(pl.BlockSpec((m, tn), lambda j, i: (0, j)),),
        scratch_shapes=[pltpu.VMEM((m, tn), F32)],
        args=(a, b), sem=("parallel", "arbitrary"))
    return out, got


def _adamw_math(w, g, m, v):
    m2 = ADAM_B1 * m + (1.0 - ADAM_B1) * g
    v2 = ADAM_B2 * v + (1.0 - ADAM_B2) * (g * g)
    m_hat = m2 / (1.0 - ADAM_B1 ** ADAM_STEP)
    v_hat = v2 / (1.0 - ADAM_B2 ** ADAM_STEP)
    delta = -ADAM_LR * (m_hat / (jnp.sqrt(v_hat) + ADAM_EPS) + ADAM_WD * w)
    return delta, m2, v2


def _adamw_sum8(pieces, w, m, v, name):
    _, rows, cols = w.shape
    tr = min([rows, 256] + [pc_[0].shape[1] for pc_ in pieces])
    n_tiles = rows // tr
    n_p = len(pieces)
    spans = [(layer, row0 // tr, arr.shape[1] // tr) for arr, layer, row0 in pieces]

    def body(*refs):
        p_refs = refs[:n_p]
        w_ref, m_ref, v_ref, g_ref, d_ref, m2_ref, v2_ref = refs[n_p:]
        l, i = pl.program_id(0), pl.program_id(1)

        def run(p_ref):
            g = p_ref[0].astype(F32)
            for d in range(1, N_DEV):
                g = g + p_ref[d].astype(F32)
            g_ref[...] = g
            d_ref[...], m2_ref[...], v2_ref[...] = _adamw_math(w_ref[...], g, m_ref[...], v_ref[...])

        for p_ref, (layer, t0, nt) in zip(p_refs, spans):
            mine = jnp.logical_and(l == layer, jnp.logical_and(i >= t0, i < t0 + nt))
            pl.when(mine)(lambda p_ref=p_ref: run(p_ref))

    def piece_spec(layer, t0, nt):
        return pl.BlockSpec((N_DEV, tr, cols),
                            lambda l, i: (0, jnp.clip(jnp.where(l == layer, i - t0, jnp.where(l < layer, 0, nt - 1)),
                                                      0, nt - 1), 0))

    tile = pl.BlockSpec((None, tr, cols), lambda l, i: (l, i, 0))
    o = jax.ShapeDtypeStruct((DEPTH, rows, cols), F32)
    return _call(
        body, name=name, grid=(DEPTH, n_tiles),
        out_shape=(o, o, o, o),
        in_specs=[*[piece_spec(*sp) for sp in spans], tile, tile, tile],
        out_specs=(tile, tile, tile, tile),
        compiler_params=_params(("arbitrary", "arbitrary"), VMEM_LIMIT),
    )(*[pc_[0] for pc_ in pieces], w, m, v)


def _small_update(blk, layered, final, conv, loss_parts):
    n_l = len(layered)
    ins = [a for item in layered for a in item] + list(final) + list(conv) + [loss_parts]
    shapes = [item[2].shape for item in layered] + [final[1].shape, conv[2].shape]
    out_shape = [jax.ShapeDtypeStruct(sh, F32) for sh in shapes for _ in range(4)]
    out_shape.append(jax.ShapeDtypeStruct((1, LANES), F32))

    def body(*refs):
        blk_ref, refs = refs[0], refs[1:]
        in_refs, out_refs, pick_ref = refs[:len(ins)], refs[len(ins):-1], refs[-1]

        def total(ref):
            g = ref[0]
            for d in range(1, N_DEV):
                g = g + ref[d]
            return g

        def update(k, at, g, w_ref, m_ref, v_ref):
            g_ref, d_ref, m2_ref, v2_ref = out_refs[4 * k:4 * k + 4]
            g_ref[at] = g
            d_ref[at], m2_ref[at], v2_ref[at] = _adamw_math(w_ref[at], g, m_ref[at], v_ref[at])

        for k in range(n_l):
            p0, p1, w_ref, m_ref, v_ref = in_refs[5 * k:5 * k + 5]
            for layer, parts in enumerate((p0, p1)):
                update(k, pl.ds(layer, 1), total(parts), w_ref, m_ref, v_ref)
        pf, w_ref, m_ref, v_ref = in_refs[5 * n_l:5 * n_l + 4]
        update(n_l, pl.ds(0, 1), total(pf), w_ref, m_ref, v_ref)
        c0, c1, w_ref, m_ref, v_ref = in_refs[5 * n_l + 4:5 * n_l + 9]
        for layer, parts in enumerate((c0, c1)):
            g8 = total(parts)
            mine = jnp.zeros((SUBLANES, HEAD_DIM), F32)
            for j in range(N_DEV):
                mine = mine + jnp.where(blk_ref[0] == j, g8[:, HEAD_DIM * j:HEAD_DIM * (j + 1)], 0.0)
            pick_ref[...] = mine
            update(n_l + 1, layer, pick_ref[0:3, :], w_ref, m_ref, v_ref)
        out_refs[-1][...] = total(in_refs[-1])

    whole = lambda shape: pl.BlockSpec(shape, lambda: (0,) * len(shape))
    outs = _call(
        body, name="adamw_small",
        out_shape=tuple(out_shape),
        in_specs=[pl.BlockSpec(memory_space=pltpu.SMEM)] + [whole(a.shape) for a in ins],
        out_specs=tuple(whole(o.shape) for o in out_shape),
        scratch_shapes=[pltpu.VMEM((SUBLANES, HEAD_DIM), F32)],
    )(blk, *ins)
    return [outs[4 * k:4 * k + 4] for k in range(n_l + 2)], outs[-1]


def kernel(x, p, norm_g, w_in, conv_w, conv_b, branch_g, w_out, ple_norm_g, w_pg, b_pg, w_pe, final_g, loss_target, m_norm_g, m_w_in, m_conv_w, m_conv_b, m_branch_g, m_w_out, m_ple_norm_g, m_w_pg, m_b_pg, m_w_pe, m_final_g, v_norm_g, v_w_in, v_conv_w, v_conv_b, v_branch_g, v_w_out, v_ple_norm_g, v_w_pg, v_b_pg, v_w_pe, v_final_g):
    s = x.shape[1]
    x0 = x.reshape(s, D_MODEL)
    target = loss_target.reshape(s, D_MODEL)
    me_blk = _my_block()

    win_s, wout_s, wpg_s, wpe_s = _cast_bf16(
        [w_in.reshape(DEPTH * D_MODEL, 512), w_out.reshape(DEPTH * 128, D_MODEL),
         w_pg.reshape(DEPTH * 128, D_MODEL), w_pe.reshape(DEPTH * PLE_DIM, 128)], "cast_weights")
    win_s, wout_s = win_s.reshape(DEPTH, D_MODEL, 512), wout_s.reshape(DEPTH, 128, D_MODEL)
    wpg_s, wpe_s = wpg_s.reshape(DEPTH, 128, D_MODEL), wpe_s.reshape(DEPTH, PLE_DIM, 128)
    cw_s = jnp.zeros((SUBLANES, LANES), F32).at[:DEPTH * 3, :HEAD_DIM].set(conv_w.reshape(DEPTH * 3, HEAD_DIM))
    bf = lambda r_, c_: jax.ShapeDtypeStruct((r_, c_), BF16)
    w_items = lambda l: [(wout_s[l], bf(D_MODEL, D_MODEL), "rows128"), (wpg_s[l], bf(D_MODEL, D_MODEL), "rows128"),
                         (wpe_s[l], bf(PLE_DIM, D_MODEL), "cols128")]
    win_f = [None] * DEPTH
    win_f[0], cw_all = _comm_call(_gather_comm([
        (win_s[0], bf(D_MODEL, N_IN), "cols512"),
        (cw_s, jax.ShapeDtypeStruct((N_DEV, SUBLANES, LANES), F32), "slot")]), "gather_w_in_0")
    cw_full = jnp.transpose(cw_all[:, :DEPTH * 3, :HEAD_DIM].reshape(N_DEV, DEPTH, 3, HEAD_DIM), (1, 2, 0, 3))
    cw_full = cw_full.reshape(DEPTH, 3, D_CONV)
    gather_rest_0 = _gather_comm(w_items(0))
    gather_win_1 = _gather_comm([(win_s[1], bf(D_MODEL, N_IN), "cols512")])
    gather_rest_1 = _gather_comm(w_items(1))

    norm3, convb3, branch3, ple3, bpg3 = [a.reshape(DEPTH, 1, -1) for a in (norm_g, conv_b, branch_g, ple_norm_g, b_pg)]

    saved = []
    xl = x0
    wout_f, wpg_f, wpe_f = [None] * DEPTH, [None] * DEPTH, [None] * DEPTH
    for l in range(DEPTH):
        (h, pc, qkv, az), got = _fwd_in(xl, norm3, l, win_f[l], f"fwd_in_{l}",
                                        comm=gather_rest_0 if l == 0 else None)
        if l == 0:
            wout_f[0], wpg_f[0], wpe_f[0] = got
        (ya, lsum, nblk), got = _attn_fwd(qkv, f"attn_fwd_{l}", comm=gather_win_1 if l == 0 else None)
        if l == 0:
            (win_f[1],) = got
        last = l == DEPTH - 1
        outs, got = _fwd_mid(
            xl, pc, az, ya, p, l, cw_full, convb3, branch3, wout_f[l],
            ple3, wpg_f[l], bpg3, wpe_f[l], f"fwd_mid_{l}",
            comm=gather_rest_1 if l == 0 else None, head=(target, final_g[None, :]) if last else None)
        x2, x3, gated, h2, gate, e = outs[:6]
        if l == 0:
            wout_f[1], wpg_f[1], wpe_f[1] = got
        saved.append(dict(x=xl, h=h, pc=pc, qkv=qkv, az=az, ya=ya, lsum=lsum, nblk=nblk, x2=x2, gated=gated, h2=h2,
                          gate=gate, e=e))
        xl = x3

    dx, (loss_acc, d_final_g) = xl, outs[6:]

    dwin, dwout, dwpg, dwpe = [None] * DEPTH, [None] * DEPTH, [None] * DEPTH, [None] * DEPTH
    small = dict(norm_g=[None] * DEPTH, conv_b=[None] * DEPTH, branch_g=[None] * DEPTH,
                 ple_norm_g=[None] * DEPTH, b_pg=[None] * DEPTH, conv_w=[None] * DEPTH)
    slot = lambda r_, c_: jax.ShapeDtypeStruct((r_, c_), BF16)
    half = D_MODEL // 2
    r_in1, r_out, r_pg, r_pe = None, [None] * DEPTH, [None] * DEPTH, [None] * DEPTH

    def rest_items(l):
        return [(dwout[l], slot(128, D_MODEL), "rows128"), (dwpg[l], slot(128, D_MODEL), "rows128"),
                (dwpe[l], slot(PLE_DIM, 128), "cols128")]

    for l in reversed(range(DEPTH)):
        sv = saved[l]
        ride = _exchange_comm(rest_items(1)) if l == 0 else None
        (dx2, dya, dmisc, dconv, dwout[l], dwpg[l], dwpe[l], d_bpg, d_pg, d_bg, d_cbias, d_cw), got = _bwd_mid(
            dx, sv["x2"], sv["gate"], sv["e"], sv["pc"], sv["az"], sv["ya"], sv["gated"], sv["h2"], p, l,
            cw_full, convb3, branch3, ple3, wpg_f[l], wout_f[l], f"bwd_mid_{l}",
            comm=ride)
        if l == 0:
            r_out[1], r_pg[1], r_pe[1] = got
        ride = _exchange_comm([(dwin[1], slot(D_MODEL, 512), "cols512")] + rest_items(0)) if l == 0 else None
        (dq, dk, dv), got = _attn_bwd(sv["qkv"], sv["lsum"], sv["nblk"], dya, f"attn_bwd_{l}", comm=ride)
        if l == 0:
            r_in1, r_out[0], r_pg[0], r_pe[0] = got
        dproj_args = (dmisc, dconv, sv["pc"], dq, dk, dv, sv["x"], dx2, norm3, cw_full, l, win_f[l])
        if l == 1:
            (dx, d_ng, dwin[1]), _ = _bwd_dproj(*dproj_args, "bwd_dproj_dw_1", h=sv["h"])
        else:
            (dproj, dx, d_ng, dwin_top), _ = _bwd_dproj(*dproj_args, "bwd_dproj_dw_0", h=sv["h"], h_rows=(half, 0))
            dwin_bot, (r_in0_top,) = _atb(sv["h"], dproj, "dw_in_0_bottom", a_cols=(half, 1),
                                          comm=_exchange_comm([(dwin_top, slot(half, 512), "cols512")]))
        small["norm_g"][l], small["conv_b"][l], small["branch_g"][l] = d_ng, d_cbias, d_bg
        small["ple_norm_g"][l], small["b_pg"][l], small["conv_w"][l] = d_pg, d_bpg, d_cw
    grad_x = dx.reshape(1, s, D_MODEL)

    names = ["norm_g", "conv_b", "branch_g", "ple_norm_g", "b_pg", "conv_w"]
    small_list = [small[n][l] for n in names for l in range(DEPTH)] + [d_final_g, loss_acc]
    got = _comm_call(_exchange_comm(
        [(dwin_bot, slot(half, 512), "cols512")]
        + [(a, jax.ShapeDtypeStruct(a.shape, F32), "slot") for a in small_list]), "exchange_last")
    r_in0_bot, r_small = got[0], got[1:]

    per_layer = lambda r: [(r[0], 0, 0), (r[1], 1, 0)]
    g_win, d_win, m_win, v_win = _adamw_sum8([(r_in0_top, 0, 0), (r_in0_bot, 0, half), (r_in1, 1, 0)],
                                             w_in, m_w_in, v_w_in, "adamw_w_in")
    g_wout, d_wout, m_wout, v_wout = _adamw_sum8(per_layer(r_out), w_out, m_w_out, v_w_out, "adamw_w_out")
    g_wpg, d_wpg, m_wpg, v_wpg = _adamw_sum8(per_layer(r_pg), w_pg, m_w_pg, v_w_pg, "adamw_w_pg")
    g_wpe, d_wpe, m_wpe, v_wpe = _adamw_sum8(per_layer(r_pe), w_pe, m_w_pe, v_w_pe, "adamw_w_pe")

    layered = [(norm_g, m_norm_g, v_norm_g), (conv_b, m_conv_b, v_conv_b), (branch_g, m_branch_g, v_branch_g),
               (ple_norm_g, m_ple_norm_g, v_ple_norm_g), (b_pg, m_b_pg, v_b_pg)]
    row = lambda a: a.reshape(1, -1)
    upd, loss_row = _small_update(
        jnp.reshape(me_blk, (1,)).astype(jnp.int32),
        [(r_small[2 * k], r_small[2 * k + 1], *wmv) for k, wmv in enumerate(layered)],
        (r_small[12], row(final_g), row(m_final_g), row(v_final_g)),
        (r_small[10], r_small[11], conv_w, m_conv_w, v_conv_w), r_small[13])
    loss = loss_row[0, 0]
    upd[5] = [a.reshape(-1) for a in upd[5]]

    big = {1: (g_win, d_win, m_win, v_win), 5: (g_wout, d_wout, m_wout, v_wout), 7: (g_wpg, d_wpg, m_wpg, v_wpg),
           9: (g_wpe, d_wpe, m_wpe, v_wpe)}
    small_at = {0: 0, 2: 6, 3: 1, 4: 2, 6: 3, 8: 4, 10: 5}
    per_kind = [[(big[i] if i in big else upd[small_at[i]])[j] for i in range(11)] for j in range(4)]
    return (loss, grad_x, *per_kind[0], *per_kind[1], *per_kind[2], *per_kind[3])
```

```python
import jax
import jax.numpy as jnp
from jax import lax
from jax.experimental import pallas as pl
from jax.experimental.pallas import tpu as pltpu

F32 = jnp.float32
BF16 = jnp.bfloat16

D_MODEL = 1024
D_CONV = 512
D_SB = 512
N_IN = 4096
HEAD_DIM = 64
PLE_DIM = 256
DEPTH = 2
EPS = 1e-6
ADAM_LR = 0.001
ADAM_B1 = 0.9
ADAM_B2 = 0.999
ADAM_EPS = 1e-08
ADAM_WD = 0.01
ADAM_STEP = 10

LANES = 128
SUBLANES = 8
VMEM_BYTES_V7X = 64 * 1024 * 1024
VMEM_LIMIT = VMEM_BYTES_V7X - 8 * 1024 * 1024

N_DEV = 8
ROW_TILE = 256
FWD_ROW_TILE = 512
ATTN_TILE = 256

NT = (((1,), (1,)), ((), ()))
TN = (((0,), (0,)), ((), ()))


def _call(body, **kw):
    return pl.pallas_call(body, **kw)


def _params(sem=None, vmem=None):
    return pltpu.CompilerParams(dimension_semantics=sem, vmem_limit_bytes=vmem)


def _sigmoid(z):
    return 0.5 * jnp.tanh(0.5 * z) + 0.5


def _group_bcast_sum(a, lo):
    s_lo = jnp.sum(jnp.where(lo, a, 0.0), axis=-1, keepdims=True)
    s_hi = jnp.sum(jnp.where(lo, 0.0, a), axis=-1, keepdims=True)
    return jnp.where(lo, s_lo, s_hi)


def _layer_rows(layer, rows, width):
    return pl.BlockSpec((None, rows, width), lambda i: (layer, 0, 0))


def _my_block():
    return 4 * lax.axis_index("x") + 2 * lax.axis_index("y") + lax.axis_index("c")


def _cast_bf16(arrays, name):
    n = len(arrays)

    def body(*refs):
        for a_ref, o_ref in zip(refs[:n], refs[n:]):
            o_ref[...] = a_ref[...].astype(BF16)

    whole = lambda a: pl.BlockSpec(a.shape, lambda: (0, 0))
    return _call(
        body, name=name,
        out_shape=tuple(jax.ShapeDtypeStruct(a.shape, BF16) for a in arrays),
        in_specs=[whole(a) for a in arrays], out_specs=tuple(whole(a) for a in arrays),
        compiler_params=_params(None, VMEM_LIMIT),
    )(*arrays)


class _Comm:
    def __init__(self, inputs, out_shapes, scratch, begin, middle, finish):
        self.inputs, self.out_shapes, self.scratch = list(inputs), list(out_shapes), list(scratch)
        self.begin, self.middle, self.finish = begin, middle, finish


def _slab(kind, ref, blk):
    if kind == "cols512":
        return ref.at[:, pl.ds(blk * 512, 512)]
    if kind == "rows128":
        return ref.at[pl.ds(blk * 128, 128), :]
    if kind == "cols128":
        return ref.at[:, pl.ds(blk * 128, 128)]
    return ref.at[blk]


def _gather_comm(items):
    n_t = len(items)
    kinds = [it[2] for it in items]

    def ctx(ins, outs, sems):
        send_sems, recv_sems, local_sems = sems
        x, y, c = lax.axis_index("x"), lax.axis_index("y"), lax.axis_index("c")
        me, sibling = (x, y, c), (x, y, 1 - c)
        chips = [(1 - x, y), (x, 1 - y), (1 - x, 1 - y)]

        def place(t, dev):
            return _slab(kinds[t], outs[t], 4 * dev[0] + 2 * dev[1] + dev[2])

        def copy(t, k, block, to, own=False):
            return pltpu.make_async_remote_copy(
                src_ref=ins[t] if own else place(t, block), dst_ref=place(t, block),
                send_sem=send_sems.at[t, k], recv_sem=recv_sems.at[t, k],
                device_id=to, device_id_type=pl.DeviceIdType.MESH)

        mine = [pltpu.make_async_copy(ins[t], place(t, me), local_sems.at[t]) for t in range(n_t)]
        first = []
        for t in range(n_t):
            first.append(copy(t, 0, me, sibling, own=True))
            first += [copy(t, 1 + j, me, (*chip, c), own=True) for j, chip in enumerate(chips)]
        passed = [copy(t, 4 + j, (*chip, c), sibling) for j, chip in enumerate(chips) for t in range(n_t)]
        landed = [copy(t, 1 + j, (*chip, c), me) for j, chip in enumerate(chips) for t in range(n_t)]
        from_sibling = []
        for t in range(n_t):
            from_sibling.append(copy(t, 0, sibling, me))
            from_sibling += [copy(t, 4 + j, (*chip, 1 - c), me) for j, chip in enumerate(chips)]
        return mine, first, landed, passed, from_sibling

    def begin(ins, outs, sems):
        mine, first, _, _, _ = ctx(ins, outs, sems)
        for cp in mine + first:
            cp.start()

    def middle(ins, outs, sems):
        _, _, landed, passed, _ = ctx(ins, outs, sems)
        for got, fwd in zip(landed, passed):
            got.wait_recv()
            fwd.start()

    def finish(ins, outs, sems):
        mine, first, _, passed, from_sibling = ctx(ins, outs, sems)
        for cp in from_sibling:
            cp.wait_recv()
        for cp in first + passed:
            cp.wait_send()
        for cp in mine:
            cp.wait()

    scratch = [pltpu.SemaphoreType.DMA((n_t, 7)), pltpu.SemaphoreType.DMA((n_t, 7)), pltpu.SemaphoreType.DMA((n_t,))]
    return _Comm([it[0] for it in items], [it[1] for it in items], scratch, begin, middle, finish)


def _exchange_comm(items):
    n_t = len(items)
    kinds = [it[2] for it in items]

    def ctx(ins, outs, sems):
        send_sems, recv_sems, local_sems = sems
        x, y, c = lax.axis_index("x"), lax.axis_index("y"), lax.axis_index("c")
        me_blk = 4 * x + 2 * y + c

        def src(t, blk):
            return ins[t] if kinds[t] == "slot" else _slab(kinds[t], ins[t], blk)

        local = [pltpu.make_async_copy(src(t, me_blk), outs[t].at[me_blk], local_sems.at[t]) for t in range(n_t)]
        remote = []
        for k in range(1, N_DEV):
            px = 1 - x if k & 4 else x
            py = 1 - y if k & 2 else y
            pc_ = 1 - c if k & 1 else c
            for t in range(n_t):
                remote.append(pltpu.make_async_remote_copy(
                    src_ref=src(t, 4 * px + 2 * py + pc_), dst_ref=outs[t].at[me_blk],
                    send_sem=send_sems.at[k - 1, t], recv_sem=recv_sems.at[k - 1, t],
                    device_id=(px, py, pc_), device_id_type=pl.DeviceIdType.MESH))
        return local, remote

    def begin(ins, outs, sems):
        local, remote = ctx(ins, outs, sems)
        for cp in local + remote:
            cp.start()

    def finish(ins, outs, sems):
        local, remote = ctx(ins, outs, sems)
        for cp in remote:
            cp.wait_recv()
        for cp in remote:
            cp.wait_send()
        for cp in local:
            cp.wait()

    scratch = [pltpu.SemaphoreType.DMA((N_DEV - 1, n_t)), pltpu.SemaphoreType.DMA((N_DEV - 1, n_t)),
               pltpu.SemaphoreType.DMA((n_t,))]
    out_shapes = [jax.ShapeDtypeStruct((N_DEV, *it[1].shape), it[1].dtype) for it in items]
    return _Comm([it[0] for it in items], out_shapes, scratch, begin, None, finish)


def _comm_call(comm, name):
    n_in, n_out = len(comm.inputs), len(comm.out_shapes)

    def body(*refs):
        ins, outs, sems = refs[:n_in], refs[n_in:n_in + n_out], refs[n_in + n_out:]
        comm.begin(ins, outs, sems)
        if comm.middle is not None:
            comm.middle(ins, outs, sems)
        comm.finish(ins, outs, sems)

    any_spec = pl.BlockSpec(memory_space=pl.ANY)
    return _call(body, name=name, out_shape=tuple(comm.out_shapes), in_specs=[any_spec] * n_in,
                 out_specs=[any_spec] * n_out, scratch_shapes=comm.scratch)(*comm.inputs)


def _hosted(body, n_in, n_out, comm, first, last, middle):
    if comm is None:
        return lambda *refs: body(*refs)
    n_ci, n_co, n_cs = len(comm.inputs), len(comm.out_shapes), len(comm.scratch)

    def wrapped(*refs):
        ins, cin = refs[:n_in], refs[n_in:n_in + n_ci]
        o0 = n_in + n_ci
        outs, cout = refs[o0:o0 + n_out], refs[o0 + n_out:o0 + n_out + n_co]
        scr, csem = refs[o0 + n_out + n_co:len(refs) - n_cs], refs[len(refs) - n_cs:]
        pl.when(first())(lambda: comm.begin(cin, cout, csem))
        body(*ins, *outs, *scr)
        if comm.middle is not None:
            pl.when(middle())(lambda: comm.middle(cin, cout, csem))
        pl.when(last())(lambda: comm.finish(cin, cout, csem))

    return wrapped


def _hosted_call(body, comm, *, name, grid, out_shape, in_specs, out_specs, args, scratch_shapes=(), sem=None):
    nd = len(grid)
    first, last, middle = _at_first(nd), _at_last(nd), _at_middle(nd)
    if comm is not None:
        sem = ("arbitrary",) * nd
    n_in, n_out = len(in_specs), len(out_shape)
    any_spec = pl.BlockSpec(memory_space=pl.ANY)
    c_in = [] if comm is None else comm.inputs
    c_out = [] if comm is None else comm.out_shapes
    c_scr = [] if comm is None else comm.scratch
    outs = _call(
        _hosted(body, n_in, n_out, comm, first, last, middle), name=name, grid=grid,
        out_shape=(*out_shape, *c_out),
        in_specs=[*in_specs, *[any_spec] * len(c_in)],
        out_specs=(*out_specs, *[any_spec] * len(c_out)),
        scratch_shapes=[*scratch_shapes, *c_scr],
        compiler_params=_params(sem, VMEM_LIMIT),
    )(*args, *c_in)
    return outs[:n_out], outs[n_out:]


def _grid_step(ndim):
    i, n = pl.program_id(0), pl.num_programs(0)
    for d in range(1, ndim):
        i, n = i * pl.num_programs(d) + pl.program_id(d), n * pl.num_programs(d)
    return i, n


def _at_first(ndim):
    return lambda: _grid_step(ndim)[0] == 0


def _at_last(ndim):
    def pred():
        i, n = _grid_step(ndim)
        return i == n - 1
    return pred


def _at_middle(ndim):
    def pred():
        i, n = _grid_step(ndim)
        return i == (3 * n) // 4
    return pred


def _fwd_in(x, g, layer, w_full, name, comm=None):
    s = x.shape[0]
    ts = min(FWD_ROW_TILE, s)

    def body(x_ref, g_ref, w_ref, h_ref, pc_ref, qkv_ref, az_ref):
        xf = x_ref[...]
        r = lax.rsqrt(jnp.mean(xf * xf, axis=-1, keepdims=True) + EPS)
        h = (xf * r * g_ref[...]).astype(BF16)
        h_ref[...] = h
        pc_ref[...] = jnp.dot(h, w_ref[:, 0:2048], preferred_element_type=F32).astype(BF16)
        q = jnp.dot(h, w_ref[:, 2048:2560], preferred_element_type=F32)
        qkv_ref[:, 0:512] = (q * 0.125).astype(BF16)
        qkv_ref[:, 512:1536] = jnp.dot(h, w_ref[:, 2560:3584], preferred_element_type=F32).astype(BF16)
        az_ref[...] = jnp.dot(h, w_ref[:, 3584:4096], preferred_element_type=F32).astype(BF16)

    row = lambda width: pl.BlockSpec((ts, width), lambda i: (i, 0))
    return _hosted_call(
        body, comm, name=name, grid=(s // ts,),
        out_shape=(jax.ShapeDtypeStruct((s, D_MODEL), BF16), jax.ShapeDtypeStruct((s, 2048), BF16),
                   jax.ShapeDtypeStruct((s, 1536), BF16), jax.ShapeDtypeStruct((s, 512), BF16)),
        in_specs=[row(D_MODEL), _layer_rows(layer, 1, D_MODEL),
                  pl.BlockSpec((D_MODEL, N_IN), lambda i: (0, 0))],
        out_specs=(row(D_MODEL), row(2048), row(1536), row(512)),
        args=(x, g, w_full), sem=("parallel",))


ATTN_ROWS = 128
ATTN_DONE = 104.0


def _attn_pieces(tq, rc):
    lane = lax.broadcasted_iota(jnp.int32, (1, LANES), 1)
    lo = lane < HEAD_DIM
    row = lax.broadcasted_iota(jnp.int32, (tq, tq), 0)
    col = lax.broadcasted_iota(jnp.int32, (tq, tq), 1)
    tri_gt = jnp.where(row > col, 1.0, 0.0).astype(BF16)
    tri_le = jnp.where(row <= col, 1.0, 0.0).astype(BF16)
    rrow = lax.broadcasted_iota(jnp.int32, (rc, tq), 0)
    rcol = lax.broadcasted_iota(jnp.int32, (rc, tq), 1)
    causal = [rcol < rrow + r * rc for r in range(tq // rc)]
    return lo, causal, tri_gt, tri_le


def _split_heads(a, lo):
    z = jnp.zeros_like(a)
    return (jnp.where(lo, a, z), jnp.where(lo, z, a))


def _softplus(z, causal, diag):
    neg_abs = lax.bitcast_convert_type(lax.bitcast_convert_type(z, jnp.uint32) | jnp.uint32(0x80000000), F32)
    sp = jnp.maximum(z, 0.0) + jnp.log(1.0 + jnp.exp(neg_abs))
    if diag:
        sp = jnp.where(causal, sp, 0.0)
    return sp


def _attn_fwd(qkv, name, comm=None):
    s = qkv.shape[0]
    tq = min(ATTN_TILE, s)
    nq = s // tq
    rc = min(ATTN_ROWS, tq)
    n_rc = tq // rc
    chains = [(r, hh) for r in range(n_rc) for hh in range(2)]

    def body(q_ref, k_ref, v_ref, o_ref, lsum_ref, nblk_ref):
        hp, qi = pl.program_id(0), pl.program_id(1)
        lo, causal, tri_gt, _ = _attn_pieces(tq, rc)
        qh = _split_heads(q_ref[...], lo)
        qc = {(r, hh): qh[hh][r * rc:(r + 1) * rc] for r, hh in chains}

        mm = lambda a_, b_: jnp.dot(a_.astype(BF16), b_, preferred_element_type=F32)
        rowsum = lambda a_: jnp.sum(a_, axis=-1, keepdims=True)

        def block(kb, carry):
            start = pl.multiple_of(kb * tq, tq)
            k = k_ref[pl.ds(start, tq), :]
            vh = _split_heads(v_ref[pl.ds(start, tq), :], lo)
            z = {ch: lax.dot_general(qc[ch], k, NT, preferred_element_type=F32) for ch in chains}
            sp = {ch: _softplus(z[ch], None, False) for ch in chains}
            later = {ch: mm(sp[ch], tri_gt) for ch in chains}
            a = {ch: jnp.exp((z[ch] - sp[ch]) - (carry[ch[0]][1 + ch[1]] + later[ch])) for ch in chains}
            pv = {ch: mm(a[ch], vh[ch[1]]) for ch in chains}
            return tuple((carry[r][0] + pv[(r, 0)] + pv[(r, 1)],
                          carry[r][1] + rowsum(sp[(r, 0)]), carry[r][2] + rowsum(sp[(r, 1)])) for r in range(n_rc))

        def first_two(prev_ok):
            d0 = pl.multiple_of(qi * tq, tq)
            p0 = pl.multiple_of(jnp.maximum(qi - 1, 0) * tq, tq)
            k_d, k_p = k_ref[pl.ds(d0, tq), :], k_ref[pl.ds(p0, tq), :]
            vh_d = _split_heads(v_ref[pl.ds(d0, tq), :], lo)
            vh_p = _split_heads(v_ref[pl.ds(p0, tq), :], lo)
            z_d = {ch: lax.dot_general(qc[ch], k_d, NT, preferred_element_type=F32) for ch in chains}
            z_p = {ch: lax.dot_general(qc[ch], k_p, NT, preferred_element_type=F32) for ch in chains}
            sp_d = {ch: _softplus(z_d[ch], causal[ch[0]], True) for ch in chains}
            sp_raw = {ch: _softplus(z_p[ch], None, False) for ch in chains}
            sp_p = {ch: jnp.where(prev_ok, sp_raw[ch], 0.0) for ch in chains}
            later_d = {ch: mm(sp_d[ch], tri_gt) for ch in chains}
            later_p = {ch: mm(sp_p[ch], tri_gt) for ch in chains}
            c_d = {ch: rowsum(sp_d[ch]) for ch in chains}
            a_d = {ch: jnp.where(causal[ch[0]], jnp.exp((z_d[ch] - sp_d[ch]) - later_d[ch]), 0.0) for ch in chains}
            a_p = {ch: jnp.where(prev_ok, jnp.exp((z_p[ch] - sp_raw[ch]) - (c_d[ch] + later_p[ch])), 0.0)
                   for ch in chains}
            pv = {ch: mm(a_d[ch], vh_d[ch[1]]) + mm(a_p[ch], vh_p[ch[1]]) for ch in chains}
            return tuple((pv[(r, 0)] + pv[(r, 1)],
                          c_d[(r, 0)] + rowsum(sp_p[(r, 0)]), c_d[(r, 1)] + rowsum(sp_p[(r, 1)]))
                         for r in range(n_rc))

        def least(carry):
            m = jnp.minimum(carry[0][1], carry[0][2])
            for r in range(1, n_rc):
                m = jnp.minimum(m, jnp.minimum(carry[r][1], carry[r][2]))
            return jnp.min(m)

        carry = first_two(qi > 0)

        def go_on(st):
            return jnp.logical_and(st[0] < qi - 1, st[1] < ATTN_DONE)

        def step(st):
            new = block(qi - 2 - st[0], st[2])
            return st[0] + 1, least(new), new

        walked, _, carry = lax.while_loop(go_on, step, (jnp.int32(0), least(carry), carry))
        for r in range(n_rc):
            o_ref[r * rc:(r + 1) * rc, :] = carry[r][0].astype(BF16)
            lsum_ref[r * rc:(r + 1) * rc, :] = jnp.where(lo, carry[r][1], carry[r][2])
        nblk_ref[hp, qi] = walked.astype(F32)

    blk = pl.BlockSpec((tq, LANES), lambda hp, qi: (qi, hp))
    o512 = jax.ShapeDtypeStruct((s, D_SB), F32)
    return _hosted_call(
        body, comm, name=name, grid=(4, nq),
        out_shape=(jax.ShapeDtypeStruct((s, D_SB), BF16), o512, jax.ShapeDtypeStruct((4, nq), F32)),
        in_specs=[blk, pl.BlockSpec((s, LANES), lambda hp, qi: (0, 4 + hp)),
                  pl.BlockSpec((s, LANES), lambda hp, qi: (0, 8 + hp))],
        out_specs=(blk, blk, pl.BlockSpec(memory_space=pltpu.SMEM)),
        args=(qkv, qkv, qkv), sem=("arbitrary", "arbitrary"))


HALO = 16


def _conv_taps(cc_ref, ch_ref, ccp_ref, chp_ref, halo_ref, first):
    u = cc_ref[...].astype(F32) * ch_ref[...].astype(F32)
    halo_ref[...] = ccp_ref[...].astype(F32) * chp_ref[...].astype(F32) * jnp.where(first, 0.0, 1.0)
    p6 = halo_ref[HALO - 2:HALO - 1, :]
    p7 = halo_ref[HALO - 1:HALO, :]
    rowi = lax.broadcasted_iota(jnp.int32, u.shape, 0)
    u1 = jnp.where(rowi == 0, p7, pltpu.roll(u, 1, 0))
    u2 = jnp.where(rowi == 0, p6, jnp.where(rowi == 1, p7, pltpu.roll(u, 2, 0)))
    return u, u1, u2


def _fwd_mid(x, pc, az, ya, p4, layer, cw, cb, bg, wout_full, pg, wpg_full, bpg, wpe_full, name, comm=None,
             head=None):
    s = x.shape[0]
    ts = min(FWD_ROW_TILE, s)
    blk_h = ts // HALO
    n_in = 18 + (2 if head else 0)

    def body(*refs):
        (x_ref, cb_ref_, cc_ref, ch_ref, cz_ref, ccp_ref, chp_ref, az_ref, ya_ref, p_ref,
         cw_ref, cbias_ref, bg_ref, wout_ref, pg_ref, wpg_ref, bpg_ref, wpe_ref) = refs[:18]
        x2_ref, x3_ref, gated_ref, h2_ref, gate_ref, e_ref = refs[n_in:n_in + 6]
        halo_ref = refs[-1]
        i = pl.program_id(0)
        lane = lax.broadcasted_iota(jnp.int32, (1, LANES), 1)
        lo = lane < HEAD_DIM
        u, u1, u2 = _conv_taps(cc_ref, ch_ref, ccp_ref, chp_ref, halo_ref, i == 0)
        conv = cbias_ref[...] + cw_ref[0:1, :] * u2 + cw_ref[1:2, :] * u1 + cw_ref[2:3, :] * u
        yc = cb_ref_[...].astype(F32) * conv
        for sl in range(8):
            cols = slice(LANES * (sl % 4), LANES * (sl % 4 + 1))
            y = yc[:, cols] if sl < 4 else ya_ref[:, cols].astype(F32)
            zc = (cz_ref[:, cols] if sl < 4 else az_ref[:, cols]).astype(F32)
            rg = lax.rsqrt(_group_bcast_sum(y * y, lo) * (1.0 / HEAD_DIM) + EPS)
            yn = y * rg * bg_ref[:, LANES * sl:LANES * (sl + 1)]
            gated_ref[:, LANES * sl:LANES * (sl + 1)] = (yn * (zc * _sigmoid(zc))).astype(BF16)
        x2 = x_ref[...] + jnp.dot(gated_ref[...], wout_ref[...], preferred_element_type=F32)
        x2_ref[...] = x2
        r2 = lax.rsqrt(jnp.mean(x2 * x2, axis=-1, keepdims=True) + EPS)
        h2 = (x2 * r2 * pg_ref[...]).astype(BF16)
        h2_ref[...] = h2
        gate = _sigmoid(jnp.dot(h2, wpg_ref[...], preferred_element_type=F32) + bpg_ref[...])
        gate_ref[...] = gate.astype(BF16)
        e = jnp.dot(p_ref[...].astype(BF16), wpe_ref[...], preferred_element_type=F32)
        e_ref[...] = e.astype(BF16)
        x3 = x2 + gate * e
        if not head:
            x3_ref[...] = x3
            return
        t_ref, fg_ref = refs[18:20]
        loss_ref, dfg_ref = refs[n_in + 6:n_in + 8]
        dx, loss, dfg = _loss_math(x3, t_ref[...], fg_ref[...])

        @pl.when(i == 0)
        def _():
            loss_ref[...] = jnp.zeros_like(loss_ref)
            dfg_ref[...] = jnp.zeros_like(dfg_ref)

        x3_ref[...] = dx
        loss_ref[...] += loss
        dfg_ref[...] += dfg

    row = lambda width, cb_=0: pl.BlockSpec((ts, width), lambda i: (i, cb_))
    prev = lambda cb_: pl.BlockSpec((HALO, 512), lambda i: (jnp.maximum(i * blk_h - 1, 0), cb_))
    vec = lambda width: pl.BlockSpec((1, width), lambda i: (0, 0))
    lvec = lambda width: _layer_rows(layer, 1, width)
    wspec = lambda r_, c_: pl.BlockSpec((r_, c_), lambda i: (0, 0))
    f32o = jax.ShapeDtypeStruct((s, D_MODEL), F32)
    bfo = jax.ShapeDtypeStruct((s, D_MODEL), BF16)
    head_in = [row(D_MODEL), vec(D_MODEL)] if head else []
    head_out = [jax.ShapeDtypeStruct((1, LANES), F32), jax.ShapeDtypeStruct((1, D_MODEL), F32)] if head else []
    return _hosted_call(
        body, comm, name=name, grid=(s // ts,),
        out_shape=(f32o, f32o, bfo, bfo, bfo, bfo, *head_out),
        scratch_shapes=[pltpu.VMEM((HALO, 512), F32)],
        in_specs=[row(D_MODEL), row(512, 0), row(512, 1), row(512, 2), row(512, 3), prev(1), prev(2),
                  row(512), row(512),
                  pl.BlockSpec((None, None, ts, PLE_DIM), lambda i: (layer, 0, i, 0)),
                  _layer_rows(layer, 3, 512), lvec(512), lvec(D_MODEL),
                  wspec(D_MODEL, D_MODEL), lvec(D_MODEL), wspec(D_MODEL, D_MODEL), lvec(D_MODEL),
                  wspec(PLE_DIM, D_MODEL), *head_in],
        out_specs=(*[row(D_MODEL)] * 6, *([vec(LANES), vec(D_MODEL)] if head else [])),
        args=(x, pc, pc, pc, pc, pc, pc, az, ya, p4, cw, cb, bg, wout_full, pg, wpg_full, bpg, wpe_full,
              *(head or ())),
        sem=("arbitrary",) if head else ("parallel",))


def _loss_math(x, target, g):
    r = lax.rsqrt(jnp.mean(x * x, axis=-1, keepdims=True) + EPS)
    xn = x * r
    err = xn * g - target
    per_row = jnp.sum(err * err, axis=-1, keepdims=True)
    loss = jnp.sum(per_row, axis=0, keepdims=True) * (0.5 / D_MODEL)
    dy = err * (1.0 / D_MODEL)
    dg = jnp.sum(dy * xn, axis=0, keepdims=True)
    dxn = dy * g
    return r * (dxn - xn * jnp.mean(dxn * xn, axis=-1, keepdims=True)), loss, dg


def _bwd_mid(dx3, x2, gate, e, pc, az, ya, gated, h2, p4, layer, cw, cb, bg, pg, wpg_full, wout_full, name,
             comm=None):
    s = x2.shape[0]
    ts = min(ROW_TILE, s)
    blk_h = ts // HALO

    def body(dx3_ref, x2_ref, gate_ref, e_ref, cb_ref_, cc_ref, ch_ref, cz_ref, ccp_ref, chp_ref, az_ref, ya_ref,
             gated_ref, h2_ref, p_ref, cw_ref, cbias_ref, bg_ref, pg_ref, wpg_ref, wout_ref,
             dx2_ref, dya_ref, dmisc_ref, dconv_ref, dwout_ref, dwpg_ref, dwpe_ref,
             dbpg_ref, dpg_ref, dbg_ref, dcbias_ref, dcw_ref,
             dgated_ref, halo_ref, acc_out, acc_pg, acc_pe):
        i = pl.program_id(0)

        @pl.when(i == 0)
        def _():
            for ref in (dbpg_ref, dpg_ref, dbg_ref, dcbias_ref, dcw_ref, acc_out, acc_pg, acc_pe):
                ref[...] = jnp.zeros_like(ref)

        lane = lax.broadcasted_iota(jnp.int32, (1, LANES), 1)
        lo = lane < HEAD_DIM
        dx3 = dx3_ref[...]
        gate = gate_ref[...].astype(F32)
        de_b = (dx3 * gate).astype(BF16)
        dgpre = dx3 * e_ref[...].astype(F32) * gate * (1.0 - gate)
        dbpg_ref[...] += jnp.sum(dgpre, axis=0, keepdims=True)
        dgpre_b = dgpre.astype(BF16)
        dh2 = lax.dot_general(dgpre_b, wpg_ref[...], NT, preferred_element_type=F32)
        acc_pe[...] += lax.dot_general(p_ref[...].astype(BF16), de_b, TN, preferred_element_type=F32)
        acc_pg[...] += lax.dot_general(h2_ref[...], dgpre_b, TN, preferred_element_type=F32)

        u, u1, u2 = _conv_taps(cc_ref, ch_ref, ccp_ref, chp_ref, halo_ref, i == 0)
        conv = cbias_ref[...] + cw_ref[0:1, :] * u2 + cw_ref[1:2, :] * u1 + cw_ref[2:3, :] * u
        c_b = cb_ref_[...].astype(F32)
        yc = c_b * conv
        fwd = []
        for sl in range(8):
            cols = slice(LANES * (sl % 4), LANES * (sl % 4 + 1))
            y = yc[:, cols] if sl < 4 else ya_ref[:, cols].astype(F32)
            zc = (cz_ref[:, cols] if sl < 4 else az_ref[:, cols]).astype(F32)
            rg = lax.rsqrt(_group_bcast_sum(y * y, lo) * (1.0 / HEAD_DIM) + EPS)
            sig = _sigmoid(zc)
            fwd.append((rg, y * rg, zc * sig, sig * (1.0 + zc * (1.0 - sig))))

        x2 = x2_ref[...]
        r2 = lax.rsqrt(jnp.mean(x2 * x2, axis=-1, keepdims=True) + EPS)
        xn2 = x2 * r2
        dpg_ref[...] += jnp.sum(dh2 * xn2, axis=0, keepdims=True)
        dxn = dh2 * pg_ref[...]
        dx2 = dx3 + r2 * (dxn - xn2 * jnp.mean(dxn * xn2, axis=-1, keepdims=True))
        dx2_ref[...] = dx2
        dx2_b = dx2.astype(BF16)
        dgated_ref[...] = lax.dot_general(dx2_b, wout_ref[...], NT, preferred_element_type=F32)
        acc_out[...] += lax.dot_general(gated_ref[...], dx2_b, TN, preferred_element_type=F32)

        for sl in range(8):
            cols = slice(LANES * (sl % 4), LANES * (sl % 4 + 1))
            wide = slice(LANES * sl, LANES * (sl + 1))
            rg, yhat, silu, dsilu = fwd[sl]
            bgs = bg_ref[:, wide]
            dgt = dgated_ref[:, wide]
            dyn = dgt * silu
            dzc = dgt * (yhat * bgs) * dsilu
            dbg_ref[:, wide] += jnp.sum(dyn * yhat, axis=0, keepdims=True)
            dyh = dyn * bgs
            dy = rg * (dyh - yhat * (_group_bcast_sum(dyh * yhat, lo) * (1.0 / HEAD_DIM)))
            if sl < 4:
                dconv = dy * c_b[:, cols]
                dmisc_ref[:, cols] = (dy * conv[:, cols]).astype(BF16)
                dmisc_ref[:, 512 + LANES * sl:512 + LANES * (sl + 1)] = dzc.astype(BF16)
                dconv_ref[:, cols] = dconv
                dcbias_ref[:, cols] += jnp.sum(dconv, axis=0, keepdims=True)
                dcw_ref[0:1, cols] += jnp.sum(dconv * u2[:, cols], axis=0, keepdims=True)
                dcw_ref[1:2, cols] += jnp.sum(dconv * u1[:, cols], axis=0, keepdims=True)
                dcw_ref[2:3, cols] += jnp.sum(dconv * u[:, cols], axis=0, keepdims=True)
            else:
                dya_ref[:, cols] = dy.astype(BF16)
                dmisc_ref[:, 1024 + LANES * (sl - 4):1024 + LANES * (sl - 3)] = dzc.astype(BF16)

        @pl.when(i == pl.num_programs(0) - 1)
        def _():
            dwout_ref[...] = acc_out[...].astype(BF16)
            dwpg_ref[...] = acc_pg[...].astype(BF16)
            dwpe_ref[...] = acc_pe[...].astype(BF16)

    row = lambda width, cb_=0: pl.BlockSpec((ts, width), lambda i: (i, cb_))
    prev = lambda cb_: pl.BlockSpec((HALO, 512), lambda i: (jnp.maximum(i * blk_h - 1, 0), cb_))
    vec = lambda width: pl.BlockSpec((1, width), lambda i: (0, 0))
    lvec = lambda width: _layer_rows(layer, 1, width)
    wspec = lambda r_, c_: pl.BlockSpec((r_, c_), lambda i: (0, 0))
    vo = lambda width: jax.ShapeDtypeStruct((1, width), F32)
    sq = jax.ShapeDtypeStruct((D_MODEL, D_MODEL), BF16)
    return _hosted_call(
        body, comm, name=name, grid=(s // ts,), sem=("arbitrary",),
        args=(dx3, x2, gate, e, pc, pc, pc, pc, pc, pc, az, ya, gated, h2, p4, cw, cb, bg, pg, wpg_full, wout_full),
        out_shape=(jax.ShapeDtypeStruct((s, D_MODEL), F32), jax.ShapeDtypeStruct((s, 512), BF16),
                   jax.ShapeDtypeStruct((s, 1536), BF16), jax.ShapeDtypeStruct((s, 512), F32),
                   sq, sq, jax.ShapeDtypeStruct((PLE_DIM, D_MODEL), BF16),
                   vo(D_MODEL), vo(D_MODEL), vo(D_MODEL), vo(512), jax.ShapeDtypeStruct((SUBLANES, 512), F32)),
        in_specs=[row(D_MODEL), row(D_MODEL), row(D_MODEL), row(D_MODEL),
                  row(512, 0), row(512, 1), row(512, 2), row(512, 3), prev(1), prev(2), row(512), row(512),
                  row(D_MODEL), row(D_MODEL),
                  pl.BlockSpec((None, None, ts, PLE_DIM), lambda i: (layer, 0, i, 0)),
                  _layer_rows(layer, 3, 512), lvec(512), lvec(D_MODEL), lvec(D_MODEL),
                  wspec(D_MODEL, D_MODEL), wspec(D_MODEL, D_MODEL)],
        out_specs=(row(D_MODEL), row(512), row(1536), row(512),
                   wspec(D_MODEL, D_MODEL), wspec(D_MODEL, D_MODEL), wspec(PLE_DIM, D_MODEL),
                   vec(D_MODEL), vec(D_MODEL), vec(D_MODEL), vec(512),
                   pl.BlockSpec((SUBLANES, 512), lambda i: (0, 0))),
        scratch_shapes=[pltpu.VMEM((ts, D_MODEL), F32), pltpu.VMEM((HALO, 512), F32),
                        pltpu.VMEM((D_MODEL, D_MODEL), F32), pltpu.VMEM((D_MODEL, D_MODEL), F32),
                        pltpu.VMEM((PLE_DIM, D_MODEL), F32)])


def _attn_bwd(qkv, lsum, nblk, dya, name, comm=None):
    s = qkv.shape[0]
    tq = min(ATTN_TILE, s)
    nq = s // tq
    rc = min(ATTN_ROWS, tq)
    n_rc = tq // rc
    chains = [(r, hh) for r in range(n_rc) for hh in range(2)]

    def body(nblk_ref, q_ref, k_ref, v_ref, lsum_ref, do_ref, dq_ref, dk_ref, dv_ref, dk_acc, dv_acc):
        hp, qi = pl.program_id(0), pl.program_id(1)

        @pl.when(qi == 0)
        def _():
            dk_acc[...] = jnp.zeros_like(dk_acc)
            dv_acc[...] = jnp.zeros_like(dv_acc)

        lo, causal, tri_gt, tri_le = _attn_pieces(tq, rc)
        lane = lax.broadcasted_iota(jnp.int32, (1, LANES), 1)
        qh = _split_heads(q_ref[...], lo)
        doh = _split_heads(do_ref[...].astype(BF16), lo)
        lt = lsum_ref[...]
        ltot_h = (jnp.sum(jnp.where(lane == 0, lt, 0.0), axis=-1, keepdims=True),
                  jnp.sum(jnp.where(lane == HEAD_DIM, lt, 0.0), axis=-1, keepdims=True))
        rows = lambda a_, r: a_[r * rc:(r + 1) * rc]
        qc = {(r, hh): rows(qh[hh], r) for r, hh in chains}
        doc = {(r, hh): rows(doh[hh], r) for r, hh in chains}
        ltot = {(r, hh): rows(ltot_h[hh], r) for r, hh in chains}

        mm = lambda a_, b_: jnp.dot(a_.astype(BF16), b_, preferred_element_type=F32)
        mm_nt = lambda a_, b_: lax.dot_general(a_, b_, NT, preferred_element_type=F32)
        mm_tn = lambda a_, b_: lax.dot_general(a_.astype(BF16), b_, TN, preferred_element_type=F32)
        rowsum = lambda a_: jnp.sum(a_, axis=-1, keepdims=True)

        def block(kb, carry, diag=False):
            start = pl.multiple_of(kb * tq, tq)
            k = k_ref[pl.ds(start, tq), :]
            v = v_ref[pl.ds(start, tq), :]
            kh = _split_heads(k, lo)
            keep = (lambda ch, a_: jnp.where(causal[ch[0]], a_, 0.0)) if diag else (lambda ch, a_: a_)
            z = {ch: mm_nt(qc[ch], k) for ch in chains}
            da = {ch: mm_nt(doc[ch], v) for ch in chains}
            sp = {ch: _softplus(z[ch], causal[ch[0]], diag) for ch in chains}
            later = {ch: mm(sp[ch], tri_gt) for ch in chains}
            walked = {ch: carry[ch[0]][1 + ch[1]] + rowsum(sp[ch]) for ch in chains}
            a = {ch: keep(ch, jnp.exp((z[ch] - sp[ch]) - ((ltot[ch] - walked[ch]) + later[ch]))) for ch in chains}
            g = {ch: a[ch] * da[ch] for ch in chains}
            upto = {ch: mm(g[ch], tri_le) for ch in chains}
            dz = {ch: keep(ch, g[ch] - jnp.exp(z[ch] - sp[ch]) * (carry[ch[0]][3 + ch[1]] + upto[ch])).astype(BF16)
                  for ch in chains}
            dqc = {ch: mm(dz[ch], kh[ch[1]]) for ch in chains}
            dkc = [mm_tn(dz[ch], qc[ch]) for ch in chains]
            dvc = [mm_tn(a[ch], doc[ch]) for ch in chains]
            dk_acc[pl.ds(start, tq), :] += sum(dkc[1:], dkc[0])
            dv_acc[pl.ds(start, tq), :] += sum(dvc[1:], dvc[0])
            return tuple((carry[r][0] + dqc[(r, 0)] + dqc[(r, 1)], walked[(r, 0)], walked[(r, 1)],
                          carry[r][3] + rowsum(g[(r, 0)]), carry[r][4] + rowsum(g[(r, 1)])) for r in range(n_rc))

        zc = jnp.zeros((rc, 1), F32)
        carry = tuple((jnp.zeros((rc, LANES), F32), zc, zc, zc, zc) for _ in range(n_rc))
        near = jnp.maximum(qi - 1, 0)
        first = near - jnp.clip(nblk_ref[hp, qi].astype(jnp.int32), 0, near)
        carry = lax.fori_loop(first, qi, block, carry)
        carry = block(qi, carry, True)
        for r in range(n_rc):
            dq_ref[r * rc:(r + 1) * rc, :] = (carry[r][0] * 0.125).astype(BF16)

        @pl.when(qi == pl.num_programs(1) - 1)
        def _():
            dk_ref[...] = dk_acc[...].astype(BF16)
            dv_ref[...] = dv_acc[...].astype(BF16)

    blk = pl.BlockSpec((tq, LANES), lambda hp, qi: (qi, hp))
    col = pl.BlockSpec((s, LANES), lambda hp, qi: (0, hp))
    o512 = jax.ShapeDtypeStruct((s, D_SB), BF16)
    return _hosted_call(
        body, comm, name=name, grid=(4, nq),
        out_shape=(o512, o512, o512),
        in_specs=[pl.BlockSpec(memory_space=pltpu.SMEM), blk,
                  pl.BlockSpec((s, LANES), lambda hp, qi: (0, 4 + hp)),
                  pl.BlockSpec((s, LANES), lambda hp, qi: (0, 8 + hp)), blk, blk],
        out_specs=(blk, col, col),
        scratch_shapes=[pltpu.VMEM((s, LANES), F32), pltpu.VMEM((s, LANES), F32)],
        args=(nblk, qkv, qkv, qkv, lsum, dya), sem=("parallel", "arbitrary"))


def _bwd_dproj(dmisc, dconv, pc, dq, dk, dv, x, dx2, g, cw, layer, win_full, name, comm=None, h=None,
               h_rows=None):
    s = x.shape[0]
    ts = min(ROW_TILE, s)
    blk8 = ts // SUBLANES
    last8 = s // SUBLANES - 1
    fused = h is not None
    emit_dproj = not fused or h_rows is not None
    dw_rows, h_blk = (D_MODEL, 0) if h_rows is None else h_rows

    def body(*refs):
        (dcb_ref, dcz_ref, daz_ref, dconv_ref, nxt_ref, cc_ref, ch_ref, dq_ref, dk_ref, dv_ref,
         x_ref, dx2_ref, g_ref, cw_ref, w_ref) = refs[:15]
        rest = list(refs[15:])
        h_ref = rest.pop(0) if fused else None
        dproj_ref = rest.pop(0) if emit_dproj else None
        dx_ref, dg_ref = rest.pop(0), rest.pop(0)
        dw_ref = rest.pop(0) if fused else None
        dproj_ref = dproj_ref if emit_dproj else rest.pop(0)
        acc_ref = rest.pop(0) if fused else None
        i = pl.program_id(0)

        @pl.when(i == 0)
        def _():
            dg_ref[...] = jnp.zeros_like(dg_ref)
            if fused:
                acc_ref[...] = jnp.zeros_like(acc_ref)

        keep = jnp.where(i == pl.num_programs(0) - 1, 0.0, 1.0)
        dc = dconv_ref[...]
        n0 = nxt_ref[0:1, :] * keep
        n1 = nxt_ref[1:2, :] * keep
        rowi = lax.broadcasted_iota(jnp.int32, dc.shape, 0)
        dc1 = jnp.where(rowi == ts - 1, n0, pltpu.roll(dc, ts - 1, 0))
        dc2 = jnp.where(rowi == ts - 2, n0, jnp.where(rowi == ts - 1, n1, pltpu.roll(dc, ts - 2, 0)))
        du = cw_ref[2:3, :] * dc + cw_ref[1:2, :] * dc1 + cw_ref[0:1, :] * dc2
        dproj_ref[:, 0:512] = dcb_ref[...]
        dproj_ref[:, 512:1024] = (du * ch_ref[...].astype(F32)).astype(BF16)
        dproj_ref[:, 1024:1536] = (du * cc_ref[...].astype(F32)).astype(BF16)
        dproj_ref[:, 1536:2048] = dcz_ref[...]
        dproj_ref[:, 2048:2560] = dq_ref[...]
        dproj_ref[:, 2560:3072] = dk_ref[...]
        dproj_ref[:, 3072:3584] = dv_ref[...]
        dproj_ref[:, 3584:4096] = daz_ref[...]
        dh = lax.dot_general(dproj_ref[...], w_ref[...], NT, preferred_element_type=F32)
        if fused:
            acc_ref[...] += lax.dot_general(h_ref[...], dproj_ref[...], TN, preferred_element_type=F32)
        x = x_ref[...]
        r = lax.rsqrt(jnp.mean(x * x, axis=-1, keepdims=True) + EPS)
        xn = x * r
        dg_ref[...] += jnp.sum(dh * xn, axis=0, keepdims=True)
        dxn = dh * g_ref[...]
        dx_ref[...] = dx2_ref[...] + r * (dxn - xn * jnp.mean(dxn * xn, axis=-1, keepdims=True))
        if fused:
            @pl.when(i == pl.num_programs(0) - 1)
            def _():
                dw_ref[...] = acc_ref[...].astype(BF16)

    row = lambda width, cb_=0: pl.BlockSpec((ts, width), lambda i: (i, cb_))
    nxt = pl.BlockSpec((SUBLANES, 512), lambda i: (jnp.minimum((i + 1) * blk8, last8), 0))
    vec = lambda width: pl.BlockSpec((1, width), lambda i: (0, 0))
    lvec = lambda width: _layer_rows(layer, 1, width)
    once = dict(pipeline_mode=pl.Buffered(1)) if fused else {}
    whole = lambda rows_: pl.BlockSpec((rows_, N_IN), lambda i: (0, 0), **once)
    in_specs = [row(512, 0), row(512, 1), row(512, 2), row(512), nxt, row(512, 1), row(512, 2),
                row(512), row(512), row(512), row(D_MODEL), row(D_MODEL), lvec(D_MODEL),
                _layer_rows(layer, 3, 512), whole(D_MODEL)]
    args = [dmisc, dmisc, dmisc, dconv, dconv, pc, pc, dq, dk, dv, x, dx2, g, cw, win_full]
    out_shape = [jax.ShapeDtypeStruct((s, D_MODEL), F32), jax.ShapeDtypeStruct((1, D_MODEL), F32)]
    out_specs = [row(D_MODEL), vec(D_MODEL)]
    scratch = []
    if emit_dproj:
        out_shape.insert(0, jax.ShapeDtypeStruct((s, N_IN), BF16))
        out_specs.insert(0, row(N_IN))
    else:
        scratch.append(pltpu.VMEM((ts, N_IN), BF16))
    if fused:
        in_specs.append(row(dw_rows, h_blk))
        args.append(h)
        out_shape.append(jax.ShapeDtypeStruct((dw_rows, N_IN), BF16))
        out_specs.append(whole(dw_rows))
        scratch.append(pltpu.VMEM((dw_rows, N_IN), F32))
    return _hosted_call(
        body, comm, name=name, grid=(s // ts,), out_shape=tuple(out_shape), in_specs=in_specs,
        out_specs=tuple(out_specs), scratch_shapes=scratch, args=tuple(args), sem=("arbitrary",))


def _atb(a, b, name, a_cols=None, comm=None):
    s, n = b.shape
    m, a_blk = (a.shape[-1], 0) if a_cols is None else a_cols
    ts = min(512, s)
    tn = min(2048, n)
    a_spec = pl.BlockSpec((ts, m), lambda j, i: (i, a_blk))

    def body(a_ref, b_ref, o_ref, acc_ref):
        i = pl.program_id(1)

        @pl.when(i == 0)
        def _():
            acc_ref[...] = jnp.zeros_like(acc_ref)

        acc_ref[...] += lax.dot_general(a_ref[...].astype(BF16), b_ref[...], TN, preferred_element_type=F32)

        @pl.when(i == pl.num_programs(1) - 1)
        def _():
            o_ref[...] = acc_ref[...].astype(BF16)

    (out,), got = _hosted_call(
        body, comm, name=name, grid=(n // tn, s // ts),
        out_shape=(jax.ShapeDtypeStruct((m, n), BF16),),
        in_specs=[a_spec, pl.BlockSpec((ts, tn), lambda j, i: (i, j))],
        out_specs=(pl.BlockSpec((m, tn), lambda j, i: (0, j)),),
        scratch_shapes=[pltpu.VMEM((m, tn), F32)],
        args=(a, b), sem=("parallel", "arbitrary"))
    return out, got


def _adamw_math(w, g, m, v):
    m2 = ADAM_B1 * m + (1.0 - ADAM_B1) * g
    v2 = ADAM_B2 * v + (1.0 - ADAM_B2) * (g * g)
    m_hat = m2 / (1.0 - ADAM_B1 ** ADAM_STEP)
    v_hat = v2 / (1.0 - ADAM_B2 ** ADAM_STEP)
    delta = -ADAM_LR * (m_hat / (jnp.sqrt(v_hat) + ADAM_EPS) + ADAM_WD * w)
    return delta, m2, v2


def _adamw_sum8(pieces, w, m, v, name):
    _, rows, cols = w.shape
    tr = min([rows, 256] + [pc_[0].shape[1] for pc_ in pieces])
    n_tiles = rows // tr
    n_p = len(pieces)
    spans = [(layer, row0 // tr, arr.shape[1] // tr) for arr, layer, row0 in pieces]

    def body(*refs):
        p_refs = refs[:n_p]
        w_ref, m_ref, v_ref, g_ref, d_ref, m2_ref, v2_ref = refs[n_p:]
        l, i = pl.program_id(0), pl.program_id(1)

        def run(p_ref):
            g = p_ref[0].astype(F32)
            for d in range(1, N_DEV):
                g = g + p_ref[d].astype(F32)
            g_ref[...] = g
            d_ref[...], m2_ref[...], v2_ref[...] = _adamw_math(w_ref[...], g, m_ref[...], v_ref[...])

        for p_ref, (layer, t0, nt) in zip(p_refs, spans):
            mine = jnp.logical_and(l == layer, jnp.logical_and(i >= t0, i < t0 + nt))
            pl.when(mine)(lambda p_ref=p_ref: run(p_ref))

    def piece_spec(layer, t0, nt):
        return pl.BlockSpec((N_DEV, tr, cols),
                            lambda l, i: (0, jnp.clip(jnp.where(l == layer, i - t0, jnp.where(l < layer, 0, nt - 1)),
                                                      0, nt - 1), 0))

    tile = pl.BlockSpec((None, tr, cols), lambda l, i: (l, i, 0))
    o = jax.ShapeDtypeStruct((DEPTH, rows, cols), F32)
    return _call(
        body, name=name, grid=(DEPTH, n_tiles),
        out_shape=(o, o, o, o),
        in_specs=[*[piece_spec(*sp) for sp in spans], tile, tile, tile],
        out_specs=(tile, tile, tile, tile),
        compiler_params=_params(("arbitrary", "arbitrary"), VMEM_LIMIT),
    )(*[pc_[0] for pc_ in pieces], w, m, v)


def _small_update(blk, layered, final, conv, loss_parts):
    n_l = len(layered)
    ins = [a for item in layered for a in item] + list(final) + list(conv) + [loss_parts]
    shapes = [item[2].shape for item in layered] + [final[1].shape, conv[2].shape]
    out_shape = [jax.ShapeDtypeStruct(sh, F32) for sh in shapes for _ in range(4)]
    out_shape.append(jax.ShapeDtypeStruct((1, LANES), F32))

    def body(*refs):
        blk_ref, refs = refs[0], refs[1:]
        in_refs, out_refs, pick_ref = refs[:len(ins)], refs[len(ins):-1], refs[-1]

        def total(ref):
            g = ref[0]
            for d in range(1, N_DEV):
                g = g + ref[d]
            return g

        def update(k, at, g, w_ref, m_ref, v_ref):
            g_ref, d_ref, m2_ref, v2_ref = out_refs[4 * k:4 * k + 4]
            g_ref[at] = g
            d_ref[at], m2_ref[at], v2_ref[at] = _adamw_math(w_ref[at], g, m_ref[at], v_ref[at])

        for k in range(n_l):
            p0, p1, w_ref, m_ref, v_ref = in_refs[5 * k:5 * k + 5]
            for layer, parts in enumerate((p0, p1)):
                update(k, pl.ds(layer, 1), total(parts), w_ref, m_ref, v_ref)
        pf, w_ref, m_ref, v_ref = in_refs[5 * n_l:5 * n_l + 4]
        update(n_l, pl.ds(0, 1), total(pf), w_ref, m_ref, v_ref)
        c0, c1, w_ref, m_ref, v_ref = in_refs[5 * n_l + 4:5 * n_l + 9]
        for layer, parts in enumerate((c0, c1)):
            g8 = total(parts)
            mine = jnp.zeros((SUBLANES, HEAD_DIM), F32)
            for j in range(N_DEV):
                mine = mine + jnp.where(blk_ref[0] == j, g8[:, HEAD_DIM * j:HEAD_DIM * (j + 1)], 0.0)
            pick_ref[...] = mine
            update(n_l + 1, layer, pick_ref[0:3, :], w_ref, m_ref, v_ref)
        out_refs[-1][...] = total(in_refs[-1])

    whole = lambda shape: pl.BlockSpec(shape, lambda: (0,) * len(shape))
    outs = _call(
        body, name="adamw_small",
        out_shape=tuple(out_shape),
        in_specs=[pl.BlockSpec(memory_space=pltpu.SMEM)] + [whole(a.shape) for a in ins],
        out_specs=tuple(whole(o.shape) for o in out_shape),
        scratch_shapes=[pltpu.VMEM((SUBLANES, HEAD_DIM), F32)],
    )(blk, *ins)
    return [outs[4 * k:4 * k + 4] for k in range(n_l + 2)], outs[-1]


def kernel(x, p, norm_g, w_in, conv_w, conv_b, branch_g, w_out, ple_norm_g, w_pg, b_pg, w_pe, final_g, loss_target, m_norm_g, m_w_in, m_conv_w, m_conv_b, m_branch_g, m_w_out, m_ple_norm_g, m_w_pg, m_b_pg, m_w_pe, m_final_g, v_norm_g, v_w_in, v_conv_w, v_conv_b, v_branch_g, v_w_out, v_ple_norm_g, v_w_pg, v_b_pg, v_w_pe, v_final_g):
    s = x.shape[1]
    x0 = x.reshape(s, D_MODEL)
    target = loss_target.reshape(s, D_MODEL)
    me_blk = _my_block()

    win_s, wout_s, wpg_s, wpe_s = _cast_bf16(
        [w_in.reshape(DEPTH * D_MODEL, 512), w_out.reshape(DEPTH * 128, D_MODEL),
         w_pg.reshape(DEPTH * 128, D_MODEL), w_pe.reshape(DEPTH * PLE_DIM, 128)], "cast_weights")
    win_s, wout_s = win_s.reshape(DEPTH, D_MODEL, 512), wout_s.reshape(DEPTH, 128, D_MODEL)
    wpg_s, wpe_s = wpg_s.reshape(DEPTH, 128, D_MODEL), wpe_s.reshape(DEPTH, PLE_DIM, 128)
    cw_s = jnp.zeros((SUBLANES, LANES), F32).at[:DEPTH * 3, :HEAD_DIM].set(conv_w.reshape(DEPTH * 3, HEAD_DIM))
    bf = lambda r_, c_: jax.ShapeDtypeStruct((r_, c_), BF16)
    w_items = lambda l: [(wout_s[l], bf(D_MODEL, D_MODEL), "rows128"), (wpg_s[l], bf(D_MODEL, D_MODEL), "rows128"),
                         (wpe_s[l], bf(PLE_DIM, D_MODEL), "cols128")]
    win_f = [None] * DEPTH
    win_f[0], cw_all = _comm_call(_gather_comm([
        (win_s[0], bf(D_MODEL, N_IN), "cols512"),
        (cw_s, jax.ShapeDtypeStruct((N_DEV, SUBLANES, LANES), F32), "slot")]), "gather_w_in_0")
    cw_full = jnp.transpose(cw_all[:, :DEPTH * 3, :HEAD_DIM].reshape(N_DEV, DEPTH, 3, HEAD_DIM), (1, 2, 0, 3))
    cw_full = cw_full.reshape(DEPTH, 3, D_CONV)
    gather_rest_0 = _gather_comm(w_items(0))
    gather_win_1 = _gather_comm([(win_s[1], bf(D_MODEL, N_IN), "cols512")])
    gather_rest_1 = _gather_comm(w_items(1))

    norm3, convb3, branch3, ple3, bpg3 = [a.reshape(DEPTH, 1, -1) for a in (norm_g, conv_b, branch_g, ple_norm_g, b_pg)]

    saved = []
    xl = x0
    wout_f, wpg_f, wpe_f = [None] * DEPTH, [None] * DEPTH, [None] * DEPTH
    for l in range(DEPTH):
        (h, pc, qkv, az), got = _fwd_in(xl, norm3, l, win_f[l], f"fwd_in_{l}",
                                        comm=gather_rest_0 if l == 0 else None)
        if l == 0:
            wout_f[0], wpg_f[0], wpe_f[0] = got
        (ya, lsum, nblk), got = _attn_fwd(qkv, f"attn_fwd_{l}", comm=gather_win_1 if l == 0 else None)
        if l == 0:
            (win_f[1],) = got
        last = l == DEPTH - 1
        outs, got = _fwd_mid(
            xl, pc, az, ya, p, l, cw_full, convb3, branch3, wout_f[l],
            ple3, wpg_f[l], bpg3, wpe_f[l], f"fwd_mid_{l}",
            comm=gather_rest_1 if l == 0 else None, head=(target, final_g[None, :]) if last else None)
        x2, x3, gated, h2, gate, e = outs[:6]
        if l == 0:
            wout_f[1], wpg_f[1], wpe_f[1] = got
        saved.append(dict(x=xl, h=h, pc=pc, qkv=qkv, az=az, ya=ya, lsum=lsum, nblk=nblk, x2=x2, gated=gated, h2=h2,
                          gate=gate, e=e))
        xl = x3

    dx, (loss_acc, d_final_g) = xl, outs[6:]

    dwin, dwout, dwpg, dwpe = [None] * DEPTH, [None] * DEPTH, [None] * DEPTH, [None] * DEPTH
    small = dict(norm_g=[None] * DEPTH, conv_b=[None] * DEPTH, branch_g=[None] * DEPTH,
                 ple_norm_g=[None] * DEPTH, b_pg=[None] * DEPTH, conv_w=[None] * DEPTH)
    slot = lambda r_, c_: jax.ShapeDtypeStruct((r_, c_), BF16)
    half = D_MODEL // 2
    r_in1, r_out, r_pg, r_pe = None, [None] * DEPTH, [None] * DEPTH, [None] * DEPTH

    def rest_items(l):
        return [(dwout[l], slot(128, D_MODEL), "rows128"), (dwpg[l], slot(128, D_MODEL), "rows128"),
                (dwpe[l], slot(PLE_DIM, 128), "cols128")]

    for l in reversed(range(DEPTH)):
        sv = saved[l]
        ride = _exchange_comm(rest_items(1)) if l == 0 else None
        (dx2, dya, dmisc, dconv, dwout[l], dwpg[l], dwpe[l], d_bpg, d_pg, d_bg, d_cbias, d_cw), got = _bwd_mid(
            dx, sv["x2"], sv["gate"], sv["e"], sv["pc"], sv["az"], sv["ya"], sv["gated"], sv["h2"], p, l,
            cw_full, convb3, branch3, ple3, wpg_f[l], wout_f[l], f"bwd_mid_{l}",
            comm=ride)
        if l == 0:
            r_out[1], r_pg[1], r_pe[1] = got
        ride = _exchange_comm([(dwin[1], slot(D_MODEL, 512), "cols512")] + rest_items(0)) if l == 0 else None
        (dq, dk, dv), got = _attn_bwd(sv["qkv"], sv["lsum"], sv["nblk"], dya, f"attn_bwd_{l}", comm=ride)
        if l == 0:
            r_in1, r_out[0], r_pg[0], r_pe[0] = got
        dproj_args = (dmisc, dconv, sv["pc"], dq, dk, dv, sv["x"], dx2, norm3, cw_full, l, win_f[l])
        if l == 1:
            (dx, d_ng, dwin[1]), _ = _bwd_dproj(*dproj_args, "bwd_dproj_dw_1", h=sv["h"])
        else:
            (dproj, dx, d_ng, dwin_top), _ = _bwd_dproj(*dproj_args, "bwd_dproj_dw_0", h=sv["h"], h_rows=(half, 0))
            dwin_bot, (r_in0_top,) = _atb(sv["h"], dproj, "dw_in_0_bottom", a_cols=(half, 1),
                                          comm=_exchange_comm([(dwin_top, slot(half, 512), "cols512")]))
        small["norm_g"][l], small["conv_b"][l], small["branch_g"][l] = d_ng, d_cbias, d_bg
        small["ple_norm_g"][l], small["b_pg"][l], small["conv_w"][l] = d_pg, d_bpg, d_cw
    grad_x = dx.reshape(1, s, D_MODEL)

    names = ["norm_g", "conv_b", "branch_g", "ple_norm_g", "b_pg", "conv_w"]
    small_list = [small[n][l] for n in names for l in range(DEPTH)] + [d_final_g, loss_acc]
    got = _comm_call(_exchange_comm(
        [(dwin_bot, slot(half, 512), "cols512")]
        + [(a, jax.ShapeDtypeStruct(a.shape, F32), "slot") for a in small_list]), "exchange_last")
    r_in0_bot, r_small = got[0], got[1:]

    per_layer = lambda r: [(r[0], 0, 0), (r[1], 1, 0)]
    g_win, d_win, m_win, v_win = _adamw_sum8([(r_in0_top, 0, 0), (r_in0_bot, 0, half), (r_in1, 1, 0)],
                                             w_in, m_w_in, v_w_in, "adamw_w_in")
    g_wout, d_wout, m_wout, v_wout = _adamw_sum8(per_layer(r_out), w_out, m_w_out, v_w_out, "adamw_w_out")
    g_wpg, d_wpg, m_wpg, v_wpg = _adamw_sum8(per_layer(r_pg), w_pg, m_w_pg, v_w_pg, "adamw_w_pg")
    g_wpe, d_wpe, m_wpe, v_wpe = _adamw_sum8(per_layer(r_pe), w_pe, m_w_pe, v_w_pe, "adamw_w_pe")

    layered = [(norm_g, m_norm_g, v_norm_g), (conv_b, m_conv_b, v_conv_b), (branch_g, m_branch_g, v_branch_g),
               (ple_norm_g, m_ple_norm_g, v_ple_norm_g), (b_pg, m_b_pg, v_b_pg)]
    row = lambda a: a.reshape(1, -1)
    upd, loss_row = _small_update(
        jnp.reshape(me_blk, (1,)).astype(jnp.int32),
        [(r_small[2 * k], r_small[2 * k + 1], *wmv) for k, wmv in enumerate(layered)],
        (r_small[12], row(final_g), row(m_final_g), row(v_final_g)),
        (r_small[10], r_small[11], conv_w, m_conv_w, v_conv_w), r_small[13])
    loss = loss_row[0, 0]
    upd[5] = [a.reshape(-1) for a in upd[5]]

    big = {1: (g_win, d_win, m_win, v_win), 5: (g_wout, d_wout, m_wout, v_wout), 7: (g_wpg, d_wpg, m_wpg, v_wpg),
           9: (g_wpe, d_wpe, m_wpe, v_wpe)}
    small_at = {0: 0, 2: 6, 3: 1, 4: 2, 6: 3, 8: 4, 10: 5}
    per_kind = [[(big[i] if i in big else upd[small_at[i]])[j] for i in range(11)] for j in range(4)]
    return (loss, grad_x, *per_kind[0], *per_kind[1], *per_kind[2], *per_kind[3])
```

```python
import jax
import jax.numpy as jnp
from jax import lax
from jax.experimental import pallas as pl
from jax.experimental.pallas import tpu as pltpu

F32 = jnp.float32
BF16 = jnp.bfloat16

D_MODEL = 1024
D_CONV = 512
D_SB = 512
N_IN = 4096
HEAD_DIM = 64
PLE_DIM = 256
DEPTH = 2
EPS = 1e-6
ADAM_LR = 0.001
ADAM_B1 = 0.9
ADAM_B2 = 0.999
ADAM_EPS = 1e-08
ADAM_WD = 0.01
ADAM_STEP = 10

LANES = 128
SUBLANES = 8
VMEM_BYTES_V7X = 64 * 1024 * 1024
VMEM_LIMIT = VMEM_BYTES_V7X - 8 * 1024 * 1024

N_DEV = 8
ROW_TILE = 256
FWD_ROW_TILE = 512
ATTN_TILE = 256

NT = (((1,), (1,)), ((), ()))
TN = (((0,), (0,)), ((), ()))


def _call(body, **kw):
    return pl.pallas_call(body, **kw)


def _params(sem=None, vmem=None):
    return pltpu.CompilerParams(dimension_semantics=sem, vmem_limit_bytes=vmem)


def _sigmoid(z):
    return 0.5 * jnp.tanh(0.5 * z) + 0.5


def _group_bcast_sum(a, lo):
    s_lo = jnp.sum(jnp.where(lo, a, 0.0), axis=-1, keepdims=True)
    s_hi = jnp.sum(jnp.where(lo, 0.0, a), axis=-1, keepdims=True)
    return jnp.where(lo, s_lo, s_hi)


def _layer_rows(layer, rows, width):
    return pl.BlockSpec((None, rows, width), lambda i: (layer, 0, 0))


def _my_block():
    return 4 * lax.axis_index("x") + 2 * lax.axis_index("y") + lax.axis_index("c")


def _cast_bf16(arrays, name):
    n = len(arrays)

    def body(*refs):
        for a_ref, o_ref in zip(refs[:n], refs[n:]):
            o_ref[...] = a_ref[...].astype(BF16)

    whole = lambda a: pl.BlockSpec(a.shape, lambda: (0, 0))
    return _call(
        body, name=name,
        out_shape=tuple(jax.ShapeDtypeStruct(a.shape, BF16) for a in arrays),
        in_specs=[whole(a) for a in arrays], out_specs=tuple(whole(a) for a in arrays),
        compiler_params=_params(None, VMEM_LIMIT),
    )(*arrays)


class _Comm:
    def __init__(self, inputs, out_shapes, scratch, begin, middle, finish):
        self.inputs, self.out_shapes, self.scratch = list(inputs), list(out_shapes), list(scratch)
        self.begin, self.middle, self.finish = begin, middle, finish


def _slab(kind, ref, blk):
    if kind == "cols512":
        return ref.at[:, pl.ds(blk * 512, 512)]
    if kind == "rows128":
        return ref.at[pl.ds(blk * 128, 128), :]
    if kind == "cols128":
        return ref.at[:, pl.ds(blk * 128, 128)]
    return ref.at[blk]


def _gather_comm(items):
    n_t = len(items)
    kinds = [it[2] for it in items]

    def ctx(ins, outs, sems):
        send_sems, recv_sems, local_sems = sems
        x, y, c = lax.axis_index("x"), lax.axis_index("y"), lax.axis_index("c")
        me, sibling = (x, y, c), (x, y, 1 - c)
        chips = [(1 - x, y), (x, 1 - y), (1 - x, 1 - y)]

        def place(t, dev):
            return _slab(kinds[t], outs[t], 4 * dev[0] + 2 * dev[1] + dev[2])

        def copy(t, k, block, to, own=False):
            return pltpu.make_async_remote_copy(
                src_ref=ins[t] if own else place(t, block), dst_ref=place(t, block),
                send_sem=send_sems.at[t, k], recv_sem=recv_sems.at[t, k],
                device_id=to, device_id_type=pl.DeviceIdType.MESH)

        mine = [pltpu.make_async_copy(ins[t], place(t, me), local_sems.at[t]) for t in range(n_t)]
        first = []
        for t in range(n_t):
            first.append(copy(t, 0, me, sibling, own=True))
            first += [copy(t, 1 + j, me, (*chip, c), own=True) for j, chip in enumerate(chips)]
        passed = [copy(t, 4 + j, (*chip, c), sibling) for j, chip in enumerate(chips) for t in range(n_t)]
        landed = [copy(t, 1 + j, (*chip, c), me) for j, chip in enumerate(chips) for t in range(n_t)]
        from_sibling = []
        for t in range(n_t):
            from_sibling.append(copy(t, 0, sibling, me))
            from_sibling += [copy(t, 4 + j, (*chip, 1 - c), me) for j, chip in enumerate(chips)]
        return mine, first, landed, passed, from_sibling

    def begin(ins, outs, sems):
        mine, first, _, _, _ = ctx(ins, outs, sems)
        for cp in mine + first:
            cp.start()

    def middle(ins, outs, sems):
        _, _, landed, passed, _ = ctx(ins, outs, sems)
        for got, fwd in zip(landed, passed):
            got.wait_recv()
            fwd.start()

    def finish(ins, outs, sems):
        mine, first, _, passed, from_sibling = ctx(ins, outs, sems)
        for cp in from_sibling:
            cp.wait_recv()
        for cp in first + passed:
            cp.wait_send()
        for cp in mine:
            cp.wait()

    scratch = [pltpu.SemaphoreType.DMA((n_t, 7)), pltpu.SemaphoreType.DMA((n_t, 7)), pltpu.SemaphoreType.DMA((n_t,))]
    return _Comm([it[0] for it in items], [it[1] for it in items], scratch, begin, middle, finish)


def _exchange_comm(items):
    n_t = len(items)
    kinds = [it[2] for it in items]

    def ctx(ins, outs, sems):
        send_sems, recv_sems, local_sems = sems
        x, y, c = lax.axis_index("x"), lax.axis_index("y"), lax.axis_index("c")
        me_blk = 4 * x + 2 * y + c

        def src(t, blk):
            return ins[t] if kinds[t] == "slot" else _slab(kinds[t], ins[t], blk)

        local = [pltpu.make_async_copy(src(t, me_blk), outs[t].at[me_blk], local_sems.at[t]) for t in range(n_t)]
        remote = []
        for k in range(1, N_DEV):
            px = 1 - x if k & 4 else x
            py = 1 - y if k & 2 else y
            pc_ = 1 - c if k & 1 else c
            for t in range(n_t):
                remote.append(pltpu.make_async_remote_copy(
                    src_ref=src(t, 4 * px + 2 * py + pc_), dst_ref=outs[t].at[me_blk],
                    send_sem=send_sems.at[k - 1, t], recv_sem=recv_sems.at[k - 1, t],
                    device_id=(px, py, pc_), device_id_type=pl.DeviceIdType.MESH))
        return local, remote

    def begin(ins, outs, sems):
        local, remote = ctx(ins, outs, sems)
        for cp in local + remote:
            cp.start()

    def finish(ins, outs, sems):
        local, remote = ctx(ins, outs, sems)
        for cp in remote:
            cp.wait_recv()
        for cp in remote:
            cp.wait_send()
        for cp in local:
            cp.wait()

    scratch = [pltpu.SemaphoreType.DMA((N_DEV - 1, n_t)), pltpu.SemaphoreType.DMA((N_DEV - 1, n_t)),
               pltpu.SemaphoreType.DMA((n_t,))]
    out_shapes = [jax.ShapeDtypeStruct((N_DEV, *it[1].shape), it[1].dtype) for it in items]
    return _Comm([it[0] for it in items], out_shapes, scratch, begin, None, finish)


def _comm_call(comm, name):
    n_in, n_out = len(comm.inputs), len(comm.out_shapes)

    def body(*refs):
        ins, outs, sems = refs[:n_in], refs[n_in:n_in + n_out], refs[n_in + n_out:]
        comm.begin(ins, outs, sems)
        if comm.middle is not None:
            comm.middle(ins, outs, sems)
        comm.finish(ins, outs, sems)

    any_spec = pl.BlockSpec(memory_space=pl.ANY)
    return _call(body, name=name, out_shape=tuple(comm.out_shapes), in_specs=[any_spec] * n_in,
                 out_specs=[any_spec] * n_out, scratch_shapes=comm.scratch)(*comm.inputs)


def _hosted(body, n_in, n_out, comm, first, last, middle):
    if comm is None:
        return lambda *refs: body(*refs)
    n_ci, n_co, n_cs = len(comm.inputs), len(comm.out_shapes), len(comm.scratch)

    def wrapped(*refs):
        ins, cin = refs[:n_in], refs[n_in:n_in + n_ci]
        o0 = n_in + n_ci
        outs, cout = refs[o0:o0 + n_out], refs[o0 + n_out:o0 + n_out + n_co]
        scr, csem = refs[o0 + n_out + n_co:len(refs) - n_cs], refs[len(refs) - n_cs:]
        pl.when(first())(lambda: comm.begin(cin, cout, csem))
        body(*ins, *outs, *scr)
        if comm.middle is not None:
            pl.when(middle())(lambda: comm.middle(cin, cout, csem))
        pl.when(last())(lambda: comm.finish(cin, cout, csem))

    return wrapped


def _hosted_call(body, comm, *, name, grid, out_shape, in_specs, out_specs, args, scratch_shapes=(), sem=None):
    nd = len(grid)
    first, last, middle = _at_first(nd), _at_last(nd), _at_middle(nd)
    if comm is not None:
        sem = ("arbitrary",) * nd
    n_in, n_out = len(in_specs), len(out_shape)
    any_spec = pl.BlockSpec(memory_space=pl.ANY)
    c_in = [] if comm is None else comm.inputs
    c_out = [] if comm is None else comm.out_shapes
    c_scr = [] if comm is None else comm.scratch
    outs = _call(
        _hosted(body, n_in, n_out, comm, first, last, middle), name=name, grid=grid,
        out_shape=(*out_shape, *c_out),
        in_specs=[*in_specs, *[any_spec] * len(c_in)],
        out_specs=(*out_specs, *[any_spec] * len(c_out)),
        scratch_shapes=[*scratch_shapes, *c_scr],
        compiler_params=_params(sem, VMEM_LIMIT),
    )(*args, *c_in)
    return outs[:n_out], outs[n_out:]


def _grid_step(ndim):
    i, n = pl.program_id(0), pl.num_programs(0)
    for d in range(1, ndim):
        i, n = i * pl.num_programs(d) + pl.program_id(d), n * pl.num_programs(d)
    return i, n


def _at_first(ndim):
    return lambda: _grid_step(ndim)[0] == 0


def _at_last(ndim):
    def pred():
        i, n = _grid_step(ndim)
        return i == n - 1
    return pred


def _at_middle(ndim):
    def pred():
        i, n = _grid_step(ndim)
        return i == (3 * n) // 4
    return pred


def _fwd_in(x, g, layer, w_full, name, comm=None):
    s = x.shape[0]
    ts = min(FWD_ROW_TILE, s)

    def body(x_ref, g_ref, w_ref, h_ref, pc_ref, qkv_ref, az_ref):
        xf = x_ref[...]
        r = lax.rsqrt(jnp.mean(xf * xf, axis=-1, keepdims=True) + EPS)
        h = (xf * r * g_ref[...]).astype(BF16)
        h_ref[...] = h
        pc_ref[...] = jnp.dot(h, w_ref[:, 0:2048], preferred_element_type=F32).astype(BF16)
        q = jnp.dot(h, w_ref[:, 2048:2560], preferred_element_type=F32)
        qkv_ref[:, 0:512] = (q * 0.125).astype(BF16)
        qkv_ref[:, 512:1536] = jnp.dot(h, w_ref[:, 2560:3584], preferred_element_type=F32).astype(BF16)
        az_ref[...] = jnp.dot(h, w_ref[:, 3584:4096], preferred_element_type=F32).astype(BF16)

    row = lambda width: pl.BlockSpec((ts, width), lambda i: (i, 0))
    return _hosted_call(
        body, comm, name=name, grid=(s // ts,),
        out_shape=(jax.ShapeDtypeStruct((s, D_MODEL), BF16), jax.ShapeDtypeStruct((s, 2048), BF16),
                   jax.ShapeDtypeStruct((s, 1536), BF16), jax.ShapeDtypeStruct((s, 512), BF16)),
        in_specs=[row(D_MODEL), _layer_rows(layer, 1, D_MODEL),
                  pl.BlockSpec((D_MODEL, N_IN), lambda i: (0, 0))],
        out_specs=(row(D_MODEL), row(2048), row(1536), row(512)),
        args=(x, g, w_full), sem=("parallel",))


ATTN_ROWS = 128
ATTN_DONE = 104.0


def _attn_pieces(tq, rc):
    lane = lax.broadcasted_iota(jnp.int32, (1, LANES), 1)
    lo = lane < HEAD_DIM
    row = lax.broadcasted_iota(jnp.int32, (tq, tq), 0)
    col = lax.broadcasted_iota(jnp.int32, (tq, tq), 1)
    tri_gt = jnp.where(row > col, 1.0, 0.0).astype(BF16)
    tri_le = jnp.where(row <= col, 1.0, 0.0).astype(BF16)
    rrow = lax.broadcasted_iota(jnp.int32, (rc, tq), 0)
    rcol = lax.broadcasted_iota(jnp.int32, (rc, tq), 1)
    causal = [rcol < rrow + r * rc for r in range(tq // rc)]
    return lo, causal, tri_gt, tri_le


def _split_heads(a, lo):
    z = jnp.zeros_like(a)
    return (jnp.where(lo, a, z), jnp.where(lo, z, a))


def _softplus(z, causal, diag):
    neg_abs = lax.bitcast_convert_type(lax.bitcast_convert_type(z, jnp.uint32) | jnp.uint32(0x80000000), F32)
    sp = jnp.maximum(z, 0.0) + jnp.log(1.0 + jnp.exp(neg_abs))
    if diag:
        sp = jnp.where(causal, sp, 0.0)
    return sp


def _attn_fwd(qkv, name, comm=None):
    s = qkv.shape[0]
    tq = min(ATTN_TILE, s)
    nq = s // tq
    rc = min(ATTN_ROWS, tq)
    n_rc = tq // rc
    nt = 2 if nq % 2 == 0 else 1
    chains = [(t, r, hh) for t in range(nt) for r in range(n_rc) for hh in range(2)]

    def body(q_ref, k_ref, v_ref, o_ref, lsum_ref, nblk_ref):
        hp = pl.program_id(0)
        qis = [pl.program_id(1) * nt + t for t in range(nt)]
        lo, causal, tri_gt, _ = _attn_pieces(tq, rc)
        qh = _split_heads(q_ref[...], lo)
        qc = {(t, r, hh): qh[hh][t * tq + r * rc:t * tq + (r + 1) * rc] for t, r, hh in chains}

        mm = lambda a_, b_: jnp.dot(a_.astype(BF16), b_, preferred_element_type=F32)
        rowsum = lambda a_: jnp.sum(a_, axis=-1, keepdims=True)

        def block(kb, carry, t):
            mine = [ch for ch in chains if ch[0] == t]
            start = pl.multiple_of(kb * tq, tq)
            k = k_ref[pl.ds(start, tq), :]
            vh = _split_heads(v_ref[pl.ds(start, tq), :], lo)
            z = {ch: lax.dot_general(qc[ch], k, NT, preferred_element_type=F32) for ch in mine}
            sp = {ch: _softplus(z[ch], None, False) for ch in mine}
            later = {ch: mm(sp[ch], tri_gt) for ch in mine}
            a = {ch: jnp.exp((z[ch] - sp[ch]) - (carry[ch[1]][1 + ch[2]] + later[ch])) for ch in mine}
            pv = {ch: mm(a[ch], vh[ch[2]]) for ch in mine}
            return tuple((carry[r][0] + pv[(t, r, 0)] + pv[(t, r, 1)], carry[r][1] + rowsum(sp[(t, r, 0)]),
                          carry[r][2] + rowsum(sp[(t, r, 1)])) for r in range(n_rc))

        def first_two():
            ok = [qi > 0 for qi in qis]
            d0 = [pl.multiple_of(qi * tq, tq) for qi in qis]
            p0 = [pl.multiple_of(jnp.maximum(qi - 1, 0) * tq, tq) for qi in qis]
            k_d = [k_ref[pl.ds(d0[t], tq), :] for t in range(nt)]
            k_p = [k_ref[pl.ds(p0[t], tq), :] for t in range(nt)]
            vh_d = [_split_heads(v_ref[pl.ds(d0[t], tq), :], lo) for t in range(nt)]
            vh_p = [_split_heads(v_ref[pl.ds(p0[t], tq), :], lo) for t in range(nt)]
            z_d = {ch: lax.dot_general(qc[ch], k_d[ch[0]], NT, preferred_element_type=F32) for ch in chains}
            z_p = {ch: lax.dot_general(qc[ch], k_p[ch[0]], NT, preferred_element_type=F32) for ch in chains}
            sp_d = {ch: _softplus(z_d[ch], causal[ch[1]], True) for ch in chains}
            sp_raw = {ch: _softplus(z_p[ch], None, False) for ch in chains}
            sp_p = {ch: jnp.where(ok[ch[0]], sp_raw[ch], 0.0) for ch in chains}
            later_d = {ch: mm(sp_d[ch], tri_gt) for ch in chains}
            later_p = {ch: mm(sp_p[ch], tri_gt) for ch in chains}
            c_d = {ch: rowsum(sp_d[ch]) for ch in chains}
            a_d = {ch: jnp.where(causal[ch[1]], jnp.exp((z_d[ch] - sp_d[ch]) - later_d[ch]), 0.0) for ch in chains}
            a_p = {ch: jnp.where(ok[ch[0]], jnp.exp((z_p[ch] - sp_raw[ch]) - (c_d[ch] + later_p[ch])), 0.0)
                   for ch in chains}
            pv = {ch: mm(a_d[ch], vh_d[ch[0]][ch[2]]) + mm(a_p[ch], vh_p[ch[0]][ch[2]]) for ch in chains}
            return [tuple((pv[(t, r, 0)] + pv[(t, r, 1)], c_d[(t, r, 0)] + rowsum(sp_p[(t, r, 0)]),
                           c_d[(t, r, 1)] + rowsum(sp_p[(t, r, 1)])) for r in range(n_rc)) for t in range(nt)]

        def least(carry):
            m = jnp.minimum(carry[0][1], carry[0][2])
            for r in range(1, n_rc):
                m = jnp.minimum(m, jnp.minimum(carry[r][1], carry[r][2]))
            return jnp.min(m)

        carries = first_two()
        for t, qi in enumerate(qis):
            def go_on(st, qi=qi):
                return jnp.logical_and(st[0] < qi - 1, st[1] < ATTN_DONE)

            def step(st, qi=qi, t=t):
                new = block(qi - 2 - st[0], st[2], t)
                return st[0] + 1, least(new), new

            walked, _, carry = lax.while_loop(go_on, step, (jnp.int32(0), least(carries[t]), carries[t]))
            for r in range(n_rc):
                rows = slice(t * tq + r * rc, t * tq + (r + 1) * rc)
                o_ref[rows, :] = carry[r][0].astype(BF16)
                lsum_ref[rows, :] = jnp.where(lo, carry[r][1], carry[r][2])
            nblk_ref[hp, qi] = walked.astype(F32)

    blk = pl.BlockSpec((nt * tq, LANES), lambda hp, qg: (qg, hp))
    o512 = jax.ShapeDtypeStruct((s, D_SB), F32)
    return _hosted_call(
        body, comm, name=name, grid=(4, nq // nt),
        out_shape=(jax.ShapeDtypeStruct((s, D_SB), BF16), o512, jax.ShapeDtypeStruct((4, nq), F32)),
        in_specs=[blk, pl.BlockSpec((s, LANES), lambda hp, qg: (0, 4 + hp)),
                  pl.BlockSpec((s, LANES), lambda hp, qg: (0, 8 + hp))],
        out_specs=(blk, blk, pl.BlockSpec(memory_space=pltpu.SMEM)),
        args=(qkv, qkv, qkv), sem=("arbitrary", "arbitrary"))


HALO = 16


def _conv_taps(cc_ref, ch_ref, ccp_ref, chp_ref, halo_ref, first):
    u = cc_ref[...].astype(F32) * ch_ref[...].astype(F32)
    halo_ref[...] = ccp_ref[...].astype(F32) * chp_ref[...].astype(F32) * jnp.where(first, 0.0, 1.0)
    p6 = halo_ref[HALO - 2:HALO - 1, :]
    p7 = halo_ref[HALO - 1:HALO, :]
    rowi = lax.broadcasted_iota(jnp.int32, u.shape, 0)
    u1 = jnp.where(rowi == 0, p7, pltpu.roll(u, 1, 0))
    u2 = jnp.where(rowi == 0, p6, jnp.where(rowi == 1, p7, pltpu.roll(u, 2, 0)))
    return u, u1, u2


def _fwd_mid(x, pc, az, ya, p4, layer, cw, cb, bg, wout_full, pg, wpg_full, bpg, wpe_full, name, comm=None,
             head=None):
    s = x.shape[0]
    ts = min(FWD_ROW_TILE, s)
    blk_h = ts // HALO
    n_in = 18 + (2 if head else 0)

    def body(*refs):
        (x_ref, cb_ref_, cc_ref, ch_ref, cz_ref, ccp_ref, chp_ref, az_ref, ya_ref, p_ref,
         cw_ref, cbias_ref, bg_ref, wout_ref, pg_ref, wpg_ref, bpg_ref, wpe_ref) = refs[:18]
        x2_ref, x3_ref, gated_ref, h2_ref, gate_ref, e_ref = refs[n_in:n_in + 6]
        halo_ref = refs[-1]
        i = pl.program_id(0)
        lane = lax.broadcasted_iota(jnp.int32, (1, LANES), 1)
        lo = lane < HEAD_DIM
        u, u1, u2 = _conv_taps(cc_ref, ch_ref, ccp_ref, chp_ref, halo_ref, i == 0)
        conv = cbias_ref[...] + cw_ref[0:1, :] * u2 + cw_ref[1:2, :] * u1 + cw_ref[2:3, :] * u
        yc = cb_ref_[...].astype(F32) * conv
        for sl in range(8):
            cols = slice(LANES * (sl % 4), LANES * (sl % 4 + 1))
            y = yc[:, cols] if sl < 4 else ya_ref[:, cols].astype(F32)
            zc = (cz_ref[:, cols] if sl < 4 else az_ref[:, cols]).astype(F32)
            rg = lax.rsqrt(_group_bcast_sum(y * y, lo) * (1.0 / HEAD_DIM) + EPS)
            yn = y * rg * bg_ref[:, LANES * sl:LANES * (sl + 1)]
            gated_ref[:, LANES * sl:LANES * (sl + 1)] = (yn * (zc * _sigmoid(zc))).astype(BF16)
        x2 = x_ref[...] + jnp.dot(gated_ref[...], wout_ref[...], preferred_element_type=F32)
        x2_ref[...] = x2
        r2 = lax.rsqrt(jnp.mean(x2 * x2, axis=-1, keepdims=True) + EPS)
        h2 = (x2 * r2 * pg_ref[...]).astype(BF16)
        h2_ref[...] = h2
        gate = _sigmoid(jnp.dot(h2, wpg_ref[...], preferred_element_type=F32) + bpg_ref[...])
        gate_ref[...] = gate.astype(BF16)
        e = jnp.dot(p_ref[...].astype(BF16), wpe_ref[...], preferred_element_type=F32)
        e_ref[...] = e.astype(BF16)
        x3 = x2 + gate * e
        if not head:
            x3_ref[...] = x3
            return
        t_ref, fg_ref = refs[18:20]
        loss_ref, dfg_ref = refs[n_in + 6:n_in + 8]
        dx, loss, dfg = _loss_math(x3, t_ref[...], fg_ref[...])

        @pl.when(i == 0)
        def _():
            loss_ref[...] = jnp.zeros_like(loss_ref)
            dfg_ref[...] = jnp.zeros_like(dfg_ref)

        x3_ref[...] = dx
        loss_ref[...] += loss
        dfg_ref[...] += dfg

    row = lambda width, cb_=0: pl.BlockSpec((ts, width), lambda i: (i, cb_))
    prev = lambda cb_: pl.BlockSpec((HALO, 512), lambda i: (jnp.maximum(i * blk_h - 1, 0), cb_))
    vec = lambda width: pl.BlockSpec((1, width), lambda i: (0, 0))
    lvec = lambda width: _layer_rows(layer, 1, width)
    wspec = lambda r_, c_: pl.BlockSpec((r_, c_), lambda i: (0, 0))
    f32o = jax.ShapeDtypeStruct((s, D_MODEL), F32)
    bfo = jax.ShapeDtypeStruct((s, D_MODEL), BF16)
    head_in = [row(D_MODEL), vec(D_MODEL)] if head else []
    head_out = [jax.ShapeDtypeStruct((1, LANES), F32), jax.ShapeDtypeStruct((1, D_MODEL), F32)] if head else []
    return _hosted_call(
        body, comm, name=name, grid=(s // ts,),
        out_shape=(f32o, f32o, bfo, bfo, bfo, bfo, *head_out),
        scratch_shapes=[pltpu.VMEM((HALO, 512), F32)],
        in_specs=[row(D_MODEL), row(512, 0), row(512, 1), row(512, 2), row(512, 3), prev(1), prev(2),
                  row(512), row(512),
                  pl.BlockSpec((None, None, ts, PLE_DIM), lambda i: (layer, 0, i, 0)),
                  _layer_rows(layer, 3, 512), lvec(512), lvec(D_MODEL),
                  wspec(D_MODEL, D_MODEL), lvec(D_MODEL), wspec(D_MODEL, D_MODEL), lvec(D_MODEL),
                  wspec(PLE_DIM, D_MODEL), *head_in],
        out_specs=(*[row(D_MODEL)] * 6, *([vec(LANES), vec(D_MODEL)] if head else [])),
        args=(x, pc, pc, pc, pc, pc, pc, az, ya, p4, cw, cb, bg, wout_full, pg, wpg_full, bpg, wpe_full,
              *(head or ())),
        sem=("arbitrary",) if head else ("parallel",))


def _loss_math(x, target, g):
    r = lax.rsqrt(jnp.mean(x * x, axis=-1, keepdims=True) + EPS)
    xn = x * r
    err = xn * g - target
    per_row = jnp.sum(err * err, axis=-1, keepdims=True)
    loss = jnp.sum(per_row, axis=0, keepdims=True) * (0.5 / D_MODEL)
    dy = err * (1.0 / D_MODEL)
    dg = jnp.sum(dy * xn, axis=0, keepdims=True)
    dxn = dy * g
    return r * (dxn - xn * jnp.mean(dxn * xn, axis=-1, keepdims=True)), loss, dg


def _bwd_mid(dx3, x2, gate, e, pc, az, ya, gated, h2, p4, layer, cw, cb, bg, pg, wpg_full, wout_full, name,
             comm=None):
    s = x2.shape[0]
    ts = min(ROW_TILE, s)
    blk_h = ts // HALO

    def body(dx3_ref, x2_ref, gate_ref, e_ref, cb_ref_, cc_ref, ch_ref, cz_ref, ccp_ref, chp_ref, az_ref, ya_ref,
             gated_ref, h2_ref, p_ref, cw_ref, cbias_ref, bg_ref, pg_ref, wpg_ref, wout_ref,
             dx2_ref, dya_ref, dmisc_ref, dconv_ref, dwout_ref, dwpg_ref, dwpe_ref,
             dbpg_ref, dpg_ref, dbg_ref, dcbias_ref, dcw_ref,
             dgated_ref, halo_ref, acc_out, acc_pg, acc_pe):
        i = pl.program_id(0)

        @pl.when(i == 0)
        def _():
            for ref in (dbpg_ref, dpg_ref, dbg_ref, dcbias_ref, dcw_ref, acc_out, acc_pg, acc_pe):
                ref[...] = jnp.zeros_like(ref)

        lane = lax.broadcasted_iota(jnp.int32, (1, LANES), 1)
        lo = lane < HEAD_DIM
        dx3 = dx3_ref[...]
        gate = gate_ref[...].astype(F32)
        de_b = (dx3 * gate).astype(BF16)
        dgpre = dx3 * e_ref[...].astype(F32) * gate * (1.0 - gate)
        dbpg_ref[...] += jnp.sum(dgpre, axis=0, keepdims=True)
        dgpre_b = dgpre.astype(BF16)
        dh2 = lax.dot_general(dgpre_b, wpg_ref[...], NT, preferred_element_type=F32)
        acc_pe[...] += lax.dot_general(p_ref[...].astype(BF16), de_b, TN, preferred_element_type=F32)
        acc_pg[...] += lax.dot_general(h2_ref[...], dgpre_b, TN, preferred_element_type=F32)

        u, u1, u2 = _conv_taps(cc_ref, ch_ref, ccp_ref, chp_ref, halo_ref, i == 0)
        conv = cbias_ref[...] + cw_ref[0:1, :] * u2 + cw_ref[1:2, :] * u1 + cw_ref[2:3, :] * u
        c_b = cb_ref_[...].astype(F32)
        yc = c_b * conv
        fwd = []
        for sl in range(8):
            cols = slice(LANES * (sl % 4), LANES * (sl % 4 + 1))
            y = yc[:, cols] if sl < 4 else ya_ref[:, cols].astype(F32)
            zc = (cz_ref[:, cols] if sl < 4 else az_ref[:, cols]).astype(F32)
            rg = lax.rsqrt(_group_bcast_sum(y * y, lo) * (1.0 / HEAD_DIM) + EPS)
            sig = _sigmoid(zc)
            fwd.append((rg, y * rg, zc * sig, sig * (1.0 + zc * (1.0 - sig))))

        x2 = x2_ref[...]
        r2 = lax.rsqrt(jnp.mean(x2 * x2, axis=-1, keepdims=True) + EPS)
        xn2 = x2 * r2
        dpg_ref[...] += jnp.sum(dh2 * xn2, axis=0, keepdims=True)
        dxn = dh2 * pg_ref[...]
        dx2 = dx3 + r2 * (dxn - xn2 * jnp.mean(dxn * xn2, axis=-1, keepdims=True))
        dx2_ref[...] = dx2
        dx2_b = dx2.astype(BF16)
        dgated_ref[...] = lax.dot_general(dx2_b, wout_ref[...], NT, preferred_element_type=F32)
        acc_out[...] += lax.dot_general(gated_ref[...], dx2_b, TN, preferred_element_type=F32)

        for sl in range(8):
            cols = slice(LANES * (sl % 4), LANES * (sl % 4 + 1))
            wide = slice(LANES * sl, LANES * (sl + 1))
            rg, yhat, silu, dsilu = fwd[sl]
            bgs = bg_ref[:, wide]
            dgt = dgated_ref[:, wide]
            dyn = dgt * silu
            dzc = dgt * (yhat * bgs) * dsilu
            dbg_ref[:, wide] += jnp.sum(dyn * yhat, axis=0, keepdims=True)
            dyh = dyn * bgs
            dy = rg * (dyh - yhat * (_group_bcast_sum(dyh * yhat, lo) * (1.0 / HEAD_DIM)))
            if sl < 4:
                dconv = dy * c_b[:, cols]
                dmisc_ref[:, cols] = (dy * conv[:, cols]).astype(BF16)
                dmisc_ref[:, 512 + LANES * sl:512 + LANES * (sl + 1)] = dzc.astype(BF16)
                dconv_ref[:, cols] = dconv
                dcbias_ref[:, cols] += jnp.sum(dconv, axis=0, keepdims=True)
                dcw_ref[0:1, cols] += jnp.sum(dconv * u2[:, cols], axis=0, keepdims=True)
                dcw_ref[1:2, cols] += jnp.sum(dconv * u1[:, cols], axis=0, keepdims=True)
                dcw_ref[2:3, cols] += jnp.sum(dconv * u[:, cols], axis=0, keepdims=True)
            else:
                dya_ref[:, cols] = dy.astype(BF16)
                dmisc_ref[:, 1024 + LANES * (sl - 4):1024 + LANES * (sl - 3)] = dzc.astype(BF16)

        @pl.when(i == pl.num_programs(0) - 1)
        def _():
            dwout_ref[...] = acc_out[...].astype(BF16)
            dwpg_ref[...] = acc_pg[...].astype(BF16)
            dwpe_ref[...] = acc_pe[...].astype(BF16)

    row = lambda width, cb_=0: pl.BlockSpec((ts, width), lambda i: (i, cb_))
    prev = lambda cb_: pl.BlockSpec((HALO, 512), lambda i: (jnp.maximum(i * blk_h - 1, 0), cb_))
    vec = lambda width: pl.BlockSpec((1, width), lambda i: (0, 0))
    lvec = lambda width: _layer_rows(layer, 1, width)
    wspec = lambda r_, c_: pl.BlockSpec((r_, c_), lambda i: (0, 0))
    vo = lambda width: jax.ShapeDtypeStruct((1, width), F32)
    sq = jax.ShapeDtypeStruct((D_MODEL, D_MODEL), BF16)
    return _hosted_call(
        body, comm, name=name, grid=(s // ts,), sem=("arbitrary",),
        args=(dx3, x2, gate, e, pc, pc, pc, pc, pc, pc, az, ya, gated, h2, p4, cw, cb, bg, pg, wpg_full, wout_full),
        out_shape=(jax.ShapeDtypeStruct((s, D_MODEL), F32), jax.ShapeDtypeStruct((s, 512), BF16),
                   jax.ShapeDtypeStruct((s, 1536), BF16), jax.ShapeDtypeStruct((s, 512), F32),
                   sq, sq, jax.ShapeDtypeStruct((PLE_DIM, D_MODEL), BF16),
                   vo(D_MODEL), vo(D_MODEL), vo(D_MODEL), vo(512), jax.ShapeDtypeStruct((SUBLANES, 512), F32)),
        in_specs=[row(D_MODEL), row(D_MODEL), row(D_MODEL), row(D_MODEL),
                  row(512, 0), row(512, 1), row(512, 2), row(512, 3), prev(1), prev(2), row(512), row(512),
                  row(D_MODEL), row(D_MODEL),
                  pl.BlockSpec((None, None, ts, PLE_DIM), lambda i: (layer, 0, i, 0)),
                  _layer_rows(layer, 3, 512), lvec(512), lvec(D_MODEL), lvec(D_MODEL),
                  wspec(D_MODEL, D_MODEL), wspec(D_MODEL, D_MODEL)],
        out_specs=(row(D_MODEL), row(512), row(1536), row(512),
                   wspec(D_MODEL, D_MODEL), wspec(D_MODEL, D_MODEL), wspec(PLE_DIM, D_MODEL),
                   vec(D_MODEL), vec(D_MODEL), vec(D_MODEL), vec(512),
                   pl.BlockSpec((SUBLANES, 512), lambda i: (0, 0))),
        scratch_shapes=[pltpu.VMEM((ts, D_MODEL), F32), pltpu.VMEM((HALO, 512), F32),
                        pltpu.VMEM((D_MODEL, D_MODEL), F32), pltpu.VMEM((D_MODEL, D_MODEL), F32),
                        pltpu.VMEM((PLE_DIM, D_MODEL), F32)])


def _attn_bwd(qkv, lsum, nblk, dya, name, comm=None):
    s = qkv.shape[0]
    tq = min(ATTN_TILE, s)
    nq = s // tq
    rc = min(ATTN_ROWS, tq)
    n_rc = tq // rc
    chains = [(r, hh) for r in range(n_rc) for hh in range(2)]

    def body(nblk_ref, q_ref, k_ref, v_ref, lsum_ref, do_ref, dq_ref, dk_ref, dv_ref, dk_acc, dv_acc):
        hp, qi = pl.program_id(0), pl.program_id(1)

        @pl.when(qi == 0)
        def _():
            dk_acc[...] = jnp.zeros_like(dk_acc)
            dv_acc[...] = jnp.zeros_like(dv_acc)

        lo, causal, tri_gt, tri_le = _attn_pieces(tq, rc)
        lane = lax.broadcasted_iota(jnp.int32, (1, LANES), 1)
        qh = _split_heads(q_ref[...], lo)
        doh = _split_heads(do_ref[...].astype(BF16), lo)
        lt = lsum_ref[...]
        ltot_h = (jnp.sum(jnp.where(lane == 0, lt, 0.0), axis=-1, keepdims=True),
                  jnp.sum(jnp.where(lane == HEAD_DIM, lt, 0.0), axis=-1, keepdims=True))
        rows = lambda a_, r: a_[r * rc:(r + 1) * rc]
        qc = {(r, hh): rows(qh[hh], r) for r, hh in chains}
        doc = {(r, hh): rows(doh[hh], r) for r, hh in chains}
        ltot = {(r, hh): rows(ltot_h[hh], r) for r, hh in chains}

        mm = lambda a_, b_: jnp.dot(a_.astype(BF16), b_, preferred_element_type=F32)
        mm_nt = lambda a_, b_: lax.dot_general(a_, b_, NT, preferred_element_type=F32)
        mm_tn = lambda a_, b_: lax.dot_general(a_.astype(BF16), b_, TN, preferred_element_type=F32)
        rowsum = lambda a_: jnp.sum(a_, axis=-1, keepdims=True)

        def block(kb, carry, diag=False):
            start = pl.multiple_of(kb * tq, tq)
            k = k_ref[pl.ds(start, tq), :]
            v = v_ref[pl.ds(start, tq), :]
            kh = _split_heads(k, lo)
            keep = (lambda ch, a_: jnp.where(causal[ch[0]], a_, 0.0)) if diag else (lambda ch, a_: a_)
            z = {ch: mm_nt(qc[ch], k) for ch in chains}
            da = {ch: mm_nt(doc[ch], v) for ch in chains}
            sp = {ch: _softplus(z[ch], causal[ch[0]], diag) for ch in chains}
            later = {ch: mm(sp[ch], tri_gt) for ch in chains}
            walked = {ch: carry[ch[0]][1 + ch[1]] + rowsum(sp[ch]) for ch in chains}
            a = {ch: keep(ch, jnp.exp((z[ch] - sp[ch]) - ((ltot[ch] - walked[ch]) + later[ch]))) for ch in chains}
            g = {ch: a[ch] * da[ch] for ch in chains}
            upto = {ch: mm(g[ch], tri_le) for ch in chains}
            dz = {ch: keep(ch, g[ch] - jnp.exp(z[ch] - sp[ch]) * (carry[ch[0]][3 + ch[1]] + upto[ch])).astype(BF16)
                  for ch in chains}
            dqc = {ch: mm(dz[ch], kh[ch[1]]) for ch in chains}
            dkc = [mm_tn(dz[ch], qc[ch]) for ch in chains]
            dvc = [mm_tn(a[ch], doc[ch]) for ch in chains]
            dk_acc[pl.ds(start, tq), :] += sum(dkc[1:], dkc[0])
            dv_acc[pl.ds(start, tq), :] += sum(dvc[1:], dvc[0])
            return tuple((carry[r][0] + dqc[(r, 0)] + dqc[(r, 1)], walked[(r, 0)], walked[(r, 1)],
                          carry[r][3] + rowsum(g[(r, 0)]), carry[r][4] + rowsum(g[(r, 1)])) for r in range(n_rc))

        zc = jnp.zeros((rc, 1), F32)
        carry = tuple((jnp.zeros((rc, LANES), F32), zc, zc, zc, zc) for _ in range(n_rc))
        near = jnp.maximum(qi - 1, 0)
        first = near - jnp.clip(nblk_ref[hp, qi].astype(jnp.int32), 0, near)
        carry = lax.fori_loop(first, qi, block, carry)
        carry = block(qi, carry, True)
        for r in range(n_rc):
            dq_ref[r * rc:(r + 1) * rc, :] = (carry[r][0] * 0.125).astype(BF16)

        @pl.when(qi == pl.num_programs(1) - 1)
        def _():
            dk_ref[...] = dk_acc[...].astype(BF16)
            dv_ref[...] = dv_acc[...].astype(BF16)

    blk = pl.BlockSpec((tq, LANES), lambda hp, qi: (qi, hp))
    col = pl.BlockSpec((s, LANES), lambda hp, qi: (0, hp))
    o512 = jax.ShapeDtypeStruct((s, D_SB), BF16)
    return _hosted_call(
        body, comm, name=name, grid=(4, nq),
        out_shape=(o512, o512, o512),
        in_specs=[pl.BlockSpec(memory_space=pltpu.SMEM), blk,
                  pl.BlockSpec((s, LANES), lambda hp, qi: (0, 4 + hp)),
                  pl.BlockSpec((s, LANES), lambda hp, qi: (0, 8 + hp)), blk, blk],
        out_specs=(blk, col, col),
        scratch_shapes=[pltpu.VMEM((s, LANES), F32), pltpu.VMEM((s, LANES), F32)],
        args=(nblk, qkv, qkv, qkv, lsum, dya), sem=("parallel", "arbitrary"))


def _bwd_dproj(dmisc, dconv, pc, dq, dk, dv, x, dx2, g, cw, layer, win_full, name, comm=None, h=None,
               h_rows=None):
    s = x.shape[0]
    ts = min(ROW_TILE, s)
    blk8 = ts // SUBLANES
    last8 = s // SUBLANES - 1
    fused = h is not None
    emit_dproj = not fused or h_rows is not None
    dw_rows, h_blk = (D_MODEL, 0) if h_rows is None else h_rows

    def body(*refs):
        (dcb_ref, dcz_ref, daz_ref, dconv_ref, nxt_ref, cc_ref, ch_ref, dq_ref, dk_ref, dv_ref,
         x_ref, dx2_ref, g_ref, cw_ref, w_ref) = refs[:15]
        rest = list(refs[15:])
        h_ref = rest.pop(0) if fused else None
        dproj_ref = rest.pop(0) if emit_dproj else None
        dx_ref, dg_ref = rest.pop(0), rest.pop(0)
        dw_ref = rest.pop(0) if fused else None
        dproj_ref = dproj_ref if emit_dproj else rest.pop(0)
        acc_ref = rest.pop(0) if fused else None
        i = pl.program_id(0)

        @pl.when(i == 0)
        def _():
            dg_ref[...] = jnp.zeros_like(dg_ref)
            if fused:
                acc_ref[...] = jnp.zeros_like(acc_ref)

        keep = jnp.where(i == pl.num_programs(0) - 1, 0.0, 1.0)
        dc = dconv_ref[...]
        n0 = nxt_ref[0:1, :] * keep
        n1 = nxt_ref[1:2, :] * keep
        rowi = lax.broadcasted_iota(jnp.int32, dc.shape, 0)
        dc1 = jnp.where(rowi == ts - 1, n0, pltpu.roll(dc, ts - 1, 0))
        dc2 = jnp.where(rowi == ts - 2, n0, jnp.where(rowi == ts - 1, n1, pltpu.roll(dc, ts - 2, 0)))
        du = cw_ref[2:3, :] * dc + cw_ref[1:2, :] * dc1 + cw_ref[0:1, :] * dc2
        dproj_ref[:, 0:512] = dcb_ref[...]
        dproj_ref[:, 512:1024] = (du * ch_ref[...].astype(F32)).astype(BF16)
        dproj_ref[:, 1024:1536] = (du * cc_ref[...].astype(F32)).astype(BF16)
        dproj_ref[:, 1536:2048] = dcz_ref[...]
        dproj_ref[:, 2048:2560] = dq_ref[...]
        dproj_ref[:, 2560:3072] = dk_ref[...]
        dproj_ref[:, 3072:3584] = dv_ref[...]
        dproj_ref[:, 3584:4096] = daz_ref[...]
        dh = lax.dot_general(dproj_ref[...], w_ref[...], NT, preferred_element_type=F32)
        if fused:
            acc_ref[...] += lax.dot_general(h_ref[...], dproj_ref[...], TN, preferred_element_type=F32)
        x = x_ref[...]
        r = lax.rsqrt(jnp.mean(x * x, axis=-1, keepdims=True) + EPS)
        xn = x * r
        dg_ref[...] += jnp.sum(dh * xn, axis=0, keepdims=True)
        dxn = dh * g_ref[...]
        dx_ref[...] = dx2_ref[...] + r * (dxn - xn * jnp.mean(dxn * xn, axis=-1, keepdims=True))
        if fused:
            @pl.when(i == pl.num_programs(0) - 1)
            def _():
                dw_ref[...] = acc_ref[...].astype(BF16)

    row = lambda width, cb_=0: pl.BlockSpec((ts, width), lambda i: (i, cb_))
    nxt = pl.BlockSpec((SUBLANES, 512), lambda i: (jnp.minimum((i + 1) * blk8, last8), 0))
    vec = lambda width: pl.BlockSpec((1, width), lambda i: (0, 0))
    lvec = lambda width: _layer_rows(layer, 1, width)
    once = dict(pipeline_mode=pl.Buffered(1)) if fused else {}
    whole = lambda rows_: pl.BlockSpec((rows_, N_IN), lambda i: (0, 0), **once)
    in_specs = [row(512, 0), row(512, 1), row(512, 2), row(512), nxt, row(512, 1), row(512, 2),
                row(512), row(512), row(512), row(D_MODEL), row(D_MODEL), lvec(D_MODEL),
                _layer_rows(layer, 3, 512), whole(D_MODEL)]
    args = [dmisc, dmisc, dmisc, dconv, dconv, pc, pc, dq, dk, dv, x, dx2, g, cw, win_full]
    out_shape = [jax.ShapeDtypeStruct((s, D_MODEL), F32), jax.ShapeDtypeStruct((1, D_MODEL), F32)]
    out_specs = [row(D_MODEL), vec(D_MODEL)]
    scratch = []
    if emit_dproj:
        out_shape.insert(0, jax.ShapeDtypeStruct((s, N_IN), BF16))
        out_specs.insert(0, row(N_IN))
    else:
        scratch.append(pltpu.VMEM((ts, N_IN), BF16))
    if fused:
        in_specs.append(row(dw_rows, h_blk))
        args.append(h)
        out_shape.append(jax.ShapeDtypeStruct((dw_rows, N_IN), BF16))
        out_specs.append(whole(dw_rows))
        scratch.append(pltpu.VMEM((dw_rows, N_IN), F32))
    return _hosted_call(
        body, comm, name=name, grid=(s // ts,), out_shape=tuple(out_shape), in_specs=in_specs,
        out_specs=tuple(out_specs), scratch_shapes=scratch, args=tuple(args), sem=("arbitrary",))


def _atb(a, b, name, a_cols=None, comm=None):
    s, n = b.shape
    m, a_blk = (a.shape[-1], 0) if a_cols is None else a_cols
    ts = min(512, s)
    tn = min(2048, n)
    a_spec = pl.BlockSpec((ts, m), lambda j, i: (i, a_blk))

    def body(a_ref, b_ref, o_ref, acc_ref):
        i = pl.program_id(1)

        @pl.when(i == 0)
        def _():
            acc_ref[...] = jnp.zeros_like(acc_ref)

        acc_ref[...] += lax.dot_general(a_ref[...].astype(BF16), b_ref[...], TN, preferred_element_type=F32)

        @pl.when(i == pl.num_programs(1) - 1)
        def _():
            o_ref[...] = acc_ref[...].astype(BF16)

    (out,), got = _hosted_call(
        body, comm, name=name, grid=(n // tn, s // ts),
        out_shape=(jax.ShapeDtypeStruct((m, n), BF16),),
        in_specs=[a_spec, pl.BlockSpec((ts, tn), lambda j, i: (i, j))],
        out_specs=(pl.BlockSpec((m, tn), lambda j, i: (0, j)),),
        scratch_shapes=[pltpu.VMEM((m, tn), F32)],
        args=(a, b), sem=("parallel", "arbitrary"))
    return out, got


def _adamw_math(w, g, m, v):
    m2 = ADAM_B1 * m + (1.0 - ADAM_B1) * g
    v2 = ADAM_B2 * v + (1.0 - ADAM_B2) * (g * g)
    m_hat = m2 / (1.0 - ADAM_B1 ** ADAM_STEP)
    v_hat = v2 / (1.0 - ADAM_B2 ** ADAM_STEP)
    delta = -ADAM_LR * (m_hat / (jnp.sqrt(v_hat) + ADAM_EPS) + ADAM_WD * w)
    return delta, m2, v2


def _adamw_sum8(pieces, w, m, v, name):
    _, rows, cols = w.shape
    tr = min([rows, 256] + [pc_[0].shape[1] for pc_ in pieces])
    n_tiles = rows // tr
    n_p = len(pieces)
    spans = [(layer, row0 // tr, arr.shape[1] // tr) for arr, layer, row0 in pieces]

    def body(*refs):
        p_refs = refs[:n_p]
        w_ref, m_ref, v_ref, g_ref, d_ref, m2_ref, v2_ref = refs[n_p:]
        l, i = pl.program_id(0), pl.program_id(1)

        def run(p_ref):
            g = p_ref[0].astype(F32)
            for d in range(1, N_DEV):
                g = g + p_ref[d].astype(F32)
            g_ref[...] = g
            d_ref[...], m2_ref[...], v2_ref[...] = _adamw_math(w_ref[...], g, m_ref[...], v_ref[...])

        for p_ref, (layer, t0, nt) in zip(p_refs, spans):
            mine = jnp.logical_and(l == layer, jnp.logical_and(i >= t0, i < t0 + nt))
            pl.when(mine)(lambda p_ref=p_ref: run(p_ref))

    def piece_spec(layer, t0, nt):
        return pl.BlockSpec((N_DEV, tr, cols),
                            lambda l, i: (0, jnp.clip(jnp.where(l == layer, i - t0, jnp.where(l < layer, 0, nt - 1)),
                                                      0, nt - 1), 0))

    tile = pl.BlockSpec((None, tr, cols), lambda l, i: (l, i, 0))
    o = jax.ShapeDtypeStruct((DEPTH, rows, cols), F32)
    return _call(
        body, name=name, grid=(DEPTH, n_tiles),
        out_shape=(o, o, o, o),
        in_specs=[*[piece_spec(*sp) for sp in spans], tile, tile, tile],
        out_specs=(tile, tile, tile, tile),
        compiler_params=_params(("arbitrary", "arbitrary"), VMEM_LIMIT),
    )(*[pc_[0] for pc_ in pieces], w, m, v)


def _small_update(blk, layered, final, conv, loss_parts):
    n_l = len(layered)
    ins = [a for item in layered for a in item] + list(final) + list(conv) + [loss_parts]
    shapes = [item[2].shape for item in layered] + [final[1].shape, conv[2].shape]
    out_shape = [jax.ShapeDtypeStruct(sh, F32) for sh in shapes for _ in range(4)]
    out_shape.append(jax.ShapeDtypeStruct((1, LANES), F32))

    def body(*refs):
        blk_ref, refs = refs[0], refs[1:]
        in_refs, out_refs, pick_ref = refs[:len(ins)], refs[len(ins):-1], refs[-1]

        def total(ref):
            g = ref[0]
            for d in range(1, N_DEV):
                g = g + ref[d]
            return g

        def update(k, at, g, w_ref, m_ref, v_ref):
            g_ref, d_ref, m2_ref, v2_ref = out_refs[4 * k:4 * k + 4]
            g_ref[at] = g
            d_ref[at], m2_ref[at], v2_ref[at] = _adamw_math(w_ref[at], g, m_ref[at], v_ref[at])

        for k in range(n_l):
            p0, p1, w_ref, m_ref, v_ref = in_refs[5 * k:5 * k + 5]
            for layer, parts in enumerate((p0, p1)):
                update(k, pl.ds(layer, 1), total(parts), w_ref, m_ref, v_ref)
        pf, w_ref, m_ref, v_ref = in_refs[5 * n_l:5 * n_l + 4]
        update(n_l, pl.ds(0, 1), total(pf), w_ref, m_ref, v_ref)
        c0, c1, w_ref, m_ref, v_ref = in_refs[5 * n_l + 4:5 * n_l + 9]
        for layer, parts in enumerate((c0, c1)):
            g8 = total(parts)
            mine = jnp.zeros((SUBLANES, HEAD_DIM), F32)
            for j in range(N_DEV):
                mine = mine + jnp.where(blk_ref[0] == j, g8[:, HEAD_DIM * j:HEAD_DIM * (j + 1)], 0.0)
            pick_ref[...] = mine
            update(n_l + 1, layer, pick_ref[0:3, :], w_ref, m_ref, v_ref)
        out_refs[-1][...] = total(in_refs[-1])

    whole = lambda shape: pl.BlockSpec(shape, lambda: (0,) * len(shape))
    outs = _call(
        body, name="adamw_small",
        out_shape=tuple(out_shape),
        in_specs=[pl.BlockSpec(memory_space=pltpu.SMEM)] + [whole(a.shape) for a in ins],
        out_specs=tuple(whole(o.shape) for o in out_shape),
        scratch_shapes=[pltpu.VMEM((SUBLANES, HEAD_DIM), F32)],
    )(blk, *ins)
    return [outs[4 * k:4 * k + 4] for k in range(n_l + 2)], outs[-1]


def kernel(x, p, norm_g, w_in, conv_w, conv_b, branch_g, w_out, ple_norm_g, w_pg, b_pg, w_pe, final_g, loss_target, m_norm_g, m_w_in, m_conv_w, m_conv_b, m_branch_g, m_w_out, m_ple_norm_g, m_w_pg, m_b_pg, m_w_pe, m_final_g, v_norm_g, v_w_in, v_conv_w, v_conv_b, v_branch_g, v_w_out, v_ple_norm_g, v_w_pg, v_b_pg, v_w_pe, v_final_g):
    s = x.shape[1]
    x0 = x.reshape(s, D_MODEL)
    target = loss_target.reshape(s, D_MODEL)
    me_blk = _my_block()

    win_s, wout_s, wpg_s, wpe_s = _cast_bf16(
        [w_in.reshape(DEPTH * D_MODEL, 512), w_out.reshape(DEPTH * 128, D_MODEL),
         w_pg.reshape(DEPTH * 128, D_MODEL), w_pe.reshape(DEPTH * PLE_DIM, 128)], "cast_weights")
    win_s, wout_s = win_s.reshape(DEPTH, D_MODEL, 512), wout_s.reshape(DEPTH, 128, D_MODEL)
    wpg_s, wpe_s = wpg_s.reshape(DEPTH, 128, D_MODEL), wpe_s.reshape(DEPTH, PLE_DIM, 128)
    cw_s = jnp.zeros((SUBLANES, LANES), F32).at[:DEPTH * 3, :HEAD_DIM].set(conv_w.reshape(DEPTH * 3, HEAD_DIM))
    bf = lambda r_, c_: jax.ShapeDtypeStruct((r_, c_), BF16)
    w_items = lambda l: [(wout_s[l], bf(D_MODEL, D_MODEL), "rows128"), (wpg_s[l], bf(D_MODEL, D_MODEL), "rows128"),
                         (wpe_s[l], bf(PLE_DIM, D_MODEL), "cols128")]
    win_f = [None] * DEPTH
    win_f[0], cw_all = _comm_call(_gather_comm([
        (win_s[0], bf(D_MODEL, N_IN), "cols512"),
        (cw_s, jax.ShapeDtypeStruct((N_DEV, SUBLANES, LANES), F32), "slot")]), "gather_w_in_0")
    cw_full = jnp.transpose(cw_all[:, :DEPTH * 3, :HEAD_DIM].reshape(N_DEV, DEPTH, 3, HEAD_DIM), (1, 2, 0, 3))
    cw_full = cw_full.reshape(DEPTH, 3, D_CONV)
    gather_rest_0 = _gather_comm(w_items(0))
    gather_win_1 = _gather_comm([(win_s[1], bf(D_MODEL, N_IN), "cols512")])
    gather_rest_1 = _gather_comm(w_items(1))

    norm3, convb3, branch3, ple3, bpg3 = [a.reshape(DEPTH, 1, -1) for a in (norm_g, conv_b, branch_g, ple_norm_g, b_pg)]

    saved = []
    xl = x0
    wout_f, wpg_f, wpe_f = [None] * DEPTH, [None] * DEPTH, [None] * DEPTH
    for l in range(DEPTH):
        (h, pc, qkv, az), got = _fwd_in(xl, norm3, l, win_f[l], f"fwd_in_{l}",
                                        comm=gather_rest_0 if l == 0 else None)
        if l == 0:
            wout_f[0], wpg_f[0], wpe_f[0] = got
        (ya, lsum, nblk), got = _attn_fwd(qkv, f"attn_fwd_{l}", comm=gather_win_1 if l == 0 else None)
        if l == 0:
            (win_f[1],) = got
        last = l == DEPTH - 1
        outs, got = _fwd_mid(
            xl, pc, az, ya, p, l, cw_full, convb3, branch3, wout_f[l],
            ple3, wpg_f[l], bpg3, wpe_f[l], f"fwd_mid_{l}",
            comm=gather_rest_1 if l == 0 else None, head=(target, final_g[None, :]) if last else None)
        x2, x3, gated, h2, gate, e = outs[:6]
        if l == 0:
            wout_f[1], wpg_f[1], wpe_f[1] = got
        saved.append(dict(x=xl, h=h, pc=pc, qkv=qkv, az=az, ya=ya, lsum=lsum, nblk=nblk, x2=x2, gated=gated, h2=h2,
                          gate=gate, e=e))
        xl = x3

    dx, (loss_acc, d_final_g) = xl, outs[6:]

    dwin, dwout, dwpg, dwpe = [None] * DEPTH, [None] * DEPTH, [None] * DEPTH, [None] * DEPTH
    small = dict(norm_g=[None] * DEPTH, conv_b=[None] * DEPTH, branch_g=[None] * DEPTH,
                 ple_norm_g=[None] * DEPTH, b_pg=[None] * DEPTH, conv_w=[None] * DEPTH)
    slot = lambda r_, c_: jax.ShapeDtypeStruct((r_, c_), BF16)
    half = D_MODEL // 2
    r_in1, r_out, r_pg, r_pe = None, [None] * DEPTH, [None] * DEPTH, [None] * DEPTH

    def rest_items(l):
        return [(dwout[l], slot(128, D_MODEL), "rows128"), (dwpg[l], slot(128, D_MODEL), "rows128"),
                (dwpe[l], slot(PLE_DIM, 128), "cols128")]

    for l in reversed(range(DEPTH)):
        sv = saved[l]
        ride = _exchange_comm(rest_items(1)) if l == 0 else None
        (dx2, dya, dmisc, dconv, dwout[l], dwpg[l], dwpe[l], d_bpg, d_pg, d_bg, d_cbias, d_cw), got = _bwd_mid(
            dx, sv["x2"], sv["gate"], sv["e"], sv["pc"], sv["az"], sv["ya"], sv["gated"], sv["h2"], p, l,
            cw_full, convb3, branch3, ple3, wpg_f[l], wout_f[l], f"bwd_mid_{l}",
            comm=ride)
        if l == 0:
            r_out[1], r_pg[1], r_pe[1] = got
        ride = _exchange_comm([(dwin[1], slot(D_MODEL, 512), "cols512")] + rest_items(0)) if l == 0 else None
        (dq, dk, dv), got = _attn_bwd(sv["qkv"], sv["lsum"], sv["nblk"], dya, f"attn_bwd_{l}", comm=ride)
        if l == 0:
            r_in1, r_out[0], r_pg[0], r_pe[0] = got
        dproj_args = (dmisc, dconv, sv["pc"], dq, dk, dv, sv["x"], dx2, norm3, cw_full, l, win_f[l])
        if l == 1:
            (dx, d_ng, dwin[1]), _ = _bwd_dproj(*dproj_args, "bwd_dproj_dw_1", h=sv["h"])
        else:
            (dproj, dx, d_ng, dwin_top), _ = _bwd_dproj(*dproj_args, "bwd_dproj_dw_0", h=sv["h"], h_rows=(half, 0))
            dwin_bot, (r_in0_top,) = _atb(sv["h"], dproj, "dw_in_0_bottom", a_cols=(half, 1),
                                          comm=_exchange_comm([(dwin_top, slot(half, 512), "cols512")]))
        small["norm_g"][l], small["conv_b"][l], small["branch_g"][l] = d_ng, d_cbias, d_bg
        small["ple_norm_g"][l], small["b_pg"][l], small["conv_w"][l] = d_pg, d_bpg, d_cw
    grad_x = dx.reshape(1, s, D_MODEL)

    names = ["norm_g", "conv_b", "branch_g", "ple_norm_g", "b_pg", "conv_w"]
    small_list = [small[n][l] for n in names for l in range(DEPTH)] + [d_final_g, loss_acc]
    got = _comm_call(_exchange_comm(
        [(dwin_bot, slot(half, 512), "cols512")]
        + [(a, jax.ShapeDtypeStruct(a.shape, F32), "slot") for a in small_list]), "exchange_last")
    r_in0_bot, r_small = got[0], got[1:]

    per_layer = lambda r: [(r[0], 0, 0), (r[1], 1, 0)]
    g_win, d_win, m_win, v_win = _adamw_sum8([(r_in0_top, 0, 0), (r_in0_bot, 0, half), (r_in1, 1, 0)],
                                             w_in, m_w_in, v_w_in, "adamw_w_in")
    g_wout, d_wout, m_wout, v_wout = _adamw_sum8(per_layer(r_out), w_out, m_w_out, v_w_out, "adamw_w_out")
    g_wpg, d_wpg, m_wpg, v_wpg = _adamw_sum8(per_layer(r_pg), w_pg, m_w_pg, v_w_pg, "adamw_w_pg")
    g_wpe, d_wpe, m_wpe, v_wpe = _adamw_sum8(per_layer(r_pe), w_pe, m_w_pe, v_w_pe, "adamw_w_pe")

    layered = [(norm_g, m_norm_g, v_norm_g), (conv_b, m_conv_b, v_conv_b), (branch_g, m_branch_g, v_branch_g),
               (ple_norm_g, m_ple_norm_g, v_ple_norm_g), (b_pg, m_b_pg, v_b_pg)]
    row = lambda a: a.reshape(1, -1)
    upd, loss_row = _small_update(
        jnp.reshape(me_blk, (1,)).astype(jnp.int32),
        [(r_small[2 * k], r_small[2 * k + 1], *wmv) for k, wmv in enumerate(layered)],
        (r_small[12], row(final_g), row(m_final_g), row(v_final_g)),
        (r_small[10], r_small[11], conv_w, m_conv_w, v_conv_w), r_small[13])
    loss = loss_row[0, 0]
    upd[5] = [a.reshape(-1) for a in upd[5]]

    big = {1: (g_win, d_win, m_win, v_win), 5: (g_wout, d_wout, m_wout, v_wout), 7: (g_wpg, d_wpg, m_wpg, v_wpg),
           9: (g_wpe, d_wpe, m_wpe, v_wpe)}
    small_at = {0: 0, 2: 6, 3: 1, 4: 2, 6: 3, 8: 4, 10: 5}
    per_kind = [[(big[i] if i in big else upd[small_at[i]])[j] for i in range(11)] for j in range(4)]
    return (loss, grad_x, *per_kind[0], *per_kind[1], *per_kind[2], *per_kind[3])
```

```python
import jax
import jax.numpy as jnp
from jax import lax
from jax.experimental import pallas as pl
from jax.experimental.pallas import tpu as pltpu

F32 = jnp.float32
BF16 = jnp.bfloat16

D_MODEL = 1024
D_CONV = 512
D_SB = 512
N_IN = 4096
HEAD_DIM = 64
PLE_DIM = 256
DEPTH = 2
EPS = 1e-6
ADAM_LR = 0.001
ADAM_B1 = 0.9
ADAM_B2 = 0.999
ADAM_EPS = 1e-08
ADAM_WD = 0.01
ADAM_STEP = 10

LANES = 128
SUBLANES = 8
VMEM_BYTES_V7X = 64 * 1024 * 1024
VMEM_LIMIT = VMEM_BYTES_V7X - 8 * 1024 * 1024

N_DEV = 8
ROW_TILE = 256
FWD_ROW_TILE = 512
ATTN_TILE = 256

NT = (((1,), (1,)), ((), ()))
TN = (((0,), (0,)), ((), ()))


def _call(body, **kw):
    return pl.pallas_call(body, **kw)


def _params(sem=None, vmem=None):
    return pltpu.CompilerParams(dimension_semantics=sem, vmem_limit_bytes=vmem)


def _sigmoid(z):
    return 0.5 * jnp.tanh(0.5 * z) + 0.5


def _group_bcast_sum(a, lo):
    s_lo = jnp.sum(jnp.where(lo, a, 0.0), axis=-1, keepdims=True)
    s_hi = jnp.sum(jnp.where(lo, 0.0, a), axis=-1, keepdims=True)
    return jnp.where(lo, s_lo, s_hi)


def _layer_rows(layer, rows, width):
    return pl.BlockSpec((None, rows, width), lambda i: (layer, 0, 0))


def _my_block():
    return 4 * lax.axis_index("x") + 2 * lax.axis_index("y") + lax.axis_index("c")


def _cast_bf16(arrays, name):
    n = len(arrays)

    def body(*refs):
        for a_ref, o_ref in zip(refs[:n], refs[n:]):
            o_ref[...] = a_ref[...].astype(BF16)

    whole = lambda a: pl.BlockSpec(a.shape, lambda: (0, 0))
    return _call(
        body, name=name,
        out_shape=tuple(jax.ShapeDtypeStruct(a.shape, BF16) for a in arrays),
        in_specs=[whole(a) for a in arrays], out_specs=tuple(whole(a) for a in arrays),
        compiler_params=_params(None, VMEM_LIMIT),
    )(*arrays)


class _Comm:
    def __init__(self, inputs, out_shapes, scratch, begin, middle, finish):
        self.inputs, self.out_shapes, self.scratch = list(inputs), list(out_shapes), list(scratch)
        self.begin, self.middle, self.finish = begin, middle, finish


def _slab(kind, ref, blk):
    if kind == "cols512":
        return ref.at[:, pl.ds(blk * 512, 512)]
    if kind == "rows128":
        return ref.at[pl.ds(blk * 128, 128), :]
    if kind == "cols128":
        return ref.at[:, pl.ds(blk * 128, 128)]
    return ref.at[blk]


def _gather_comm(items):
    n_t = len(items)
    kinds = [it[2] for it in items]

    def ctx(ins, outs, sems):
        send_sems, recv_sems, local_sems = sems
        x, y, c = lax.axis_index("x"), lax.axis_index("y"), lax.axis_index("c")
        me, sibling = (x, y, c), (x, y, 1 - c)
        chips = [(1 - x, y), (x, 1 - y), (1 - x, 1 - y)]

        def place(t, dev):
            return _slab(kinds[t], outs[t], 4 * dev[0] + 2 * dev[1] + dev[2])

        def copy(t, k, block, to, own=False):
            return pltpu.make_async_remote_copy(
                src_ref=ins[t] if own else place(t, block), dst_ref=place(t, block),
                send_sem=send_sems.at[t, k], recv_sem=recv_sems.at[t, k],
                device_id=to, device_id_type=pl.DeviceIdType.MESH)

        mine = [pltpu.make_async_copy(ins[t], place(t, me), local_sems.at[t]) for t in range(n_t)]
        first = []
        for t in range(n_t):
            first.append(copy(t, 0, me, sibling, own=True))
            first += [copy(t, 1 + j, me, (*chip, c), own=True) for j, chip in enumerate(chips)]
        passed = [copy(t, 4 + j, (*chip, c), sibling) for j, chip in enumerate(chips) for t in range(n_t)]
        landed = [copy(t, 1 + j, (*chip, c), me) for j, chip in enumerate(chips) for t in range(n_t)]
        from_sibling = []
        for t in range(n_t):
            from_sibling.append(copy(t, 0, sibling, me))
            from_sibling += [copy(t, 4 + j, (*chip, 1 - c), me) for j, chip in enumerate(chips)]
        return mine, first, landed, passed, from_sibling

    def begin(ins, outs, sems):
        mine, first, _, _, _ = ctx(ins, outs, sems)
        for cp in mine + first:
            cp.start()

    def middle(ins, outs, sems):
        _, _, landed, passed, _ = ctx(ins, outs, sems)
        for got, fwd in zip(landed, passed):
            got.wait_recv()
            fwd.start()

    def finish(ins, outs, sems):
        mine, first, _, passed, from_sibling = ctx(ins, outs, sems)
        for cp in from_sibling:
            cp.wait_recv()
        for cp in first + passed:
            cp.wait_send()
        for cp in mine:
            cp.wait()

    scratch = [pltpu.SemaphoreType.DMA((n_t, 7)), pltpu.SemaphoreType.DMA((n_t, 7)), pltpu.SemaphoreType.DMA((n_t,))]
    return _Comm([it[0] for it in items], [it[1] for it in items], scratch, begin, middle, finish)


def _exchange_comm(items):
    n_t = len(items)
    kinds = [it[2] for it in items]

    def ctx(ins, outs, sems):
        send_sems, recv_sems, local_sems = sems
        x, y, c = lax.axis_index("x"), lax.axis_index("y"), lax.axis_index("c")
        me_blk = 4 * x + 2 * y + c

        def src(t, blk):
            return ins[t] if kinds[t] == "slot" else _slab(kinds[t], ins[t], blk)

        local = [pltpu.make_async_copy(src(t, me_blk), outs[t].at[me_blk], local_sems.at[t]) for t in range(n_t)]
        remote = []
        for k in range(1, N_DEV):
            px = 1 - x if k & 4 else x
            py = 1 - y if k & 2 else y
            pc_ = 1 - c if k & 1 else c
            for t in range(n_t):
                remote.append(pltpu.make_async_remote_copy(
                    src_ref=src(t, 4 * px + 2 * py + pc_), dst_ref=outs[t].at[me_blk],
                    send_sem=send_sems.at[k - 1, t], recv_sem=recv_sems.at[k - 1, t],
                    device_id=(px, py, pc_), device_id_type=pl.DeviceIdType.MESH))
        return local, remote

    def begin(ins, outs, sems):
        local, remote = ctx(ins, outs, sems)
        for cp in local + remote:
            cp.start()

    def finish(ins, outs, sems):
        local, remote = ctx(ins, outs, sems)
        for cp in remote:
            cp.wait_recv()
        for cp in remote:
            cp.wait_send()
        for cp in local:
            cp.wait()

    scratch = [pltpu.SemaphoreType.DMA((N_DEV - 1, n_t)), pltpu.SemaphoreType.DMA((N_DEV - 1, n_t)),
               pltpu.SemaphoreType.DMA((n_t,))]
    out_shapes = [jax.ShapeDtypeStruct((N_DEV, *it[1].shape), it[1].dtype) for it in items]
    return _Comm([it[0] for it in items], out_shapes, scratch, begin, None, finish)


def _comm_call(comm, name):
    n_in, n_out = len(comm.inputs), len(comm.out_shapes)

    def body(*refs):
        ins, outs, sems = refs[:n_in], refs[n_in:n_in + n_out], refs[n_in + n_out:]
        comm.begin(ins, outs, sems)
        if comm.middle is not None:
            comm.middle(ins, outs, sems)
        comm.finish(ins, outs, sems)

    any_spec = pl.BlockSpec(memory_space=pl.ANY)
    return _call(body, name=name, out_shape=tuple(comm.out_shapes), in_specs=[any_spec] * n_in,
                 out_specs=[any_spec] * n_out, scratch_shapes=comm.scratch)(*comm.inputs)


def _hosted(body, n_in, n_out, comm, first, last, middle):
    if comm is None:
        return lambda *refs: body(*refs)
    n_ci, n_co, n_cs = len(comm.inputs), len(comm.out_shapes), len(comm.scratch)

    def wrapped(*refs):
        ins, cin = refs[:n_in], refs[n_in:n_in + n_ci]
        o0 = n_in + n_ci
        outs, cout = refs[o0:o0 + n_out], refs[o0 + n_out:o0 + n_out + n_co]
        scr, csem = refs[o0 + n_out + n_co:len(refs) - n_cs], refs[len(refs) - n_cs:]
        pl.when(first())(lambda: comm.begin(cin, cout, csem))
        body(*ins, *outs, *scr)
        if comm.middle is not None:
            pl.when(middle())(lambda: comm.middle(cin, cout, csem))
        pl.when(last())(lambda: comm.finish(cin, cout, csem))

    return wrapped


def _hosted_call(body, comm, *, name, grid, out_shape, in_specs, out_specs, args, scratch_shapes=(), sem=None):
    nd = len(grid)
    first, last, middle = _at_first(nd), _at_last(nd), _at_middle(nd)
    if comm is not None:
        sem = ("arbitrary",) * nd
    n_in, n_out = len(in_specs), len(out_shape)
    any_spec = pl.BlockSpec(memory_space=pl.ANY)
    c_in = [] if comm is None else comm.inputs
    c_out = [] if comm is None else comm.out_shapes
    c_scr = [] if comm is None else comm.scratch
    outs = _call(
        _hosted(body, n_in, n_out, comm, first, last, middle), name=name, grid=grid,
        out_shape=(*out_shape, *c_out),
        in_specs=[*in_specs, *[any_spec] * len(c_in)],
        out_specs=(*out_specs, *[any_spec] * len(c_out)),
        scratch_shapes=[*scratch_shapes, *c_scr],
        compiler_params=_params(sem, VMEM_LIMIT),
    )(*args, *c_in)
    return outs[:n_out], outs[n_out:]


def _grid_step(ndim):
    i, n = pl.program_id(0), pl.num_programs(0)
    for d in range(1, ndim):
        i, n = i * pl.num_programs(d) + pl.program_id(d), n * pl.num_programs(d)
    return i, n


def _at_first(ndim):
    return lambda: _grid_step(ndim)[0] == 0


def _at_last(ndim):
    def pred():
        i, n = _grid_step(ndim)
        return i == n - 1
    return pred


def _at_middle(ndim):
    def pred():
        i, n = _grid_step(ndim)
        return i == (3 * n) // 4
    return pred


def _fwd_in(x, g, layer, w_full, name, comm=None):
    s = x.shape[0]
    ts = min(FWD_ROW_TILE, s)

    def body(x_ref, g_ref, w_ref, h_ref, pc_ref, qkv_ref, az_ref):
        xf = x_ref[...]
        r = lax.rsqrt(jnp.mean(xf * xf, axis=-1, keepdims=True) + EPS)
        h = (xf * r * g_ref[...]).astype(BF16)
        h_ref[...] = h
        pc_ref[...] = jnp.dot(h, w_ref[:, 0:2048], preferred_element_type=F32).astype(BF16)
        q = jnp.dot(h, w_ref[:, 2048:2560], preferred_element_type=F32)
        qkv_ref[:, 0:512] = (q * 0.125).astype(BF16)
        qkv_ref[:, 512:1536] = jnp.dot(h, w_ref[:, 2560:3584], preferred_element_type=F32).astype(BF16)
        az_ref[...] = jnp.dot(h, w_ref[:, 3584:4096], preferred_element_type=F32).astype(BF16)

    row = lambda width: pl.BlockSpec((ts, width), lambda i: (i, 0))
    return _hosted_call(
        body, comm, name=name, grid=(s // ts,),
        out_shape=(jax.ShapeDtypeStruct((s, D_MODEL), BF16), jax.ShapeDtypeStruct((s, 2048), BF16),
                   jax.ShapeDtypeStruct((s, 1536), BF16), jax.ShapeDtypeStruct((s, 512), BF16)),
        in_specs=[row(D_MODEL), _layer_rows(layer, 1, D_MODEL),
                  pl.BlockSpec((D_MODEL, N_IN), lambda i: (0, 0))],
        out_specs=(row(D_MODEL), row(2048), row(1536), row(512)),
        args=(x, g, w_full), sem=("parallel",))


ATTN_ROWS = 128
ATTN_DONE = 104.0


def _attn_pieces(tq, rc):
    lane = lax.broadcasted_iota(jnp.int32, (1, LANES), 1)
    lo = lane < HEAD_DIM
    row = lax.broadcasted_iota(jnp.int32, (tq, tq), 0)
    col = lax.broadcasted_iota(jnp.int32, (tq, tq), 1)
    tri_gt = jnp.where(row > col, 1.0, 0.0).astype(BF16)
    tri_le = jnp.where(row <= col, 1.0, 0.0).astype(BF16)
    rrow = lax.broadcasted_iota(jnp.int32, (rc, tq), 0)
    rcol = lax.broadcasted_iota(jnp.int32, (rc, tq), 1)
    causal = [rcol < rrow + r * rc for r in range(tq // rc)]
    return lo, causal, tri_gt, tri_le


def _split_heads(a, lo):
    z = jnp.zeros_like(a)
    return (jnp.where(lo, a, z), jnp.where(lo, z, a))


def _softplus(z, causal, diag):
    neg_abs = lax.bitcast_convert_type(lax.bitcast_convert_type(z, jnp.uint32) | jnp.uint32(0x80000000), F32)
    sp = jnp.maximum(z, 0.0) + jnp.log(1.0 + jnp.exp(neg_abs))
    if diag:
        sp = jnp.where(causal, sp, 0.0)
    return sp


def _attn_fwd(qkv, name, comm=None):
    s = qkv.shape[0]
    tq = min(ATTN_TILE, s)
    nq = s // tq
    rc = min(ATTN_ROWS, tq)
    n_rc = tq // rc
    nt = 4 if nq % 4 == 0 else 2 if nq % 2 == 0 else 1
    chains = [(t, r, hh) for t in range(nt) for r in range(n_rc) for hh in range(2)]

    def body(q_ref, k_ref, v_ref, o_ref, lsum_ref, nblk_ref):
        hp = pl.program_id(0)
        qis = [pl.program_id(1) * nt + t for t in range(nt)]
        lo, causal, tri_gt, _ = _attn_pieces(tq, rc)
        qh = _split_heads(q_ref[...], lo)
        qc = {(t, r, hh): qh[hh][t * tq + r * rc:t * tq + (r + 1) * rc] for t, r, hh in chains}

        mm = lambda a_, b_: jnp.dot(a_.astype(BF16), b_, preferred_element_type=F32)
        rowsum = lambda a_: jnp.sum(a_, axis=-1, keepdims=True)

        def block(kb, carry, t):
            mine = [ch for ch in chains if ch[0] == t]
            start = pl.multiple_of(kb * tq, tq)
            k = k_ref[pl.ds(start, tq), :]
            vh = _split_heads(v_ref[pl.ds(start, tq), :], lo)
            z = {ch: lax.dot_general(qc[ch], k, NT, preferred_element_type=F32) for ch in mine}
            sp = {ch: _softplus(z[ch], None, False) for ch in mine}
            later = {ch: mm(sp[ch], tri_gt) for ch in mine}
            a = {ch: jnp.exp((z[ch] - sp[ch]) - (carry[ch[1]][1 + ch[2]] + later[ch])) for ch in mine}
            pv = {ch: mm(a[ch], vh[ch[2]]) for ch in mine}
            return tuple((carry[r][0] + pv[(t, r, 0)] + pv[(t, r, 1)], carry[r][1] + rowsum(sp[(t, r, 0)]),
                          carry[r][2] + rowsum(sp[(t, r, 1)])) for r in range(n_rc))

        def first_two():
            ok = [qi > 0 for qi in qis]
            d0 = [pl.multiple_of(qi * tq, tq) for qi in qis]
            p0 = [pl.multiple_of(jnp.maximum(qi - 1, 0) * tq, tq) for qi in qis]
            k_d = [k_ref[pl.ds(d0[t], tq), :] for t in range(nt)]
            k_p = [k_ref[pl.ds(p0[t], tq), :] for t in range(nt)]
            vh_d = [_split_heads(v_ref[pl.ds(d0[t], tq), :], lo) for t in range(nt)]
            vh_p = [_split_heads(v_ref[pl.ds(p0[t], tq), :], lo) for t in range(nt)]
            z_d = {ch: lax.dot_general(qc[ch], k_d[ch[0]], NT, preferred_element_type=F32) for ch in chains}
            z_p = {ch: lax.dot_general(qc[ch], k_p[ch[0]], NT, preferred_element_type=F32) for ch in chains}
            sp_d = {ch: _softplus(z_d[ch], causal[ch[1]], True) for ch in chains}
            sp_raw = {ch: _softplus(z_p[ch], None, False) for ch in chains}
            sp_p = {ch: jnp.where(ok[ch[0]], sp_raw[ch], 0.0) for ch in chains}
            later_d = {ch: mm(sp_d[ch], tri_gt) for ch in chains}
            later_p = {ch: mm(sp_p[ch], tri_gt) for ch in chains}
            c_d = {ch: rowsum(sp_d[ch]) for ch in chains}
            a_d = {ch: jnp.where(causal[ch[1]], jnp.exp((z_d[ch] - sp_d[ch]) - later_d[ch]), 0.0) for ch in chains}
            a_p = {ch: jnp.where(ok[ch[0]], jnp.exp((z_p[ch] - sp_raw[ch]) - (c_d[ch] + later_p[ch])), 0.0)
                   for ch in chains}
            pv = {ch: mm(a_d[ch], vh_d[ch[0]][ch[2]]) + mm(a_p[ch], vh_p[ch[0]][ch[2]]) for ch in chains}
            return [tuple((pv[(t, r, 0)] + pv[(t, r, 1)], c_d[(t, r, 0)] + rowsum(sp_p[(t, r, 0)]),
                           c_d[(t, r, 1)] + rowsum(sp_p[(t, r, 1)])) for r in range(n_rc)) for t in range(nt)]

        def least(carry):
            m = jnp.minimum(carry[0][1], carry[0][2])
            for r in range(1, n_rc):
                m = jnp.minimum(m, jnp.minimum(carry[r][1], carry[r][2]))
            return jnp.min(m)

        carries = first_two()
        for t, qi in enumerate(qis):
            def go_on(st, qi=qi):
                return jnp.logical_and(st[0] < qi - 1, st[1] < ATTN_DONE)

            def step(st, qi=qi, t=t):
                new = block(qi - 2 - st[0], st[2], t)
                return st[0] + 1, least(new), new

            walked, _, carry = lax.while_loop(go_on, step, (jnp.int32(0), least(carries[t]), carries[t]))
            for r in range(n_rc):
                rows = slice(t * tq + r * rc, t * tq + (r + 1) * rc)
                o_ref[rows, :] = carry[r][0].astype(BF16)
                lsum_ref[rows, :] = jnp.where(lo, carry[r][1], carry[r][2])
            nblk_ref[hp, qi] = walked.astype(F32)

    blk = pl.BlockSpec((nt * tq, LANES), lambda hp, qg: (qg, hp))
    o512 = jax.ShapeDtypeStruct((s, D_SB), F32)
    return _hosted_call(
        body, comm, name=name, grid=(4, nq // nt),
        out_shape=(jax.ShapeDtypeStruct((s, D_SB), BF16), o512, jax.ShapeDtypeStruct((4, nq), F32)),
        in_specs=[blk, pl.BlockSpec((s, LANES), lambda hp, qg: (0, 4 + hp)),
                  pl.BlockSpec((s, LANES), lambda hp, qg: (0, 8 + hp))],
        out_specs=(blk, blk, pl.BlockSpec(memory_space=pltpu.SMEM)),
        args=(qkv, qkv, qkv), sem=("arbitrary", "arbitrary"))


HALO = 16


def _conv_taps(cc_ref, ch_ref, ccp_ref, chp_ref, halo_ref, first):
    u = cc_ref[...].astype(F32) * ch_ref[...].astype(F32)
    halo_ref[...] = ccp_ref[...].astype(F32) * chp_ref[...].astype(F32) * jnp.where(first, 0.0, 1.0)
    p6 = halo_ref[HALO - 2:HALO - 1, :]
    p7 = halo_ref[HALO - 1:HALO, :]
    rowi = lax.broadcasted_iota(jnp.int32, u.shape, 0)
    u1 = jnp.where(rowi == 0, p7, pltpu.roll(u, 1, 0))
    u2 = jnp.where(rowi == 0, p6, jnp.where(rowi == 1, p7, pltpu.roll(u, 2, 0)))
    return u, u1, u2


def _fwd_mid(x, pc, az, ya, p4, layer, cw, cb, bg, wout_full, pg, wpg_full, bpg, wpe_full, name, comm=None,
             head=None):
    s = x.shape[0]
    ts = min(FWD_ROW_TILE, s)
    blk_h = ts // HALO
    n_in = 18 + (2 if head else 0)

    def body(*refs):
        (x_ref, cb_ref_, cc_ref, ch_ref, cz_ref, ccp_ref, chp_ref, az_ref, ya_ref, p_ref,
         cw_ref, cbias_ref, bg_ref, wout_ref, pg_ref, wpg_ref, bpg_ref, wpe_ref) = refs[:18]
        x2_ref, x3_ref, gated_ref, h2_ref, gate_ref, e_ref = refs[n_in:n_in + 6]
        halo_ref = refs[-1]
        i = pl.program_id(0)
        lane = lax.broadcasted_iota(jnp.int32, (1, LANES), 1)
        lo = lane < HEAD_DIM
        u, u1, u2 = _conv_taps(cc_ref, ch_ref, ccp_ref, chp_ref, halo_ref, i == 0)
        conv = cbias_ref[...] + cw_ref[0:1, :] * u2 + cw_ref[1:2, :] * u1 + cw_ref[2:3, :] * u
        yc = cb_ref_[...].astype(F32) * conv
        for sl in range(8):
            cols = slice(LANES * (sl % 4), LANES * (sl % 4 + 1))
            y = yc[:, cols] if sl < 4 else ya_ref[:, cols].astype(F32)
            zc = (cz_ref[:, cols] if sl < 4 else az_ref[:, cols]).astype(F32)
            rg = lax.rsqrt(_group_bcast_sum(y * y, lo) * (1.0 / HEAD_DIM) + EPS)
            yn = y * rg * bg_ref[:, LANES * sl:LANES * (sl + 1)]
            gated_ref[:, LANES * sl:LANES * (sl + 1)] = (yn * (zc * _sigmoid(zc))).astype(BF16)
        x2 = x_ref[...] + jnp.dot(gated_ref[...], wout_ref[...], preferred_element_type=F32)
        x2_ref[...] = x2
        r2 = lax.rsqrt(jnp.mean(x2 * x2, axis=-1, keepdims=True) + EPS)
        h2 = (x2 * r2 * pg_ref[...]).astype(BF16)
        h2_ref[...] = h2
        gate = _sigmoid(jnp.dot(h2, wpg_ref[...], preferred_element_type=F32) + bpg_ref[...])
        gate_ref[...] = gate.astype(BF16)
        e = jnp.dot(p_ref[...].astype(BF16), wpe_ref[...], preferred_element_type=F32)
        e_ref[...] = e.astype(BF16)
        x3 = x2 + gate * e
        if not head:
            x3_ref[...] = x3
            return
        t_ref, fg_ref = refs[18:20]
        loss_ref, dfg_ref = refs[n_in + 6:n_in + 8]
        dx, loss, dfg = _loss_math(x3, t_ref[...], fg_ref[...])

        @pl.when(i == 0)
        def _():
            loss_ref[...] = jnp.zeros_like(loss_ref)
            dfg_ref[...] = jnp.zeros_like(dfg_ref)

        x3_ref[...] = dx
        loss_ref[...] += loss
        dfg_ref[...] += dfg

    row = lambda width, cb_=0: pl.BlockSpec((ts, width), lambda i: (i, cb_))
    prev = lambda cb_: pl.BlockSpec((HALO, 512), lambda i: (jnp.maximum(i * blk_h - 1, 0), cb_))
    vec = lambda width: pl.BlockSpec((1, width), lambda i: (0, 0))
    lvec = lambda width: _layer_rows(layer, 1, width)
    wspec = lambda r_, c_: pl.BlockSpec((r_, c_), lambda i: (0, 0))
    f32o = jax.ShapeDtypeStruct((s, D_MODEL), F32)
    bfo = jax.ShapeDtypeStruct((s, D_MODEL), BF16)
    head_in = [row(D_MODEL), vec(D_MODEL)] if head else []
    head_out = [jax.ShapeDtypeStruct((1, LANES), F32), jax.ShapeDtypeStruct((1, D_MODEL), F32)] if head else []
    return _hosted_call(
        body, comm, name=name, grid=(s // ts,),
        out_shape=(f32o, f32o, bfo, bfo, bfo, bfo, *head_out),
        scratch_shapes=[pltpu.VMEM((HALO, 512), F32)],
        in_specs=[row(D_MODEL), row(512, 0), row(512, 1), row(512, 2), row(512, 3), prev(1), prev(2),
                  row(512), row(512),
                  pl.BlockSpec((None, None, ts, PLE_DIM), lambda i: (layer, 0, i, 0)),
                  _layer_rows(layer, 3, 512), lvec(512), lvec(D_MODEL),
                  wspec(D_MODEL, D_MODEL), lvec(D_MODEL), wspec(D_MODEL, D_MODEL), lvec(D_MODEL),
                  wspec(PLE_DIM, D_MODEL), *head_in],
        out_specs=(*[row(D_MODEL)] * 6, *([vec(LANES), vec(D_MODEL)] if head else [])),
        args=(x, pc, pc, pc, pc, pc, pc, az, ya, p4, cw, cb, bg, wout_full, pg, wpg_full, bpg, wpe_full,
              *(head or ())),
        sem=("arbitrary",) if head else ("parallel",))


def _loss_math(x, target, g):
    r = lax.rsqrt(jnp.mean(x * x, axis=-1, keepdims=True) + EPS)
    xn = x * r
    err = xn * g - target
    per_row = jnp.sum(err * err, axis=-1, keepdims=True)
    loss = jnp.sum(per_row, axis=0, keepdims=True) * (0.5 / D_MODEL)
    dy = err * (1.0 / D_MODEL)
    dg = jnp.sum(dy * xn, axis=0, keepdims=True)
    dxn = dy * g
    return r * (dxn - xn * jnp.mean(dxn * xn, axis=-1, keepdims=True)), loss, dg


def _bwd_mid(dx3, x2, gate, e, pc, az, ya, gated, h2, p4, layer, cw, cb, bg, pg, wpg_full, wout_full, name,
             comm=None):
    s = x2.shape[0]
    ts = min(ROW_TILE, s)
    blk_h = ts // HALO

    def body(dx3_ref, x2_ref, gate_ref, e_ref, cb_ref_, cc_ref, ch_ref, cz_ref, ccp_ref, chp_ref, az_ref, ya_ref,
             gated_ref, h2_ref, p_ref, cw_ref, cbias_ref, bg_ref, pg_ref, wpg_ref, wout_ref,
             dx2_ref, dya_ref, dmisc_ref, dconv_ref, dwout_ref, dwpg_ref, dwpe_ref,
             dbpg_ref, dpg_ref, dbg_ref, dcbias_ref, dcw_ref,
             dgated_ref, halo_ref, acc_out, acc_pg, acc_pe):
        i = pl.program_id(0)

        @pl.when(i == 0)
        def _():
            for ref in (dbpg_ref, dpg_ref, dbg_ref, dcbias_ref, dcw_ref, acc_out, acc_pg, acc_pe):
                ref[...] = jnp.zeros_like(ref)

        lane = lax.broadcasted_iota(jnp.int32, (1, LANES), 1)
        lo = lane < HEAD_DIM
        dx3 = dx3_ref[...]
        gate = gate_ref[...].astype(F32)
        de_b = (dx3 * gate).astype(BF16)
        dgpre = dx3 * e_ref[...].astype(F32) * gate * (1.0 - gate)
        dbpg_ref[...] += jnp.sum(dgpre, axis=0, keepdims=True)
        dgpre_b = dgpre.astype(BF16)
        dh2 = lax.dot_general(dgpre_b, wpg_ref[...], NT, preferred_element_type=F32)
        acc_pe[...] += lax.dot_general(p_ref[...].astype(BF16), de_b, TN, preferred_element_type=F32)
        acc_pg[...] += lax.dot_general(h2_ref[...], dgpre_b, TN, preferred_element_type=F32)

        u, u1, u2 = _conv_taps(cc_ref, ch_ref, ccp_ref, chp_ref, halo_ref, i == 0)
        conv = cbias_ref[...] + cw_ref[0:1, :] * u2 + cw_ref[1:2, :] * u1 + cw_ref[2:3, :] * u
        c_b = cb_ref_[...].astype(F32)
        yc = c_b * conv
        fwd = []
        for sl in range(8):
            cols = slice(LANES * (sl % 4), LANES * (sl % 4 + 1))
            y = yc[:, cols] if sl < 4 else ya_ref[:, cols].astype(F32)
            zc = (cz_ref[:, cols] if sl < 4 else az_ref[:, cols]).astype(F32)
            rg = lax.rsqrt(_group_bcast_sum(y * y, lo) * (1.0 / HEAD_DIM) + EPS)
            sig = _sigmoid(zc)
            fwd.append((rg, y * rg, zc * sig, sig * (1.0 + zc * (1.0 - sig))))

        x2 = x2_ref[...]
        r2 = lax.rsqrt(jnp.mean(x2 * x2, axis=-1, keepdims=True) + EPS)
        xn2 = x2 * r2
        dpg_ref[...] += jnp.sum(dh2 * xn2, axis=0, keepdims=True)
        dxn = dh2 * pg_ref[...]
        dx2 = dx3 + r2 * (dxn - xn2 * jnp.mean(dxn * xn2, axis=-1, keepdims=True))
        dx2_ref[...] = dx2
        dx2_b = dx2.astype(BF16)
        dgated_ref[...] = lax.dot_general(dx2_b, wout_ref[...], NT, preferred_element_type=F32)
        acc_out[...] += lax.dot_general(gated_ref[...], dx2_b, TN, preferred_element_type=F32)

        for sl in range(8):
            cols = slice(LANES * (sl % 4), LANES * (sl % 4 + 1))
            wide = slice(LANES * sl, LANES * (sl + 1))
            rg, yhat, silu, dsilu = fwd[sl]
            bgs = bg_ref[:, wide]
            dgt = dgated_ref[:, wide]
            dyn = dgt * silu
            dzc = dgt * (yhat * bgs) * dsilu
            dbg_ref[:, wide] += jnp.sum(dyn * yhat, axis=0, keepdims=True)
            dyh = dyn * bgs
            dy = rg * (dyh - yhat * (_group_bcast_sum(dyh * yhat, lo) * (1.0 / HEAD_DIM)))
            if sl < 4:
                dconv = dy * c_b[:, cols]
                dmisc_ref[:, cols] = (dy * conv[:, cols]).astype(BF16)
                dmisc_ref[:, 512 + LANES * sl:512 + LANES * (sl + 1)] = dzc.astype(BF16)
                dconv_ref[:, cols] = dconv
                dcbias_ref[:, cols] += jnp.sum(dconv, axis=0, keepdims=True)
                dcw_ref[0:1, cols] += jnp.sum(dconv * u2[:, cols], axis=0, keepdims=True)
                dcw_ref[1:2, cols] += jnp.sum(dconv * u1[:, cols], axis=0, keepdims=True)
                dcw_ref[2:3, cols] += jnp.sum(dconv * u[:, cols], axis=0, keepdims=True)
            else:
                dya_ref[:, cols] = dy.astype(BF16)
                dmisc_ref[:, 1024 + LANES * (sl - 4):1024 + LANES * (sl - 3)] = dzc.astype(BF16)

        @pl.when(i == pl.num_programs(0) - 1)
        def _():
            dwout_ref[...] = acc_out[...].astype(BF16)
            dwpg_ref[...] = acc_pg[...].astype(BF16)
            dwpe_ref[...] = acc_pe[...].astype(BF16)

    row = lambda width, cb_=0: pl.BlockSpec((ts, width), lambda i: (i, cb_))
    prev = lambda cb_: pl.BlockSpec((HALO, 512), lambda i: (jnp.maximum(i * blk_h - 1, 0), cb_))
    vec = lambda width: pl.BlockSpec((1, width), lambda i: (0, 0))
    lvec = lambda width: _layer_rows(layer, 1, width)
    wspec = lambda r_, c_: pl.BlockSpec((r_, c_), lambda i: (0, 0))
    vo = lambda width: jax.ShapeDtypeStruct((1, width), F32)
    sq = jax.ShapeDtypeStruct((D_MODEL, D_MODEL), BF16)
    return _hosted_call(
        body, comm, name=name, grid=(s // ts,), sem=("arbitrary",),
        args=(dx3, x2, gate, e, pc, pc, pc, pc, pc, pc, az, ya, gated, h2, p4, cw, cb, bg, pg, wpg_full, wout_full),
        out_shape=(jax.ShapeDtypeStruct((s, D_MODEL), F32), jax.ShapeDtypeStruct((s, 512), BF16),
                   jax.ShapeDtypeStruct((s, 1536), BF16), jax.ShapeDtypeStruct((s, 512), F32),
                   sq, sq, jax.ShapeDtypeStruct((PLE_DIM, D_MODEL), BF16),
                   vo(D_MODEL), vo(D_MODEL), vo(D_MODEL), vo(512), jax.ShapeDtypeStruct((SUBLANES, 512), F32)),
        in_specs=[row(D_MODEL), row(D_MODEL), row(D_MODEL), row(D_MODEL),
                  row(512, 0), row(512, 1), row(512, 2), row(512, 3), prev(1), prev(2), row(512), row(512),
                  row(D_MODEL), row(D_MODEL),
                  pl.BlockSpec((None, None, ts, PLE_DIM), lambda i: (layer, 0, i, 0)),
                  _layer_rows(layer, 3, 512), lvec(512), lvec(D_MODEL), lvec(D_MODEL),
                  wspec(D_MODEL, D_MODEL), wspec(D_MODEL, D_MODEL)],
        out_specs=(row(D_MODEL), row(512), row(1536), row(512),
                   wspec(D_MODEL, D_MODEL), wspec(D_MODEL, D_MODEL), wspec(PLE_DIM, D_MODEL),
                   vec(D_MODEL), vec(D_MODEL), vec(D_MODEL), vec(512),
                   pl.BlockSpec((SUBLANES, 512), lambda i: (0, 0))),
        scratch_shapes=[pltpu.VMEM((ts, D_MODEL), F32), pltpu.VMEM((HALO, 512), F32),
                        pltpu.VMEM((D_MODEL, D_MODEL), F32), pltpu.VMEM((D_MODEL, D_MODEL), F32),
                        pltpu.VMEM((PLE_DIM, D_MODEL), F32)])


def _attn_bwd(qkv, lsum, nblk, dya, name, comm=None):
    s = qkv.shape[0]
    tq = min(ATTN_TILE, s)
    nq = s // tq
    rc = min(ATTN_ROWS, tq)
    n_rc = tq // rc
    chains = [(r, hh) for r in range(n_rc) for hh in range(2)]

    def body(nblk_ref, q_ref, k_ref, v_ref, lsum_ref, do_ref, dq_ref, dk_ref, dv_ref, dk_acc, dv_acc):
        hp, qi = pl.program_id(0), pl.program_id(1)

        @pl.when(qi == 0)
        def _():
            dk_acc[...] = jnp.zeros_like(dk_acc)
            dv_acc[...] = jnp.zeros_like(dv_acc)

        lo, causal, tri_gt, tri_le = _attn_pieces(tq, rc)
        lane = lax.broadcasted_iota(jnp.int32, (1, LANES), 1)
        qh = _split_heads(q_ref[...], lo)
        doh = _split_heads(do_ref[...].astype(BF16), lo)
        lt = lsum_ref[...]
        ltot_h = (jnp.sum(jnp.where(lane == 0, lt, 0.0), axis=-1, keepdims=True),
                  jnp.sum(jnp.where(lane == HEAD_DIM, lt, 0.0), axis=-1, keepdims=True))
        rows = lambda a_, r: a_[r * rc:(r + 1) * rc]
        qc = {(r, hh): rows(qh[hh], r) for r, hh in chains}
        doc = {(r, hh): rows(doh[hh], r) for r, hh in chains}
        ltot = {(r, hh): rows(ltot_h[hh], r) for r, hh in chains}

        mm = lambda a_, b_: jnp.dot(a_.astype(BF16), b_, preferred_element_type=F32)
        mm_nt = lambda a_, b_: lax.dot_general(a_, b_, NT, preferred_element_type=F32)
        mm_tn = lambda a_, b_: lax.dot_general(a_.astype(BF16), b_, TN, preferred_element_type=F32)
        rowsum = lambda a_: jnp.sum(a_, axis=-1, keepdims=True)

        def block(kb, carry, diag=False):
            start = pl.multiple_of(kb * tq, tq)
            k = k_ref[pl.ds(start, tq), :]
            v = v_ref[pl.ds(start, tq), :]
            kh = _split_heads(k, lo)
            keep = (lambda ch, a_: jnp.where(causal[ch[0]], a_, 0.0)) if diag else (lambda ch, a_: a_)
            z = {ch: mm_nt(qc[ch], k) for ch in chains}
            da = {ch: mm_nt(doc[ch], v) for ch in chains}
            sp = {ch: _softplus(z[ch], causal[ch[0]], diag) for ch in chains}
            later = {ch: mm(sp[ch], tri_gt) for ch in chains}
            walked = {ch: carry[ch[0]][1 + ch[1]] + rowsum(sp[ch]) for ch in chains}
            a = {ch: keep(ch, jnp.exp((z[ch] - sp[ch]) - ((ltot[ch] - walked[ch]) + later[ch]))) for ch in chains}
            g = {ch: a[ch] * da[ch] for ch in chains}
            upto = {ch: mm(g[ch], tri_le) for ch in chains}
            dz = {ch: keep(ch, g[ch] - jnp.exp(z[ch] - sp[ch]) * (carry[ch[0]][3 + ch[1]] + upto[ch])).astype(BF16)
                  for ch in chains}
            dqc = {ch: mm(dz[ch], kh[ch[1]]) for ch in chains}
            dkc = [mm_tn(dz[ch], qc[ch]) for ch in chains]
            dvc = [mm_tn(a[ch], doc[ch]) for ch in chains]
            dk_acc[pl.ds(start, tq), :] += sum(dkc[1:], dkc[0])
            dv_acc[pl.ds(start, tq), :] += sum(dvc[1:], dvc[0])
            return tuple((carry[r][0] + dqc[(r, 0)] + dqc[(r, 1)], walked[(r, 0)], walked[(r, 1)],
                          carry[r][3] + rowsum(g[(r, 0)]), carry[r][4] + rowsum(g[(r, 1)])) for r in range(n_rc))

        zc = jnp.zeros((rc, 1), F32)
        carry = tuple((jnp.zeros((rc, LANES), F32), zc, zc, zc, zc) for _ in range(n_rc))
        near = jnp.maximum(qi - 1, 0)
        first = near - jnp.clip(nblk_ref[hp, qi].astype(jnp.int32), 0, near)
        carry = lax.fori_loop(first, qi, block, carry)
        carry = block(qi, carry, True)
        for r in range(n_rc):
            dq_ref[r * rc:(r + 1) * rc, :] = (carry[r][0] * 0.125).astype(BF16)

        @pl.when(qi == pl.num_programs(1) - 1)
        def _():
            dk_ref[...] = dk_acc[...].astype(BF16)
            dv_ref[...] = dv_acc[...].astype(BF16)

    blk = pl.BlockSpec((tq, LANES), lambda hp, qi: (qi, hp))
    col = pl.BlockSpec((s, LANES), lambda hp, qi: (0, hp))
    o512 = jax.ShapeDtypeStruct((s, D_SB), BF16)
    return _hosted_call(
        body, comm, name=name, grid=(4, nq),
        out_shape=(o512, o512, o512),
        in_specs=[pl.BlockSpec(memory_space=pltpu.SMEM), blk,
                  pl.BlockSpec((s, LANES), lambda hp, qi: (0, 4 + hp)),
                  pl.BlockSpec((s, LANES), lambda hp, qi: (0, 8 + hp)), blk, blk],
        out_specs=(blk, col, col),
        scratch_shapes=[pltpu.VMEM((s, LANES), F32), pltpu.VMEM((s, LANES), F32)],
        args=(nblk, qkv, qkv, qkv, lsum, dya), sem=("parallel", "arbitrary"))


def _bwd_dproj(dmisc, dconv, pc, dq, dk, dv, x, dx2, g, cw, layer, win_full, name, comm=None, h=None,
               h_rows=None):
    s = x.shape[0]
    ts = min(ROW_TILE, s)
    blk8 = ts // SUBLANES
    last8 = s // SUBLANES - 1
    fused = h is not None
    emit_dproj = not fused or h_rows is not None
    dw_rows, h_blk = (D_MODEL, 0) if h_rows is None else h_rows

    def body(*refs):
        (dcb_ref, dcz_ref, daz_ref, dconv_ref, nxt_ref, cc_ref, ch_ref, dq_ref, dk_ref, dv_ref,
         x_ref, dx2_ref, g_ref, cw_ref, w_ref) = refs[:15]
        rest = list(refs[15:])
        h_ref = rest.pop(0) if fused else None
        dproj_ref = rest.pop(0) if emit_dproj else None
        dx_ref, dg_ref = rest.pop(0), rest.pop(0)
        dw_ref = rest.pop(0) if fused else None
        dproj_ref = dproj_ref if emit_dproj else rest.pop(0)
        acc_ref = rest.pop(0) if fused else None
        i = pl.program_id(0)

        @pl.when(i == 0)
        def _():
            dg_ref[...] = jnp.zeros_like(dg_ref)
            if fused:
                acc_ref[...] = jnp.zeros_like(acc_ref)

        keep = jnp.where(i == pl.num_programs(0) - 1, 0.0, 1.0)
        dc = dconv_ref[...]
        n0 = nxt_ref[0:1, :] * keep
        n1 = nxt_ref[1:2, :] * keep
        rowi = lax.broadcasted_iota(jnp.int32, dc.shape, 0)
        dc1 = jnp.where(rowi == ts - 1, n0, pltpu.roll(dc, ts - 1, 0))
        dc2 = jnp.where(rowi == ts - 2, n0, jnp.where(rowi == ts - 1, n1, pltpu.roll(dc, ts - 2, 0)))
        du = cw_ref[2:3, :] * dc + cw_ref[1:2, :] * dc1 + cw_ref[0:1, :] * dc2
        dproj_ref[:, 0:512] = dcb_ref[...]
        dproj_ref[:, 512:1024] = (du * ch_ref[...].astype(F32)).astype(BF16)
        dproj_ref[:, 1024:1536] = (du * cc_ref[...].astype(F32)).astype(BF16)
        dproj_ref[:, 1536:2048] = dcz_ref[...]
        dproj_ref[:, 2048:2560] = dq_ref[...]
        dproj_ref[:, 2560:3072] = dk_ref[...]
        dproj_ref[:, 3072:3584] = dv_ref[...]
        dproj_ref[:, 3584:4096] = daz_ref[...]
        dh = lax.dot_general(dproj_ref[...], w_ref[...], NT, preferred_element_type=F32)
        if fused:
            acc_ref[...] += lax.dot_general(h_ref[...], dproj_ref[...], TN, preferred_element_type=F32)
        x = x_ref[...]
        r = lax.rsqrt(jnp.mean(x * x, axis=-1, keepdims=True) + EPS)
        xn = x * r
        dg_ref[...] += jnp.sum(dh * xn, axis=0, keepdims=True)
        dxn = dh * g_ref[...]
        dx_ref[...] = dx2_ref[...] + r * (dxn - xn * jnp.mean(dxn * xn, axis=-1, keepdims=True))
        if fused:
            @pl.when(i == pl.num_programs(0) - 1)
            def _():
                dw_ref[...] = acc_ref[...].astype(BF16)

    row = lambda width, cb_=0: pl.BlockSpec((ts, width), lambda i: (i, cb_))
    nxt = pl.BlockSpec((SUBLANES, 512), lambda i: (jnp.minimum((i + 1) * blk8, last8), 0))
    vec = lambda width: pl.BlockSpec((1, width), lambda i: (0, 0))
    lvec = lambda width: _layer_rows(layer, 1, width)
    once = dict(pipeline_mode=pl.Buffered(1)) if fused else {}
    whole = lambda rows_: pl.BlockSpec((rows_, N_IN), lambda i: (0, 0), **once)
    in_specs = [row(512, 0), row(512, 1), row(512, 2), row(512), nxt, row(512, 1), row(512, 2),
                row(512), row(512), row(512), row(D_MODEL), row(D_MODEL), lvec(D_MODEL),
                _layer_rows(layer, 3, 512), whole(D_MODEL)]
    args = [dmisc, dmisc, dmisc, dconv, dconv, pc, pc, dq, dk, dv, x, dx2, g, cw, win_full]
    out_shape = [jax.ShapeDtypeStruct((s, D_MODEL), F32), jax.ShapeDtypeStruct((1, D_MODEL), F32)]
    out_specs = [row(D_MODEL), vec(D_MODEL)]
    scratch = []
    if emit_dproj:
        out_shape.insert(0, jax.ShapeDtypeStruct((s, N_IN), BF16))
        out_specs.insert(0, row(N_IN))
    else:
        scratch.append(pltpu.VMEM((ts, N_IN), BF16))
    if fused:
        in_specs.append(row(dw_rows, h_blk))
        args.append(h)
        out_shape.append(jax.ShapeDtypeStruct((dw_rows, N_IN), BF16))
        out_specs.append(whole(dw_rows))
        scratch.append(pltpu.VMEM((dw_rows, N_IN), F32))
    return _hosted_call(
        body, comm, name=name, grid=(s // ts,), out_shape=tuple(out_shape), in_specs=in_specs,
        out_specs=tuple(out_specs), scratch_shapes=scratch, args=tuple(args), sem=("arbitrary",))


def _atb(a, b, name, a_cols=None, comm=None):
    s, n = b.shape
    m, a_blk = (a.shape[-1], 0) if a_cols is None else a_cols
    ts = min(512, s)
    tn = min(2048, n)
    a_spec = pl.BlockSpec((ts, m), lambda j, i: (i, a_blk))

    def body(a_ref, b_ref, o_ref, acc_ref):
        i = pl.program_id(1)

        @pl.when(i == 0)
        def _():
            acc_ref[...] = jnp.zeros_like(acc_ref)

        acc_ref[...] += lax.dot_general(a_ref[...].astype(BF16), b_ref[...], TN, preferred_element_type=F32)

        @pl.when(i == pl.num_programs(1) - 1)
        def _():
            o_ref[...] = acc_ref[...].astype(BF16)

    (out,), got = _hosted_call(
        body, comm, name=name, grid=(n // tn, s // ts),
        out_shape=(jax.ShapeDtypeStruct((m, n), BF16),),
        in_specs=[a_spec, pl.BlockSpec((ts, tn), lambda j, i: (i, j))],
        out_specs=(pl.BlockSpec((m, tn), lambda j, i: (0, j)),),
        scratch_shapes=[pltpu.VMEM((m, tn), F32)],
        args=(a, b), sem=("parallel", "arbitrary"))
    return out, got


def _adamw_math(w, g, m, v):
    m2 = ADAM_B1 * m + (1.0 - ADAM_B1) * g
    v2 = ADAM_B2 * v + (1.0 - ADAM_B2) * (g * g)
    m_hat = m2 / (1.0 - ADAM_B1 ** ADAM_STEP)
    v_hat = v2 / (1.0 - ADAM_B2 ** ADAM_STEP)
    delta = -ADAM_LR * (m_hat / (jnp.sqrt(v_hat) + ADAM_EPS) + ADAM_WD * w)
    return delta, m2, v2


def _adamw_sum8(pieces, w, m, v, name):
    _, rows, cols = w.shape
    tr = min([rows, 256] + [pc_[0].shape[1] for pc_ in pieces])
    n_tiles = rows // tr
    n_p = len(pieces)
    spans = [(layer, row0 // tr, arr.shape[1] // tr) for arr, layer, row0 in pieces]

    def body(*refs):
        p_refs = refs[:n_p]
        w_ref, m_ref, v_ref, g_ref, d_ref, m2_ref, v2_ref = refs[n_p:]
        l, i = pl.program_id(0), pl.program_id(1)

        def run(p_ref):
            g = p_ref[0].astype(F32)
            for d in range(1, N_DEV):
                g = g + p_ref[d].astype(F32)
            g_ref[...] = g
            d_ref[...], m2_ref[...], v2_ref[...] = _adamw_math(w_ref[...], g, m_ref[...], v_ref[...])

        for p_ref, (layer, t0, nt) in zip(p_refs, spans):
            mine = jnp.logical_and(l == layer, jnp.logical_and(i >= t0, i < t0 + nt))
            pl.when(mine)(lambda p_ref=p_ref: run(p_ref))

    def piece_spec(layer, t0, nt):
        return pl.BlockSpec((N_DEV, tr, cols),
                            lambda l, i: (0, jnp.clip(jnp.where(l == layer, i - t0, jnp.where(l < layer, 0, nt - 1)),
                                                      0, nt - 1), 0))

    tile = pl.BlockSpec((None, tr, cols), lambda l, i: (l, i, 0))
    o = jax.ShapeDtypeStruct((DEPTH, rows, cols), F32)
    return _call(
        body, name=name, grid=(DEPTH, n_tiles),
        out_shape=(o, o, o, o),
        in_specs=[*[piece_spec(*sp) for sp in spans], tile, tile, tile],
        out_specs=(tile, tile, tile, tile),
        compiler_params=_params(("arbitrary", "arbitrary"), VMEM_LIMIT),
    )(*[pc_[0] for pc_ in pieces], w, m, v)


def _small_update(blk, layered, final, conv, loss_parts):
    n_l = len(layered)
    ins = [a for item in layered for a in item] + list(final) + list(conv) + [loss_parts]
    shapes = [item[2].shape for item in layered] + [final[1].shape, conv[2].shape]
    out_shape = [jax.ShapeDtypeStruct(sh, F32) for sh in shapes for _ in range(4)]
    out_shape.append(jax.ShapeDtypeStruct((1, LANES), F32))

    def body(*refs):
        blk_ref, refs = refs[0], refs[1:]
        in_refs, out_refs, pick_ref = refs[:len(ins)], refs[len(ins):-1], refs[-1]

        def total(ref):
            g = ref[0]
            for d in range(1, N_DEV):
                g = g + ref[d]
            return g

        def update(k, at, g, w_ref, m_ref, v_ref):
            g_ref, d_ref, m2_ref, v2_ref = out_refs[4 * k:4 * k + 4]
            g_ref[at] = g
            d_ref[at], m2_ref[at], v2_ref[at] = _adamw_math(w_ref[at], g, m_ref[at], v_ref[at])

        for k in range(n_l):
            p0, p1, w_ref, m_ref, v_ref = in_refs[5 * k:5 * k + 5]
            for layer, parts in enumerate((p0, p1)):
                update(k, pl.ds(layer, 1), total(parts), w_ref, m_ref, v_ref)
        pf, w_ref, m_ref, v_ref = in_refs[5 * n_l:5 * n_l + 4]
        update(n_l, pl.ds(0, 1), total(pf), w_ref, m_ref, v_ref)
        c0, c1, w_ref, m_ref, v_ref = in_refs[5 * n_l + 4:5 * n_l + 9]
        for layer, parts in enumerate((c0, c1)):
            g8 = total(parts)
            mine = jnp.zeros((SUBLANES, HEAD_DIM), F32)
            for j in range(N_DEV):
                mine = mine + jnp.where(blk_ref[0] == j, g8[:, HEAD_DIM * j:HEAD_DIM * (j + 1)], 0.0)
            pick_ref[...] = mine
            update(n_l + 1, layer, pick_ref[0:3, :], w_ref, m_ref, v_ref)
        out_refs[-1][...] = total(in_refs[-1])

    whole = lambda shape: pl.BlockSpec(shape, lambda: (0,) * len(shape))
    outs = _call(
        body, name="adamw_small",
        out_shape=tuple(out_shape),
        in_specs=[pl.BlockSpec(memory_space=pltpu.SMEM)] + [whole(a.shape) for a in ins],
        out_specs=tuple(whole(o.shape) for o in out_shape),
        scratch_shapes=[pltpu.VMEM((SUBLANES, HEAD_DIM), F32)],
    )(blk, *ins)
    return [outs[4 * k:4 * k + 4] for k in range(n_l + 2)], outs[-1]


def kernel(x, p, norm_g, w_in, conv_w, conv_b, branch_g, w_out, ple_norm_g, w_pg, b_pg, w_pe, final_g, loss_target, m_norm_g, m_w_in, m_conv_w, m_conv_b, m_branch_g, m_w_out, m_ple_norm_g, m_w_pg, m_b_pg, m_w_pe, m_final_g, v_norm_g, v_w_in, v_conv_w, v_conv_b, v_branch_g, v_w_out, v_ple_norm_g, v_w_pg, v_b_pg, v_w_pe, v_final_g):
    s = x.shape[1]
    x0 = x.reshape(s, D_MODEL)
    target = loss_target.reshape(s, D_MODEL)
    me_blk = _my_block()

    win_s, wout_s, wpg_s, wpe_s = _cast_bf16(
        [w_in.reshape(DEPTH * D_MODEL, 512), w_out.reshape(DEPTH * 128, D_MODEL),
         w_pg.reshape(DEPTH * 128, D_MODEL), w_pe.reshape(DEPTH * PLE_DIM, 128)], "cast_weights")
    win_s, wout_s = win_s.reshape(DEPTH, D_MODEL, 512), wout_s.reshape(DEPTH, 128, D_MODEL)
    wpg_s, wpe_s = wpg_s.reshape(DEPTH, 128, D_MODEL), wpe_s.reshape(DEPTH, PLE_DIM, 128)
    cw_s = jnp.zeros((SUBLANES, LANES), F32).at[:DEPTH * 3, :HEAD_DIM].set(conv_w.reshape(DEPTH * 3, HEAD_DIM))
    bf = lambda r_, c_: jax.ShapeDtypeStruct((r_, c_), BF16)
    w_items = lambda l: [(wout_s[l], bf(D_MODEL, D_MODEL), "rows128"), (wpg_s[l], bf(D_MODEL, D_MODEL), "rows128"),
                         (wpe_s[l], bf(PLE_DIM, D_MODEL), "cols128")]
    win_f = [None] * DEPTH
    win_f[0], cw_all = _comm_call(_gather_comm([
        (win_s[0], bf(D_MODEL, N_IN), "cols512"),
        (cw_s, jax.ShapeDtypeStruct((N_DEV, SUBLANES, LANES), F32), "slot")]), "gather_w_in_0")
    cw_full = jnp.transpose(cw_all[:, :DEPTH * 3, :HEAD_DIM].reshape(N_DEV, DEPTH, 3, HEAD_DIM), (1, 2, 0, 3))
    cw_full = cw_full.reshape(DEPTH, 3, D_CONV)
    gather_rest_0 = _gather_comm(w_items(0))
    gather_win_1 = _gather_comm([(win_s[1], bf(D_MODEL, N_IN), "cols512")])
    gather_rest_1 = _gather_comm(w_items(1))

    norm3, convb3, branch3, ple3, bpg3 = [a.reshape(DEPTH, 1, -1) for a in (norm_g, conv_b, branch_g, ple_norm_g, b_pg)]

    saved = []
    xl = x0
    wout_f, wpg_f, wpe_f = [None] * DEPTH, [None] * DEPTH, [None] * DEPTH
    for l in range(DEPTH):
        (h, pc, qkv, az), got = _fwd_in(xl, norm3, l, win_f[l], f"fwd_in_{l}",
                                        comm=gather_rest_0 if l == 0 else None)
        if l == 0:
            wout_f[0], wpg_f[0], wpe_f[0] = got
        (ya, lsum, nblk), got = _attn_fwd(qkv, f"attn_fwd_{l}", comm=gather_win_1 if l == 0 else None)
        if l == 0:
            (win_f[1],) = got
        last = l == DEPTH - 1
        outs, got = _fwd_mid(
            xl, pc, az, ya, p, l, cw_full, convb3, branch3, wout_f[l],
            ple3, wpg_f[l], bpg3, wpe_f[l], f"fwd_mid_{l}",
            comm=gather_rest_1 if l == 0 else None, head=(target, final_g[None, :]) if last else None)
        x2, x3, gated, h2, gate, e = outs[:6]
        if l == 0:
            wout_f[1], wpg_f[1], wpe_f[1] = got
        saved.append(dict(x=xl, h=h, pc=pc, qkv=qkv, az=az, ya=ya, lsum=lsum, nblk=nblk, x2=x2, gated=gated, h2=h2,
                          gate=gate, e=e))
        xl = x3

    dx, (loss_acc, d_final_g) = xl, outs[6:]

    dwin, dwout, dwpg, dwpe = [None] * DEPTH, [None] * DEPTH, [None] * DEPTH, [None] * DEPTH
    small = dict(norm_g=[None] * DEPTH, conv_b=[None] * DEPTH, branch_g=[None] * DEPTH,
                 ple_norm_g=[None] * DEPTH, b_pg=[None] * DEPTH, conv_w=[None] * DEPTH)
    slot = lambda r_, c_: jax.ShapeDtypeStruct((r_, c_), BF16)
    half = D_MODEL // 2
    r_in1, r_out, r_pg, r_pe = None, [None] * DEPTH, [None] * DEPTH, [None] * DEPTH

    def rest_items(l):
        return [(dwout[l], slot(128, D_MODEL), "rows128"), (dwpg[l], slot(128, D_MODEL), "rows128"),
                (dwpe[l], slot(PLE_DIM, 128), "cols128")]

    for l in reversed(range(DEPTH)):
        sv = saved[l]
        ride = _exchange_comm(rest_items(1)) if l == 0 else None
        (dx2, dya, dmisc, dconv, dwout[l], dwpg[l], dwpe[l], d_bpg, d_pg, d_bg, d_cbias, d_cw), got = _bwd_mid(
            dx, sv["x2"], sv["gate"], sv["e"], sv["pc"], sv["az"], sv["ya"], sv["gated"], sv["h2"], p, l,
            cw_full, convb3, branch3, ple3, wpg_f[l], wout_f[l], f"bwd_mid_{l}",
            comm=ride)
        if l == 0:
            r_out[1], r_pg[1], r_pe[1] = got
        ride = _exchange_comm([(dwin[1], slot(D_MODEL, 512), "cols512")] + rest_items(0)) if l == 0 else None
        (dq, dk, dv), got = _attn_bwd(sv["qkv"], sv["lsum"], sv["nblk"], dya, f"attn_bwd_{l}", comm=ride)
        if l == 0:
            r_in1, r_out[0], r_pg[0], r_pe[0] = got
        dproj_args = (dmisc, dconv, sv["pc"], dq, dk, dv, sv["x"], dx2, norm3, cw_full, l, win_f[l])
        if l == 1:
            (dx, d_ng, dwin[1]), _ = _bwd_dproj(*dproj_args, "bwd_dproj_dw_1", h=sv["h"])
        else:
            (dproj, dx, d_ng, dwin_top), _ = _bwd_dproj(*dproj_args, "bwd_dproj_dw_0", h=sv["h"], h_rows=(half, 0))
            dwin_bot, (r_in0_top,) = _atb(sv["h"], dproj, "dw_in_0_bottom", a_cols=(half, 1),
                                          comm=_exchange_comm([(dwin_top, slot(half, 512), "cols512")]))
        small["norm_g"][l], small["conv_b"][l], small["branch_g"][l] = d_ng, d_cbias, d_bg
        small["ple_norm_g"][l], small["b_pg"][l], small["conv_w"][l] = d_pg, d_bpg, d_cw
    grad_x = dx.reshape(1, s, D_MODEL)

    names = ["norm_g", "conv_b", "branch_g", "ple_norm_g", "b_pg", "conv_w"]
    small_list = [small[n][l] for n in names for l in range(DEPTH)] + [d_final_g, loss_acc]
    got = _comm_call(_exchange_comm(
        [(dwin_bot, slot(half, 512), "cols512")]
        + [(a, jax.ShapeDtypeStruct(a.shape, F32), "slot") for a in small_list]), "exchange_last")
    r_in0_bot, r_small = got[0], got[1:]

    per_layer = lambda r: [(r[0], 0, 0), (r[1], 1, 0)]
    g_win, d_win, m_win, v_win = _adamw_sum8([(r_in0_top, 0, 0), (r_in0_bot, 0, half), (r_in1, 1, 0)],
                                             w_in, m_w_in, v_w_in, "adamw_w_in")
    g_wout, d_wout, m_wout, v_wout = _adamw_sum8(per_layer(r_out), w_out, m_w_out, v_w_out, "adamw_w_out")
    g_wpg, d_wpg, m_wpg, v_wpg = _adamw_sum8(per_layer(r_pg), w_pg, m_w_pg, v_w_pg, "adamw_w_pg")
    g_wpe, d_wpe, m_wpe, v_wpe = _adamw_sum8(per_layer(r_pe), w_pe, m_w_pe, v_w_pe, "adamw_w_pe")

    layered = [(norm_g, m_norm_g, v_norm_g), (conv_b, m_conv_b, v_conv_b), (branch_g, m_branch_g, v_branch_g),
               (ple_norm_g, m_ple_norm_g, v_ple_norm_g), (b_pg, m_b_pg, v_b_pg)]
    row = lambda a: a.reshape(1, -1)
    upd, loss_row = _small_update(
        jnp.reshape(me_blk, (1,)).astype(jnp.int32),
        [(r_small[2 * k], r_small[2 * k + 1], *wmv) for k, wmv in enumerate(layered)],
        (r_small[12], row(final_g), row(m_final_g), row(v_final_g)),
        (r_small[10], r_small[11], conv_w, m_conv_w, v_conv_w), r_small[13])
    loss = loss_row[0, 0]
    upd[5] = [a.reshape(-1) for a in upd[5]]

    big = {1: (g_win, d_win, m_win, v_win), 5: (g_wout, d_wout, m_wout, v_wout), 7: (g_wpg, d_wpg, m_wpg, v_wpg),
           9: (g_wpe, d_wpe, m_wpe, v_wpe)}
    small_at = {0: 0, 2: 6, 3: 1, 4: 2, 6: 3, 8: 4, 10: 5}
    per_kind = [[(big[i] if i in big else upd[small_at[i]])[j] for i in range(11)] for j in range(4)]
    return (loss, grad_x, *per_kind[0], *per_kind[1], *per_kind[2], *per_kind[3])
```

```python
import jax
import jax.numpy as jnp
from jax import lax
from jax.experimental import pallas as pl
from jax.experimental.pallas import tpu as pltpu

F32 = jnp.float32
BF16 = jnp.bfloat16

D_MODEL = 1024
D_CONV = 512
D_SB = 512
N_IN = 4096
HEAD_DIM = 64
PLE_DIM = 256
DEPTH = 2
EPS = 1e-6
ADAM_LR = 0.001
ADAM_B1 = 0.9
ADAM_B2 = 0.999
ADAM_EPS = 1e-08
ADAM_WD = 0.01
ADAM_STEP = 10

LANES = 128
SUBLANES = 8
VMEM_BYTES_V7X = 64 * 1024 * 1024
VMEM_LIMIT = VMEM_BYTES_V7X - 8 * 1024 * 1024

N_DEV = 8
ROW_TILE = 256
FWD_ROW_TILE = 512
ATTN_TILE = 256

NT = (((1,), (1,)), ((), ()))
TN = (((0,), (0,)), ((), ()))


def _call(body, **kw):
    return pl.pallas_call(body, **kw)


def _params(sem=None, vmem=None):
    return pltpu.CompilerParams(dimension_semantics=sem, vmem_limit_bytes=vmem)


def _sigmoid(z):
    return 0.5 * jnp.tanh(0.5 * z) + 0.5


def _group_bcast_sum(a, lo):
    s_lo = jnp.sum(jnp.where(lo, a, 0.0), axis=-1, keepdims=True)
    s_hi = jnp.sum(jnp.where(lo, 0.0, a), axis=-1, keepdims=True)
    return jnp.where(lo, s_lo, s_hi)


def _layer_rows(layer, rows, width):
    return pl.BlockSpec((None, rows, width), lambda i: (layer, 0, 0))


def _my_block():
    return 4 * lax.axis_index("x") + 2 * lax.axis_index("y") + lax.axis_index("c")


def _cast_bf16(arrays, name):
    n = len(arrays)

    def body(*refs):
        for a_ref, o_ref in zip(refs[:n], refs[n:]):
            o_ref[...] = a_ref[...].astype(BF16)

    whole = lambda a: pl.BlockSpec(a.shape, lambda: (0, 0))
    return _call(
        body, name=name,
        out_shape=tuple(jax.ShapeDtypeStruct(a.shape, BF16) for a in arrays),
        in_specs=[whole(a) for a in arrays], out_specs=tuple(whole(a) for a in arrays),
        compiler_params=_params(None, VMEM_LIMIT),
    )(*arrays)


class _Comm:
    def __init__(self, inputs, out_shapes, scratch, begin, middle, finish):
        self.inputs, self.out_shapes, self.scratch = list(inputs), list(out_shapes), list(scratch)
        self.begin, self.middle, self.finish = begin, middle, finish


def _slab(kind, ref, blk):
    if kind == "cols512":
        return ref.at[:, pl.ds(blk * 512, 512)]
    if kind == "rows128":
        return ref.at[pl.ds(blk * 128, 128), :]
    if kind == "cols128":
        return ref.at[:, pl.ds(blk * 128, 128)]
    return ref.at[blk]


def _gather_comm(items):
    n_t = len(items)
    kinds = [it[2] for it in items]

    def ctx(ins, outs, sems):
        send_sems, recv_sems, local_sems = sems
        x, y, c = lax.axis_index("x"), lax.axis_index("y"), lax.axis_index("c")
        me, sibling = (x, y, c), (x, y, 1 - c)
        chips = [(1 - x, y), (x, 1 - y), (1 - x, 1 - y)]

        def place(t, dev):
            return _slab(kinds[t], outs[t], 4 * dev[0] + 2 * dev[1] + dev[2])

        def copy(t, k, block, to, own=False):
            return pltpu.make_async_remote_copy(
                src_ref=ins[t] if own else place(t, block), dst_ref=place(t, block),
                send_sem=send_sems.at[t, k], recv_sem=recv_sems.at[t, k],
                device_id=to, device_id_type=pl.DeviceIdType.MESH)

        mine = [pltpu.make_async_copy(ins[t], place(t, me), local_sems.at[t]) for t in range(n_t)]
        first = []
        for t in range(n_t):
            first.append(copy(t, 0, me, sibling, own=True))
            first += [copy(t, 1 + j, me, (*chip, c), own=True) for j, chip in enumerate(chips)]
        passed = [copy(t, 4 + j, (*chip, c), sibling) for j, chip in enumerate(chips) for t in range(n_t)]
        landed = [copy(t, 1 + j, (*chip, c), me) for j, chip in enumerate(chips) for t in range(n_t)]
        from_sibling = []
        for t in range(n_t):
            from_sibling.append(copy(t, 0, sibling, me))
            from_sibling += [copy(t, 4 + j, (*chip, 1 - c), me) for j, chip in enumerate(chips)]
        return mine, first, landed, passed, from_sibling

    def begin(ins, outs, sems):
        mine, first, _, _, _ = ctx(ins, outs, sems)
        for cp in mine + first:
            cp.start()

    def middle(ins, outs, sems):
        _, _, landed, passed, _ = ctx(ins, outs, sems)
        for got, fwd in zip(landed, passed):
            got.wait_recv()
            fwd.start()

    def finish(ins, outs, sems):
        mine, first, _, passed, from_sibling = ctx(ins, outs, sems)
        for cp in from_sibling:
            cp.wait_recv()
        for cp in first + passed:
            cp.wait_send()
        for cp in mine:
            cp.wait()

    scratch = [pltpu.SemaphoreType.DMA((n_t, 7)), pltpu.SemaphoreType.DMA((n_t, 7)), pltpu.SemaphoreType.DMA((n_t,))]
    return _Comm([it[0] for it in items], [it[1] for it in items], scratch, begin, middle, finish)


def _exchange_comm(items):
    n_t = len(items)
    kinds = [it[2] for it in items]

    def ctx(ins, outs, sems):
        send_sems, recv_sems, local_sems = sems
        x, y, c = lax.axis_index("x"), lax.axis_index("y"), lax.axis_index("c")
        me_blk = 4 * x + 2 * y + c

        def src(t, blk):
            return ins[t] if kinds[t] == "slot" else _slab(kinds[t], ins[t], blk)

        local = [pltpu.make_async_copy(src(t, me_blk), outs[t].at[me_blk], local_sems.at[t]) for t in range(n_t)]
        remote = []
        for k in range(1, N_DEV):
            px = 1 - x if k & 4 else x
            py = 1 - y if k & 2 else y
            pc_ = 1 - c if k & 1 else c
            for t in range(n_t):
                remote.append(pltpu.make_async_remote_copy(
                    src_ref=src(t, 4 * px + 2 * py + pc_), dst_ref=outs[t].at[me_blk],
                    send_sem=send_sems.at[k - 1, t], recv_sem=recv_sems.at[k - 1, t],
                    device_id=(px, py, pc_), device_id_type=pl.DeviceIdType.MESH))
        return local, remote

    def begin(ins, outs, sems):
        local, remote = ctx(ins, outs, sems)
        for cp in local + remote:
            cp.start()

    def finish(ins, outs, sems):
        local, remote = ctx(ins, outs, sems)
        for cp in remote:
            cp.wait_recv()
        for cp in remote:
            cp.wait_send()
        for cp in local:
            cp.wait()

    scratch = [pltpu.SemaphoreType.DMA((N_DEV - 1, n_t)), pltpu.SemaphoreType.DMA((N_DEV - 1, n_t)),
               pltpu.SemaphoreType.DMA((n_t,))]
    out_shapes = [jax.ShapeDtypeStruct((N_DEV, *it[1].shape), it[1].dtype) for it in items]
    return _Comm([it[0] for it in items], out_shapes, scratch, begin, None, finish)


def _comm_call(comm, name):
    n_in, n_out = len(comm.inputs), len(comm.out_shapes)

    def body(*refs):
        ins, outs, sems = refs[:n_in], refs[n_in:n_in + n_out], refs[n_in + n_out:]
        comm.begin(ins, outs, sems)
        if comm.middle is not None:
            comm.middle(ins, outs, sems)
        comm.finish(ins, outs, sems)

    any_spec = pl.BlockSpec(memory_space=pl.ANY)
    return _call(body, name=name, out_shape=tuple(comm.out_shapes), in_specs=[any_spec] * n_in,
                 out_specs=[any_spec] * n_out, scratch_shapes=comm.scratch)(*comm.inputs)


def _hosted(body, n_in, n_out, comm, first, last, middle):
    if comm is None:
        return lambda *refs: body(*refs)
    n_ci, n_co, n_cs = len(comm.inputs), len(comm.out_shapes), len(comm.scratch)

    def wrapped(*refs):
        ins, cin = refs[:n_in], refs[n_in:n_in + n_ci]
        o0 = n_in + n_ci
        outs, cout = refs[o0:o0 + n_out], refs[o0 + n_out:o0 + n_out + n_co]
        scr, csem = refs[o0 + n_out + n_co:len(refs) - n_cs], refs[len(refs) - n_cs:]
        pl.when(first())(lambda: comm.begin(cin, cout, csem))
        body(*ins, *outs, *scr)
        if comm.middle is not None:
            pl.when(middle())(lambda: comm.middle(cin, cout, csem))
        pl.when(last())(lambda: comm.finish(cin, cout, csem))

    return wrapped


def _hosted_call(body, comm, *, name, grid, out_shape, in_specs, out_specs, args, scratch_shapes=(), sem=None):
    nd = len(grid)
    first, last, middle = _at_first(nd), _at_last(nd), _at_middle(nd)
    if comm is not None:
        sem = ("arbitrary",) * nd
    n_in, n_out = len(in_specs), len(out_shape)
    any_spec = pl.BlockSpec(memory_space=pl.ANY)
    c_in = [] if comm is None else comm.inputs
    c_out = [] if comm is None else comm.out_shapes
    c_scr = [] if comm is None else comm.scratch
    outs = _call(
        _hosted(body, n_in, n_out, comm, first, last, middle), name=name, grid=grid,
        out_shape=(*out_shape, *c_out),
        in_specs=[*in_specs, *[any_spec] * len(c_in)],
        out_specs=(*out_specs, *[any_spec] * len(c_out)),
        scratch_shapes=[*scratch_shapes, *c_scr],
        compiler_params=_params(sem, VMEM_LIMIT),
    )(*args, *c_in)
    return outs[:n_out], outs[n_out:]


def _grid_step(ndim):
    i, n = pl.program_id(0), pl.num_programs(0)
    for d in range(1, ndim):
        i, n = i * pl.num_programs(d) + pl.program_id(d), n * pl.num_programs(d)
    return i, n


def _at_first(ndim):
    return lambda: _grid_step(ndim)[0] == 0


def _at_last(ndim):
    def pred():
        i, n = _grid_step(ndim)
        return i == n - 1
    return pred


def _at_middle(ndim):
    def pred():
        i, n = _grid_step(ndim)
        return i == (3 * n) // 4
    return pred


def _fwd_in(x, g, layer, w_full, name, comm=None):
    s = x.shape[0]
    ts = min(FWD_ROW_TILE, s)

    def body(x_ref, g_ref, w_ref, h_ref, pc_ref, qkv_ref, az_ref):
        xf = x_ref[...]
        r = lax.rsqrt(jnp.mean(xf * xf, axis=-1, keepdims=True) + EPS)
        h = (xf * r * g_ref[...]).astype(BF16)
        h_ref[...] = h
        pc_ref[...] = jnp.dot(h, w_ref[:, 0:2048], preferred_element_type=F32).astype(BF16)
        q = jnp.dot(h, w_ref[:, 2048:2560], preferred_element_type=F32)
        qkv_ref[:, 0:512] = (q * 0.125).astype(BF16)
        qkv_ref[:, 512:1536] = jnp.dot(h, w_ref[:, 2560:3584], preferred_element_type=F32).astype(BF16)
        az_ref[...] = jnp.dot(h, w_ref[:, 3584:4096], preferred_element_type=F32).astype(BF16)

    row = lambda width: pl.BlockSpec((ts, width), lambda i: (i, 0))
    return _hosted_call(
        body, comm, name=name, grid=(s // ts,),
        out_shape=(jax.ShapeDtypeStruct((s, D_MODEL), BF16), jax.ShapeDtypeStruct((s, 2048), BF16),
                   jax.ShapeDtypeStruct((s, 1536), BF16), jax.ShapeDtypeStruct((s, 512), BF16)),
        in_specs=[row(D_MODEL), _layer_rows(layer, 1, D_MODEL),
                  pl.BlockSpec((D_MODEL, N_IN), lambda i: (0, 0))],
        out_specs=(row(D_MODEL), row(2048), row(1536), row(512)),
        args=(x, g, w_full), sem=("parallel",))


ATTN_ROWS = 128
ATTN_DONE = 104.0


def _attn_pieces(tq, rc):
    lane = lax.broadcasted_iota(jnp.int32, (1, LANES), 1)
    lo = lane < HEAD_DIM
    row = lax.broadcasted_iota(jnp.int32, (tq, tq), 0)
    col = lax.broadcasted_iota(jnp.int32, (tq, tq), 1)
    tri_gt = jnp.where(row > col, 1.0, 0.0).astype(BF16)
    tri_le = jnp.where(row <= col, 1.0, 0.0).astype(BF16)
    rrow = lax.broadcasted_iota(jnp.int32, (rc, tq), 0)
    rcol = lax.broadcasted_iota(jnp.int32, (rc, tq), 1)
    causal = [rcol < rrow + r * rc for r in range(tq // rc)]
    return lo, causal, tri_gt, tri_le


def _split_heads(a, lo):
    z = jnp.zeros_like(a)
    return (jnp.where(lo, a, z), jnp.where(lo, z, a))


def _softplus(z, causal, diag):
    neg_abs = lax.bitcast_convert_type(lax.bitcast_convert_type(z, jnp.uint32) | jnp.uint32(0x80000000), F32)
    sp = jnp.maximum(z, 0.0) + jnp.log(1.0 + jnp.exp(neg_abs))
    if diag:
        sp = jnp.where(causal, sp, 0.0)
    return sp


def _attn_fwd(qkv, name, comm=None):
    s = qkv.shape[0]
    tq = min(ATTN_TILE, s)
    nq = s // tq
    rc = min(ATTN_ROWS, tq)
    n_rc = tq // rc
    nt = 4 if nq % 4 == 0 else 2 if nq % 2 == 0 else 1
    chains = [(t, r, hh) for t in range(nt) for r in range(n_rc) for hh in range(2)]

    def body(q_ref, k_ref, v_ref, o_ref, lsum_ref, nblk_ref):
        hp = pl.program_id(0)
        qis = [pl.program_id(1) * nt + t for t in range(nt)]
        lo, causal, tri_gt, _ = _attn_pieces(tq, rc)
        qh = _split_heads(q_ref[...], lo)
        qc = {(t, r, hh): qh[hh][t * tq + r * rc:t * tq + (r + 1) * rc] for t, r, hh in chains}

        mm = lambda a_, b_: jnp.dot(a_.astype(BF16), b_, preferred_element_type=F32)
        rowsum = lambda a_: jnp.sum(a_, axis=-1, keepdims=True)

        def block(kb, carry, t):
            mine = [ch for ch in chains if ch[0] == t]
            start = pl.multiple_of(kb * tq, tq)
            k = k_ref[pl.ds(start, tq), :]
            vh = _split_heads(v_ref[pl.ds(start, tq), :], lo)
            z = {ch: lax.dot_general(qc[ch], k, NT, preferred_element_type=F32) for ch in mine}
            sp = {ch: _softplus(z[ch], None, False) for ch in mine}
            later = {ch: mm(sp[ch], tri_gt) for ch in mine}
            a = {ch: jnp.exp((z[ch] - sp[ch]) - (carry[ch[1]][1 + ch[2]] + later[ch])) for ch in mine}
            pv = {ch: mm(a[ch], vh[ch[2]]) for ch in mine}
            return tuple((carry[r][0] + pv[(t, r, 0)] + pv[(t, r, 1)], carry[r][1] + rowsum(sp[(t, r, 0)]),
                          carry[r][2] + rowsum(sp[(t, r, 1)])) for r in range(n_rc))

        def first_two():
            ok = [qi > 0 for qi in qis]
            d0 = [pl.multiple_of(qi * tq, tq) for qi in qis]
            p0 = [pl.multiple_of(jnp.maximum(qi - 1, 0) * tq, tq) for qi in qis]
            k_d = [k_ref[pl.ds(d0[t], tq), :] for t in range(nt)]
            k_p = [k_ref[pl.ds(p0[t], tq), :] for t in range(nt)]
            vh_d = [_split_heads(v_ref[pl.ds(d0[t], tq), :], lo) for t in range(nt)]
            vh_p = [_split_heads(v_ref[pl.ds(p0[t], tq), :], lo) for t in range(nt)]
            z_d = {ch: lax.dot_general(qc[ch], k_d[ch[0]], NT, preferred_element_type=F32) for ch in chains}
            z_p = {ch: lax.dot_general(qc[ch], k_p[ch[0]], NT, preferred_element_type=F32) for ch in chains}
            sp_d = {ch: _softplus(z_d[ch], causal[ch[1]], True) for ch in chains}
            sp_raw = {ch: _softplus(z_p[ch], None, False) for ch in chains}
            sp_p = {ch: jnp.where(ok[ch[0]], sp_raw[ch], 0.0) for ch in chains}
            later_d = {ch: mm(sp_d[ch], tri_gt) for ch in chains}
            later_p = {ch: mm(sp_p[ch], tri_gt) for ch in chains}
            c_d = {ch: rowsum(sp_d[ch]) for ch in chains}
            a_d = {ch: jnp.where(causal[ch[1]], jnp.exp((z_d[ch] - sp_d[ch]) - later_d[ch]), 0.0) for ch in chains}
            a_p = {ch: jnp.where(ok[ch[0]], jnp.exp((z_p[ch] - sp_raw[ch]) - (c_d[ch] + later_p[ch])), 0.0)
                   for ch in chains}
            pv = {ch: mm(a_d[ch], vh_d[ch[0]][ch[2]]) + mm(a_p[ch], vh_p[ch[0]][ch[2]]) for ch in chains}
            return [tuple((pv[(t, r, 0)] + pv[(t, r, 1)], c_d[(t, r, 0)] + rowsum(sp_p[(t, r, 0)]),
                           c_d[(t, r, 1)] + rowsum(sp_p[(t, r, 1)])) for r in range(n_rc)) for t in range(nt)]

        def least(carry):
            m = jnp.minimum(carry[0][1], carry[0][2])
            for r in range(1, n_rc):
                m = jnp.minimum(m, jnp.minimum(carry[r][1], carry[r][2]))
            return jnp.min(m)

        carries = first_two()
        for t, qi in enumerate(qis):
            def go_on(st, qi=qi):
                return jnp.logical_and(st[0] < qi - 1, st[1] < ATTN_DONE)

            def step(st, qi=qi, t=t):
                new = block(qi - 2 - st[0], st[2], t)
                return st[0] + 1, least(new), new

            walked, _, carry = lax.while_loop(go_on, step, (jnp.int32(0), least(carries[t]), carries[t]))
            for r in range(n_rc):
                rows = slice(t * tq + r * rc, t * tq + (r + 1) * rc)
                o_ref[rows, :] = carry[r][0].astype(BF16)
                lsum_ref[rows, :] = jnp.where(lo, carry[r][1], carry[r][2])
            nblk_ref[hp, qi] = walked.astype(F32)

    blk = pl.BlockSpec((nt * tq, LANES), lambda hp, qg: (qg, hp))
    o512 = jax.ShapeDtypeStruct((s, D_SB), F32)
    return _hosted_call(
        body, comm, name=name, grid=(4, nq // nt),
        out_shape=(jax.ShapeDtypeStruct((s, D_SB), BF16), o512, jax.ShapeDtypeStruct((4, nq), F32)),
        in_specs=[blk, pl.BlockSpec((s, LANES), lambda hp, qg: (0, 4 + hp)),
                  pl.BlockSpec((s, LANES), lambda hp, qg: (0, 8 + hp))],
        out_specs=(blk, blk, pl.BlockSpec(memory_space=pltpu.SMEM)),
        args=(qkv, qkv, qkv), sem=("arbitrary", "arbitrary"))


HALO = 16


def _conv_taps(cc_ref, ch_ref, ccp_ref, chp_ref, halo_ref, first):
    u = cc_ref[...].astype(F32) * ch_ref[...].astype(F32)
    halo_ref[...] = ccp_ref[...].astype(F32) * chp_ref[...].astype(F32) * jnp.where(first, 0.0, 1.0)
    p6 = halo_ref[HALO - 2:HALO - 1, :]
    p7 = halo_ref[HALO - 1:HALO, :]
    rowi = lax.broadcasted_iota(jnp.int32, u.shape, 0)
    u1 = jnp.where(rowi == 0, p7, pltpu.roll(u, 1, 0))
    u2 = jnp.where(rowi == 0, p6, jnp.where(rowi == 1, p7, pltpu.roll(u, 2, 0)))
    return u, u1, u2


def _fwd_mid(x, pc, az, ya, p4, layer, cw, cb, bg, wout_full, pg, wpg_full, bpg, wpe_full, name, comm=None,
             head=None):
    s = x.shape[0]
    ts = min(FWD_ROW_TILE, s)
    blk_h = ts // HALO
    n_in = 18 + (2 if head else 0)

    def body(*refs):
        (x_ref, cb_ref_, cc_ref, ch_ref, cz_ref, ccp_ref, chp_ref, az_ref, ya_ref, p_ref,
         cw_ref, cbias_ref, bg_ref, wout_ref, pg_ref, wpg_ref, bpg_ref, wpe_ref) = refs[:18]
        x2_ref, x3_ref, gated_ref, h2_ref, gate_ref, e_ref = refs[n_in:n_in + 6]
        halo_ref = refs[-1]
        i = pl.program_id(0)
        lane = lax.broadcasted_iota(jnp.int32, (1, LANES), 1)
        lo = lane < HEAD_DIM
        u, u1, u2 = _conv_taps(cc_ref, ch_ref, ccp_ref, chp_ref, halo_ref, i == 0)
        conv = cbias_ref[...] + cw_ref[0:1, :] * u2 + cw_ref[1:2, :] * u1 + cw_ref[2:3, :] * u
        yc = cb_ref_[...].astype(F32) * conv
        for sl in range(8):
            cols = slice(LANES * (sl % 4), LANES * (sl % 4 + 1))
            y = yc[:, cols] if sl < 4 else ya_ref[:, cols].astype(F32)
            zc = (cz_ref[:, cols] if sl < 4 else az_ref[:, cols]).astype(F32)
            rg = lax.rsqrt(_group_bcast_sum(y * y, lo) * (1.0 / HEAD_DIM) + EPS)
            yn = y * rg * bg_ref[:, LANES * sl:LANES * (sl + 1)]
            gated_ref[:, LANES * sl:LANES * (sl + 1)] = (yn * (zc * _sigmoid(zc))).astype(BF16)
        x2 = x_ref[...] + jnp.dot(gated_ref[...], wout_ref[...], preferred_element_type=F32)
        x2_ref[...] = x2
        r2 = lax.rsqrt(jnp.mean(x2 * x2, axis=-1, keepdims=True) + EPS)
        h2 = (x2 * r2 * pg_ref[...]).astype(BF16)
        h2_ref[...] = h2
        gate = _sigmoid(jnp.dot(h2, wpg_ref[...], preferred_element_type=F32) + bpg_ref[...])
        gate_ref[...] = gate.astype(BF16)
        e = jnp.dot(p_ref[...].astype(BF16), wpe_ref[...], preferred_element_type=F32)
        e_ref[...] = e.astype(BF16)
        x3 = x2 + gate * e
        if not head:
            x3_ref[...] = x3
            return
        t_ref, fg_ref = refs[18:20]
        loss_ref, dfg_ref = refs[n_in + 6:n_in + 8]
        dx, loss, dfg = _loss_math(x3, t_ref[...], fg_ref[...])

        @pl.when(i == 0)
        def _():
            loss_ref[...] = jnp.zeros_like(loss_ref)
            dfg_ref[...] = jnp.zeros_like(dfg_ref)

        x3_ref[...] = dx
        loss_ref[...] += loss
        dfg_ref[...] += dfg

    row = lambda width, cb_=0: pl.BlockSpec((ts, width), lambda i: (i, cb_))
    prev = lambda cb_: pl.BlockSpec((HALO, 512), lambda i: (jnp.maximum(i * blk_h - 1, 0), cb_))
    vec = lambda width: pl.BlockSpec((1, width), lambda i: (0, 0))
    lvec = lambda width: _layer_rows(layer, 1, width)
    wspec = lambda r_, c_: pl.BlockSpec((r_, c_), lambda i: (0, 0))
    f32o = jax.ShapeDtypeStruct((s, D_MODEL), F32)
    bfo = jax.ShapeDtypeStruct((s, D_MODEL), BF16)
    head_in = [row(D_MODEL), vec(D_MODEL)] if head else []
    head_out = [jax.ShapeDtypeStruct((1, LANES), F32), jax.ShapeDtypeStruct((1, D_MODEL), F32)] if head else []
    return _hosted_call(
        body, comm, name=name, grid=(s // ts,),
        out_shape=(f32o, f32o, bfo, bfo, bfo, bfo, *head_out),
        scratch_shapes=[pltpu.VMEM((HALO, 512), F32)],
        in_specs=[row(D_MODEL), row(512, 0), row(512, 1), row(512, 2), row(512, 3), prev(1), prev(2),
                  row(512), row(512),
                  pl.BlockSpec((None, None, ts, PLE_DIM), lambda i: (layer, 0, i, 0)),
                  _layer_rows(layer, 3, 512), lvec(512), lvec(D_MODEL),
                  wspec(D_MODEL, D_MODEL), lvec(D_MODEL), wspec(D_MODEL, D_MODEL), lvec(D_MODEL),
                  wspec(PLE_DIM, D_MODEL), *head_in],
        out_specs=(*[row(D_MODEL)] * 6, *([vec(LANES), vec(D_MODEL)] if head else [])),
        args=(x, pc, pc, pc, pc, pc, pc, az, ya, p4, cw, cb, bg, wout_full, pg, wpg_full, bpg, wpe_full,
              *(head or ())),
        sem=("arbitrary",) if head else ("parallel",))


def _loss_math(x, target, g):
    r = lax.rsqrt(jnp.mean(x * x, axis=-1, keepdims=True) + EPS)
    xn = x * r
    err = xn * g - target
    per_row = jnp.sum(err * err, axis=-1, keepdims=True)
    loss = jnp.sum(per_row, axis=0, keepdims=True) * (0.5 / D_MODEL)
    dy = err * (1.0 / D_MODEL)
    dg = jnp.sum(dy * xn, axis=0, keepdims=True)
    dxn = dy * g
    return r * (dxn - xn * jnp.mean(dxn * xn, axis=-1, keepdims=True)), loss, dg


def _bwd_mid(dx3, x2, gate, e, pc, az, ya, gated, h2, p4, layer, cw, cb, bg, pg, wpg_full, wout_full, name,
             comm=None):
    s = x2.shape[0]
    ts = min(ROW_TILE, s)
    blk_h = ts // HALO

    def body(dx3_ref, x2_ref, gate_ref, e_ref, cb_ref_, cc_ref, ch_ref, cz_ref, ccp_ref, chp_ref, az_ref, ya_ref,
             gated_ref, h2_ref, p_ref, cw_ref, cbias_ref, bg_ref, pg_ref, wpg_ref, wout_ref,
             dx2_ref, dya_ref, dmisc_ref, dconv_ref, dwout_ref, dwpg_ref, dwpe_ref,
             dbpg_ref, dpg_ref, dbg_ref, dcbias_ref, dcw_ref,
             dgated_ref, halo_ref, acc_out, acc_pg, acc_pe):
        i = pl.program_id(0)

        @pl.when(i == 0)
        def _():
            for ref in (dbpg_ref, dpg_ref, dbg_ref, dcbias_ref, dcw_ref, acc_out, acc_pg, acc_pe):
                ref[...] = jnp.zeros_like(ref)

        lane = lax.broadcasted_iota(jnp.int32, (1, LANES), 1)
        lo = lane < HEAD_DIM
        dx3 = dx3_ref[...]
        gate = gate_ref[...].astype(F32)
        de_b = (dx3 * gate).astype(BF16)
        dgpre = dx3 * e_ref[...].astype(F32) * gate * (1.0 - gate)
        dbpg_ref[...] += jnp.sum(dgpre, axis=0, keepdims=True)
        dgpre_b = dgpre.astype(BF16)
        dh2 = lax.dot_general(dgpre_b, wpg_ref[...], NT, preferred_element_type=F32)
        acc_pe[...] += lax.dot_general(p_ref[...].astype(BF16), de_b, TN, preferred_element_type=F32)
        acc_pg[...] += lax.dot_general(h2_ref[...], dgpre_b, TN, preferred_element_type=F32)

        u, u1, u2 = _conv_taps(cc_ref, ch_ref, ccp_ref, chp_ref, halo_ref, i == 0)
        conv = cbias_ref[...] + cw_ref[0:1, :] * u2 + cw_ref[1:2, :] * u1 + cw_ref[2:3, :] * u
        c_b = cb_ref_[...].astype(F32)
        yc = c_b * conv
        fwd = []
        for sl in range(8):
            cols = slice(LANES * (sl % 4), LANES * (sl % 4 + 1))
            y = yc[:, cols] if sl < 4 else ya_ref[:, cols].astype(F32)
            zc = (cz_ref[:, cols] if sl < 4 else az_ref[:, cols]).astype(F32)
            rg = lax.rsqrt(_group_bcast_sum(y * y, lo) * (1.0 / HEAD_DIM) + EPS)
            sig = _sigmoid(zc)
            fwd.append((rg, y * rg, zc * sig, sig * (1.0 + zc * (1.0 - sig))))

        x2 = x2_ref[...]
        r2 = lax.rsqrt(jnp.mean(x2 * x2, axis=-1, keepdims=True) + EPS)
        xn2 = x2 * r2
        dpg_ref[...] += jnp.sum(dh2 * xn2, axis=0, keepdims=True)
        dxn = dh2 * pg_ref[...]
        dx2 = dx3 + r2 * (dxn - xn2 * jnp.mean(dxn * xn2, axis=-1, keepdims=True))
        dx2_ref[...] = dx2
        dx2_b = dx2.astype(BF16)
        dgated_ref[...] = lax.dot_general(dx2_b, wout_ref[...], NT, preferred_element_type=F32)
        acc_out[...] += lax.dot_general(gated_ref[...], dx2_b, TN, preferred_element_type=F32)

        for sl in range(8):
            cols = slice(LANES * (sl % 4), LANES * (sl % 4 + 1))
            wide = slice(LANES * sl, LANES * (sl + 1))
            rg, yhat, silu, dsilu = fwd[sl]
            bgs = bg_ref[:, wide]
            dgt = dgated_ref[:, wide]
            dyn = dgt * silu
            dzc = dgt * (yhat * bgs) * dsilu
            dbg_ref[:, wide] += jnp.sum(dyn * yhat, axis=0, keepdims=True)
            dyh = dyn * bgs
            dy = rg * (dyh - yhat * (_group_bcast_sum(dyh * yhat, lo) * (1.0 / HEAD_DIM)))
            if sl < 4:
                dconv = dy * c_b[:, cols]
                dmisc_ref[:, cols] = (dy * conv[:, cols]).astype(BF16)
                dmisc_ref[:, 512 + LANES * sl:512 + LANES * (sl + 1)] = dzc.astype(BF16)
                dconv_ref[:, cols] = dconv
                dcbias_ref[:, cols] += jnp.sum(dconv, axis=0, keepdims=True)
                dcw_ref[0:1, cols] += jnp.sum(dconv * u2[:, cols], axis=0, keepdims=True)
                dcw_ref[1:2, cols] += jnp.sum(dconv * u1[:, cols], axis=0, keepdims=True)
                dcw_ref[2:3, cols] += jnp.sum(dconv * u[:, cols], axis=0, keepdims=True)
            else:
                dya_ref[:, cols] = dy.astype(BF16)
                dmisc_ref[:, 1024 + LANES * (sl - 4):1024 + LANES * (sl - 3)] = dzc.astype(BF16)

        @pl.when(i == pl.num_programs(0) - 1)
        def _():
            dwout_ref[...] = acc_out[...].astype(BF16)
            dwpg_ref[...] = acc_pg[...].astype(BF16)
            dwpe_ref[...] = acc_pe[...].astype(BF16)

    row = lambda width, cb_=0: pl.BlockSpec((ts, width), lambda i: (i, cb_))
    prev = lambda cb_: pl.BlockSpec((HALO, 512), lambda i: (jnp.maximum(i * blk_h - 1, 0), cb_))
    vec = lambda width: pl.BlockSpec((1, width), lambda i: (0, 0))
    lvec = lambda width: _layer_rows(layer, 1, width)
    wspec = lambda r_, c_: pl.BlockSpec((r_, c_), lambda i: (0, 0))
    vo = lambda width: jax.ShapeDtypeStruct((1, width), F32)
    sq = jax.ShapeDtypeStruct((D_MODEL, D_MODEL), BF16)
    return _hosted_call(
        body, comm, name=name, grid=(s // ts,), sem=("arbitrary",),
        args=(dx3, x2, gate, e, pc, pc, pc, pc, pc, pc, az, ya, gated, h2, p4, cw, cb, bg, pg, wpg_full, wout_full),
        out_shape=(jax.ShapeDtypeStruct((s, D_MODEL), F32), jax.ShapeDtypeStruct((s, 512), BF16),
                   jax.ShapeDtypeStruct((s, 1536), BF16), jax.ShapeDtypeStruct((s, 512), F32),
                   sq, sq, jax.ShapeDtypeStruct((PLE_DIM, D_MODEL), BF16),
                   vo(D_MODEL), vo(D_MODEL), vo(D_MODEL), vo(512), jax.ShapeDtypeStruct((SUBLANES, 512), F32)),
        in_specs=[row(D_MODEL), row(D_MODEL), row(D_MODEL), row(D_MODEL),
                  row(512, 0), row(512, 1), row(512, 2), row(512, 3), prev(1), prev(2), row(512), row(512),
                  row(D_MODEL), row(D_MODEL),
                  pl.BlockSpec((None, None, ts, PLE_DIM), lambda i: (layer, 0, i, 0)),
                  _layer_rows(layer, 3, 512), lvec(512), lvec(D_MODEL), lvec(D_MODEL),
                  wspec(D_MODEL, D_MODEL), wspec(D_MODEL, D_MODEL)],
        out_specs=(row(D_MODEL), row(512), row(1536), row(512),
                   wspec(D_MODEL, D_MODEL), wspec(D_MODEL, D_MODEL), wspec(PLE_DIM, D_MODEL),
                   vec(D_MODEL), vec(D_MODEL), vec(D_MODEL), vec(512),
                   pl.BlockSpec((SUBLANES, 512), lambda i: (0, 0))),
        scratch_shapes=[pltpu.VMEM((ts, D_MODEL), F32), pltpu.VMEM((HALO, 512), F32),
                        pltpu.VMEM((D_MODEL, D_MODEL), F32), pltpu.VMEM((D_MODEL, D_MODEL), F32),
                        pltpu.VMEM((PLE_DIM, D_MODEL), F32)])


def _attn_bwd(qkv, lsum, nblk, dya, name, comm=None):
    s = qkv.shape[0]
    tq = min(ATTN_TILE, s)
    nq = s // tq
    rc = min(ATTN_ROWS, tq)
    n_rc = tq // rc
    nt = 2 if nq % 2 == 0 else 1
    chains = [(t, r, hh) for t in range(nt) for r in range(n_rc) for hh in range(2)]

    def body(nblk_ref, q_ref, k_ref, v_ref, lsum_ref, do_ref, dq_ref, dk_ref, dv_ref, dk_acc, dv_acc):
        hp, qg = pl.program_id(0), pl.program_id(1)
        qis = [qg * nt + t for t in range(nt)]

        @pl.when(qg == 0)
        def _():
            dk_acc[...] = jnp.zeros_like(dk_acc)
            dv_acc[...] = jnp.zeros_like(dv_acc)

        lo, causal, tri_gt, tri_le = _attn_pieces(tq, rc)
        lane = lax.broadcasted_iota(jnp.int32, (1, LANES), 1)
        qh = _split_heads(q_ref[...], lo)
        doh = _split_heads(do_ref[...].astype(BF16), lo)
        lt = lsum_ref[...]
        ltot_h = (jnp.sum(jnp.where(lane == 0, lt, 0.0), axis=-1, keepdims=True),
                  jnp.sum(jnp.where(lane == HEAD_DIM, lt, 0.0), axis=-1, keepdims=True))
        rows = lambda a_, t, r: a_[t * tq + r * rc:t * tq + (r + 1) * rc]
        qc = {(t, r, hh): rows(qh[hh], t, r) for t, r, hh in chains}
        doc = {(t, r, hh): rows(doh[hh], t, r) for t, r, hh in chains}
        ltot = {(t, r, hh): rows(ltot_h[hh], t, r) for t, r, hh in chains}

        mm = lambda a_, b_: jnp.dot(a_.astype(BF16), b_, preferred_element_type=F32)
        mm_nt = lambda a_, b_: lax.dot_general(a_, b_, NT, preferred_element_type=F32)
        mm_tn = lambda a_, b_: lax.dot_general(a_.astype(BF16), b_, TN, preferred_element_type=F32)
        rowsum = lambda a_: jnp.sum(a_, axis=-1, keepdims=True)

        def block(kbs, carries, tiles, diag=False, ok=None):
            mine = [ch for ch in chains if ch[0] in tiles]
            start = {t: pl.multiple_of(kbs[t] * tq, tq) for t in tiles}
            k = {t: k_ref[pl.ds(start[t], tq), :] for t in tiles}
            v = {t: v_ref[pl.ds(start[t], tq), :] for t in tiles}
            kh = {t: _split_heads(k[t], lo) for t in tiles}
            keep = (lambda ch, a_: jnp.where(causal[ch[1]], a_, 0.0)) if diag else (lambda ch, a_: a_)
            live = (lambda ch, a_: a_) if ok is None else (lambda ch, a_: jnp.where(ok[ch[0]], a_, 0.0))
            z = {ch: mm_nt(qc[ch], k[ch[0]]) for ch in mine}
            da = {ch: mm_nt(doc[ch], v[ch[0]]) for ch in mine}
            sp_all = {ch: _softplus(z[ch], None, False) for ch in mine}
            sp = {ch: live(ch, keep(ch, sp_all[ch])) for ch in mine}
            later = {ch: mm(sp[ch], tri_gt) for ch in mine}
            walked = {ch: carries[ch[0]][ch[1]][1 + ch[2]] + rowsum(sp[ch]) for ch in mine}
            a = {ch: live(ch, keep(ch, jnp.exp((z[ch] - sp_all[ch]) - ((ltot[ch] - walked[ch]) + later[ch]))))
                 for ch in mine}
            g = {ch: a[ch] * da[ch] for ch in mine}
            upto = {ch: mm(g[ch], tri_le) for ch in mine}
            dz = {ch: keep(ch, g[ch] - jnp.exp(z[ch] - sp_all[ch]) * (carries[ch[0]][ch[1]][3 + ch[2]] + upto[ch])
                           ).astype(BF16) for ch in mine}
            dqc = {ch: mm(dz[ch], kh[ch[0]][ch[2]]) for ch in mine}
            for t in tiles:
                dkc = [mm_tn(dz[ch], qc[ch]) for ch in mine if ch[0] == t]
                dvc = [mm_tn(a[ch], doc[ch]) for ch in mine if ch[0] == t]
                dk_acc[pl.ds(start[t], tq), :] += sum(dkc[1:], dkc[0])
                dv_acc[pl.ds(start[t], tq), :] += sum(dvc[1:], dvc[0])
            return {t: tuple((carries[t][r][0] + dqc[(t, r, 0)] + dqc[(t, r, 1)], walked[(t, r, 0)], walked[(t, r, 1)],
                              carries[t][r][3] + rowsum(g[(t, r, 0)]), carries[t][r][4] + rowsum(g[(t, r, 1)]))
                             for r in range(n_rc)) for t in tiles}

        zc = jnp.zeros((rc, 1), F32)
        tiles = list(range(nt))
        carries = {t: tuple((jnp.zeros((rc, LANES), F32), zc, zc, zc, zc) for _ in range(n_rc)) for t in tiles}
        near = {t: jnp.maximum(qis[t] - 1, 0) for t in tiles}
        for t in tiles:
            first = near[t] - jnp.clip(nblk_ref[hp, qis[t]].astype(jnp.int32), 0, near[t])
            carries[t] = lax.fori_loop(first, near[t], lambda kb, c, t=t: block({t: kb}, {t: c}, [t])[t], carries[t])
        carries = block(near, carries, tiles, ok={t: qis[t] > 0 for t in tiles})
        carries = block({t: qis[t] for t in tiles}, carries, tiles, diag=True)
        for t in tiles:
            for r in range(n_rc):
                dq_ref[t * tq + r * rc:t * tq + (r + 1) * rc, :] = (carries[t][r][0] * 0.125).astype(BF16)

        @pl.when(qg == pl.num_programs(1) - 1)
        def _():
            dk_ref[...] = dk_acc[...].astype(BF16)
            dv_ref[...] = dv_acc[...].astype(BF16)

    blk = pl.BlockSpec((nt * tq, LANES), lambda hp, qg: (qg, hp))
    col = pl.BlockSpec((s, LANES), lambda hp, qg: (0, hp))
    o512 = jax.ShapeDtypeStruct((s, D_SB), BF16)
    return _hosted_call(
        body, comm, name=name, grid=(4, nq // nt),
        out_shape=(o512, o512, o512),
        in_specs=[pl.BlockSpec(memory_space=pltpu.SMEM), blk,
                  pl.BlockSpec((s, LANES), lambda hp, qg: (0, 4 + hp)),
                  pl.BlockSpec((s, LANES), lambda hp, qg: (0, 8 + hp)), blk, blk],
        out_specs=(blk, col, col),
        scratch_shapes=[pltpu.VMEM((s, LANES), F32), pltpu.VMEM((s, LANES), F32)],
        args=(nblk, qkv, qkv, qkv, lsum, dya), sem=("parallel", "arbitrary"))


def _bwd_dproj(dmisc, dconv, pc, dq, dk, dv, x, dx2, g, cw, layer, win_full, name, comm=None, h=None,
               h_rows=None):
    s = x.shape[0]
    ts = min(ROW_TILE, s)
    blk8 = ts // SUBLANES
    last8 = s // SUBLANES - 1
    fused = h is not None
    emit_dproj = not fused or h_rows is not None
    dw_rows, h_blk = (D_MODEL, 0) if h_rows is None else h_rows

    def body(*refs):
        (dcb_ref, dcz_ref, daz_ref, dconv_ref, nxt_ref, cc_ref, ch_ref, dq_ref, dk_ref, dv_ref,
         x_ref, dx2_ref, g_ref, cw_ref, w_ref) = refs[:15]
        rest = list(refs[15:])
        h_ref = rest.pop(0) if fused else None
        dproj_ref = rest.pop(0) if emit_dproj else None
        dx_ref, dg_ref = rest.pop(0), rest.pop(0)
        dw_ref = rest.pop(0) if fused else None
        dproj_ref = dproj_ref if emit_dproj else rest.pop(0)
        acc_ref = rest.pop(0) if fused else None
        i = pl.program_id(0)

        @pl.when(i == 0)
        def _():
            dg_ref[...] = jnp.zeros_like(dg_ref)
            if fused:
                acc_ref[...] = jnp.zeros_like(acc_ref)

        keep = jnp.where(i == pl.num_programs(0) - 1, 0.0, 1.0)
        dc = dconv_ref[...]
        n0 = nxt_ref[0:1, :] * keep
        n1 = nxt_ref[1:2, :] * keep
        rowi = lax.broadcasted_iota(jnp.int32, dc.shape, 0)
        dc1 = jnp.where(rowi == ts - 1, n0, pltpu.roll(dc, ts - 1, 0))
        dc2 = jnp.where(rowi == ts - 2, n0, jnp.where(rowi == ts - 1, n1, pltpu.roll(dc, ts - 2, 0)))
        du = cw_ref[2:3, :] * dc + cw_ref[1:2, :] * dc1 + cw_ref[0:1, :] * dc2
        dproj_ref[:, 0:512] = dcb_ref[...]
        dproj_ref[:, 512:1024] = (du * ch_ref[...].astype(F32)).astype(BF16)
        dproj_ref[:, 1024:1536] = (du * cc_ref[...].astype(F32)).astype(BF16)
        dproj_ref[:, 1536:2048] = dcz_ref[...]
        dproj_ref[:, 2048:2560] = dq_ref[...]
        dproj_ref[:, 2560:3072] = dk_ref[...]
        dproj_ref[:, 3072:3584] = dv_ref[...]
        dproj_ref[:, 3584:4096] = daz_ref[...]
        dh = lax.dot_general(dproj_ref[...], w_ref[...], NT, preferred_element_type=F32)
        if fused:
            acc_ref[...] += lax.dot_general(h_ref[...], dproj_ref[...], TN, preferred_element_type=F32)
        x = x_ref[...]
        r = lax.rsqrt(jnp.mean(x * x, axis=-1, keepdims=True) + EPS)
        xn = x * r
        dg_ref[...] += jnp.sum(dh * xn, axis=0, keepdims=True)
        dxn = dh * g_ref[...]
        dx_ref[...] = dx2_ref[...] + r * (dxn - xn * jnp.mean(dxn * xn, axis=-1, keepdims=True))
        if fused:
            @pl.when(i == pl.num_programs(0) - 1)
            def _():
                dw_ref[...] = acc_ref[...].astype(BF16)

    row = lambda width, cb_=0: pl.BlockSpec((ts, width), lambda i: (i, cb_))
    nxt = pl.BlockSpec((SUBLANES, 512), lambda i: (jnp.minimum((i + 1) * blk8, last8), 0))
    vec = lambda width: pl.BlockSpec((1, width), lambda i: (0, 0))
    lvec = lambda width: _layer_rows(layer, 1, width)
    once = dict(pipeline_mode=pl.Buffered(1)) if fused else {}
    whole = lambda rows_: pl.BlockSpec((rows_, N_IN), lambda i: (0, 0), **once)
    in_specs = [row(512, 0), row(512, 1), row(512, 2), row(512), nxt, row(512, 1), row(512, 2),
                row(512), row(512), row(512), row(D_MODEL), row(D_MODEL), lvec(D_MODEL),
                _layer_rows(layer, 3, 512), whole(D_MODEL)]
    args = [dmisc, dmisc, dmisc, dconv, dconv, pc, pc, dq, dk, dv, x, dx2, g, cw, win_full]
    out_shape = [jax.ShapeDtypeStruct((s, D_MODEL), F32), jax.ShapeDtypeStruct((1, D_MODEL), F32)]
    out_specs = [row(D_MODEL), vec(D_MODEL)]
    scratch = []
    if emit_dproj:
        out_shape.insert(0, jax.ShapeDtypeStruct((s, N_IN), BF16))
        out_specs.insert(0, row(N_IN))
    else:
        scratch.append(pltpu.VMEM((ts, N_IN), BF16))
    if fused:
        in_specs.append(row(dw_rows, h_blk))
        args.append(h)
        out_shape.append(jax.ShapeDtypeStruct((dw_rows, N_IN), BF16))
        out_specs.append(whole(dw_rows))
        scratch.append(pltpu.VMEM((dw_rows, N_IN), F32))
    return _hosted_call(
        body, comm, name=name, grid=(s // ts,), out_shape=tuple(out_shape), in_specs=in_specs,
        out_specs=tuple(out_specs), scratch_shapes=scratch, args=tuple(args), sem=("arbitrary",))


def _atb(a, b, name, a_cols=None, comm=None):
    s, n = b.shape
    m, a_blk = (a.shape[-1], 0) if a_cols is None else a_cols
    ts = min(512, s)
    tn = min(2048, n)
    a_spec = pl.BlockSpec((ts, m), lambda j, i: (i, a_blk))

    def body(a_ref, b_ref, o_ref, acc_ref):
        i = pl.program_id(1)

        @pl.when(i == 0)
        def _():
            acc_ref[...] = jnp.zeros_like(acc_ref)

        acc_ref[...] += lax.dot_general(a_ref[...].astype(BF16), b_ref[...], TN, preferred_element_type=F32)

        @pl.when(i == pl.num_programs(1) - 1)
        def _():
            o_ref[...] = acc_ref[...].astype(BF16)

    (out,), got = _hosted_call(
        body, comm, name=name, grid=(n // tn, s // ts),
        out_shape=(jax.ShapeDtypeStruct((m, n), BF16),),
        in_specs=[a_spec, pl.BlockSpec((ts, tn), lambda j, i: (i, j))],
        out_specs=(pl.BlockSpec((m, tn), lambda j, i: (0, j)),),
        scratch_shapes=[pltpu.VMEM((m, tn), F32)],
        args=(a, b), sem=("parallel", "arbitrary"))
    return out, got


def _adamw_math(w, g, m, v):
    m2 = ADAM_B1 * m + (1.0 - ADAM_B1) * g
    v2 = ADAM_B2 * v + (1.0 - ADAM_B2) * (g * g)
    m_hat = m2 / (1.0 - ADAM_B1 ** ADAM_STEP)
    v_hat = v2 / (1.0 - ADAM_B2 ** ADAM_STEP)
    delta = -ADAM_LR * (m_hat / (jnp.sqrt(v_hat) + ADAM_EPS) + ADAM_WD * w)
    return delta, m2, v2


def _adamw_sum8(pieces, w, m, v, name):
    _, rows, cols = w.shape
    tr = min([rows, 256] + [pc_[0].shape[1] for pc_ in pieces])
    n_tiles = rows // tr
    n_p = len(pieces)
    spans = [(layer, row0 // tr, arr.shape[1] // tr) for arr, layer, row0 in pieces]

    def body(*refs):
        p_refs = refs[:n_p]
        w_ref, m_ref, v_ref, g_ref, d_ref, m2_ref, v2_ref = refs[n_p:]
        l, i = pl.program_id(0), pl.program_id(1)

        def run(p_ref):
            g = p_ref[0].astype(F32)
            for d in range(1, N_DEV):
                g = g + p_ref[d].astype(F32)
            g_ref[...] = g
            d_ref[...], m2_ref[...], v2_ref[...] = _adamw_math(w_ref[...], g, m_ref[...], v_ref[...])

        for p_ref, (layer, t0, nt) in zip(p_refs, spans):
            mine = jnp.logical_and(l == layer, jnp.logical_and(i >= t0, i < t0 + nt))
            pl.when(mine)(lambda p_ref=p_ref: run(p_ref))

    def piece_spec(layer, t0, nt):
        return pl.BlockSpec((N_DEV, tr, cols),
                            lambda l, i: (0, jnp.clip(jnp.where(l == layer, i - t0, jnp.where(l < layer, 0, nt - 1)),
                                                      0, nt - 1), 0))

    tile = pl.BlockSpec((None, tr, cols), lambda l, i: (l, i, 0))
    o = jax.ShapeDtypeStruct((DEPTH, rows, cols), F32)
    return _call(
        body, name=name, grid=(DEPTH, n_tiles),
        out_shape=(o, o, o, o),
        in_specs=[*[piece_spec(*sp) for sp in spans], tile, tile, tile],
        out_specs=(tile, tile, tile, tile),
        compiler_params=_params(("arbitrary", "arbitrary"), VMEM_LIMIT),
    )(*[pc_[0] for pc_ in pieces], w, m, v)


def _small_update(blk, layered, final, conv, loss_parts):
    n_l = len(layered)
    ins = [a for item in layered for a in item] + list(final) + list(conv) + [loss_parts]
    shapes = [item[2].shape for item in layered] + [final[1].shape, conv[2].shape]
    out_shape = [jax.ShapeDtypeStruct(sh, F32) for sh in shapes for _ in range(4)]
    out_shape.append(jax.ShapeDtypeStruct((1, LANES), F32))

    def body(*refs):
        blk_ref, refs = refs[0], refs[1:]
        in_refs, out_refs, pick_ref = refs[:len(ins)], refs[len(ins):-1], refs[-1]

        def total(ref):
            g = ref[0]
            for d in range(1, N_DEV):
                g = g + ref[d]
            return g

        def update(k, at, g, w_ref, m_ref, v_ref):
            g_ref, d_ref, m2_ref, v2_ref = out_refs[4 * k:4 * k + 4]
            g_ref[at] = g
            d_ref[at], m2_ref[at], v2_ref[at] = _adamw_math(w_ref[at], g, m_ref[at], v_ref[at])

        for k in range(n_l):
            p0, p1, w_ref, m_ref, v_ref = in_refs[5 * k:5 * k + 5]
            for layer, parts in enumerate((p0, p1)):
                update(k, pl.ds(layer, 1), total(parts), w_ref, m_ref, v_ref)
        pf, w_ref, m_ref, v_ref = in_refs[5 * n_l:5 * n_l + 4]
        update(n_l, pl.ds(0, 1), total(pf), w_ref, m_ref, v_ref)
        c0, c1, w_ref, m_ref, v_ref = in_refs[5 * n_l + 4:5 * n_l + 9]
        for layer, parts in enumerate((c0, c1)):
            g8 = total(parts)
            mine = jnp.zeros((SUBLANES, HEAD_DIM), F32)
            for j in range(N_DEV):
                mine = mine + jnp.where(blk_ref[0] == j, g8[:, HEAD_DIM * j:HEAD_DIM * (j + 1)], 0.0)
            pick_ref[...] = mine
            update(n_l + 1, layer, pick_ref[0:3, :], w_ref, m_ref, v_ref)
        out_refs[-1][...] = total(in_refs[-1])

    whole = lambda shape: pl.BlockSpec(shape, lambda: (0,) * len(shape))
    outs = _call(
        body, name="adamw_small",
        out_shape=tuple(out_shape),
        in_specs=[pl.BlockSpec(memory_space=pltpu.SMEM)] + [whole(a.shape) for a in ins],
        out_specs=tuple(whole(o.shape) for o in out_shape),
        scratch_shapes=[pltpu.VMEM((SUBLANES, HEAD_DIM), F32)],
    )(blk, *ins)
    return [outs[4 * k:4 * k + 4] for k in range(n_l + 2)], outs[-1]


def kernel(x, p, norm_g, w_in, conv_w, conv_b, branch_g, w_out, ple_norm_g, w_pg, b_pg, w_pe, final_g, loss_target, m_norm_g, m_w_in, m_conv_w, m_conv_b, m_branch_g, m_w_out, m_ple_norm_g, m_w_pg, m_b_pg, m_w_pe, m_final_g, v_norm_g, v_w_in, v_conv_w, v_conv_b, v_branch_g, v_w_out, v_ple_norm_g, v_w_pg, v_b_pg, v_w_pe, v_final_g):
    s = x.shape[1]
    x0 = x.reshape(s, D_MODEL)
    target = loss_target.reshape(s, D_MODEL)
    me_blk = _my_block()

    win_s, wout_s, wpg_s, wpe_s = _cast_bf16(
        [w_in.reshape(DEPTH * D_MODEL, 512), w_out.reshape(DEPTH * 128, D_MODEL),
         w_pg.reshape(DEPTH * 128, D_MODEL), w_pe.reshape(DEPTH * PLE_DIM, 128)], "cast_weights")
    win_s, wout_s = win_s.reshape(DEPTH, D_MODEL, 512), wout_s.reshape(DEPTH, 128, D_MODEL)
    wpg_s, wpe_s = wpg_s.reshape(DEPTH, 128, D_MODEL), wpe_s.reshape(DEPTH, PLE_DIM, 128)
    cw_s = jnp.zeros((SUBLANES, LANES), F32).at[:DEPTH * 3, :HEAD_DIM].set(conv_w.reshape(DEPTH * 3, HEAD_DIM))
    bf = lambda r_, c_: jax.ShapeDtypeStruct((r_, c_), BF16)
    w_items = lambda l: [(wout_s[l], bf(D_MODEL, D_MODEL), "rows128"), (wpg_s[l], bf(D_MODEL, D_MODEL), "rows128"),
                         (wpe_s[l], bf(PLE_DIM, D_MODEL), "cols128")]
    win_f = [None] * DEPTH
    win_f[0], cw_all = _comm_call(_gather_comm([
        (win_s[0], bf(D_MODEL, N_IN), "cols512"),
        (cw_s, jax.ShapeDtypeStruct((N_DEV, SUBLANES, LANES), F32), "slot")]), "gather_w_in_0")
    cw_full = jnp.transpose(cw_all[:, :DEPTH * 3, :HEAD_DIM].reshape(N_DEV, DEPTH, 3, HEAD_DIM), (1, 2, 0, 3))
    cw_full = cw_full.reshape(DEPTH, 3, D_CONV)
    gather_rest_0 = _gather_comm(w_items(0))
    gather_win_1 = _gather_comm([(win_s[1], bf(D_MODEL, N_IN), "cols512")])
    gather_rest_1 = _gather_comm(w_items(1))

    norm3, convb3, branch3, ple3, bpg3 = [a.reshape(DEPTH, 1, -1) for a in (norm_g, conv_b, branch_g, ple_norm_g, b_pg)]

    saved = []
    xl = x0
    wout_f, wpg_f, wpe_f = [None] * DEPTH, [None] * DEPTH, [None] * DEPTH
    for l in range(DEPTH):
        (h, pc, qkv, az), got = _fwd_in(xl, norm3, l, win_f[l], f"fwd_in_{l}",
                                        comm=gather_rest_0 if l == 0 else None)
        if l == 0:
            wout_f[0], wpg_f[0], wpe_f[0] = got
        (ya, lsum, nblk), got = _attn_fwd(qkv, f"attn_fwd_{l}", comm=gather_win_1 if l == 0 else None)
        if l == 0:
            (win_f[1],) = got
        last = l == DEPTH - 1
        outs, got = _fwd_mid(
            xl, pc, az, ya, p, l, cw_full, convb3, branch3, wout_f[l],
            ple3, wpg_f[l], bpg3, wpe_f[l], f"fwd_mid_{l}",
            comm=gather_rest_1 if l == 0 else None, head=(target, final_g[None, :]) if last else None)
        x2, x3, gated, h2, gate, e = outs[:6]
        if l == 0:
            wout_f[1], wpg_f[1], wpe_f[1] = got
        saved.append(dict(x=xl, h=h, pc=pc, qkv=qkv, az=az, ya=ya, lsum=lsum, nblk=nblk, x2=x2, gated=gated, h2=h2,
                          gate=gate, e=e))
        xl = x3

    dx, (loss_acc, d_final_g) = xl, outs[6:]

    dwin, dwout, dwpg, dwpe = [None] * DEPTH, [None] * DEPTH, [None] * DEPTH, [None] * DEPTH
    small = dict(norm_g=[None] * DEPTH, conv_b=[None] * DEPTH, branch_g=[None] * DEPTH,
                 ple_norm_g=[None] * DEPTH, b_pg=[None] * DEPTH, conv_w=[None] * DEPTH)
    slot = lambda r_, c_: jax.ShapeDtypeStruct((r_, c_), BF16)
    half = D_MODEL // 2
    r_in1, r_out, r_pg, r_pe = None, [None] * DEPTH, [None] * DEPTH, [None] * DEPTH

    def rest_items(l):
        return [(dwout[l], slot(128, D_MODEL), "rows128"), (dwpg[l], slot(128, D_MODEL), "rows128"),
                (dwpe[l], slot(PLE_DIM, 128), "cols128")]

    for l in reversed(range(DEPTH)):
        sv = saved[l]
        ride = _exchange_comm(rest_items(1)) if l == 0 else None
        (dx2, dya, dmisc, dconv, dwout[l], dwpg[l], dwpe[l], d_bpg, d_pg, d_bg, d_cbias, d_cw), got = _bwd_mid(
            dx, sv["x2"], sv["gate"], sv["e"], sv["pc"], sv["az"], sv["ya"], sv["gated"], sv["h2"], p, l,
            cw_full, convb3, branch3, ple3, wpg_f[l], wout_f[l], f"bwd_mid_{l}",
            comm=ride)
        if l == 0:
            r_out[1], r_pg[1], r_pe[1] = got
        ride = _exchange_comm([(dwin[1], slot(D_MODEL, 512), "cols512")] + rest_items(0)) if l == 0 else None
        (dq, dk, dv), got = _attn_bwd(sv["qkv"], sv["lsum"], sv["nblk"], dya, f"attn_bwd_{l}", comm=ride)
        if l == 0:
            r_in1, r_out[0], r_pg[0], r_pe[0] = got
        dproj_args = (dmisc, dconv, sv["pc"], dq, dk, dv, sv["x"], dx2, norm3, cw_full, l, win_f[l])
        if l == 1:
            (dx, d_ng, dwin[1]), _ = _bwd_dproj(*dproj_args, "bwd_dproj_dw_1", h=sv["h"])
        else:
            (dproj, dx, d_ng, dwin_top), _ = _bwd_dproj(*dproj_args, "bwd_dproj_dw_0", h=sv["h"], h_rows=(half, 0))
            dwin_bot, (r_in0_top,) = _atb(sv["h"], dproj, "dw_in_0_bottom", a_cols=(half, 1),
                                          comm=_exchange_comm([(dwin_top, slot(half, 512), "cols512")]))
        small["norm_g"][l], small["conv_b"][l], small["branch_g"][l] = d_ng, d_cbias, d_bg
        small["ple_norm_g"][l], small["b_pg"][l], small["conv_w"][l] = d_pg, d_bpg, d_cw
    grad_x = dx.reshape(1, s, D_MODEL)

    names = ["norm_g", "conv_b", "branch_g", "ple_norm_g", "b_pg", "conv_w"]
    small_list = [small[n][l] for n in names for l in range(DEPTH)] + [d_final_g, loss_acc]
    got = _comm_call(_exchange_comm(
        [(dwin_bot, slot(half, 512), "cols512")]
        + [(a, jax.ShapeDtypeStruct(a.shape, F32), "slot") for a in small_list]), "exchange_last")
    r_in0_bot, r_small = got[0], got[1:]

    per_layer = lambda r: [(r[0], 0, 0), (r[1], 1, 0)]
    g_win, d_win, m_win, v_win = _adamw_sum8([(r_in0_top, 0, 0), (r_in0_bot, 0, half), (r_in1, 1, 0)],
                                             w_in, m_w_in, v_w_in, "adamw_w_in")
    g_wout, d_wout, m_wout, v_wout = _adamw_sum8(per_layer(r_out), w_out, m_w_out, v_w_out, "adamw_w_out")
    g_wpg, d_wpg, m_wpg, v_wpg = _adamw_sum8(per_layer(r_pg), w_pg, m_w_pg, v_w_pg, "adamw_w_pg")
    g_wpe, d_wpe, m_wpe, v_wpe = _adamw_sum8(per_layer(r_pe), w_pe, m_w_pe, v_w_pe, "adamw_w_pe")

    layered = [(norm_g, m_norm_g, v_norm_g), (conv_b, m_conv_b, v_conv_b), (branch_g, m_branch_g, v_branch_g),
               (ple_norm_g, m_ple_norm_g, v_ple_norm_g), (b_pg, m_b_pg, v_b_pg)]
    row = lambda a: a.reshape(1, -1)
    upd, loss_row = _small_update(
        jnp.reshape(me_blk, (1,)).astype(jnp.int32),
        [(r_small[2 * k], r_small[2 * k + 1], *wmv) for k, wmv in enumerate(layered)],
        (r_small[12], row(final_g), row(m_final_g), row(v_final_g)),
        (r_small[10], r_small[11], conv_w, m_conv_w, v_conv_w), r_small[13])
    loss = loss_row[0, 0]
    upd[5] = [a.reshape(-1) for a in upd[5]]

    big = {1: (g_win, d_win, m_win, v_win), 5: (g_wout, d_wout, m_wout, v_wout), 7: (g_wpg, d_wpg, m_wpg, v_wpg),
           9: (g_wpe, d_wpe, m_wpe, v_wpe)}
    small_at = {0: 0, 2: 6, 3: 1, 4: 2, 6: 3, 8: 4, 10: 5}
    per_kind = [[(big[i] if i in big else upd[small_at[i]])[j] for i in range(11)] for j in range(4)]
    return (loss, grad_x, *per_kind[0], *per_kind[1], *per_kind[2], *per_kind[3])
```

```python
import jax
import jax.numpy as jnp
from jax import lax
from jax.experimental import pallas as pl
from jax.experimental.pallas import tpu as pltpu

F32 = jnp.float32
BF16 = jnp.bfloat16

D_MODEL = 1024
D_CONV = 512
D_SB = 512
N_IN = 4096
HEAD_DIM = 64
PLE_DIM = 256
DEPTH = 2
EPS = 1e-6
ADAM_LR = 0.001
ADAM_B1 = 0.9
ADAM_B2 = 0.999
ADAM_EPS = 1e-08
ADAM_WD = 0.01
ADAM_STEP = 10

LANES = 128
SUBLANES = 8
VMEM_BYTES_V7X = 64 * 1024 * 1024
VMEM_LIMIT = VMEM_BYTES_V7X - 8 * 1024 * 1024

N_DEV = 8
ROW_TILE = 256
FWD_ROW_TILE = 512
ATTN_TILE = 256

NT = (((1,), (1,)), ((), ()))
TN = (((0,), (0,)), ((), ()))


def _call(body, **kw):
    return pl.pallas_call(body, **kw)


def _params(sem=None, vmem=None):
    return pltpu.CompilerParams(dimension_semantics=sem, vmem_limit_bytes=vmem)


def _sigmoid(z):
    return 0.5 * jnp.tanh(0.5 * z) + 0.5


def _group_bcast_sum(a, lo):
    s_lo = jnp.sum(jnp.where(lo, a, 0.0), axis=-1, keepdims=True)
    s_hi = jnp.sum(jnp.where(lo, 0.0, a), axis=-1, keepdims=True)
    return jnp.where(lo, s_lo, s_hi)


def _layer_rows(layer, rows, width):
    return pl.BlockSpec((None, rows, width), lambda i: (layer, 0, 0))


def _my_block():
    return 4 * lax.axis_index("x") + 2 * lax.axis_index("y") + lax.axis_index("c")


def _cast_bf16(arrays, name):
    n = len(arrays)

    def body(*refs):
        for a_ref, o_ref in zip(refs[:n], refs[n:]):
            o_ref[...] = a_ref[...].astype(BF16)

    whole = lambda a: pl.BlockSpec(a.shape, lambda: (0, 0))
    return _call(
        body, name=name,
        out_shape=tuple(jax.ShapeDtypeStruct(a.shape, BF16) for a in arrays),
        in_specs=[whole(a) for a in arrays], out_specs=tuple(whole(a) for a in arrays),
        compiler_params=_params(None, VMEM_LIMIT),
    )(*arrays)


class _Comm:
    def __init__(self, inputs, out_shapes, scratch, begin, middle, finish):
        self.inputs, self.out_shapes, self.scratch = list(inputs), list(out_shapes), list(scratch)
        self.begin, self.middle, self.finish = begin, middle, finish


def _slab(kind, ref, blk):
    if kind == "cols512":
        return ref.at[:, pl.ds(blk * 512, 512)]
    if kind == "rows128":
        return ref.at[pl.ds(blk * 128, 128), :]
    if kind == "cols128":
        return ref.at[:, pl.ds(blk * 128, 128)]
    return ref.at[blk]


def _gather_comm(items):
    n_t = len(items)
    kinds = [it[2] for it in items]

    def ctx(ins, outs, sems):
        send_sems, recv_sems, local_sems = sems
        x, y, c = lax.axis_index("x"), lax.axis_index("y"), lax.axis_index("c")
        me, sibling = (x, y, c), (x, y, 1 - c)
        chips = [(1 - x, y), (x, 1 - y), (1 - x, 1 - y)]

        def place(t, dev):
            return _slab(kinds[t], outs[t], 4 * dev[0] + 2 * dev[1] + dev[2])

        def copy(t, k, block, to, own=False):
            return pltpu.make_async_remote_copy(
                src_ref=ins[t] if own else place(t, block), dst_ref=place(t, block),
                send_sem=send_sems.at[t, k], recv_sem=recv_sems.at[t, k],
                device_id=to, device_id_type=pl.DeviceIdType.MESH)

        mine = [pltpu.make_async_copy(ins[t], place(t, me), local_sems.at[t]) for t in range(n_t)]
        first = []
        for t in range(n_t):
            first.append(copy(t, 0, me, sibling, own=True))
            first += [copy(t, 1 + j, me, (*chip, c), own=True) for j, chip in enumerate(chips)]
        passed = [copy(t, 4 + j, (*chip, c), sibling) for j, chip in enumerate(chips) for t in range(n_t)]
        landed = [copy(t, 1 + j, (*chip, c), me) for j, chip in enumerate(chips) for t in range(n_t)]
        from_sibling = []
        for t in range(n_t):
            from_sibling.append(copy(t, 0, sibling, me))
            from_sibling += [copy(t, 4 + j, (*chip, 1 - c), me) for j, chip in enumerate(chips)]
        return mine, first, landed, passed, from_sibling

    def begin(ins, outs, sems):
        mine, first, _, _, _ = ctx(ins, outs, sems)
        for cp in mine + first:
            cp.start()

    def middle(ins, outs, sems):
        _, _, landed, passed, _ = ctx(ins, outs, sems)
        for got, fwd in zip(landed, passed):
            got.wait_recv()
            fwd.start()

    def finish(ins, outs, sems):
        mine, first, _, passed, from_sibling = ctx(ins, outs, sems)
        for cp in from_sibling:
            cp.wait_recv()
        for cp in first + passed:
            cp.wait_send()
        for cp in mine:
            cp.wait()

    scratch = [pltpu.SemaphoreType.DMA((n_t, 7)), pltpu.SemaphoreType.DMA((n_t, 7)), pltpu.SemaphoreType.DMA((n_t,))]
    return _Comm([it[0] for it in items], [it[1] for it in items], scratch, begin, middle, finish)


def _exchange_comm(items):
    n_t = len(items)
    kinds = [it[2] for it in items]

    def ctx(ins, outs, sems):
        send_sems, recv_sems, local_sems = sems
        x, y, c = lax.axis_index("x"), lax.axis_index("y"), lax.axis_index("c")
        me_blk = 4 * x + 2 * y + c

        def src(t, blk):
            return ins[t] if kinds[t] == "slot" else _slab(kinds[t], ins[t], blk)

        local = [pltpu.make_async_copy(src(t, me_blk), outs[t].at[me_blk], local_sems.at[t]) for t in range(n_t)]
        remote = []
        for k in range(1, N_DEV):
            px = 1 - x if k & 4 else x
            py = 1 - y if k & 2 else y
            pc_ = 1 - c if k & 1 else c
            for t in range(n_t):
                remote.append(pltpu.make_async_remote_copy(
                    src_ref=src(t, 4 * px + 2 * py + pc_), dst_ref=outs[t].at[me_blk],
                    send_sem=send_sems.at[k - 1, t], recv_sem=recv_sems.at[k - 1, t],
                    device_id=(px, py, pc_), device_id_type=pl.DeviceIdType.MESH))
        return local, remote

    def begin(ins, outs, sems):
        local, remote = ctx(ins, outs, sems)
        for cp in local + remote:
            cp.start()

    def finish(ins, outs, sems):
        local, remote = ctx(ins, outs, sems)
        for cp in remote:
            cp.wait_recv()
        for cp in remote:
            cp.wait_send()
        for cp in local:
            cp.wait()

    scratch = [pltpu.SemaphoreType.DMA((N_DEV - 1, n_t)), pltpu.SemaphoreType.DMA((N_DEV - 1, n_t)),
               pltpu.SemaphoreType.DMA((n_t,))]
    out_shapes = [jax.ShapeDtypeStruct((N_DEV, *it[1].shape), it[1].dtype) for it in items]
    return _Comm([it[0] for it in items], out_shapes, scratch, begin, None, finish)


def _comm_call(comm, name):
    n_in, n_out = len(comm.inputs), len(comm.out_shapes)

    def body(*refs):
        ins, outs, sems = refs[:n_in], refs[n_in:n_in + n_out], refs[n_in + n_out:]
        comm.begin(ins, outs, sems)
        if comm.middle is not None:
            comm.middle(ins, outs, sems)
        comm.finish(ins, outs, sems)

    any_spec = pl.BlockSpec(memory_space=pl.ANY)
    return _call(body, name=name, out_shape=tuple(comm.out_shapes), in_specs=[any_spec] * n_in,
                 out_specs=[any_spec] * n_out, scratch_shapes=comm.scratch)(*comm.inputs)


def _hosted(body, n_in, n_out, comm, first, last, middle):
    if comm is None:
        return lambda *refs: body(*refs)
    n_ci, n_co, n_cs = len(comm.inputs), len(comm.out_shapes), len(comm.scratch)

    def wrapped(*refs):
        ins, cin = refs[:n_in], refs[n_in:n_in + n_ci]
        o0 = n_in + n_ci
        outs, cout = refs[o0:o0 + n_out], refs[o0 + n_out:o0 + n_out + n_co]
        scr, csem = refs[o0 + n_out + n_co:len(refs) - n_cs], refs[len(refs) - n_cs:]
        pl.when(first())(lambda: comm.begin(cin, cout, csem))
        body(*ins, *outs, *scr)
        if comm.middle is not None:
            pl.when(middle())(lambda: comm.middle(cin, cout, csem))
        pl.when(last())(lambda: comm.finish(cin, cout, csem))

    return wrapped


def _hosted_call(body, comm, *, name, grid, out_shape, in_specs, out_specs, args, scratch_shapes=(), sem=None):
    nd = len(grid)
    first, last, middle = _at_first(nd), _at_last(nd), _at_middle(nd)
    if comm is not None:
        sem = ("arbitrary",) * nd
    n_in, n_out = len(in_specs), len(out_shape)
    any_spec = pl.BlockSpec(memory_space=pl.ANY)
    c_in = [] if comm is None else comm.inputs
    c_out = [] if comm is None else comm.out_shapes
    c_scr = [] if comm is None else comm.scratch
    outs = _call(
        _hosted(body, n_in, n_out, comm, first, last, middle), name=name, grid=grid,
        out_shape=(*out_shape, *c_out),
        in_specs=[*in_specs, *[any_spec] * len(c_in)],
        out_specs=(*out_specs, *[any_spec] * len(c_out)),
        scratch_shapes=[*scratch_shapes, *c_scr],
        compiler_params=_params(sem, VMEM_LIMIT),
    )(*args, *c_in)
    return outs[:n_out], outs[n_out:]


def _grid_step(ndim):
    i, n = pl.program_id(0), pl.num_programs(0)
    for d in range(1, ndim):
        i, n = i * pl.num_programs(d) + pl.program_id(d), n * pl.num_programs(d)
    return i, n


def _at_first(ndim):
    return lambda: _grid_step(ndim)[0] == 0


def _at_last(ndim):
    def pred():
        i, n = _grid_step(ndim)
        return i == n - 1
    return pred


def _at_middle(ndim):
    def pred():
        i, n = _grid_step(ndim)
        return i == (3 * n) // 4
    return pred


def _fwd_in(x, g, layer, w_full, name, comm=None):
    s = x.shape[0]
    ts = min(FWD_ROW_TILE, s)

    def body(x_ref, g_ref, w_ref, h_ref, pc_ref, qkv_ref, az_ref):
        xf = x_ref[...]
        r = lax.rsqrt(jnp.mean(xf * xf, axis=-1, keepdims=True) + EPS)
        h = (xf * r * g_ref[...]).astype(BF16)
        h_ref[...] = h
        pc_ref[...] = jnp.dot(h, w_ref[:, 0:2048], preferred_element_type=F32).astype(BF16)
        q = jnp.dot(h, w_ref[:, 2048:2560], preferred_element_type=F32)
        qkv_ref[:, 0:512] = (q * 0.125).astype(BF16)
        qkv_ref[:, 512:1536] = jnp.dot(h, w_ref[:, 2560:3584], preferred_element_type=F32).astype(BF16)
        az_ref[...] = jnp.dot(h, w_ref[:, 3584:4096], preferred_element_type=F32).astype(BF16)

    row = lambda width: pl.BlockSpec((ts, width), lambda i: (i, 0))
    return _hosted_call(
        body, comm, name=name, grid=(s // ts,),
        out_shape=(jax.ShapeDtypeStruct((s, D_MODEL), BF16), jax.ShapeDtypeStruct((s, 2048), BF16),
                   jax.ShapeDtypeStruct((s, 1536), BF16), jax.ShapeDtypeStruct((s, 512), BF16)),
        in_specs=[row(D_MODEL), _layer_rows(layer, 1, D_MODEL),
                  pl.BlockSpec((D_MODEL, N_IN), lambda i: (0, 0))],
        out_specs=(row(D_MODEL), row(2048), row(1536), row(512)),
        args=(x, g, w_full), sem=("parallel",))


ATTN_ROWS = 128
ATTN_DONE = 104.0


def _attn_pieces(tq, rc):
    lane = lax.broadcasted_iota(jnp.int32, (1, LANES), 1)
    lo = lane < HEAD_DIM
    row = lax.broadcasted_iota(jnp.int32, (tq, tq), 0)
    col = lax.broadcasted_iota(jnp.int32, (tq, tq), 1)
    tri_gt = jnp.where(row > col, 1.0, 0.0).astype(BF16)
    tri_le = jnp.where(row <= col, 1.0, 0.0).astype(BF16)
    rrow = lax.broadcasted_iota(jnp.int32, (rc, tq), 0)
    rcol = lax.broadcasted_iota(jnp.int32, (rc, tq), 1)
    causal = [rcol < rrow + r * rc for r in range(tq // rc)]
    return lo, causal, tri_gt, tri_le


def _split_heads(a, lo):
    z = jnp.zeros_like(a)
    return (jnp.where(lo, a, z), jnp.where(lo, z, a))


def _softplus(z, causal, diag):
    neg_abs = lax.bitcast_convert_type(lax.bitcast_convert_type(z, jnp.uint32) | jnp.uint32(0x80000000), F32)
    sp = jnp.maximum(z, 0.0) + jnp.log(1.0 + jnp.exp(neg_abs))
    if diag:
        sp = jnp.where(causal, sp, 0.0)
    return sp


def _attn_fwd(qkv, name, comm=None):
    s = qkv.shape[0]
    tq = min(ATTN_TILE, s)
    nq = s // tq
    rc = min(ATTN_ROWS, tq)
    n_rc = tq // rc
    nt = 4 if nq % 4 == 0 else 2 if nq % 2 == 0 else 1
    chains = [(t, r, hh) for t in range(nt) for r in range(n_rc) for hh in range(2)]

    def body(q_ref, k_ref, v_ref, o_ref, lsum_ref, nblk_ref):
        hp = pl.program_id(0)
        qis = [pl.program_id(1) * nt + t for t in range(nt)]
        lo, causal, tri_gt, _ = _attn_pieces(tq, rc)
        qh = _split_heads(q_ref[...], lo)
        qc = {(t, r, hh): qh[hh][t * tq + r * rc:t * tq + (r + 1) * rc] for t, r, hh in chains}

        mm = lambda a_, b_: jnp.dot(a_.astype(BF16), b_, preferred_element_type=F32)
        rowsum = lambda a_: jnp.sum(a_, axis=-1, keepdims=True)

        def block(kb, carry, t):
            mine = [ch for ch in chains if ch[0] == t]
            start = pl.multiple_of(kb * tq, tq)
            k = k_ref[pl.ds(start, tq), :]
            vh = _split_heads(v_ref[pl.ds(start, tq), :], lo)
            z = {ch: lax.dot_general(qc[ch], k, NT, preferred_element_type=F32) for ch in mine}
            sp = {ch: _softplus(z[ch], None, False) for ch in mine}
            later = {ch: mm(sp[ch], tri_gt) for ch in mine}
            a = {ch: jnp.exp((z[ch] - sp[ch]) - (carry[ch[1]][1 + ch[2]] + later[ch])) for ch in mine}
            pv = {ch: mm(a[ch], vh[ch[2]]) for ch in mine}
            return tuple((carry[r][0] + pv[(t, r, 0)] + pv[(t, r, 1)], carry[r][1] + rowsum(sp[(t, r, 0)]),
                          carry[r][2] + rowsum(sp[(t, r, 1)])) for r in range(n_rc))

        def first_two():
            ok = [qi > 0 for qi in qis]
            d0 = [pl.multiple_of(qi * tq, tq) for qi in qis]
            p0 = [pl.multiple_of(jnp.maximum(qi - 1, 0) * tq, tq) for qi in qis]
            k_d = [k_ref[pl.ds(d0[t], tq), :] for t in range(nt)]
            k_p = [k_ref[pl.ds(p0[t], tq), :] for t in range(nt)]
            vh_d = [_split_heads(v_ref[pl.ds(d0[t], tq), :], lo) for t in range(nt)]
            vh_p = [_split_heads(v_ref[pl.ds(p0[t], tq), :], lo) for t in range(nt)]
            z_d = {ch: lax.dot_general(qc[ch], k_d[ch[0]], NT, preferred_element_type=F32) for ch in chains}
            z_p = {ch: lax.dot_general(qc[ch], k_p[ch[0]], NT, preferred_element_type=F32) for ch in chains}
            sp_d = {ch: _softplus(z_d[ch], causal[ch[1]], True) for ch in chains}
            sp_raw = {ch: _softplus(z_p[ch], None, False) for ch in chains}
            sp_p = {ch: jnp.where(ok[ch[0]], sp_raw[ch], 0.0) for ch in chains}
            later_d = {ch: mm(sp_d[ch], tri_gt) for ch in chains}
            later_p = {ch: mm(sp_p[ch], tri_gt) for ch in chains}
            c_d = {ch: rowsum(sp_d[ch]) for ch in chains}
            a_d = {ch: jnp.where(causal[ch[1]], jnp.exp((z_d[ch] - sp_d[ch]) - later_d[ch]), 0.0) for ch in chains}
            a_p = {ch: jnp.where(ok[ch[0]], jnp.exp((z_p[ch] - sp_raw[ch]) - (c_d[ch] + later_p[ch])), 0.0)
                   for ch in chains}
            pv = {ch: mm(a_d[ch], vh_d[ch[0]][ch[2]]) + mm(a_p[ch], vh_p[ch[0]][ch[2]]) for ch in chains}
            return [tuple((pv[(t, r, 0)] + pv[(t, r, 1)], c_d[(t, r, 0)] + rowsum(sp_p[(t, r, 0)]),
                           c_d[(t, r, 1)] + rowsum(sp_p[(t, r, 1)])) for r in range(n_rc)) for t in range(nt)]

        def least(carry):
            m = jnp.minimum(carry[0][1], carry[0][2])
            for r in range(1, n_rc):
                m = jnp.minimum(m, jnp.minimum(carry[r][1], carry[r][2]))
            return jnp.min(m)

        carries = first_two()
        for t, qi in enumerate(qis):
            def go_on(st, qi=qi):
                return jnp.logical_and(st[0] < qi - 1, st[1] < ATTN_DONE)

            def step(st, qi=qi, t=t):
                new = block(qi - 2 - st[0], st[2], t)
                return st[0] + 1, least(new), new

            walked, _, carry = lax.while_loop(go_on, step, (jnp.int32(0), least(carries[t]), carries[t]))
            for r in range(n_rc):
                rows = slice(t * tq + r * rc, t * tq + (r + 1) * rc)
                o_ref[rows, :] = carry[r][0].astype(BF16)
                lsum_ref[rows, :] = jnp.where(lo, carry[r][1], carry[r][2])
            nblk_ref[hp, qi] = walked.astype(F32)

    blk = pl.BlockSpec((nt * tq, LANES), lambda hp, qg: (qg, hp))
    o512 = jax.ShapeDtypeStruct((s, D_SB), F32)
    return _hosted_call(
        body, comm, name=name, grid=(4, nq // nt),
        out_shape=(jax.ShapeDtypeStruct((s, D_SB), BF16), o512, jax.ShapeDtypeStruct((4, nq), F32)),
        in_specs=[blk, pl.BlockSpec((s, LANES), lambda hp, qg: (0, 4 + hp)),
                  pl.BlockSpec((s, LANES), lambda hp, qg: (0, 8 + hp))],
        out_specs=(blk, blk, pl.BlockSpec(memory_space=pltpu.SMEM)),
        args=(qkv, qkv, qkv), sem=("arbitrary", "arbitrary"))


HALO = 16


def _conv_taps(cc_ref, ch_ref, ccp_ref, chp_ref, halo_ref, first):
    u = cc_ref[...].astype(F32) * ch_ref[...].astype(F32)
    halo_ref[...] = ccp_ref[...].astype(F32) * chp_ref[...].astype(F32) * jnp.where(first, 0.0, 1.0)
    p6 = halo_ref[HALO - 2:HALO - 1, :]
    p7 = halo_ref[HALO - 1:HALO, :]
    rowi = lax.broadcasted_iota(jnp.int32, u.shape, 0)
    u1 = jnp.where(rowi == 0, p7, pltpu.roll(u, 1, 0))
    u2 = jnp.where(rowi == 0, p6, jnp.where(rowi == 1, p7, pltpu.roll(u, 2, 0)))
    return u, u1, u2


def _fwd_mid(x, pc, az, ya, p4, layer, cw, cb, bg, wout_full, pg, wpg_full, bpg, wpe_full, name, comm=None,
             head=None):
    s = x.shape[0]
    ts = min(FWD_ROW_TILE, s)
    blk_h = ts // HALO
    n_in = 18 + (2 if head else 0)

    def body(*refs):
        (x_ref, cb_ref_, cc_ref, ch_ref, cz_ref, ccp_ref, chp_ref, az_ref, ya_ref, p_ref,
         cw_ref, cbias_ref, bg_ref, wout_ref, pg_ref, wpg_ref, bpg_ref, wpe_ref) = refs[:18]
        x2_ref, x3_ref, gated_ref, h2_ref, gate_ref, e_ref = refs[n_in:n_in + 6]
        halo_ref = refs[-1]
        i = pl.program_id(0)
        lane = lax.broadcasted_iota(jnp.int32, (1, LANES), 1)
        lo = lane < HEAD_DIM
        u, u1, u2 = _conv_taps(cc_ref, ch_ref, ccp_ref, chp_ref, halo_ref, i == 0)
        conv = cbias_ref[...] + cw_ref[0:1, :] * u2 + cw_ref[1:2, :] * u1 + cw_ref[2:3, :] * u
        yc = cb_ref_[...].astype(F32) * conv
        for sl in range(8):
            cols = slice(LANES * (sl % 4), LANES * (sl % 4 + 1))
            y = yc[:, cols] if sl < 4 else ya_ref[:, cols].astype(F32)
            zc = (cz_ref[:, cols] if sl < 4 else az_ref[:, cols]).astype(F32)
            rg = lax.rsqrt(_group_bcast_sum(y * y, lo) * (1.0 / HEAD_DIM) + EPS)
            yn = y * rg * bg_ref[:, LANES * sl:LANES * (sl + 1)]
            gated_ref[:, LANES * sl:LANES * (sl + 1)] = (yn * (zc * _sigmoid(zc))).astype(BF16)
        x2 = x_ref[...] + jnp.dot(gated_ref[...], wout_ref[...], preferred_element_type=F32)
        x2_ref[...] = x2
        r2 = lax.rsqrt(jnp.mean(x2 * x2, axis=-1, keepdims=True) + EPS)
        h2 = (x2 * r2 * pg_ref[...]).astype(BF16)
        h2_ref[...] = h2
        gate = _sigmoid(jnp.dot(h2, wpg_ref[...], preferred_element_type=F32) + bpg_ref[...])
        gate_ref[...] = gate.astype(BF16)
        e = jnp.dot(p_ref[...].astype(BF16), wpe_ref[...], preferred_element_type=F32)
        e_ref[...] = e.astype(BF16)
        x3 = x2 + gate * e
        if not head:
            x3_ref[...] = x3
            return
        t_ref, fg_ref = refs[18:20]
        loss_ref, dfg_ref = refs[n_in + 6:n_in + 8]
        dx, loss, dfg = _loss_math(x3, t_ref[...], fg_ref[...])

        @pl.when(i == 0)
        def _():
            loss_ref[...] = jnp.zeros_like(loss_ref)
            dfg_ref[...] = jnp.zeros_like(dfg_ref)

        x3_ref[...] = dx
        loss_ref[...] += loss
        dfg_ref[...] += dfg

    row = lambda width, cb_=0: pl.BlockSpec((ts, width), lambda i: (i, cb_))
    prev = lambda cb_: pl.BlockSpec((HALO, 512), lambda i: (jnp.maximum(i * blk_h - 1, 0), cb_))
    vec = lambda width: pl.BlockSpec((1, width), lambda i: (0, 0))
    lvec = lambda width: _layer_rows(layer, 1, width)
    wspec = lambda r_, c_: pl.BlockSpec((r_, c_), lambda i: (0, 0))
    f32o = jax.ShapeDtypeStruct((s, D_MODEL), F32)
    bfo = jax.ShapeDtypeStruct((s, D_MODEL), BF16)
    head_in = [row(D_MODEL), vec(D_MODEL)] if head else []
    head_out = [jax.ShapeDtypeStruct((1, LANES), F32), jax.ShapeDtypeStruct((1, D_MODEL), F32)] if head else []
    return _hosted_call(
        body, comm, name=name, grid=(s // ts,),
        out_shape=(f32o, f32o, bfo, bfo, bfo, bfo, *head_out),
        scratch_shapes=[pltpu.VMEM((HALO, 512), F32)],
        in_specs=[row(D_MODEL), row(512, 0), row(512, 1), row(512, 2), row(512, 3), prev(1), prev(2),
                  row(512), row(512),
                  pl.BlockSpec((None, None, ts, PLE_DIM), lambda i: (layer, 0, i, 0)),
                  _layer_rows(layer, 3, 512), lvec(512), lvec(D_MODEL),
                  wspec(D_MODEL, D_MODEL), lvec(D_MODEL), wspec(D_MODEL, D_MODEL), lvec(D_MODEL),
                  wspec(PLE_DIM, D_MODEL), *head_in],
        out_specs=(*[row(D_MODEL)] * 6, *([vec(LANES), vec(D_MODEL)] if head else [])),
        args=(x, pc, pc, pc, pc, pc, pc, az, ya, p4, cw, cb, bg, wout_full, pg, wpg_full, bpg, wpe_full,
              *(head or ())),
        sem=("arbitrary",) if head else ("parallel",))


def _loss_math(x, target, g):
    r = lax.rsqrt(jnp.mean(x * x, axis=-1, keepdims=True) + EPS)
    xn = x * r
    err = xn * g - target
    per_row = jnp.sum(err * err, axis=-1, keepdims=True)
    loss = jnp.sum(per_row, axis=0, keepdims=True) * (0.5 / D_MODEL)
    dy = err * (1.0 / D_MODEL)
    dg = jnp.sum(dy * xn, axis=0, keepdims=True)
    dxn = dy * g
    return r * (dxn - xn * jnp.mean(dxn * xn, axis=-1, keepdims=True)), loss, dg


def _bwd_mid(dx3, x2, gate, e, pc, az, ya, gated, h2, p4, layer, cw, cb, bg, pg, wpg_full, wout_full, name,
             comm=None):
    s = x2.shape[0]
    ts = min(ROW_TILE, s)
    blk_h = ts // HALO

    def body(dx3_ref, x2_ref, gate_ref, e_ref, cb_ref_, cc_ref, ch_ref, cz_ref, ccp_ref, chp_ref, az_ref, ya_ref,
             gated_ref, h2_ref, p_ref, cw_ref, cbias_ref, bg_ref, pg_ref, wpg_ref, wout_ref,
             dx2_ref, dya_ref, dmisc_ref, dconv_ref, dwout_ref, dwpg_ref, dwpe_ref,
             dbpg_ref, dpg_ref, dbg_ref, dcbias_ref, dcw_ref,
             dgated_ref, halo_ref, acc_out, acc_pg, acc_pe):
        i = pl.program_id(0)

        @pl.when(i == 0)
        def _():
            for ref in (dbpg_ref, dpg_ref, dbg_ref, dcbias_ref, dcw_ref, acc_out, acc_pg, acc_pe):
                ref[...] = jnp.zeros_like(ref)

        lane = lax.broadcasted_iota(jnp.int32, (1, LANES), 1)
        lo = lane < HEAD_DIM
        dx3 = dx3_ref[...]
        gate = gate_ref[...].astype(F32)
        de_b = (dx3 * gate).astype(BF16)
        dgpre = dx3 * e_ref[...].astype(F32) * gate * (1.0 - gate)
        dbpg_ref[...] += jnp.sum(dgpre, axis=0, keepdims=True)
        dgpre_b = dgpre.astype(BF16)
        dh2 = lax.dot_general(dgpre_b, wpg_ref[...], NT, preferred_element_type=F32)
        acc_pe[...] += lax.dot_general(p_ref[...].astype(BF16), de_b, TN, preferred_element_type=F32)
        acc_pg[...] += lax.dot_general(h2_ref[...], dgpre_b, TN, preferred_element_type=F32)

        u, u1, u2 = _conv_taps(cc_ref, ch_ref, ccp_ref, chp_ref, halo_ref, i == 0)
        conv = cbias_ref[...] + cw_ref[0:1, :] * u2 + cw_ref[1:2, :] * u1 + cw_ref[2:3, :] * u
        c_b = cb_ref_[...].astype(F32)
        yc = c_b * conv
        fwd = []
        for sl in range(8):
            cols = slice(LANES * (sl % 4), LANES * (sl % 4 + 1))
            y = yc[:, cols] if sl < 4 else ya_ref[:, cols].astype(F32)
            zc = (cz_ref[:, cols] if sl < 4 else az_ref[:, cols]).astype(F32)
            rg = lax.rsqrt(_group_bcast_sum(y * y, lo) * (1.0 / HEAD_DIM) + EPS)
            sig = _sigmoid(zc)
            fwd.append((rg, y * rg, zc * sig, sig * (1.0 + zc * (1.0 - sig))))

        x2 = x2_ref[...]
        r2 = lax.rsqrt(jnp.mean(x2 * x2, axis=-1, keepdims=True) + EPS)
        xn2 = x2 * r2
        dpg_ref[...] += jnp.sum(dh2 * xn2, axis=0, keepdims=True)
        dxn = dh2 * pg_ref[...]
        dx2 = dx3 + r2 * (dxn - xn2 * jnp.mean(dxn * xn2, axis=-1, keepdims=True))
        dx2_ref[...] = dx2
        dx2_b = dx2.astype(BF16)
        dgated_ref[...] = lax.dot_general(dx2_b, wout_ref[...], NT, preferred_element_type=F32)
        acc_out[...] += lax.dot_general(gated_ref[...], dx2_b, TN, preferred_element_type=F32)

        for sl in range(8):
            cols = slice(LANES * (sl % 4), LANES * (sl % 4 + 1))
            wide = slice(LANES * sl, LANES * (sl + 1))
            rg, yhat, silu, dsilu = fwd[sl]
            bgs = bg_ref[:, wide]
            dgt = dgated_ref[:, wide]
            dyn = dgt * silu
            dzc = dgt * (yhat * bgs) * dsilu
            dbg_ref[:, wide] += jnp.sum(dyn * yhat, axis=0, keepdims=True)
            dyh = dyn * bgs
            dy = rg * (dyh - yhat * (_group_bcast_sum(dyh * yhat, lo) * (1.0 / HEAD_DIM)))
            if sl < 4:
                dconv = dy * c_b[:, cols]
                dmisc_ref[:, cols] = (dy * conv[:, cols]).astype(BF16)
                dmisc_ref[:, 512 + LANES * sl:512 + LANES * (sl + 1)] = dzc.astype(BF16)
                dconv_ref[:, cols] = dconv
                dcbias_ref[:, cols] += jnp.sum(dconv, axis=0, keepdims=True)
                dcw_ref[0:1, cols] += jnp.sum(dconv * u2[:, cols], axis=0, keepdims=True)
                dcw_ref[1:2, cols] += jnp.sum(dconv * u1[:, cols], axis=0, keepdims=True)
                dcw_ref[2:3, cols] += jnp.sum(dconv * u[:, cols], axis=0, keepdims=True)
            else:
                dya_ref[:, cols] = dy.astype(BF16)
                dmisc_ref[:, 1024 + LANES * (sl - 4):1024 + LANES * (sl - 3)] = dzc.astype(BF16)

        @pl.when(i == pl.num_programs(0) - 1)
        def _():
            dwout_ref[...] = acc_out[...].astype(BF16)
            dwpg_ref[...] = acc_pg[...].astype(BF16)
            dwpe_ref[...] = acc_pe[...].astype(BF16)

    row = lambda width, cb_=0: pl.BlockSpec((ts, width), lambda i: (i, cb_))
    prev = lambda cb_: pl.BlockSpec((HALO, 512), lambda i: (jnp.maximum(i * blk_h - 1, 0), cb_))
    vec = lambda width: pl.BlockSpec((1, width), lambda i: (0, 0))
    lvec = lambda width: _layer_rows(layer, 1, width)
    wspec = lambda r_, c_: pl.BlockSpec((r_, c_), lambda i: (0, 0))
    vo = lambda width: jax.ShapeDtypeStruct((1, width), F32)
    sq = jax.ShapeDtypeStruct((D_MODEL, D_MODEL), BF16)
    return _hosted_call(
        body, comm, name=name, grid=(s // ts,), sem=("arbitrary",),
        args=(dx3, x2, gate, e, pc, pc, pc, pc, pc, pc, az, ya, gated, h2, p4, cw, cb, bg, pg, wpg_full, wout_full),
        out_shape=(jax.ShapeDtypeStruct((s, D_MODEL), F32), jax.ShapeDtypeStruct((s, 512), BF16),
                   jax.ShapeDtypeStruct((s, 1536), BF16), jax.ShapeDtypeStruct((s, 512), F32),
                   sq, sq, jax.ShapeDtypeStruct((PLE_DIM, D_MODEL), BF16),
                   vo(D_MODEL), vo(D_MODEL), vo(D_MODEL), vo(512), jax.ShapeDtypeStruct((SUBLANES, 512), F32)),
        in_specs=[row(D_MODEL), row(D_MODEL), row(D_MODEL), row(D_MODEL),
                  row(512, 0), row(512, 1), row(512, 2), row(512, 3), prev(1), prev(2), row(512), row(512),
                  row(D_MODEL), row(D_MODEL),
                  pl.BlockSpec((None, None, ts, PLE_DIM), lambda i: (layer, 0, i, 0)),
                  _layer_rows(layer, 3, 512), lvec(512), lvec(D_MODEL), lvec(D_MODEL),
                  wspec(D_MODEL, D_MODEL), wspec(D_MODEL, D_MODEL)],
        out_specs=(row(D_MODEL), row(512), row(1536), row(512),
                   wspec(D_MODEL, D_MODEL), wspec(D_MODEL, D_MODEL), wspec(PLE_DIM, D_MODEL),
                   vec(D_MODEL), vec(D_MODEL), vec(D_MODEL), vec(512),
                   pl.BlockSpec((SUBLANES, 512), lambda i: (0, 0))),
        scratch_shapes=[pltpu.VMEM((ts, D_MODEL), F32), pltpu.VMEM((HALO, 512), F32),
                        pltpu.VMEM((D_MODEL, D_MODEL), F32), pltpu.VMEM((D_MODEL, D_MODEL), F32),
                        pltpu.VMEM((PLE_DIM, D_MODEL), F32)])


def _attn_bwd(qkv, lsum, nblk, dya, name, comm=None):
    s = qkv.shape[0]
    tq = min(ATTN_TILE, s)
    nq = s // tq
    rc = min(ATTN_ROWS, tq)
    n_rc = tq // rc
    nt = 2 if nq % 2 == 0 else 1
    chains = [(t, r, hh) for t in range(nt) for r in range(n_rc) for hh in range(2)]

    def body(nblk_ref, q_ref, k_ref, v_ref, lsum_ref, do_ref, dq_ref, dk_ref, dv_ref, dk_acc, dv_acc):
        hp, qg = pl.program_id(0), pl.program_id(1)
        qis = [qg * nt + t for t in range(nt)]

        @pl.when(qg == 0)
        def _():
            dk_acc[...] = jnp.zeros_like(dk_acc)
            dv_acc[...] = jnp.zeros_like(dv_acc)

        lo, causal, tri_gt, tri_le = _attn_pieces(tq, rc)
        lane = lax.broadcasted_iota(jnp.int32, (1, LANES), 1)
        qh = _split_heads(q_ref[...], lo)
        doh = _split_heads(do_ref[...].astype(BF16), lo)
        lt = lsum_ref[...]
        ltot_h = (jnp.sum(jnp.where(lane == 0, lt, 0.0), axis=-1, keepdims=True),
                  jnp.sum(jnp.where(lane == HEAD_DIM, lt, 0.0), axis=-1, keepdims=True))
        rows = lambda a_, t, r: a_[t * tq + r * rc:t * tq + (r + 1) * rc]
        qc = {(t, r, hh): rows(qh[hh], t, r) for t, r, hh in chains}
        doc = {(t, r, hh): rows(doh[hh], t, r) for t, r, hh in chains}
        ltot = {(t, r, hh): rows(ltot_h[hh], t, r) for t, r, hh in chains}

        mm = lambda a_, b_: jnp.dot(a_.astype(BF16), b_, preferred_element_type=F32)
        mm_nt = lambda a_, b_: lax.dot_general(a_, b_, NT, preferred_element_type=F32)
        mm_tn = lambda a_, b_: lax.dot_general(a_.astype(BF16), b_, TN, preferred_element_type=F32)
        rowsum = lambda a_: jnp.sum(a_, axis=-1, keepdims=True)

        def block(kbs, carries, tiles, diag=False, ok=None):
            mine = [ch for ch in chains if ch[0] in tiles]
            start = {t: pl.multiple_of(kbs[t] * tq, tq) for t in tiles}
            k = {t: k_ref[pl.ds(start[t], tq), :] for t in tiles}
            v = {t: v_ref[pl.ds(start[t], tq), :] for t in tiles}
            kh = {t: _split_heads(k[t], lo) for t in tiles}
            keep = (lambda ch, a_: jnp.where(causal[ch[1]], a_, 0.0)) if diag else (lambda ch, a_: a_)
            live = (lambda ch, a_: a_) if ok is None else (lambda ch, a_: jnp.where(ok[ch[0]], a_, 0.0))
            z = {ch: mm_nt(qc[ch], k[ch[0]]) for ch in mine}
            da = {ch: mm_nt(doc[ch], v[ch[0]]) for ch in mine}
            sp_all = {ch: _softplus(z[ch], None, False) for ch in mine}
            sp = {ch: live(ch, keep(ch, sp_all[ch])) for ch in mine}
            later = {ch: mm(sp[ch], tri_gt) for ch in mine}
            walked = {ch: carries[ch[0]][ch[1]][1 + ch[2]] + rowsum(sp[ch]) for ch in mine}
            a = {ch: live(ch, keep(ch, jnp.exp((z[ch] - sp_all[ch]) - ((ltot[ch] - walked[ch]) + later[ch]))))
                 for ch in mine}
            g = {ch: a[ch] * da[ch] for ch in mine}
            upto = {ch: mm(g[ch], tri_le) for ch in mine}
            dz = {ch: keep(ch, g[ch] - jnp.exp(z[ch] - sp_all[ch]) * (carries[ch[0]][ch[1]][3 + ch[2]] + upto[ch])
                           ).astype(BF16) for ch in mine}
            dqc = {ch: mm(dz[ch], kh[ch[0]][ch[2]]) for ch in mine}
            for t in tiles:
                dkc = [mm_tn(dz[ch], qc[ch]) for ch in mine if ch[0] == t]
                dvc = [mm_tn(a[ch], doc[ch]) for ch in mine if ch[0] == t]
                dk_acc[pl.ds(start[t], tq), :] += sum(dkc[1:], dkc[0])
                dv_acc[pl.ds(start[t], tq), :] += sum(dvc[1:], dvc[0])
            return {t: tuple((carries[t][r][0] + dqc[(t, r, 0)] + dqc[(t, r, 1)], walked[(t, r, 0)], walked[(t, r, 1)],
                              carries[t][r][3] + rowsum(g[(t, r, 0)]), carries[t][r][4] + rowsum(g[(t, r, 1)]))
                             for r in range(n_rc)) for t in tiles}

        zc = jnp.zeros((rc, 1), F32)
        tiles = list(range(nt))
        carries = {t: tuple((jnp.zeros((rc, LANES), F32), zc, zc, zc, zc) for _ in range(n_rc)) for t in tiles}
        near = {t: jnp.maximum(qis[t] - 1, 0) for t in tiles}
        for t in tiles:
            first = near[t] - jnp.clip(nblk_ref[hp, qis[t]].astype(jnp.int32), 0, near[t])
            carries[t] = lax.fori_loop(first, near[t], lambda kb, c, t=t: block({t: kb}, {t: c}, [t])[t], carries[t])
        carries = block(near, carries, tiles, ok={t: qis[t] > 0 for t in tiles})
        carries = block({t: qis[t] for t in tiles}, carries, tiles, diag=True)
        for t in tiles:
            for r in range(n_rc):
                dq_ref[t * tq + r * rc:t * tq + (r + 1) * rc, :] = (carries[t][r][0] * 0.125).astype(BF16)

        @pl.when(qg == pl.num_programs(1) - 1)
        def _():
            dk_ref[...] = dk_acc[...].astype(BF16)
            dv_ref[...] = dv_acc[...].astype(BF16)

    blk = pl.BlockSpec((nt * tq, LANES), lambda hp, qg: (qg, hp))
    col = pl.BlockSpec((s, LANES), lambda hp, qg: (0, hp))
    o512 = jax.ShapeDtypeStruct((s, D_SB), BF16)
    return _hosted_call(
        body, comm, name=name, grid=(4, nq // nt),
        out_shape=(o512, o512, o512),
        in_specs=[pl.BlockSpec(memory_space=pltpu.SMEM), blk,
                  pl.BlockSpec((s, LANES), lambda hp, qg: (0, 4 + hp)),
                  pl.BlockSpec((s, LANES), lambda hp, qg: (0, 8 + hp)), blk, blk],
        out_specs=(blk, col, col),
        scratch_shapes=[pltpu.VMEM((s, LANES), F32), pltpu.VMEM((s, LANES), F32)],
        args=(nblk, qkv, qkv, qkv, lsum, dya), sem=("parallel", "arbitrary"))


def _bwd_dproj(dmisc, dconv, pc, dq, dk, dv, x, dx2, g, cw, layer, win_full, name, comm=None, h=None,
               h_rows=None):
    s = x.shape[0]
    ts = min(ROW_TILE, s)
    blk8 = ts // SUBLANES
    last8 = s // SUBLANES - 1
    fused = h is not None
    emit_dproj = not fused or h_rows is not None
    dw_rows, h_blk = (D_MODEL, 0) if h_rows is None else h_rows

    def body(*refs):
        (dcb_ref, dcz_ref, daz_ref, dconv_ref, nxt_ref, cc_ref, ch_ref, dq_ref, dk_ref, dv_ref,
         x_ref, dx2_ref, g_ref, cw_ref, w_ref) = refs[:15]
        rest = list(refs[15:])
        h_ref = rest.pop(0) if fused else None
        dproj_ref = rest.pop(0) if emit_dproj else None
        dx_ref, dg_ref = rest.pop(0), rest.pop(0)
        dw_ref = rest.pop(0) if fused else None
        dproj_ref = dproj_ref if emit_dproj else rest.pop(0)
        acc_ref = rest.pop(0) if fused else None
        i = pl.program_id(0)

        @pl.when(i == 0)
        def _():
            dg_ref[...] = jnp.zeros_like(dg_ref)
            if fused:
                acc_ref[...] = jnp.zeros_like(acc_ref)

        keep = jnp.where(i == pl.num_programs(0) - 1, 0.0, 1.0)
        dc = dconv_ref[...]
        n0 = nxt_ref[0:1, :] * keep
        n1 = nxt_ref[1:2, :] * keep
        rowi = lax.broadcasted_iota(jnp.int32, dc.shape, 0)
        dc1 = jnp.where(rowi == ts - 1, n0, pltpu.roll(dc, ts - 1, 0))
        dc2 = jnp.where(rowi == ts - 2, n0, jnp.where(rowi == ts - 1, n1, pltpu.roll(dc, ts - 2, 0)))
        du = cw_ref[2:3, :] * dc + cw_ref[1:2, :] * dc1 + cw_ref[0:1, :] * dc2
        dproj_ref[:, 0:512] = dcb_ref[...]
        dproj_ref[:, 512:1024] = (du * ch_ref[...].astype(F32)).astype(BF16)
        dproj_ref[:, 1024:1536] = (du * cc_ref[...].astype(F32)).astype(BF16)
        dproj_ref[:, 1536:2048] = dcz_ref[...]
        dproj_ref[:, 2048:2560] = dq_ref[...]
        dproj_ref[:, 2560:3072] = dk_ref[...]
        dproj_ref[:, 3072:3584] = dv_ref[...]
        dproj_ref[:, 3584:4096] = daz_ref[...]
        dh = lax.dot_general(dproj_ref[...], w_ref[...], NT, preferred_element_type=F32)
        if fused:
            acc_ref[...] += lax.dot_general(h_ref[...], dproj_ref[...], TN, preferred_element_type=F32)
        x = x_ref[...]
        r = lax.rsqrt(jnp.mean(x * x, axis=-1, keepdims=True) + EPS)
        xn = x * r
        dg_ref[...] += jnp.sum(dh * xn, axis=0, keepdims=True)
        dxn = dh * g_ref[...]
        dx_ref[...] = dx2_ref[...] + r * (dxn - xn * jnp.mean(dxn * xn, axis=-1, keepdims=True))
        if fused:
            @pl.when(i == pl.num_programs(0) - 1)
            def _():
                dw_ref[...] = acc_ref[...].astype(BF16)

    row = lambda width, cb_=0: pl.BlockSpec((ts, width), lambda i: (i, cb_))
    nxt = pl.BlockSpec((SUBLANES, 512), lambda i: (jnp.minimum((i + 1) * blk8, last8), 0))
    vec = lambda width: pl.BlockSpec((1, width), lambda i: (0, 0))
    lvec = lambda width: _layer_rows(layer, 1, width)
    once = dict(pipeline_mode=pl.Buffered(1)) if fused else {}
    whole = lambda rows_: pl.BlockSpec((rows_, N_IN), lambda i: (0, 0), **once)
    in_specs = [row(512, 0), row(512, 1), row(512, 2), row(512), nxt, row(512, 1), row(512, 2),
                row(512), row(512), row(512), row(D_MODEL), row(D_MODEL), lvec(D_MODEL),
                _layer_rows(layer, 3, 512), whole(D_MODEL)]
    args = [dmisc, dmisc, dmisc, dconv, dconv, pc, pc, dq, dk, dv, x, dx2, g, cw, win_full]
    out_shape = [jax.ShapeDtypeStruct((s, D_MODEL), F32), jax.ShapeDtypeStruct((1, D_MODEL), F32)]
    out_specs = [row(D_MODEL), vec(D_MODEL)]
    scratch = []
    if emit_dproj:
        out_shape.insert(0, jax.ShapeDtypeStruct((s, N_IN), BF16))
        out_specs.insert(0, row(N_IN))
    else:
        scratch.append(pltpu.VMEM((ts, N_IN), BF16))
    if fused:
        in_specs.append(row(dw_rows, h_blk))
        args.append(h)
        out_shape.append(jax.ShapeDtypeStruct((dw_rows, N_IN), BF16))
        out_specs.append(whole(dw_rows))
        scratch.append(pltpu.VMEM((dw_rows, N_IN), F32))
    return _hosted_call(
        body, comm, name=name, grid=(s // ts,), out_shape=tuple(out_shape), in_specs=in_specs,
        out_specs=tuple(out_specs), scratch_shapes=scratch, args=tuple(args), sem=("arbitrary",))


def _atb(a, b, name, a_cols=None, comm=None):
    s, n = b.shape
    m, a_blk = (a.shape[-1], 0) if a_cols is None else a_cols
    ts = min(512, s)
    tn = min(2048, n)
    a_spec = pl.BlockSpec((ts, m), lambda j, i: (i, a_blk))

    def body(a_ref, b_ref, o_ref, acc_ref):
        i = pl.program_id(1)

        @pl.when(i == 0)
        def _():
            acc_ref[...] = jnp.zeros_like(acc_ref)

        acc_ref[...] += lax.dot_general(a_ref[...].astype(BF16), b_ref[...], TN, preferred_element_type=F32)

        @pl.when(i == pl.num_programs(1) - 1)
        def _():
            o_ref[...] = acc_ref[...].astype(BF16)

    (out,), got = _hosted_call(
        body, comm, name=name, grid=(n // tn, s // ts),
        out_shape=(jax.ShapeDtypeStruct((m, n), BF16),),
        in_specs=[a_spec, pl.BlockSpec((ts, tn), lambda j, i: (i, j))],
        out_specs=(pl.BlockSpec((m, tn), lambda j, i: (0, j)),),
        scratch_shapes=[pltpu.VMEM((m, tn), F32)],
        args=(a, b), sem=("parallel", "arbitrary"))
    return out, got


def _adamw_math(w, g, m, v):
    m2 = ADAM_B1 * m + (1.0 - ADAM_B1) * g
    v2 = ADAM_B2 * v + (1.0 - ADAM_B2) * (g * g)
    m_hat = m2 / (1.0 - ADAM_B1 ** ADAM_STEP)
    v_hat = v2 / (1.0 - ADAM_B2 ** ADAM_STEP)
    delta = -ADAM_LR * (m_hat / (jnp.sqrt(v_hat) + ADAM_EPS) + ADAM_WD * w)
    return delta, m2, v2


def _adamw_sum8(pieces, w, m, v, name):
    _, rows, cols = w.shape
    tr = min([rows, 256] + [pc_[0].shape[1] for pc_ in pieces])
    n_tiles = rows // tr
    n_p = len(pieces)
    spans = [(layer, row0 // tr, arr.shape[1] // tr) for arr, layer, row0 in pieces]

    def body(*refs):
        p_refs = refs[:n_p]
        w_ref, m_ref, v_ref, g_ref, d_ref, m2_ref, v2_ref = refs[n_p:]
        l, i = pl.program_id(0), pl.program_id(1)

        def run(p_ref):
            g = p_ref[0].astype(F32)
            for d in range(1, N_DEV):
                g = g + p_ref[d].astype(F32)
            g_ref[...] = g
            d_ref[...], m2_ref[...], v2_ref[...] = _adamw_math(w_ref[...], g, m_ref[...], v_ref[...])

        for p_ref, (layer, t0, nt) in zip(p_refs, spans):
            mine = jnp.logical_and(l == layer, jnp.logical_and(i >= t0, i < t0 + nt))
            pl.when(mine)(lambda p_ref=p_ref: run(p_ref))

    def piece_spec(layer, t0, nt):
        return pl.BlockSpec((N_DEV, tr, cols),
                            lambda l, i: (0, jnp.clip(jnp.where(l == layer, i - t0, jnp.where(l < layer, 0, nt - 1)),
                                                      0, nt - 1), 0))

    tile = pl.BlockSpec((None, tr, cols), lambda l, i: (l, i, 0))
    o = jax.ShapeDtypeStruct((DEPTH, rows, cols), F32)
    return _call(
        body, name=name, grid=(DEPTH, n_tiles),
        out_shape=(o, o, o, o),
        in_specs=[*[piece_spec(*sp) for sp in spans], tile, tile, tile],
        out_specs=(tile, tile, tile, tile),
        compiler_params=_params(("arbitrary", "arbitrary"), VMEM_LIMIT),
    )(*[pc_[0] for pc_ in pieces], w, m, v)


def _small_update(blk, layered, final, conv, loss_parts):
    n_l = len(layered)
    ins = [a for item in layered for a in item] + list(final) + list(conv) + [loss_parts]
    shapes = [item[2].shape for item in layered] + [final[1].shape, conv[2].shape]
    out_shape = [jax.ShapeDtypeStruct(sh, F32) for sh in shapes for _ in range(4)]
    out_shape.append(jax.ShapeDtypeStruct((1, LANES), F32))

    def body(*refs):
        blk_ref, refs = refs[0], refs[1:]
        in_refs, out_refs, pick_ref = refs[:len(ins)], refs[len(ins):-1], refs[-1]

        def total(ref):
            g = ref[0]
            for d in range(1, N_DEV):
                g = g + ref[d]
            return g

        def update(k, at, g, w_ref, m_ref, v_ref):
            g_ref, d_ref, m2_ref, v2_ref = out_refs[4 * k:4 * k + 4]
            g_ref[at] = g
            d_ref[at], m2_ref[at], v2_ref[at] = _adamw_math(w_ref[at], g, m_ref[at], v_ref[at])

        for k in range(n_l):
            p0, p1, w_ref, m_ref, v_ref = in_refs[5 * k:5 * k + 5]
            for layer, parts in enumerate((p0, p1)):
                update(k, pl.ds(layer, 1), total(parts), w_ref, m_ref, v_ref)
        pf, w_ref, m_ref, v_ref = in_refs[5 * n_l:5 * n_l + 4]
        update(n_l, pl.ds(0, 1), total(pf), w_ref, m_ref, v_ref)
        c0, c1, w_ref, m_ref, v_ref = in_refs[5 * n_l + 4:5 * n_l + 9]
        for layer, parts in enumerate((c0, c1)):
            g8 = total(parts)
            mine = jnp.zeros((SUBLANES, HEAD_DIM), F32)
            for j in range(N_DEV):
                mine = mine + jnp.where(blk_ref[0] == j, g8[:, HEAD_DIM * j:HEAD_DIM * (j + 1)], 0.0)
            pick_ref[...] = mine
            update(n_l + 1, layer, pick_ref[0:3, :], w_ref, m_ref, v_ref)
        out_refs[-1][...] = total(in_refs[-1])

    whole = lambda shape: pl.BlockSpec(shape, lambda: (0,) * len(shape))
    outs = _call(
        body, name="adamw_small",
        out_shape=tuple(out_shape),
        in_specs=[pl.BlockSpec(memory_space=pltpu.SMEM)] + [whole(a.shape) for a in ins],
        out_specs=tuple(whole(o.shape) for o in out_shape),
        scratch_shapes=[pltpu.VMEM((SUBLANES, HEAD_DIM), F32)],
    )(blk, *ins)
    return [outs[4 * k:4 * k + 4] for k in range(n_l + 2)], outs[-1]


def kernel(x, p, norm_g, w_in, conv_w, conv_b, branch_g, w_out, ple_norm_g, w_pg, b_pg, w_pe, final_g, loss_target, m_norm_g, m_w_in, m_conv_w, m_conv_b, m_branch_g, m_w_out, m_ple_norm_g, m_w_pg, m_b_pg, m_w_pe, m_final_g, v_norm_g, v_w_in, v_conv_w, v_conv_b, v_branch_g, v_w_out, v_ple_norm_g, v_w_pg, v_b_pg, v_w_pe, v_final_g):
    s = x.shape[1]
    x0 = x.reshape(s, D_MODEL)
    target = loss_target.reshape(s, D_MODEL)
    me_blk = _my_block()

    win_s, wout_s, wpg_s, wpe_s = _cast_bf16(
        [w_in.reshape(DEPTH * D_MODEL, 512), w_out.reshape(DEPTH * 128, D_MODEL),
         w_pg.reshape(DEPTH * 128, D_MODEL), w_pe.reshape(DEPTH * PLE_DIM, 128)], "cast_weights")
    win_s, wout_s = win_s.reshape(DEPTH, D_MODEL, 512), wout_s.reshape(DEPTH, 128, D_MODEL)
    wpg_s, wpe_s = wpg_s.reshape(DEPTH, 128, D_MODEL), wpe_s.reshape(DEPTH, PLE_DIM, 128)
    cw_s = jnp.zeros((SUBLANES, LANES), F32).at[:DEPTH * 3, :HEAD_DIM].set(conv_w.reshape(DEPTH * 3, HEAD_DIM))
    bf = lambda r_, c_: jax.ShapeDtypeStruct((r_, c_), BF16)
    w_items = lambda l: [(wout_s[l], bf(D_MODEL, D_MODEL), "rows128"), (wpg_s[l], bf(D_MODEL, D_MODEL), "rows128"),
                         (wpe_s[l], bf(PLE_DIM, D_MODEL), "cols128")]
    win_f = [None] * DEPTH
    win_f[0], cw_all = _comm_call(_gather_comm([
        (win_s[0], bf(D_MODEL, N_IN), "cols512"),
        (cw_s, jax.ShapeDtypeStruct((N_DEV, SUBLANES, LANES), F32), "slot")]), "gather_w_in_0")
    cw_full = jnp.transpose(cw_all[:, :DEPTH * 3, :HEAD_DIM].reshape(N_DEV, DEPTH, 3, HEAD_DIM), (1, 2, 0, 3))
    cw_full = cw_full.reshape(DEPTH, 3, D_CONV)
    gather_rest_0 = _gather_comm(w_items(0))
    gather_win_1 = _gather_comm([(win_s[1], bf(D_MODEL, N_IN), "cols512")])
    gather_rest_1 = _gather_comm(w_items(1))

    norm3, convb3, branch3, ple3, bpg3 = [a.reshape(DEPTH, 1, -1) for a in (norm_g, conv_b, branch_g, ple_norm_g, b_pg)]

    saved = []
    xl = x0
    wout_f, wpg_f, wpe_f = [None] * DEPTH, [None] * DEPTH, [None] * DEPTH
    for l in range(DEPTH):
        (h, pc, qkv, az), got = _fwd_in(xl, norm3, l, win_f[l], f"fwd_in_{l}",
                                        comm=gather_rest_0 if l == 0 else None)
        if l == 0:
            wout_f[0], wpg_f[0], wpe_f[0] = got
        (ya, lsum, nblk), got = _attn_fwd(qkv, f"attn_fwd_{l}", comm=gather_win_1 if l == 0 else None)
        if l == 0:
            (win_f[1],) = got
        last = l == DEPTH - 1
        outs, got = _fwd_mid(
            xl, pc, az, ya, p, l, cw_full, convb3, branch3, wout_f[l],
            ple3, wpg_f[l], bpg3, wpe_f[l], f"fwd_mid_{l}",
            comm=gather_rest_1 if l == 0 else None, head=(target, final_g[None, :]) if last else None)
        x2, x3, gated, h2, gate, e = outs[:6]
        if l == 0:
            wout_f[1], wpg_f[1], wpe_f[1] = got
        saved.append(dict(x=xl, h=h, pc=pc, qkv=qkv, az=az, ya=ya, lsum=lsum, nblk=nblk, x2=x2, gated=gated, h2=h2,
                          gate=gate, e=e))
        xl = x3

    dx, (loss_acc, d_final_g) = xl, outs[6:]

    dwin, dwout, dwpg, dwpe = [None] * DEPTH, [None] * DEPTH, [None] * DEPTH, [None] * DEPTH
    small = dict(norm_g=[None] * DEPTH, conv_b=[None] * DEPTH, branch_g=[None] * DEPTH,
                 ple_norm_g=[None] * DEPTH, b_pg=[None] * DEPTH, conv_w=[None] * DEPTH)
    slot = lambda r_, c_: jax.ShapeDtypeStruct((r_, c_), BF16)
    half = D_MODEL // 2
    r_out, r_pg, r_pe = [None] * DEPTH, [None] * DEPTH, [None] * DEPTH

    def rest_items(l):
        return [(dwout[l], slot(128, D_MODEL), "rows128"), (dwpg[l], slot(128, D_MODEL), "rows128"),
                (dwpe[l], slot(PLE_DIM, 128), "cols128")]

    for l in reversed(range(DEPTH)):
        sv = saved[l]
        quarter = D_MODEL // 4
        ride = None
        if l == 0:
            ride = _exchange_comm(rest_items(1) + [(dwin[1][:quarter], slot(quarter, 512), "cols512")])
        (dx2, dya, dmisc, dconv, dwout[l], dwpg[l], dwpe[l], d_bpg, d_pg, d_bg, d_cbias, d_cw), got = _bwd_mid(
            dx, sv["x2"], sv["gate"], sv["e"], sv["pc"], sv["az"], sv["ya"], sv["gated"], sv["h2"], p, l,
            cw_full, convb3, branch3, ple3, wpg_f[l], wout_f[l], f"bwd_mid_{l}",
            comm=ride)
        if l == 0:
            r_out[1], r_pg[1], r_pe[1], r_in1_a = got
            ride = _exchange_comm([(dwin[1][quarter:], slot(D_MODEL - quarter, 512), "cols512")] + rest_items(0))
        (dq, dk, dv), got = _attn_bwd(sv["qkv"], sv["lsum"], sv["nblk"], dya, f"attn_bwd_{l}", comm=ride)
        if l == 0:
            r_in1_b, r_out[0], r_pg[0], r_pe[0] = got
        dproj_args = (dmisc, dconv, sv["pc"], dq, dk, dv, sv["x"], dx2, norm3, cw_full, l, win_f[l])
        if l == 1:
            (dx, d_ng, dwin[1]), _ = _bwd_dproj(*dproj_args, "bwd_dproj_dw_1", h=sv["h"])
        else:
            (dproj, dx, d_ng, dwin_top), _ = _bwd_dproj(*dproj_args, "bwd_dproj_dw_0", h=sv["h"], h_rows=(half, 0))
            dwin_bot, (r_in0_top,) = _atb(sv["h"], dproj, "dw_in_0_bottom", a_cols=(half, 1),
                                          comm=_exchange_comm([(dwin_top, slot(half, 512), "cols512")]))
        small["norm_g"][l], small["conv_b"][l], small["branch_g"][l] = d_ng, d_cbias, d_bg
        small["ple_norm_g"][l], small["b_pg"][l], small["conv_w"][l] = d_pg, d_bpg, d_cw
    grad_x = dx.reshape(1, s, D_MODEL)

    names = ["norm_g", "conv_b", "branch_g", "ple_norm_g", "b_pg", "conv_w"]
    small_list = [small[n][l] for n in names for l in range(DEPTH)] + [d_final_g, loss_acc]
    got = _comm_call(_exchange_comm(
        [(dwin_bot, slot(half, 512), "cols512")]
        + [(a, jax.ShapeDtypeStruct(a.shape, F32), "slot") for a in small_list]), "exchange_last")
    r_in0_bot, r_small = got[0], got[1:]

    per_layer = lambda r: [(r[0], 0, 0), (r[1], 1, 0)]
    g_win, d_win, m_win, v_win = _adamw_sum8([(r_in0_top, 0, 0), (r_in0_bot, 0, half), (r_in1_a, 1, 0),
                                              (r_in1_b, 1, D_MODEL // 4)],
                                             w_in, m_w_in, v_w_in, "adamw_w_in")
    g_wout, d_wout, m_wout, v_wout = _adamw_sum8(per_layer(r_out), w_out, m_w_out, v_w_out, "adamw_w_out")
    g_wpg, d_wpg, m_wpg, v_wpg = _adamw_sum8(per_layer(r_pg), w_pg, m_w_pg, v_w_pg, "adamw_w_pg")
    g_wpe, d_wpe, m_wpe, v_wpe = _adamw_sum8(per_layer(r_pe), w_pe, m_w_pe, v_w_pe, "adamw_w_pe")

    layered = [(norm_g, m_norm_g, v_norm_g), (conv_b, m_conv_b, v_conv_b), (branch_g, m_branch_g, v_branch_g),
               (ple_norm_g, m_ple_norm_g, v_ple_norm_g), (b_pg, m_b_pg, v_b_pg)]
    row = lambda a: a.reshape(1, -1)
    upd, loss_row = _small_update(
        jnp.reshape(me_blk, (1,)).astype(jnp.int32),
        [(r_small[2 * k], r_small[2 * k + 1], *wmv) for k, wmv in enumerate(layered)],
        (r_small[12], row(final_g), row(m_final_g), row(v_final_g)),
        (r_small[10], r_small[11], conv_w, m_conv_w, v_conv_w), r_small[13])
    loss = loss_row[0, 0]
    upd[5] = [a.reshape(-1) for a in upd[5]]

    big = {1: (g_win, d_win, m_win, v_win), 5: (g_wout, d_wout, m_wout, v_wout), 7: (g_wpg, d_wpg, m_wpg, v_wpg),
           9: (g_wpe, d_wpe, m_wpe, v_wpe)}
    small_at = {0: 0, 2: 6, 3: 1, 4: 2, 6: 3, 8: 4, 10: 5}
    per_kind = [[(big[i] if i in big else upd[small_at[i]])[j] for i in range(11)] for j in range(4)]
    return (loss, grad_x, *per_kind[0], *per_kind[1], *per_kind[2], *per_kind[3])
```

```python
import jax
import jax.numpy as jnp
from jax import lax
from jax.experimental import pallas as pl
from jax.experimental.pallas import tpu as pltpu

F32 = jnp.float32
BF16 = jnp.bfloat16

D_MODEL = 1024
D_CONV = 512
D_SB = 512
N_IN = 4096
HEAD_DIM = 64
PLE_DIM = 256
DEPTH = 2
EPS = 1e-6
ADAM_LR = 0.001
ADAM_B1 = 0.9
ADAM_B2 = 0.999
ADAM_EPS = 1e-08
ADAM_WD = 0.01
ADAM_STEP = 10

LANES = 128
SUBLANES = 8
VMEM_BYTES_V7X = 64 * 1024 * 1024
VMEM_LIMIT = VMEM_BYTES_V7X - 8 * 1024 * 1024

N_DEV = 8
ROW_TILE = 256
FWD_ROW_TILE = 512
ATTN_TILE = 256

NT = (((1,), (1,)), ((), ()))
TN = (((0,), (0,)), ((), ()))


def _call(body, **kw):
    return pl.pallas_call(body, **kw)


def _params(sem=None, vmem=None):
    return pltpu.CompilerParams(dimension_semantics=sem, vmem_limit_bytes=vmem)


def _sigmoid(z):
    return 0.5 * jnp.tanh(0.5 * z) + 0.5


def _group_bcast_sum(a, lo):
    s_lo = jnp.sum(jnp.where(lo, a, 0.0), axis=-1, keepdims=True)
    s_hi = jnp.sum(jnp.where(lo, 0.0, a), axis=-1, keepdims=True)
    return jnp.where(lo, s_lo, s_hi)


def _layer_rows(layer, rows, width):
    return pl.BlockSpec((None, rows, width), lambda i: (layer, 0, 0))


def _my_block():
    return 4 * lax.axis_index("x") + 2 * lax.axis_index("y") + lax.axis_index("c")


def _cast_bf16(arrays, name):
    n = len(arrays)

    def body(*refs):
        for a_ref, o_ref in zip(refs[:n], refs[n:]):
            o_ref[...] = a_ref[...].astype(BF16)

    whole = lambda a: pl.BlockSpec(a.shape, lambda: (0, 0))
    return _call(
        body, name=name,
        out_shape=tuple(jax.ShapeDtypeStruct(a.shape, BF16) for a in arrays),
        in_specs=[whole(a) for a in arrays], out_specs=tuple(whole(a) for a in arrays),
        compiler_params=_params(None, VMEM_LIMIT),
    )(*arrays)


class _Comm:
    def __init__(self, inputs, out_shapes, scratch, begin, middle, finish):
        self.inputs, self.out_shapes, self.scratch = list(inputs), list(out_shapes), list(scratch)
        self.begin, self.middle, self.finish = begin, middle, finish


def _slab(kind, ref, blk):
    if kind == "cols512":
        return ref.at[:, pl.ds(blk * 512, 512)]
    if kind == "rows128":
        return ref.at[pl.ds(blk * 128, 128), :]
    if kind == "cols128":
        return ref.at[:, pl.ds(blk * 128, 128)]
    return ref.at[blk]


def _gather_comm(items):
    n_t = len(items)
    kinds = [it[2] for it in items]

    def ctx(ins, outs, sems):
        send_sems, recv_sems, local_sems = sems
        x, y, c = lax.axis_index("x"), lax.axis_index("y"), lax.axis_index("c")
        me, sibling = (x, y, c), (x, y, 1 - c)
        chips = [(1 - x, y), (x, 1 - y), (1 - x, 1 - y)]

        def place(t, dev):
            return _slab(kinds[t], outs[t], 4 * dev[0] + 2 * dev[1] + dev[2])

        def copy(t, k, block, to, own=False):
            return pltpu.make_async_remote_copy(
                src_ref=ins[t] if own else place(t, block), dst_ref=place(t, block),
                send_sem=send_sems.at[t, k], recv_sem=recv_sems.at[t, k],
                device_id=to, device_id_type=pl.DeviceIdType.MESH)

        mine = [pltpu.make_async_copy(ins[t], place(t, me), local_sems.at[t]) for t in range(n_t)]
        first = []
        for t in range(n_t):
            first.append(copy(t, 0, me, sibling, own=True))
            first += [copy(t, 1 + j, me, (*chip, c), own=True) for j, chip in enumerate(chips)]
        passed = [copy(t, 4 + j, (*chip, c), sibling) for j, chip in enumerate(chips) for t in range(n_t)]
        landed = [copy(t, 1 + j, (*chip, c), me) for j, chip in enumerate(chips) for t in range(n_t)]
        from_sibling = []
        for t in range(n_t):
            from_sibling.append(copy(t, 0, sibling, me))
            from_sibling += [copy(t, 4 + j, (*chip, 1 - c), me) for j, chip in enumerate(chips)]
        return mine, first, landed, passed, from_sibling

    def begin(ins, outs, sems):
        mine, first, _, _, _ = ctx(ins, outs, sems)
        for cp in mine + first:
            cp.start()

    def middle(ins, outs, sems):
        _, _, landed, passed, _ = ctx(ins, outs, sems)
        for got, fwd in zip(landed, passed):
            got.wait_recv()
            fwd.start()

    def finish(ins, outs, sems):
        mine, first, _, passed, from_sibling = ctx(ins, outs, sems)
        for cp in from_sibling:
            cp.wait_recv()
        for cp in first + passed:
            cp.wait_send()
        for cp in mine:
            cp.wait()

    scratch = [pltpu.SemaphoreType.DMA((n_t, 7)), pltpu.SemaphoreType.DMA((n_t, 7)), pltpu.SemaphoreType.DMA((n_t,))]
    return _Comm([it[0] for it in items], [it[1] for it in items], scratch, begin, middle, finish)


def _exchange_comm(items):
    n_t = len(items)
    kinds = [it[2] for it in items]
    rows = [it[3] if len(it) > 3 else None for it in items]

    def ctx(ins, outs, sems):
        send_sems, recv_sems, local_sems = sems
        x, y, c = lax.axis_index("x"), lax.axis_index("y"), lax.axis_index("c")
        me_blk = 4 * x + 2 * y + c

        def src(t, blk):
            ref = ins[t] if rows[t] is None else ins[t].at[pl.ds(rows[t][0], rows[t][1])]
            return ref if kinds[t] == "slot" else _slab(kinds[t], ref, blk)

        local = [pltpu.make_async_copy(src(t, me_blk), outs[t].at[me_blk], local_sems.at[t]) for t in range(n_t)]
        remote = []
        for k in range(1, N_DEV):
            px = 1 - x if k & 4 else x
            py = 1 - y if k & 2 else y
            pc_ = 1 - c if k & 1 else c
            for t in range(n_t):
                remote.append(pltpu.make_async_remote_copy(
                    src_ref=src(t, 4 * px + 2 * py + pc_), dst_ref=outs[t].at[me_blk],
                    send_sem=send_sems.at[k - 1, t], recv_sem=recv_sems.at[k - 1, t],
                    device_id=(px, py, pc_), device_id_type=pl.DeviceIdType.MESH))
        return local, remote

    def begin(ins, outs, sems):
        local, remote = ctx(ins, outs, sems)
        for cp in local + remote:
            cp.start()

    def finish(ins, outs, sems):
        local, remote = ctx(ins, outs, sems)
        for cp in remote:
            cp.wait_recv()
        for cp in remote:
            cp.wait_send()
        for cp in local:
            cp.wait()

    scratch = [pltpu.SemaphoreType.DMA((N_DEV - 1, n_t)), pltpu.SemaphoreType.DMA((N_DEV - 1, n_t)),
               pltpu.SemaphoreType.DMA((n_t,))]
    out_shapes = [jax.ShapeDtypeStruct((N_DEV, *it[1].shape), it[1].dtype) for it in items]
    return _Comm([it[0] for it in items], out_shapes, scratch, begin, None, finish)


def _comm_call(comm, name):
    n_in, n_out = len(comm.inputs), len(comm.out_shapes)

    def body(*refs):
        ins, outs, sems = refs[:n_in], refs[n_in:n_in + n_out], refs[n_in + n_out:]
        comm.begin(ins, outs, sems)
        if comm.middle is not None:
            comm.middle(ins, outs, sems)
        comm.finish(ins, outs, sems)

    any_spec = pl.BlockSpec(memory_space=pl.ANY)
    return _call(body, name=name, out_shape=tuple(comm.out_shapes), in_specs=[any_spec] * n_in,
                 out_specs=[any_spec] * n_out, scratch_shapes=comm.scratch)(*comm.inputs)


def _hosted(body, n_in, n_out, comm, first, last, middle):
    if comm is None:
        return lambda *refs: body(*refs)
    n_ci, n_co, n_cs = len(comm.inputs), len(comm.out_shapes), len(comm.scratch)

    def wrapped(*refs):
        ins, cin = refs[:n_in], refs[n_in:n_in + n_ci]
        o0 = n_in + n_ci
        outs, cout = refs[o0:o0 + n_out], refs[o0 + n_out:o0 + n_out + n_co]
        scr, csem = refs[o0 + n_out + n_co:len(refs) - n_cs], refs[len(refs) - n_cs:]
        pl.when(first())(lambda: comm.begin(cin, cout, csem))
        body(*ins, *outs, *scr)
        if comm.middle is not None:
            pl.when(middle())(lambda: comm.middle(cin, cout, csem))
        pl.when(last())(lambda: comm.finish(cin, cout, csem))

    return wrapped


def _hosted_call(body, comm, *, name, grid, out_shape, in_specs, out_specs, args, scratch_shapes=(), sem=None):
    nd = len(grid)
    first, last, middle = _at_first(nd), _at_last(nd), _at_middle(nd)
    if comm is not None:
        sem = ("arbitrary",) * nd
    n_in, n_out = len(in_specs), len(out_shape)
    any_spec = pl.BlockSpec(memory_space=pl.ANY)
    c_in = [] if comm is None else comm.inputs
    c_out = [] if comm is None else comm.out_shapes
    c_scr = [] if comm is None else comm.scratch
    outs = _call(
        _hosted(body, n_in, n_out, comm, first, last, middle), name=name, grid=grid,
        out_shape=(*out_shape, *c_out),
        in_specs=[*in_specs, *[any_spec] * len(c_in)],
        out_specs=(*out_specs, *[any_spec] * len(c_out)),
        scratch_shapes=[*scratch_shapes, *c_scr],
        compiler_params=_params(sem, VMEM_LIMIT),
    )(*args, *c_in)
    return outs[:n_out], outs[n_out:]


def _grid_step(ndim):
    i, n = pl.program_id(0), pl.num_programs(0)
    for d in range(1, ndim):
        i, n = i * pl.num_programs(d) + pl.program_id(d), n * pl.num_programs(d)
    return i, n


def _at_first(ndim):
    return lambda: _grid_step(ndim)[0] == 0


def _at_last(ndim):
    def pred():
        i, n = _grid_step(ndim)
        return i == n - 1
    return pred


def _at_middle(ndim):
    def pred():
        i, n = _grid_step(ndim)
        return i == (3 * n) // 4
    return pred


def _fwd_in(x, g, layer, w_full, name, comm=None):
    s = x.shape[0]
    ts = min(FWD_ROW_TILE, s)

    def body(x_ref, g_ref, w_ref, h_ref, pc_ref, qkv_ref, az_ref):
        xf = x_ref[...]
        r = lax.rsqrt(jnp.mean(xf * xf, axis=-1, keepdims=True) + EPS)
        h = (xf * r * g_ref[...]).astype(BF16)
        h_ref[...] = h
        pc_ref[...] = jnp.dot(h, w_ref[:, 0:2048], preferred_element_type=F32).astype(BF16)
        q = jnp.dot(h, w_ref[:, 2048:2560], preferred_element_type=F32)
        qkv_ref[:, 0:512] = (q * 0.125).astype(BF16)
        qkv_ref[:, 512:1536] = jnp.dot(h, w_ref[:, 2560:3584], preferred_element_type=F32).astype(BF16)
        az_ref[...] = jnp.dot(h, w_ref[:, 3584:4096], preferred_element_type=F32).astype(BF16)

    row = lambda width: pl.BlockSpec((ts, width), lambda i: (i, 0))
    return _hosted_call(
        body, comm, name=name, grid=(s // ts,),
        out_shape=(jax.ShapeDtypeStruct((s, D_MODEL), BF16), jax.ShapeDtypeStruct((s, 2048), BF16),
                   jax.ShapeDtypeStruct((s, 1536), BF16), jax.ShapeDtypeStruct((s, 512), BF16)),
        in_specs=[row(D_MODEL), _layer_rows(layer, 1, D_MODEL),
                  pl.BlockSpec((D_MODEL, N_IN), lambda i: (0, 0))],
        out_specs=(row(D_MODEL), row(2048), row(1536), row(512)),
        args=(x, g, w_full), sem=("parallel",))


ATTN_ROWS = 128
ATTN_DONE = 104.0


def _attn_pieces(tq, rc):
    lane = lax.broadcasted_iota(jnp.int32, (1, LANES), 1)
    lo = lane < HEAD_DIM
    row = lax.broadcasted_iota(jnp.int32, (tq, tq), 0)
    col = lax.broadcasted_iota(jnp.int32, (tq, tq), 1)
    tri_gt = jnp.where(row > col, 1.0, 0.0).astype(BF16)
    tri_le = jnp.where(row <= col, 1.0, 0.0).astype(BF16)
    rrow = lax.broadcasted_iota(jnp.int32, (rc, tq), 0)
    rcol = lax.broadcasted_iota(jnp.int32, (rc, tq), 1)
    causal = [rcol < rrow + r * rc for r in range(tq // rc)]
    return lo, causal, tri_gt, tri_le


def _split_heads(a, lo):
    z = jnp.zeros_like(a)
    return (jnp.where(lo, a, z), jnp.where(lo, z, a))


def _softplus(z, causal, diag):
    neg_abs = lax.bitcast_convert_type(lax.bitcast_convert_type(z, jnp.uint32) | jnp.uint32(0x80000000), F32)
    sp = jnp.maximum(z, 0.0) + jnp.log(1.0 + jnp.exp(neg_abs))
    if diag:
        sp = jnp.where(causal, sp, 0.0)
    return sp


def _attn_fwd(qkv, name, comm=None):
    s = qkv.shape[0]
    tq = min(ATTN_TILE, s)
    nq = s // tq
    rc = min(ATTN_ROWS, tq)
    n_rc = tq // rc
    nt = 4 if nq % 4 == 0 else 2 if nq % 2 == 0 else 1
    chains = [(t, r, hh) for t in range(nt) for r in range(n_rc) for hh in range(2)]

    def body(q_ref, k_ref, v_ref, o_ref, lsum_ref, nblk_ref):
        hp = pl.program_id(0)
        qis = [pl.program_id(1) * nt + t for t in range(nt)]
        lo, causal, tri_gt, _ = _attn_pieces(tq, rc)
        qh = _split_heads(q_ref[...], lo)
        qc = {(t, r, hh): qh[hh][t * tq + r * rc:t * tq + (r + 1) * rc] for t, r, hh in chains}

        mm = lambda a_, b_: jnp.dot(a_.astype(BF16), b_, preferred_element_type=F32)
        rowsum = lambda a_: jnp.sum(a_, axis=-1, keepdims=True)

        def block(kb, carry, t):
            mine = [ch for ch in chains if ch[0] == t]
            start = pl.multiple_of(kb * tq, tq)
            k = k_ref[pl.ds(start, tq), :]
            vh = _split_heads(v_ref[pl.ds(start, tq), :], lo)
            z = {ch: lax.dot_general(qc[ch], k, NT, preferred_element_type=F32) for ch in mine}
            sp = {ch: _softplus(z[ch], None, False) for ch in mine}
            later = {ch: mm(sp[ch], tri_gt) for ch in mine}
            a = {ch: jnp.exp((z[ch] - sp[ch]) - (carry[ch[1]][1 + ch[2]] + later[ch])) for ch in mine}
            pv = {ch: mm(a[ch], vh[ch[2]]) for ch in mine}
            return tuple((carry[r][0] + pv[(t, r, 0)] + pv[(t, r, 1)], carry[r][1] + rowsum(sp[(t, r, 0)]),
                          carry[r][2] + rowsum(sp[(t, r, 1)])) for r in range(n_rc))

        def first_two():
            ok = [qi > 0 for qi in qis]
            d0 = [pl.multiple_of(qi * tq, tq) for qi in qis]
            p0 = [pl.multiple_of(jnp.maximum(qi - 1, 0) * tq, tq) for qi in qis]
            k_d = [k_ref[pl.ds(d0[t], tq), :] for t in range(nt)]
            k_p = [k_ref[pl.ds(p0[t], tq), :] for t in range(nt)]
            vh_d = [_split_heads(v_ref[pl.ds(d0[t], tq), :], lo) for t in range(nt)]
            vh_p = [_split_heads(v_ref[pl.ds(p0[t], tq), :], lo) for t in range(nt)]
            z_d = {ch: lax.dot_general(qc[ch], k_d[ch[0]], NT, preferred_element_type=F32) for ch in chains}
            z_p = {ch: lax.dot_general(qc[ch], k_p[ch[0]], NT, preferred_element_type=F32) for ch in chains}
            sp_d = {ch: _softplus(z_d[ch], causal[ch[1]], True) for ch in chains}
            sp_raw = {ch: _softplus(z_p[ch], None, False) for ch in chains}
            sp_p = {ch: jnp.where(ok[ch[0]], sp_raw[ch], 0.0) for ch in chains}
            later_d = {ch: mm(sp_d[ch], tri_gt) for ch in chains}
            later_p = {ch: mm(sp_p[ch], tri_gt) for ch in chains}
            c_d = {ch: rowsum(sp_d[ch]) for ch in chains}
            a_d = {ch: jnp.where(causal[ch[1]], jnp.exp((z_d[ch] - sp_d[ch]) - later_d[ch]), 0.0) for ch in chains}
            a_p = {ch: jnp.where(ok[ch[0]], jnp.exp((z_p[ch] - sp_raw[ch]) - (c_d[ch] + later_p[ch])), 0.0)
                   for ch in chains}
            pv = {ch: mm(a_d[ch], vh_d[ch[0]][ch[2]]) + mm(a_p[ch], vh_p[ch[0]][ch[2]]) for ch in chains}
            return [tuple((pv[(t, r, 0)] + pv[(t, r, 1)], c_d[(t, r, 0)] + rowsum(sp_p[(t, r, 0)]),
                           c_d[(t, r, 1)] + rowsum(sp_p[(t, r, 1)])) for r in range(n_rc)) for t in range(nt)]

        def least(carry):
            m = jnp.minimum(carry[0][1], carry[0][2])
            for r in range(1, n_rc):
                m = jnp.minimum(m, jnp.minimum(carry[r][1], carry[r][2]))
            return jnp.min(m)

        carries = first_two()
        for t, qi in enumerate(qis):
            def go_on(st, qi=qi):
                return jnp.logical_and(st[0] < qi - 1, st[1] < ATTN_DONE)

            def step(st, qi=qi, t=t):
                new = block(qi - 2 - st[0], st[2], t)
                return st[0] + 1, least(new), new

            walked, _, carry = lax.while_loop(go_on, step, (jnp.int32(0), least(carries[t]), carries[t]))
            for r in range(n_rc):
                rows = slice(t * tq + r * rc, t * tq + (r + 1) * rc)
                o_ref[rows, :] = carry[r][0].astype(BF16)
                lsum_ref[rows, :] = jnp.where(lo, carry[r][1], carry[r][2])
            nblk_ref[hp, qi] = walked.astype(F32)

    blk = pl.BlockSpec((nt * tq, LANES), lambda hp, qg: (qg, hp))
    o512 = jax.ShapeDtypeStruct((s, D_SB), F32)
    return _hosted_call(
        body, comm, name=name, grid=(4, nq // nt),
        out_shape=(jax.ShapeDtypeStruct((s, D_SB), BF16), o512, jax.ShapeDtypeStruct((4, nq), F32)),
        in_specs=[blk, pl.BlockSpec((s, LANES), lambda hp, qg: (0, 4 + hp)),
                  pl.BlockSpec((s, LANES), lambda hp, qg: (0, 8 + hp))],
        out_specs=(blk, blk, pl.BlockSpec(memory_space=pltpu.SMEM)),
        args=(qkv, qkv, qkv), sem=("arbitrary", "arbitrary"))


HALO = 16


def _conv_taps(cc_ref, ch_ref, ccp_ref, chp_ref, halo_ref, first):
    u = cc_ref[...].astype(F32) * ch_ref[...].astype(F32)
    halo_ref[...] = ccp_ref[...].astype(F32) * chp_ref[...].astype(F32) * jnp.where(first, 0.0, 1.0)
    p6 = halo_ref[HALO - 2:HALO - 1, :]
    p7 = halo_ref[HALO - 1:HALO, :]
    rowi = lax.broadcasted_iota(jnp.int32, u.shape, 0)
    u1 = jnp.where(rowi == 0, p7, pltpu.roll(u, 1, 0))
    u2 = jnp.where(rowi == 0, p6, jnp.where(rowi == 1, p7, pltpu.roll(u, 2, 0)))
    return u, u1, u2


def _fwd_mid(x, pc, az, ya, p4, layer, cw, cb, bg, wout_full, pg, wpg_full, bpg, wpe_full, name, comm=None,
             head=None):
    s = x.shape[0]
    ts = min(FWD_ROW_TILE, s)
    blk_h = ts // HALO
    n_in = 18 + (2 if head else 0)

    def body(*refs):
        (x_ref, cb_ref_, cc_ref, ch_ref, cz_ref, ccp_ref, chp_ref, az_ref, ya_ref, p_ref,
         cw_ref, cbias_ref, bg_ref, wout_ref, pg_ref, wpg_ref, bpg_ref, wpe_ref) = refs[:18]
        x2_ref, x3_ref, gated_ref, h2_ref, gate_ref, e_ref = refs[n_in:n_in + 6]
        halo_ref = refs[-1]
        i = pl.program_id(0)
        lane = lax.broadcasted_iota(jnp.int32, (1, LANES), 1)
        lo = lane < HEAD_DIM
        u, u1, u2 = _conv_taps(cc_ref, ch_ref, ccp_ref, chp_ref, halo_ref, i == 0)
        conv = cbias_ref[...] + cw_ref[0:1, :] * u2 + cw_ref[1:2, :] * u1 + cw_ref[2:3, :] * u
        yc = cb_ref_[...].astype(F32) * conv
        for sl in range(8):
            cols = slice(LANES * (sl % 4), LANES * (sl % 4 + 1))
            y = yc[:, cols] if sl < 4 else ya_ref[:, cols].astype(F32)
            zc = (cz_ref[:, cols] if sl < 4 else az_ref[:, cols]).astype(F32)
            rg = lax.rsqrt(_group_bcast_sum(y * y, lo) * (1.0 / HEAD_DIM) + EPS)
            yn = y * rg * bg_ref[:, LANES * sl:LANES * (sl + 1)]
            gated_ref[:, LANES * sl:LANES * (sl + 1)] = (yn * (zc * _sigmoid(zc))).astype(BF16)
        x2 = x_ref[...] + jnp.dot(gated_ref[...], wout_ref[...], preferred_element_type=F32)
        x2_ref[...] = x2
        r2 = lax.rsqrt(jnp.mean(x2 * x2, axis=-1, keepdims=True) + EPS)
        h2 = (x2 * r2 * pg_ref[...]).astype(BF16)
        h2_ref[...] = h2
        gate = _sigmoid(jnp.dot(h2, wpg_ref[...], preferred_element_type=F32) + bpg_ref[...])
        gate_ref[...] = gate.astype(BF16)
        e = jnp.dot(p_ref[...].astype(BF16), wpe_ref[...], preferred_element_type=F32)
        e_ref[...] = e.astype(BF16)
        x3 = x2 + gate * e
        if not head:
            x3_ref[...] = x3
            return
        t_ref, fg_ref = refs[18:20]
        loss_ref, dfg_ref = refs[n_in + 6:n_in + 8]
        dx, loss, dfg = _loss_math(x3, t_ref[...], fg_ref[...])

        @pl.when(i == 0)
        def _():
            loss_ref[...] = jnp.zeros_like(loss_ref)
            dfg_ref[...] = jnp.zeros_like(dfg_ref)

        x3_ref[...] = dx
        loss_ref[...] += loss
        dfg_ref[...] += dfg

    row = lambda width, cb_=0: pl.BlockSpec((ts, width), lambda i: (i, cb_))
    prev = lambda cb_: pl.BlockSpec((HALO, 512), lambda i: (jnp.maximum(i * blk_h - 1, 0), cb_))
    vec = lambda width: pl.BlockSpec((1, width), lambda i: (0, 0))
    lvec = lambda width: _layer_rows(layer, 1, width)
    wspec = lambda r_, c_: pl.BlockSpec((r_, c_), lambda i: (0, 0))
    f32o = jax.ShapeDtypeStruct((s, D_MODEL), F32)
    bfo = jax.ShapeDtypeStruct((s, D_MODEL), BF16)
    head_in = [row(D_MODEL), vec(D_MODEL)] if head else []
    head_out = [jax.ShapeDtypeStruct((1, LANES), F32), jax.ShapeDtypeStruct((1, D_MODEL), F32)] if head else []
    return _hosted_call(
        body, comm, name=name, grid=(s // ts,),
        out_shape=(f32o, f32o, bfo, bfo, bfo, bfo, *head_out),
        scratch_shapes=[pltpu.VMEM((HALO, 512), F32)],
        in_specs=[row(D_MODEL), row(512, 0), row(512, 1), row(512, 2), row(512, 3), prev(1), prev(2),
                  row(512), row(512),
                  pl.BlockSpec((None, None, ts, PLE_DIM), lambda i: (layer, 0, i, 0)),
                  _layer_rows(layer, 3, 512), lvec(512), lvec(D_MODEL),
                  wspec(D_MODEL, D_MODEL), lvec(D_MODEL), wspec(D_MODEL, D_MODEL), lvec(D_MODEL),
                  wspec(PLE_DIM, D_MODEL), *head_in],
        out_specs=(*[row(D_MODEL)] * 6, *([vec(LANES), vec(D_MODEL)] if head else [])),
        args=(x, pc, pc, pc, pc, pc, pc, az, ya, p4, cw, cb, bg, wout_full, pg, wpg_full, bpg, wpe_full,
              *(head or ())),
        sem=("arbitrary",) if head else ("parallel",))


def _loss_math(x, target, g):
    r = lax.rsqrt(jnp.mean(x * x, axis=-1, keepdims=True) + EPS)
    xn = x * r
    err = xn * g - target
    per_row = jnp.sum(err * err, axis=-1, keepdims=True)
    loss = jnp.sum(per_row, axis=0, keepdims=True) * (0.5 / D_MODEL)
    dy = err * (1.0 / D_MODEL)
    dg = jnp.sum(dy * xn, axis=0, keepdims=True)
    dxn = dy * g
    return r * (dxn - xn * jnp.mean(dxn * xn, axis=-1, keepdims=True)), loss, dg


def _bwd_mid(dx3, x2, gate, e, pc, az, ya, gated, h2, p4, layer, cw, cb, bg, pg, wpg_full, wout_full, name,
             comm=None):
    s = x2.shape[0]
    ts = min(ROW_TILE, s)
    blk_h = ts // HALO

    def body(dx3_ref, x2_ref, gate_ref, e_ref, cb_ref_, cc_ref, ch_ref, cz_ref, ccp_ref, chp_ref, az_ref, ya_ref,
             gated_ref, h2_ref, p_ref, cw_ref, cbias_ref, bg_ref, pg_ref, wpg_ref, wout_ref,
             dx2_ref, dya_ref, dmisc_ref, dconv_ref, dwout_ref, dwpg_ref, dwpe_ref,
             dbpg_ref, dpg_ref, dbg_ref, dcbias_ref, dcw_ref,
             dgated_ref, halo_ref, acc_out, acc_pg, acc_pe):
        i = pl.program_id(0)

        @pl.when(i == 0)
        def _():
            for ref in (dbpg_ref, dpg_ref, dbg_ref, dcbias_ref, dcw_ref, acc_out, acc_pg, acc_pe):
                ref[...] = jnp.zeros_like(ref)

        lane = lax.broadcasted_iota(jnp.int32, (1, LANES), 1)
        lo = lane < HEAD_DIM
        dx3 = dx3_ref[...]
        gate = gate_ref[...].astype(F32)
        de_b = (dx3 * gate).astype(BF16)
        dgpre = dx3 * e_ref[...].astype(F32) * gate * (1.0 - gate)
        dbpg_ref[...] += jnp.sum(dgpre, axis=0, keepdims=True)
        dgpre_b = dgpre.astype(BF16)
        dh2 = lax.dot_general(dgpre_b, wpg_ref[...], NT, preferred_element_type=F32)
        acc_pe[...] += lax.dot_general(p_ref[...].astype(BF16), de_b, TN, preferred_element_type=F32)
        acc_pg[...] += lax.dot_general(h2_ref[...], dgpre_b, TN, preferred_element_type=F32)

        u, u1, u2 = _conv_taps(cc_ref, ch_ref, ccp_ref, chp_ref, halo_ref, i == 0)
        conv = cbias_ref[...] + cw_ref[0:1, :] * u2 + cw_ref[1:2, :] * u1 + cw_ref[2:3, :] * u
        c_b = cb_ref_[...].astype(F32)
        yc = c_b * conv
        fwd = []
        for sl in range(8):
            cols = slice(LANES * (sl % 4), LANES * (sl % 4 + 1))
            y = yc[:, cols] if sl < 4 else ya_ref[:, cols].astype(F32)
            zc = (cz_ref[:, cols] if sl < 4 else az_ref[:, cols]).astype(F32)
            rg = lax.rsqrt(_group_bcast_sum(y * y, lo) * (1.0 / HEAD_DIM) + EPS)
            sig = _sigmoid(zc)
            fwd.append((rg, y * rg, zc * sig, sig * (1.0 + zc * (1.0 - sig))))

        x2 = x2_ref[...]
        r2 = lax.rsqrt(jnp.mean(x2 * x2, axis=-1, keepdims=True) + EPS)
        xn2 = x2 * r2
        dpg_ref[...] += jnp.sum(dh2 * xn2, axis=0, keepdims=True)
        dxn = dh2 * pg_ref[...]
        dx2 = dx3 + r2 * (dxn - xn2 * jnp.mean(dxn * xn2, axis=-1, keepdims=True))
        dx2_ref[...] = dx2
        dx2_b = dx2.astype(BF16)
        dgated_ref[...] = lax.dot_general(dx2_b, wout_ref[...], NT, preferred_element_type=F32)
        acc_out[...] += lax.dot_general(gated_ref[...], dx2_b, TN, preferred_element_type=F32)

        for sl in range(8):
            cols = slice(LANES * (sl % 4), LANES * (sl % 4 + 1))
            wide = slice(LANES * sl, LANES * (sl + 1))
            rg, yhat, silu, dsilu = fwd[sl]
            bgs = bg_ref[:, wide]
            dgt = dgated_ref[:, wide]
            dyn = dgt * silu
            dzc = dgt * (yhat * bgs) * dsilu
            dbg_ref[:, wide] += jnp.sum(dyn * yhat, axis=0, keepdims=True)
            dyh = dyn * bgs
            dy = rg * (dyh - yhat * (_group_bcast_sum(dyh * yhat, lo) * (1.0 / HEAD_DIM)))
            if sl < 4:
                dconv = dy * c_b[:, cols]
                dmisc_ref[:, cols] = (dy * conv[:, cols]).astype(BF16)
                dmisc_ref[:, 512 + LANES * sl:512 + LANES * (sl + 1)] = dzc.astype(BF16)
                dconv_ref[:, cols] = dconv
                dcbias_ref[:, cols] += jnp.sum(dconv, axis=0, keepdims=True)
                dcw_ref[0:1, cols] += jnp.sum(dconv * u2[:, cols], axis=0, keepdims=True)
                dcw_ref[1:2, cols] += jnp.sum(dconv * u1[:, cols], axis=0, keepdims=True)
                dcw_ref[2:3, cols] += jnp.sum(dconv * u[:, cols], axis=0, keepdims=True)
            else:
                dya_ref[:, cols] = dy.astype(BF16)
                dmisc_ref[:, 1024 + LANES * (sl - 4):1024 + LANES * (sl - 3)] = dzc.astype(BF16)

        @pl.when(i == pl.num_programs(0) - 1)
        def _():
            dwout_ref[...] = acc_out[...].astype(BF16)
            dwpg_ref[...] = acc_pg[...].astype(BF16)
            dwpe_ref[...] = acc_pe[...].astype(BF16)

    row = lambda width, cb_=0: pl.BlockSpec((ts, width), lambda i: (i, cb_))
    prev = lambda cb_: pl.BlockSpec((HALO, 512), lambda i: (jnp.maximum(i * blk_h - 1, 0), cb_))
    vec = lambda width: pl.BlockSpec((1, width), lambda i: (0, 0))
    lvec = lambda width: _layer_rows(layer, 1, width)
    wspec = lambda r_, c_: pl.BlockSpec((r_, c_), lambda i: (0, 0))
    vo = lambda width: jax.ShapeDtypeStruct((1, width), F32)
    sq = jax.ShapeDtypeStruct((D_MODEL, D_MODEL), BF16)
    return _hosted_call(
        body, comm, name=name, grid=(s // ts,), sem=("arbitrary",),
        args=(dx3, x2, gate, e, pc, pc, pc, pc, pc, pc, az, ya, gated, h2, p4, cw, cb, bg, pg, wpg_full, wout_full),
        out_shape=(jax.ShapeDtypeStruct((s, D_MODEL), F32), jax.ShapeDtypeStruct((s, 512), BF16),
                   jax.ShapeDtypeStruct((s, 1536), BF16), jax.ShapeDtypeStruct((s, 512), F32),
                   sq, sq, jax.ShapeDtypeStruct((PLE_DIM, D_MODEL), BF16),
                   vo(D_MODEL), vo(D_MODEL), vo(D_MODEL), vo(512), jax.ShapeDtypeStruct((SUBLANES, 512), F32)),
        in_specs=[row(D_MODEL), row(D_MODEL), row(D_MODEL), row(D_MODEL),
                  row(512, 0), row(512, 1), row(512, 2), row(512, 3), prev(1), prev(2), row(512), row(512),
                  row(D_MODEL), row(D_MODEL),
                  pl.BlockSpec((None, None, ts, PLE_DIM), lambda i: (layer, 0, i, 0)),
                  _layer_rows(layer, 3, 512), lvec(512), lvec(D_MODEL), lvec(D_MODEL),
                  wspec(D_MODEL, D_MODEL), wspec(D_MODEL, D_MODEL)],
        out_specs=(row(D_MODEL), row(512), row(1536), row(512),
                   wspec(D_MODEL, D_MODEL), wspec(D_MODEL, D_MODEL), wspec(PLE_DIM, D_MODEL),
                   vec(D_MODEL), vec(D_MODEL), vec(D_MODEL), vec(512),
                   pl.BlockSpec((SUBLANES, 512), lambda i: (0, 0))),
        scratch_shapes=[pltpu.VMEM((ts, D_MODEL), F32), pltpu.VMEM((HALO, 512), F32),
                        pltpu.VMEM((D_MODEL, D_MODEL), F32), pltpu.VMEM((D_MODEL, D_MODEL), F32),
                        pltpu.VMEM((PLE_DIM, D_MODEL), F32)])


def _attn_bwd(qkv, lsum, nblk, dya, name, comm=None):
    s = qkv.shape[0]
    tq = min(ATTN_TILE, s)
    nq = s // tq
    rc = min(ATTN_ROWS, tq)
    n_rc = tq // rc
    nt = 2 if nq % 2 == 0 else 1
    chains = [(t, r, hh) for t in range(nt) for r in range(n_rc) for hh in range(2)]

    def body(nblk_ref, q_ref, k_ref, v_ref, lsum_ref, do_ref, dq_ref, dk_ref, dv_ref, dk_acc, dv_acc):
        hp, qg = pl.program_id(0), pl.program_id(1)
        qis = [qg * nt + t for t in range(nt)]

        @pl.when(qg == 0)
        def _():
            dk_acc[...] = jnp.zeros_like(dk_acc)
            dv_acc[...] = jnp.zeros_like(dv_acc)

        lo, causal, tri_gt, tri_le = _attn_pieces(tq, rc)
        lane = lax.broadcasted_iota(jnp.int32, (1, LANES), 1)
        qh = _split_heads(q_ref[...], lo)
        doh = _split_heads(do_ref[...].astype(BF16), lo)
        lt = lsum_ref[...]
        ltot_h = (jnp.sum(jnp.where(lane == 0, lt, 0.0), axis=-1, keepdims=True),
                  jnp.sum(jnp.where(lane == HEAD_DIM, lt, 0.0), axis=-1, keepdims=True))
        rows = lambda a_, t, r: a_[t * tq + r * rc:t * tq + (r + 1) * rc]
        qc = {(t, r, hh): rows(qh[hh], t, r) for t, r, hh in chains}
        doc = {(t, r, hh): rows(doh[hh], t, r) for t, r, hh in chains}
        ltot = {(t, r, hh): rows(ltot_h[hh], t, r) for t, r, hh in chains}

        mm = lambda a_, b_: jnp.dot(a_.astype(BF16), b_, preferred_element_type=F32)
        mm_nt = lambda a_, b_: lax.dot_general(a_, b_, NT, preferred_element_type=F32)
        mm_tn = lambda a_, b_: lax.dot_general(a_.astype(BF16), b_, TN, preferred_element_type=F32)
        rowsum = lambda a_: jnp.sum(a_, axis=-1, keepdims=True)

        def block(kbs, carries, tiles, diag=False, ok=None):
            mine = [ch for ch in chains if ch[0] in tiles]
            start = {t: pl.multiple_of(kbs[t] * tq, tq) for t in tiles}
            k = {t: k_ref[pl.ds(start[t], tq), :] for t in tiles}
            v = {t: v_ref[pl.ds(start[t], tq), :] for t in tiles}
            kh = {t: _split_heads(k[t], lo) for t in tiles}
            keep = (lambda ch, a_: jnp.where(causal[ch[1]], a_, 0.0)) if diag else (lambda ch, a_: a_)
            live = (lambda ch, a_: a_) if ok is None else (lambda ch, a_: jnp.where(ok[ch[0]], a_, 0.0))
            z = {ch: mm_nt(qc[ch], k[ch[0]]) for ch in mine}
            da = {ch: mm_nt(doc[ch], v[ch[0]]) for ch in mine}
            sp_all = {ch: _softplus(z[ch], None, False) for ch in mine}
            sp = {ch: live(ch, keep(ch, sp_all[ch])) for ch in mine}
            later = {ch: mm(sp[ch], tri_gt) for ch in mine}
            walked = {ch: carries[ch[0]][ch[1]][1 + ch[2]] + rowsum(sp[ch]) for ch in mine}
            a = {ch: live(ch, keep(ch, jnp.exp((z[ch] - sp_all[ch]) - ((ltot[ch] - walked[ch]) + later[ch]))))
                 for ch in mine}
            g = {ch: a[ch] * da[ch] for ch in mine}
            upto = {ch: mm(g[ch], tri_le) for ch in mine}
            dz = {ch: keep(ch, g[ch] - jnp.exp(z[ch] - sp_all[ch]) * (carries[ch[0]][ch[1]][3 + ch[2]] + upto[ch])
                           ).astype(BF16) for ch in mine}
            dqc = {ch: mm(dz[ch], kh[ch[0]][ch[2]]) for ch in mine}
            for t in tiles:
                dkc = [mm_tn(dz[ch], qc[ch]) for ch in mine if ch[0] == t]
                dvc = [mm_tn(a[ch], doc[ch]) for ch in mine if ch[0] == t]
                dk_acc[pl.ds(start[t], tq), :] += sum(dkc[1:], dkc[0])
                dv_acc[pl.ds(start[t], tq), :] += sum(dvc[1:], dvc[0])
            return {t: tuple((carries[t][r][0] + dqc[(t, r, 0)] + dqc[(t, r, 1)], walked[(t, r, 0)], walked[(t, r, 1)],
                              carries[t][r][3] + rowsum(g[(t, r, 0)]), carries[t][r][4] + rowsum(g[(t, r, 1)]))
                             for r in range(n_rc)) for t in tiles}

        zc = jnp.zeros((rc, 1), F32)
        tiles = list(range(nt))
        carries = {t: tuple((jnp.zeros((rc, LANES), F32), zc, zc, zc, zc) for _ in range(n_rc)) for t in tiles}
        near = {t: jnp.maximum(qis[t] - 1, 0) for t in tiles}
        for t in tiles:
            first = near[t] - jnp.clip(nblk_ref[hp, qis[t]].astype(jnp.int32), 0, near[t])
            carries[t] = lax.fori_loop(first, near[t], lambda kb, c, t=t: block({t: kb}, {t: c}, [t])[t], carries[t])
        carries = block(near, carries, tiles, ok={t: qis[t] > 0 for t in tiles})
        carries = block({t: qis[t] for t in tiles}, carries, tiles, diag=True)
        for t in tiles:
            for r in range(n_rc):
                dq_ref[t * tq + r * rc:t * tq + (r + 1) * rc, :] = (carries[t][r][0] * 0.125).astype(BF16)

        @pl.when(qg == pl.num_programs(1) - 1)
        def _():
            dk_ref[...] = dk_acc[...].astype(BF16)
            dv_ref[...] = dv_acc[...].astype(BF16)

    blk = pl.BlockSpec((nt * tq, LANES), lambda hp, qg: (qg, hp))
    col = pl.BlockSpec((s, LANES), lambda hp, qg: (0, hp))
    o512 = jax.ShapeDtypeStruct((s, D_SB), BF16)
    return _hosted_call(
        body, comm, name=name, grid=(4, nq // nt),
        out_shape=(o512, o512, o512),
        in_specs=[pl.BlockSpec(memory_space=pltpu.SMEM), blk,
                  pl.BlockSpec((s, LANES), lambda hp, qg: (0, 4 + hp)),
                  pl.BlockSpec((s, LANES), lambda hp, qg: (0, 8 + hp)), blk, blk],
        out_specs=(blk, col, col),
        scratch_shapes=[pltpu.VMEM((s, LANES), F32), pltpu.VMEM((s, LANES), F32)],
        args=(nblk, qkv, qkv, qkv, lsum, dya), sem=("parallel", "arbitrary"))


def _bwd_dproj(dmisc, dconv, pc, dq, dk, dv, x, dx2, g, cw, layer, win_full, name, comm=None, h=None,
               h_rows=None):
    s = x.shape[0]
    ts = min(ROW_TILE, s)
    blk8 = ts // SUBLANES
    last8 = s // SUBLANES - 1
    fused = h is not None
    emit_dproj = not fused or h_rows is not None
    dw_rows, h_blk = (D_MODEL, 0) if h_rows is None else h_rows

    def body(*refs):
        (dcb_ref, dcz_ref, daz_ref, dconv_ref, nxt_ref, cc_ref, ch_ref, dq_ref, dk_ref, dv_ref,
         x_ref, dx2_ref, g_ref, cw_ref, w_ref) = refs[:15]
        rest = list(refs[15:])
        h_ref = rest.pop(0) if fused else None
        dproj_ref = rest.pop(0) if emit_dproj else None
        dx_ref, dg_ref = rest.pop(0), rest.pop(0)
        dw_ref = rest.pop(0) if fused else None
        dproj_ref = dproj_ref if emit_dproj else rest.pop(0)
        acc_ref = rest.pop(0) if fused else None
        i = pl.program_id(0)

        @pl.when(i == 0)
        def _():
            dg_ref[...] = jnp.zeros_like(dg_ref)
            if fused:
                acc_ref[...] = jnp.zeros_like(acc_ref)

        keep = jnp.where(i == pl.num_programs(0) - 1, 0.0, 1.0)
        dc = dconv_ref[...]
        n0 = nxt_ref[0:1, :] * keep
        n1 = nxt_ref[1:2, :] * keep
        rowi = lax.broadcasted_iota(jnp.int32, dc.shape, 0)
        dc1 = jnp.where(rowi == ts - 1, n0, pltpu.roll(dc, ts - 1, 0))
        dc2 = jnp.where(rowi == ts - 2, n0, jnp.where(rowi == ts - 1, n1, pltpu.roll(dc, ts - 2, 0)))
        du = cw_ref[2:3, :] * dc + cw_ref[1:2, :] * dc1 + cw_ref[0:1, :] * dc2
        dproj_ref[:, 0:512] = dcb_ref[...]
        dproj_ref[:, 512:1024] = (du * ch_ref[...].astype(F32)).astype(BF16)
        dproj_ref[:, 1024:1536] = (du * cc_ref[...].astype(F32)).astype(BF16)
        dproj_ref[:, 1536:2048] = dcz_ref[...]
        dproj_ref[:, 2048:2560] = dq_ref[...]
        dproj_ref[:, 2560:3072] = dk_ref[...]
        dproj_ref[:, 3072:3584] = dv_ref[...]
        dproj_ref[:, 3584:4096] = daz_ref[...]
        dh = lax.dot_general(dproj_ref[...], w_ref[...], NT, preferred_element_type=F32)
        if fused:
            acc_ref[...] += lax.dot_general(h_ref[...], dproj_ref[...], TN, preferred_element_type=F32)
        x = x_ref[...]
        r = lax.rsqrt(jnp.mean(x * x, axis=-1, keepdims=True) + EPS)
        xn = x * r
        dg_ref[...] += jnp.sum(dh * xn, axis=0, keepdims=True)
        dxn = dh * g_ref[...]
        dx_ref[...] = dx2_ref[...] + r * (dxn - xn * jnp.mean(dxn * xn, axis=-1, keepdims=True))
        if fused:
            @pl.when(i == pl.num_programs(0) - 1)
            def _():
                dw_ref[...] = acc_ref[...].astype(BF16)

    row = lambda width, cb_=0: pl.BlockSpec((ts, width), lambda i: (i, cb_))
    nxt = pl.BlockSpec((SUBLANES, 512), lambda i: (jnp.minimum((i + 1) * blk8, last8), 0))
    vec = lambda width: pl.BlockSpec((1, width), lambda i: (0, 0))
    lvec = lambda width: _layer_rows(layer, 1, width)
    once = dict(pipeline_mode=pl.Buffered(1)) if fused else {}
    whole = lambda rows_: pl.BlockSpec((rows_, N_IN), lambda i: (0, 0), **once)
    in_specs = [row(512, 0), row(512, 1), row(512, 2), row(512), nxt, row(512, 1), row(512, 2),
                row(512), row(512), row(512), row(D_MODEL), row(D_MODEL), lvec(D_MODEL),
                _layer_rows(layer, 3, 512), whole(D_MODEL)]
    args = [dmisc, dmisc, dmisc, dconv, dconv, pc, pc, dq, dk, dv, x, dx2, g, cw, win_full]
    out_shape = [jax.ShapeDtypeStruct((s, D_MODEL), F32), jax.ShapeDtypeStruct((1, D_MODEL), F32)]
    out_specs = [row(D_MODEL), vec(D_MODEL)]
    scratch = []
    if emit_dproj:
        out_shape.insert(0, jax.ShapeDtypeStruct((s, N_IN), BF16))
        out_specs.insert(0, row(N_IN))
    else:
        scratch.append(pltpu.VMEM((ts, N_IN), BF16))
    if fused:
        in_specs.append(row(dw_rows, h_blk))
        args.append(h)
        out_shape.append(jax.ShapeDtypeStruct((dw_rows, N_IN), BF16))
        out_specs.append(whole(dw_rows))
        scratch.append(pltpu.VMEM((dw_rows, N_IN), F32))
    return _hosted_call(
        body, comm, name=name, grid=(s // ts,), out_shape=tuple(out_shape), in_specs=in_specs,
        out_specs=tuple(out_specs), scratch_shapes=scratch, args=tuple(args), sem=("arbitrary",))


def _atb(a, b, name, a_cols=None, comm=None):
    s, n = b.shape
    m, a_blk = (a.shape[-1], 0) if a_cols is None else a_cols
    ts = min(512, s)
    tn = min(2048, n)
    a_spec = pl.BlockSpec((ts, m), lambda j, i: (i, a_blk))

    def body(a_ref, b_ref, o_ref, acc_ref):
        i = pl.program_id(1)

        @pl.when(i == 0)
        def _():
            acc_ref[...] = jnp.zeros_like(acc_ref)

        acc_ref[...] += lax.dot_general(a_ref[...].astype(BF16), b_ref[...], TN, preferred_element_type=F32)

        @pl.when(i == pl.num_programs(1) - 1)
        def _():
            o_ref[...] = acc_ref[...].astype(BF16)

    (out,), got = _hosted_call(
        body, comm, name=name, grid=(n // tn, s // ts),
        out_shape=(jax.ShapeDtypeStruct((m, n), BF16),),
        in_specs=[a_spec, pl.BlockSpec((ts, tn), lambda j, i: (i, j))],
        out_specs=(pl.BlockSpec((m, tn), lambda j, i: (0, j)),),
        scratch_shapes=[pltpu.VMEM((m, tn), F32)],
        args=(a, b), sem=("parallel", "arbitrary"))
    return out, got


def _adamw_math(w, g, m, v):
    m2 = ADAM_B1 * m + (1.0 - ADAM_B1) * g
    v2 = ADAM_B2 * v + (1.0 - ADAM_B2) * (g * g)
    m_hat = m2 / (1.0 - ADAM_B1 ** ADAM_STEP)
    v_hat = v2 / (1.0 - ADAM_B2 ** ADAM_STEP)
    delta = -ADAM_LR * (m_hat / (jnp.sqrt(v_hat) + ADAM_EPS) + ADAM_WD * w)
    return delta, m2, v2


def _adamw_sum8(pieces, w, m, v, name):
    _, rows, cols = w.shape
    tr = min([rows, 256] + [pc_[0].shape[1] for pc_ in pieces])
    n_tiles = rows // tr
    n_p = len(pieces)
    spans = [(layer, row0 // tr, arr.shape[1] // tr) for arr, layer, row0 in pieces]

    def body(*refs):
        p_refs = refs[:n_p]
        w_ref, m_ref, v_ref, g_ref, d_ref, m2_ref, v2_ref = refs[n_p:]
        l, i = pl.program_id(0), pl.program_id(1)

        def run(p_ref):
            g = p_ref[0].astype(F32)
            for d in range(1, N_DEV):
                g = g + p_ref[d].astype(F32)
            g_ref[...] = g
            d_ref[...], m2_ref[...], v2_ref[...] = _adamw_math(w_ref[...], g, m_ref[...], v_ref[...])

        for p_ref, (layer, t0, nt) in zip(p_refs, spans):
            mine = jnp.logical_and(l == layer, jnp.logical_and(i >= t0, i < t0 + nt))
            pl.when(mine)(lambda p_ref=p_ref: run(p_ref))

    def piece_spec(layer, t0, nt):
        return pl.BlockSpec((N_DEV, tr, cols),
                            lambda l, i: (0, jnp.clip(jnp.where(l == layer, i - t0, jnp.where(l < layer, 0, nt - 1)),
                                                      0, nt - 1), 0))

    tile = pl.BlockSpec((None, tr, cols), lambda l, i: (l, i, 0))
    o = jax.ShapeDtypeStruct((DEPTH, rows, cols), F32)
    return _call(
        body, name=name, grid=(DEPTH, n_tiles),
        out_shape=(o, o, o, o),
        in_specs=[*[piece_spec(*sp) for sp in spans], tile, tile, tile],
        out_specs=(tile, tile, tile, tile),
        compiler_params=_params(("arbitrary", "arbitrary"), VMEM_LIMIT),
    )(*[pc_[0] for pc_ in pieces], w, m, v)


def _small_update(blk, layered, final, conv, loss_parts):
    n_l = len(layered)
    ins = [a for item in layered for a in item] + list(final) + list(conv) + [loss_parts]
    shapes = [item[2].shape for item in layered] + [final[1].shape, conv[2].shape]
    out_shape = [jax.ShapeDtypeStruct(sh, F32) for sh in shapes for _ in range(4)]
    out_shape.append(jax.ShapeDtypeStruct((1, LANES), F32))

    def body(*refs):
        blk_ref, refs = refs[0], refs[1:]
        in_refs, out_refs, pick_ref = refs[:len(ins)], refs[len(ins):-1], refs[-1]

        def total(ref):
            g = ref[0]
            for d in range(1, N_DEV):
                g = g + ref[d]
            return g

        def update(k, at, g, w_ref, m_ref, v_ref):
            g_ref, d_ref, m2_ref, v2_ref = out_refs[4 * k:4 * k + 4]
            g_ref[at] = g
            d_ref[at], m2_ref[at], v2_ref[at] = _adamw_math(w_ref[at], g, m_ref[at], v_ref[at])

        for k in range(n_l):
            p0, p1, w_ref, m_ref, v_ref = in_refs[5 * k:5 * k + 5]
            for layer, parts in enumerate((p0, p1)):
                update(k, pl.ds(layer, 1), total(parts), w_ref, m_ref, v_ref)
        pf, w_ref, m_ref, v_ref = in_refs[5 * n_l:5 * n_l + 4]
        update(n_l, pl.ds(0, 1), total(pf), w_ref, m_ref, v_ref)
        c0, c1, w_ref, m_ref, v_ref = in_refs[5 * n_l + 4:5 * n_l + 9]
        for layer, parts in enumerate((c0, c1)):
            g8 = total(parts)
            mine = jnp.zeros((SUBLANES, HEAD_DIM), F32)
            for j in range(N_DEV):
                mine = mine + jnp.where(blk_ref[0] == j, g8[:, HEAD_DIM * j:HEAD_DIM * (j + 1)], 0.0)
            pick_ref[...] = mine
            update(n_l + 1, layer, pick_ref[0:3, :], w_ref, m_ref, v_ref)
        out_refs[-1][...] = total(in_refs[-1])

    whole = lambda shape: pl.BlockSpec(shape, lambda: (0,) * len(shape))
    outs = _call(
        body, name="adamw_small",
        out_shape=tuple(out_shape),
        in_specs=[pl.BlockSpec(memory_space=pltpu.SMEM)] + [whole(a.shape) for a in ins],
        out_specs=tuple(whole(o.shape) for o in out_shape),
        scratch_shapes=[pltpu.VMEM((SUBLANES, HEAD_DIM), F32)],
    )(blk, *ins)
    return [outs[4 * k:4 * k + 4] for k in range(n_l + 2)], outs[-1]


def kernel(x, p, norm_g, w_in, conv_w, conv_b, branch_g, w_out, ple_norm_g, w_pg, b_pg, w_pe, final_g, loss_target, m_norm_g, m_w_in, m_conv_w, m_conv_b, m_branch_g, m_w_out, m_ple_norm_g, m_w_pg, m_b_pg, m_w_pe, m_final_g, v_norm_g, v_w_in, v_conv_w, v_conv_b, v_branch_g, v_w_out, v_ple_norm_g, v_w_pg, v_b_pg, v_w_pe, v_final_g):
    s = x.shape[1]
    x0 = x.reshape(s, D_MODEL)
    target = loss_target.reshape(s, D_MODEL)
    me_blk = _my_block()

    win_s, wout_s, wpg_s, wpe_s = _cast_bf16(
        [w_in.reshape(DEPTH * D_MODEL, 512), w_out.reshape(DEPTH * 128, D_MODEL),
         w_pg.reshape(DEPTH * 128, D_MODEL), w_pe.reshape(DEPTH * PLE_DIM, 128)], "cast_weights")
    win_s, wout_s = win_s.reshape(DEPTH, D_MODEL, 512), wout_s.reshape(DEPTH, 128, D_MODEL)
    wpg_s, wpe_s = wpg_s.reshape(DEPTH, 128, D_MODEL), wpe_s.reshape(DEPTH, PLE_DIM, 128)
    cw_s = jnp.zeros((SUBLANES, LANES), F32).at[:DEPTH * 3, :HEAD_DIM].set(conv_w.reshape(DEPTH * 3, HEAD_DIM))
    bf = lambda r_, c_: jax.ShapeDtypeStruct((r_, c_), BF16)
    w_items = lambda l: [(wout_s[l], bf(D_MODEL, D_MODEL), "rows128"), (wpg_s[l], bf(D_MODEL, D_MODEL), "rows128"),
                         (wpe_s[l], bf(PLE_DIM, D_MODEL), "cols128")]
    win_f = [None] * DEPTH
    win_f[0], cw_all = _comm_call(_gather_comm([
        (win_s[0], bf(D_MODEL, N_IN), "cols512"),
        (cw_s, jax.ShapeDtypeStruct((N_DEV, SUBLANES, LANES), F32), "slot")]), "gather_w_in_0")
    cw_full = jnp.transpose(cw_all[:, :DEPTH * 3, :HEAD_DIM].reshape(N_DEV, DEPTH, 3, HEAD_DIM), (1, 2, 0, 3))
    cw_full = cw_full.reshape(DEPTH, 3, D_CONV)
    gather_rest_0 = _gather_comm(w_items(0))
    gather_win_1 = _gather_comm([(win_s[1], bf(D_MODEL, N_IN), "cols512")])
    gather_rest_1 = _gather_comm(w_items(1))

    norm3, convb3, branch3, ple3, bpg3 = [a.reshape(DEPTH, 1, -1) for a in (norm_g, conv_b, branch_g, ple_norm_g, b_pg)]

    saved = []
    xl = x0
    wout_f, wpg_f, wpe_f = [None] * DEPTH, [None] * DEPTH, [None] * DEPTH
    for l in range(DEPTH):
        (h, pc, qkv, az), got = _fwd_in(xl, norm3, l, win_f[l], f"fwd_in_{l}",
                                        comm=gather_rest_0 if l == 0 else None)
        if l == 0:
            wout_f[0], wpg_f[0], wpe_f[0] = got
        (ya, lsum, nblk), got = _attn_fwd(qkv, f"attn_fwd_{l}", comm=gather_win_1 if l == 0 else None)
        if l == 0:
            (win_f[1],) = got
        last = l == DEPTH - 1
        outs, got = _fwd_mid(
            xl, pc, az, ya, p, l, cw_full, convb3, branch3, wout_f[l],
            ple3, wpg_f[l], bpg3, wpe_f[l], f"fwd_mid_{l}",
            comm=gather_rest_1 if l == 0 else None, head=(target, final_g[None, :]) if last else None)
        x2, x3, gated, h2, gate, e = outs[:6]
        if l == 0:
            wout_f[1], wpg_f[1], wpe_f[1] = got
        saved.append(dict(x=xl, h=h, pc=pc, qkv=qkv, az=az, ya=ya, lsum=lsum, nblk=nblk, x2=x2, gated=gated, h2=h2,
                          gate=gate, e=e))
        xl = x3

    dx, (loss_acc, d_final_g) = xl, outs[6:]

    dwin, dwout, dwpg, dwpe = [None] * DEPTH, [None] * DEPTH, [None] * DEPTH, [None] * DEPTH
    small = dict(norm_g=[None] * DEPTH, conv_b=[None] * DEPTH, branch_g=[None] * DEPTH,
                 ple_norm_g=[None] * DEPTH, b_pg=[None] * DEPTH, conv_w=[None] * DEPTH)
    slot = lambda r_, c_: jax.ShapeDtypeStruct((r_, c_), BF16)
    half = D_MODEL // 2
    r_out, r_pg, r_pe = [None] * DEPTH, [None] * DEPTH, [None] * DEPTH

    def rest_items(l):
        return [(dwout[l], slot(128, D_MODEL), "rows128"), (dwpg[l], slot(128, D_MODEL), "rows128"),
                (dwpe[l], slot(PLE_DIM, 128), "cols128")]

    for l in reversed(range(DEPTH)):
        sv = saved[l]
        quarter = D_MODEL // 4
        ride = None
        if l == 0:
            ride = _exchange_comm(rest_items(1) + [(dwin[1], slot(quarter, 512), "cols512", (0, quarter))])
        (dx2, dya, dmisc, dconv, dwout[l], dwpg[l], dwpe[l], d_bpg, d_pg, d_bg, d_cbias, d_cw), got = _bwd_mid(
            dx, sv["x2"], sv["gate"], sv["e"], sv["pc"], sv["az"], sv["ya"], sv["gated"], sv["h2"], p, l,
            cw_full, convb3, branch3, ple3, wpg_f[l], wout_f[l], f"bwd_mid_{l}",
            comm=ride)
        if l == 0:
            r_out[1], r_pg[1], r_pe[1], r_in1_a = got
            ride = _exchange_comm([(dwin[1], slot(D_MODEL - quarter, 512), "cols512", (quarter, D_MODEL - quarter))]
                                  + rest_items(0))
        (dq, dk, dv), got = _attn_bwd(sv["qkv"], sv["lsum"], sv["nblk"], dya, f"attn_bwd_{l}", comm=ride)
        if l == 0:
            r_in1_b, r_out[0], r_pg[0], r_pe[0] = got
        dproj_args = (dmisc, dconv, sv["pc"], dq, dk, dv, sv["x"], dx2, norm3, cw_full, l, win_f[l])
        if l == 1:
            (dx, d_ng, dwin[1]), _ = _bwd_dproj(*dproj_args, "bwd_dproj_dw_1", h=sv["h"])
        else:
            (dproj, dx, d_ng, dwin_top), _ = _bwd_dproj(*dproj_args, "bwd_dproj_dw_0", h=sv["h"], h_rows=(half, 0))
            dwin_bot, (r_in0_top,) = _atb(sv["h"], dproj, "dw_in_0_bottom", a_cols=(half, 1),
                                          comm=_exchange_comm([(dwin_top, slot(half, 512), "cols512")]))
        small["norm_g"][l], small["conv_b"][l], small["branch_g"][l] = d_ng, d_cbias, d_bg
        small["ple_norm_g"][l], small["b_pg"][l], small["conv_w"][l] = d_pg, d_bpg, d_cw
    grad_x = dx.reshape(1, s, D_MODEL)

    names = ["norm_g", "conv_b", "branch_g", "ple_norm_g", "b_pg", "conv_w"]
    small_list = [small[n][l] for n in names for l in range(DEPTH)] + [d_final_g, loss_acc]
    got = _comm_call(_exchange_comm(
        [(dwin_bot, slot(half, 512), "cols512")]
        + [(a, jax.ShapeDtypeStruct(a.shape, F32), "slot") for a in small_list]), "exchange_last")
    r_in0_bot, r_small = got[0], got[1:]

    per_layer = lambda r: [(r[0], 0, 0), (r[1], 1, 0)]
    g_win, d_win, m_win, v_win = _adamw_sum8([(r_in0_top, 0, 0), (r_in0_bot, 0, half), (r_in1_a, 1, 0),
                                              (r_in1_b, 1, D_MODEL // 4)],
                                             w_in, m_w_in, v_w_in, "adamw_w_in")
    g_wout, d_wout, m_wout, v_wout = _adamw_sum8(per_layer(r_out), w_out, m_w_out, v_w_out, "adamw_w_out")
    g_wpg, d_wpg, m_wpg, v_wpg = _adamw_sum8(per_layer(r_pg), w_pg, m_w_pg, v_w_pg, "adamw_w_pg")
    g_wpe, d_wpe, m_wpe, v_wpe = _adamw_sum8(per_layer(r_pe), w_pe, m_w_pe, v_w_pe, "adamw_w_pe")

    layered = [(norm_g, m_norm_g, v_norm_g), (conv_b, m_conv_b, v_conv_b), (branch_g, m_branch_g, v_branch_g),
               (ple_norm_g, m_ple_norm_g, v_ple_norm_g), (b_pg, m_b_pg, v_b_pg)]
    row = lambda a: a.reshape(1, -1)
    upd, loss_row = _small_update(
        jnp.reshape(me_blk, (1,)).astype(jnp.int32),
        [(r_small[2 * k], r_small[2 * k + 1], *wmv) for k, wmv in enumerate(layered)],
        (r_small[12], row(final_g), row(m_final_g), row(v_final_g)),
        (r_small[10], r_small[11], conv_w, m_conv_w, v_conv_w), r_small[13])
    loss = loss_row[0, 0]
    upd[5] = [a.reshape(-1) for a in upd[5]]

    big = {1: (g_win, d_win, m_win, v_win), 5: (g_wout, d_wout, m_wout, v_wout), 7: (g_wpg, d_wpg, m_wpg, v_wpg),
           9: (g_wpe, d_wpe, m_wpe, v_wpe)}
    small_at = {0: 0, 2: 6, 3: 1, 4: 2, 6: 3, 8: 4, 10: 5}
    per_kind = [[(big[i] if i in big else upd[small_at[i]])[j] for i in range(11)] for j in range(4)]
    return (loss, grad_x, *per_kind[0], *per_kind[1], *per_kind[2], *per_kind[3])
```
